```python
import math
import jax, jax.numpy as jnp
from jax import lax
import numpy as np

D_MODEL = 1024
BATCH = 4
SEQ = 4096
DEPTH = 1
DEC_BATCH = 128
DEC_SEQ = 8
PAST_LEN = 2048
PAGE_SIZE = 128

D_MIX = D_MODEL
D_SSM = D_MIX // 2
D_ATTN = D_MIX - D_SSM
SSM_GROUP = 16
N_SSM_GROUPS = D_SSM // SSM_GROUP
SSM_STATE = 64
DT_MIN = 1e-3
DT_MAX = 1e-1
HEAD_DIM = 64
N_HEADS = D_ATTN // HEAD_DIM
D_IN = D_SSM + 3 * D_ATTN + N_HEADS
Q_BLOCK = 128
N_EXPERTS = 64
TOP_K = 6
N_EXPERT_GROUPS = 8
TOPK_GROUPS = 4
D_EXPERT = 256
D_SHARED = 256
ROUTED_SCALE = 2.5
MOE_BLOCK = 128
EPS = 1e-6

kernel_name = 'hymba_s5_fox_moe_adaln_step'


def _rmsnorm(x, g):
    xf = x.astype(jnp.float32)
    xf = xf * lax.rsqrt(jnp.mean(xf * xf, axis=-1, keepdims=True) + EPS)
    return xf * g.astype(jnp.float32)


def _norm_modulate(x, g, shift, scale):
    h = _rmsnorm(x, g) * (1.0 + scale[:, None, :]) + shift[:, None, :]
    return h.astype(x.dtype)


def _adaln(c, w, b, n):
    mod = jax.nn.silu(c.astype(jnp.float32)) @ w.astype(jnp.float32) + b.astype(jnp.float32)
    return jnp.split(mod, n, axis=-1)


def _swiglu(x, wg, wu, wd):
    return (jax.nn.silu(x @ wg) * (x @ wu)) @ wd


def _cmul(ar, ai, br, bi):
    return ar * br - ai * bi, ar * bi + ai * br


def _scan_combine(e1, e2):
    a1r, a1i, b1r, b1i = e1
    a2r, a2i, b2r, b2i = e2
    ar, ai = _cmul(a2r, a2i, a1r, a1i)
    br, bi = _cmul(a2r, a2i, b1r, b1i)
    return ar, ai, br + b2r, bi + b2i


def _s5_scan(u, h0, lam_re, lam_im, log_step, b_re, b_im, c_re, c_im, d_skip):
    f32 = jnp.float32
    lam_re = lam_re.astype(f32)
    lam_im = lam_im.astype(f32)
    dt = jnp.exp(log_step.astype(f32))[:, None]
    mag = jnp.exp(lam_re * dt)
    a_re = mag * jnp.cos(lam_im * dt)
    a_im = mag * jnp.sin(lam_im * dt)
    den = lam_re * lam_re + lam_im * lam_im
    n_re = a_re - 1.0
    f_re = (n_re * lam_re + a_im * lam_im) / den
    f_im = (a_im * lam_re - n_re * lam_im) / den
    b_re = b_re.astype(f32)
    b_im = b_im.astype(f32)
    bb_re = f_re[..., None] * b_re - f_im[..., None] * b_im
    bb_im = f_re[..., None] * b_im + f_im[..., None] * b_re
    bu_re = jnp.einsum('blgi,gpi->blgp', u, bb_re)
    bu_im = jnp.einsum('blgi,gpi->blgp', u, bb_im)
    if h0 is not None:
        init_re, init_im = _cmul(a_re, a_im, h0[0].astype(f32), h0[1].astype(f32))
        bu_re = bu_re.at[:, 0].add(init_re)
        bu_im = bu_im.at[:, 0].add(init_im)
    A_re = jnp.broadcast_to(a_re, bu_re.shape)
    A_im = jnp.broadcast_to(a_im, bu_im.shape)
    _, _, h_re, h_im = lax.associative_scan(_scan_combine, (A_re, A_im, bu_re, bu_im), axis=1)
    y = (jnp.einsum('blgp,gip->blgi', h_re, c_re.astype(f32))
         - jnp.einsum('blgp,gip->blgi', h_im, c_im.astype(f32))
         + d_skip.astype(f32) * u)
    return y, h_re[:, -1], h_im[:, -1]


def _fox_block(q, cq, q_pos, k, v, ck, k_pos):
    s = jnp.einsum('bqhd,bkhd->bhqk', q, k).astype(jnp.float32) * (HEAD_DIM ** -0.5)
    s = s + jnp.transpose(cq, (0, 2, 1))[..., :, None] - jnp.transpose(ck, (0, 2, 1))[..., None, :]
    causal = k_pos[None, :] <= q_pos[:, None]
    s = jnp.where(causal, s, -jnp.inf)
    p = jax.nn.softmax(s, axis=-1)
    return jnp.einsum('bhqk,bkhd->bqhd', p.astype(v.dtype), v)


def _hybrid_mixer(h, past, w_in, b_fgate, lam_re, lam_im, log_step, b_re, b_im, c_re, c_im,
                  d_skip, w_glu, b_glu, g_ssm_out, g_attn_out, w_out):
    Bn, L, _ = h.shape
    f32 = jnp.float32
    proj = h @ w_in
    u = proj[..., :D_SSM]
    q = proj[..., D_SSM:D_SSM + D_ATTN].reshape(Bn, L, N_HEADS, HEAD_DIM)
    k = proj[..., D_SSM + D_ATTN:D_SSM + 2 * D_ATTN].reshape(Bn, L, N_HEADS, HEAD_DIM)
    v = proj[..., D_SSM + 2 * D_ATTN:D_SSM + 3 * D_ATTN].reshape(Bn, L, N_HEADS, HEAD_DIM)
    fg = proj[..., D_SSM + 3 * D_ATTN:]
    logf = jax.nn.log_sigmoid(fg.astype(f32) + b_fgate.astype(f32))

    ug = u.astype(f32).reshape(Bn, L, N_SSM_GROUPS, SSM_GROUP)
    h0 = None if past is None else (past[0], past[1])
    y_ssm, hr, hi = _s5_scan(ug, h0, lam_re, lam_im, log_step, b_re, b_im, c_re, c_im, d_skip)
    y_ssm = jax.nn.gelu(y_ssm.reshape(Bn, L, D_SSM))
    y_ssm = y_ssm * jax.nn.sigmoid(y_ssm @ w_glu.astype(f32) + b_glu.astype(f32))

    if past is None:
        c_all = jnp.cumsum(logf, axis=1)
        nb = L // Q_BLOCK
        k_pos = jnp.arange(L, dtype=jnp.int32)
        qb = q.reshape(Bn, nb, Q_BLOCK, N_HEADS, HEAD_DIM).swapaxes(0, 1)
        cqb = c_all.reshape(Bn, nb, Q_BLOCK, N_HEADS).swapaxes(0, 1)
        pb = k_pos.reshape(nb, Q_BLOCK)
        o = lax.map(lambda a: _fox_block(a[0], a[1], a[2], k, v, c_all, k_pos), (qb, cqb, pb))
        o = o.swapaxes(0, 1).reshape(Bn, L, D_ATTN)
    else:
        k_past, v_past, logf_past = past[2], past[3], past[4]
        n_past = k_past.shape[1]
        k_all = jnp.concatenate([k_past.astype(k.dtype), k], axis=1)
        v_all = jnp.concatenate([v_past.astype(v.dtype), v], axis=1)
        c_all = jnp.cumsum(jnp.concatenate([logf_past.astype(f32), logf], axis=1), axis=1)
        k_pos = jnp.arange(n_past + L, dtype=jnp.int32)
        o = _fox_block(q, c_all[:, n_past:], k_pos[n_past:], k_all, v_all, c_all, k_pos)
        o = o.reshape(Bn, L, D_ATTN)

    merged = jnp.concatenate([_rmsnorm(y_ssm, g_ssm_out), _rmsnorm(o, g_attn_out)], axis=-1)
    out = merged.astype(h.dtype) @ w_out
    return out, (k, v, logf, hr, hi)


def _grouped_experts(x2, idx, wts, w_eg, w_eu, w_ed):
    T, Dm = x2.shape
    K = idx.shape[1]
    E = w_eg.shape[0]
    M = MOE_BLOCK
    A = T * K
    e_flat = idx.reshape(-1).astype(jnp.int32)
    tok_flat = jnp.arange(A, dtype=jnp.int32) // K
    order = jnp.argsort(e_flat, stable=True)
    e_sorted = e_flat[order]
    counts = jnp.bincount(e_flat, length=E).astype(jnp.int32)
    padded = ((counts + M - 1) // M) * M
    pad_end = jnp.cumsum(padded)
    pad_start = pad_end - padded
    start = jnp.cumsum(counts) - counts
    rank = jnp.arange(A, dtype=jnp.int32) - start[e_sorted]
    dest_sorted = pad_start[e_sorted] + rank
    n_rows = (-(-A // M)) * M + E * M
    n_blocks = n_rows // M
    src_tok = jnp.full((n_rows,), T, jnp.int32).at[dest_sorted].set(tok_flat[order])
    x_pad = jnp.concatenate([x2, jnp.zeros((1, Dm), x2.dtype)], axis=0)
    block_start = jnp.arange(n_blocks, dtype=jnp.int32) * M
    block_e = jnp.minimum(jnp.searchsorted(pad_end, block_start, side='right'), E - 1)
    xb = x_pad[src_tok].reshape(n_blocks, M, Dm)

    def run(a):
        xblk, e = a
        return _swiglu(xblk, w_eg[e], w_eu[e], w_ed[e])

    yb = lax.map(run, (xb, block_e)).reshape(n_rows, Dm)
    dest_flat = jnp.zeros((A,), jnp.int32).at[order].set(dest_sorted)
    y = (yb[dest_flat].reshape(T, K, Dm).astype(jnp.float32) * wts[..., None]).sum(axis=1)
    return y.astype(x2.dtype)


def _moe_ffn(h, w_router, router_bias, w_eg, w_eu, w_ed, w_sg, w_su, w_sd):
    Bn, L, Dm = h.shape
    x2 = h.reshape(-1, Dm)
    T = x2.shape[0]
    scores = jax.nn.sigmoid((x2 @ w_router).astype(jnp.float32))
    biased = scores + router_bias.astype(jnp.float32)
    per_group = N_EXPERTS // N_EXPERT_GROUPS
    grp_score = lax.top_k(biased.reshape(T, N_EXPERT_GROUPS, per_group), 2)[0].sum(-1)
    _, top_grp = lax.top_k(grp_score, TOPK_GROUPS)
    grp_mask = jax.nn.one_hot(top_grp, N_EXPERT_GROUPS, dtype=jnp.float32).sum(1) > 0
    expert_mask = jnp.repeat(grp_mask, per_group, axis=1)
    _, idx = lax.top_k(jnp.where(expert_mask, biased, -jnp.inf), TOP_K)
    w = jnp.take_along_axis(scores, idx, axis=1)
    w = w / jnp.sum(w, axis=-1, keepdims=True) * ROUTED_SCALE
    routed = _grouped_experts(x2, idx, w, w_eg, w_eu, w_ed)
    shared = _swiglu(x2, w_sg, w_su, w_sd)
    return (routed + shared).reshape(Bn, L, Dm)


def _block(x, c, past, w_ada, b_ada, g_norm1, w_in, b_fgate, lam_re, lam_im, log_step,
           b_re, b_im, c_re, c_im, d_skip, w_glu, b_glu, g_ssm_out, g_attn_out, w_out,
           g_norm2, w_router, router_bias, w_eg, w_eu, w_ed, w_sg, w_su, w_sd):
    sh1, sc1, gt1, sh2, sc2, gt2 = _adaln(c, w_ada, b_ada, 6)
    h = _norm_modulate(x, g_norm1, sh1, sc1)
    mix, state = _hybrid_mixer(h, past, w_in, b_fgate, lam_re, lam_im, log_step, b_re, b_im,
                               c_re, c_im, d_skip, w_glu, b_glu, g_ssm_out, g_attn_out, w_out)
    x = x + (gt1[:, None, :] * mix.astype(jnp.float32)).astype(x.dtype)
    h = _norm_modulate(x, g_norm2, sh2, sc2)
    ff = _moe_ffn(h, w_router, router_bias, w_eg, w_eu, w_ed, w_sg, w_su, w_sd)
    x = x + (gt2[:, None, :] * ff.astype(jnp.float32)).astype(x.dtype)
    return x, state


def setup_inputs(seed: int = 0) -> dict:
    key = jax.random.key(seed)
    keys = iter(jax.random.split(key, 48))
    f32 = jnp.float32
    D = D_MODEL

    def normal(shape, scale=1.0):
        return scale * jax.random.normal(next(keys), shape, f32)

    def gain(shape):
        return 1.0 + normal(shape, 0.05)

    n_pages = PAST_LEN // PAGE_SIZE
    n_used = DEC_BATCH * n_pages
    n_phys = n_used + n_used // 4
    page_table = jax.random.permutation(next(keys), n_phys)[:n_used].reshape(DEC_BATCH, n_pages).astype(jnp.int32)

    x_prompt = normal((BATCH, SEQ, D))
    x_sample = normal((DEC_BATCH, DEC_SEQ, D))
    c_prompt = normal((BATCH, D))
    c_sample = normal((DEC_BATCH, D))
    cache_k = normal((DEPTH, n_phys, PAGE_SIZE, N_HEADS, HEAD_DIM))
    cache_v = normal((DEPTH, n_phys, PAGE_SIZE, N_HEADS, HEAD_DIM))
    cache_logf = jax.nn.log_sigmoid(4.5 + normal((DEPTH, n_phys, PAGE_SIZE, N_HEADS)))
    state_ssm_re = normal((DEPTH, DEC_BATCH, N_SSM_GROUPS, SSM_STATE), 0.3)
    state_ssm_im = normal((DEPTH, DEC_BATCH, N_SSM_GROUPS, SSM_STATE), 0.3)

    w_ada = normal((DEPTH, D, 6 * D), 0.5 * D ** -0.5)
    b_ada = normal((DEPTH, 6 * D), 0.02)
    g_norm1 = gain((DEPTH, D))
    w_in = jnp.concatenate([normal((DEPTH, D, D_SSM + 3 * D_ATTN), D ** -0.5),
                            normal((DEPTH, D, N_HEADS), 0.3 * D ** -0.5)], axis=-1)
    b_fgate = jax.random.uniform(next(keys), (DEPTH, N_HEADS), f32, 3.0, 6.0)
    ssm_lambda_re = -0.5 + normal((DEPTH, N_SSM_GROUPS, SSM_STATE), 0.01)
    ssm_lambda_im = math.pi * jnp.arange(SSM_STATE, dtype=f32)[None, None, :] + normal((DEPTH, N_SSM_GROUPS, SSM_STATE), 0.01)
    ssm_log_step = jax.random.uniform(next(keys), (DEPTH, N_SSM_GROUPS), f32, math.log(DT_MIN), math.log(DT_MAX))
    ssm_b_re = normal((DEPTH, N_SSM_GROUPS, SSM_STATE, SSM_GROUP), (2 * SSM_GROUP) ** -0.5)
    ssm_b_im = normal((DEPTH, N_SSM_GROUPS, SSM_STATE, SSM_GROUP), (2 * SSM_GROUP) ** -0.5)
    ssm_c_re = normal((DEPTH, N_SSM_GROUPS, SSM_GROUP, SSM_STATE), (2 * SSM_STATE) ** -0.5)
    ssm_c_im = normal((DEPTH, N_SSM_GROUPS, SSM_GROUP, SSM_STATE), (2 * SSM_STATE) ** -0.5)
    ssm_d = normal((DEPTH, N_SSM_GROUPS, SSM_GROUP))
    w_glu = normal((DEPTH, D_SSM, D_SSM), D_SSM ** -0.5)
    b_glu = normal((DEPTH, D_SSM), 0.02)
    g_ssm_out = gain((DEPTH, D_SSM))
    g_attn_out = gain((DEPTH, D_ATTN))
    w_out = normal((DEPTH, D_MIX, D), D_MIX ** -0.5)
    g_norm2 = gain((DEPTH, D))
    w_router = normal((DEPTH, D, N_EXPERTS), D ** -0.5)
    router_bias = normal((DEPTH, N_EXPERTS), 0.01)
    w_exp_gate = normal((DEPTH, N_EXPERTS, D, D_EXPERT), D ** -0.5)
    w_exp_up = normal((DEPTH, N_EXPERTS, D, D_EXPERT), D ** -0.5)
    w_exp_down = normal((DEPTH, N_EXPERTS, D_EXPERT, D), D_EXPERT ** -0.5)
    w_sh_gate = normal((DEPTH, D, D_SHARED), D ** -0.5)
    w_sh_up = normal((DEPTH, D, D_SHARED), D ** -0.5)
    w_sh_down = normal((DEPTH, D_SHARED, D), D_SHARED ** -0.5)
    g_final = gain((D,))
    w_ada_final = normal((D, 2 * D), 0.5 * D ** -0.5)
    b_ada_final = normal((2 * D,), 0.02)
    return {'x_prompt': x_prompt, 'x_sample': x_sample, 'c_prompt': c_prompt, 'c_sample': c_sample,
            'cache_k': cache_k, 'cache_v': cache_v, 'cache_logf': cache_logf,
            'state_ssm_re': state_ssm_re, 'state_ssm_im': state_ssm_im, 'page_table': page_table,
            'w_ada': w_ada, 'b_ada': b_ada, 'g_norm1': g_norm1, 'w_in': w_in, 'b_fgate': b_fgate,
            'ssm_lambda_re': ssm_lambda_re, 'ssm_lambda_im': ssm_lambda_im, 'ssm_log_step': ssm_log_step,
            'ssm_b_re': ssm_b_re, 'ssm_b_im': ssm_b_im, 'ssm_c_re': ssm_c_re, 'ssm_c_im': ssm_c_im,
            'ssm_d': ssm_d, 'w_glu': w_glu, 'b_glu': b_glu, 'g_ssm_out': g_ssm_out,
            'g_attn_out': g_attn_out, 'w_out': w_out, 'g_norm2': g_norm2, 'w_router': w_router,
            'router_bias': router_bias, 'w_exp_gate': w_exp_gate, 'w_exp_up': w_exp_up,
            'w_exp_down': w_exp_down, 'w_sh_gate': w_sh_gate, 'w_sh_up': w_sh_up,
            'w_sh_down': w_sh_down, 'g_final': g_final, 'w_ada_final': w_ada_final,
            'b_ada_final': b_ada_final}


def reference(x_prompt, x_sample, c_prompt, c_sample, cache_k, cache_v, cache_logf,
              state_ssm_re, state_ssm_im, page_table, w_ada, b_ada, g_norm1, w_in, b_fgate,
              ssm_lambda_re, ssm_lambda_im, ssm_log_step, ssm_b_re, ssm_b_im, ssm_c_re, ssm_c_im,
              ssm_d, w_glu, b_glu, g_ssm_out, g_attn_out, w_out, g_norm2, w_router, router_bias,
              w_exp_gate, w_exp_up, w_exp_down, w_sh_gate, w_sh_up, w_sh_down, g_final,
              w_ada_final, b_ada_final):
    n_seq, n_pages = page_table.shape
    n_past = n_pages * PAGE_SIZE
    xp, xs = x_prompt, x_sample
    kp, vp, lfp, srp, sip = [], [], [], [], []
    ks, vs, lfs, srs, sis = [], [], [], [], []
    for l in range(DEPTH):
        lp = dict(w_ada=w_ada[l], b_ada=b_ada[l], g_norm1=g_norm1[l], w_in=w_in[l],
                  b_fgate=b_fgate[l], lam_re=ssm_lambda_re[l], lam_im=ssm_lambda_im[l],
                  log_step=ssm_log_step[l], b_re=ssm_b_re[l], b_im=ssm_b_im[l],
                  c_re=ssm_c_re[l], c_im=ssm_c_im[l], d_skip=ssm_d[l], w_glu=w_glu[l],
                  b_glu=b_glu[l], g_ssm_out=g_ssm_out[l], g_attn_out=g_attn_out[l],
                  w_out=w_out[l], g_norm2=g_norm2[l], w_router=w_router[l],
                  router_bias=router_bias[l], w_eg=w_exp_gate[l], w_eu=w_exp_up[l],
                  w_ed=w_exp_down[l], w_sg=w_sh_gate[l], w_su=w_sh_up[l], w_sd=w_sh_down[l])
        k_past = cache_k[l][page_table].reshape(n_seq, n_past, N_HEADS, HEAD_DIM)
        v_past = cache_v[l][page_table].reshape(n_seq, n_past, N_HEADS, HEAD_DIM)
        logf_past = cache_logf[l][page_table].reshape(n_seq, n_past, N_HEADS)
        past = (state_ssm_re[l], state_ssm_im[l], k_past, v_past, logf_past)
        xp, st_p = _block(xp, c_prompt, None, **lp)
        xs, st_s = _block(xs, c_sample, past, **lp)
        kp.append(st_p[0]); vp.append(st_p[1]); lfp.append(st_p[2]); srp.append(st_p[3]); sip.append(st_p[4])
        ks.append(st_s[0]); vs.append(st_s[1]); lfs.append(st_s[2]); srs.append(st_s[3]); sis.append(st_s[4])
    shp, scp = _adaln(c_prompt, w_ada_final, b_ada_final, 2)
    shs, scs = _adaln(c_sample, w_ada_final, b_ada_final, 2)
    y_prompt = _norm_modulate(xp, g_final, shp, scp)
    y_sample = _norm_modulate(xs, g_final, shs, scs)
    return (y_prompt, y_sample, jnp.stack(kp), jnp.stack(vp), jnp.stack(lfp), jnp.stack(srp), jnp.stack(sip),
            jnp.stack(ks), jnp.stack(vs), jnp.stack(lfs), jnp.stack(srs), jnp.stack(sis))
```

```python
import functools
import math

import jax
import jax.numpy as jnp
import numpy as np
from jax import lax
from jax.experimental import pallas as pl
from jax.experimental.pallas import tpu as pltpu

F32 = jnp.float32
BF16 = jnp.bfloat16
I32 = jnp.int32

EPS = 1e-6
HEAD_DIM = 64
N_HEADS = 8
D_SSM = 512
D_ATTN = 512
SSM_GROUP = 16
SSM_STATE = 64
N_EXPERTS = 64
TOP_K = 6
N_EXPERT_GROUPS = 8
TOPK_GROUPS = 4
ROUTED_SCALE = 2.5
PAGE = 128

LANES = 128
SUBLANES = 8
VMEM_LIMIT = 48 * 1024 * 1024
MOE_ROWS = 256
NEG_INF = float("-inf")


def _cparams(sem):
    return pltpu.CompilerParams(dimension_semantics=sem, vmem_limit_bytes=VMEM_LIMIT)


def _bdot(a, b):
    return jnp.dot(a.astype(BF16), b.astype(BF16), preferred_element_type=F32)


def _bdot_nt(a, b):
    return lax.dot_general(a.astype(BF16), b.astype(BF16), (((1,), (1,)), ((), ())),
                           preferred_element_type=F32)


def _split3(v):
    hi = v.astype(BF16)
    r1 = v - hi.astype(F32)
    mid = r1.astype(BF16)
    lo = (r1 - mid.astype(F32)).astype(BF16)
    return hi, mid, lo


def _rms(x, g):
    return x * lax.rsqrt(jnp.mean(x * x, axis=-1, keepdims=True) + EPS) * g


def _adaln_kernel(c_ref, w_ref, b_ref, o_ref):
    c = c_ref[...]
    s = c * jax.nn.sigmoid(c)
    o_ref[...] = _bdot(s, w_ref[...]) + b_ref[...]


def _adaln(c, w, b):
    m, k = c.shape
    n = w.shape[1]
    tn = 1024
    return pl.pallas_call(
        _adaln_kernel,
        grid=(n // tn,),
        in_specs=[pl.BlockSpec((m, k), lambda j: (0, 0)),
                  pl.BlockSpec((k, tn), lambda j: (0, j)),
                  pl.BlockSpec((1, tn), lambda j: (0, j))],
        out_specs=pl.BlockSpec((m, tn), lambda j: (0, j)),
        out_shape=jax.ShapeDtypeStruct((m, n), F32),
        compiler_params=_cparams(("arbitrary",)),
    )(c, w, b.reshape(1, n))


def _inproj_kernel(x_ref, sc_ref, sh_ref, g_ref, w_ref, wft_ref, bf_ref,
                   u_ref, q_ref, k_ref, v_ref, kb_ref, vb_ref, lf_ref):
    nbb, rb, d = x_ref.shape
    x = x_ref[...]
    h = _rms(x, g_ref[...]) * (1.0 + sc_ref[...]) + sh_ref[...]
    hb = h.reshape(nbb * rb, d).astype(BF16)
    proj = jnp.dot(hb, w_ref[...], preferred_element_type=F32)
    u_ref[...] = proj[:, :D_SSM]
    q_ref[...] = (proj[:, D_SSM:D_SSM + D_ATTN] * (HEAD_DIM ** -0.5)).astype(BF16)
    k = proj[:, D_SSM + D_ATTN:D_SSM + 2 * D_ATTN]
    v = proj[:, D_SSM + 2 * D_ATTN:]
    k_ref[...] = k
    v_ref[...] = v
    kb_ref[...] = k.astype(BF16)
    vb_ref[...] = v.astype(BF16)
    z = _bdot_nt(wft_ref[...], hb) + bf_ref[...]
    lf_ref[...] = jnp.minimum(z, 0.0) - jnp.log1p(jnp.exp(-jnp.abs(z)))


def _inproj(x3, sc, sh, g, w_main, w_ft, b_f, nbb, rb):
    nb, r, d = x3.shape
    t = nb * r
    rows = nbb * rb
    nj = r // rb
    grid = (nb // nbb, nj)
    row_map = lambda i, j: (i * nj + j, 0)
    n_main = w_main.shape[1]
    outs = [jax.ShapeDtypeStruct((t, D_SSM), F32), jax.ShapeDtypeStruct((t, D_ATTN), BF16),
            jax.ShapeDtypeStruct((t, D_ATTN), F32), jax.ShapeDtypeStruct((t, D_ATTN), F32),
            jax.ShapeDtypeStruct((t, D_ATTN), BF16), jax.ShapeDtypeStruct((t, D_ATTN), BF16),
            jax.ShapeDtypeStruct((N_HEADS, t), F32)]
    out_specs = [pl.BlockSpec((rows, D_SSM), row_map)] + [pl.BlockSpec((rows, D_ATTN), row_map)] * 5 \
        + [pl.BlockSpec((N_HEADS, rows), lambda i, j: (0, i * nj + j))]
    return pl.pallas_call(
        _inproj_kernel,
        grid=grid,
        in_specs=[pl.BlockSpec((nbb, rb, d), lambda i, j: (i, j, 0)),
                  pl.BlockSpec((nbb, 1, d), lambda i, j: (i, 0, 0)),
                  pl.BlockSpec((nbb, 1, d), lambda i, j: (i, 0, 0)),
                  pl.BlockSpec((1, 1, d), lambda i, j: (0, 0, 0)),
                  pl.BlockSpec((d, n_main), lambda i, j: (0, 0)),
                  pl.BlockSpec((N_HEADS, d), lambda i, j: (0, 0)),
                  pl.BlockSpec((N_HEADS, 1), lambda i, j: (0, 0))],
        out_specs=out_specs,
        out_shape=outs,
        compiler_params=_cparams(("arbitrary", "arbitrary")),
    )(x3, sc, sh, g, w_main, w_ft, b_f)


def _cumsum_kernel(x_ref, tri_ref, lt_ref, o_ref):
    tri = tri_ref[...]
    lt = lt_ref[...]
    cs = sum(jnp.dot(p, tri, preferred_element_type=F32) for p in _split3(x_ref[...]))
    tot = jnp.broadcast_to(cs[:, LANES - 1:LANES], cs.shape)
    carry = sum(jnp.dot(lt, p, preferred_element_type=F32) for p in _split3(tot))
    o_ref[...] = -(cs + carry)


def _neg_cumsum(x2, gsz, rb):
    r = x2.shape[0]
    tri = np.triu(np.ones((LANES, LANES), np.float32))
    ii = np.arange(rb)
    lt = ((ii[:, None] // gsz == ii[None, :] // gsz) & (ii[None, :] < ii[:, None])).astype(np.float32)
    return pl.pallas_call(
        _cumsum_kernel,
        grid=(r // rb,),
        in_specs=[pl.BlockSpec((rb, LANES), lambda i: (i, 0)),
                  pl.BlockSpec((LANES, LANES), lambda i: (0, 0)),
                  pl.BlockSpec((rb, rb), lambda i: (0, 0))],
        out_specs=pl.BlockSpec((rb, LANES), lambda i: (i, 0)),
        out_shape=jax.ShapeDtypeStruct((r, LANES), F32),
        compiler_params=_cparams(("arbitrary",)),
    )(x2, jnp.asarray(tri, BF16), jnp.asarray(lt, BF16))


def _gelu_tanh(x):
    return 0.5 * x * (1.0 + jnp.tanh(math.sqrt(2.0 / math.pi) * (x + 0.044715 * (x * x * x))))


def _s5_kernel(u_ref, h0_ref, wb_ref, wc_ref, are_ref, aim_ref, dsk_ref, wglu_ref, bglu_ref, gout_ref,
               y_ref, ht_ref, s_ref, hc_ref, *, nseq, tm):
    rows = nseq * tm
    sr = rows + SUBLANES
    ti = pl.program_id(1)

    @pl.when(ti == 0)
    def _():
        hc_ref[...] = h0_ref[...]

    u = u_ref[...].reshape(rows, D_SSM)
    ub = u.astype(BF16)
    for c in range(4):
        bu = jnp.dot(ub[:, c * LANES:(c + 1) * LANES], wb_ref[c], preferred_element_type=F32)
        for jj in range(4):
            s_ref[pl.ds((4 * c + jj) * sr, rows), :] = bu[:, jj * LANES:(jj + 1) * LANES]
            s_ref[pl.ds((16 + 4 * c + jj) * sr, rows), :] = bu[:, 512 + jj * LANES:512 + (jj + 1) * LANES]

    ar = (are_ref[0:8, :], are_ref[8:16, :])
    ai = (aim_ref[0:8, :], aim_ref[8:16, :])

    def seq_group(sg, carry):
        base = sg * 4
        hs = []
        for b in range(4):
            hs.append(tuple(hc_ref[base + b, pl.ds(8 * q, 8), :] for q in range(4)))

        def step(t, hs):
            new = []
            for b in range(4):
                row = (base + b) * tm + t
                hr0, hr1, hi0, hi1 = hs[b]
                bre0 = s_ref[pl.ds(row, 8, stride=sr), :]
                bre1 = s_ref[pl.ds(8 * sr + row, 8, stride=sr), :]
                bim0 = s_ref[pl.ds(16 * sr + row, 8, stride=sr), :]
                bim1 = s_ref[pl.ds(24 * sr + row, 8, stride=sr), :]
                nr0 = ar[0] * hr0 - ai[0] * hi0 + bre0
                nr1 = ar[1] * hr1 - ai[1] * hi1 + bre1
                ni0 = ar[0] * hi0 + ai[0] * hr0 + bim0
                ni1 = ar[1] * hi1 + ai[1] * hr1 + bim1
                s_ref[pl.ds(row, 8, stride=sr), :] = nr0
                s_ref[pl.ds(8 * sr + row, 8, stride=sr), :] = nr1
                s_ref[pl.ds(16 * sr + row, 8, stride=sr), :] = ni0
                s_ref[pl.ds(24 * sr + row, 8, stride=sr), :] = ni1
                new.append((nr0, nr1, ni0, ni1))
            return tuple(new)

        hs = lax.fori_loop(0, tm, step, tuple(hs))
        for b in range(4):
            for q in range(4):
                hc_ref[base + b, pl.ds(8 * q, 8), :] = hs[b][q]
        return carry

    lax.fori_loop(0, nseq // 4, seq_group, 0)
    ht_ref[...] = hc_ref[...]

    ys = []
    for c in range(4):
        blocks = [s_ref[pl.ds((4 * c + jj) * sr, rows), :].astype(BF16) for jj in range(4)]
        blocks += [s_ref[pl.ds((16 + 4 * c + jj) * sr, rows), :].astype(BF16) for jj in range(4)]
        hcat = jnp.concatenate(blocks, axis=1)
        ys.append(jnp.dot(hcat, wc_ref[c], preferred_element_type=F32))
    y = jnp.concatenate(ys, axis=1) + dsk_ref[...] * u
    y = _gelu_tanh(y)
    gate = jax.nn.sigmoid(jnp.dot(y.astype(BF16), wglu_ref[...], preferred_element_type=F32) + bglu_ref[...])
    y = y * gate
    y_ref[...] = _rms(y, gout_ref[...]).astype(BF16).reshape(y_ref.shape)


def _s5(u3, h0, wb, wc, a_re, a_im, dsk, wglu, bglu, gout, nseq, tm):
    nb, length, _ = u3.shape
    rows = nseq * tm
    sr = rows + SUBLANES
    grid = (nb // nseq, length // tm)
    const2 = lambda i, j: (0, 0)
    const3 = lambda i, j: (0, 0, 0)
    if tm == length:
        y_spec = pl.BlockSpec((rows, D_SSM), lambda i, j: (i, 0))
        y_shape = jax.ShapeDtypeStruct((nb * length, D_SSM), BF16)
    else:
        y_spec = pl.BlockSpec((nseq, tm, D_SSM), lambda i, j: (i, j, 0))
        y_shape = jax.ShapeDtypeStruct((nb, length, D_SSM), BF16)
    return pl.pallas_call(
        functools.partial(_s5_kernel, nseq=nseq, tm=tm),
        grid=grid,
        in_specs=[pl.BlockSpec((nseq, tm, D_SSM), lambda i, j: (i, j, 0)),
                  pl.BlockSpec((nseq, 32, LANES), lambda i, j: (i, 0, 0)),
                  pl.BlockSpec((4, LANES, 1024), const3),
                  pl.BlockSpec((4, 1024, LANES), const3),
                  pl.BlockSpec((16, LANES), const2),
                  pl.BlockSpec((16, LANES), const2),
                  pl.BlockSpec((1, D_SSM), const2),
                  pl.BlockSpec((D_SSM, D_SSM), const2),
                  pl.BlockSpec((1, D_SSM), const2),
                  pl.BlockSpec((1, D_SSM), const2)],
        out_specs=[y_spec, pl.BlockSpec((nseq, 32, LANES), lambda i, j: (i, 0, 0))],
        out_shape=[y_shape, jax.ShapeDtypeStruct((nb, 32, LANES), F32)],
        scratch_shapes=[pltpu.VMEM((32 * sr, LANES), F32), pltpu.VMEM((nseq, 32, LANES), F32)],
        compiler_params=_cparams(("arbitrary", "arbitrary")),
    )(u3, h0, wb, wc, a_re, a_im, dsk, wglu, bglu, gout)


def _s5_params(lam_re, lam_im, log_step, b_re, b_im, c_re, c_im):
    g = lam_re.shape[0]
    dt = jnp.exp(log_step)[:, None]
    mag = jnp.exp(lam_re * dt)
    a_re = mag * jnp.cos(lam_im * dt)
    a_im = mag * jnp.sin(lam_im * dt)
    den = lam_re * lam_re + lam_im * lam_im
    n_re = a_re - 1.0
    f_re = (n_re * lam_re + a_im * lam_im) / den
    f_im = (a_im * lam_re - n_re * lam_im) / den
    bb_re = f_re[..., None] * b_re - f_im[..., None] * b_im
    bb_im = f_re[..., None] * b_im + f_im[..., None] * b_re
    eye = jnp.eye(g, dtype=F32)
    n_state = g * SSM_STATE

    def in_mat(bb):
        return jnp.einsum('gpi,gh->gihp', bb, eye).reshape(g * SSM_GROUP, n_state)

    def out_mat(cc):
        return jnp.einsum('gip,gh->gphi', cc, eye).reshape(n_state, g * SSM_GROUP)

    wbr, wbi = in_mat(bb_re), in_mat(bb_im)
    wcr, wci = out_mat(c_re), out_mat(-c_im)
    wb = jnp.stack([jnp.concatenate([wbr[c * 128:(c + 1) * 128, c * 512:(c + 1) * 512],
                                     wbi[c * 128:(c + 1) * 128, c * 512:(c + 1) * 512]], axis=1)
                    for c in range(4)]).astype(BF16)
    wc = jnp.stack([jnp.concatenate([wcr[c * 512:(c + 1) * 512, c * 128:(c + 1) * 128],
                                     wci[c * 512:(c + 1) * 512, c * 128:(c + 1) * 128]], axis=0)
                    for c in range(4)]).astype(BF16)
    return wb, wc, a_re.reshape(16, LANES), a_im.reshape(16, LANES)


def _attn_kernel(q_ref, k_ref, v_ref, cn_ref, g_ref, o_ref, acc_ref, *, tq):
    qi = pl.program_id(1)
    row = lax.broadcasted_iota(I32, (tq, tq), 0)
    col = lax.broadcasted_iota(I32, (tq, tq), 1)
    causal = col <= row

    for h in range(N_HEADS):
        lanes = slice(h * HEAD_DIM, (h + 1) * HEAD_DIM)
        qh = q_ref[0, :, lanes]

        def block(kb, carry, masked):
            m, l, acc = carry
            ks = pl.multiple_of(kb * tq, tq)
            kh = k_ref[0, pl.ds(ks, tq), lanes]
            vh = v_ref[0, pl.ds(ks, tq), lanes]
            s = lax.dot_general(qh, kh, (((1,), (1,)), ((), ())), preferred_element_type=F32)
            s = s + cn_ref[kb, h:h + 1, :]
            if masked:
                s = jnp.where(causal, s, NEG_INF)
            m_new = jnp.maximum(m, jnp.max(s, axis=1, keepdims=True))
            alpha = jnp.exp(m - m_new)
            p = jnp.exp(s - m_new)
            l = alpha * l + jnp.sum(p, axis=1, keepdims=True)
            acc = alpha * acc + jnp.dot(p.astype(BF16), vh, preferred_element_type=F32)
            return m_new, l, acc

        init = (jnp.full((tq, 1), NEG_INF, F32), jnp.zeros((tq, 1), F32), jnp.zeros((tq, HEAD_DIM), F32))
        carry = lax.fori_loop(0, qi, lambda kb, c: block(kb, c, False), init)
        m, l, acc = block(qi, carry, True)
        acc_ref[:, lanes] = acc / l

    o_ref[0] = _rms(acc_ref[...], g_ref[...]).astype(BF16)


def _attn_prompt(q, k, v, cn3, g, tq):
    b, length, d = q.shape
    nq = length // tq
    return pl.pallas_call(
        functools.partial(_attn_kernel, tq=tq),
        grid=(b, nq),
        in_specs=[pl.BlockSpec((1, tq, d), lambda i, j: (i, j, 0)),
                  pl.BlockSpec((1, length, d), lambda i, j: (i, 0, 0)),
                  pl.BlockSpec((1, length, d), lambda i, j: (i, 0, 0)),
                  pl.BlockSpec((nq, N_HEADS, tq), lambda i, j: (i, 0, 0)),
                  pl.BlockSpec((1, d), lambda i, j: (0, 0))],
        out_specs=pl.BlockSpec((1, tq, d), lambda i, j: (i, j, 0)),
        out_shape=jax.ShapeDtypeStruct((b, length, d), BF16),
        scratch_shapes=[pltpu.VMEM((tq, d), F32)],
        compiler_params=_cparams(("arbitrary", "arbitrary")),
    )(q, k, v, cn3, g)


def _attn_paged_kernel(pt_ref, q_ref, kn_ref, vn_ref, cn_ref, g_ref, kc_hbm, vc_hbm, o_ref,
                       kbuf, vbuf, sem, *, n_pages, n_new):
    i = pl.program_id(0)
    n = pl.num_programs(0)
    n_past = n_pages * PAGE

    def page_copies(seq, slot):
        cps = []
        for p in range(n_pages):
            pg = pt_ref[seq * n_pages + p]
            cps.append(pltpu.make_async_copy(kc_hbm.at[pg], kbuf.at[slot, pl.ds(p * PAGE, PAGE)], sem.at[0, slot]))
            cps.append(pltpu.make_async_copy(vc_hbm.at[pg], vbuf.at[slot, pl.ds(p * PAGE, PAGE)], sem.at[1, slot]))
        return cps

    @pl.when(i == 0)
    def _():
        for cp in page_copies(0, 0):
            cp.start()

    @pl.when(i + 1 < n)
    def _():
        for cp in page_copies(i + 1, (i + 1) % 2):
            cp.start()

    slot = i % 2
    for cp in page_copies(i, slot):
        cp.wait()

    d = q_ref.shape[-1]
    nr = N_HEADS * n_new
    q8 = q_ref[0]
    new_bits = n_new.bit_length() - 1
    head_bits = HEAD_DIM.bit_length() - 1
    rowh = lax.shift_right_logical(lax.broadcasted_iota(I32, (nr, d), 0), new_bits)
    colh = lax.shift_right_logical(lax.broadcasted_iota(I32, (nr, d), 1), head_bits)
    bd = rowh == colh
    qrep = jnp.broadcast_to(q8[None], (N_HEADS, n_new, d)).reshape(nr, d)
    qbd = jnp.where(bd, qrep, 0.0).astype(BF16)

    cn = cn_ref[0]
    cnr = jnp.broadcast_to(cn[:, None, :], (N_HEADS, n_new, cn.shape[-1])).reshape(nr, cn.shape[-1])

    s_p = _bdot_nt(qbd, kbuf[slot]) + cnr[:, :n_past]
    s_n = _bdot_nt(qbd, kn_ref[0]) + cnr[:, n_past:n_past + n_new]
    qpos = lax.broadcasted_iota(I32, (nr, n_new), 0) & (n_new - 1)
    kpos = lax.broadcasted_iota(I32, (nr, n_new), 1)
    s_n = jnp.where(kpos <= qpos, s_n, NEG_INF)
    m = jnp.maximum(jnp.max(s_p, axis=1, keepdims=True), jnp.max(s_n, axis=1, keepdims=True))
    p_p = jnp.exp(s_p - m)
    p_n = jnp.exp(s_n - m)
    l = jnp.sum(p_p, axis=1, keepdims=True) + jnp.sum(p_n, axis=1, keepdims=True)
    of = _bdot(p_p, vbuf[slot]) + _bdot(p_n, vn_ref[0])
    of = jnp.where(bd, of / l, 0.0)
    o = jnp.sum(of.reshape(N_HEADS, n_new, d), axis=0)
    o_ref[0] = _rms(o, g_ref[...])


def _attn_paged(page_table, q, kn, vn, cn, g, cache_k, cache_v):
    nseq, n_new, d = q.shape
    assert n_new & (n_new - 1) == 0, "new-token count must be a power of two"
    n_pages = page_table.shape[1]
    n_past = n_pages * PAGE
    grid_spec = pltpu.PrefetchScalarGridSpec(
        num_scalar_prefetch=1,
        grid=(nseq,),
        in_specs=[pl.BlockSpec((1, n_new, d), lambda i, pt: (i, 0, 0)),
                  pl.BlockSpec((1, n_new, d), lambda i, pt: (i, 0, 0)),
                  pl.BlockSpec((1, n_new, d), lambda i, pt: (i, 0, 0)),
                  pl.BlockSpec((1, N_HEADS, cn.shape[-1]), lambda i, pt: (i, 0, 0)),
                  pl.BlockSpec((1, d), lambda i, pt: (0, 0)),
                  pl.BlockSpec(memory_space=pl.ANY),
                  pl.BlockSpec(memory_space=pl.ANY)],
        out_specs=pl.BlockSpec((1, n_new, d), lambda i, pt: (i, 0, 0)),
        scratch_shapes=[pltpu.VMEM((2, n_past, d), F32), pltpu.VMEM((2, n_past, d), F32),
                        pltpu.SemaphoreType.DMA((2, 2))],
    )
    return pl.pallas_call(
        functools.partial(_attn_paged_kernel, n_pages=n_pages, n_new=n_new),
        grid_spec=grid_spec,
        out_shape=jax.ShapeDtypeStruct((nseq, n_new, d), F32),
        compiler_params=_cparams(("arbitrary",)),
    )(page_table.reshape(-1), q, kn, vn, cn, g, cache_k, cache_v)


def _outproj_kernel(x_ref, ys_ref, oa_ref, gt_ref, sc_ref, sh_ref, g_ref, wo_ref, wrt_ref,
                    x1_ref, h2_ref, lg_ref):
    nbb, rb, d = x_ref.shape
    mix = jnp.dot(ys_ref[...], wo_ref[0:D_SSM, :], preferred_element_type=F32)
    mix = mix + jnp.dot(oa_ref[...], wo_ref[D_SSM:, :], preferred_element_type=F32)
    x1 = x_ref[...] + gt_ref[...] * mix.reshape(nbb, rb, d)
    x1_ref[...] = x1
    h2 = (_rms(x1, g_ref[...]) * (1.0 + sc_ref[...]) + sh_ref[...]).reshape(nbb * rb, d)
    hb = h2.astype(BF16)
    h2_ref[...] = hb
    hlo = (h2 - hb.astype(F32)).astype(BF16)
    whi = wrt_ref[0]
    wlo = wrt_ref[1]
    nt = (((1,), (1,)), ((), ()))
    lg = lax.dot_general(whi, hb, nt, preferred_element_type=F32)
    lg = lg + lax.dot_general(wlo, hb, nt, preferred_element_type=F32)
    lg = lg + lax.dot_general(whi, hlo, nt, preferred_element_type=F32)
    lg_ref[...] = lg


def _outproj(x3, ys, oa, gt, sc, sh, g, wo, wrt, nbb, rb):
    nb, r, d = x3.shape
    t = nb * r
    rows = nbb * rb
    nj = r // rb
    row_map = lambda i, j: (i * nj + j, 0)
    mod_spec = pl.BlockSpec((nbb, 1, d), lambda i, j: (i, 0, 0))
    return pl.pallas_call(
        _outproj_kernel,
        grid=(nb // nbb, nj),
        in_specs=[pl.BlockSpec((nbb, rb, d), lambda i, j: (i, j, 0)),
                  pl.BlockSpec((rows, D_SSM), row_map),
                  pl.BlockSpec((rows, D_ATTN), row_map),
                  mod_spec, mod_spec, mod_spec,
                  pl.BlockSpec((1, 1, d), lambda i, j: (0, 0, 0)),
                  pl.BlockSpec((D_SSM + D_ATTN, d), lambda i, j: (0, 0)),
                  pl.BlockSpec((2, N_EXPERTS, d), lambda i, j: (0, 0, 0))],
        out_specs=[pl.BlockSpec((nbb, rb, d), lambda i, j: (i, j, 0)),
                   pl.BlockSpec((rows, d), row_map),
                   pl.BlockSpec((N_EXPERTS, rows), lambda i, j: (0, i * nj + j))],
        out_shape=[jax.ShapeDtypeStruct((nb, r, d), F32),
                   jax.ShapeDtypeStruct((t, d), BF16),
                   jax.ShapeDtypeStruct((N_EXPERTS, t), F32)],
        compiler_params=_cparams(("arbitrary", "arbitrary")),
    )(x3, ys, oa, gt, sc, sh, g, wo, wrt)


def _router_kernel(lg_ref, rb_ref, ut_ref, idx_ref, w_ref, rank_ref, cnt_ref, carry_ref):
    i = pl.program_id(0)
    tm = lg_ref.shape[1]
    per_group = N_EXPERTS // N_EXPERT_GROUPS

    @pl.when(i == 0)
    def _():
        carry_ref[...] = jnp.zeros_like(carry_ref)

    scores = jax.nn.sigmoid(lg_ref[...])
    biased = scores + rb_ref[...]
    blks, grp = [], []
    for g in range(N_EXPERT_GROUPS):
        blk = biased[g * per_group:(g + 1) * per_group, :]
        m1 = jnp.max(blk, axis=0, keepdims=True)
        eq = blk == m1
        n_eq = jnp.sum(jnp.where(eq, 1.0, 0.0), axis=0, keepdims=True)
        m2 = jnp.max(jnp.where(eq, NEG_INF, blk), axis=0, keepdims=True)
        blks.append(blk)
        grp.append(m1 + jnp.where(n_eq >= 2.0, m1, m2))
    masked = []
    for g in range(N_EXPERT_GROUPS):
        beaten = jnp.zeros((1, tm), F32)
        for o in range(N_EXPERT_GROUPS):
            if o == g:
                continue
            ahead = (grp[o] >= grp[g]) if o < g else (grp[o] > grp[g])
            beaten = beaten + jnp.where(ahead, 1.0, 0.0)
        masked.append(jnp.where(beaten < float(TOPK_GROUPS), blks[g], NEG_INF))
    work = jnp.concatenate(masked, axis=0)

    eid = lax.broadcasted_iota(I32, (N_EXPERTS, tm), 0)
    chosen = jnp.zeros((N_EXPERTS, tm), F32)
    idxs, ws, sels = [], [], []
    for _ in range(TOP_K):
        m = jnp.max(work, axis=0, keepdims=True)
        first = jnp.min(jnp.where(work == m, eid, N_EXPERTS), axis=0, keepdims=True)
        sel = eid == first
        idxs.append(first)
        ws.append(jnp.sum(jnp.where(sel, scores, 0.0), axis=0, keepdims=True))
        sels.append(sel)
        chosen = jnp.where(sel, 1.0, chosen)
        work = jnp.where(sel, NEG_INF, work)
    wsum = ws[0]
    for wk in ws[1:]:
        wsum = wsum + wk

    prefix = jnp.dot(chosen.astype(BF16), ut_ref[...], preferred_element_type=F32) + carry_ref[...]
    carry_ref[...] = carry_ref[...] + jnp.sum(chosen, axis=1, keepdims=True)
    cnt_ref[...] = carry_ref[...]

    idx_ref[...] = jnp.zeros_like(idx_ref)
    w_ref[...] = jnp.zeros_like(w_ref)
    rank_ref[...] = jnp.zeros_like(rank_ref)
    for k in range(TOP_K):
        idx_ref[k:k + 1, :] = idxs[k]
        w_ref[k:k + 1, :] = ws[k] / wsum * ROUTED_SCALE
        rank = jnp.sum(jnp.where(sels[k], prefix, 0.0), axis=0, keepdims=True)
        rank_ref[k:k + 1, :] = rank.astype(I32)


def _router(lg, router_bias, tm):
    e, t = lg.shape
    ut = np.triu(np.ones((tm, tm), np.float32), 1)
    tok_spec = pl.BlockSpec((SUBLANES, tm), lambda i: (0, i))
    return pl.pallas_call(
        _router_kernel,
        grid=(t // tm,),
        in_specs=[pl.BlockSpec((e, tm), lambda i: (0, i)),
                  pl.BlockSpec((e, 1), lambda i: (0, 0)),
                  pl.BlockSpec((tm, tm), lambda i: (0, 0))],
        out_specs=[tok_spec, tok_spec, tok_spec, pl.BlockSpec((e, 1), lambda i: (0, 0))],
        out_shape=[jax.ShapeDtypeStruct((SUBLANES, t), I32), jax.ShapeDtypeStruct((SUBLANES, t), F32),
                   jax.ShapeDtypeStruct((SUBLANES, t), I32), jax.ShapeDtypeStruct((e, 1), F32)],
        scratch_shapes=[pltpu.VMEM((e, 1), F32)],
        compiler_params=_cparams(("arbitrary",)),
    )(lg, router_bias.reshape(e, 1), jnp.asarray(ut, BF16))


def _experts_kernel(be_ref, nv_ref, x_ref, wg_ref, wu_ref, wd_ref, o_ref):
    i = pl.program_id(0)

    @pl.when(i < nv_ref[0])
    def _():
        x = x_ref[...]
        g = jnp.dot(x, wg_ref[0].astype(BF16), preferred_element_type=F32)
        u = jnp.dot(x, wu_ref[0].astype(BF16), preferred_element_type=F32)
        a = (g * jax.nn.sigmoid(g) * u).astype(BF16)
        o_ref[...] = jnp.dot(a, wd_ref[0].astype(BF16), preferred_element_type=F32)

    @pl.when(i >= nv_ref[0])
    def _():
        o_ref[...] = jnp.zeros_like(o_ref)


def _experts(block_e, n_valid, xs, w_eg, w_eu, w_ed, m):
    n_rows, d = xs.shape
    de = w_eg.shape[-1]
    nblk = n_rows // m
    xmap = lambda i, be, nv: (jnp.minimum(i, nv[0] - 1), 0)
    wmap = lambda i, be, nv: (be[i], 0, 0)
    grid_spec = pltpu.PrefetchScalarGridSpec(
        num_scalar_prefetch=2,
        grid=(nblk,),
        in_specs=[pl.BlockSpec((m, d), xmap),
                  pl.BlockSpec((1, d, de), wmap),
                  pl.BlockSpec((1, d, de), wmap),
                  pl.BlockSpec((1, de, d), wmap)],
        out_specs=pl.BlockSpec((m, d), lambda i, be, nv: (i, 0)),
    )
    return pl.pallas_call(
        _experts_kernel,
        grid_spec=grid_spec,
        out_shape=jax.ShapeDtypeStruct((n_rows, d), F32),
        compiler_params=_cparams(("arbitrary",)),
    )(block_e, n_valid, xs, w_eg, w_eu, w_ed)


def _final_kernel(x1_ref, h2_ref, rt_ref, gt_ref, sc_ref, sh_ref, g_ref, wgu_ref, wd_ref, y_ref):
    nbb, rb, d = x1_ref.shape
    ds = wd_ref.shape[0]
    gu = jnp.dot(h2_ref[...], wgu_ref[...], preferred_element_type=F32)
    g = gu[:, :ds]
    a = (g * jax.nn.sigmoid(g) * gu[:, ds:]).astype(BF16)
    ff = rt_ref[...] + jnp.dot(a, wd_ref[...], preferred_element_type=F32)
    x2 = x1_ref[...] + gt_ref[...] * ff.reshape(nbb, rb, d)
    y_ref[...] = _rms(x2, g_ref[...]) * (1.0 + sc_ref[...]) + sh_ref[...]


def _final(x1, h2, routed, gt, sc, sh, g, wgu, wd, nbb, rb):
    nb, r, d = x1.shape
    rows = nbb * rb
    nj = r // rb
    row_map = lambda i, j: (i * nj + j, 0)
    mod_spec = pl.BlockSpec((nbb, 1, d), lambda i, j: (i, 0, 0))
    return pl.pallas_call(
        _final_kernel,
        grid=(nb // nbb, nj),
        in_specs=[pl.BlockSpec((nbb, rb, d), lambda i, j: (i, j, 0)),
                  pl.BlockSpec((rows, d), row_map),
                  pl.BlockSpec((rows, d), row_map),
                  mod_spec, mod_spec, mod_spec,
                  pl.BlockSpec((1, 1, d), lambda i, j: (0, 0, 0)),
                  pl.BlockSpec(wgu.shape, lambda i, j: (0, 0)),
                  pl.BlockSpec(wd.shape, lambda i, j: (0, 0))],
        out_specs=pl.BlockSpec((nbb, rb, d), lambda i, j: (i, j, 0)),
        out_shape=jax.ShapeDtypeStruct((nb, r, d), F32),
        compiler_params=_cparams(("arbitrary", "arbitrary")),
    )(x1, h2, routed, gt, sc, sh, g, wgu, wd)


def _moe(h2, lg, router_bias, w_eg, w_eu, w_ed, tm_router):
    t, d = h2.shape
    e = N_EXPERTS
    m = MOE_ROWS
    idx_t, w_t, rank_t, counts = _router(lg, router_bias, tm_router)
    idx = idx_t[:TOP_K].T
    wts = w_t[:TOP_K].T
    rank = rank_t[:TOP_K].T
    counts = counts.reshape(e).astype(I32)
    padded = ((counts + m - 1) // m) * m
    pad_end = jnp.cumsum(padded)
    pad_start = pad_end - padded
    dest = pad_start[idx] + rank
    a = t * TOP_K
    n_rows = (-(-a // m)) * m + e * m
    nblk = n_rows // m
    tok = jnp.broadcast_to(jnp.arange(t, dtype=I32)[:, None], (t, TOP_K))
    src_tok = jnp.full((n_rows,), t, I32).at[dest.reshape(-1)].set(tok.reshape(-1))
    block_start = jnp.arange(nblk, dtype=I32) * m
    block_e = jnp.minimum(jnp.searchsorted(pad_end, block_start, side='right'), e - 1).astype(I32)
    n_valid = (pad_end[-1] // m).astype(I32).reshape(1)
    x_pad = jnp.concatenate([h2, jnp.zeros((1, d), h2.dtype)], axis=0)
    xs = x_pad[src_tok]
    yb = _experts(block_e, n_valid, xs, w_eg, w_eu, w_ed, m)
    return (yb[dest.reshape(-1)].reshape(t, TOP_K, d) * wts[..., None]).sum(axis=1)


def kernel(x_prompt, x_sample, c_prompt, c_sample, cache_k, cache_v, cache_logf, state_ssm_re, state_ssm_im, page_table, w_ada, b_ada, g_norm1, w_in, b_fgate, ssm_lambda_re, ssm_lambda_im, ssm_log_step, ssm_b_re, ssm_b_im, ssm_c_re, ssm_c_im, ssm_d, w_glu, b_glu, g_ssm_out, g_attn_out, w_out, g_norm2, w_router, router_bias, w_exp_gate, w_exp_up, w_exp_down, w_sh_gate, w_sh_up, w_sh_down, g_final, w_ada_final, b_ada_final):
    depth = w_ada.shape[0]
    assert depth == 1, "one layer is supported"
    bp, lp, d = x_prompt.shape
    bs, ls, _ = x_sample.shape
    n_pages = page_table.shape[1]
    n_past = n_pages * PAGE
    n_groups = ssm_lambda_re.shape[1]

    n_c = bp + bs
    n_c_pad = -(-n_c // SUBLANES) * SUBLANES
    c_all = jnp.concatenate([c_prompt, c_sample, jnp.zeros((n_c_pad - n_c, d), F32)], axis=0)
    mod = _adaln(c_all, w_ada[0], b_ada[0])
    modf = _adaln(c_all, w_ada_final, b_ada_final)

    def mods(lo, hi):
        parts = [mod[lo:hi, k * d:(k + 1) * d][:, None, :] for k in range(6)]
        parts += [modf[lo:hi, k * d:(k + 1) * d][:, None, :] for k in range(2)]
        return parts

    w_main = w_in[0][:, :D_SSM + 3 * D_ATTN].astype(BF16)
    w_ft = w_in[0][:, D_SSM + 3 * D_ATTN:].T.astype(BF16)
    b_f = b_fgate[0].reshape(N_HEADS, 1)
    g1 = g_norm1[0].reshape(1, 1, d)
    g2 = g_norm2[0].reshape(1, 1, d)
    gf = g_final.reshape(1, 1, d)
    wb, wc, a_re, a_im = _s5_params(ssm_lambda_re[0], ssm_lambda_im[0], ssm_log_step[0], ssm_b_re[0],
                                    ssm_b_im[0], ssm_c_re[0], ssm_c_im[0])
    dsk = ssm_d[0].reshape(1, D_SSM)
    wglu = w_glu[0].astype(BF16)
    bglu = b_glu[0].reshape(1, D_SSM)
    g_so = g_ssm_out[0].reshape(1, D_SSM)
    g_ao = g_attn_out[0].reshape(1, D_ATTN)
    wo = w_out[0].astype(BF16)
    wr_t = w_router[0].T
    wr_hi = wr_t.astype(BF16)
    wrt = jnp.stack([wr_hi, (wr_t - wr_hi.astype(F32)).astype(BF16)])
    wgu = jnp.concatenate([w_sh_gate[0], w_sh_up[0]], axis=1).astype(BF16)
    wsd = w_sh_down[0].astype(BF16)

    def ssm_state(re, im):
        return jnp.concatenate([re.reshape(-1, 16, LANES), im.reshape(-1, 16, LANES)], axis=1)

    def split_state(ht):
        n = ht.shape[0]
        return (ht[:, :16].reshape(1, n, n_groups, SSM_STATE), ht[:, 16:].reshape(1, n, n_groups, SSM_STATE))

    tm = 512
    sh1, sc1, gt1, sh2, sc2, gt2, shf, scf = mods(0, bp)
    u, q, k, v, kb, vb, lft = _inproj(x_prompt, sc1, sh1, g1, w_main, w_ft, b_f, 1, tm)
    tq = 256
    cn = _neg_cumsum(lft.reshape(-1, LANES), lp // LANES, N_HEADS * bp * lp // LANES)
    cn3 = cn.reshape(N_HEADS, bp * lp // tq, tq).transpose(1, 0, 2)
    ys, ht = _s5(u.reshape(bp, lp, D_SSM), jnp.zeros((bp, 32, LANES), F32), wb, wc, a_re, a_im, dsk,
                 wglu, bglu, g_so, bp, 256)
    oa = _attn_prompt(q.reshape(bp, lp, D_ATTN), kb.reshape(bp, lp, D_ATTN), vb.reshape(bp, lp, D_ATTN),
                      cn3, g_ao, tq)
    x1, h2, lg = _outproj(x_prompt, ys.reshape(bp * lp, D_SSM), oa.reshape(bp * lp, D_ATTN),
                          gt1, sc2, sh2, g2, wo, wrt, 1, tm)
    routed = _moe(h2, lg, router_bias[0], w_exp_gate[0], w_exp_up[0], w_exp_down[0], 512)
    y_prompt = _final(x1, h2, routed, gt2, scf, shf, gf, wgu, wsd, 1, tm)
    k_prompt = k.reshape(1, bp, lp, N_HEADS, HEAD_DIM)
    v_prompt = v.reshape(1, bp, lp, N_HEADS, HEAD_DIM)
    logf_prompt = lft.T.reshape(1, bp, lp, N_HEADS)
    sre_p, sim_p = split_state(ht)

    nbb = 64
    sh1, sc1, gt1, sh2, sc2, gt2, shf, scf = mods(bp, bp + bs)
    u, q, k, v, kb, vb, lft = _inproj(x_sample, sc1, sh1, g1, w_main, w_ft, b_f, nbb, ls)
    ys, ht = _s5(u.reshape(bs, ls, D_SSM), ssm_state(state_ssm_re[0], state_ssm_im[0]), wb, wc, a_re, a_im,
                 dsk, wglu, bglu, g_so, 32, ls)
    lf_past = cache_logf[0][page_table].reshape(bs, n_past, N_HEADS).transpose(0, 2, 1)
    lf_new = lft.reshape(N_HEADS, bs, ls).transpose(1, 0, 2)
    n_key_pad = -(-(n_past + ls) // LANES) * LANES
    lf_all = jnp.concatenate([lf_past, lf_new, jnp.zeros((bs, N_HEADS, n_key_pad - n_past - ls), F32)], axis=2)
    gsz = n_key_pad // LANES
    cn_s = _neg_cumsum(lf_all.reshape(-1, LANES), gsz, gsz * 64).reshape(bs, N_HEADS, n_key_pad)
    oa = _attn_paged(page_table, q.astype(F32).reshape(bs, ls, D_ATTN), k.reshape(bs, ls, D_ATTN),
                     v.reshape(bs, ls, D_ATTN), cn_s, g_ao,
                     cache_k[0].reshape(-1, PAGE, D_ATTN), cache_v[0].reshape(-1, PAGE, D_ATTN))
    x1, h2, lg = _outproj(x_sample, ys, oa.reshape(bs * ls, D_ATTN).astype(BF16),
                          gt1, sc2, sh2, g2, wo, wrt, nbb, ls)
    routed = _moe(h2, lg, router_bias[0], w_exp_gate[0], w_exp_up[0], w_exp_down[0], 512)
    y_sample = _final(x1, h2, routed, gt2, scf, shf, gf, wgu, wsd, nbb, ls)
    k_sample = k.reshape(1, bs, ls, N_HEADS, HEAD_DIM)
    v_sample = v.reshape(1, bs, ls, N_HEADS, HEAD_DIM)
    logf_sample = lft.T.reshape(1, bs, ls, N_HEADS)
    sre_s, sim_s = split_state(ht)

    return (y_prompt, y_sample, k_prompt, v_prompt, logf_prompt, sre_p, sim_p,
            k_sample, v_sample, logf_sample, sre_s, sim_s)
```

```python
import functools
import math

import jax
import jax.numpy as jnp
import numpy as np
from jax import lax
from jax.experimental import pallas as pl
from jax.experimental.pallas import tpu as pltpu

F32 = jnp.float32
BF16 = jnp.bfloat16
I32 = jnp.int32

EPS = 1e-6
HEAD_DIM = 64
N_HEADS = 8
D_SSM = 512
D_ATTN = 512
SSM_GROUP = 16
SSM_STATE = 64
N_EXPERTS = 64
TOP_K = 6
N_EXPERT_GROUPS = 8
TOPK_GROUPS = 4
ROUTED_SCALE = 2.5
PAGE = 128

LANES = 128
SUBLANES = 8
VMEM_LIMIT = 48 * 1024 * 1024
MOE_ROWS = 256
NEG_INF = float("-inf")


def _cparams(sem):
    return pltpu.CompilerParams(dimension_semantics=sem, vmem_limit_bytes=VMEM_LIMIT)


def _bdot(a, b):
    return jnp.dot(a.astype(BF16), b.astype(BF16), preferred_element_type=F32)


def _bdot_nt(a, b):
    return lax.dot_general(a.astype(BF16), b.astype(BF16), (((1,), (1,)), ((), ())),
                           preferred_element_type=F32)


def _split3(v):
    hi = v.astype(BF16)
    r1 = v - hi.astype(F32)
    mid = r1.astype(BF16)
    lo = (r1 - mid.astype(F32)).astype(BF16)
    return hi, mid, lo


def _rms(x, g):
    return x * lax.rsqrt(jnp.mean(x * x, axis=-1, keepdims=True) + EPS) * g


def _adaln_kernel(c_ref, w_ref, b_ref, o_ref):
    c = c_ref[...]
    s = c * jax.nn.sigmoid(c)
    o_ref[...] = _bdot(s, w_ref[...]) + b_ref[...]


def _adaln(c, w, b):
    m, k = c.shape
    n = w.shape[1]
    tn = 1024
    return pl.pallas_call(
        _adaln_kernel,
        grid=(n // tn,),
        in_specs=[pl.BlockSpec((m, k), lambda j: (0, 0)),
                  pl.BlockSpec((k, tn), lambda j: (0, j)),
                  pl.BlockSpec((1, tn), lambda j: (0, j))],
        out_specs=pl.BlockSpec((m, tn), lambda j: (0, j)),
        out_shape=jax.ShapeDtypeStruct((m, n), F32),
        compiler_params=_cparams(("arbitrary",)),
    )(c, w, b.reshape(1, n))


def _inproj_kernel(x_ref, sc_ref, sh_ref, g_ref, w_ref, wft_ref, bf_ref,
                   u_ref, q_ref, k_ref, v_ref, kb_ref, vb_ref, lf_ref):
    nbb, rb, d = x_ref.shape
    x = x_ref[...]
    h = _rms(x, g_ref[...]) * (1.0 + sc_ref[...]) + sh_ref[...]
    hb = h.reshape(nbb * rb, d).astype(BF16)
    proj = jnp.dot(hb, w_ref[...], preferred_element_type=F32)
    u_ref[...] = proj[:, :D_SSM]
    q_ref[...] = (proj[:, D_SSM:D_SSM + D_ATTN] * (HEAD_DIM ** -0.5)).astype(BF16)
    k = proj[:, D_SSM + D_ATTN:D_SSM + 2 * D_ATTN]
    v = proj[:, D_SSM + 2 * D_ATTN:]
    k_ref[...] = k
    v_ref[...] = v
    kb_ref[...] = k.astype(BF16)
    vb_ref[...] = v.astype(BF16)
    z = _bdot_nt(wft_ref[...], hb) + bf_ref[...]
    lf_ref[...] = jnp.minimum(z, 0.0) - jnp.log1p(jnp.exp(-jnp.abs(z)))


def _inproj(x3, sc, sh, g, w_main, w_ft, b_f, nbb, rb):
    nb, r, d = x3.shape
    t = nb * r
    rows = nbb * rb
    nj = r // rb
    grid = (nb // nbb, nj)
    row_map = lambda i, j: (i * nj + j, 0)
    n_main = w_main.shape[1]
    outs = [jax.ShapeDtypeStruct((t, D_SSM), F32), jax.ShapeDtypeStruct((t, D_ATTN), BF16),
            jax.ShapeDtypeStruct((t, D_ATTN), F32), jax.ShapeDtypeStruct((t, D_ATTN), F32),
            jax.ShapeDtypeStruct((t, D_ATTN), BF16), jax.ShapeDtypeStruct((t, D_ATTN), BF16),
            jax.ShapeDtypeStruct((N_HEADS, t), F32)]
    out_specs = [pl.BlockSpec((rows, D_SSM), row_map)] + [pl.BlockSpec((rows, D_ATTN), row_map)] * 5 \
        + [pl.BlockSpec((N_HEADS, rows), lambda i, j: (0, i * nj + j))]
    return pl.pallas_call(
        _inproj_kernel,
        grid=grid,
        in_specs=[pl.BlockSpec((nbb, rb, d), lambda i, j: (i, j, 0)),
                  pl.BlockSpec((nbb, 1, d), lambda i, j: (i, 0, 0)),
                  pl.BlockSpec((nbb, 1, d), lambda i, j: (i, 0, 0)),
                  pl.BlockSpec((1, 1, d), lambda i, j: (0, 0, 0)),
                  pl.BlockSpec((d, n_main), lambda i, j: (0, 0)),
                  pl.BlockSpec((N_HEADS, d), lambda i, j: (0, 0)),
                  pl.BlockSpec((N_HEADS, 1), lambda i, j: (0, 0))],
        out_specs=out_specs,
        out_shape=outs,
        compiler_params=_cparams(("arbitrary", "arbitrary")),
    )(x3, sc, sh, g, w_main, w_ft, b_f)


def _cumsum_kernel(x_ref, tri_ref, lt_ref, o_ref):
    tri = tri_ref[...]
    lt = lt_ref[...]
    cs = sum(jnp.dot(p, tri, preferred_element_type=F32) for p in _split3(x_ref[...]))
    tot = jnp.broadcast_to(cs[:, LANES - 1:LANES], cs.shape)
    carry = sum(jnp.dot(lt, p, preferred_element_type=F32) for p in _split3(tot))
    o_ref[...] = -(cs + carry)


def _neg_cumsum(x2, gsz, rb):
    r = x2.shape[0]
    tri = np.triu(np.ones((LANES, LANES), np.float32))
    ii = np.arange(rb)
    lt = ((ii[:, None] // gsz == ii[None, :] // gsz) & (ii[None, :] < ii[:, None])).astype(np.float32)
    return pl.pallas_call(
        _cumsum_kernel,
        grid=(r // rb,),
        in_specs=[pl.BlockSpec((rb, LANES), lambda i: (i, 0)),
                  pl.BlockSpec((LANES, LANES), lambda i: (0, 0)),
                  pl.BlockSpec((rb, rb), lambda i: (0, 0))],
        out_specs=pl.BlockSpec((rb, LANES), lambda i: (i, 0)),
        out_shape=jax.ShapeDtypeStruct((r, LANES), F32),
        compiler_params=_cparams(("arbitrary",)),
    )(x2, jnp.asarray(tri, BF16), jnp.asarray(lt, BF16))


def _gelu_tanh(x):
    return 0.5 * x * (1.0 + jnp.tanh(math.sqrt(2.0 / math.pi) * (x + 0.044715 * (x * x * x))))


def _s5_kernel(u_ref, h0_ref, wb_ref, wc_ref, are_ref, aim_ref, dsk_ref, wglu_ref, bglu_ref, gout_ref,
               y_ref, ht_ref, s_ref, hc_ref, *, nseq, tm):
    rows = nseq * tm
    sr = rows + SUBLANES
    ti = pl.program_id(1)

    @pl.when(ti == 0)
    def _():
        hc_ref[...] = h0_ref[...]

    u = u_ref[...].reshape(rows, D_SSM)
    ub = u.astype(BF16)
    for c in range(4):
        bu = jnp.dot(ub[:, c * LANES:(c + 1) * LANES], wb_ref[c], preferred_element_type=F32)
        for jj in range(4):
            s_ref[pl.ds((4 * c + jj) * sr, rows), :] = bu[:, jj * LANES:(jj + 1) * LANES]
            s_ref[pl.ds((16 + 4 * c + jj) * sr, rows), :] = bu[:, 512 + jj * LANES:512 + (jj + 1) * LANES]

    ar = (are_ref[0:8, :], are_ref[8:16, :])
    ai = (aim_ref[0:8, :], aim_ref[8:16, :])

    def seq_group(sg, carry):
        base = sg * 4
        hs = []
        for b in range(4):
            hs.append(tuple(hc_ref[base + b, pl.ds(8 * q, 8), :] for q in range(4)))

        def step(t, hs):
            new = []
            for b in range(4):
                row = (base + b) * tm + t
                hr0, hr1, hi0, hi1 = hs[b]
                bre0 = s_ref[pl.ds(row, 8, stride=sr), :]
                bre1 = s_ref[pl.ds(8 * sr + row, 8, stride=sr), :]
                bim0 = s_ref[pl.ds(16 * sr + row, 8, stride=sr), :]
                bim1 = s_ref[pl.ds(24 * sr + row, 8, stride=sr), :]
                nr0 = ar[0] * hr0 - ai[0] * hi0 + bre0
                nr1 = ar[1] * hr1 - ai[1] * hi1 + bre1
                ni0 = ar[0] * hi0 + ai[0] * hr0 + bim0
                ni1 = ar[1] * hi1 + ai[1] * hr1 + bim1
                s_ref[pl.ds(row, 8, stride=sr), :] = nr0
                s_ref[pl.ds(8 * sr + row, 8, stride=sr), :] = nr1
                s_ref[pl.ds(16 * sr + row, 8, stride=sr), :] = ni0
                s_ref[pl.ds(24 * sr + row, 8, stride=sr), :] = ni1
                new.append((nr0, nr1, ni0, ni1))
            return tuple(new)

        hs = lax.fori_loop(0, tm, step, tuple(hs))
        for b in range(4):
            for q in range(4):
                hc_ref[base + b, pl.ds(8 * q, 8), :] = hs[b][q]
        return carry

    lax.fori_loop(0, nseq // 4, seq_group, 0)
    ht_ref[...] = hc_ref[...]

    ys = []
    for c in range(4):
        blocks = [s_ref[pl.ds((4 * c + jj) * sr, rows), :].astype(BF16) for jj in range(4)]
        blocks += [s_ref[pl.ds((16 + 4 * c + jj) * sr, rows), :].astype(BF16) for jj in range(4)]
        hcat = jnp.concatenate(blocks, axis=1)
        ys.append(jnp.dot(hcat, wc_ref[c], preferred_element_type=F32))
    y = jnp.concatenate(ys, axis=1) + dsk_ref[...] * u
    y = _gelu_tanh(y)
    gate = jax.nn.sigmoid(jnp.dot(y.astype(BF16), wglu_ref[...], preferred_element_type=F32) + bglu_ref[...])
    y = y * gate
    y_ref[...] = _rms(y, gout_ref[...]).astype(BF16).reshape(y_ref.shape)


def _s5(u3, h0, wb, wc, a_re, a_im, dsk, wglu, bglu, gout, nseq, tm):
    nb, length, _ = u3.shape
    rows = nseq * tm
    sr = rows + SUBLANES
    grid = (nb // nseq, length // tm)
    const2 = lambda i, j: (0, 0)
    const3 = lambda i, j: (0, 0, 0)
    if tm == length:
        y_spec = pl.BlockSpec((rows, D_SSM), lambda i, j: (i, 0))
        y_shape = jax.ShapeDtypeStruct((nb * length, D_SSM), BF16)
    else:
        y_spec = pl.BlockSpec((nseq, tm, D_SSM), lambda i, j: (i, j, 0))
        y_shape = jax.ShapeDtypeStruct((nb, length, D_SSM), BF16)
    return pl.pallas_call(
        functools.partial(_s5_kernel, nseq=nseq, tm=tm),
        grid=grid,
        in_specs=[pl.BlockSpec((nseq, tm, D_SSM), lambda i, j: (i, j, 0)),
                  pl.BlockSpec((nseq, 32, LANES), lambda i, j: (i, 0, 0)),
                  pl.BlockSpec((4, LANES, 1024), const3),
                  pl.BlockSpec((4, 1024, LANES), const3),
                  pl.BlockSpec((16, LANES), const2),
                  pl.BlockSpec((16, LANES), const2),
                  pl.BlockSpec((1, D_SSM), const2),
                  pl.BlockSpec((D_SSM, D_SSM), const2),
                  pl.BlockSpec((1, D_SSM), const2),
                  pl.BlockSpec((1, D_SSM), const2)],
        out_specs=[y_spec, pl.BlockSpec((nseq, 32, LANES), lambda i, j: (i, 0, 0))],
        out_shape=[y_shape, jax.ShapeDtypeStruct((nb, 32, LANES), F32)],
        scratch_shapes=[pltpu.VMEM((32 * sr, LANES), F32), pltpu.VMEM((nseq, 32, LANES), F32)],
        compiler_params=_cparams(("arbitrary", "arbitrary")),
    )(u3, h0, wb, wc, a_re, a_im, dsk, wglu, bglu, gout)


def _s5_params(lam_re, lam_im, log_step, b_re, b_im, c_re, c_im):
    g = lam_re.shape[0]
    dt = jnp.exp(log_step)[:, None]
    mag = jnp.exp(lam_re * dt)
    a_re = mag * jnp.cos(lam_im * dt)
    a_im = mag * jnp.sin(lam_im * dt)
    den = lam_re * lam_re + lam_im * lam_im
    n_re = a_re - 1.0
    f_re = (n_re * lam_re + a_im * lam_im) / den
    f_im = (a_im * lam_re - n_re * lam_im) / den
    bb_re = f_re[..., None] * b_re - f_im[..., None] * b_im
    bb_im = f_re[..., None] * b_im + f_im[..., None] * b_re
    eye = jnp.eye(g, dtype=F32)
    n_state = g * SSM_STATE

    def in_mat(bb):
        return jnp.einsum('gpi,gh->gihp', bb, eye).reshape(g * SSM_GROUP, n_state)

    def out_mat(cc):
        return jnp.einsum('gip,gh->gphi', cc, eye).reshape(n_state, g * SSM_GROUP)

    wbr, wbi = in_mat(bb_re), in_mat(bb_im)
    wcr, wci = out_mat(c_re), out_mat(-c_im)
    wb = jnp.stack([jnp.concatenate([wbr[c * 128:(c + 1) * 128, c * 512:(c + 1) * 512],
                                     wbi[c * 128:(c + 1) * 128, c * 512:(c + 1) * 512]], axis=1)
                    for c in range(4)]).astype(BF16)
    wc = jnp.stack([jnp.concatenate([wcr[c * 512:(c + 1) * 512, c * 128:(c + 1) * 128],
                                     wci[c * 512:(c + 1) * 512, c * 128:(c + 1) * 128]], axis=0)
                    for c in range(4)]).astype(BF16)
    return wb, wc, a_re.reshape(16, LANES), a_im.reshape(16, LANES)


def _attn_kernel(q_ref, k_ref, v_ref, cn_ref, g_ref, o_ref, acc_ref, *, tq):
    qi = pl.program_id(1)
    row = lax.broadcasted_iota(I32, (tq, tq), 0)
    col = lax.broadcasted_iota(I32, (tq, tq), 1)
    causal = col <= row

    for h in range(N_HEADS):
        lanes = slice(h * HEAD_DIM, (h + 1) * HEAD_DIM)
        qh = q_ref[0, :, lanes]

        def block(kb, carry, masked):
            m, l, acc = carry
            ks = pl.multiple_of(kb * tq, tq)
            kh = k_ref[0, pl.ds(ks, tq), lanes]
            vh = v_ref[0, pl.ds(ks, tq), lanes]
            s = lax.dot_general(qh, kh, (((1,), (1,)), ((), ())), preferred_element_type=F32)
            s = s + cn_ref[kb, h:h + 1, :]
            if masked:
                s = jnp.where(causal, s, NEG_INF)
            m_new = jnp.maximum(m, jnp.max(s, axis=1, keepdims=True))
            alpha = jnp.exp(m - m_new)
            p = jnp.exp(s - m_new)
            l = alpha * l + jnp.sum(p, axis=1, keepdims=True)
            acc = alpha * acc + jnp.dot(p.astype(BF16), vh, preferred_element_type=F32)
            return m_new, l, acc

        init = (jnp.full((tq, 1), NEG_INF, F32), jnp.zeros((tq, 1), F32), jnp.zeros((tq, HEAD_DIM), F32))
        carry = lax.fori_loop(0, qi, lambda kb, c: block(kb, c, False), init)
        m, l, acc = block(qi, carry, True)
        acc_ref[:, lanes] = acc / l

    o_ref[0] = _rms(acc_ref[...], g_ref[...]).astype(BF16)


def _attn_prompt(q, k, v, cn3, g, tq):
    b, length, d = q.shape
    nq = length // tq
    return pl.pallas_call(
        functools.partial(_attn_kernel, tq=tq),
        grid=(b, nq),
        in_specs=[pl.BlockSpec((1, tq, d), lambda i, j: (i, j, 0)),
                  pl.BlockSpec((1, length, d), lambda i, j: (i, 0, 0)),
                  pl.BlockSpec((1, length, d), lambda i, j: (i, 0, 0)),
                  pl.BlockSpec((nq, N_HEADS, tq), lambda i, j: (i, 0, 0)),
                  pl.BlockSpec((1, d), lambda i, j: (0, 0))],
        out_specs=pl.BlockSpec((1, tq, d), lambda i, j: (i, j, 0)),
        out_shape=jax.ShapeDtypeStruct((b, length, d), BF16),
        scratch_shapes=[pltpu.VMEM((tq, d), F32)],
        compiler_params=_cparams(("arbitrary", "arbitrary")),
    )(q, k, v, cn3, g)


def _attn_paged_kernel(pt_ref, q_ref, kn_ref, vn_ref, cn_ref, g_ref, kc_hbm, vc_hbm, o_ref,
                       kbuf, vbuf, sem, *, n_pages, n_new):
    i = pl.program_id(0)
    n = pl.num_programs(0)
    n_past = n_pages * PAGE

    def page_copies(seq, slot):
        cps = []
        for p in range(n_pages):
            pg = pt_ref[seq * n_pages + p]
            for h in range(N_HEADS):
                rows = pl.ds(p * PAGE, PAGE)
                cps.append(pltpu.make_async_copy(kc_hbm.at[pg, :, h], kbuf.at[slot, h, rows], sem.at[0, slot]))
                cps.append(pltpu.make_async_copy(vc_hbm.at[pg, :, h], vbuf.at[slot, h, rows], sem.at[1, slot]))
        return cps

    @pl.when(i == 0)
    def _():
        for cp in page_copies(0, 0):
            cp.start()

    @pl.when(i + 1 < n)
    def _():
        for cp in page_copies(i + 1, (i + 1) % 2):
            cp.start()

    slot = i % 2
    for cp in page_copies(i, slot):
        cp.wait()

    q8 = q_ref[0]
    kn = kn_ref[0]
    vn = vn_ref[0]
    qpos = lax.broadcasted_iota(I32, (n_new, n_new), 0)
    kpos = lax.broadcasted_iota(I32, (n_new, n_new), 1)
    outs = []
    for h in range(N_HEADS):
        lanes = slice(h * HEAD_DIM, (h + 1) * HEAD_DIM)
        qh = q8[:, lanes]
        bias = cn_ref[0, h:h + 1, :]
        s_p = _bdot_nt(qh, kbuf[slot, h]) + bias[:, :n_past]
        s_n = _bdot_nt(qh, kn[:, lanes]) + bias[:, n_past:n_past + n_new]
        s_n = jnp.where(kpos <= qpos, s_n, NEG_INF)
        m = jnp.maximum(jnp.max(s_p, axis=1, keepdims=True), jnp.max(s_n, axis=1, keepdims=True))
        p_p = jnp.exp(s_p - m)
        p_n = jnp.exp(s_n - m)
        l = jnp.sum(p_p, axis=1, keepdims=True) + jnp.sum(p_n, axis=1, keepdims=True)
        oh = _bdot(p_p, vbuf[slot, h]) + _bdot(p_n, vn[:, lanes])
        outs.append(oh / l)
    o = jnp.concatenate(outs, axis=1)
    o_ref[0] = _rms(o, g_ref[...])


def _attn_paged(page_table, q, kn, vn, cn, g, cache_k, cache_v):
    nseq, n_new, d = q.shape
    assert n_new & (n_new - 1) == 0, "new-token count must be a power of two"
    n_pages = page_table.shape[1]
    n_past = n_pages * PAGE
    grid_spec = pltpu.PrefetchScalarGridSpec(
        num_scalar_prefetch=1,
        grid=(nseq,),
        in_specs=[pl.BlockSpec((1, n_new, d), lambda i, pt: (i, 0, 0)),
                  pl.BlockSpec((1, n_new, d), lambda i, pt: (i, 0, 0)),
                  pl.BlockSpec((1, n_new, d), lambda i, pt: (i, 0, 0)),
                  pl.BlockSpec((1, N_HEADS, cn.shape[-1]), lambda i, pt: (i, 0, 0)),
                  pl.BlockSpec((1, d), lambda i, pt: (0, 0)),
                  pl.BlockSpec(memory_space=pl.ANY),
                  pl.BlockSpec(memory_space=pl.ANY)],
        out_specs=pl.BlockSpec((1, n_new, d), lambda i, pt: (i, 0, 0)),
        scratch_shapes=[pltpu.VMEM((2, N_HEADS, n_past, HEAD_DIM), F32),
                        pltpu.VMEM((2, N_HEADS, n_past, HEAD_DIM), F32),
                        pltpu.SemaphoreType.DMA((2, 2))],
    )
    return pl.pallas_call(
        functools.partial(_attn_paged_kernel, n_pages=n_pages, n_new=n_new),
        grid_spec=grid_spec,
        out_shape=jax.ShapeDtypeStruct((nseq, n_new, d), F32),
        compiler_params=_cparams(("arbitrary",)),
    )(page_table.reshape(-1), q, kn, vn, cn, g, cache_k, cache_v)


def _outproj_kernel(x_ref, ys_ref, oa_ref, gt_ref, sc_ref, sh_ref, g_ref, wo_ref, wrt_ref,
                    x1_ref, h2_ref, lg_ref):
    nbb, rb, d = x_ref.shape
    mix = jnp.dot(ys_ref[...], wo_ref[0:D_SSM, :], preferred_element_type=F32)
    mix = mix + jnp.dot(oa_ref[...], wo_ref[D_SSM:, :], preferred_element_type=F32)
    x1 = x_ref[...] + gt_ref[...] * mix.reshape(nbb, rb, d)
    x1_ref[...] = x1
    h2 = (_rms(x1, g_ref[...]) * (1.0 + sc_ref[...]) + sh_ref[...]).reshape(nbb * rb, d)
    hb = h2.astype(BF16)
    h2_ref[...] = h2
    hlo = (h2 - hb.astype(F32)).astype(BF16)
    whi = wrt_ref[0]
    wlo = wrt_ref[1]
    nt = (((1,), (1,)), ((), ()))
    lg = lax.dot_general(whi, hb, nt, preferred_element_type=F32)
    lg = lg + lax.dot_general(wlo, hb, nt, preferred_element_type=F32)
    lg = lg + lax.dot_general(whi, hlo, nt, preferred_element_type=F32)
    lg_ref[...] = lg


def _outproj(x3, ys, oa, gt, sc, sh, g, wo, wrt, nbb, rb):
    nb, r, d = x3.shape
    t = nb * r
    rows = nbb * rb
    nj = r // rb
    row_map = lambda i, j: (i * nj + j, 0)
    mod_spec = pl.BlockSpec((nbb, 1, d), lambda i, j: (i, 0, 0))
    return pl.pallas_call(
        _outproj_kernel,
        grid=(nb // nbb, nj),
        in_specs=[pl.BlockSpec((nbb, rb, d), lambda i, j: (i, j, 0)),
                  pl.BlockSpec((rows, D_SSM), row_map),
                  pl.BlockSpec((rows, D_ATTN), row_map),
                  mod_spec, mod_spec, mod_spec,
                  pl.BlockSpec((1, 1, d), lambda i, j: (0, 0, 0)),
                  pl.BlockSpec((D_SSM + D_ATTN, d), lambda i, j: (0, 0)),
                  pl.BlockSpec((2, N_EXPERTS, d), lambda i, j: (0, 0, 0))],
        out_specs=[pl.BlockSpec((nbb, rb, d), lambda i, j: (i, j, 0)),
                   pl.BlockSpec((rows, d), row_map),
                   pl.BlockSpec((N_EXPERTS, rows), lambda i, j: (0, i * nj + j))],
        out_shape=[jax.ShapeDtypeStruct((nb, r, d), F32),
                   jax.ShapeDtypeStruct((t, d), F32),
                   jax.ShapeDtypeStruct((N_EXPERTS, t), F32)],
        compiler_params=_cparams(("arbitrary", "arbitrary")),
    )(x3, ys, oa, gt, sc, sh, g, wo, wrt)


def _router_kernel(lg_ref, rb_ref, ut_ref, idx_ref, w_ref, rank_ref, cnt_ref, carry_ref):
    i = pl.program_id(0)
    tm = lg_ref.shape[1]
    per_group = N_EXPERTS // N_EXPERT_GROUPS

    @pl.when(i == 0)
    def _():
        carry_ref[...] = jnp.zeros_like(carry_ref)

    scores = jax.nn.sigmoid(lg_ref[...])
    biased = scores + rb_ref[...]
    blks, grp = [], []
    for g in range(N_EXPERT_GROUPS):
        blk = biased[g * per_group:(g + 1) * per_group, :]
        m1 = jnp.max(blk, axis=0, keepdims=True)
        eq = blk == m1
        n_eq = jnp.sum(jnp.where(eq, 1.0, 0.0), axis=0, keepdims=True)
        m2 = jnp.max(jnp.where(eq, NEG_INF, blk), axis=0, keepdims=True)
        blks.append(blk)
        grp.append(m1 + jnp.where(n_eq >= 2.0, m1, m2))
    masked = []
    for g in range(N_EXPERT_GROUPS):
        beaten = jnp.zeros((1, tm), F32)
        for o in range(N_EXPERT_GROUPS):
            if o == g:
                continue
            ahead = (grp[o] >= grp[g]) if o < g else (grp[o] > grp[g])
            beaten = beaten + jnp.where(ahead, 1.0, 0.0)
        masked.append(jnp.where(beaten < float(TOPK_GROUPS), blks[g], NEG_INF))
    work = jnp.concatenate(masked, axis=0)

    eid = lax.broadcasted_iota(I32, (N_EXPERTS, tm), 0)
    chosen = jnp.zeros((N_EXPERTS, tm), F32)
    idxs, ws, sels = [], [], []
    for _ in range(TOP_K):
        m = jnp.max(work, axis=0, keepdims=True)
        first = jnp.min(jnp.where(work == m, eid, N_EXPERTS), axis=0, keepdims=True)
        sel = eid == first
        idxs.append(first)
        ws.append(jnp.sum(jnp.where(sel, scores, 0.0), axis=0, keepdims=True))
        sels.append(sel)
        chosen = jnp.where(sel, 1.0, chosen)
        work = jnp.where(sel, NEG_INF, work)
    wsum = ws[0]
    for wk in ws[1:]:
        wsum = wsum + wk

    prefix = jnp.dot(chosen.astype(BF16), ut_ref[...], preferred_element_type=F32) + carry_ref[...]
    carry_ref[...] = carry_ref[...] + jnp.sum(chosen, axis=1, keepdims=True)
    cnt_ref[...] = carry_ref[...]

    idx_ref[...] = jnp.zeros_like(idx_ref)
    w_ref[...] = jnp.zeros_like(w_ref)
    rank_ref[...] = jnp.zeros_like(rank_ref)
    for k in range(TOP_K):
        idx_ref[k:k + 1, :] = idxs[k]
        w_ref[k:k + 1, :] = ws[k] / wsum * ROUTED_SCALE
        rank = jnp.sum(jnp.where(sels[k], prefix, 0.0), axis=0, keepdims=True)
        rank_ref[k:k + 1, :] = rank.astype(I32)


def _router(lg, router_bias, tm):
    e, t = lg.shape
    ut = np.triu(np.ones((tm, tm), np.float32), 1)
    tok_spec = pl.BlockSpec((SUBLANES, tm), lambda i: (0, i))
    return pl.pallas_call(
        _router_kernel,
        grid=(t // tm,),
        in_specs=[pl.BlockSpec((e, tm), lambda i: (0, i)),
                  pl.BlockSpec((e, 1), lambda i: (0, 0)),
                  pl.BlockSpec((tm, tm), lambda i: (0, 0))],
        out_specs=[tok_spec, tok_spec, tok_spec, pl.BlockSpec((e, 1), lambda i: (0, 0))],
        out_shape=[jax.ShapeDtypeStruct((SUBLANES, t), I32), jax.ShapeDtypeStruct((SUBLANES, t), F32),
                   jax.ShapeDtypeStruct((SUBLANES, t), I32), jax.ShapeDtypeStruct((e, 1), F32)],
        scratch_shapes=[pltpu.VMEM((e, 1), F32)],
        compiler_params=_cparams(("arbitrary",)),
    )(lg, router_bias.reshape(e, 1), jnp.asarray(ut, BF16))


def _experts_kernel(be_ref, nv_ref, x_ref, wg_ref, wu_ref, wd_ref, o_ref):
    i = pl.program_id(0)

    @pl.when(i < nv_ref[0])
    def _():
        x = x_ref[...].astype(BF16)
        g = jnp.dot(x, wg_ref[0].astype(BF16), preferred_element_type=F32)
        u = jnp.dot(x, wu_ref[0].astype(BF16), preferred_element_type=F32)
        a = (g * jax.nn.sigmoid(g) * u).astype(BF16)
        o_ref[...] = jnp.dot(a, wd_ref[0].astype(BF16), preferred_element_type=F32)

    @pl.when(i >= nv_ref[0])
    def _():
        o_ref[...] = jnp.zeros_like(o_ref)


def _experts(block_e, n_valid, xs, w_eg, w_eu, w_ed, m):
    n_rows, d = xs.shape
    de = w_eg.shape[-1]
    nblk = n_rows // m
    xmap = lambda i, be, nv: (jnp.minimum(i, nv[0] - 1), 0)
    wmap = lambda i, be, nv: (be[i], 0, 0)
    grid_spec = pltpu.PrefetchScalarGridSpec(
        num_scalar_prefetch=2,
        grid=(nblk,),
        in_specs=[pl.BlockSpec((m, d), xmap),
                  pl.BlockSpec((1, d, de), wmap),
                  pl.BlockSpec((1, d, de), wmap),
                  pl.BlockSpec((1, de, d), wmap)],
        out_specs=pl.BlockSpec((m, d), lambda i, be, nv: (i, 0)),
    )
    return pl.pallas_call(
        _experts_kernel,
        grid_spec=grid_spec,
        out_shape=jax.ShapeDtypeStruct((n_rows, d), F32),
        compiler_params=_cparams(("arbitrary",)),
    )(block_e, n_valid, xs, w_eg, w_eu, w_ed)


def _row_copy(src, src_row, dst, dst_row, sem):
    return pltpu.make_async_copy(src.at[pl.ds(src_row, 1)], dst.at[pl.ds(dst_row, 1)], sem)


def _final_kernel(dest_ref, x1_ref, h2_ref, w_ref, gt_ref, sc_ref, sh_ref, g_ref, wgu_ref, wd_ref, yb_hbm,
                  y_ref, gbuf, sem):
    nbb, rb, d = x1_ref.shape
    rows = nbb * rb
    ds = wd_ref.shape[0]

    def issue(r, c):
        for k in range(TOP_K):
            _row_copy(yb_hbm, dest_ref[0, 0, r * TOP_K + k], gbuf.at[k], r, sem).start()
        return c

    lax.fori_loop(0, rows, issue, 0, unroll=2)

    gu = jnp.dot(h2_ref[...].astype(BF16), wgu_ref[...], preferred_element_type=F32)
    g = gu[:, :ds]
    a = (g * jax.nn.sigmoid(g) * gu[:, ds:]).astype(BF16)
    ff = jnp.dot(a, wd_ref[...], preferred_element_type=F32)

    for k in range(TOP_K):
        pltpu.make_async_copy(yb_hbm.at[pl.ds(0, rows)], gbuf.at[k], sem).wait()
    routed = gbuf[0] * w_ref[:, 0:1]
    for k in range(1, TOP_K):
        routed = routed + gbuf[k] * w_ref[:, k:k + 1]
    ff = routed + ff
    x2 = x1_ref[...] + gt_ref[...] * ff.reshape(nbb, rb, d)
    y_ref[...] = _rms(x2, g_ref[...]) * (1.0 + sc_ref[...]) + sh_ref[...]


def _final(dest3, x1, h2, wts, yb, gt, sc, sh, g, wgu, wd, nbb, rb):
    nb, r, d = x1.shape
    rows = nbb * rb
    nj = r // rb
    row_map = lambda i, j: (i * nj + j, 0)
    mod_spec = pl.BlockSpec((nbb, 1, d), lambda i, j: (i, 0, 0))
    return pl.pallas_call(
        _final_kernel,
        grid=(nb // nbb, nj),
        in_specs=[pl.BlockSpec((1, 1, rows * TOP_K), lambda i, j: (i * nj + j, 0, 0), memory_space=pltpu.SMEM),
                  pl.BlockSpec((nbb, rb, d), lambda i, j: (i, j, 0)),
                  pl.BlockSpec((rows, d), row_map),
                  pl.BlockSpec((rows, SUBLANES), row_map),
                  mod_spec, mod_spec, mod_spec,
                  pl.BlockSpec((1, 1, d), lambda i, j: (0, 0, 0)),
                  pl.BlockSpec(wgu.shape, lambda i, j: (0, 0)),
                  pl.BlockSpec(wd.shape, lambda i, j: (0, 0)),
                  pl.BlockSpec(memory_space=pl.ANY)],
        out_specs=pl.BlockSpec((nbb, rb, d), lambda i, j: (i, j, 0)),
        out_shape=jax.ShapeDtypeStruct((nb, r, d), F32),
        scratch_shapes=[pltpu.VMEM((TOP_K, rows, d), F32), pltpu.SemaphoreType.DMA(())],
        compiler_params=_cparams(("arbitrary", "arbitrary")),
    )(dest3, x1, h2, wts, gt, sc, sh, g, wgu, wd, yb)


def _dispatch_kernel(dest_ref, x_ref, xs_in, xs_hbm, sem):
    del xs_in
    rows = x_ref.shape[0]

    def issue(r, c):
        for k in range(TOP_K):
            _row_copy(x_ref, r, xs_hbm, dest_ref[0, 0, r * TOP_K + k], sem).start()
        return c

    lax.fori_loop(0, rows, issue, 0, unroll=2)
    for k in range(TOP_K):
        pltpu.make_async_copy(x_ref, xs_hbm.at[pl.ds(0, rows)], sem).wait()


def _dispatch(dest3, h2, n_rows):
    t, d = h2.shape
    rows = dest3.shape[-1] // TOP_K
    return pl.pallas_call(
        _dispatch_kernel,
        grid=(t // rows,),
        in_specs=[pl.BlockSpec((1, 1, rows * TOP_K), lambda i: (i, 0, 0), memory_space=pltpu.SMEM),
                  pl.BlockSpec((rows, d), lambda i: (i, 0)),
                  pl.BlockSpec(memory_space=pl.ANY)],
        out_specs=pl.BlockSpec(memory_space=pl.ANY),
        out_shape=jax.ShapeDtypeStruct((n_rows, d), F32),
        scratch_shapes=[pltpu.SemaphoreType.DMA(())],
        input_output_aliases={2: 0},
        compiler_params=_cparams(("arbitrary",)),
    )(dest3, h2, jnp.zeros((n_rows, d), F32))


def _moe(h2, lg, router_bias, w_eg, w_eu, w_ed, tm_router, tile):
    t, d = h2.shape
    e = N_EXPERTS
    m = MOE_ROWS
    idx_t, w_t, rank_t, counts = _router(lg, router_bias, tm_router)
    counts = counts.reshape(e).astype(I32)
    padded = ((counts + m - 1) // m) * m
    pad_end = jnp.cumsum(padded)
    pad_start = pad_end - padded
    onehot = idx_t[:, :, None] == jnp.arange(e, dtype=I32)
    dest_t = jnp.sum(jnp.where(onehot, pad_start, 0), axis=-1) + rank_t
    dest3 = dest_t[:TOP_K].T.reshape(t // tile, 1, tile * TOP_K)
    n_rows = (-(-(t * TOP_K) // m)) * m + e * m
    nblk = n_rows // m
    block_start = jnp.arange(nblk, dtype=I32) * m
    block_e = jnp.minimum(jnp.sum(pad_end[None, :] <= block_start[:, None], axis=1), e - 1).astype(I32)
    n_valid = (pad_end[-1] // m).astype(I32).reshape(1)
    xs = _dispatch(dest3, h2, n_rows)
    yb = _experts(block_e, n_valid, xs, w_eg, w_eu, w_ed, m)
    return dest3, w_t.T, yb


def kernel(x_prompt, x_sample, c_prompt, c_sample, cache_k, cache_v, cache_logf, state_ssm_re, state_ssm_im, page_table, w_ada, b_ada, g_norm1, w_in, b_fgate, ssm_lambda_re, ssm_lambda_im, ssm_log_step, ssm_b_re, ssm_b_im, ssm_c_re, ssm_c_im, ssm_d, w_glu, b_glu, g_ssm_out, g_attn_out, w_out, g_norm2, w_router, router_bias, w_exp_gate, w_exp_up, w_exp_down, w_sh_gate, w_sh_up, w_sh_down, g_final, w_ada_final, b_ada_final):
    depth = w_ada.shape[0]
    assert depth == 1, "one layer is supported"
    bp, lp, d = x_prompt.shape
    bs, ls, _ = x_sample.shape
    n_pages = page_table.shape[1]
    n_past = n_pages * PAGE
    n_groups = ssm_lambda_re.shape[1]

    n_c = bp + bs
    n_c_pad = -(-n_c // SUBLANES) * SUBLANES
    c_all = jnp.concatenate([c_prompt, c_sample, jnp.zeros((n_c_pad - n_c, d), F32)], axis=0)
    mod = _adaln(c_all, w_ada[0], b_ada[0])
    modf = _adaln(c_all, w_ada_final, b_ada_final)

    def mods(lo, hi):
        parts = [mod[lo:hi, k * d:(k + 1) * d][:, None, :] for k in range(6)]
        parts += [modf[lo:hi, k * d:(k + 1) * d][:, None, :] for k in range(2)]
        return parts

    w_main = w_in[0][:, :D_SSM + 3 * D_ATTN].astype(BF16)
    w_ft = w_in[0][:, D_SSM + 3 * D_ATTN:].T.astype(BF16)
    b_f = b_fgate[0].reshape(N_HEADS, 1)
    g1 = g_norm1[0].reshape(1, 1, d)
    g2 = g_norm2[0].reshape(1, 1, d)
    gf = g_final.reshape(1, 1, d)
    wb, wc, a_re, a_im = _s5_params(ssm_lambda_re[0], ssm_lambda_im[0], ssm_log_step[0], ssm_b_re[0],
                                    ssm_b_im[0], ssm_c_re[0], ssm_c_im[0])
    dsk = ssm_d[0].reshape(1, D_SSM)
    wglu = w_glu[0].astype(BF16)
    bglu = b_glu[0].reshape(1, D_SSM)
    g_so = g_ssm_out[0].reshape(1, D_SSM)
    g_ao = g_attn_out[0].reshape(1, D_ATTN)
    wo = w_out[0].astype(BF16)
    wr_t = w_router[0].T
    wr_hi = wr_t.astype(BF16)
    wrt = jnp.stack([wr_hi, (wr_t - wr_hi.astype(F32)).astype(BF16)])
    wgu = jnp.concatenate([w_sh_gate[0], w_sh_up[0]], axis=1).astype(BF16)
    wsd = w_sh_down[0].astype(BF16)

    def ssm_state(re, im):
        return jnp.concatenate([re.reshape(-1, 16, LANES), im.reshape(-1, 16, LANES)], axis=1)

    def split_state(ht):
        n = ht.shape[0]
        return (ht[:, :16].reshape(1, n, n_groups, SSM_STATE), ht[:, 16:].reshape(1, n, n_groups, SSM_STATE))

    tm = 512
    sh1, sc1, gt1, sh2, sc2, gt2, shf, scf = mods(0, bp)
    u, q, k, v, kb, vb, lft = _inproj(x_prompt, sc1, sh1, g1, w_main, w_ft, b_f, 1, tm)
    tq = 256
    cn = _neg_cumsum(lft.reshape(-1, LANES), lp // LANES, N_HEADS * bp * lp // LANES)
    cn3 = cn.reshape(N_HEADS, bp * lp // tq, tq).transpose(1, 0, 2)
    ys, ht = _s5(u.reshape(bp, lp, D_SSM), jnp.zeros((bp, 32, LANES), F32), wb, wc, a_re, a_im, dsk,
                 wglu, bglu, g_so, bp, 256)
    oa = _attn_prompt(q.reshape(bp, lp, D_ATTN), kb.reshape(bp, lp, D_ATTN), vb.reshape(bp, lp, D_ATTN),
                      cn3, g_ao, tq)
    x1, h2, lg = _outproj(x_prompt, ys.reshape(bp * lp, D_SSM), oa.reshape(bp * lp, D_ATTN),
                          gt1, sc2, sh2, g2, wo, wrt, 1, tm)
    dest3, wts, yb = _moe(h2, lg, router_bias[0], w_exp_gate[0], w_exp_up[0], w_exp_down[0], 512, tm)
    y_prompt = _final(dest3, x1, h2, wts, yb, gt2, scf, shf, gf, wgu, wsd, 1, tm)
    k_prompt = k.reshape(1, bp, lp, N_HEADS, HEAD_DIM)
    v_prompt = v.reshape(1, bp, lp, N_HEADS, HEAD_DIM)
    logf_prompt = lft.T.reshape(1, bp, lp, N_HEADS)
    sre_p, sim_p = split_state(ht)

    nbb = 64
    sh1, sc1, gt1, sh2, sc2, gt2, shf, scf = mods(bp, bp + bs)
    u, q, k, v, kb, vb, lft = _inproj(x_sample, sc1, sh1, g1, w_main, w_ft, b_f, nbb, ls)
    ys, ht = _s5(u.reshape(bs, ls, D_SSM), ssm_state(state_ssm_re[0], state_ssm_im[0]), wb, wc, a_re, a_im,
                 dsk, wglu, bglu, g_so, 32, ls)
    lf_past = cache_logf[0][page_table].reshape(bs, n_past, N_HEADS).transpose(0, 2, 1)
    lf_new = lft.reshape(N_HEADS, bs, ls).transpose(1, 0, 2)
    n_key_pad = -(-(n_past + ls) // LANES) * LANES
    lf_all = jnp.concatenate([lf_past, lf_new, jnp.zeros((bs, N_HEADS, n_key_pad - n_past - ls), F32)], axis=2)
    gsz = n_key_pad // LANES
    cn_s = _neg_cumsum(lf_all.reshape(-1, LANES), gsz, gsz * 64).reshape(bs, N_HEADS, n_key_pad)
    oa = _attn_paged(page_table, q.astype(F32).reshape(bs, ls, D_ATTN), k.reshape(bs, ls, D_ATTN),
                     v.reshape(bs, ls, D_ATTN), cn_s, g_ao,
                     cache_k[0], cache_v[0])
    x1, h2, lg = _outproj(x_sample, ys, oa.reshape(bs * ls, D_ATTN).astype(BF16),
                          gt1, sc2, sh2, g2, wo, wrt, nbb, ls)
    dest3, wts, yb = _moe(h2, lg, router_bias[0], w_exp_gate[0], w_exp_up[0], w_exp_down[0], 512, nbb * ls)
    y_sample = _final(dest3, x1, h2, wts, yb, gt2, scf, shf, gf, wgu, wsd, nbb, ls)
    k_sample = k.reshape(1, bs, ls, N_HEADS, HEAD_DIM)
    v_sample = v.reshape(1, bs, ls, N_HEADS, HEAD_DIM)
    logf_sample = lft.T.reshape(1, bs, ls, N_HEADS)
    sre_s, sim_s = split_state(ht)

    return (y_prompt, y_sample, k_prompt, v_prompt, logf_prompt, sre_p, sim_p,
            k_sample, v_sample, logf_sample, sre_s, sim_s)
```

```python
import functools
import math

import jax
import jax.numpy as jnp
import numpy as np
from jax import lax
from jax.experimental import pallas as pl
from jax.experimental.pallas import tpu as pltpu

F32 = jnp.float32
BF16 = jnp.bfloat16
I32 = jnp.int32

EPS = 1e-6
HEAD_DIM = 64
N_HEADS = 8
D_SSM = 512
D_ATTN = 512
SSM_GROUP = 16
SSM_STATE = 64
N_EXPERTS = 64
TOP_K = 6
N_EXPERT_GROUPS = 8
TOPK_GROUPS = 4
ROUTED_SCALE = 2.5
PAGE = 128

LANES = 128
SUBLANES = 8
VMEM_LIMIT = 48 * 1024 * 1024
MOE_ROWS = 256
NEG_INF = float("-inf")


def _cparams(sem):
    return pltpu.CompilerParams(dimension_semantics=sem, vmem_limit_bytes=VMEM_LIMIT)


def _bdot(a, b):
    return jnp.dot(a.astype(BF16), b.astype(BF16), preferred_element_type=F32)


def _bdot_nt(a, b):
    return lax.dot_general(a.astype(BF16), b.astype(BF16), (((1,), (1,)), ((), ())),
                           preferred_element_type=F32)


def _split3(v):
    hi = v.astype(BF16)
    r1 = v - hi.astype(F32)
    mid = r1.astype(BF16)
    lo = (r1 - mid.astype(F32)).astype(BF16)
    return hi, mid, lo


def _rms(x, g):
    return x * lax.rsqrt(jnp.mean(x * x, axis=-1, keepdims=True) + EPS) * g


def _adaln_kernel(c_ref, w_ref, b_ref, o_ref):
    c = c_ref[...]
    s = c * jax.nn.sigmoid(c)
    o_ref[...] = _bdot(s, w_ref[...]) + b_ref[...]


def _adaln(c, w, b):
    m, k = c.shape
    n = w.shape[1]
    tn = 1024
    return pl.pallas_call(
        _adaln_kernel,
        grid=(n // tn,),
        in_specs=[pl.BlockSpec((m, k), lambda j: (0, 0)),
                  pl.BlockSpec((k, tn), lambda j: (0, j)),
                  pl.BlockSpec((1, tn), lambda j: (0, j))],
        out_specs=pl.BlockSpec((m, tn), lambda j: (0, j)),
        out_shape=jax.ShapeDtypeStruct((m, n), F32),
        compiler_params=_cparams(("arbitrary",)),
    )(c, w, b.reshape(1, n))


def _inproj_kernel(x_ref, sc_ref, sh_ref, g_ref, w_ref, wft_ref, bf_ref,
                   u_ref, q_ref, k_ref, v_ref, kb_ref, vb_ref, lf_ref):
    nbb, rb, d = x_ref.shape
    x = x_ref[...]
    h = _rms(x, g_ref[...]) * (1.0 + sc_ref[...]) + sh_ref[...]
    hb = h.reshape(nbb * rb, d).astype(BF16)
    proj = jnp.dot(hb, w_ref[...], preferred_element_type=F32)
    u_ref[...] = proj[:, :D_SSM]
    q_ref[...] = (proj[:, D_SSM:D_SSM + D_ATTN] * (HEAD_DIM ** -0.5)).astype(BF16)
    k = proj[:, D_SSM + D_ATTN:D_SSM + 2 * D_ATTN]
    v = proj[:, D_SSM + 2 * D_ATTN:]
    k_ref[...] = k
    v_ref[...] = v
    kb_ref[...] = k.astype(BF16)
    vb_ref[...] = v.astype(BF16)
    z = _bdot_nt(wft_ref[...], hb) + bf_ref[...]
    lf_ref[...] = jnp.minimum(z, 0.0) - jnp.log1p(jnp.exp(-jnp.abs(z)))


def _inproj(x3, sc, sh, g, w_main, w_ft, b_f, nbb, rb):
    nb, r, d = x3.shape
    t = nb * r
    rows = nbb * rb
    nj = r // rb
    grid = (nb // nbb, nj)
    row_map = lambda i, j: (i * nj + j, 0)
    n_main = w_main.shape[1]
    outs = [jax.ShapeDtypeStruct((t, D_SSM), F32), jax.ShapeDtypeStruct((t, D_ATTN), BF16),
            jax.ShapeDtypeStruct((t, D_ATTN), F32), jax.ShapeDtypeStruct((t, D_ATTN), F32),
            jax.ShapeDtypeStruct((t, D_ATTN), BF16), jax.ShapeDtypeStruct((t, D_ATTN), BF16),
            jax.ShapeDtypeStruct((N_HEADS, t), F32)]
    out_specs = [pl.BlockSpec((rows, D_SSM), row_map)] + [pl.BlockSpec((rows, D_ATTN), row_map)] * 5 \
        + [pl.BlockSpec((N_HEADS, rows), lambda i, j: (0, i * nj + j))]
    return pl.pallas_call(
        _inproj_kernel,
        grid=grid,
        in_specs=[pl.BlockSpec((nbb, rb, d), lambda i, j: (i, j, 0)),
                  pl.BlockSpec((nbb, 1, d), lambda i, j: (i, 0, 0)),
                  pl.BlockSpec((nbb, 1, d), lambda i, j: (i, 0, 0)),
                  pl.BlockSpec((1, 1, d), lambda i, j: (0, 0, 0)),
                  pl.BlockSpec((d, n_main), lambda i, j: (0, 0)),
                  pl.BlockSpec((N_HEADS, d), lambda i, j: (0, 0)),
                  pl.BlockSpec((N_HEADS, 1), lambda i, j: (0, 0))],
        out_specs=out_specs,
        out_shape=outs,
        compiler_params=_cparams(("arbitrary", "arbitrary")),
    )(x3, sc, sh, g, w_main, w_ft, b_f)


def _cumsum_kernel(x_ref, tri_ref, lt_ref, o_ref):
    tri = tri_ref[...]
    lt = lt_ref[...]
    cs = sum(jnp.dot(p, tri, preferred_element_type=F32) for p in _split3(x_ref[...]))
    tot = jnp.broadcast_to(cs[:, LANES - 1:LANES], cs.shape)
    carry = sum(jnp.dot(lt, p, preferred_element_type=F32) for p in _split3(tot))
    o_ref[...] = -(cs + carry)


def _neg_cumsum(x2, gsz, rb):
    r = x2.shape[0]
    tri = np.triu(np.ones((LANES, LANES), np.float32))
    ii = np.arange(rb)
    lt = ((ii[:, None] // gsz == ii[None, :] // gsz) & (ii[None, :] < ii[:, None])).astype(np.float32)
    return pl.pallas_call(
        _cumsum_kernel,
        grid=(r // rb,),
        in_specs=[pl.BlockSpec((rb, LANES), lambda i: (i, 0)),
                  pl.BlockSpec((LANES, LANES), lambda i: (0, 0)),
                  pl.BlockSpec((rb, rb), lambda i: (0, 0))],
        out_specs=pl.BlockSpec((rb, LANES), lambda i: (i, 0)),
        out_shape=jax.ShapeDtypeStruct((r, LANES), F32),
        compiler_params=_cparams(("arbitrary",)),
    )(x2, jnp.asarray(tri, BF16), jnp.asarray(lt, BF16))


def _gelu_tanh(x):
    return 0.5 * x * (1.0 + jnp.tanh(math.sqrt(2.0 / math.pi) * (x + 0.044715 * (x * x * x))))


def _s5_kernel(u_ref, h0_ref, wb_ref, wc_ref, are_ref, aim_ref, dsk_ref, wglu_ref, bglu_ref, gout_ref,
               y_ref, ht_ref, s_ref, hc_ref, *, nseq, tm):
    rows = nseq * tm
    sr = rows + SUBLANES
    ti = pl.program_id(1)

    @pl.when(ti == 0)
    def _():
        hc_ref[...] = h0_ref[...]

    u = u_ref[...].reshape(rows, D_SSM)
    ub = u.astype(BF16)
    for c in range(4):
        bu = jnp.dot(ub[:, c * LANES:(c + 1) * LANES], wb_ref[c], preferred_element_type=F32)
        for jj in range(4):
            s_ref[pl.ds((4 * c + jj) * sr, rows), :] = bu[:, jj * LANES:(jj + 1) * LANES]
            s_ref[pl.ds((16 + 4 * c + jj) * sr, rows), :] = bu[:, 512 + jj * LANES:512 + (jj + 1) * LANES]

    ar = (are_ref[0:8, :], are_ref[8:16, :])
    ai = (aim_ref[0:8, :], aim_ref[8:16, :])

    def seq_group(sg, carry):
        base = sg * 4
        hs = []
        for b in range(4):
            hs.append(tuple(hc_ref[base + b, pl.ds(8 * q, 8), :] for q in range(4)))

        def step(t, hs):
            new = []
            for b in range(4):
                row = (base + b) * tm + t
                hr0, hr1, hi0, hi1 = hs[b]
                bre0 = s_ref[pl.ds(row, 8, stride=sr), :]
                bre1 = s_ref[pl.ds(8 * sr + row, 8, stride=sr), :]
                bim0 = s_ref[pl.ds(16 * sr + row, 8, stride=sr), :]
                bim1 = s_ref[pl.ds(24 * sr + row, 8, stride=sr), :]
                nr0 = ar[0] * hr0 - ai[0] * hi0 + bre0
                nr1 = ar[1] * hr1 - ai[1] * hi1 + bre1
                ni0 = ar[0] * hi0 + ai[0] * hr0 + bim0
                ni1 = ar[1] * hi1 + ai[1] * hr1 + bim1
                s_ref[pl.ds(row, 8, stride=sr), :] = nr0
                s_ref[pl.ds(8 * sr + row, 8, stride=sr), :] = nr1
                s_ref[pl.ds(16 * sr + row, 8, stride=sr), :] = ni0
                s_ref[pl.ds(24 * sr + row, 8, stride=sr), :] = ni1
                new.append((nr0, nr1, ni0, ni1))
            return tuple(new)

        hs = lax.fori_loop(0, tm, step, tuple(hs))
        for b in range(4):
            for q in range(4):
                hc_ref[base + b, pl.ds(8 * q, 8), :] = hs[b][q]
        return carry

    lax.fori_loop(0, nseq // 4, seq_group, 0)
    ht_ref[...] = hc_ref[...]

    ys = []
    for c in range(4):
        blocks = [s_ref[pl.ds((4 * c + jj) * sr, rows), :].astype(BF16) for jj in range(4)]
        blocks += [s_ref[pl.ds((16 + 4 * c + jj) * sr, rows), :].astype(BF16) for jj in range(4)]
        hcat = jnp.concatenate(blocks, axis=1)
        ys.append(jnp.dot(hcat, wc_ref[c], preferred_element_type=F32))
    y = jnp.concatenate(ys, axis=1) + dsk_ref[...] * u
    y = _gelu_tanh(y)
    gate = jax.nn.sigmoid(jnp.dot(y.astype(BF16), wglu_ref[...], preferred_element_type=F32) + bglu_ref[...])
    y = y * gate
    y_ref[...] = _rms(y, gout_ref[...]).astype(BF16).reshape(y_ref.shape)


def _s5(u3, h0, wb, wc, a_re, a_im, dsk, wglu, bglu, gout, nseq, tm):
    nb, length, _ = u3.shape
    rows = nseq * tm
    sr = rows + SUBLANES
    grid = (nb // nseq, length // tm)
    const2 = lambda i, j: (0, 0)
    const3 = lambda i, j: (0, 0, 0)
    if tm == length:
        y_spec = pl.BlockSpec((rows, D_SSM), lambda i, j: (i, 0))
        y_shape = jax.ShapeDtypeStruct((nb * length, D_SSM), BF16)
    else:
        y_spec = pl.BlockSpec((nseq, tm, D_SSM), lambda i, j: (i, j, 0))
        y_shape = jax.ShapeDtypeStruct((nb, length, D_SSM), BF16)
    return pl.pallas_call(
        functools.partial(_s5_kernel, nseq=nseq, tm=tm),
        grid=grid,
        in_specs=[pl.BlockSpec((nseq, tm, D_SSM), lambda i, j: (i, j, 0)),
                  pl.BlockSpec((nseq, 32, LANES), lambda i, j: (i, 0, 0)),
                  pl.BlockSpec((4, LANES, 1024), const3),
                  pl.BlockSpec((4, 1024, LANES), const3),
                  pl.BlockSpec((16, LANES), const2),
                  pl.BlockSpec((16, LANES), const2),
                  pl.BlockSpec((1, D_SSM), const2),
                  pl.BlockSpec((D_SSM, D_SSM), const2),
                  pl.BlockSpec((1, D_SSM), const2),
                  pl.BlockSpec((1, D_SSM), const2)],
        out_specs=[y_spec, pl.BlockSpec((nseq, 32, LANES), lambda i, j: (i, 0, 0))],
        out_shape=[y_shape, jax.ShapeDtypeStruct((nb, 32, LANES), F32)],
        scratch_shapes=[pltpu.VMEM((32 * sr, LANES), F32), pltpu.VMEM((nseq, 32, LANES), F32)],
        compiler_params=_cparams(("arbitrary", "arbitrary")),
    )(u3, h0, wb, wc, a_re, a_im, dsk, wglu, bglu, gout)


def _s5_params(lam_re, lam_im, log_step, b_re, b_im, c_re, c_im):
    g = lam_re.shape[0]
    dt = jnp.exp(log_step)[:, None]
    mag = jnp.exp(lam_re * dt)
    a_re = mag * jnp.cos(lam_im * dt)
    a_im = mag * jnp.sin(lam_im * dt)
    den = lam_re * lam_re + lam_im * lam_im
    n_re = a_re - 1.0
    f_re = (n_re * lam_re + a_im * lam_im) / den
    f_im = (a_im * lam_re - n_re * lam_im) / den
    bb_re = f_re[..., None] * b_re - f_im[..., None] * b_im
    bb_im = f_re[..., None] * b_im + f_im[..., None] * b_re
    eye = jnp.eye(g, dtype=F32)
    n_state = g * SSM_STATE

    def in_mat(bb):
        return jnp.einsum('gpi,gh->gihp', bb, eye).reshape(g * SSM_GROUP, n_state)

    def out_mat(cc):
        return jnp.einsum('gip,gh->gphi', cc, eye).reshape(n_state, g * SSM_GROUP)

    wbr, wbi = in_mat(bb_re), in_mat(bb_im)
    wcr, wci = out_mat(c_re), out_mat(-c_im)
    wb = jnp.stack([jnp.concatenate([wbr[c * 128:(c + 1) * 128, c * 512:(c + 1) * 512],
                                     wbi[c * 128:(c + 1) * 128, c * 512:(c + 1) * 512]], axis=1)
                    for c in range(4)]).astype(BF16)
    wc = jnp.stack([jnp.concatenate([wcr[c * 512:(c + 1) * 512, c * 128:(c + 1) * 128],
                                     wci[c * 512:(c + 1) * 512, c * 128:(c + 1) * 128]], axis=0)
                    for c in range(4)]).astype(BF16)
    return wb, wc, a_re.reshape(16, LANES), a_im.reshape(16, LANES)


def _attn_kernel(q_ref, k_ref, v_ref, cn_ref, g_ref, o_ref, acc_ref, *, tq):
    qi = pl.program_id(1)
    row = lax.broadcasted_iota(I32, (tq, tq), 0)
    col = lax.broadcasted_iota(I32, (tq, tq), 1)
    causal = col <= row

    for h in range(N_HEADS):
        lanes = slice(h * HEAD_DIM, (h + 1) * HEAD_DIM)
        qh = q_ref[0, :, lanes]

        def block(kb, carry, masked):
            m, l, acc = carry
            ks = pl.multiple_of(kb * tq, tq)
            kh = k_ref[0, pl.ds(ks, tq), lanes]
            vh = v_ref[0, pl.ds(ks, tq), lanes]
            s = lax.dot_general(qh, kh, (((1,), (1,)), ((), ())), preferred_element_type=F32)
            s = s + cn_ref[kb, h:h + 1, :]
            if masked:
                s = jnp.where(causal, s, NEG_INF)
            m_new = jnp.maximum(m, jnp.max(s, axis=1, keepdims=True))
            alpha = jnp.exp(m - m_new)
            p = jnp.exp(s - m_new)
            l = alpha * l + jnp.sum(p, axis=1, keepdims=True)
            acc = alpha * acc + jnp.dot(p.astype(BF16), vh, preferred_element_type=F32)
            return m_new, l, acc

        init = (jnp.full((tq, 1), NEG_INF, F32), jnp.zeros((tq, 1), F32), jnp.zeros((tq, HEAD_DIM), F32))
        carry = lax.fori_loop(0, qi, lambda kb, c: block(kb, c, False), init)
        m, l, acc = block(qi, carry, True)
        acc_ref[:, lanes] = acc / l

    o_ref[0] = _rms(acc_ref[...], g_ref[...]).astype(BF16)


def _attn_prompt(q, k, v, cn3, g, tq):
    b, length, d = q.shape
    nq = length // tq
    return pl.pallas_call(
        functools.partial(_attn_kernel, tq=tq),
        grid=(b, nq),
        in_specs=[pl.BlockSpec((1, tq, d), lambda i, j: (i, j, 0)),
                  pl.BlockSpec((1, length, d), lambda i, j: (i, 0, 0)),
                  pl.BlockSpec((1, length, d), lambda i, j: (i, 0, 0)),
                  pl.BlockSpec((nq, N_HEADS, tq), lambda i, j: (i, 0, 0)),
                  pl.BlockSpec((1, d), lambda i, j: (0, 0))],
        out_specs=pl.BlockSpec((1, tq, d), lambda i, j: (i, j, 0)),
        out_shape=jax.ShapeDtypeStruct((b, length, d), BF16),
        scratch_shapes=[pltpu.VMEM((tq, d), F32)],
        compiler_params=_cparams(("arbitrary", "arbitrary")),
    )(q, k, v, cn3, g)


def _attn_paged_kernel(pt_ref, q_ref, kn_ref, vn_ref, cn_ref, g_ref, kc_hbm, vc_hbm, o_ref,
                       kbuf, vbuf, sem, *, n_pages, n_new):
    i = pl.program_id(0)
    n = pl.num_programs(0)
    n_past = n_pages * PAGE

    def page_copies(seq, slot):
        cps = []
        for p in range(n_pages):
            pg = pt_ref[seq * n_pages + p]
            cps.append(pltpu.make_async_copy(kc_hbm.at[pg], kbuf.at[slot, p], sem.at[0, slot]))
            cps.append(pltpu.make_async_copy(vc_hbm.at[pg], vbuf.at[slot, p], sem.at[1, slot]))
        return cps

    @pl.when(i == 0)
    def _():
        for cp in page_copies(0, 0):
            cp.start()

    @pl.when(i + 1 < n)
    def _():
        for cp in page_copies(i + 1, (i + 1) % 2):
            cp.start()

    slot = i % 2
    for cp in page_copies(i, slot):
        cp.wait()

    d = q_ref.shape[-1]
    nr = N_HEADS * n_new
    new_bits = n_new.bit_length() - 1
    head_bits = HEAD_DIM.bit_length() - 1
    rowh = lax.shift_right_logical(lax.broadcasted_iota(I32, (nr, d), 0), new_bits)
    colh = lax.shift_right_logical(lax.broadcasted_iota(I32, (nr, d), 1), head_bits)
    bd = rowh == colh
    qrep = jnp.broadcast_to(q_ref[0][None], (N_HEADS, n_new, d)).reshape(nr, d)
    qbd = jnp.where(bd, qrep, 0.0).astype(BF16)
    cn = cn_ref[0]
    cnr = jnp.broadcast_to(cn[:, None, :], (N_HEADS, n_new, cn.shape[-1])).reshape(nr, cn.shape[-1])

    s_p = []
    for p in range(n_pages):
        kt = kbuf[slot, p].reshape(d, PAGE)
        s_p.append(_bdot(qbd, kt) + cnr[:, p * PAGE:(p + 1) * PAGE])
    s_n = _bdot_nt(qbd, kn_ref[0]) + cnr[:, n_past:n_past + n_new]
    qpos = lax.broadcasted_iota(I32, (nr, n_new), 0) & (n_new - 1)
    kpos = lax.broadcasted_iota(I32, (nr, n_new), 1)
    s_n = jnp.where(kpos <= qpos, s_n, NEG_INF)
    m = jnp.max(s_n, axis=1, keepdims=True)
    for sp in s_p:
        m = jnp.maximum(m, jnp.max(sp, axis=1, keepdims=True))
    p_n = jnp.exp(s_n - m)
    l = jnp.sum(p_n, axis=1, keepdims=True)
    of = _bdot(p_n, vn_ref[0])
    for p in range(n_pages):
        pp = jnp.exp(s_p[p] - m)
        l = l + jnp.sum(pp, axis=1, keepdims=True)
        of = of + _bdot_nt(pp, vbuf[slot, p].reshape(d, PAGE))
    of = jnp.where(bd, of / l, 0.0)
    o = jnp.sum(of.reshape(N_HEADS, n_new, d), axis=0)
    o_ref[0] = _rms(o, g_ref[...])


def _attn_paged(page_table, q, kn, vn, cn, g, cache_k, cache_v):
    nseq, n_new, d = q.shape
    assert n_new & (n_new - 1) == 0, "new-token count must be a power of two"
    n_pages = page_table.shape[1]
    n_past = n_pages * PAGE
    grid_spec = pltpu.PrefetchScalarGridSpec(
        num_scalar_prefetch=1,
        grid=(nseq,),
        in_specs=[pl.BlockSpec((1, n_new, d), lambda i, pt: (i, 0, 0)),
                  pl.BlockSpec((1, n_new, d), lambda i, pt: (i, 0, 0)),
                  pl.BlockSpec((1, n_new, d), lambda i, pt: (i, 0, 0)),
                  pl.BlockSpec((1, N_HEADS, cn.shape[-1]), lambda i, pt: (i, 0, 0)),
                  pl.BlockSpec((1, d), lambda i, pt: (0, 0)),
                  pl.BlockSpec(memory_space=pl.ANY),
                  pl.BlockSpec(memory_space=pl.ANY)],
        out_specs=pl.BlockSpec((1, n_new, d), lambda i, pt: (i, 0, 0)),
        scratch_shapes=[pltpu.VMEM((2, n_pages, N_HEADS, HEAD_DIM, PAGE), F32),
                        pltpu.VMEM((2, n_pages, N_HEADS, HEAD_DIM, PAGE), F32),
                        pltpu.SemaphoreType.DMA((2, 2))],
    )
    return pl.pallas_call(
        functools.partial(_attn_paged_kernel, n_pages=n_pages, n_new=n_new),
        grid_spec=grid_spec,
        out_shape=jax.ShapeDtypeStruct((nseq, n_new, d), F32),
        compiler_params=_cparams(("arbitrary",)),
    )(page_table.reshape(-1), q, kn, vn, cn, g, cache_k, cache_v)


def _outproj_kernel(x_ref, ys_ref, oa_ref, gt_ref, sc_ref, sh_ref, g_ref, wo_ref, wrt_ref,
                    x1_ref, h2_ref, lg_ref):
    nbb, rb, d = x_ref.shape
    mix = jnp.dot(ys_ref[...], wo_ref[0:D_SSM, :], preferred_element_type=F32)
    mix = mix + jnp.dot(oa_ref[...], wo_ref[D_SSM:, :], preferred_element_type=F32)
    x1 = x_ref[...] + gt_ref[...] * mix.reshape(nbb, rb, d)
    x1_ref[...] = x1
    h2 = (_rms(x1, g_ref[...]) * (1.0 + sc_ref[...]) + sh_ref[...]).reshape(nbb * rb, d)
    hb = h2.astype(BF16)
    h2_ref[...] = h2
    hlo = (h2 - hb.astype(F32)).astype(BF16)
    whi = wrt_ref[0]
    wlo = wrt_ref[1]
    nt = (((1,), (1,)), ((), ()))
    lg = lax.dot_general(whi, hb, nt, preferred_element_type=F32)
    lg = lg + lax.dot_general(wlo, hb, nt, preferred_element_type=F32)
    lg = lg + lax.dot_general(whi, hlo, nt, preferred_element_type=F32)
    lg_ref[...] = lg


def _outproj(x3, ys, oa, gt, sc, sh, g, wo, wrt, nbb, rb):
    nb, r, d = x3.shape
    t = nb * r
    rows = nbb * rb
    nj = r // rb
    row_map = lambda i, j: (i * nj + j, 0)
    mod_spec = pl.BlockSpec((nbb, 1, d), lambda i, j: (i, 0, 0))
    return pl.pallas_call(
        _outproj_kernel,
        grid=(nb // nbb, nj),
        in_specs=[pl.BlockSpec((nbb, rb, d), lambda i, j: (i, j, 0)),
                  pl.BlockSpec((rows, D_SSM), row_map),
                  pl.BlockSpec((rows, D_ATTN), row_map),
                  mod_spec, mod_spec, mod_spec,
                  pl.BlockSpec((1, 1, d), lambda i, j: (0, 0, 0)),
                  pl.BlockSpec((D_SSM + D_ATTN, d), lambda i, j: (0, 0)),
                  pl.BlockSpec((2, N_EXPERTS, d), lambda i, j: (0, 0, 0))],
        out_specs=[pl.BlockSpec((nbb, rb, d), lambda i, j: (i, j, 0)),
                   pl.BlockSpec((rows, d), row_map),
                   pl.BlockSpec((N_EXPERTS, rows), lambda i, j: (0, i * nj + j))],
        out_shape=[jax.ShapeDtypeStruct((nb, r, d), F32),
                   jax.ShapeDtypeStruct((t, d), F32),
                   jax.ShapeDtypeStruct((N_EXPERTS, t), F32)],
        compiler_params=_cparams(("arbitrary", "arbitrary")),
    )(x3, ys, oa, gt, sc, sh, g, wo, wrt)


def _router_kernel(lg_ref, rb_ref, ut_ref, idx_ref, w_ref, rank_ref, cnt_ref, carry_ref):
    i = pl.program_id(0)
    tm = lg_ref.shape[1]
    per_group = N_EXPERTS // N_EXPERT_GROUPS

    @pl.when(i == 0)
    def _():
        carry_ref[...] = jnp.zeros_like(carry_ref)

    scores = jax.nn.sigmoid(lg_ref[...])
    biased = scores + rb_ref[...]
    blks, grp = [], []
    for g in range(N_EXPERT_GROUPS):
        blk = biased[g * per_group:(g + 1) * per_group, :]
        m1 = jnp.max(blk, axis=0, keepdims=True)
        eq = blk == m1
        n_eq = jnp.sum(jnp.where(eq, 1.0, 0.0), axis=0, keepdims=True)
        m2 = jnp.max(jnp.where(eq, NEG_INF, blk), axis=0, keepdims=True)
        blks.append(blk)
        grp.append(m1 + jnp.where(n_eq >= 2.0, m1, m2))
    masked = []
    for g in range(N_EXPERT_GROUPS):
        beaten = jnp.zeros((1, tm), F32)
        for o in range(N_EXPERT_GROUPS):
            if o == g:
                continue
            ahead = (grp[o] >= grp[g]) if o < g else (grp[o] > grp[g])
            beaten = beaten + jnp.where(ahead, 1.0, 0.0)
        masked.append(jnp.where(beaten < float(TOPK_GROUPS), blks[g], NEG_INF))
    work = jnp.concatenate(masked, axis=0)

    eid = lax.broadcasted_iota(I32, (N_EXPERTS, tm), 0)
    chosen = jnp.zeros((N_EXPERTS, tm), F32)
    idxs, ws, sels = [], [], []
    for _ in range(TOP_K):
        m = jnp.max(work, axis=0, keepdims=True)
        first = jnp.min(jnp.where(work == m, eid, N_EXPERTS), axis=0, keepdims=True)
        sel = eid == first
        idxs.append(first)
        ws.append(jnp.sum(jnp.where(sel, scores, 0.0), axis=0, keepdims=True))
        sels.append(sel)
        chosen = jnp.where(sel, 1.0, chosen)
        work = jnp.where(sel, NEG_INF, work)
    wsum = ws[0]
    for wk in ws[1:]:
        wsum = wsum + wk

    prefix = jnp.dot(chosen.astype(BF16), ut_ref[...], preferred_element_type=F32) + carry_ref[...]
    carry_ref[...] = carry_ref[...] + jnp.sum(chosen, axis=1, keepdims=True)
    cnt_ref[...] = carry_ref[...]

    idx_ref[...] = jnp.zeros_like(idx_ref)
    w_ref[...] = jnp.zeros_like(w_ref)
    rank_ref[...] = jnp.zeros_like(rank_ref)
    for k in range(TOP_K):
        idx_ref[k:k + 1, :] = idxs[k]
        w_ref[k:k + 1, :] = ws[k] / wsum * ROUTED_SCALE
        rank = jnp.sum(jnp.where(sels[k], prefix, 0.0), axis=0, keepdims=True)
        rank_ref[k:k + 1, :] = rank.astype(I32)


def _router(lg, router_bias, tm):
    e, t = lg.shape
    ut = np.triu(np.ones((tm, tm), np.float32), 1)
    tok_spec = pl.BlockSpec((SUBLANES, tm), lambda i: (0, i))
    return pl.pallas_call(
        _router_kernel,
        grid=(t // tm,),
        in_specs=[pl.BlockSpec((e, tm), lambda i: (0, i)),
                  pl.BlockSpec((e, 1), lambda i: (0, 0)),
                  pl.BlockSpec((tm, tm), lambda i: (0, 0))],
        out_specs=[tok_spec, tok_spec, tok_spec, pl.BlockSpec((e, 1), lambda i: (0, 0))],
        out_shape=[jax.ShapeDtypeStruct((SUBLANES, t), I32), jax.ShapeDtypeStruct((SUBLANES, t), F32),
                   jax.ShapeDtypeStruct((SUBLANES, t), I32), jax.ShapeDtypeStruct((e, 1), F32)],
        scratch_shapes=[pltpu.VMEM((e, 1), F32)],
        compiler_params=_cparams(("arbitrary",)),
    )(lg, router_bias.reshape(e, 1), jnp.asarray(ut, BF16))


def _experts_kernel(be_ref, nv_ref, x_ref, wg_ref, wu_ref, wd_ref, o_ref):
    i = pl.program_id(0)

    @pl.when(i < nv_ref[0])
    def _():
        x = x_ref[...].astype(BF16)
        g = jnp.dot(x, wg_ref[0].astype(BF16), preferred_element_type=F32)
        u = jnp.dot(x, wu_ref[0].astype(BF16), preferred_element_type=F32)
        a = (g * jax.nn.sigmoid(g) * u).astype(BF16)
        o_ref[...] = jnp.dot(a, wd_ref[0].astype(BF16), preferred_element_type=F32)

    @pl.when(i >= nv_ref[0])
    def _():
        o_ref[...] = jnp.zeros_like(o_ref)


def _experts(block_e, n_valid, xs, w_eg, w_eu, w_ed, m):
    n_rows, d = xs.shape
    de = w_eg.shape[-1]
    nblk = n_rows // m
    xmap = lambda i, be, nv: (jnp.minimum(i, nv[0] - 1), 0)
    wmap = lambda i, be, nv: (be[i], 0, 0)
    grid_spec = pltpu.PrefetchScalarGridSpec(
        num_scalar_prefetch=2,
        grid=(nblk,),
        in_specs=[pl.BlockSpec((m, d), xmap),
                  pl.BlockSpec((1, d, de), wmap),
                  pl.BlockSpec((1, d, de), wmap),
                  pl.BlockSpec((1, de, d), wmap)],
        out_specs=pl.BlockSpec((m, d), lambda i, be, nv: (i, 0)),
    )
    return pl.pallas_call(
        _experts_kernel,
        grid_spec=grid_spec,
        out_shape=jax.ShapeDtypeStruct((n_rows, d), F32),
        compiler_params=_cparams(("arbitrary",)),
    )(block_e, n_valid, xs, w_eg, w_eu, w_ed)


def _row_copy(src, src_row, dst, dst_row, sem):
    return pltpu.make_async_copy(src.at[pl.ds(src_row, 1)], dst.at[pl.ds(dst_row, 1)], sem)


def _final_kernel(dest_ref, x1_ref, h2_ref, w_ref, gt_ref, sc_ref, sh_ref, g_ref, wgu_ref, wd_ref, yb_hbm,
                  y_ref, gbuf, sem):
    nbb, rb, d = x1_ref.shape
    rows = nbb * rb
    ds = wd_ref.shape[0]

    def issue(r, c):
        for k in range(TOP_K):
            _row_copy(yb_hbm, dest_ref[0, 0, r * TOP_K + k], gbuf.at[k], r, sem).start()
        return c

    lax.fori_loop(0, rows, issue, 0, unroll=2)

    gu = jnp.dot(h2_ref[...].astype(BF16), wgu_ref[...], preferred_element_type=F32)
    g = gu[:, :ds]
    a = (g * jax.nn.sigmoid(g) * gu[:, ds:]).astype(BF16)
    ff = jnp.dot(a, wd_ref[...], preferred_element_type=F32)

    for k in range(TOP_K):
        pltpu.make_async_copy(yb_hbm.at[pl.ds(0, rows)], gbuf.at[k], sem).wait()
    routed = gbuf[0] * w_ref[:, 0:1]
    for k in range(1, TOP_K):
        routed = routed + gbuf[k] * w_ref[:, k:k + 1]
    ff = routed + ff
    x2 = x1_ref[...] + gt_ref[...] * ff.reshape(nbb, rb, d)
    y_ref[...] = _rms(x2, g_ref[...]) * (1.0 + sc_ref[...]) + sh_ref[...]


def _final(dest3, x1, h2, wts, yb, gt, sc, sh, g, wgu, wd, nbb, rb):
    nb, r, d = x1.shape
    rows = nbb * rb
    nj = r // rb
    row_map = lambda i, j: (i * nj + j, 0)
    mod_spec = pl.BlockSpec((nbb, 1, d), lambda i, j: (i, 0, 0))
    return pl.pallas_call(
        _final_kernel,
        grid=(nb // nbb, nj),
        in_specs=[pl.BlockSpec((1, 1, rows * TOP_K), lambda i, j: (i * nj + j, 0, 0), memory_space=pltpu.SMEM),
                  pl.BlockSpec((nbb, rb, d), lambda i, j: (i, j, 0)),
                  pl.BlockSpec((rows, d), row_map),
                  pl.BlockSpec((rows, SUBLANES), row_map),
                  mod_spec, mod_spec, mod_spec,
                  pl.BlockSpec((1, 1, d), lambda i, j: (0, 0, 0)),
                  pl.BlockSpec(wgu.shape, lambda i, j: (0, 0)),
                  pl.BlockSpec(wd.shape, lambda i, j: (0, 0)),
                  pl.BlockSpec(memory_space=pl.ANY)],
        out_specs=pl.BlockSpec((nbb, rb, d), lambda i, j: (i, j, 0)),
        out_shape=jax.ShapeDtypeStruct((nb, r, d), F32),
        scratch_shapes=[pltpu.VMEM((TOP_K, rows, d), F32), pltpu.SemaphoreType.DMA(())],
        compiler_params=_cparams(("arbitrary", "arbitrary")),
    )(dest3, x1, h2, wts, gt, sc, sh, g, wgu, wd, yb)


def _dispatch_kernel(dest_ref, x_ref, xs_in, xs_hbm, sem):
    del xs_in
    rows = x_ref.shape[0]

    def issue(r, c):
        for k in range(TOP_K):
            _row_copy(x_ref, r, xs_hbm, dest_ref[0, 0, r * TOP_K + k], sem).start()
        return c

    lax.fori_loop(0, rows, issue, 0, unroll=2)
    for k in range(TOP_K):
        pltpu.make_async_copy(x_ref, xs_hbm.at[pl.ds(0, rows)], sem).wait()


def _dispatch(dest3, h2, n_rows):
    t, d = h2.shape
    rows = dest3.shape[-1] // TOP_K
    return pl.pallas_call(
        _dispatch_kernel,
        grid=(t // rows,),
        in_specs=[pl.BlockSpec((1, 1, rows * TOP_K), lambda i: (i, 0, 0), memory_space=pltpu.SMEM),
                  pl.BlockSpec((rows, d), lambda i: (i, 0)),
                  pl.BlockSpec(memory_space=pl.ANY)],
        out_specs=pl.BlockSpec(memory_space=pl.ANY),
        out_shape=jax.ShapeDtypeStruct((n_rows, d), F32),
        scratch_shapes=[pltpu.SemaphoreType.DMA(())],
        input_output_aliases={2: 0},
        compiler_params=_cparams(("arbitrary",)),
    )(dest3, h2, jnp.zeros((n_rows, d), F32))


def _moe(h2, lg, router_bias, w_eg, w_eu, w_ed, tm_router, tile):
    t, d = h2.shape
    e = N_EXPERTS
    m = MOE_ROWS
    idx_t, w_t, rank_t, counts = _router(lg, router_bias, tm_router)
    counts = counts.reshape(e).astype(I32)
    padded = ((counts + m - 1) // m) * m
    pad_end = jnp.cumsum(padded)
    pad_start = pad_end - padded
    onehot = idx_t[:, :, None] == jnp.arange(e, dtype=I32)
    dest_t = jnp.sum(jnp.where(onehot, pad_start, 0), axis=-1) + rank_t
    dest3 = dest_t[:TOP_K].T.reshape(t // tile, 1, tile * TOP_K)
    n_rows = (-(-(t * TOP_K) // m)) * m + e * m
    nblk = n_rows // m
    block_start = jnp.arange(nblk, dtype=I32) * m
    block_e = jnp.minimum(jnp.sum(pad_end[None, :] <= block_start[:, None], axis=1), e - 1).astype(I32)
    n_valid = (pad_end[-1] // m).astype(I32).reshape(1)
    xs = _dispatch(dest3, h2, n_rows)
    yb = _experts(block_e, n_valid, xs, w_eg, w_eu, w_ed, m)
    return dest3, w_t.T, yb


def kernel(x_prompt, x_sample, c_prompt, c_sample, cache_k, cache_v, cache_logf, state_ssm_re, state_ssm_im, page_table, w_ada, b_ada, g_norm1, w_in, b_fgate, ssm_lambda_re, ssm_lambda_im, ssm_log_step, ssm_b_re, ssm_b_im, ssm_c_re, ssm_c_im, ssm_d, w_glu, b_glu, g_ssm_out, g_attn_out, w_out, g_norm2, w_router, router_bias, w_exp_gate, w_exp_up, w_exp_down, w_sh_gate, w_sh_up, w_sh_down, g_final, w_ada_final, b_ada_final):
    depth = w_ada.shape[0]
    assert depth == 1, "one layer is supported"
    bp, lp, d = x_prompt.shape
    bs, ls, _ = x_sample.shape
    n_pages = page_table.shape[1]
    n_past = n_pages * PAGE
    n_groups = ssm_lambda_re.shape[1]

    n_c = bp + bs
    n_c_pad = -(-n_c // SUBLANES) * SUBLANES
    c_all = jnp.concatenate([c_prompt, c_sample, jnp.zeros((n_c_pad - n_c, d), F32)], axis=0)
    mod = _adaln(c_all, w_ada[0], b_ada[0])
    modf = _adaln(c_all, w_ada_final, b_ada_final)

    def mods(lo, hi):
        parts = [mod[lo:hi, k * d:(k + 1) * d][:, None, :] for k in range(6)]
        parts += [modf[lo:hi, k * d:(k + 1) * d][:, None, :] for k in range(2)]
        return parts

    w_main = w_in[0][:, :D_SSM + 3 * D_ATTN].astype(BF16)
    w_ft = w_in[0][:, D_SSM + 3 * D_ATTN:].T.astype(BF16)
    b_f = b_fgate[0].reshape(N_HEADS, 1)
    g1 = g_norm1[0].reshape(1, 1, d)
    g2 = g_norm2[0].reshape(1, 1, d)
    gf = g_final.reshape(1, 1, d)
    wb, wc, a_re, a_im = _s5_params(ssm_lambda_re[0], ssm_lambda_im[0], ssm_log_step[0], ssm_b_re[0],
                                    ssm_b_im[0], ssm_c_re[0], ssm_c_im[0])
    dsk = ssm_d[0].reshape(1, D_SSM)
    wglu = w_glu[0].astype(BF16)
    bglu = b_glu[0].reshape(1, D_SSM)
    g_so = g_ssm_out[0].reshape(1, D_SSM)
    g_ao = g_attn_out[0].reshape(1, D_ATTN)
    wo = w_out[0].astype(BF16)
    wr_t = w_router[0].T
    wr_hi = wr_t.astype(BF16)
    wrt = jnp.stack([wr_hi, (wr_t - wr_hi.astype(F32)).astype(BF16)])
    wgu = jnp.concatenate([w_sh_gate[0], w_sh_up[0]], axis=1).astype(BF16)
    wsd = w_sh_down[0].astype(BF16)

    def ssm_state(re, im):
        return jnp.concatenate([re.reshape(-1, 16, LANES), im.reshape(-1, 16, LANES)], axis=1)

    def split_state(ht):
        n = ht.shape[0]
        return (ht[:, :16].reshape(1, n, n_groups, SSM_STATE), ht[:, 16:].reshape(1, n, n_groups, SSM_STATE))

    tm = 512
    sh1, sc1, gt1, sh2, sc2, gt2, shf, scf = mods(0, bp)
    u, q, k, v, kb, vb, lft = _inproj(x_prompt, sc1, sh1, g1, w_main, w_ft, b_f, 1, tm)
    tq = 256
    cn = _neg_cumsum(lft.reshape(-1, LANES), lp // LANES, N_HEADS * bp * lp // LANES)
    cn3 = cn.reshape(N_HEADS, bp * lp // tq, tq).transpose(1, 0, 2)
    ys, ht = _s5(u.reshape(bp, lp, D_SSM), jnp.zeros((bp, 32, LANES), F32), wb, wc, a_re, a_im, dsk,
                 wglu, bglu, g_so, bp, 256)
    oa = _attn_prompt(q.reshape(bp, lp, D_ATTN), kb.reshape(bp, lp, D_ATTN), vb.reshape(bp, lp, D_ATTN),
                      cn3, g_ao, tq)
    x1, h2, lg = _outproj(x_prompt, ys.reshape(bp * lp, D_SSM), oa.reshape(bp * lp, D_ATTN),
                          gt1, sc2, sh2, g2, wo, wrt, 1, tm)
    dest3, wts, yb = _moe(h2, lg, router_bias[0], w_exp_gate[0], w_exp_up[0], w_exp_down[0], 512, tm)
    y_prompt = _final(dest3, x1, h2, wts, yb, gt2, scf, shf, gf, wgu, wsd, 1, tm)
    k_prompt = k.reshape(1, bp, lp, N_HEADS, HEAD_DIM)
    v_prompt = v.reshape(1, bp, lp, N_HEADS, HEAD_DIM)
    logf_prompt = lft.T.reshape(1, bp, lp, N_HEADS)
    sre_p, sim_p = split_state(ht)

    nbb = 64
    sh1, sc1, gt1, sh2, sc2, gt2, shf, scf = mods(bp, bp + bs)
    u, q, k, v, kb, vb, lft = _inproj(x_sample, sc1, sh1, g1, w_main, w_ft, b_f, nbb, ls)
    ys, ht = _s5(u.reshape(bs, ls, D_SSM), ssm_state(state_ssm_re[0], state_ssm_im[0]), wb, wc, a_re, a_im,
                 dsk, wglu, bglu, g_so, 32, ls)
    lf_past = cache_logf[0][page_table].reshape(bs, n_past, N_HEADS).transpose(0, 2, 1)
    lf_new = lft.reshape(N_HEADS, bs, ls).transpose(1, 0, 2)
    n_key_pad = -(-(n_past + ls) // LANES) * LANES
    lf_all = jnp.concatenate([lf_past, lf_new, jnp.zeros((bs, N_HEADS, n_key_pad - n_past - ls), F32)], axis=2)
    gsz = n_key_pad // LANES
    cn_s = _neg_cumsum(lf_all.reshape(-1, LANES), gsz, gsz * 64).reshape(bs, N_HEADS, n_key_pad)
    oa = _attn_paged(page_table, q.astype(F32).reshape(bs, ls, D_ATTN), k.reshape(bs, ls, D_ATTN),
                     v.reshape(bs, ls, D_ATTN), cn_s, g_ao,
                     cache_k[0].transpose(0, 2, 3, 1), cache_v[0].transpose(0, 2, 3, 1))
    x1, h2, lg = _outproj(x_sample, ys, oa.reshape(bs * ls, D_ATTN).astype(BF16),
                          gt1, sc2, sh2, g2, wo, wrt, nbb, ls)
    dest3, wts, yb = _moe(h2, lg, router_bias[0], w_exp_gate[0], w_exp_up[0], w_exp_down[0], 512, nbb * ls)
    y_sample = _final(dest3, x1, h2, wts, yb, gt2, scf, shf, gf, wgu, wsd, nbb, ls)
    k_sample = k.reshape(1, bs, ls, N_HEADS, HEAD_DIM)
    v_sample = v.reshape(1, bs, ls, N_HEADS, HEAD_DIM)
    logf_sample = lft.T.reshape(1, bs, ls, N_HEADS)
    sre_s, sim_s = split_state(ht)

    return (y_prompt, y_sample, k_prompt, v_prompt, logf_prompt, sre_p, sim_p,
            k_sample, v_sample, logf_sample, sre_s, sim_s)
```

```python
import functools
import math

import jax
import jax.numpy as jnp
import numpy as np
from jax import lax
from jax.experimental import pallas as pl
from jax.experimental.pallas import tpu as pltpu

F32 = jnp.float32
BF16 = jnp.bfloat16
I32 = jnp.int32

EPS = 1e-6
HEAD_DIM = 64
N_HEADS = 8
D_SSM = 512
D_ATTN = 512
SSM_GROUP = 16
SSM_STATE = 64
N_EXPERTS = 64
TOP_K = 6
N_EXPERT_GROUPS = 8
TOPK_GROUPS = 4
ROUTED_SCALE = 2.5
PAGE = 128

LANES = 128
SUBLANES = 8
VMEM_LIMIT = 48 * 1024 * 1024
MOE_ROWS = 256
ATTN_TILE = 256
LOG2E = math.log2(math.e)
NEG_INF = float("-inf")


def _cparams(sem):
    return pltpu.CompilerParams(dimension_semantics=sem, vmem_limit_bytes=VMEM_LIMIT)


def _bdot(a, b):
    return jnp.dot(a.astype(BF16), b.astype(BF16), preferred_element_type=F32)


def _bdot_nt(a, b):
    return lax.dot_general(a.astype(BF16), b.astype(BF16), (((1,), (1,)), ((), ())),
                           preferred_element_type=F32)


def _split3(v):
    hi = v.astype(BF16)
    r1 = v - hi.astype(F32)
    mid = r1.astype(BF16)
    lo = (r1 - mid.astype(F32)).astype(BF16)
    return hi, mid, lo


def _rms(x, g):
    return x * lax.rsqrt(jnp.mean(x * x, axis=-1, keepdims=True) + EPS) * g


def _adaln_kernel(c_ref, w_ref, b_ref, o_ref):
    c = c_ref[...]
    s = c * jax.nn.sigmoid(c)
    o_ref[...] = _bdot(s, w_ref[...]) + b_ref[...]


def _adaln(c, w, b):
    m, k = c.shape
    n = w.shape[1]
    tn = 1024
    return pl.pallas_call(
        _adaln_kernel,
        grid=(n // tn,),
        in_specs=[pl.BlockSpec((m, k), lambda j: (0, 0)),
                  pl.BlockSpec((k, tn), lambda j: (0, j)),
                  pl.BlockSpec((1, tn), lambda j: (0, j))],
        out_specs=pl.BlockSpec((m, tn), lambda j: (0, j)),
        out_shape=jax.ShapeDtypeStruct((m, n), F32),
        compiler_params=_cparams(("arbitrary",)),
    )(c, w, b.reshape(1, n))


def _inproj_kernel(x_ref, sc_ref, sh_ref, g_ref, w_ref, wqv_ref, bf_ref, *out_refs, transposed, q_scale):
    nbb, rb, d = x_ref.shape
    rows = nbb * rb
    x = x_ref[...]
    h = _rms(x, g_ref[...]) * (1.0 + sc_ref[...]) + sh_ref[...]
    hb = h.reshape(rows, d).astype(BF16)
    proj = jnp.dot(hb, w_ref[...], preferred_element_type=F32)
    u_ref, k_ref, v_ref, lf_ref = out_refs[:4]
    u_ref[...] = proj[:, :D_SSM]
    k = proj[:, D_SSM:D_SSM + D_ATTN]
    k_ref[...] = k
    v_ref[...] = proj[:, D_SSM + D_ATTN:D_SSM + 2 * D_ATTN]
    z = proj[:, D_SSM + 2 * D_ATTN:] + bf_ref[...]
    lf_ref[...] = jnp.minimum(z, 0.0) - jnp.log1p(jnp.exp(-jnp.abs(z)))
    nt = (((1,), (1,)), ((), ()))
    if transposed:
        kb_ref, qt_ref, vt_ref = out_refs[4:]
        kb_ref[...] = k.astype(BF16)
        qt = (lax.dot_general(wqv_ref[0], hb, nt, preferred_element_type=F32) * q_scale).astype(BF16)
        vt = lax.dot_general(wqv_ref[1], hb, nt, preferred_element_type=F32).astype(BF16)
        for c in range(rows // ATTN_TILE):
            qt_ref[c] = qt[:, c * ATTN_TILE:(c + 1) * ATTN_TILE]
            vt_ref[c] = vt[:, c * ATTN_TILE:(c + 1) * ATTN_TILE]
    else:
        (q_ref,) = out_refs[4:]
        q_ref[...] = lax.dot_general(hb, wqv_ref[0], nt, preferred_element_type=F32) * q_scale


def _inproj(x3, sc, sh, g, w_main, w_qv, b_f, nbb, rb, transposed, q_scale):
    nb, r, d = x3.shape
    t = nb * r
    rows = nbb * rb
    nj = r // rb
    grid = (nb // nbb, nj)
    row_map = lambda i, j: (i * nj + j, 0)
    n_main = w_main.shape[1]
    outs = [jax.ShapeDtypeStruct((t, D_SSM), F32), jax.ShapeDtypeStruct((t, D_ATTN), F32),
            jax.ShapeDtypeStruct((t, D_ATTN), F32), jax.ShapeDtypeStruct((t, LANES), F32)]
    out_specs = [pl.BlockSpec((rows, D_SSM), row_map), pl.BlockSpec((rows, D_ATTN), row_map),
                 pl.BlockSpec((rows, D_ATTN), row_map), pl.BlockSpec((rows, LANES), row_map)]
    if transposed:
        nc = rows // ATTN_TILE
        chunk_map = lambda i, j: (i * nj + j, 0, 0)
        outs += [jax.ShapeDtypeStruct((t, D_ATTN), BF16),
                 jax.ShapeDtypeStruct((t // ATTN_TILE, D_ATTN, ATTN_TILE), BF16),
                 jax.ShapeDtypeStruct((t // ATTN_TILE, D_ATTN, ATTN_TILE), BF16)]
        out_specs += [pl.BlockSpec((rows, D_ATTN), row_map),
                      pl.BlockSpec((nc, D_ATTN, ATTN_TILE), chunk_map),
                      pl.BlockSpec((nc, D_ATTN, ATTN_TILE), chunk_map)]
    else:
        outs += [jax.ShapeDtypeStruct((t, D_ATTN), F32)]
        out_specs += [pl.BlockSpec((rows, D_ATTN), row_map)]
    return pl.pallas_call(
        functools.partial(_inproj_kernel, transposed=transposed, q_scale=q_scale),
        grid=grid,
        in_specs=[pl.BlockSpec((nbb, rb, d), lambda i, j: (i, j, 0)),
                  pl.BlockSpec((nbb, 1, d), lambda i, j: (i, 0, 0)),
                  pl.BlockSpec((nbb, 1, d), lambda i, j: (i, 0, 0)),
                  pl.BlockSpec((1, 1, d), lambda i, j: (0, 0, 0)),
                  pl.BlockSpec((d, n_main), lambda i, j: (0, 0)),
                  pl.BlockSpec((2, D_ATTN, d), lambda i, j: (0, 0, 0)),
                  pl.BlockSpec((1, LANES), lambda i, j: (0, 0))],
        out_specs=out_specs,
        out_shape=outs,
        compiler_params=_cparams(("arbitrary", "arbitrary")),
    )(x3, sc, sh, g, w_main, w_qv, b_f)


def _crep_kernel(lf_ref, tri_ref, ex_ref, o_ref, carry_ref, *, scale):
    j = pl.program_id(1)

    @pl.when(j == 0)
    def _():
        carry_ref[...] = jnp.zeros_like(carry_ref)

    tri = tri_ref[...]
    ex = ex_ref[...]
    cs = sum(jnp.dot(tri, p, preferred_element_type=F32) for p in _split3(lf_ref[...]))
    rep = sum(jnp.dot(p, ex, preferred_element_type=F32) for p in _split3(cs))
    rep = rep + carry_ref[...]
    rows = rep.shape[0]
    carry_ref[...] = rep[rows - 1:rows, :]
    for h in range(N_HEADS):
        o_ref[0, h] = rep[:, h * LANES:(h + 1) * LANES] * (-scale)


def _crep(lf128, nb, length, scale):
    rows = ATTN_TILE
    nj = length // rows
    tri = np.tril(np.ones((rows, rows), np.float32))
    ex = np.zeros((LANES, N_HEADS * LANES), np.float32)
    for h in range(N_HEADS):
        ex[h, h * LANES:(h + 1) * LANES] = 1.0
    return pl.pallas_call(
        functools.partial(_crep_kernel, scale=scale),
        grid=(nb, nj),
        in_specs=[pl.BlockSpec((rows, LANES), lambda i, j: (i * nj + j, 0)),
                  pl.BlockSpec((rows, rows), lambda i, j: (0, 0)),
                  pl.BlockSpec((LANES, N_HEADS * LANES), lambda i, j: (0, 0))],
        out_specs=pl.BlockSpec((1, N_HEADS, rows, LANES), lambda i, j: (i, 0, j, 0)),
        out_shape=jax.ShapeDtypeStruct((nb, N_HEADS, length, LANES), F32),
        scratch_shapes=[pltpu.VMEM((1, N_HEADS * LANES), F32)],
        compiler_params=_cparams(("arbitrary", "arbitrary")),
    )(lf128, jnp.asarray(tri, BF16), jnp.asarray(ex, BF16))


def _cumsum_kernel(x_ref, tri_ref, lt_ref, o_ref):
    tri = tri_ref[...]
    lt = lt_ref[...]
    cs = sum(jnp.dot(p, tri, preferred_element_type=F32) for p in _split3(x_ref[...]))
    tot = jnp.broadcast_to(cs[:, LANES - 1:LANES], cs.shape)
    carry = sum(jnp.dot(lt, p, preferred_element_type=F32) for p in _split3(tot))
    o_ref[...] = -(cs + carry)


def _neg_cumsum(x2, gsz, rb):
    r = x2.shape[0]
    tri = np.triu(np.ones((LANES, LANES), np.float32))
    ii = np.arange(rb)
    lt = ((ii[:, None] // gsz == ii[None, :] // gsz) & (ii[None, :] < ii[:, None])).astype(np.float32)
    return pl.pallas_call(
        _cumsum_kernel,
        grid=(r // rb,),
        in_specs=[pl.BlockSpec((rb, LANES), lambda i: (i, 0)),
                  pl.BlockSpec((LANES, LANES), lambda i: (0, 0)),
                  pl.BlockSpec((rb, rb), lambda i: (0, 0))],
        out_specs=pl.BlockSpec((rb, LANES), lambda i: (i, 0)),
        out_shape=jax.ShapeDtypeStruct((r, LANES), F32),
        compiler_params=_cparams(("arbitrary",)),
    )(x2, jnp.asarray(tri, BF16), jnp.asarray(lt, BF16))


def _gelu_tanh(x):
    return 0.5 * x * (1.0 + jnp.tanh(math.sqrt(2.0 / math.pi) * (x + 0.044715 * (x * x * x))))


def _s5_kernel(u_ref, h0_ref, wb_ref, wc_ref, are_ref, aim_ref, dsk_ref, wglu_ref, bglu_ref, gout_ref,
               y_ref, ht_ref, s_ref, hc_ref, *, nseq, tm):
    rows = nseq * tm
    sr = rows + SUBLANES
    ti = pl.program_id(1)

    @pl.when(ti == 0)
    def _():
        hc_ref[...] = h0_ref[...]

    u = u_ref[...].reshape(rows, D_SSM)
    ub = u.astype(BF16)
    for c in range(4):
        bu = jnp.dot(ub[:, c * LANES:(c + 1) * LANES], wb_ref[c], preferred_element_type=F32)
        for jj in range(4):
            s_ref[pl.ds((4 * c + jj) * sr, rows), :] = bu[:, jj * LANES:(jj + 1) * LANES]
            s_ref[pl.ds((16 + 4 * c + jj) * sr, rows), :] = bu[:, 512 + jj * LANES:512 + (jj + 1) * LANES]

    ar = (are_ref[0:8, :], are_ref[8:16, :])
    ai = (aim_ref[0:8, :], aim_ref[8:16, :])

    def seq_group(sg, carry):
        base = sg * 4
        hs = []
        for b in range(4):
            hs.append(tuple(hc_ref[base + b, pl.ds(8 * q, 8), :] for q in range(4)))

        def step(t, hs):
            new = []
            for b in range(4):
                row = (base + b) * tm + t
                hr0, hr1, hi0, hi1 = hs[b]
                bre0 = s_ref[pl.ds(row, 8, stride=sr), :]
                bre1 = s_ref[pl.ds(8 * sr + row, 8, stride=sr), :]
                bim0 = s_ref[pl.ds(16 * sr + row, 8, stride=sr), :]
                bim1 = s_ref[pl.ds(24 * sr + row, 8, stride=sr), :]
                nr0 = ar[0] * hr0 - ai[0] * hi0 + bre0
                nr1 = ar[1] * hr1 - ai[1] * hi1 + bre1
                ni0 = ar[0] * hi0 + ai[0] * hr0 + bim0
                ni1 = ar[1] * hi1 + ai[1] * hr1 + bim1
                s_ref[pl.ds(row, 8, stride=sr), :] = nr0
                s_ref[pl.ds(8 * sr + row, 8, stride=sr), :] = nr1
                s_ref[pl.ds(16 * sr + row, 8, stride=sr), :] = ni0
                s_ref[pl.ds(24 * sr + row, 8, stride=sr), :] = ni1
                new.append((nr0, nr1, ni0, ni1))
            return tuple(new)

        hs = lax.fori_loop(0, tm, step, tuple(hs))
        for b in range(4):
            for q in range(4):
                hc_ref[base + b, pl.ds(8 * q, 8), :] = hs[b][q]
        return carry

    lax.fori_loop(0, nseq // 4, seq_group, 0)
    ht_ref[...] = hc_ref[...]

    ys = []
    for c in range(4):
        blocks = [s_ref[pl.ds((4 * c + jj) * sr, rows), :].astype(BF16) for jj in range(4)]
        blocks += [s_ref[pl.ds((16 + 4 * c + jj) * sr, rows), :].astype(BF16) for jj in range(4)]
        hcat = jnp.concatenate(blocks, axis=1)
        ys.append(jnp.dot(hcat, wc_ref[c], preferred_element_type=F32))
    y = jnp.concatenate(ys, axis=1) + dsk_ref[...] * u
    y = _gelu_tanh(y)
    gate = jax.nn.sigmoid(jnp.dot(y.astype(BF16), wglu_ref[...], preferred_element_type=F32) + bglu_ref[...])
    y = y * gate
    y_ref[...] = _rms(y, gout_ref[...]).astype(BF16).reshape(y_ref.shape)


def _s5(u3, h0, wb, wc, a_re, a_im, dsk, wglu, bglu, gout, nseq, tm):
    nb, length, _ = u3.shape
    rows = nseq * tm
    sr = rows + SUBLANES
    grid = (nb // nseq, length // tm)
    const2 = lambda i, j: (0, 0)
    const3 = lambda i, j: (0, 0, 0)
    if tm == length:
        y_spec = pl.BlockSpec((rows, D_SSM), lambda i, j: (i, 0))
        y_shape = jax.ShapeDtypeStruct((nb * length, D_SSM), BF16)
    else:
        y_spec = pl.BlockSpec((nseq, tm, D_SSM), lambda i, j: (i, j, 0))
        y_shape = jax.ShapeDtypeStruct((nb, length, D_SSM), BF16)
    return pl.pallas_call(
        functools.partial(_s5_kernel, nseq=nseq, tm=tm),
        grid=grid,
        in_specs=[pl.BlockSpec((nseq, tm, D_SSM), lambda i, j: (i, j, 0)),
                  pl.BlockSpec((nseq, 32, LANES), lambda i, j: (i, 0, 0)),
                  pl.BlockSpec((4, LANES, 1024), const3),
                  pl.BlockSpec((4, 1024, LANES), const3),
                  pl.BlockSpec((16, LANES), const2),
                  pl.BlockSpec((16, LANES), const2),
                  pl.BlockSpec((1, D_SSM), const2),
                  pl.BlockSpec((D_SSM, D_SSM), const2),
                  pl.BlockSpec((1, D_SSM), const2),
                  pl.BlockSpec((1, D_SSM), const2)],
        out_specs=[y_spec, pl.BlockSpec((nseq, 32, LANES), lambda i, j: (i, 0, 0))],
        out_shape=[y_shape, jax.ShapeDtypeStruct((nb, 32, LANES), F32)],
        scratch_shapes=[pltpu.VMEM((32 * sr, LANES), F32), pltpu.VMEM((nseq, 32, LANES), F32)],
        compiler_params=_cparams(("arbitrary", "arbitrary")),
    )(u3, h0, wb, wc, a_re, a_im, dsk, wglu, bglu, gout)


def _s5_params(lam_re, lam_im, log_step, b_re, b_im, c_re, c_im):
    g = lam_re.shape[0]
    dt = jnp.exp(log_step)[:, None]
    mag = jnp.exp(lam_re * dt)
    a_re = mag * jnp.cos(lam_im * dt)
    a_im = mag * jnp.sin(lam_im * dt)
    den = lam_re * lam_re + lam_im * lam_im
    n_re = a_re - 1.0
    f_re = (n_re * lam_re + a_im * lam_im) / den
    f_im = (a_im * lam_re - n_re * lam_im) / den
    bb_re = f_re[..., None] * b_re - f_im[..., None] * b_im
    bb_im = f_re[..., None] * b_im + f_im[..., None] * b_re
    eye = jnp.eye(g, dtype=F32)
    n_state = g * SSM_STATE

    def in_mat(bb):
        return jnp.einsum('gpi,gh->gihp', bb, eye).reshape(g * SSM_GROUP, n_state)

    def out_mat(cc):
        return jnp.einsum('gip,gh->gphi', cc, eye).reshape(n_state, g * SSM_GROUP)

    wbr, wbi = in_mat(bb_re), in_mat(bb_im)
    wcr, wci = out_mat(c_re), out_mat(-c_im)
    wb = jnp.stack([jnp.concatenate([wbr[c * 128:(c + 1) * 128, c * 512:(c + 1) * 512],
                                     wbi[c * 128:(c + 1) * 128, c * 512:(c + 1) * 512]], axis=1)
                    for c in range(4)]).astype(BF16)
    wc = jnp.stack([jnp.concatenate([wcr[c * 512:(c + 1) * 512, c * 128:(c + 1) * 128],
                                     wci[c * 512:(c + 1) * 512, c * 128:(c + 1) * 128]], axis=0)
                    for c in range(4)]).astype(BF16)
    return wb, wc, a_re.reshape(16, LANES), a_im.reshape(16, LANES)


def _attn_kernel(qt_ref, k_ref, vt_ref, cr_ref, g_ref, o_ref, qz_ref, m_ref, l_ref, acc_ref, st_ref):
    tq = ATTN_TILE
    qi = pl.program_id(1)
    half = LANES // 2

    rowid = lax.broadcasted_iota(I32, (LANES, tq), 0)
    for j in range(N_HEADS // 2):
        qp = qt_ref[0, j * LANES:(j + 1) * LANES, :]
        qz_ref[2 * j] = jnp.where(rowid < half, qp, jnp.zeros_like(qp))
        qz_ref[2 * j + 1] = jnp.where(rowid >= half, qp, jnp.zeros_like(qp))
    m_ref[...] = jnp.full_like(m_ref, NEG_INF)
    l_ref[...] = jnp.zeros_like(l_ref)
    acc_ref[...] = jnp.zeros_like(acc_ref)

    key_row = lax.broadcasted_iota(I32, (tq, tq), 0)
    q_col = lax.broadcasted_iota(I32, (tq, tq), 1)
    causal = key_row <= q_col

    def tile(kb, masked):
        ks = pl.multiple_of(kb * tq, tq)
        m_new = []
        for j in range(N_HEADS // 2):
            kp = k_ref[0, pl.ds(ks, tq), j * LANES:(j + 1) * LANES]
            for e in range(2):
                h = 2 * j + e
                st = jnp.dot(kp, qz_ref[h], preferred_element_type=F32)
                bias = cr_ref[0, h, pl.ds(ks, tq), :]
                st = st + jnp.concatenate([bias] * (tq // LANES), axis=1)
                if masked:
                    st = jnp.where(causal, st, NEG_INF)
                st_ref[h] = st
                m_new.append(jnp.maximum(m_ref[h:h + 1, :], jnp.max(st, axis=0, keepdims=True)))
        for h in range(N_HEADS):
            alpha = jnp.exp2(m_ref[h:h + 1, :] - m_new[h])
            p = jnp.exp2(st_ref[h] - m_new[h])
            l_ref[h:h + 1, :] = alpha * l_ref[h:h + 1, :] + jnp.sum(p, axis=0, keepdims=True)
            m_ref[h:h + 1, :] = m_new[h]
            rows = slice(h * HEAD_DIM, (h + 1) * HEAD_DIM)
            pv = jnp.dot(vt_ref[kb, rows, :], p.astype(BF16), preferred_element_type=F32)
            acc_ref[rows, :] = alpha * acc_ref[rows, :] + pv

    def body(kb, c):
        tile(kb, False)
        return c

    lax.fori_loop(0, qi, body, 0)
    tile(qi, True)

    for h in range(N_HEADS):
        rows = slice(h * HEAD_DIM, (h + 1) * HEAD_DIM)
        acc_ref[rows, :] = acc_ref[rows, :] / l_ref[h:h + 1, :]
    o = acc_ref[...].T
    o_ref[0] = _rms(o, g_ref[...]).astype(BF16)


def _attn_prompt(qt, k, vt, crep, g):
    b, length, d = k.shape
    tq = ATTN_TILE
    nq = length // tq
    once = pl.Buffered(1)
    return pl.pallas_call(
        _attn_kernel,
        grid=(b, nq),
        in_specs=[pl.BlockSpec((1, d, tq), lambda i, j: (i * nq + j, 0, 0)),
                  pl.BlockSpec((1, length, d), lambda i, j: (i, 0, 0), pipeline_mode=once),
                  pl.BlockSpec((nq, d, tq), lambda i, j: (i, 0, 0), pipeline_mode=once),
                  pl.BlockSpec((1, N_HEADS, length, LANES), lambda i, j: (i, 0, 0, 0), pipeline_mode=once),
                  pl.BlockSpec((1, d), lambda i, j: (0, 0))],
        out_specs=pl.BlockSpec((1, tq, d), lambda i, j: (i, j, 0)),
        out_shape=jax.ShapeDtypeStruct((b, length, d), BF16),
        scratch_shapes=[pltpu.VMEM((N_HEADS, LANES, tq), BF16), pltpu.VMEM((N_HEADS, tq), F32),
                        pltpu.VMEM((N_HEADS, tq), F32), pltpu.VMEM((d, tq), F32),
                        pltpu.VMEM((N_HEADS, tq, tq), F32)],
        compiler_params=_cparams(("arbitrary", "arbitrary")),
    )(qt, k, vt, crep, g)


def _attn_paged_kernel(pt_ref, q_ref, kn_ref, vn_ref, cn_ref, g_ref, kc_hbm, vc_hbm, o_ref,
                       kbuf, vbuf, sem, *, n_pages, n_new):
    i = pl.program_id(0)
    n = pl.num_programs(0)
    n_past = n_pages * PAGE

    def page_copies(seq, slot):
        cps = []
        for p in range(n_pages):
            pg = pt_ref[seq * n_pages + p]
            cps.append(pltpu.make_async_copy(kc_hbm.at[pg], kbuf.at[slot, p], sem.at[0, slot]))
            cps.append(pltpu.make_async_copy(vc_hbm.at[pg], vbuf.at[slot, p], sem.at[1, slot]))
        return cps

    @pl.when(i == 0)
    def _():
        for cp in page_copies(0, 0):
            cp.start()

    @pl.when(i + 1 < n)
    def _():
        for cp in page_copies(i + 1, (i + 1) % 2):
            cp.start()

    slot = i % 2
    for cp in page_copies(i, slot):
        cp.wait()

    d = q_ref.shape[-1]
    nr = N_HEADS * n_new
    new_bits = n_new.bit_length() - 1
    head_bits = HEAD_DIM.bit_length() - 1
    rowh = lax.shift_right_logical(lax.broadcasted_iota(I32, (nr, d), 0), new_bits)
    colh = lax.shift_right_logical(lax.broadcasted_iota(I32, (nr, d), 1), head_bits)
    bd = rowh == colh
    qrep = jnp.broadcast_to(q_ref[0][None], (N_HEADS, n_new, d)).reshape(nr, d)
    qbd = jnp.where(bd, qrep, 0.0).astype(BF16)
    cn = cn_ref[0]
    cnr = jnp.broadcast_to(cn[:, None, :], (N_HEADS, n_new, cn.shape[-1])).reshape(nr, cn.shape[-1])

    s_p = []
    for p in range(n_pages):
        kt = kbuf[slot, p].reshape(d, PAGE)
        s_p.append(_bdot(qbd, kt) + cnr[:, p * PAGE:(p + 1) * PAGE])
    s_n = _bdot_nt(qbd, kn_ref[0]) + cnr[:, n_past:n_past + n_new]
    qpos = lax.broadcasted_iota(I32, (nr, n_new), 0) & (n_new - 1)
    kpos = lax.broadcasted_iota(I32, (nr, n_new), 1)
    s_n = jnp.where(kpos <= qpos, s_n, NEG_INF)
    m = jnp.max(s_n, axis=1, keepdims=True)
    for sp in s_p:
        m = jnp.maximum(m, jnp.max(sp, axis=1, keepdims=True))
    p_n = jnp.exp(s_n - m)
    l = jnp.sum(p_n, axis=1, keepdims=True)
    of = _bdot(p_n, vn_ref[0])
    for p in range(n_pages):
        pp = jnp.exp(s_p[p] - m)
        l = l + jnp.sum(pp, axis=1, keepdims=True)
        of = of + _bdot_nt(pp, vbuf[slot, p].reshape(d, PAGE))
    of = jnp.where(bd, of / l, 0.0)
    o = jnp.sum(of.reshape(N_HEADS, n_new, d), axis=0)
    o_ref[0] = _rms(o, g_ref[...])


def _attn_paged(page_table, q, kn, vn, cn, g, cache_k, cache_v):
    nseq, n_new, d = q.shape
    assert n_new & (n_new - 1) == 0, "new-token count must be a power of two"
    n_pages = page_table.shape[1]
    n_past = n_pages * PAGE
    grid_spec = pltpu.PrefetchScalarGridSpec(
        num_scalar_prefetch=1,
        grid=(nseq,),
        in_specs=[pl.BlockSpec((1, n_new, d), lambda i, pt: (i, 0, 0)),
                  pl.BlockSpec((1, n_new, d), lambda i, pt: (i, 0, 0)),
                  pl.BlockSpec((1, n_new, d), lambda i, pt: (i, 0, 0)),
                  pl.BlockSpec((1, N_HEADS, cn.shape[-1]), lambda i, pt: (i, 0, 0)),
                  pl.BlockSpec((1, d), lambda i, pt: (0, 0)),
                  pl.BlockSpec(memory_space=pl.ANY),
                  pl.BlockSpec(memory_space=pl.ANY)],
        out_specs=pl.BlockSpec((1, n_new, d), lambda i, pt: (i, 0, 0)),
        scratch_shapes=[pltpu.VMEM((2, n_pages, N_HEADS, HEAD_DIM, PAGE), F32),
                        pltpu.VMEM((2, n_pages, N_HEADS, HEAD_DIM, PAGE), F32),
                        pltpu.SemaphoreType.DMA((2, 2))],
    )
    return pl.pallas_call(
        functools.partial(_attn_paged_kernel, n_pages=n_pages, n_new=n_new),
        grid_spec=grid_spec,
        out_shape=jax.ShapeDtypeStruct((nseq, n_new, d), F32),
        compiler_params=_cparams(("arbitrary",)),
    )(page_table.reshape(-1), q, kn, vn, cn, g, cache_k, cache_v)


def _outproj_kernel(x_ref, ys_ref, oa_ref, gt_ref, sc_ref, sh_ref, g_ref, wo_ref, wrt_ref,
                    x1_ref, h2_ref, lg_ref):
    nbb, rb, d = x_ref.shape
    mix = jnp.dot(ys_ref[...], wo_ref[0:D_SSM, :], preferred_element_type=F32)
    mix = mix + jnp.dot(oa_ref[...], wo_ref[D_SSM:, :], preferred_element_type=F32)
    x1 = x_ref[...] + gt_ref[...] * mix.reshape(nbb, rb, d)
    x1_ref[...] = x1
    h2 = (_rms(x1, g_ref[...]) * (1.0 + sc_ref[...]) + sh_ref[...]).reshape(nbb * rb, d)
    hb = h2.astype(BF16)
    h2_ref[...] = h2
    hlo = (h2 - hb.astype(F32)).astype(BF16)
    whi = wrt_ref[0]
    wlo = wrt_ref[1]
    nt = (((1,), (1,)), ((), ()))
    lg = lax.dot_general(whi, hb, nt, preferred_element_type=F32)
    lg = lg + lax.dot_general(wlo, hb, nt, preferred_element_type=F32)
    lg = lg + lax.dot_general(whi, hlo, nt, preferred_element_type=F32)
    lg_ref[...] = lg


def _outproj(x3, ys, oa, gt, sc, sh, g, wo, wrt, nbb, rb):
    nb, r, d = x3.shape
    t = nb * r
    rows = nbb * rb
    nj = r // rb
    row_map = lambda i, j: (i * nj + j, 0)
    mod_spec = pl.BlockSpec((nbb, 1, d), lambda i, j: (i, 0, 0))
    return pl.pallas_call(
        _outproj_kernel,
        grid=(nb // nbb, nj),
        in_specs=[pl.BlockSpec((nbb, rb, d), lambda i, j: (i, j, 0)),
                  pl.BlockSpec((rows, D_SSM), row_map),
                  pl.BlockSpec((rows, D_ATTN), row_map),
                  mod_spec, mod_spec, mod_spec,
                  pl.BlockSpec((1, 1, d), lambda i, j: (0, 0, 0)),
                  pl.BlockSpec((D_SSM + D_ATTN, d), lambda i, j: (0, 0)),
                  pl.BlockSpec((2, N_EXPERTS, d), lambda i, j: (0, 0, 0))],
        out_specs=[pl.BlockSpec((nbb, rb, d), lambda i, j: (i, j, 0)),
                   pl.BlockSpec((rows, d), row_map),
                   pl.BlockSpec((N_EXPERTS, rows), lambda i, j: (0, i * nj + j))],
        out_shape=[jax.ShapeDtypeStruct((nb, r, d), F32),
                   jax.ShapeDtypeStruct((t, d), F32),
                   jax.ShapeDtypeStruct((N_EXPERTS, t), F32)],
        compiler_params=_cparams(("arbitrary", "arbitrary")),
    )(x3, ys, oa, gt, sc, sh, g, wo, wrt)


def _router_kernel(lg_ref, rb_ref, ut_ref, idx_ref, w_ref, rank_ref, cnt_ref, carry_ref):
    i = pl.program_id(0)
    tm = lg_ref.shape[1]
    per_group = N_EXPERTS // N_EXPERT_GROUPS

    @pl.when(i == 0)
    def _():
        carry_ref[...] = jnp.zeros_like(carry_ref)

    scores = jax.nn.sigmoid(lg_ref[...])
    biased = scores + rb_ref[...]
    blks, grp = [], []
    for g in range(N_EXPERT_GROUPS):
        blk = biased[g * per_group:(g + 1) * per_group, :]
        m1 = jnp.max(blk, axis=0, keepdims=True)
        eq = blk == m1
        n_eq = jnp.sum(jnp.where(eq, 1.0, 0.0), axis=0, keepdims=True)
        m2 = jnp.max(jnp.where(eq, NEG_INF, blk), axis=0, keepdims=True)
        blks.append(blk)
        grp.append(m1 + jnp.where(n_eq >= 2.0, m1, m2))
    masked = []
    for g in range(N_EXPERT_GROUPS):
        beaten = jnp.zeros((1, tm), F32)
        for o in range(N_EXPERT_GROUPS):
            if o == g:
                continue
            ahead = (grp[o] >= grp[g]) if o < g else (grp[o] > grp[g])
            beaten = beaten + jnp.where(ahead, 1.0, 0.0)
        masked.append(jnp.where(beaten < float(TOPK_GROUPS), blks[g], NEG_INF))
    work = jnp.concatenate(masked, axis=0)

    eid = lax.broadcasted_iota(I32, (N_EXPERTS, tm), 0)
    chosen = jnp.zeros((N_EXPERTS, tm), F32)
    idxs, ws, sels = [], [], []
    for _ in range(TOP_K):
        m = jnp.max(work, axis=0, keepdims=True)
        first = jnp.min(jnp.where(work == m, eid, N_EXPERTS), axis=0, keepdims=True)
        sel = eid == first
        idxs.append(first)
        ws.append(jnp.sum(jnp.where(sel, scores, 0.0), axis=0, keepdims=True))
        sels.append(sel)
        chosen = jnp.where(sel, 1.0, chosen)
        work = jnp.where(sel, NEG_INF, work)
    wsum = ws[0]
    for wk in ws[1:]:
        wsum = wsum + wk

    prefix = jnp.dot(chosen.astype(BF16), ut_ref[...], preferred_element_type=F32) + carry_ref[...]
    carry_ref[...] = carry_ref[...] + jnp.sum(chosen, axis=1, keepdims=True)
    cnt_ref[...] = carry_ref[...]

    idx_ref[...] = jnp.zeros_like(idx_ref)
    w_ref[...] = jnp.zeros_like(w_ref)
    rank_ref[...] = jnp.zeros_like(rank_ref)
    for k in range(TOP_K):
        idx_ref[k:k + 1, :] = idxs[k]
        w_ref[k:k + 1, :] = ws[k] / wsum * ROUTED_SCALE
        rank = jnp.sum(jnp.where(sels[k], prefix, 0.0), axis=0, keepdims=True)
        rank_ref[k:k + 1, :] = rank.astype(I32)


def _router(lg, router_bias, tm):
    e, t = lg.shape
    ut = np.triu(np.ones((tm, tm), np.float32), 1)
    tok_spec = pl.BlockSpec((SUBLANES, tm), lambda i: (0, i))
    return pl.pallas_call(
        _router_kernel,
        grid=(t // tm,),
        in_specs=[pl.BlockSpec((e, tm), lambda i: (0, i)),
                  pl.BlockSpec((e, 1), lambda i: (0, 0)),
                  pl.BlockSpec((tm, tm), lambda i: (0, 0))],
        out_specs=[tok_spec, tok_spec, tok_spec, pl.BlockSpec((e, 1), lambda i: (0, 0))],
        out_shape=[jax.ShapeDtypeStruct((SUBLANES, t), I32), jax.ShapeDtypeStruct((SUBLANES, t), F32),
                   jax.ShapeDtypeStruct((SUBLANES, t), I32), jax.ShapeDtypeStruct((e, 1), F32)],
        scratch_shapes=[pltpu.VMEM((e, 1), F32)],
        compiler_params=_cparams(("arbitrary",)),
    )(lg, router_bias.reshape(e, 1), jnp.asarray(ut, BF16))


def _experts_kernel(be_ref, nv_ref, x_ref, wg_ref, wu_ref, wd_ref, o_ref):
    i = pl.program_id(0)

    @pl.when(i < nv_ref[0])
    def _():
        x = x_ref[...].astype(BF16)
        g = jnp.dot(x, wg_ref[0].astype(BF16), preferred_element_type=F32)
        u = jnp.dot(x, wu_ref[0].astype(BF16), preferred_element_type=F32)
        a = (g * jax.nn.sigmoid(g) * u).astype(BF16)
        o_ref[...] = jnp.dot(a, wd_ref[0].astype(BF16), preferred_element_type=F32)

    @pl.when(i >= nv_ref[0])
    def _():
        o_ref[...] = jnp.zeros_like(o_ref)


def _experts(block_e, n_valid, xs, w_eg, w_eu, w_ed, m):
    n_rows, d = xs.shape
    de = w_eg.shape[-1]
    nblk = n_rows // m
    xmap = lambda i, be, nv: (jnp.minimum(i, nv[0] - 1), 0)
    wmap = lambda i, be, nv: (be[i], 0, 0)
    grid_spec = pltpu.PrefetchScalarGridSpec(
        num_scalar_prefetch=2,
        grid=(nblk,),
        in_specs=[pl.BlockSpec((m, d), xmap),
                  pl.BlockSpec((1, d, de), wmap),
                  pl.BlockSpec((1, d, de), wmap),
                  pl.BlockSpec((1, de, d), wmap)],
        out_specs=pl.BlockSpec((m, d), lambda i, be, nv: (i, 0)),
    )
    return pl.pallas_call(
        _experts_kernel,
        grid_spec=grid_spec,
        out_shape=jax.ShapeDtypeStruct((n_rows, d), F32),
        compiler_params=_cparams(("arbitrary",)),
    )(block_e, n_valid, xs, w_eg, w_eu, w_ed)


def _row_copy(src, src_row, dst, dst_row, sem):
    return pltpu.make_async_copy(src.at[pl.ds(src_row, 1)], dst.at[pl.ds(dst_row, 1)], sem)


def _final_kernel(dest_ref, x1_ref, h2_ref, w_ref, gt_ref, sc_ref, sh_ref, g_ref, wgu_ref, wd_ref, yb_hbm,
                  y_ref, gbuf, sem):
    nbb, rb, d = x1_ref.shape
    rows = nbb * rb
    ds = wd_ref.shape[0]

    def issue(r, c):
        for k in range(TOP_K):
            _row_copy(yb_hbm, dest_ref[0, 0, r * TOP_K + k], gbuf.at[k], r, sem).start()
        return c

    lax.fori_loop(0, rows, issue, 0, unroll=2)

    gu = jnp.dot(h2_ref[...].astype(BF16), wgu_ref[...], preferred_element_type=F32)
    g = gu[:, :ds]
    a = (g * jax.nn.sigmoid(g) * gu[:, ds:]).astype(BF16)
    ff = jnp.dot(a, wd_ref[...], preferred_element_type=F32)

    for k in range(TOP_K):
        pltpu.make_async_copy(yb_hbm.at[pl.ds(0, rows)], gbuf.at[k], sem).wait()
    routed = gbuf[0] * w_ref[:, 0:1]
    for k in range(1, TOP_K):
        routed = routed + gbuf[k] * w_ref[:, k:k + 1]
    ff = routed + ff
    x2 = x1_ref[...] + gt_ref[...] * ff.reshape(nbb, rb, d)
    y_ref[...] = _rms(x2, g_ref[...]) * (1.0 + sc_ref[...]) + sh_ref[...]


def _final(dest3, x1, h2, wts, yb, gt, sc, sh, g, wgu, wd, nbb, rb):
    nb, r, d = x1.shape
    rows = nbb * rb
    nj = r // rb
    row_map = lambda i, j: (i * nj + j, 0)
    mod_spec = pl.BlockSpec((nbb, 1, d), lambda i, j: (i, 0, 0))
    return pl.pallas_call(
        _final_kernel,
        grid=(nb // nbb, nj),
        in_specs=[pl.BlockSpec((1, 1, rows * TOP_K), lambda i, j: (i * nj + j, 0, 0), memory_space=pltpu.SMEM),
                  pl.BlockSpec((nbb, rb, d), lambda i, j: (i, j, 0)),
                  pl.BlockSpec((rows, d), row_map),
                  pl.BlockSpec((rows, SUBLANES), row_map),
                  mod_spec, mod_spec, mod_spec,
                  pl.BlockSpec((1, 1, d), lambda i, j: (0, 0, 0)),
                  pl.BlockSpec(wgu.shape, lambda i, j: (0, 0)),
                  pl.BlockSpec(wd.shape, lambda i, j: (0, 0)),
                  pl.BlockSpec(memory_space=pl.ANY)],
        out_specs=pl.BlockSpec((nbb, rb, d), lambda i, j: (i, j, 0)),
        out_shape=jax.ShapeDtypeStruct((nb, r, d), F32),
        scratch_shapes=[pltpu.VMEM((TOP_K, rows, d), F32), pltpu.SemaphoreType.DMA(())],
        compiler_params=_cparams(("arbitrary", "arbitrary")),
    )(dest3, x1, h2, wts, gt, sc, sh, g, wgu, wd, yb)


def _dispatch_kernel(dest_ref, x_ref, xs_in, xs_hbm, sem):
    del xs_in
    rows = x_ref.shape[0]

    def issue(r, c):
        for k in range(TOP_K):
            _row_copy(x_ref, r, xs_hbm, dest_ref[0, 0, r * TOP_K + k], sem).start()
        return c

    lax.fori_loop(0, rows, issue, 0, unroll=2)
    for k in range(TOP_K):
        pltpu.make_async_copy(x_ref, xs_hbm.at[pl.ds(0, rows)], sem).wait()


def _dispatch(dest3, h2, n_rows):
    t, d = h2.shape
    rows = dest3.shape[-1] // TOP_K
    return pl.pallas_call(
        _dispatch_kernel,
        grid=(t // rows,),
        in_specs=[pl.BlockSpec((1, 1, rows * TOP_K), lambda i: (i, 0, 0), memory_space=pltpu.SMEM),
                  pl.BlockSpec((rows, d), lambda i: (i, 0)),
                  pl.BlockSpec(memory_space=pl.ANY)],
        out_specs=pl.BlockSpec(memory_space=pl.ANY),
        out_shape=jax.ShapeDtypeStruct((n_rows, d), F32),
        scratch_shapes=[pltpu.SemaphoreType.DMA(())],
        input_output_aliases={2: 0},
        compiler_params=_cparams(("arbitrary",)),
    )(dest3, h2, jnp.zeros((n_rows, d), F32))


def _moe(h2, lg, router_bias, w_eg, w_eu, w_ed, tm_router, tile):
    t, d = h2.shape
    e = N_EXPERTS
    m = MOE_ROWS
    idx_t, w_t, rank_t, counts = _router(lg, router_bias, tm_router)
    counts = counts.reshape(e).astype(I32)
    padded = ((counts + m - 1) // m) * m
    pad_end = jnp.cumsum(padded)
    pad_start = pad_end - padded
    onehot = idx_t[:, :, None] == jnp.arange(e, dtype=I32)
    dest_t = jnp.sum(jnp.where(onehot, pad_start, 0), axis=-1) + rank_t
    dest3 = dest_t[:TOP_K].T.reshape(t // tile, 1, tile * TOP_K)
    n_rows = (-(-(t * TOP_K) // m)) * m + e * m
    nblk = n_rows // m
    block_start = jnp.arange(nblk, dtype=I32) * m
    block_e = jnp.minimum(jnp.sum(pad_end[None, :] <= block_start[:, None], axis=1), e - 1).astype(I32)
    n_valid = (pad_end[-1] // m).astype(I32).reshape(1)
    xs = _dispatch(dest3, h2, n_rows)
    yb = _experts(block_e, n_valid, xs, w_eg, w_eu, w_ed, m)
    return dest3, w_t.T, yb


def kernel(x_prompt, x_sample, c_prompt, c_sample, cache_k, cache_v, cache_logf, state_ssm_re, state_ssm_im, page_table, w_ada, b_ada, g_norm1, w_in, b_fgate, ssm_lambda_re, ssm_lambda_im, ssm_log_step, ssm_b_re, ssm_b_im, ssm_c_re, ssm_c_im, ssm_d, w_glu, b_glu, g_ssm_out, g_attn_out, w_out, g_norm2, w_router, router_bias, w_exp_gate, w_exp_up, w_exp_down, w_sh_gate, w_sh_up, w_sh_down, g_final, w_ada_final, b_ada_final):
    depth = w_ada.shape[0]
    assert depth == 1, "one layer is supported"
    bp, lp, d = x_prompt.shape
    bs, ls, _ = x_sample.shape
    n_pages = page_table.shape[1]
    n_past = n_pages * PAGE
    n_groups = ssm_lambda_re.shape[1]

    n_c = bp + bs
    n_c_pad = -(-n_c // SUBLANES) * SUBLANES
    c_all = jnp.concatenate([c_prompt, c_sample, jnp.zeros((n_c_pad - n_c, d), F32)], axis=0)
    mod = _adaln(c_all, w_ada[0], b_ada[0])
    modf = _adaln(c_all, w_ada_final, b_ada_final)

    def mods(lo, hi):
        parts = [mod[lo:hi, k * d:(k + 1) * d][:, None, :] for k in range(6)]
        parts += [modf[lo:hi, k * d:(k + 1) * d][:, None, :] for k in range(2)]
        return parts

    w_u, w_q, w_k, w_v, w_f = jnp.split(w_in[0], [D_SSM, D_SSM + D_ATTN, D_SSM + 2 * D_ATTN,
                                                   D_SSM + 3 * D_ATTN], axis=1)
    w_main = jnp.concatenate([w_u, w_k, w_v, w_f, jnp.zeros((d, LANES - N_HEADS), F32)], axis=1).astype(BF16)
    w_qv = jnp.stack([w_q.T, w_v.T]).astype(BF16)
    b_f = jnp.concatenate([b_fgate[0], jnp.zeros((LANES - N_HEADS,), F32)]).reshape(1, LANES)
    g1 = g_norm1[0].reshape(1, 1, d)
    g2 = g_norm2[0].reshape(1, 1, d)
    gf = g_final.reshape(1, 1, d)
    wb, wc, a_re, a_im = _s5_params(ssm_lambda_re[0], ssm_lambda_im[0], ssm_log_step[0], ssm_b_re[0],
                                    ssm_b_im[0], ssm_c_re[0], ssm_c_im[0])
    dsk = ssm_d[0].reshape(1, D_SSM)
    wglu = w_glu[0].astype(BF16)
    bglu = b_glu[0].reshape(1, D_SSM)
    g_so = g_ssm_out[0].reshape(1, D_SSM)
    g_ao = g_attn_out[0].reshape(1, D_ATTN)
    wo = w_out[0].astype(BF16)
    wr_t = w_router[0].T
    wr_hi = wr_t.astype(BF16)
    wrt = jnp.stack([wr_hi, (wr_t - wr_hi.astype(F32)).astype(BF16)])
    wgu = jnp.concatenate([w_sh_gate[0], w_sh_up[0]], axis=1).astype(BF16)
    wsd = w_sh_down[0].astype(BF16)

    def ssm_state(re, im):
        return jnp.concatenate([re.reshape(-1, 16, LANES), im.reshape(-1, 16, LANES)], axis=1)

    def split_state(ht):
        n = ht.shape[0]
        return (ht[:, :16].reshape(1, n, n_groups, SSM_STATE), ht[:, 16:].reshape(1, n, n_groups, SSM_STATE))

    tm = 512
    sh1, sc1, gt1, sh2, sc2, gt2, shf, scf = mods(0, bp)
    u, k, v, lf, kb, qt, vt = _inproj(x_prompt, sc1, sh1, g1, w_main, w_qv, b_f, 1, tm, True,
                                      HEAD_DIM ** -0.5 * LOG2E)
    crep = _crep(lf, bp, lp, LOG2E)
    ys, ht = _s5(u.reshape(bp, lp, D_SSM), jnp.zeros((bp, 32, LANES), F32), wb, wc, a_re, a_im, dsk,
                 wglu, bglu, g_so, bp, 256)
    oa = _attn_prompt(qt, kb.reshape(bp, lp, D_ATTN), vt, crep, g_ao)
    x1, h2, lg = _outproj(x_prompt, ys.reshape(bp * lp, D_SSM), oa.reshape(bp * lp, D_ATTN),
                          gt1, sc2, sh2, g2, wo, wrt, 1, tm)
    dest3, wts, yb = _moe(h2, lg, router_bias[0], w_exp_gate[0], w_exp_up[0], w_exp_down[0], 512, tm)
    y_prompt = _final(dest3, x1, h2, wts, yb, gt2, scf, shf, gf, wgu, wsd, 1, tm)
    k_prompt = k.reshape(1, bp, lp, N_HEADS, HEAD_DIM)
    v_prompt = v.reshape(1, bp, lp, N_HEADS, HEAD_DIM)
    logf_prompt = lf[:, :N_HEADS].reshape(1, bp, lp, N_HEADS)
    sre_p, sim_p = split_state(ht)

    nbb = 64
    sh1, sc1, gt1, sh2, sc2, gt2, shf, scf = mods(bp, bp + bs)
    u, k, v, lf, q = _inproj(x_sample, sc1, sh1, g1, w_main, w_qv, b_f, nbb, ls, False, HEAD_DIM ** -0.5)
    ys, ht = _s5(u.reshape(bs, ls, D_SSM), ssm_state(state_ssm_re[0], state_ssm_im[0]), wb, wc, a_re, a_im,
                 dsk, wglu, bglu, g_so, 32, ls)
    lf_past = cache_logf[0][page_table].reshape(bs, n_past, N_HEADS).transpose(0, 2, 1)
    lf_new = lf[:, :N_HEADS].reshape(bs, ls, N_HEADS).transpose(0, 2, 1)
    n_key_pad = -(-(n_past + ls) // LANES) * LANES
    lf_all = jnp.concatenate([lf_past, lf_new, jnp.zeros((bs, N_HEADS, n_key_pad - n_past - ls), F32)], axis=2)
    gsz = n_key_pad // LANES
    cn_s = _neg_cumsum(lf_all.reshape(-1, LANES), gsz, gsz * 64).reshape(bs, N_HEADS, n_key_pad)
    oa = _attn_paged(page_table, q.reshape(bs, ls, D_ATTN), k.reshape(bs, ls, D_ATTN),
                     v.reshape(bs, ls, D_ATTN), cn_s, g_ao,
                     cache_k[0].transpose(0, 2, 3, 1), cache_v[0].transpose(0, 2, 3, 1))
    x1, h2, lg = _outproj(x_sample, ys, oa.reshape(bs * ls, D_ATTN).astype(BF16),
                          gt1, sc2, sh2, g2, wo, wrt, nbb, ls)
    dest3, wts, yb = _moe(h2, lg, router_bias[0], w_exp_gate[0], w_exp_up[0], w_exp_down[0], 512, nbb * ls)
    y_sample = _final(dest3, x1, h2, wts, yb, gt2, scf, shf, gf, wgu, wsd, nbb, ls)
    k_sample = k.reshape(1, bs, ls, N_HEADS, HEAD_DIM)
    v_sample = v.reshape(1, bs, ls, N_HEADS, HEAD_DIM)
    logf_sample = lf[:, :N_HEADS].reshape(1, bs, ls, N_HEADS)
    sre_s, sim_s = split_state(ht)

    return (y_prompt, y_sample, k_prompt, v_prompt, logf_prompt, sre_p, sim_p,
            k_sample, v_sample, logf_sample, sre_s, sim_s)
```

```python
import functools
import math

import jax
import jax.numpy as jnp
import numpy as np
from jax import lax
from jax.experimental import pallas as pl
from jax.experimental.pallas import tpu as pltpu

F32 = jnp.float32
BF16 = jnp.bfloat16
I32 = jnp.int32

EPS = 1e-6
HEAD_DIM = 64
N_HEADS = 8
D_SSM = 512
D_ATTN = 512
SSM_GROUP = 16
SSM_STATE = 64
N_EXPERTS = 64
TOP_K = 6
N_EXPERT_GROUPS = 8
TOPK_GROUPS = 4
ROUTED_SCALE = 2.5
PAGE = 128

LANES = 128
SUBLANES = 8
VMEM_LIMIT = 48 * 1024 * 1024
MOE_ROWS = 256
ATTN_TILE = 256
LOG2E = math.log2(math.e)
NEG_INF = float("-inf")


def _cparams(sem):
    return pltpu.CompilerParams(dimension_semantics=sem, vmem_limit_bytes=VMEM_LIMIT)


def _bdot(a, b):
    return jnp.dot(a.astype(BF16), b.astype(BF16), preferred_element_type=F32)


def _bdot_nt(a, b):
    return lax.dot_general(a.astype(BF16), b.astype(BF16), (((1,), (1,)), ((), ())),
                           preferred_element_type=F32)


def _split3(v):
    hi = v.astype(BF16)
    r1 = v - hi.astype(F32)
    mid = r1.astype(BF16)
    lo = (r1 - mid.astype(F32)).astype(BF16)
    return hi, mid, lo


def _rms(x, g):
    return x * lax.rsqrt(jnp.mean(x * x, axis=-1, keepdims=True) + EPS) * g


def _adaln_kernel(c_ref, w_ref, b_ref, o_ref):
    c = c_ref[...]
    s = c * jax.nn.sigmoid(c)
    o_ref[...] = _bdot(s, w_ref[...]) + b_ref[...]


def _adaln(c, w, b):
    m, k = c.shape
    n = w.shape[1]
    tn = 1024
    return pl.pallas_call(
        _adaln_kernel,
        grid=(n // tn,),
        in_specs=[pl.BlockSpec((m, k), lambda j: (0, 0)),
                  pl.BlockSpec((k, tn), lambda j: (0, j)),
                  pl.BlockSpec((1, tn), lambda j: (0, j))],
        out_specs=pl.BlockSpec((m, tn), lambda j: (0, j)),
        out_shape=jax.ShapeDtypeStruct((m, n), F32),
        compiler_params=_cparams(("arbitrary",)),
    )(c, w, b.reshape(1, n))


def _inproj_kernel(x_ref, sc_ref, sh_ref, g_ref, w_ref, wqv_ref, bf_ref, *out_refs, transposed, q_scale):
    nbb, rb, d = x_ref.shape
    rows = nbb * rb
    x = x_ref[...]
    h = _rms(x, g_ref[...]) * (1.0 + sc_ref[...]) + sh_ref[...]
    hb = h.reshape(rows, d).astype(BF16)
    proj = jnp.dot(hb, w_ref[...], preferred_element_type=F32)
    u_ref, lf_ref = out_refs[:2]
    u_ref[...] = proj[:, :D_SSM]
    k = proj[:, D_SSM:D_SSM + D_ATTN]
    z = proj[:, proj.shape[1] - LANES:] + bf_ref[...]
    lf_ref[...] = jnp.minimum(z, 0.0) - jnp.log1p(jnp.exp(-jnp.abs(z)))
    nt = (((1,), (1,)), ((), ()))
    if transposed:
        kb_ref, qt_ref, vt_ref, ktf_ref, vtf_ref = out_refs[2:]
        kb_ref[...] = k.astype(BF16)
        qt = (lax.dot_general(wqv_ref[0], hb, nt, preferred_element_type=F32) * q_scale).astype(BF16)
        ktf_ref[0] = lax.dot_general(wqv_ref[1], hb, nt, preferred_element_type=F32)
        vt = lax.dot_general(wqv_ref[2], hb, nt, preferred_element_type=F32)
        vtf_ref[0] = vt
        vt = vt.astype(BF16)
        for c in range(rows // ATTN_TILE):
            qt_ref[c] = qt[:, c * ATTN_TILE:(c + 1) * ATTN_TILE]
            vt_ref[c] = vt[:, c * ATTN_TILE:(c + 1) * ATTN_TILE]
    else:
        k_ref, v_ref, q_ref = out_refs[2:]
        k_ref[...] = k
        v_ref[...] = proj[:, D_SSM + D_ATTN:D_SSM + 2 * D_ATTN]
        q_ref[...] = lax.dot_general(hb, wqv_ref[0], nt, preferred_element_type=F32) * q_scale


def _inproj(x3, sc, sh, g, w_main, w_qv, b_f, nbb, rb, transposed, q_scale):
    nb, r, d = x3.shape
    t = nb * r
    rows = nbb * rb
    nj = r // rb
    grid = (nb // nbb, nj)
    row_map = lambda i, j: (i * nj + j, 0)
    n_main = w_main.shape[1]
    outs = [jax.ShapeDtypeStruct((t, D_SSM), F32), jax.ShapeDtypeStruct((t, LANES), F32)]
    out_specs = [pl.BlockSpec((rows, D_SSM), row_map), pl.BlockSpec((rows, LANES), row_map)]
    if transposed:
        assert nbb == 1, "transposed outputs are laid out per sequence"
        nc = rows // ATTN_TILE
        chunk_map = lambda i, j: (i * nj + j, 0, 0)
        seq_map = lambda i, j: (i, 0, j)
        outs += [jax.ShapeDtypeStruct((t, D_ATTN), BF16),
                 jax.ShapeDtypeStruct((t // ATTN_TILE, D_ATTN, ATTN_TILE), BF16),
                 jax.ShapeDtypeStruct((t // ATTN_TILE, D_ATTN, ATTN_TILE), BF16),
                 jax.ShapeDtypeStruct((nb, D_ATTN, r), F32),
                 jax.ShapeDtypeStruct((nb, D_ATTN, r), F32)]
        out_specs += [pl.BlockSpec((rows, D_ATTN), row_map),
                      pl.BlockSpec((nc, D_ATTN, ATTN_TILE), chunk_map),
                      pl.BlockSpec((nc, D_ATTN, ATTN_TILE), chunk_map),
                      pl.BlockSpec((1, D_ATTN, rows), seq_map),
                      pl.BlockSpec((1, D_ATTN, rows), seq_map)]
    else:
        outs += [jax.ShapeDtypeStruct((t, D_ATTN), F32)] * 3
        out_specs += [pl.BlockSpec((rows, D_ATTN), row_map)] * 3
    return pl.pallas_call(
        functools.partial(_inproj_kernel, transposed=transposed, q_scale=q_scale),
        grid=grid,
        in_specs=[pl.BlockSpec((nbb, rb, d), lambda i, j: (i, j, 0)),
                  pl.BlockSpec((nbb, 1, d), lambda i, j: (i, 0, 0)),
                  pl.BlockSpec((nbb, 1, d), lambda i, j: (i, 0, 0)),
                  pl.BlockSpec((1, 1, d), lambda i, j: (0, 0, 0)),
                  pl.BlockSpec((d, n_main), lambda i, j: (0, 0)),
                  pl.BlockSpec((3, D_ATTN, d), lambda i, j: (0, 0, 0)),
                  pl.BlockSpec((1, LANES), lambda i, j: (0, 0))],
        out_specs=out_specs,
        out_shape=outs,
        compiler_params=_cparams(("arbitrary", "arbitrary")),
    )(x3, sc, sh, g, w_main, w_qv, b_f)


def _crep_kernel(lf_ref, tri_ref, ex_ref, o_ref, carry_ref, *, scale):
    j = pl.program_id(1)

    @pl.when(j == 0)
    def _():
        carry_ref[...] = jnp.zeros_like(carry_ref)

    tri = tri_ref[...]
    ex = ex_ref[...]
    cs = sum(jnp.dot(tri, p, preferred_element_type=F32) for p in _split3(lf_ref[...]))
    rep = sum(jnp.dot(p, ex, preferred_element_type=F32) for p in _split3(cs))
    rep = rep + carry_ref[...]
    rows = rep.shape[0]
    carry_ref[...] = rep[rows - 1:rows, :]
    for h in range(N_HEADS):
        o_ref[0, h] = rep[:, h * LANES:(h + 1) * LANES] * (-scale)


def _crep(lf128, nb, length, scale):
    rows = ATTN_TILE
    nj = length // rows
    tri = np.tril(np.ones((rows, rows), np.float32))
    ex = np.zeros((LANES, N_HEADS * LANES), np.float32)
    for h in range(N_HEADS):
        ex[h, h * LANES:(h + 1) * LANES] = 1.0
    return pl.pallas_call(
        functools.partial(_crep_kernel, scale=scale),
        grid=(nb, nj),
        in_specs=[pl.BlockSpec((rows, LANES), lambda i, j: (i * nj + j, 0)),
                  pl.BlockSpec((rows, rows), lambda i, j: (0, 0)),
                  pl.BlockSpec((LANES, N_HEADS * LANES), lambda i, j: (0, 0))],
        out_specs=pl.BlockSpec((1, N_HEADS, rows, LANES), lambda i, j: (i, 0, j, 0)),
        out_shape=jax.ShapeDtypeStruct((nb, N_HEADS, length, LANES), F32),
        scratch_shapes=[pltpu.VMEM((1, N_HEADS * LANES), F32)],
        compiler_params=_cparams(("arbitrary", "arbitrary")),
    )(lf128, jnp.asarray(tri, BF16), jnp.asarray(ex, BF16))


def _cumsum_kernel(x_ref, tri_ref, lt_ref, o_ref):
    tri = tri_ref[...]
    lt = lt_ref[...]
    cs = sum(jnp.dot(p, tri, preferred_element_type=F32) for p in _split3(x_ref[...]))
    tot = jnp.broadcast_to(cs[:, LANES - 1:LANES], cs.shape)
    carry = sum(jnp.dot(lt, p, preferred_element_type=F32) for p in _split3(tot))
    o_ref[...] = -(cs + carry)


def _neg_cumsum(x2, gsz, rb):
    r = x2.shape[0]
    tri = np.triu(np.ones((LANES, LANES), np.float32))
    ii = np.arange(rb)
    lt = ((ii[:, None] // gsz == ii[None, :] // gsz) & (ii[None, :] < ii[:, None])).astype(np.float32)
    return pl.pallas_call(
        _cumsum_kernel,
        grid=(r // rb,),
        in_specs=[pl.BlockSpec((rb, LANES), lambda i: (i, 0)),
                  pl.BlockSpec((LANES, LANES), lambda i: (0, 0)),
                  pl.BlockSpec((rb, rb), lambda i: (0, 0))],
        out_specs=pl.BlockSpec((rb, LANES), lambda i: (i, 0)),
        out_shape=jax.ShapeDtypeStruct((r, LANES), F32),
        compiler_params=_cparams(("arbitrary",)),
    )(x2, jnp.asarray(tri, BF16), jnp.asarray(lt, BF16))


def _gelu_tanh(x):
    return 0.5 * x * (1.0 + jnp.tanh(math.sqrt(2.0 / math.pi) * (x + 0.044715 * (x * x * x))))


def _s5_kernel(u_ref, h0_ref, wb_ref, wc_ref, are_ref, aim_ref, dsk_ref, wglu_ref, bglu_ref, gout_ref,
               y_ref, ht_ref, s_ref, hc_ref, *, nseq, tm):
    rows = nseq * tm
    sr = rows + SUBLANES
    ti = pl.program_id(1)

    @pl.when(ti == 0)
    def _():
        hc_ref[...] = h0_ref[...]

    u = u_ref[...].reshape(rows, D_SSM)
    ub = u.astype(BF16)
    for c in range(4):
        bu = jnp.dot(ub[:, c * LANES:(c + 1) * LANES], wb_ref[c], preferred_element_type=F32)
        for jj in range(4):
            s_ref[pl.ds((4 * c + jj) * sr, rows), :] = bu[:, jj * LANES:(jj + 1) * LANES]
            s_ref[pl.ds((16 + 4 * c + jj) * sr, rows), :] = bu[:, 512 + jj * LANES:512 + (jj + 1) * LANES]

    ar = (are_ref[0:8, :], are_ref[8:16, :])
    ai = (aim_ref[0:8, :], aim_ref[8:16, :])

    def seq_group(sg, carry):
        base = sg * 4
        hs = []
        for b in range(4):
            hs.append(tuple(hc_ref[base + b, pl.ds(8 * q, 8), :] for q in range(4)))

        def step(t, hs):
            new = []
            for b in range(4):
                row = (base + b) * tm + t
                hr0, hr1, hi0, hi1 = hs[b]
                bre0 = s_ref[pl.ds(row, 8, stride=sr), :]
                bre1 = s_ref[pl.ds(8 * sr + row, 8, stride=sr), :]
                bim0 = s_ref[pl.ds(16 * sr + row, 8, stride=sr), :]
                bim1 = s_ref[pl.ds(24 * sr + row, 8, stride=sr), :]
                nr0 = ar[0] * hr0 - ai[0] * hi0 + bre0
                nr1 = ar[1] * hr1 - ai[1] * hi1 + bre1
                ni0 = ar[0] * hi0 + ai[0] * hr0 + bim0
                ni1 = ar[1] * hi1 + ai[1] * hr1 + bim1
                s_ref[pl.ds(row, 8, stride=sr), :] = nr0
                s_ref[pl.ds(8 * sr + row, 8, stride=sr), :] = nr1
                s_ref[pl.ds(16 * sr + row, 8, stride=sr), :] = ni0
                s_ref[pl.ds(24 * sr + row, 8, stride=sr), :] = ni1
                new.append((nr0, nr1, ni0, ni1))
            return tuple(new)

        hs = lax.fori_loop(0, tm, step, tuple(hs))
        for b in range(4):
            for q in range(4):
                hc_ref[base + b, pl.ds(8 * q, 8), :] = hs[b][q]
        return carry

    lax.fori_loop(0, nseq // 4, seq_group, 0)
    ht_ref[...] = hc_ref[...]

    ys = []
    for c in range(4):
        blocks = [s_ref[pl.ds((4 * c + jj) * sr, rows), :].astype(BF16) for jj in range(4)]
        blocks += [s_ref[pl.ds((16 + 4 * c + jj) * sr, rows), :].astype(BF16) for jj in range(4)]
        hcat = jnp.concatenate(blocks, axis=1)
        ys.append(jnp.dot(hcat, wc_ref[c], preferred_element_type=F32))
    y = jnp.concatenate(ys, axis=1) + dsk_ref[...] * u
    y = _gelu_tanh(y)
    gate = jax.nn.sigmoid(jnp.dot(y.astype(BF16), wglu_ref[...], preferred_element_type=F32) + bglu_ref[...])
    y = y * gate
    y_ref[...] = _rms(y, gout_ref[...]).astype(BF16).reshape(y_ref.shape)


def _s5(u3, h0, wb, wc, a_re, a_im, dsk, wglu, bglu, gout, nseq, tm):
    nb, length, _ = u3.shape
    rows = nseq * tm
    sr = rows + SUBLANES
    grid = (nb // nseq, length // tm)
    const2 = lambda i, j: (0, 0)
    const3 = lambda i, j: (0, 0, 0)
    if tm == length:
        y_spec = pl.BlockSpec((rows, D_SSM), lambda i, j: (i, 0))
        y_shape = jax.ShapeDtypeStruct((nb * length, D_SSM), BF16)
    else:
        y_spec = pl.BlockSpec((nseq, tm, D_SSM), lambda i, j: (i, j, 0))
        y_shape = jax.ShapeDtypeStruct((nb, length, D_SSM), BF16)
    return pl.pallas_call(
        functools.partial(_s5_kernel, nseq=nseq, tm=tm),
        grid=grid,
        in_specs=[pl.BlockSpec((nseq, tm, D_SSM), lambda i, j: (i, j, 0)),
                  pl.BlockSpec((nseq, 32, LANES), lambda i, j: (i, 0, 0)),
                  pl.BlockSpec((4, LANES, 1024), const3),
                  pl.BlockSpec((4, 1024, LANES), const3),
                  pl.BlockSpec((16, LANES), const2),
                  pl.BlockSpec((16, LANES), const2),
                  pl.BlockSpec((1, D_SSM), const2),
                  pl.BlockSpec((D_SSM, D_SSM), const2),
                  pl.BlockSpec((1, D_SSM), const2),
                  pl.BlockSpec((1, D_SSM), const2)],
        out_specs=[y_spec, pl.BlockSpec((nseq, 32, LANES), lambda i, j: (i, 0, 0))],
        out_shape=[y_shape, jax.ShapeDtypeStruct((nb, 32, LANES), F32)],
        scratch_shapes=[pltpu.VMEM((32 * sr, LANES), F32), pltpu.VMEM((nseq, 32, LANES), F32)],
        compiler_params=_cparams(("arbitrary", "arbitrary")),
    )(u3, h0, wb, wc, a_re, a_im, dsk, wglu, bglu, gout)


def _s5_params(lam_re, lam_im, log_step, b_re, b_im, c_re, c_im):
    g = lam_re.shape[0]
    dt = jnp.exp(log_step)[:, None]
    mag = jnp.exp(lam_re * dt)
    a_re = mag * jnp.cos(lam_im * dt)
    a_im = mag * jnp.sin(lam_im * dt)
    den = lam_re * lam_re + lam_im * lam_im
    n_re = a_re - 1.0
    f_re = (n_re * lam_re + a_im * lam_im) / den
    f_im = (a_im * lam_re - n_re * lam_im) / den
    bb_re = f_re[..., None] * b_re - f_im[..., None] * b_im
    bb_im = f_re[..., None] * b_im + f_im[..., None] * b_re
    eye = jnp.eye(g, dtype=F32)
    n_state = g * SSM_STATE

    def in_mat(bb):
        return jnp.einsum('gpi,gh->gihp', bb, eye).reshape(g * SSM_GROUP, n_state)

    def out_mat(cc):
        return jnp.einsum('gip,gh->gphi', cc, eye).reshape(n_state, g * SSM_GROUP)

    wbr, wbi = in_mat(bb_re), in_mat(bb_im)
    wcr, wci = out_mat(c_re), out_mat(-c_im)
    wb = jnp.stack([jnp.concatenate([wbr[c * 128:(c + 1) * 128, c * 512:(c + 1) * 512],
                                     wbi[c * 128:(c + 1) * 128, c * 512:(c + 1) * 512]], axis=1)
                    for c in range(4)]).astype(BF16)
    wc = jnp.stack([jnp.concatenate([wcr[c * 512:(c + 1) * 512, c * 128:(c + 1) * 128],
                                     wci[c * 512:(c + 1) * 512, c * 128:(c + 1) * 128]], axis=0)
                    for c in range(4)]).astype(BF16)
    return wb, wc, a_re.reshape(16, LANES), a_im.reshape(16, LANES)


def _attn_kernel(qt_ref, k_ref, vt_ref, cr_ref, g_ref, o_ref, qz_ref, m_ref, l_ref, acc_ref, st_ref):
    tq = ATTN_TILE
    qi = pl.program_id(1)
    half = LANES // 2

    rowid = lax.broadcasted_iota(I32, (LANES, tq), 0)
    for j in range(N_HEADS // 2):
        qp = qt_ref[0, j * LANES:(j + 1) * LANES, :]
        qz_ref[2 * j] = jnp.where(rowid < half, qp, jnp.zeros_like(qp))
        qz_ref[2 * j + 1] = jnp.where(rowid >= half, qp, jnp.zeros_like(qp))
    m_ref[...] = jnp.full_like(m_ref, NEG_INF)
    l_ref[...] = jnp.zeros_like(l_ref)
    acc_ref[...] = jnp.zeros_like(acc_ref)

    key_row = lax.broadcasted_iota(I32, (tq, tq), 0)
    q_col = lax.broadcasted_iota(I32, (tq, tq), 1)
    causal = key_row <= q_col

    def tile(kb, masked):
        ks = pl.multiple_of(kb * tq, tq)
        m_new = []
        for j in range(N_HEADS // 2):
            kp = k_ref[0, pl.ds(ks, tq), j * LANES:(j + 1) * LANES]
            for e in range(2):
                h = 2 * j + e
                st = jnp.dot(kp, qz_ref[h], preferred_element_type=F32)
                bias = cr_ref[0, h, pl.ds(ks, tq), :]
                st = st + jnp.concatenate([bias] * (tq // LANES), axis=1)
                if masked:
                    st = jnp.where(causal, st, NEG_INF)
                st_ref[h] = st
                m_new.append(jnp.maximum(m_ref[h:h + 1, :], jnp.max(st, axis=0, keepdims=True)))
        for h in range(N_HEADS):
            alpha = jnp.exp2(m_ref[h:h + 1, :] - m_new[h])
            p = jnp.exp2(st_ref[h] - m_new[h])
            l_ref[h:h + 1, :] = alpha * l_ref[h:h + 1, :] + jnp.sum(p, axis=0, keepdims=True)
            m_ref[h:h + 1, :] = m_new[h]
            rows = slice(h * HEAD_DIM, (h + 1) * HEAD_DIM)
            pv = jnp.dot(vt_ref[kb, rows, :], p.astype(BF16), preferred_element_type=F32)
            acc_ref[rows, :] = alpha * acc_ref[rows, :] + pv

    def body(kb, c):
        tile(kb, False)
        return c

    lax.fori_loop(0, qi, body, 0)
    tile(qi, True)

    for h in range(N_HEADS):
        rows = slice(h * HEAD_DIM, (h + 1) * HEAD_DIM)
        acc_ref[rows, :] = acc_ref[rows, :] / l_ref[h:h + 1, :]
    o = acc_ref[...].T
    o_ref[0] = _rms(o, g_ref[...]).astype(BF16)


def _attn_prompt(qt, k, vt, crep, g):
    b, length, d = k.shape
    tq = ATTN_TILE
    nq = length // tq
    once = pl.Buffered(1)
    return pl.pallas_call(
        _attn_kernel,
        grid=(b, nq),
        in_specs=[pl.BlockSpec((1, d, tq), lambda i, j: (i * nq + j, 0, 0)),
                  pl.BlockSpec((1, length, d), lambda i, j: (i, 0, 0), pipeline_mode=once),
                  pl.BlockSpec((nq, d, tq), lambda i, j: (i, 0, 0), pipeline_mode=once),
                  pl.BlockSpec((1, N_HEADS, length, LANES), lambda i, j: (i, 0, 0, 0), pipeline_mode=once),
                  pl.BlockSpec((1, d), lambda i, j: (0, 0))],
        out_specs=pl.BlockSpec((1, tq, d), lambda i, j: (i, j, 0)),
        out_shape=jax.ShapeDtypeStruct((b, length, d), BF16),
        scratch_shapes=[pltpu.VMEM((N_HEADS, LANES, tq), BF16), pltpu.VMEM((N_HEADS, tq), F32),
                        pltpu.VMEM((N_HEADS, tq), F32), pltpu.VMEM((d, tq), F32),
                        pltpu.VMEM((N_HEADS, tq, tq), F32)],
        compiler_params=_cparams(("arbitrary", "arbitrary")),
    )(qt, k, vt, crep, g)


def _attn_paged_kernel(pt_ref, q_ref, kn_ref, vn_ref, cn_ref, g_ref, kc_hbm, vc_hbm, o_ref,
                       kbuf, vbuf, sem, *, n_pages, n_new):
    i = pl.program_id(0)
    n = pl.num_programs(0)
    n_past = n_pages * PAGE

    def page_copies(seq, slot):
        cps = []
        for p in range(n_pages):
            pg = pt_ref[seq * n_pages + p]
            cps.append(pltpu.make_async_copy(kc_hbm.at[pg], kbuf.at[slot, p], sem.at[0, slot]))
            cps.append(pltpu.make_async_copy(vc_hbm.at[pg], vbuf.at[slot, p], sem.at[1, slot]))
        return cps

    @pl.when(i == 0)
    def _():
        for cp in page_copies(0, 0):
            cp.start()

    @pl.when(i + 1 < n)
    def _():
        for cp in page_copies(i + 1, (i + 1) % 2):
            cp.start()

    slot = i % 2
    for cp in page_copies(i, slot):
        cp.wait()

    d = q_ref.shape[-1]
    nr = N_HEADS * n_new
    new_bits = n_new.bit_length() - 1
    head_bits = HEAD_DIM.bit_length() - 1
    rowh = lax.shift_right_logical(lax.broadcasted_iota(I32, (nr, d), 0), new_bits)
    colh = lax.shift_right_logical(lax.broadcasted_iota(I32, (nr, d), 1), head_bits)
    bd = rowh == colh
    qrep = jnp.broadcast_to(q_ref[0][None], (N_HEADS, n_new, d)).reshape(nr, d)
    qbd = jnp.where(bd, qrep, 0.0).astype(BF16)
    cn = cn_ref[0]
    cnr = jnp.broadcast_to(cn[:, None, :], (N_HEADS, n_new, cn.shape[-1])).reshape(nr, cn.shape[-1])

    s_p = []
    for p in range(n_pages):
        kt = kbuf[slot, p].reshape(d, PAGE)
        s_p.append(_bdot(qbd, kt) + cnr[:, p * PAGE:(p + 1) * PAGE])
    s_n = _bdot_nt(qbd, kn_ref[0]) + cnr[:, n_past:n_past + n_new]
    qpos = lax.broadcasted_iota(I32, (nr, n_new), 0) & (n_new - 1)
    kpos = lax.broadcasted_iota(I32, (nr, n_new), 1)
    s_n = jnp.where(kpos <= qpos, s_n, NEG_INF)
    m = jnp.max(s_n, axis=1, keepdims=True)
    for sp in s_p:
        m = jnp.maximum(m, jnp.max(sp, axis=1, keepdims=True))
    p_n = jnp.exp(s_n - m)
    l = jnp.sum(p_n, axis=1, keepdims=True)
    of = _bdot(p_n, vn_ref[0])
    for p in range(n_pages):
        pp = jnp.exp(s_p[p] - m)
        l = l + jnp.sum(pp, axis=1, keepdims=True)
        of = of + _bdot_nt(pp, vbuf[slot, p].reshape(d, PAGE))
    of = jnp.where(bd, of / l, 0.0)
    o = jnp.sum(of.reshape(N_HEADS, n_new, d), axis=0)
    o_ref[0] = _rms(o, g_ref[...])


def _attn_paged(page_table, q, kn, vn, cn, g, cache_k, cache_v):
    nseq, n_new, d = q.shape
    assert n_new & (n_new - 1) == 0, "new-token count must be a power of two"
    n_pages = page_table.shape[1]
    n_past = n_pages * PAGE
    grid_spec = pltpu.PrefetchScalarGridSpec(
        num_scalar_prefetch=1,
        grid=(nseq,),
        in_specs=[pl.BlockSpec((1, n_new, d), lambda i, pt: (i, 0, 0)),
                  pl.BlockSpec((1, n_new, d), lambda i, pt: (i, 0, 0)),
                  pl.BlockSpec((1, n_new, d), lambda i, pt: (i, 0, 0)),
                  pl.BlockSpec((1, N_HEADS, cn.shape[-1]), lambda i, pt: (i, 0, 0)),
                  pl.BlockSpec((1, d), lambda i, pt: (0, 0)),
                  pl.BlockSpec(memory_space=pl.ANY),
                  pl.BlockSpec(memory_space=pl.ANY)],
        out_specs=pl.BlockSpec((1, n_new, d), lambda i, pt: (i, 0, 0)),
        scratch_shapes=[pltpu.VMEM((2, n_pages, N_HEADS, HEAD_DIM, PAGE), F32),
                        pltpu.VMEM((2, n_pages, N_HEADS, HEAD_DIM, PAGE), F32),
                        pltpu.SemaphoreType.DMA((2, 2))],
    )
    return pl.pallas_call(
        functools.partial(_attn_paged_kernel, n_pages=n_pages, n_new=n_new),
        grid_spec=grid_spec,
        out_shape=jax.ShapeDtypeStruct((nseq, n_new, d), F32),
        compiler_params=_cparams(("arbitrary",)),
    )(page_table.reshape(-1), q, kn, vn, cn, g, cache_k, cache_v)


def _outproj_kernel(x_ref, ys_ref, oa_ref, gt_ref, sc_ref, sh_ref, g_ref, wo_ref, wrt_ref,
                    x1_ref, h2_ref, lg_ref):
    nbb, rb, d = x_ref.shape
    mix = jnp.dot(ys_ref[...], wo_ref[0:D_SSM, :], preferred_element_type=F32)
    mix = mix + jnp.dot(oa_ref[...], wo_ref[D_SSM:, :], preferred_element_type=F32)
    x1 = x_ref[...] + gt_ref[...] * mix.reshape(nbb, rb, d)
    x1_ref[...] = x1
    h2 = (_rms(x1, g_ref[...]) * (1.0 + sc_ref[...]) + sh_ref[...]).reshape(nbb * rb, d)
    hb = h2.astype(BF16)
    h2_ref[...] = h2
    hlo = (h2 - hb.astype(F32)).astype(BF16)
    whi = wrt_ref[0]
    wlo = wrt_ref[1]
    nt = (((1,), (1,)), ((), ()))
    lg = lax.dot_general(whi, hb, nt, preferred_element_type=F32)
    lg = lg + lax.dot_general(wlo, hb, nt, preferred_element_type=F32)
    lg = lg + lax.dot_general(whi, hlo, nt, preferred_element_type=F32)
    lg_ref[...] = lg


def _outproj(x3, ys, oa, gt, sc, sh, g, wo, wrt, nbb, rb):
    nb, r, d = x3.shape
    t = nb * r
    rows = nbb * rb
    nj = r // rb
    row_map = lambda i, j: (i * nj + j, 0)
    mod_spec = pl.BlockSpec((nbb, 1, d), lambda i, j: (i, 0, 0))
    return pl.pallas_call(
        _outproj_kernel,
        grid=(nb // nbb, nj),
        in_specs=[pl.BlockSpec((nbb, rb, d), lambda i, j: (i, j, 0)),
                  pl.BlockSpec((rows, D_SSM), row_map),
                  pl.BlockSpec((rows, D_ATTN), row_map),
                  mod_spec, mod_spec, mod_spec,
                  pl.BlockSpec((1, 1, d), lambda i, j: (0, 0, 0)),
                  pl.BlockSpec((D_SSM + D_ATTN, d), lambda i, j: (0, 0)),
                  pl.BlockSpec((2, N_EXPERTS, d), lambda i, j: (0, 0, 0))],
        out_specs=[pl.BlockSpec((nbb, rb, d), lambda i, j: (i, j, 0)),
                   pl.BlockSpec((rows, d), row_map),
                   pl.BlockSpec((N_EXPERTS, rows), lambda i, j: (0, i * nj + j))],
        out_shape=[jax.ShapeDtypeStruct((nb, r, d), F32),
                   jax.ShapeDtypeStruct((t, d), F32),
                   jax.ShapeDtypeStruct((N_EXPERTS, t), F32)],
        compiler_params=_cparams(("arbitrary", "arbitrary")),
    )(x3, ys, oa, gt, sc, sh, g, wo, wrt)


def _router_kernel(lg_ref, rb_ref, ut_ref, idx_ref, w_ref, rank_ref, cnt_ref, carry_ref):
    i = pl.program_id(0)
    tm = lg_ref.shape[1]
    per_group = N_EXPERTS // N_EXPERT_GROUPS

    @pl.when(i == 0)
    def _():
        carry_ref[...] = jnp.zeros_like(carry_ref)

    scores = jax.nn.sigmoid(lg_ref[...])
    biased = scores + rb_ref[...]
    blks, grp = [], []
    for g in range(N_EXPERT_GROUPS):
        blk = biased[g * per_group:(g + 1) * per_group, :]
        m1 = jnp.max(blk, axis=0, keepdims=True)
        eq = blk == m1
        n_eq = jnp.sum(jnp.where(eq, 1.0, 0.0), axis=0, keepdims=True)
        m2 = jnp.max(jnp.where(eq, NEG_INF, blk), axis=0, keepdims=True)
        blks.append(blk)
        grp.append(m1 + jnp.where(n_eq >= 2.0, m1, m2))
    masked = []
    for g in range(N_EXPERT_GROUPS):
        beaten = jnp.zeros((1, tm), F32)
        for o in range(N_EXPERT_GROUPS):
            if o == g:
                continue
            ahead = (grp[o] >= grp[g]) if o < g else (grp[o] > grp[g])
            beaten = beaten + jnp.where(ahead, 1.0, 0.0)
        masked.append(jnp.where(beaten < float(TOPK_GROUPS), blks[g], NEG_INF))
    work = jnp.concatenate(masked, axis=0)

    eid = lax.broadcasted_iota(I32, (N_EXPERTS, tm), 0)
    chosen = jnp.zeros((N_EXPERTS, tm), F32)
    idxs, ws, sels = [], [], []
    for _ in range(TOP_K):
        m = jnp.max(work, axis=0, keepdims=True)
        first = jnp.min(jnp.where(work == m, eid, N_EXPERTS), axis=0, keepdims=True)
        sel = eid == first
        idxs.append(first)
        ws.append(jnp.sum(jnp.where(sel, scores, 0.0), axis=0, keepdims=True))
        sels.append(sel)
        chosen = jnp.where(sel, 1.0, chosen)
        work = jnp.where(sel, NEG_INF, work)
    wsum = ws[0]
    for wk in ws[1:]:
        wsum = wsum + wk

    prefix = jnp.dot(chosen.astype(BF16), ut_ref[...], preferred_element_type=F32) + carry_ref[...]
    carry_ref[...] = carry_ref[...] + jnp.sum(chosen, axis=1, keepdims=True)
    cnt_ref[...] = carry_ref[...]

    idx_ref[...] = jnp.zeros_like(idx_ref)
    w_ref[...] = jnp.zeros_like(w_ref)
    rank_ref[...] = jnp.zeros_like(rank_ref)
    for k in range(TOP_K):
        idx_ref[k:k + 1, :] = idxs[k]
        w_ref[k:k + 1, :] = ws[k] / wsum * ROUTED_SCALE
        rank = jnp.sum(jnp.where(sels[k], prefix, 0.0), axis=0, keepdims=True)
        rank_ref[k:k + 1, :] = rank.astype(I32)


def _router(lg, router_bias, tm):
    e, t = lg.shape
    ut = np.triu(np.ones((tm, tm), np.float32), 1)
    tok_spec = pl.BlockSpec((SUBLANES, tm), lambda i: (0, i))
    return pl.pallas_call(
        _router_kernel,
        grid=(t // tm,),
        in_specs=[pl.BlockSpec((e, tm), lambda i: (0, i)),
                  pl.BlockSpec((e, 1), lambda i: (0, 0)),
                  pl.BlockSpec((tm, tm), lambda i: (0, 0))],
        out_specs=[tok_spec, tok_spec, tok_spec, pl.BlockSpec((e, 1), lambda i: (0, 0))],
        out_shape=[jax.ShapeDtypeStruct((SUBLANES, t), I32), jax.ShapeDtypeStruct((SUBLANES, t), F32),
                   jax.ShapeDtypeStruct((SUBLANES, t), I32), jax.ShapeDtypeStruct((e, 1), F32)],
        scratch_shapes=[pltpu.VMEM((e, 1), F32)],
        compiler_params=_cparams(("arbitrary",)),
    )(lg, router_bias.reshape(e, 1), jnp.asarray(ut, BF16))


def _experts_kernel(be_ref, nv_ref, new_ref, x_ref, wg_ref, wu_ref, wd_ref, o_ref, wgu_s, wd_s):
    i = pl.program_id(0)
    de = wg_ref.shape[-1]
    valid = i < nv_ref[0]

    @pl.when(valid & (new_ref[i] == 1))
    def _():
        wgu_s[:, :de] = wg_ref[0].astype(BF16)
        wgu_s[:, de:] = wu_ref[0].astype(BF16)
        wd_s[...] = wd_ref[0].astype(BF16)

    @pl.when(valid)
    def _():
        gu = jnp.dot(x_ref[...].astype(BF16), wgu_s[...], preferred_element_type=F32)
        g = gu[:, :de]
        a = (g * jax.nn.sigmoid(g) * gu[:, de:]).astype(BF16)
        o_ref[...] = jnp.dot(a, wd_s[...], preferred_element_type=F32)

    @pl.when(jnp.logical_not(valid))
    def _():
        o_ref[...] = jnp.zeros_like(o_ref)


def _experts(block_e, n_valid, block_new, xs, w_eg, w_eu, w_ed, m):
    n_rows, d = xs.shape
    de = w_eg.shape[-1]
    nblk = n_rows // m
    xmap = lambda i, be, nv, nw: (jnp.minimum(i, nv[0] - 1), 0)
    wmap = lambda i, be, nv, nw: (be[i], 0, 0)
    grid_spec = pltpu.PrefetchScalarGridSpec(
        num_scalar_prefetch=3,
        grid=(nblk,),
        in_specs=[pl.BlockSpec((m, d), xmap),
                  pl.BlockSpec((1, d, de), wmap),
                  pl.BlockSpec((1, d, de), wmap),
                  pl.BlockSpec((1, de, d), wmap)],
        out_specs=pl.BlockSpec((m, d), lambda i, be, nv, nw: (i, 0)),
        scratch_shapes=[pltpu.VMEM((d, 2 * de), BF16), pltpu.VMEM((de, d), BF16)],
    )
    return pl.pallas_call(
        _experts_kernel,
        grid_spec=grid_spec,
        out_shape=jax.ShapeDtypeStruct((n_rows, d), F32),
        compiler_params=_cparams(("arbitrary",)),
    )(block_e, n_valid, block_new, xs, w_eg, w_eu, w_ed)


def _row_copy(src, src_row, dst, dst_row, sem):
    return pltpu.make_async_copy(src.at[pl.ds(src_row, 1)], dst.at[pl.ds(dst_row, 1)], sem)


def _final_kernel(dest_ref, x1_ref, h2_ref, w_ref, gt_ref, sc_ref, sh_ref, g_ref, wgu_ref, wd_ref, yb_hbm,
                  y_ref, gbuf, sem):
    nbb, rb, d = x1_ref.shape
    rows = nbb * rb
    ds = wd_ref.shape[0]

    def issue(r, c):
        for k in range(TOP_K):
            _row_copy(yb_hbm, dest_ref[0, 0, r * TOP_K + k], gbuf.at[k], r, sem).start()
        return c

    lax.fori_loop(0, rows, issue, 0, unroll=2)

    gu = jnp.dot(h2_ref[...].astype(BF16), wgu_ref[...], preferred_element_type=F32)
    g = gu[:, :ds]
    a = (g * jax.nn.sigmoid(g) * gu[:, ds:]).astype(BF16)
    ff = jnp.dot(a, wd_ref[...], preferred_element_type=F32)

    for k in range(TOP_K):
        pltpu.make_async_copy(yb_hbm.at[pl.ds(0, rows)], gbuf.at[k], sem).wait()
    routed = gbuf[0] * w_ref[:, 0:1]
    for k in range(1, TOP_K):
        routed = routed + gbuf[k] * w_ref[:, k:k + 1]
    ff = routed + ff
    x2 = x1_ref[...] + gt_ref[...] * ff.reshape(nbb, rb, d)
    y_ref[...] = _rms(x2, g_ref[...]) * (1.0 + sc_ref[...]) + sh_ref[...]


def _final(dest3, x1, h2, wts, yb, gt, sc, sh, g, wgu, wd, nbb, rb):
    nb, r, d = x1.shape
    rows = nbb * rb
    nj = r // rb
    row_map = lambda i, j: (i * nj + j, 0)
    mod_spec = pl.BlockSpec((nbb, 1, d), lambda i, j: (i, 0, 0))
    return pl.pallas_call(
        _final_kernel,
        grid=(nb // nbb, nj),
        in_specs=[pl.BlockSpec((1, 1, rows * TOP_K), lambda i, j: (i * nj + j, 0, 0), memory_space=pltpu.SMEM),
                  pl.BlockSpec((nbb, rb, d), lambda i, j: (i, j, 0)),
                  pl.BlockSpec((rows, d), row_map),
                  pl.BlockSpec((rows, SUBLANES), row_map),
                  mod_spec, mod_spec, mod_spec,
                  pl.BlockSpec((1, 1, d), lambda i, j: (0, 0, 0)),
                  pl.BlockSpec(wgu.shape, lambda i, j: (0, 0)),
                  pl.BlockSpec(wd.shape, lambda i, j: (0, 0)),
                  pl.BlockSpec(memory_space=pl.ANY)],
        out_specs=pl.BlockSpec((nbb, rb, d), lambda i, j: (i, j, 0)),
        out_shape=jax.ShapeDtypeStruct((nb, r, d), F32),
        scratch_shapes=[pltpu.VMEM((TOP_K, rows, d), F32), pltpu.SemaphoreType.DMA(())],
        compiler_params=_cparams(("arbitrary", "arbitrary")),
    )(dest3, x1, h2, wts, gt, sc, sh, g, wgu, wd, yb)


def _dispatch_kernel(plo_ref, pln_ref, nv_ref, dest_ref, *rest, group_tiles, m):
    x_refs = rest[:len(group_tiles)]
    xs_hbm, sem, zsem, zbuf = rest[len(group_tiles):]
    rows = x_refs[0].shape[0]
    step = pl.program_id(0)

    nblk = xs_hbm.shape[0] // m

    def pad_copies(act):
        def single_rows(off, n):
            for r in range(SUBLANES - 1):
                @pl.when(r < n)
                def _(r=r):
                    act(_row_copy(zbuf, 0, xs_hbm, off + r, zsem))

        def per_expert(e, c):
            off = plo_ref[e]
            ln = pln_ref[e]
            head = jnp.minimum((-off) & (SUBLANES - 1), ln)
            single_rows(off, head)
            off = off + head
            ln = ln - head
            bit = m // 2
            while bit >= SUBLANES:
                has = (ln & bit) != 0

                @pl.when(has)
                def _(off=off, bit=bit):
                    dst = xs_hbm.at[pl.ds(pl.multiple_of(off, SUBLANES), bit)]
                    act(pltpu.make_async_copy(zbuf.at[pl.ds(0, bit)], dst, zsem))

                off = off + jnp.where(has, bit, 0)
                bit //= 2
            single_rows(off, ln & (SUBLANES - 1))
            return c

        lax.fori_loop(0, N_EXPERTS, per_expert, 0)

        def per_block(b, c):
            @pl.when(b >= nv_ref[0])
            def _():
                act(pltpu.make_async_copy(zbuf, xs_hbm.at[pl.ds(pl.multiple_of(b * m, m), m)], zsem))

            return c

        lax.fori_loop(0, nblk, per_block, 0)

    @pl.when(step == 0)
    def _():
        zbuf[...] = jnp.zeros_like(zbuf)
        pad_copies(lambda cp: cp.start())
        pad_copies(lambda cp: cp.wait())

    first = 0
    for x_ref, n_tiles in zip(x_refs, group_tiles):
        @pl.when((step >= first) & (step < first + n_tiles))
        def _(x_ref=x_ref):
            def issue(r, c):
                for k in range(TOP_K):
                    _row_copy(x_ref, r, xs_hbm, dest_ref[0, 0, r * TOP_K + k], sem).start()
                return c

            lax.fori_loop(0, rows, issue, 0, unroll=2)
            for k in range(TOP_K):
                pltpu.make_async_copy(x_ref, xs_hbm.at[pl.ds(0, rows)], sem).wait()

        first += n_tiles


def _dispatch(pad_lo, pad_len, n_valid, dest3, h2_groups, n_rows, m):
    d = h2_groups[0].shape[1]
    rows = dest3.shape[-1] // TOP_K
    group_tiles = tuple(h2.shape[0] // rows for h2 in h2_groups)
    in_specs = [pl.BlockSpec((1, 1, rows * TOP_K), lambda i, *_: (i, 0, 0), memory_space=pltpu.SMEM)]
    first = 0
    for n_tiles in group_tiles:
        tile_map = lambda i, *_, first=first, n_tiles=n_tiles: (jnp.clip(i - first, 0, n_tiles - 1), 0)
        in_specs.append(pl.BlockSpec((rows, d), tile_map))
        first += n_tiles
    grid_spec = pltpu.PrefetchScalarGridSpec(
        num_scalar_prefetch=3,
        grid=(sum(group_tiles),),
        in_specs=in_specs,
        out_specs=pl.BlockSpec(memory_space=pl.ANY),
        scratch_shapes=[pltpu.SemaphoreType.DMA(()), pltpu.SemaphoreType.DMA(()), pltpu.VMEM((m, d), F32)],
    )
    return pl.pallas_call(
        functools.partial(_dispatch_kernel, group_tiles=group_tiles, m=m),
        grid_spec=grid_spec,
        out_shape=jax.ShapeDtypeStruct((n_rows, d), F32),
        compiler_params=_cparams(("arbitrary",)),
    )(pad_lo, pad_len, n_valid, dest3, *h2_groups)


def _moe(h2_groups, lg, router_bias, w_eg, w_eu, w_ed, tile):
    t = lg.shape[1]
    e = N_EXPERTS
    m = MOE_ROWS
    idx_t, w_t, rank_t, counts = _router(lg, router_bias, tile)
    counts = counts.reshape(e).astype(I32)
    padded = ((counts + m - 1) // m) * m
    pad_end = jnp.cumsum(padded)
    pad_start = pad_end - padded
    onehot = idx_t[:, :, None] == jnp.arange(e, dtype=I32)
    dest_t = jnp.sum(jnp.where(onehot, pad_start, 0), axis=-1) + rank_t
    dest3 = dest_t[:TOP_K].T.reshape(t // tile, 1, tile * TOP_K)
    n_rows = (-(-(t * TOP_K) // m)) * m + e * m
    nblk = n_rows // m
    block_start = jnp.arange(nblk, dtype=I32) * m
    block_e = jnp.minimum(jnp.sum(pad_end[None, :] <= block_start[:, None], axis=1), e - 1).astype(I32)
    block_new = jnp.concatenate([jnp.ones((1,), I32), (block_e[1:] != block_e[:-1]).astype(I32)])
    n_valid = (pad_end[-1] // m).astype(I32).reshape(1)
    xs = _dispatch(pad_start + counts, padded - counts, n_valid, dest3, h2_groups, n_rows, m)
    yb = _experts(block_e, n_valid, block_new, xs, w_eg, w_eu, w_ed, m)
    return dest3, w_t.T, yb


def kernel(x_prompt, x_sample, c_prompt, c_sample, cache_k, cache_v, cache_logf, state_ssm_re, state_ssm_im, page_table, w_ada, b_ada, g_norm1, w_in, b_fgate, ssm_lambda_re, ssm_lambda_im, ssm_log_step, ssm_b_re, ssm_b_im, ssm_c_re, ssm_c_im, ssm_d, w_glu, b_glu, g_ssm_out, g_attn_out, w_out, g_norm2, w_router, router_bias, w_exp_gate, w_exp_up, w_exp_down, w_sh_gate, w_sh_up, w_sh_down, g_final, w_ada_final, b_ada_final):
    depth = w_ada.shape[0]
    assert depth == 1, "one layer is supported"
    bp, lp, d = x_prompt.shape
    bs, ls, _ = x_sample.shape
    n_pages = page_table.shape[1]
    n_past = n_pages * PAGE
    n_groups = ssm_lambda_re.shape[1]

    n_c = bp + bs
    n_c_pad = -(-n_c // SUBLANES) * SUBLANES
    c_all = jnp.concatenate([c_prompt, c_sample, jnp.zeros((n_c_pad - n_c, d), F32)], axis=0)
    mod = _adaln(c_all, w_ada[0], b_ada[0])
    modf = _adaln(c_all, w_ada_final, b_ada_final)

    def mods(lo, hi):
        parts = [mod[lo:hi, k * d:(k + 1) * d][:, None, :] for k in range(6)]
        parts += [modf[lo:hi, k * d:(k + 1) * d][:, None, :] for k in range(2)]
        return parts

    w_u, w_q, w_k, w_v, w_f = jnp.split(w_in[0], [D_SSM, D_SSM + D_ATTN, D_SSM + 2 * D_ATTN,
                                                   D_SSM + 3 * D_ATTN], axis=1)
    w_fpad = jnp.concatenate([w_f, jnp.zeros((d, LANES - N_HEADS), F32)], axis=1)
    w_main_s = jnp.concatenate([w_u, w_k, w_v, w_fpad], axis=1).astype(BF16)
    w_main_p = jnp.concatenate([w_u, w_k, w_fpad], axis=1).astype(BF16)
    w_qkv = jnp.stack([w_q.T, w_k.T, w_v.T]).astype(BF16)
    b_f = jnp.concatenate([b_fgate[0], jnp.zeros((LANES - N_HEADS,), F32)]).reshape(1, LANES)
    g1 = g_norm1[0].reshape(1, 1, d)
    g2 = g_norm2[0].reshape(1, 1, d)
    gf = g_final.reshape(1, 1, d)
    wb, wc, a_re, a_im = _s5_params(ssm_lambda_re[0], ssm_lambda_im[0], ssm_log_step[0], ssm_b_re[0],
                                    ssm_b_im[0], ssm_c_re[0], ssm_c_im[0])
    dsk = ssm_d[0].reshape(1, D_SSM)
    wglu = w_glu[0].astype(BF16)
    bglu = b_glu[0].reshape(1, D_SSM)
    g_so = g_ssm_out[0].reshape(1, D_SSM)
    g_ao = g_attn_out[0].reshape(1, D_ATTN)
    wo = w_out[0].astype(BF16)
    wr_t = w_router[0].T
    wr_hi = wr_t.astype(BF16)
    wrt = jnp.stack([wr_hi, (wr_t - wr_hi.astype(F32)).astype(BF16)])
    wgu = jnp.concatenate([w_sh_gate[0], w_sh_up[0]], axis=1).astype(BF16)
    wsd = w_sh_down[0].astype(BF16)

    def ssm_state(re, im):
        return jnp.concatenate([re.reshape(-1, 16, LANES), im.reshape(-1, 16, LANES)], axis=1)

    def split_state(ht):
        n = ht.shape[0]
        return (ht[:, :16].reshape(1, n, n_groups, SSM_STATE), ht[:, 16:].reshape(1, n, n_groups, SSM_STATE))

    tm = 512
    sh1, sc1, gt1, sh2, sc2, gt2_p, shf_p, scf_p = mods(0, bp)
    u, lf, kb, qt, vt, ktf, vtf = _inproj(x_prompt, sc1, sh1, g1, w_main_p, w_qkv, b_f, 1, tm, True,
                                          HEAD_DIM ** -0.5 * LOG2E)
    crep = _crep(lf, bp, lp, LOG2E)
    ys, ht = _s5(u.reshape(bp, lp, D_SSM), jnp.zeros((bp, 32, LANES), F32), wb, wc, a_re, a_im, dsk,
                 wglu, bglu, g_so, bp, 256)
    oa = _attn_prompt(qt, kb.reshape(bp, lp, D_ATTN), vt, crep, g_ao)
    x1_p, h2_p, lg_p = _outproj(x_prompt, ys.reshape(bp * lp, D_SSM), oa.reshape(bp * lp, D_ATTN),
                                gt1, sc2, sh2, g2, wo, wrt, 1, tm)
    k_prompt = ktf.reshape(bp, N_HEADS, HEAD_DIM, lp).transpose(0, 3, 1, 2)[None]
    v_prompt = vtf.reshape(bp, N_HEADS, HEAD_DIM, lp).transpose(0, 3, 1, 2)[None]
    logf_prompt = lf[:, :N_HEADS].reshape(1, bp, lp, N_HEADS)
    sre_p, sim_p = split_state(ht)

    nbb = 64
    sh1, sc1, gt1, sh2, sc2, gt2_s, shf_s, scf_s = mods(bp, bp + bs)
    u, lf, k, v, q = _inproj(x_sample, sc1, sh1, g1, w_main_s, w_qkv, b_f, nbb, ls, False, HEAD_DIM ** -0.5)
    ys, ht = _s5(u.reshape(bs, ls, D_SSM), ssm_state(state_ssm_re[0], state_ssm_im[0]), wb, wc, a_re, a_im,
                 dsk, wglu, bglu, g_so, 32, ls)
    lf_past = cache_logf[0][page_table].reshape(bs, n_past, N_HEADS).transpose(0, 2, 1)
    lf_new = lf[:, :N_HEADS].reshape(bs, ls, N_HEADS).transpose(0, 2, 1)
    n_key_pad = -(-(n_past + ls) // LANES) * LANES
    lf_all = jnp.concatenate([lf_past, lf_new, jnp.zeros((bs, N_HEADS, n_key_pad - n_past - ls), F32)], axis=2)
    gsz = n_key_pad // LANES
    cn_s = _neg_cumsum(lf_all.reshape(-1, LANES), gsz, gsz * 64).reshape(bs, N_HEADS, n_key_pad)
    oa = _attn_paged(page_table, q.reshape(bs, ls, D_ATTN), k.reshape(bs, ls, D_ATTN),
                     v.reshape(bs, ls, D_ATTN), cn_s, g_ao,
                     cache_k[0].transpose(0, 2, 3, 1), cache_v[0].transpose(0, 2, 3, 1))
    x1_s, h2_s, lg_s = _outproj(x_sample, ys, oa.reshape(bs * ls, D_ATTN).astype(BF16),
                                gt1, sc2, sh2, g2, wo, wrt, nbb, ls)

    assert nbb * ls == tm
    dest3, wts, yb = _moe([h2_p, h2_s], jnp.concatenate([lg_p, lg_s], axis=1), router_bias[0],
                          w_exp_gate[0], w_exp_up[0], w_exp_down[0], tm)
    tiles_p = bp * lp // tm
    y_prompt = _final(dest3[:tiles_p], x1_p, h2_p, wts[:bp * lp], yb, gt2_p, scf_p, shf_p, gf, wgu, wsd, 1, tm)
    y_sample = _final(dest3[tiles_p:], x1_s, h2_s, wts[bp * lp:], yb, gt2_s, scf_s, shf_s, gf, wgu, wsd, nbb, ls)
    k_sample = k.reshape(1, bs, ls, N_HEADS, HEAD_DIM)
    v_sample = v.reshape(1, bs, ls, N_HEADS, HEAD_DIM)
    logf_sample = lf[:, :N_HEADS].reshape(1, bs, ls, N_HEADS)
    sre_s, sim_s = split_state(ht)

    return (y_prompt, y_sample, k_prompt, v_prompt, logf_prompt, sre_p, sim_p,
            k_sample, v_sample, logf_sample, sre_s, sim_s)
```

```python
import functools
import math

import jax
import jax.numpy as jnp
import numpy as np
from jax import lax
from jax.experimental import pallas as pl
from jax.experimental.pallas import tpu as pltpu

F32 = jnp.float32
BF16 = jnp.bfloat16
I32 = jnp.int32

EPS = 1e-6
HEAD_DIM = 64
N_HEADS = 8
D_SSM = 512
D_ATTN = 512
SSM_GROUP = 16
SSM_STATE = 64
N_EXPERTS = 64
TOP_K = 6
N_EXPERT_GROUPS = 8
TOPK_GROUPS = 4
ROUTED_SCALE = 2.5
PAGE = 128

LANES = 128
SUBLANES = 8
VMEM_LIMIT = 48 * 1024 * 1024
MOE_ROWS = 512
ATTN_TILE = 256
LOG2E = math.log2(math.e)
NEG_INF = float("-inf")


def _cparams(sem):
    return pltpu.CompilerParams(dimension_semantics=sem, vmem_limit_bytes=VMEM_LIMIT)


def _bdot(a, b):
    return jnp.dot(a.astype(BF16), b.astype(BF16), preferred_element_type=F32)


def _bdot_nt(a, b):
    return lax.dot_general(a.astype(BF16), b.astype(BF16), (((1,), (1,)), ((), ())),
                           preferred_element_type=F32)


def _split3(v):
    hi = v.astype(BF16)
    r1 = v - hi.astype(F32)
    mid = r1.astype(BF16)
    lo = (r1 - mid.astype(F32)).astype(BF16)
    return hi, mid, lo


def _rms(x, g):
    return x * lax.rsqrt(jnp.mean(x * x, axis=-1, keepdims=True) + EPS) * g


def _adaln_kernel(c_ref, w_ref, b_ref, o_ref):
    c = c_ref[...]
    s = c * jax.nn.sigmoid(c)
    o_ref[...] = _bdot(s, w_ref[...]) + b_ref[...]


def _adaln(c, w, b):
    m, k = c.shape
    n = w.shape[1]
    tn = 1024
    return pl.pallas_call(
        _adaln_kernel,
        grid=(n // tn,),
        in_specs=[pl.BlockSpec((m, k), lambda j: (0, 0)),
                  pl.BlockSpec((k, tn), lambda j: (0, j)),
                  pl.BlockSpec((1, tn), lambda j: (0, j))],
        out_specs=pl.BlockSpec((m, tn), lambda j: (0, j)),
        out_shape=jax.ShapeDtypeStruct((m, n), F32),
        compiler_params=_cparams(("arbitrary",)),
    )(c, w, b.reshape(1, n))


def _inproj_kernel(x_ref, sc_ref, sh_ref, g_ref, w_ref, wqv_ref, bf_ref, *out_refs, transposed, q_scale):
    nbb, rb, d = x_ref.shape
    rows = nbb * rb
    x = x_ref[...]
    h = _rms(x, g_ref[...]) * (1.0 + sc_ref[...]) + sh_ref[...]
    hb = h.reshape(rows, d).astype(BF16)
    proj = jnp.dot(hb, w_ref[...], preferred_element_type=F32)
    u_ref, lf_ref = out_refs[:2]
    u_ref[...] = proj[:, :D_SSM]
    k = proj[:, D_SSM:D_SSM + D_ATTN]
    z = proj[:, proj.shape[1] - LANES:] + bf_ref[...]
    lf_ref[...] = jnp.minimum(z, 0.0) - jnp.log1p(jnp.exp(-jnp.abs(z)))
    nt = (((1,), (1,)), ((), ()))
    if transposed:
        kb_ref, qt_ref, vt_ref, ktf_ref, vtf_ref = out_refs[2:]
        kb_ref[...] = k.astype(BF16)
        qt = (lax.dot_general(wqv_ref[0], hb, nt, preferred_element_type=F32) * q_scale).astype(BF16)
        ktf_ref[0] = lax.dot_general(wqv_ref[1], hb, nt, preferred_element_type=F32)
        vt = lax.dot_general(wqv_ref[2], hb, nt, preferred_element_type=F32)
        vtf_ref[0] = vt
        vt = vt.astype(BF16)
        for c in range(rows // ATTN_TILE):
            qt_ref[c] = qt[:, c * ATTN_TILE:(c + 1) * ATTN_TILE]
            vt_ref[c] = vt[:, c * ATTN_TILE:(c + 1) * ATTN_TILE]
    else:
        k_ref, v_ref, q_ref = out_refs[2:]
        k_ref[...] = k
        v_ref[...] = proj[:, D_SSM + D_ATTN:D_SSM + 2 * D_ATTN]
        q_ref[...] = lax.dot_general(hb, wqv_ref[0], nt, preferred_element_type=F32) * q_scale


def _inproj(x3, sc, sh, g, w_main, w_qv, b_f, nbb, rb, transposed, q_scale):
    nb, r, d = x3.shape
    t = nb * r
    rows = nbb * rb
    nj = r // rb
    grid = (nb // nbb, nj)
    row_map = lambda i, j: (i * nj + j, 0)
    n_main = w_main.shape[1]
    outs = [jax.ShapeDtypeStruct((t, D_SSM), F32), jax.ShapeDtypeStruct((t, LANES), F32)]
    out_specs = [pl.BlockSpec((rows, D_SSM), row_map), pl.BlockSpec((rows, LANES), row_map)]
    if transposed:
        assert nbb == 1, "transposed outputs are laid out per sequence"
        nc = rows // ATTN_TILE
        chunk_map = lambda i, j: (i * nj + j, 0, 0)
        seq_map = lambda i, j: (i, 0, j)
        outs += [jax.ShapeDtypeStruct((t, D_ATTN), BF16),
                 jax.ShapeDtypeStruct((t // ATTN_TILE, D_ATTN, ATTN_TILE), BF16),
                 jax.ShapeDtypeStruct((t // ATTN_TILE, D_ATTN, ATTN_TILE), BF16),
                 jax.ShapeDtypeStruct((nb, D_ATTN, r), F32),
                 jax.ShapeDtypeStruct((nb, D_ATTN, r), F32)]
        out_specs += [pl.BlockSpec((rows, D_ATTN), row_map),
                      pl.BlockSpec((nc, D_ATTN, ATTN_TILE), chunk_map),
                      pl.BlockSpec((nc, D_ATTN, ATTN_TILE), chunk_map),
                      pl.BlockSpec((1, D_ATTN, rows), seq_map),
                      pl.BlockSpec((1, D_ATTN, rows), seq_map)]
    else:
        outs += [jax.ShapeDtypeStruct((t, D_ATTN), F32)] * 3
        out_specs += [pl.BlockSpec((rows, D_ATTN), row_map)] * 3
    return pl.pallas_call(
        functools.partial(_inproj_kernel, transposed=transposed, q_scale=q_scale),
        grid=grid,
        in_specs=[pl.BlockSpec((nbb, rb, d), lambda i, j: (i, j, 0)),
                  pl.BlockSpec((nbb, 1, d), lambda i, j: (i, 0, 0)),
                  pl.BlockSpec((nbb, 1, d), lambda i, j: (i, 0, 0)),
                  pl.BlockSpec((1, 1, d), lambda i, j: (0, 0, 0)),
                  pl.BlockSpec((d, n_main), lambda i, j: (0, 0)),
                  pl.BlockSpec((3, D_ATTN, d), lambda i, j: (0, 0, 0)),
                  pl.BlockSpec((1, LANES), lambda i, j: (0, 0))],
        out_specs=out_specs,
        out_shape=outs,
        compiler_params=_cparams(("arbitrary", "arbitrary")),
    )(x3, sc, sh, g, w_main, w_qv, b_f)


def _crep_kernel(lf_ref, tri_ref, ex_ref, o_ref, carry_ref, *, scale):
    j = pl.program_id(1)

    @pl.when(j == 0)
    def _():
        carry_ref[...] = jnp.zeros_like(carry_ref)

    tri = tri_ref[...]
    ex = ex_ref[...]
    cs = sum(jnp.dot(tri, p, preferred_element_type=F32) for p in _split3(lf_ref[...]))
    rep = sum(jnp.dot(p, ex, preferred_element_type=F32) for p in _split3(cs))
    rep = rep + carry_ref[...]
    rows = rep.shape[0]
    carry_ref[...] = rep[rows - 1:rows, :]
    for h in range(N_HEADS):
        o_ref[0, h] = rep[:, h * LANES:(h + 1) * LANES] * (-scale)


def _crep(lf128, nb, length, scale):
    rows = ATTN_TILE
    nj = length // rows
    tri = np.tril(np.ones((rows, rows), np.float32))
    ex = np.zeros((LANES, N_HEADS * LANES), np.float32)
    for h in range(N_HEADS):
        ex[h, h * LANES:(h + 1) * LANES] = 1.0
    return pl.pallas_call(
        functools.partial(_crep_kernel, scale=scale),
        grid=(nb, nj),
        in_specs=[pl.BlockSpec((rows, LANES), lambda i, j: (i * nj + j, 0)),
                  pl.BlockSpec((rows, rows), lambda i, j: (0, 0)),
                  pl.BlockSpec((LANES, N_HEADS * LANES), lambda i, j: (0, 0))],
        out_specs=pl.BlockSpec((1, N_HEADS, rows, LANES), lambda i, j: (i, 0, j, 0)),
        out_shape=jax.ShapeDtypeStruct((nb, N_HEADS, length, LANES), F32),
        scratch_shapes=[pltpu.VMEM((1, N_HEADS * LANES), F32)],
        compiler_params=_cparams(("arbitrary", "arbitrary")),
    )(lf128, jnp.asarray(tri, BF16), jnp.asarray(ex, BF16))


def _cumsum_kernel(x_ref, tri_ref, lt_ref, o_ref):
    tri = tri_ref[...]
    lt = lt_ref[...]
    cs = sum(jnp.dot(p, tri, preferred_element_type=F32) for p in _split3(x_ref[...]))
    tot = jnp.broadcast_to(cs[:, LANES - 1:LANES], cs.shape)
    carry = sum(jnp.dot(lt, p, preferred_element_type=F32) for p in _split3(tot))
    o_ref[...] = -(cs + carry)


def _neg_cumsum(x2, gsz, rb):
    r = x2.shape[0]
    tri = np.triu(np.ones((LANES, LANES), np.float32))
    ii = np.arange(rb)
    lt = ((ii[:, None] // gsz == ii[None, :] // gsz) & (ii[None, :] < ii[:, None])).astype(np.float32)
    return pl.pallas_call(
        _cumsum_kernel,
        grid=(r // rb,),
        in_specs=[pl.BlockSpec((rb, LANES), lambda i: (i, 0)),
                  pl.BlockSpec((LANES, LANES), lambda i: (0, 0)),
                  pl.BlockSpec((rb, rb), lambda i: (0, 0))],
        out_specs=pl.BlockSpec((rb, LANES), lambda i: (i, 0)),
        out_shape=jax.ShapeDtypeStruct((r, LANES), F32),
        compiler_params=_cparams(("arbitrary",)),
    )(x2, jnp.asarray(tri, BF16), jnp.asarray(lt, BF16))


def _gelu_tanh(x):
    return 0.5 * x * (1.0 + jnp.tanh(math.sqrt(2.0 / math.pi) * (x + 0.044715 * (x * x * x))))


def _s5_kernel(u_ref, h0_ref, wb_ref, wc_ref, are_ref, aim_ref, dsk_ref, wglu_ref, bglu_ref, gout_ref,
               y_ref, ht_ref, s_ref, hc_ref, *, nseq, tm):
    rows = nseq * tm
    sr = rows + SUBLANES
    ti = pl.program_id(1)

    @pl.when(ti == 0)
    def _():
        hc_ref[...] = h0_ref[...]

    u = u_ref[...].reshape(rows, D_SSM)
    ub = u.astype(BF16)
    for c in range(4):
        bu = jnp.dot(ub[:, c * LANES:(c + 1) * LANES], wb_ref[c], preferred_element_type=F32)
        for jj in range(4):
            s_ref[pl.ds((4 * c + jj) * sr, rows), :] = bu[:, jj * LANES:(jj + 1) * LANES]
            s_ref[pl.ds((16 + 4 * c + jj) * sr, rows), :] = bu[:, 512 + jj * LANES:512 + (jj + 1) * LANES]

    ar = (are_ref[0:8, :], are_ref[8:16, :])
    ai = (aim_ref[0:8, :], aim_ref[8:16, :])

    def seq_group(sg, carry):
        base = sg * 4
        hs = []
        for b in range(4):
            hs.append(tuple(hc_ref[base + b, pl.ds(8 * q, 8), :] for q in range(4)))

        def step(t, hs):
            new = []
            for b in range(4):
                row = (base + b) * tm + t
                hr0, hr1, hi0, hi1 = hs[b]
                bre0 = s_ref[pl.ds(row, 8, stride=sr), :]
                bre1 = s_ref[pl.ds(8 * sr + row, 8, stride=sr), :]
                bim0 = s_ref[pl.ds(16 * sr + row, 8, stride=sr), :]
                bim1 = s_ref[pl.ds(24 * sr + row, 8, stride=sr), :]
                nr0 = ar[0] * hr0 - ai[0] * hi0 + bre0
                nr1 = ar[1] * hr1 - ai[1] * hi1 + bre1
                ni0 = ar[0] * hi0 + ai[0] * hr0 + bim0
                ni1 = ar[1] * hi1 + ai[1] * hr1 + bim1
                s_ref[pl.ds(row, 8, stride=sr), :] = nr0
                s_ref[pl.ds(8 * sr + row, 8, stride=sr), :] = nr1
                s_ref[pl.ds(16 * sr + row, 8, stride=sr), :] = ni0
                s_ref[pl.ds(24 * sr + row, 8, stride=sr), :] = ni1
                new.append((nr0, nr1, ni0, ni1))
            return tuple(new)

        hs = lax.fori_loop(0, tm, step, tuple(hs), unroll=2)
        for b in range(4):
            for q in range(4):
                hc_ref[base + b, pl.ds(8 * q, 8), :] = hs[b][q]
        return carry

    lax.fori_loop(0, nseq // 4, seq_group, 0)
    ht_ref[...] = hc_ref[...]

    ys = []
    for c in range(4):
        blocks = [s_ref[pl.ds((4 * c + jj) * sr, rows), :].astype(BF16) for jj in range(4)]
        blocks += [s_ref[pl.ds((16 + 4 * c + jj) * sr, rows), :].astype(BF16) for jj in range(4)]
        hcat = jnp.concatenate(blocks, axis=1)
        ys.append(jnp.dot(hcat, wc_ref[c], preferred_element_type=F32))
    y = jnp.concatenate(ys, axis=1) + dsk_ref[...] * u
    y = _gelu_tanh(y)
    gate = jax.nn.sigmoid(jnp.dot(y.astype(BF16), wglu_ref[...], preferred_element_type=F32) + bglu_ref[...])
    y = y * gate
    y_ref[...] = _rms(y, gout_ref[...]).astype(BF16).reshape(y_ref.shape)


def _s5(u3, h0, wb, wc, a_re, a_im, dsk, wglu, bglu, gout, nseq, tm):
    nb, length, _ = u3.shape
    rows = nseq * tm
    sr = rows + SUBLANES
    grid = (nb // nseq, length // tm)
    const2 = lambda i, j: (0, 0)
    const3 = lambda i, j: (0, 0, 0)
    if tm == length:
        y_spec = pl.BlockSpec((rows, D_SSM), lambda i, j: (i, 0))
        y_shape = jax.ShapeDtypeStruct((nb * length, D_SSM), BF16)
    else:
        y_spec = pl.BlockSpec((nseq, tm, D_SSM), lambda i, j: (i, j, 0))
        y_shape = jax.ShapeDtypeStruct((nb, length, D_SSM), BF16)
    return pl.pallas_call(
        functools.partial(_s5_kernel, nseq=nseq, tm=tm),
        grid=grid,
        in_specs=[pl.BlockSpec((nseq, tm, D_SSM), lambda i, j: (i, j, 0)),
                  pl.BlockSpec((nseq, 32, LANES), lambda i, j: (i, 0, 0)),
                  pl.BlockSpec((4, LANES, 1024), const3),
                  pl.BlockSpec((4, 1024, LANES), const3),
                  pl.BlockSpec((16, LANES), const2),
                  pl.BlockSpec((16, LANES), const2),
                  pl.BlockSpec((1, D_SSM), const2),
                  pl.BlockSpec((D_SSM, D_SSM), const2),
                  pl.BlockSpec((1, D_SSM), const2),
                  pl.BlockSpec((1, D_SSM), const2)],
        out_specs=[y_spec, pl.BlockSpec((nseq, 32, LANES), lambda i, j: (i, 0, 0))],
        out_shape=[y_shape, jax.ShapeDtypeStruct((nb, 32, LANES), F32)],
        scratch_shapes=[pltpu.VMEM((32 * sr, LANES), F32), pltpu.VMEM((nseq, 32, LANES), F32)],
        compiler_params=_cparams(("arbitrary", "arbitrary")),
    )(u3, h0, wb, wc, a_re, a_im, dsk, wglu, bglu, gout)


def _s5_params(lam_re, lam_im, log_step, b_re, b_im, c_re, c_im):
    g = lam_re.shape[0]
    dt = jnp.exp(log_step)[:, None]
    mag = jnp.exp(lam_re * dt)
    a_re = mag * jnp.cos(lam_im * dt)
    a_im = mag * jnp.sin(lam_im * dt)
    den = lam_re * lam_re + lam_im * lam_im
    n_re = a_re - 1.0
    f_re = (n_re * lam_re + a_im * lam_im) / den
    f_im = (a_im * lam_re - n_re * lam_im) / den
    bb_re = f_re[..., None] * b_re - f_im[..., None] * b_im
    bb_im = f_re[..., None] * b_im + f_im[..., None] * b_re
    eye = jnp.eye(g, dtype=F32)
    n_state = g * SSM_STATE

    def in_mat(bb):
        return jnp.einsum('gpi,gh->gihp', bb, eye).reshape(g * SSM_GROUP, n_state)

    def out_mat(cc):
        return jnp.einsum('gip,gh->gphi', cc, eye).reshape(n_state, g * SSM_GROUP)

    wbr, wbi = in_mat(bb_re), in_mat(bb_im)
    wcr, wci = out_mat(c_re), out_mat(-c_im)
    wb = jnp.stack([jnp.concatenate([wbr[c * 128:(c + 1) * 128, c * 512:(c + 1) * 512],
                                     wbi[c * 128:(c + 1) * 128, c * 512:(c + 1) * 512]], axis=1)
                    for c in range(4)]).astype(BF16)
    wc = jnp.stack([jnp.concatenate([wcr[c * 512:(c + 1) * 512, c * 128:(c + 1) * 128],
                                     wci[c * 512:(c + 1) * 512, c * 128:(c + 1) * 128]], axis=0)
                    for c in range(4)]).astype(BF16)
    return wb, wc, a_re.reshape(16, LANES), a_im.reshape(16, LANES)


def _attn_kernel(qt_ref, k_ref, vt_ref, cr_ref, g_ref, o_ref, qz_ref, m_ref, l_ref, acc_ref, st_ref):
    tq = ATTN_TILE
    qi = pl.program_id(1)
    half = LANES // 2

    rowid = lax.broadcasted_iota(I32, (LANES, tq), 0)
    for j in range(N_HEADS // 2):
        qp = qt_ref[0, j * LANES:(j + 1) * LANES, :]
        qz_ref[2 * j] = jnp.where(rowid < half, qp, jnp.zeros_like(qp))
        qz_ref[2 * j + 1] = jnp.where(rowid >= half, qp, jnp.zeros_like(qp))
    m_ref[...] = jnp.full_like(m_ref, NEG_INF)
    l_ref[...] = jnp.zeros_like(l_ref)
    acc_ref[...] = jnp.zeros_like(acc_ref)

    key_row = lax.broadcasted_iota(I32, (tq, tq), 0)
    q_col = lax.broadcasted_iota(I32, (tq, tq), 1)
    causal = key_row <= q_col

    def tile(kb, masked):
        ks = pl.multiple_of(kb * tq, tq)
        m_new = []
        for j in range(N_HEADS // 2):
            kp = k_ref[0, pl.ds(ks, tq), j * LANES:(j + 1) * LANES]
            for e in range(2):
                h = 2 * j + e
                st = jnp.dot(kp, qz_ref[h], preferred_element_type=F32)
                bias = cr_ref[0, h, pl.ds(ks, tq), :]
                st = st + jnp.concatenate([bias] * (tq // LANES), axis=1)
                if masked:
                    st = jnp.where(causal, st, NEG_INF)
                st_ref[h] = st
                m_new.append(jnp.maximum(m_ref[h:h + 1, :], jnp.max(st, axis=0, keepdims=True)))
        for h in range(N_HEADS):
            alpha = jnp.exp2(m_ref[h:h + 1, :] - m_new[h])
            p = jnp.exp2(st_ref[h] - m_new[h])
            l_ref[h:h + 1, :] = alpha * l_ref[h:h + 1, :] + jnp.sum(p, axis=0, keepdims=True)
            m_ref[h:h + 1, :] = m_new[h]
            rows = slice(h * HEAD_DIM, (h + 1) * HEAD_DIM)
            pv = jnp.dot(vt_ref[kb, rows, :], p.astype(BF16), preferred_element_type=F32)
            acc_ref[rows, :] = alpha * acc_ref[rows, :] + pv

    def body(kb, c):
        tile(kb, False)
        return c

    lax.fori_loop(0, qi, body, 0)
    tile(qi, True)

    for h in range(N_HEADS):
        rows = slice(h * HEAD_DIM, (h + 1) * HEAD_DIM)
        acc_ref[rows, :] = acc_ref[rows, :] / l_ref[h:h + 1, :]
    o = acc_ref[...].T
    o_ref[0] = _rms(o, g_ref[...]).astype(BF16)


def _attn_prompt(qt, k, vt, crep, g):
    b, length, d = k.shape
    tq = ATTN_TILE
    nq = length // tq
    once = pl.Buffered(1)
    return pl.pallas_call(
        _attn_kernel,
        grid=(b, nq),
        in_specs=[pl.BlockSpec((1, d, tq), lambda i, j: (i * nq + j, 0, 0)),
                  pl.BlockSpec((1, length, d), lambda i, j: (i, 0, 0), pipeline_mode=once),
                  pl.BlockSpec((nq, d, tq), lambda i, j: (i, 0, 0), pipeline_mode=once),
                  pl.BlockSpec((1, N_HEADS, length, LANES), lambda i, j: (i, 0, 0, 0), pipeline_mode=once),
                  pl.BlockSpec((1, d), lambda i, j: (0, 0))],
        out_specs=pl.BlockSpec((1, tq, d), lambda i, j: (i, j, 0)),
        out_shape=jax.ShapeDtypeStruct((b, length, d), BF16),
        scratch_shapes=[pltpu.VMEM((N_HEADS, LANES, tq), BF16), pltpu.VMEM((N_HEADS, tq), F32),
                        pltpu.VMEM((N_HEADS, tq), F32), pltpu.VMEM((d, tq), F32),
                        pltpu.VMEM((N_HEADS, tq, tq), F32)],
        compiler_params=_cparams(("arbitrary", "arbitrary")),
    )(qt, k, vt, crep, g)


def _attn_paged_kernel(pt_ref, q_ref, kn_ref, vn_ref, cn_ref, g_ref, kc_hbm, vc_hbm, o_ref,
                       kbuf, vbuf, sem, *, n_pages, n_new):
    i = pl.program_id(0)
    n = pl.num_programs(0)
    n_past = n_pages * PAGE

    def page_copies(seq, slot):
        cps = []
        for p in range(n_pages):
            pg = pt_ref[seq * n_pages + p]
            cps.append(pltpu.make_async_copy(kc_hbm.at[pg], kbuf.at[slot, p], sem.at[0, slot]))
            cps.append(pltpu.make_async_copy(vc_hbm.at[pg], vbuf.at[slot, p], sem.at[1, slot]))
        return cps

    @pl.when(i == 0)
    def _():
        for cp in page_copies(0, 0):
            cp.start()

    @pl.when(i + 1 < n)
    def _():
        for cp in page_copies(i + 1, (i + 1) % 2):
            cp.start()

    slot = i % 2
    for cp in page_copies(i, slot):
        cp.wait()

    d = q_ref.shape[-1]
    nr = N_HEADS * n_new
    new_bits = n_new.bit_length() - 1
    head_bits = HEAD_DIM.bit_length() - 1
    rowh = lax.shift_right_logical(lax.broadcasted_iota(I32, (nr, d), 0), new_bits)
    colh = lax.shift_right_logical(lax.broadcasted_iota(I32, (nr, d), 1), head_bits)
    bd = rowh == colh
    qrep = jnp.broadcast_to(q_ref[0][None], (N_HEADS, n_new, d)).reshape(nr, d)
    qbd = jnp.where(bd, qrep, 0.0).astype(BF16)
    cn = cn_ref[0]
    cnr = jnp.broadcast_to(cn[:, None, :], (N_HEADS, n_new, cn.shape[-1])).reshape(nr, cn.shape[-1])

    s_p = []
    for p in range(n_pages):
        kt = kbuf[slot, p].reshape(d, PAGE)
        s_p.append(_bdot(qbd, kt) + cnr[:, p * PAGE:(p + 1) * PAGE])
    s_n = _bdot_nt(qbd, kn_ref[0]) + cnr[:, n_past:n_past + n_new]
    qpos = lax.broadcasted_iota(I32, (nr, n_new), 0) & (n_new - 1)
    kpos = lax.broadcasted_iota(I32, (nr, n_new), 1)
    s_n = jnp.where(kpos <= qpos, s_n, NEG_INF)
    m = jnp.max(s_n, axis=1, keepdims=True)
    for sp in s_p:
        m = jnp.maximum(m, jnp.max(sp, axis=1, keepdims=True))
    p_n = jnp.exp(s_n - m)
    l = jnp.sum(p_n, axis=1, keepdims=True)
    of = _bdot(p_n, vn_ref[0])
    for p in range(n_pages):
        pp = jnp.exp(s_p[p] - m)
        l = l + jnp.sum(pp, axis=1, keepdims=True)
        of = of + _bdot_nt(pp, vbuf[slot, p].reshape(d, PAGE))
    of = jnp.where(bd, of / l, 0.0)
    o = jnp.sum(of.reshape(N_HEADS, n_new, d), axis=0)
    o_ref[0] = _rms(o, g_ref[...])


def _attn_paged(page_table, q, kn, vn, cn, g, cache_k, cache_v):
    nseq, n_new, d = q.shape
    assert n_new & (n_new - 1) == 0, "new-token count must be a power of two"
    n_pages = page_table.shape[1]
    n_past = n_pages * PAGE
    grid_spec = pltpu.PrefetchScalarGridSpec(
        num_scalar_prefetch=1,
        grid=(nseq,),
        in_specs=[pl.BlockSpec((1, n_new, d), lambda i, pt: (i, 0, 0)),
                  pl.BlockSpec((1, n_new, d), lambda i, pt: (i, 0, 0)),
                  pl.BlockSpec((1, n_new, d), lambda i, pt: (i, 0, 0)),
                  pl.BlockSpec((1, N_HEADS, cn.shape[-1]), lambda i, pt: (i, 0, 0)),
                  pl.BlockSpec((1, d), lambda i, pt: (0, 0)),
                  pl.BlockSpec(memory_space=pl.ANY),
                  pl.BlockSpec(memory_space=pl.ANY)],
        out_specs=pl.BlockSpec((1, n_new, d), lambda i, pt: (i, 0, 0)),
        scratch_shapes=[pltpu.VMEM((2, n_pages, N_HEADS, HEAD_DIM, PAGE), F32),
                        pltpu.VMEM((2, n_pages, N_HEADS, HEAD_DIM, PAGE), F32),
                        pltpu.SemaphoreType.DMA((2, 2))],
    )
    return pl.pallas_call(
        functools.partial(_attn_paged_kernel, n_pages=n_pages, n_new=n_new),
        grid_spec=grid_spec,
        out_shape=jax.ShapeDtypeStruct((nseq, n_new, d), F32),
        compiler_params=_cparams(("arbitrary",)),
    )(page_table.reshape(-1), q, kn, vn, cn, g, cache_k, cache_v)


def _outproj_kernel(x_ref, ys_ref, oa_ref, gt_ref, sc_ref, sh_ref, g_ref, wo_ref, wrt_ref,
                    x1_ref, h2_ref, lg_ref):
    nbb, rb, d = x_ref.shape
    mix = jnp.dot(ys_ref[...], wo_ref[0:D_SSM, :], preferred_element_type=F32)
    mix = mix + jnp.dot(oa_ref[...], wo_ref[D_SSM:, :], preferred_element_type=F32)
    x1 = x_ref[...] + gt_ref[...] * mix.reshape(nbb, rb, d)
    x1_ref[...] = x1
    h2 = (_rms(x1, g_ref[...]) * (1.0 + sc_ref[...]) + sh_ref[...]).reshape(nbb * rb, d)
    hb = h2.astype(BF16)
    h2_ref[...] = h2
    hlo = (h2 - hb.astype(F32)).astype(BF16)
    whi = wrt_ref[0]
    wlo = wrt_ref[1]
    nt = (((1,), (1,)), ((), ()))
    lg = lax.dot_general(whi, hb, nt, preferred_element_type=F32)
    lg = lg + lax.dot_general(wlo, hb, nt, preferred_element_type=F32)
    lg = lg + lax.dot_general(whi, hlo, nt, preferred_element_type=F32)
    lg_ref[...] = lg


def _outproj(x3, ys, oa, gt, sc, sh, g, wo, wrt, nbb, rb):
    nb, r, d = x3.shape
    t = nb * r
    rows = nbb * rb
    nj = r // rb
    row_map = lambda i, j: (i * nj + j, 0)
    mod_spec = pl.BlockSpec((nbb, 1, d), lambda i, j: (i, 0, 0))
    return pl.pallas_call(
        _outproj_kernel,
        grid=(nb // nbb, nj),
        in_specs=[pl.BlockSpec((nbb, rb, d), lambda i, j: (i, j, 0)),
                  pl.BlockSpec((rows, D_SSM), row_map),
                  pl.BlockSpec((rows, D_ATTN), row_map),
                  mod_spec, mod_spec, mod_spec,
                  pl.BlockSpec((1, 1, d), lambda i, j: (0, 0, 0)),
                  pl.BlockSpec((D_SSM + D_ATTN, d), lambda i, j: (0, 0)),
                  pl.BlockSpec((2, N_EXPERTS, d), lambda i, j: (0, 0, 0))],
        out_specs=[pl.BlockSpec((nbb, rb, d), lambda i, j: (i, j, 0)),
                   pl.BlockSpec((rows, d), row_map),
                   pl.BlockSpec((N_EXPERTS, rows), lambda i, j: (0, i * nj + j))],
        out_shape=[jax.ShapeDtypeStruct((nb, r, d), F32),
                   jax.ShapeDtypeStruct((t, d), F32),
                   jax.ShapeDtypeStruct((N_EXPERTS, t), F32)],
        compiler_params=_cparams(("arbitrary", "arbitrary")),
    )(x3, ys, oa, gt, sc, sh, g, wo, wrt)


def _router_kernel(lg_ref, rb_ref, ut_ref, idx_ref, w_ref, rank_ref, cnt_ref, carry_ref):
    i = pl.program_id(0)
    tm = lg_ref.shape[1]
    per_group = N_EXPERTS // N_EXPERT_GROUPS

    @pl.when(i == 0)
    def _():
        carry_ref[...] = jnp.zeros_like(carry_ref)

    scores = jax.nn.sigmoid(lg_ref[...])
    biased = scores + rb_ref[...]
    blks, grp = [], []
    for g in range(N_EXPERT_GROUPS):
        blk = biased[g * per_group:(g + 1) * per_group, :]
        m1 = jnp.max(blk, axis=0, keepdims=True)
        eq = blk == m1
        n_eq = jnp.sum(jnp.where(eq, 1.0, 0.0), axis=0, keepdims=True)
        m2 = jnp.max(jnp.where(eq, NEG_INF, blk), axis=0, keepdims=True)
        blks.append(blk)
        grp.append(m1 + jnp.where(n_eq >= 2.0, m1, m2))
    masked = []
    for g in range(N_EXPERT_GROUPS):
        beaten = jnp.zeros((1, tm), F32)
        for o in range(N_EXPERT_GROUPS):
            if o == g:
                continue
            ahead = (grp[o] >= grp[g]) if o < g else (grp[o] > grp[g])
            beaten = beaten + jnp.where(ahead, 1.0, 0.0)
        masked.append(jnp.where(beaten < float(TOPK_GROUPS), blks[g], NEG_INF))
    work = jnp.concatenate(masked, axis=0)

    eid = lax.broadcasted_iota(I32, (N_EXPERTS, tm), 0)
    chosen = jnp.zeros((N_EXPERTS, tm), F32)
    idxs, ws, sels = [], [], []
    for _ in range(TOP_K):
        m = jnp.max(work, axis=0, keepdims=True)
        first = jnp.min(jnp.where(work == m, eid, N_EXPERTS), axis=0, keepdims=True)
        sel = eid == first
        idxs.append(first)
        ws.append(jnp.sum(jnp.where(sel, scores, 0.0), axis=0, keepdims=True))
        sels.append(sel)
        chosen = jnp.where(sel, 1.0, chosen)
        work = jnp.where(sel, NEG_INF, work)
    wsum = ws[0]
    for wk in ws[1:]:
        wsum = wsum + wk

    prefix = jnp.dot(chosen.astype(BF16), ut_ref[...], preferred_element_type=F32) + carry_ref[...]
    carry_ref[...] = carry_ref[...] + jnp.sum(chosen, axis=1, keepdims=True)
    cnt_ref[...] = carry_ref[...]

    idx_ref[...] = jnp.zeros_like(idx_ref)
    w_ref[...] = jnp.zeros_like(w_ref)
    rank_ref[...] = jnp.zeros_like(rank_ref)
    for k in range(TOP_K):
        idx_ref[k:k + 1, :] = idxs[k]
        w_ref[k:k + 1, :] = ws[k] / wsum * ROUTED_SCALE
        rank = jnp.sum(jnp.where(sels[k], prefix, 0.0), axis=0, keepdims=True)
        rank_ref[k:k + 1, :] = rank.astype(I32)


def _router(lg, router_bias, tm):
    e, t = lg.shape
    ut = np.triu(np.ones((tm, tm), np.float32), 1)
    tok_spec = pl.BlockSpec((SUBLANES, tm), lambda i: (0, i))
    return pl.pallas_call(
        _router_kernel,
        grid=(t // tm,),
        in_specs=[pl.BlockSpec((e, tm), lambda i: (0, i)),
                  pl.BlockSpec((e, 1), lambda i: (0, 0)),
                  pl.BlockSpec((tm, tm), lambda i: (0, 0))],
        out_specs=[tok_spec, tok_spec, tok_spec, pl.BlockSpec((e, 1), lambda i: (0, 0))],
        out_shape=[jax.ShapeDtypeStruct((SUBLANES, t), I32), jax.ShapeDtypeStruct((SUBLANES, t), F32),
                   jax.ShapeDtypeStruct((SUBLANES, t), I32), jax.ShapeDtypeStruct((e, 1), F32)],
        scratch_shapes=[pltpu.VMEM((e, 1), F32)],
        compiler_params=_cparams(("arbitrary",)),
    )(lg, router_bias.reshape(e, 1), jnp.asarray(ut, BF16))


def _experts_kernel(be_ref, nv_ref, new_ref, x_ref, wg_ref, wu_ref, wd_ref, o_ref, wgu_s, wd_s):
    i = pl.program_id(0)
    de = wg_ref.shape[-1]
    valid = i < nv_ref[0]

    @pl.when(valid & (new_ref[i] == 1))
    def _():
        wgu_s[:, :de] = wg_ref[0].astype(BF16)
        wgu_s[:, de:] = wu_ref[0].astype(BF16)
        wd_s[...] = wd_ref[0].astype(BF16)

    @pl.when(valid)
    def _():
        gu = jnp.dot(x_ref[...].astype(BF16), wgu_s[...], preferred_element_type=F32)
        g = gu[:, :de]
        a = (g * jax.nn.sigmoid(g) * gu[:, de:]).astype(BF16)
        o_ref[...] = jnp.dot(a, wd_s[...], preferred_element_type=F32)

    @pl.when(jnp.logical_not(valid))
    def _():
        o_ref[...] = jnp.zeros_like(o_ref)


def _experts(block_e, n_valid, block_new, xs, w_eg, w_eu, w_ed, m):
    n_rows, d = xs.shape
    de = w_eg.shape[-1]
    nblk = n_rows // m
    xmap = lambda i, be, nv, nw: (jnp.minimum(i, nv[0] - 1), 0)
    wmap = lambda i, be, nv, nw: (be[i], 0, 0)
    grid_spec = pltpu.PrefetchScalarGridSpec(
        num_scalar_prefetch=3,
        grid=(nblk,),
        in_specs=[pl.BlockSpec((m, d), xmap),
                  pl.BlockSpec((1, d, de), wmap),
                  pl.BlockSpec((1, d, de), wmap),
                  pl.BlockSpec((1, de, d), wmap)],
        out_specs=pl.BlockSpec((m, d), lambda i, be, nv, nw: (i, 0)),
        scratch_shapes=[pltpu.VMEM((d, 2 * de), BF16), pltpu.VMEM((de, d), BF16)],
    )
    return pl.pallas_call(
        _experts_kernel,
        grid_spec=grid_spec,
        out_shape=jax.ShapeDtypeStruct((n_rows, d), F32),
        compiler_params=_cparams(("arbitrary",)),
    )(block_e, n_valid, block_new, xs, w_eg, w_eu, w_ed)


def _row_copy(src, src_row, dst, dst_row, sem):
    return pltpu.make_async_copy(src.at[pl.ds(src_row, 1)], dst.at[pl.ds(dst_row, 1)], sem)


def _final_kernel(dest_ref, x1_ref, h2_ref, w_ref, gt_ref, sc_ref, sh_ref, g_ref, wgu_ref, wd_ref, yb_hbm,
                  y_ref, gbuf, sem):
    nbb, rb, d = x1_ref.shape
    rows = nbb * rb
    ds = wd_ref.shape[0]

    def issue(r, c):
        for k in range(TOP_K):
            _row_copy(yb_hbm, dest_ref[0, 0, r * TOP_K + k], gbuf.at[k], r, sem).start(priority=k % 2)
        return c

    lax.fori_loop(0, rows, issue, 0, unroll=2)

    gu = jnp.dot(h2_ref[...].astype(BF16), wgu_ref[...], preferred_element_type=F32)
    g = gu[:, :ds]
    a = (g * jax.nn.sigmoid(g) * gu[:, ds:]).astype(BF16)
    ff = jnp.dot(a, wd_ref[...], preferred_element_type=F32)

    for k in range(TOP_K):
        pltpu.make_async_copy(yb_hbm.at[pl.ds(0, rows)], gbuf.at[k], sem).wait()
    routed = gbuf[0] * w_ref[:, 0:1]
    for k in range(1, TOP_K):
        routed = routed + gbuf[k] * w_ref[:, k:k + 1]
    ff = routed + ff
    x2 = x1_ref[...] + gt_ref[...] * ff.reshape(nbb, rb, d)
    y_ref[...] = _rms(x2, g_ref[...]) * (1.0 + sc_ref[...]) + sh_ref[...]


def _final(dest3, x1, h2, wts, yb, gt, sc, sh, g, wgu, wd, nbb, rb):
    nb, r, d = x1.shape
    rows = nbb * rb
    nj = r // rb
    row_map = lambda i, j: (i * nj + j, 0)
    mod_spec = pl.BlockSpec((nbb, 1, d), lambda i, j: (i, 0, 0))
    return pl.pallas_call(
        _final_kernel,
        grid=(nb // nbb, nj),
        in_specs=[pl.BlockSpec((1, 1, rows * TOP_K), lambda i, j: (i * nj + j, 0, 0), memory_space=pltpu.SMEM),
                  pl.BlockSpec((nbb, rb, d), lambda i, j: (i, j, 0)),
                  pl.BlockSpec((rows, d), row_map),
                  pl.BlockSpec((rows, SUBLANES), row_map),
                  mod_spec, mod_spec, mod_spec,
                  pl.BlockSpec((1, 1, d), lambda i, j: (0, 0, 0)),
                  pl.BlockSpec(wgu.shape, lambda i, j: (0, 0)),
                  pl.BlockSpec(wd.shape, lambda i, j: (0, 0)),
                  pl.BlockSpec(memory_space=pl.ANY)],
        out_specs=pl.BlockSpec((nbb, rb, d), lambda i, j: (i, j, 0)),
        out_shape=jax.ShapeDtypeStruct((nb, r, d), F32),
        scratch_shapes=[pltpu.VMEM((TOP_K, rows, d), F32), pltpu.SemaphoreType.DMA(())],
        compiler_params=_cparams(("arbitrary", "arbitrary")),
    )(dest3, x1, h2, wts, gt, sc, sh, g, wgu, wd, yb)


def _dispatch_kernel(plo_ref, pln_ref, nv_ref, dest_ref, *rest, group_tiles, m):
    x_refs = rest[:len(group_tiles)]
    xs_hbm, sem, zsem, zbuf = rest[len(group_tiles):]
    rows = x_refs[0].shape[0]
    step = pl.program_id(0)

    nblk = xs_hbm.shape[0] // m

    def pad_copies(act):
        def single_rows(off, n):
            for r in range(SUBLANES - 1):
                @pl.when(r < n)
                def _(r=r):
                    act(_row_copy(zbuf, 0, xs_hbm, off + r, zsem))

        def per_expert(e, c):
            off = plo_ref[e]
            ln = pln_ref[e]
            head = jnp.minimum((-off) & (SUBLANES - 1), ln)
            single_rows(off, head)
            off = off + head
            ln = ln - head
            bit = m // 2
            while bit >= SUBLANES:
                has = (ln & bit) != 0

                @pl.when(has)
                def _(off=off, bit=bit):
                    dst = xs_hbm.at[pl.ds(pl.multiple_of(off, SUBLANES), bit)]
                    act(pltpu.make_async_copy(zbuf.at[pl.ds(0, bit)], dst, zsem))

                off = off + jnp.where(has, bit, 0)
                bit //= 2
            single_rows(off, ln & (SUBLANES - 1))
            return c

        lax.fori_loop(0, N_EXPERTS, per_expert, 0)

        def per_block(b, c):
            @pl.when(b >= nv_ref[0])
            def _():
                act(pltpu.make_async_copy(zbuf, xs_hbm.at[pl.ds(pl.multiple_of(b * m, m), m)], zsem))

            return c

        lax.fori_loop(0, nblk, per_block, 0)

    @pl.when(step == 0)
    def _():
        zbuf[...] = jnp.zeros_like(zbuf)
        pad_copies(lambda cp: cp.start())
        pad_copies(lambda cp: cp.wait())

    first = 0
    for x_ref, n_tiles in zip(x_refs, group_tiles):
        @pl.when((step >= first) & (step < first + n_tiles))
        def _(x_ref=x_ref):
            def issue(r, c):
                for k in range(TOP_K):
                    _row_copy(x_ref, r, xs_hbm, dest_ref[0, 0, r * TOP_K + k], sem).start(priority=k % 2)
                return c

            lax.fori_loop(0, rows, issue, 0, unroll=2)
            for k in range(TOP_K):
                pltpu.make_async_copy(x_ref, xs_hbm.at[pl.ds(0, rows)], sem).wait()

        first += n_tiles


def _dispatch(pad_lo, pad_len, n_valid, dest3, h2_groups, n_rows, m):
    d = h2_groups[0].shape[1]
    rows = dest3.shape[-1] // TOP_K
    group_tiles = tuple(h2.shape[0] // rows for h2 in h2_groups)
    in_specs = [pl.BlockSpec((1, 1, rows * TOP_K), lambda i, *_: (i, 0, 0), memory_space=pltpu.SMEM)]
    first = 0
    for n_tiles in group_tiles:
        tile_map = lambda i, *_, first=first, n_tiles=n_tiles: (jnp.clip(i - first, 0, n_tiles - 1), 0)
        in_specs.append(pl.BlockSpec((rows, d), tile_map))
        first += n_tiles
    grid_spec = pltpu.PrefetchScalarGridSpec(
        num_scalar_prefetch=3,
        grid=(sum(group_tiles),),
        in_specs=in_specs,
        out_specs=pl.BlockSpec(memory_space=pl.ANY),
        scratch_shapes=[pltpu.SemaphoreType.DMA(()), pltpu.SemaphoreType.DMA(()), pltpu.VMEM((m, d), F32)],
    )
    return pl.pallas_call(
        functools.partial(_dispatch_kernel, group_tiles=group_tiles, m=m),
        grid_spec=grid_spec,
        out_shape=jax.ShapeDtypeStruct((n_rows, d), F32),
        compiler_params=_cparams(("arbitrary",)),
    )(pad_lo, pad_len, n_valid, dest3, *h2_groups)


def _moe(h2_groups, lg, router_bias, w_eg, w_eu, w_ed, tile):
    t = lg.shape[1]
    e = N_EXPERTS
    m = MOE_ROWS
    idx_t, w_t, rank_t, counts = _router(lg, router_bias, tile)
    counts = counts.reshape(e).astype(I32)
    padded = ((counts + m - 1) // m) * m
    pad_end = jnp.cumsum(padded)
    pad_start = pad_end - padded
    onehot = idx_t[:, :, None] == jnp.arange(e, dtype=I32)
    dest_t = jnp.sum(jnp.where(onehot, pad_start, 0), axis=-1) + rank_t
    dest3 = dest_t[:TOP_K].T.reshape(t // tile, 1, tile * TOP_K)
    n_rows = (-(-(t * TOP_K) // m)) * m + e * m
    nblk = n_rows // m
    block_start = jnp.arange(nblk, dtype=I32) * m
    block_e = jnp.minimum(jnp.sum(pad_end[None, :] <= block_start[:, None], axis=1), e - 1).astype(I32)
    block_new = jnp.concatenate([jnp.ones((1,), I32), (block_e[1:] != block_e[:-1]).astype(I32)])
    n_valid = (pad_end[-1] // m).astype(I32).reshape(1)
    xs = _dispatch(pad_start + counts, padded - counts, n_valid, dest3, h2_groups, n_rows, m)
    yb = _experts(block_e, n_valid, block_new, xs, w_eg, w_eu, w_ed, m)
    return dest3, w_t.T, yb


def kernel(x_prompt, x_sample, c_prompt, c_sample, cache_k, cache_v, cache_logf, state_ssm_re, state_ssm_im, page_table, w_ada, b_ada, g_norm1, w_in, b_fgate, ssm_lambda_re, ssm_lambda_im, ssm_log_step, ssm_b_re, ssm_b_im, ssm_c_re, ssm_c_im, ssm_d, w_glu, b_glu, g_ssm_out, g_attn_out, w_out, g_norm2, w_router, router_bias, w_exp_gate, w_exp_up, w_exp_down, w_sh_gate, w_sh_up, w_sh_down, g_final, w_ada_final, b_ada_final):
    depth = w_ada.shape[0]
    assert depth == 1, "one layer is supported"
    bp, lp, d = x_prompt.shape
    bs, ls, _ = x_sample.shape
    n_pages = page_table.shape[1]
    n_past = n_pages * PAGE
    n_groups = ssm_lambda_re.shape[1]

    n_c = bp + bs
    n_c_pad = -(-n_c // SUBLANES) * SUBLANES
    c_all = jnp.concatenate([c_prompt, c_sample, jnp.zeros((n_c_pad - n_c, d), F32)], axis=0)
    mod = _adaln(c_all, w_ada[0], b_ada[0])
    modf = _adaln(c_all, w_ada_final, b_ada_final)

    def mods(lo, hi):
        parts = [mod[lo:hi, k * d:(k + 1) * d][:, None, :] for k in range(6)]
        parts += [modf[lo:hi, k * d:(k + 1) * d][:, None, :] for k in range(2)]
        return parts

    w_u, w_q, w_k, w_v, w_f = jnp.split(w_in[0], [D_SSM, D_SSM + D_ATTN, D_SSM + 2 * D_ATTN,
                                                   D_SSM + 3 * D_ATTN], axis=1)
    w_fpad = jnp.concatenate([w_f, jnp.zeros((d, LANES - N_HEADS), F32)], axis=1)
    w_main_s = jnp.concatenate([w_u, w_k, w_v, w_fpad], axis=1).astype(BF16)
    w_main_p = jnp.concatenate([w_u, w_k, w_fpad], axis=1).astype(BF16)
    w_qkv = jnp.stack([w_q.T, w_k.T, w_v.T]).astype(BF16)
    b_f = jnp.concatenate([b_fgate[0], jnp.zeros((LANES - N_HEADS,), F32)]).reshape(1, LANES)
    g1 = g_norm1[0].reshape(1, 1, d)
    g2 = g_norm2[0].reshape(1, 1, d)
    gf = g_final.reshape(1, 1, d)
    wb, wc, a_re, a_im = _s5_params(ssm_lambda_re[0], ssm_lambda_im[0], ssm_log_step[0], ssm_b_re[0],
                                    ssm_b_im[0], ssm_c_re[0], ssm_c_im[0])
    dsk = ssm_d[0].reshape(1, D_SSM)
    wglu = w_glu[0].astype(BF16)
    bglu = b_glu[0].reshape(1, D_SSM)
    g_so = g_ssm_out[0].reshape(1, D_SSM)
    g_ao = g_attn_out[0].reshape(1, D_ATTN)
    wo = w_out[0].astype(BF16)
    wr_t = w_router[0].T
    wr_hi = wr_t.astype(BF16)
    wrt = jnp.stack([wr_hi, (wr_t - wr_hi.astype(F32)).astype(BF16)])
    wgu = jnp.concatenate([w_sh_gate[0], w_sh_up[0]], axis=1).astype(BF16)
    wsd = w_sh_down[0].astype(BF16)

    def ssm_state(re, im):
        return jnp.concatenate([re.reshape(-1, 16, LANES), im.reshape(-1, 16, LANES)], axis=1)

    def split_state(ht):
        n = ht.shape[0]
        return (ht[:, :16].reshape(1, n, n_groups, SSM_STATE), ht[:, 16:].reshape(1, n, n_groups, SSM_STATE))

    tm = 512
    sh1, sc1, gt1, sh2, sc2, gt2_p, shf_p, scf_p = mods(0, bp)
    u, lf, kb, qt, vt, ktf, vtf = _inproj(x_prompt, sc1, sh1, g1, w_main_p, w_qkv, b_f, 1, tm, True,
                                          HEAD_DIM ** -0.5 * LOG2E)
    crep = _crep(lf, bp, lp, LOG2E)
    ys, ht = _s5(u.reshape(bp, lp, D_SSM), jnp.zeros((bp, 32, LANES), F32), wb, wc, a_re, a_im, dsk,
                 wglu, bglu, g_so, bp, 256)
    oa = _attn_prompt(qt, kb.reshape(bp, lp, D_ATTN), vt, crep, g_ao)
    x1_p, h2_p, lg_p = _outproj(x_prompt, ys.reshape(bp * lp, D_SSM), oa.reshape(bp * lp, D_ATTN),
                                gt1, sc2, sh2, g2, wo, wrt, 1, tm)
    k_prompt = ktf.reshape(bp, N_HEADS, HEAD_DIM, lp).transpose(0, 3, 1, 2)[None]
    v_prompt = vtf.reshape(bp, N_HEADS, HEAD_DIM, lp).transpose(0, 3, 1, 2)[None]
    logf_prompt = lf[:, :N_HEADS].reshape(1, bp, lp, N_HEADS)
    sre_p, sim_p = split_state(ht)

    nbb = 64
    sh1, sc1, gt1, sh2, sc2, gt2_s, shf_s, scf_s = mods(bp, bp + bs)
    u, lf, k, v, q = _inproj(x_sample, sc1, sh1, g1, w_main_s, w_qkv, b_f, nbb, ls, False, HEAD_DIM ** -0.5)
    ys, ht = _s5(u.reshape(bs, ls, D_SSM), ssm_state(state_ssm_re[0], state_ssm_im[0]), wb, wc, a_re, a_im,
                 dsk, wglu, bglu, g_so, 32, ls)
    lf_past = cache_logf[0][page_table].reshape(bs, n_past, N_HEADS).transpose(0, 2, 1)
    lf_new = lf[:, :N_HEADS].reshape(bs, ls, N_HEADS).transpose(0, 2, 1)
    n_key_pad = -(-(n_past + ls) // LANES) * LANES
    lf_all = jnp.concatenate([lf_past, lf_new, jnp.zeros((bs, N_HEADS, n_key_pad - n_past - ls), F32)], axis=2)
    gsz = n_key_pad // LANES
    cn_s = _neg_cumsum(lf_all.reshape(-1, LANES), gsz, gsz * 64).reshape(bs, N_HEADS, n_key_pad)
    oa = _attn_paged(page_table, q.reshape(bs, ls, D_ATTN), k.reshape(bs, ls, D_ATTN),
                     v.reshape(bs, ls, D_ATTN), cn_s, g_ao,
                     cache_k[0].transpose(0, 2, 3, 1), cache_v[0].transpose(0, 2, 3, 1))
    x1_s, h2_s, lg_s = _outproj(x_sample, ys, oa.reshape(bs * ls, D_ATTN).astype(BF16),
                                gt1, sc2, sh2, g2, wo, wrt, nbb, ls)

    assert nbb * ls == tm
    dest3, wts, yb = _moe([h2_p, h2_s], jnp.concatenate([lg_p, lg_s], axis=1), router_bias[0],
                          w_exp_gate[0], w_exp_up[0], w_exp_down[0], tm)
    tiles_p = bp * lp // tm
    y_prompt = _final(dest3[:tiles_p], x1_p, h2_p, wts[:bp * lp], yb, gt2_p, scf_p, shf_p, gf, wgu, wsd, 1, tm)
    y_sample = _final(dest3[tiles_p:], x1_s, h2_s, wts[bp * lp:], yb, gt2_s, scf_s, shf_s, gf, wgu, wsd, nbb, ls)
    k_sample = k.reshape(1, bs, ls, N_HEADS, HEAD_DIM)
    v_sample = v.reshape(1, bs, ls, N_HEADS, HEAD_DIM)
    logf_sample = lf[:, :N_HEADS].reshape(1, bs, ls, N_HEADS)
    sre_s, sim_s = split_state(ht)

    return (y_prompt, y_sample, k_prompt, v_prompt, logf_prompt, sre_p, sim_p,
            k_sample, v_sample, logf_sample, sre_s, sim_s)
```

```python
import functools
import math

import jax
import jax.numpy as jnp
import numpy as np
from jax import lax
from jax.experimental import pallas as pl
from jax.experimental.pallas import tpu as pltpu

F32 = jnp.float32
BF16 = jnp.bfloat16
I32 = jnp.int32

EPS = 1e-6
HEAD_DIM = 64
N_HEADS = 8
D_SSM = 512
D_ATTN = 512
SSM_GROUP = 16
SSM_STATE = 64
N_EXPERTS = 64
TOP_K = 6
N_EXPERT_GROUPS = 8
TOPK_GROUPS = 4
ROUTED_SCALE = 2.5
PAGE = 128

LANES = 128
SUBLANES = 8
VMEM_LIMIT = 48 * 1024 * 1024
MOE_ROWS = 512
ATTN_TILE = 256
ATTN_GROUP = 4
LOG2E = math.log2(math.e)
NEG_INF = float("-inf")


def _cparams(sem):
    return pltpu.CompilerParams(dimension_semantics=sem, vmem_limit_bytes=VMEM_LIMIT)


def _bdot(a, b):
    return jnp.dot(a.astype(BF16), b.astype(BF16), preferred_element_type=F32)


def _bdot_nt(a, b):
    return lax.dot_general(a.astype(BF16), b.astype(BF16), (((1,), (1,)), ((), ())),
                           preferred_element_type=F32)


def _split3(v):
    hi = v.astype(BF16)
    r1 = v - hi.astype(F32)
    mid = r1.astype(BF16)
    lo = (r1 - mid.astype(F32)).astype(BF16)
    return hi, mid, lo


def _rms(x, g):
    return x * lax.rsqrt(jnp.mean(x * x, axis=-1, keepdims=True) + EPS) * g


def _adaln_kernel(c_ref, w_ref, b_ref, o_ref):
    c = c_ref[...]
    s = c * jax.nn.sigmoid(c)
    o_ref[...] = _bdot(s, w_ref[...]) + b_ref[...]


def _adaln(c, w, b):
    m, k = c.shape
    n = w.shape[1]
    tn = 1024
    return pl.pallas_call(
        _adaln_kernel,
        grid=(n // tn,),
        in_specs=[pl.BlockSpec((m, k), lambda j: (0, 0)),
                  pl.BlockSpec((k, tn), lambda j: (0, j)),
                  pl.BlockSpec((1, tn), lambda j: (0, j))],
        out_specs=pl.BlockSpec((m, tn), lambda j: (0, j)),
        out_shape=jax.ShapeDtypeStruct((m, n), F32),
        compiler_params=_cparams(("arbitrary",)),
    )(c, w, b.reshape(1, n))


def _inproj_kernel(x_ref, sc_ref, sh_ref, g_ref, w_ref, wqv_ref, bf_ref, *out_refs, transposed, q_scale):
    nbb, rb, d = x_ref.shape
    rows = nbb * rb
    x = x_ref[...]
    h = _rms(x, g_ref[...]) * (1.0 + sc_ref[...]) + sh_ref[...]
    hb = h.reshape(rows, d).astype(BF16)
    proj = jnp.dot(hb, w_ref[...], preferred_element_type=F32)
    u_ref, lf_ref = out_refs[:2]
    u_ref[...] = proj[:, :D_SSM]
    k = proj[:, D_SSM:D_SSM + D_ATTN]
    z = proj[:, proj.shape[1] - LANES:] + bf_ref[...]
    lf_ref[...] = jnp.minimum(z, 0.0) - jnp.log1p(jnp.exp(-jnp.abs(z)))
    nt = (((1,), (1,)), ((), ()))
    if transposed:
        kb_ref, qt_ref, vt_ref, ktf_ref, vtf_ref = out_refs[2:]
        kb_ref[...] = k.astype(BF16)
        qt = (lax.dot_general(wqv_ref[0], hb, nt, preferred_element_type=F32) * q_scale).astype(BF16)
        ktf_ref[0] = lax.dot_general(wqv_ref[1], hb, nt, preferred_element_type=F32)
        vt = lax.dot_general(wqv_ref[2], hb, nt, preferred_element_type=F32)
        vtf_ref[0] = vt
        vt = vt.astype(BF16)
        for c in range(rows // ATTN_TILE):
            qt_ref[c] = qt[:, c * ATTN_TILE:(c + 1) * ATTN_TILE]
            vt_ref[c] = vt[:, c * ATTN_TILE:(c + 1) * ATTN_TILE]
    else:
        k_ref, v_ref, q_ref = out_refs[2:]
        k_ref[...] = k
        v_ref[...] = proj[:, D_SSM + D_ATTN:D_SSM + 2 * D_ATTN]
        q_ref[...] = lax.dot_general(hb, wqv_ref[0], nt, preferred_element_type=F32) * q_scale


def _inproj(x3, sc, sh, g, w_main, w_qv, b_f, nbb, rb, transposed, q_scale):
    nb, r, d = x3.shape
    t = nb * r
    rows = nbb * rb
    nj = r // rb
    grid = (nb // nbb, nj)
    row_map = lambda i, j: (i * nj + j, 0)
    n_main = w_main.shape[1]
    outs = [jax.ShapeDtypeStruct((t, D_SSM), F32), jax.ShapeDtypeStruct((t, LANES), F32)]
    out_specs = [pl.BlockSpec((rows, D_SSM), row_map), pl.BlockSpec((rows, LANES), row_map)]
    if transposed:
        assert nbb == 1, "transposed outputs are laid out per sequence"
        nc = rows // ATTN_TILE
        chunk_map = lambda i, j: (i * nj + j, 0, 0)
        seq_map = lambda i, j: (i, 0, j)
        outs += [jax.ShapeDtypeStruct((t, D_ATTN), BF16),
                 jax.ShapeDtypeStruct((t // ATTN_TILE, D_ATTN, ATTN_TILE), BF16),
                 jax.ShapeDtypeStruct((t // ATTN_TILE, D_ATTN, ATTN_TILE), BF16),
                 jax.ShapeDtypeStruct((nb, D_ATTN, r), F32),
                 jax.ShapeDtypeStruct((nb, D_ATTN, r), F32)]
        out_specs += [pl.BlockSpec((rows, D_ATTN), row_map),
                      pl.BlockSpec((nc, D_ATTN, ATTN_TILE), chunk_map),
                      pl.BlockSpec((nc, D_ATTN, ATTN_TILE), chunk_map),
                      pl.BlockSpec((1, D_ATTN, rows), seq_map),
                      pl.BlockSpec((1, D_ATTN, rows), seq_map)]
    else:
        outs += [jax.ShapeDtypeStruct((t, D_ATTN), F32)] * 3
        out_specs += [pl.BlockSpec((rows, D_ATTN), row_map)] * 3
    return pl.pallas_call(
        functools.partial(_inproj_kernel, transposed=transposed, q_scale=q_scale),
        grid=grid,
        in_specs=[pl.BlockSpec((nbb, rb, d), lambda i, j: (i, j, 0)),
                  pl.BlockSpec((nbb, 1, d), lambda i, j: (i, 0, 0)),
                  pl.BlockSpec((nbb, 1, d), lambda i, j: (i, 0, 0)),
                  pl.BlockSpec((1, 1, d), lambda i, j: (0, 0, 0)),
                  pl.BlockSpec((d, n_main), lambda i, j: (0, 0)),
                  pl.BlockSpec((3, D_ATTN, d), lambda i, j: (0, 0, 0)),
                  pl.BlockSpec((1, LANES), lambda i, j: (0, 0))],
        out_specs=out_specs,
        out_shape=outs,
        compiler_params=_cparams(("arbitrary", "arbitrary")),
    )(x3, sc, sh, g, w_main, w_qv, b_f)


def _crep_kernel(lf_ref, tri_ref, ex_ref, o_ref, carry_ref, *, scale):
    j = pl.program_id(1)

    @pl.when(j == 0)
    def _():
        carry_ref[...] = jnp.zeros_like(carry_ref)

    tri = tri_ref[...]
    ex = ex_ref[...]
    cs = sum(jnp.dot(tri, p, preferred_element_type=F32) for p in _split3(lf_ref[...]))
    rep = sum(jnp.dot(p, ex, preferred_element_type=F32) for p in _split3(cs))
    rep = rep + carry_ref[...]
    rows = rep.shape[0]
    carry_ref[...] = rep[rows - 1:rows, :]
    for h in range(N_HEADS):
        o_ref[0, h] = rep[:, h * LANES:(h + 1) * LANES] * (-scale)


def _crep(lf128, nb, length, scale):
    rows = ATTN_TILE
    nj = length // rows
    tri = np.tril(np.ones((rows, rows), np.float32))
    ex = np.zeros((LANES, N_HEADS * LANES), np.float32)
    for h in range(N_HEADS):
        ex[h, h * LANES:(h + 1) * LANES] = 1.0
    return pl.pallas_call(
        functools.partial(_crep_kernel, scale=scale),
        grid=(nb, nj),
        in_specs=[pl.BlockSpec((rows, LANES), lambda i, j: (i * nj + j, 0)),
                  pl.BlockSpec((rows, rows), lambda i, j: (0, 0)),
                  pl.BlockSpec((LANES, N_HEADS * LANES), lambda i, j: (0, 0))],
        out_specs=pl.BlockSpec((1, N_HEADS, rows, LANES), lambda i, j: (i, 0, j, 0)),
        out_shape=jax.ShapeDtypeStruct((nb, N_HEADS, length, LANES), F32),
        scratch_shapes=[pltpu.VMEM((1, N_HEADS * LANES), F32)],
        compiler_params=_cparams(("arbitrary", "arbitrary")),
    )(lf128, jnp.asarray(tri, BF16), jnp.asarray(ex, BF16))


def _cumsum_kernel(x_ref, tri_ref, lt_ref, o_ref):
    tri = tri_ref[...]
    lt = lt_ref[...]
    cs = sum(jnp.dot(p, tri, preferred_element_type=F32) for p in _split3(x_ref[...]))
    tot = jnp.broadcast_to(cs[:, LANES - 1:LANES], cs.shape)
    carry = sum(jnp.dot(lt, p, preferred_element_type=F32) for p in _split3(tot))
    o_ref[...] = -(cs + carry)


def _neg_cumsum(x2, gsz, rb):
    r = x2.shape[0]
    tri = np.triu(np.ones((LANES, LANES), np.float32))
    ii = np.arange(rb)
    lt = ((ii[:, None] // gsz == ii[None, :] // gsz) & (ii[None, :] < ii[:, None])).astype(np.float32)
    return pl.pallas_call(
        _cumsum_kernel,
        grid=(r // rb,),
        in_specs=[pl.BlockSpec((rb, LANES), lambda i: (i, 0)),
                  pl.BlockSpec((LANES, LANES), lambda i: (0, 0)),
                  pl.BlockSpec((rb, rb), lambda i: (0, 0))],
        out_specs=pl.BlockSpec((rb, LANES), lambda i: (i, 0)),
        out_shape=jax.ShapeDtypeStruct((r, LANES), F32),
        compiler_params=_cparams(("arbitrary",)),
    )(x2, jnp.asarray(tri, BF16), jnp.asarray(lt, BF16))


def _gelu_tanh(x):
    return 0.5 * x * (1.0 + jnp.tanh(math.sqrt(2.0 / math.pi) * (x + 0.044715 * (x * x * x))))


def _s5_kernel(u_ref, h0_ref, wb_ref, wc_ref, are_ref, aim_ref, dsk_ref, wglu_ref, bglu_ref, gout_ref,
               y_ref, ht_ref, s_ref, hc_ref, *, nseq, tm):
    rows = nseq * tm
    sr = rows + SUBLANES
    ti = pl.program_id(1)

    @pl.when(ti == 0)
    def _():
        hc_ref[...] = h0_ref[...]

    u = u_ref[...].reshape(rows, D_SSM)
    ub = u.astype(BF16)
    for c in range(4):
        bu = jnp.dot(ub[:, c * LANES:(c + 1) * LANES], wb_ref[c], preferred_element_type=F32)
        for jj in range(4):
            s_ref[pl.ds((4 * c + jj) * sr, rows), :] = bu[:, jj * LANES:(jj + 1) * LANES]
            s_ref[pl.ds((16 + 4 * c + jj) * sr, rows), :] = bu[:, 512 + jj * LANES:512 + (jj + 1) * LANES]

    ar = (are_ref[0:8, :], are_ref[8:16, :])
    ai = (aim_ref[0:8, :], aim_ref[8:16, :])

    def seq_group(sg, carry):
        base = sg * 4
        hs = []
        for b in range(4):
            hs.append(tuple(hc_ref[base + b, pl.ds(8 * q, 8), :] for q in range(4)))

        def step(t, hs):
            new = []
            for b in range(4):
                row = (base + b) * tm + t
                hr0, hr1, hi0, hi1 = hs[b]
                bre0 = s_ref[pl.ds(row, 8, stride=sr), :]
                bre1 = s_ref[pl.ds(8 * sr + row, 8, stride=sr), :]
                bim0 = s_ref[pl.ds(16 * sr + row, 8, stride=sr), :]
                bim1 = s_ref[pl.ds(24 * sr + row, 8, stride=sr), :]
                nr0 = ar[0] * hr0 - ai[0] * hi0 + bre0
                nr1 = ar[1] * hr1 - ai[1] * hi1 + bre1
                ni0 = ar[0] * hi0 + ai[0] * hr0 + bim0
                ni1 = ar[1] * hi1 + ai[1] * hr1 + bim1
                s_ref[pl.ds(row, 8, stride=sr), :] = nr0
                s_ref[pl.ds(8 * sr + row, 8, stride=sr), :] = nr1
                s_ref[pl.ds(16 * sr + row, 8, stride=sr), :] = ni0
                s_ref[pl.ds(24 * sr + row, 8, stride=sr), :] = ni1
                new.append((nr0, nr1, ni0, ni1))
            return tuple(new)

        hs = lax.fori_loop(0, tm, step, tuple(hs), unroll=2)
        for b in range(4):
            for q in range(4):
                hc_ref[base + b, pl.ds(8 * q, 8), :] = hs[b][q]
        return carry

    lax.fori_loop(0, nseq // 4, seq_group, 0)
    ht_ref[...] = hc_ref[...]

    ys = []
    for c in range(4):
        blocks = [s_ref[pl.ds((4 * c + jj) * sr, rows), :].astype(BF16) for jj in range(4)]
        blocks += [s_ref[pl.ds((16 + 4 * c + jj) * sr, rows), :].astype(BF16) for jj in range(4)]
        hcat = jnp.concatenate(blocks, axis=1)
        ys.append(jnp.dot(hcat, wc_ref[c], preferred_element_type=F32))
    y = jnp.concatenate(ys, axis=1) + dsk_ref[...] * u
    y = _gelu_tanh(y)
    gate = jax.nn.sigmoid(jnp.dot(y.astype(BF16), wglu_ref[...], preferred_element_type=F32) + bglu_ref[...])
    y = y * gate
    y_ref[...] = _rms(y, gout_ref[...]).astype(BF16).reshape(y_ref.shape)


def _s5(u3, h0, wb, wc, a_re, a_im, dsk, wglu, bglu, gout, nseq, tm):
    nb, length, _ = u3.shape
    rows = nseq * tm
    sr = rows + SUBLANES
    grid = (nb // nseq, length // tm)
    const2 = lambda i, j: (0, 0)
    const3 = lambda i, j: (0, 0, 0)
    if tm == length:
        y_spec = pl.BlockSpec((rows, D_SSM), lambda i, j: (i, 0))
        y_shape = jax.ShapeDtypeStruct((nb * length, D_SSM), BF16)
    else:
        y_spec = pl.BlockSpec((nseq, tm, D_SSM), lambda i, j: (i, j, 0))
        y_shape = jax.ShapeDtypeStruct((nb, length, D_SSM), BF16)
    return pl.pallas_call(
        functools.partial(_s5_kernel, nseq=nseq, tm=tm),
        grid=grid,
        in_specs=[pl.BlockSpec((nseq, tm, D_SSM), lambda i, j: (i, j, 0)),
                  pl.BlockSpec((nseq, 32, LANES), lambda i, j: (i, 0, 0)),
                  pl.BlockSpec((4, LANES, 1024), const3),
                  pl.BlockSpec((4, 1024, LANES), const3),
                  pl.BlockSpec((16, LANES), const2),
                  pl.BlockSpec((16, LANES), const2),
                  pl.BlockSpec((1, D_SSM), const2),
                  pl.BlockSpec((D_SSM, D_SSM), const2),
                  pl.BlockSpec((1, D_SSM), const2),
                  pl.BlockSpec((1, D_SSM), const2)],
        out_specs=[y_spec, pl.BlockSpec((nseq, 32, LANES), lambda i, j: (i, 0, 0))],
        out_shape=[y_shape, jax.ShapeDtypeStruct((nb, 32, LANES), F32)],
        scratch_shapes=[pltpu.VMEM((32 * sr, LANES), F32), pltpu.VMEM((nseq, 32, LANES), F32)],
        compiler_params=_cparams(("arbitrary", "arbitrary")),
    )(u3, h0, wb, wc, a_re, a_im, dsk, wglu, bglu, gout)


def _s5_params(lam_re, lam_im, log_step, b_re, b_im, c_re, c_im):
    g = lam_re.shape[0]
    dt = jnp.exp(log_step)[:, None]
    mag = jnp.exp(lam_re * dt)
    a_re = mag * jnp.cos(lam_im * dt)
    a_im = mag * jnp.sin(lam_im * dt)
    den = lam_re * lam_re + lam_im * lam_im
    n_re = a_re - 1.0
    f_re = (n_re * lam_re + a_im * lam_im) / den
    f_im = (a_im * lam_re - n_re * lam_im) / den
    bb_re = f_re[..., None] * b_re - f_im[..., None] * b_im
    bb_im = f_re[..., None] * b_im + f_im[..., None] * b_re
    eye = jnp.eye(g, dtype=F32)
    n_state = g * SSM_STATE

    def in_mat(bb):
        return jnp.einsum('gpi,gh->gihp', bb, eye).reshape(g * SSM_GROUP, n_state)

    def out_mat(cc):
        return jnp.einsum('gip,gh->gphi', cc, eye).reshape(n_state, g * SSM_GROUP)

    wbr, wbi = in_mat(bb_re), in_mat(bb_im)
    wcr, wci = out_mat(c_re), out_mat(-c_im)
    wb = jnp.stack([jnp.concatenate([wbr[c * 128:(c + 1) * 128, c * 512:(c + 1) * 512],
                                     wbi[c * 128:(c + 1) * 128, c * 512:(c + 1) * 512]], axis=1)
                    for c in range(4)]).astype(BF16)
    wc = jnp.stack([jnp.concatenate([wcr[c * 512:(c + 1) * 512, c * 128:(c + 1) * 128],
                                     wci[c * 512:(c + 1) * 512, c * 128:(c + 1) * 128]], axis=0)
                    for c in range(4)]).astype(BF16)
    return wb, wc, a_re.reshape(16, LANES), a_im.reshape(16, LANES)


def _attn_kernel(qt_ref, k_ref, vt_ref, cr_ref, g_ref, o_ref, qz_ref, m_ref, l_ref, acc_ref, st_ref):
    tq = ATTN_TILE
    qi = pl.program_id(1)
    half = LANES // 2

    rowid = lax.broadcasted_iota(I32, (LANES, tq), 0)
    for j in range(N_HEADS // 2):
        qp = qt_ref[0, j * LANES:(j + 1) * LANES, :]
        qz_ref[2 * j] = jnp.where(rowid < half, qp, jnp.zeros_like(qp))
        qz_ref[2 * j + 1] = jnp.where(rowid >= half, qp, jnp.zeros_like(qp))
    m_ref[...] = jnp.full_like(m_ref, NEG_INF)
    l_ref[...] = jnp.zeros_like(l_ref)
    acc_ref[...] = jnp.zeros_like(acc_ref)

    key_row = lax.broadcasted_iota(I32, (tq, tq), 0)
    q_col = lax.broadcasted_iota(I32, (tq, tq), 1)
    causal = key_row <= q_col

    def tiles(kbs, masked):
        m_new = [m_ref[h:h + 1, :] for h in range(N_HEADS)]
        for c, kb in enumerate(kbs):
            ks = pl.multiple_of(kb * tq, tq)
            for j in range(N_HEADS // 2):
                kp = k_ref[0, pl.ds(ks, tq), j * LANES:(j + 1) * LANES]
                for e in range(2):
                    h = 2 * j + e
                    st = jnp.dot(kp, qz_ref[h], preferred_element_type=F32)
                    bias = cr_ref[0, h, pl.ds(ks, tq), :]
                    st = st + jnp.concatenate([bias] * (tq // LANES), axis=1)
                    if masked:
                        st = jnp.where(causal, st, NEG_INF)
                    st_ref[c, h] = st
                    m_new[h] = jnp.maximum(m_new[h], jnp.max(st, axis=0, keepdims=True))
        for h in range(N_HEADS):
            alpha = jnp.exp2(m_ref[h:h + 1, :] - m_new[h])
            rows = slice(h * HEAD_DIM, (h + 1) * HEAD_DIM)
            l_new = alpha * l_ref[h:h + 1, :]
            acc = alpha * acc_ref[rows, :]
            for c, kb in enumerate(kbs):
                p = jnp.exp2(st_ref[c, h] - m_new[h])
                l_new = l_new + jnp.sum(p, axis=0, keepdims=True)
                acc = acc + jnp.dot(vt_ref[kb, rows, :], p.astype(BF16), preferred_element_type=F32)
            l_ref[h:h + 1, :] = l_new
            m_ref[h:h + 1, :] = m_new[h]
            acc_ref[rows, :] = acc

    def body(g, c):
        tiles([g * ATTN_GROUP + i for i in range(ATTN_GROUP)], False)
        return c

    n_groups = qi // ATTN_GROUP
    lax.fori_loop(0, n_groups, body, 0)
    for r in range(1, ATTN_GROUP):
        @pl.when(qi - n_groups * ATTN_GROUP == r)
        def _(r=r):
            tiles([n_groups * ATTN_GROUP + i for i in range(r)], False)
    tiles([qi], True)

    for h in range(N_HEADS):
        rows = slice(h * HEAD_DIM, (h + 1) * HEAD_DIM)
        acc_ref[rows, :] = acc_ref[rows, :] / l_ref[h:h + 1, :]
    o = acc_ref[...].T
    o_ref[0] = _rms(o, g_ref[...]).astype(BF16)


def _attn_prompt(qt, k, vt, crep, g):
    b, length, d = k.shape
    tq = ATTN_TILE
    nq = length // tq
    once = pl.Buffered(1)
    return pl.pallas_call(
        _attn_kernel,
        grid=(b, nq),
        in_specs=[pl.BlockSpec((1, d, tq), lambda i, j: (i * nq + j, 0, 0)),
                  pl.BlockSpec((1, length, d), lambda i, j: (i, 0, 0), pipeline_mode=once),
                  pl.BlockSpec((nq, d, tq), lambda i, j: (i, 0, 0), pipeline_mode=once),
                  pl.BlockSpec((1, N_HEADS, length, LANES), lambda i, j: (i, 0, 0, 0), pipeline_mode=once),
                  pl.BlockSpec((1, d), lambda i, j: (0, 0))],
        out_specs=pl.BlockSpec((1, tq, d), lambda i, j: (i, j, 0)),
        out_shape=jax.ShapeDtypeStruct((b, length, d), BF16),
        scratch_shapes=[pltpu.VMEM((N_HEADS, LANES, tq), BF16), pltpu.VMEM((N_HEADS, tq), F32),
                        pltpu.VMEM((N_HEADS, tq), F32), pltpu.VMEM((d, tq), F32),
                        pltpu.VMEM((ATTN_GROUP, N_HEADS, tq, tq), F32)],
        compiler_params=_cparams(("arbitrary", "arbitrary")),
    )(qt, k, vt, crep, g)


def _attn_paged_kernel(pt_ref, q_ref, kn_ref, vn_ref, cn_ref, g_ref, kc_hbm, vc_hbm, o_ref,
                       kbuf, vbuf, sem, *, n_pages, n_new):
    i = pl.program_id(0)
    n = pl.num_programs(0)
    n_past = n_pages * PAGE

    def page_copies(seq, slot):
        cps = []
        for p in range(n_pages):
            pg = pt_ref[seq * n_pages + p]
            cps.append(pltpu.make_async_copy(kc_hbm.at[pg], kbuf.at[slot, p], sem.at[0, slot]))
            cps.append(pltpu.make_async_copy(vc_hbm.at[pg], vbuf.at[slot, p], sem.at[1, slot]))
        return cps

    @pl.when(i == 0)
    def _():
        for cp in page_copies(0, 0):
            cp.start()

    @pl.when(i + 1 < n)
    def _():
        for cp in page_copies(i + 1, (i + 1) % 2):
            cp.start()

    slot = i % 2
    for cp in page_copies(i, slot):
        cp.wait()

    d = q_ref.shape[-1]
    nr = N_HEADS * n_new
    new_bits = n_new.bit_length() - 1
    head_bits = HEAD_DIM.bit_length() - 1
    rowh = lax.shift_right_logical(lax.broadcasted_iota(I32, (nr, d), 0), new_bits)
    colh = lax.shift_right_logical(lax.broadcasted_iota(I32, (nr, d), 1), head_bits)
    bd = rowh == colh
    qrep = jnp.broadcast_to(q_ref[0][None], (N_HEADS, n_new, d)).reshape(nr, d)
    qbd = jnp.where(bd, qrep, 0.0).astype(BF16)
    cn = cn_ref[0]
    cnr = jnp.broadcast_to(cn[:, None, :], (N_HEADS, n_new, cn.shape[-1])).reshape(nr, cn.shape[-1])

    s_p = []
    for p in range(n_pages):
        kt = kbuf[slot, p].reshape(d, PAGE)
        s_p.append(_bdot(qbd, kt) + cnr[:, p * PAGE:(p + 1) * PAGE])
    s_n = _bdot_nt(qbd, kn_ref[0]) + cnr[:, n_past:n_past + n_new]
    qpos = lax.broadcasted_iota(I32, (nr, n_new), 0) & (n_new - 1)
    kpos = lax.broadcasted_iota(I32, (nr, n_new), 1)
    s_n = jnp.where(kpos <= qpos, s_n, NEG_INF)
    m = jnp.max(s_n, axis=1, keepdims=True)
    for sp in s_p:
        m = jnp.maximum(m, jnp.max(sp, axis=1, keepdims=True))
    p_n = jnp.exp(s_n - m)
    l = jnp.sum(p_n, axis=1, keepdims=True)
    of = _bdot(p_n, vn_ref[0])
    for p in range(n_pages):
        pp = jnp.exp(s_p[p] - m)
        l = l + jnp.sum(pp, axis=1, keepdims=True)
        of = of + _bdot_nt(pp, vbuf[slot, p].reshape(d, PAGE))
    of = jnp.where(bd, of / l, 0.0)
    o = jnp.sum(of.reshape(N_HEADS, n_new, d), axis=0)
    o_ref[0] = _rms(o, g_ref[...])


def _attn_paged(page_table, q, kn, vn, cn, g, cache_k, cache_v):
    nseq, n_new, d = q.shape
    assert n_new & (n_new - 1) == 0, "new-token count must be a power of two"
    n_pages = page_table.shape[1]
    n_past = n_pages * PAGE
    grid_spec = pltpu.PrefetchScalarGridSpec(
        num_scalar_prefetch=1,
        grid=(nseq,),
        in_specs=[pl.BlockSpec((1, n_new, d), lambda i, pt: (i, 0, 0)),
                  pl.BlockSpec((1, n_new, d), lambda i, pt: (i, 0, 0)),
                  pl.BlockSpec((1, n_new, d), lambda i, pt: (i, 0, 0)),
                  pl.BlockSpec((1, N_HEADS, cn.shape[-1]), lambda i, pt: (i, 0, 0)),
                  pl.BlockSpec((1, d), lambda i, pt: (0, 0)),
                  pl.BlockSpec(memory_space=pl.ANY),
                  pl.BlockSpec(memory_space=pl.ANY)],
        out_specs=pl.BlockSpec((1, n_new, d), lambda i, pt: (i, 0, 0)),
        scratch_shapes=[pltpu.VMEM((2, n_pages, N_HEADS, HEAD_DIM, PAGE), F32),
                        pltpu.VMEM((2, n_pages, N_HEADS, HEAD_DIM, PAGE), F32),
                        pltpu.SemaphoreType.DMA((2, 2))],
    )
    return pl.pallas_call(
        functools.partial(_attn_paged_kernel, n_pages=n_pages, n_new=n_new),
        grid_spec=grid_spec,
        out_shape=jax.ShapeDtypeStruct((nseq, n_new, d), F32),
        compiler_params=_cparams(("arbitrary",)),
    )(page_table.reshape(-1), q, kn, vn, cn, g, cache_k, cache_v)


def _outproj_kernel(x_ref, ys_ref, oa_ref, gt_ref, sc_ref, sh_ref, g_ref, wo_ref, wrt_ref,
                    x1_ref, h2_ref, lg_ref):
    nbb, rb, d = x_ref.shape
    mix = jnp.dot(ys_ref[...], wo_ref[0:D_SSM, :], preferred_element_type=F32)
    mix = mix + jnp.dot(oa_ref[...], wo_ref[D_SSM:, :], preferred_element_type=F32)
    x1 = x_ref[...] + gt_ref[...] * mix.reshape(nbb, rb, d)
    x1_ref[...] = x1
    h2 = (_rms(x1, g_ref[...]) * (1.0 + sc_ref[...]) + sh_ref[...]).reshape(nbb * rb, d)
    hb = h2.astype(BF16)
    h2_ref[...] = h2
    hlo = (h2 - hb.astype(F32)).astype(BF16)
    whi = wrt_ref[0]
    wlo = wrt_ref[1]
    nt = (((1,), (1,)), ((), ()))
    lg = lax.dot_general(whi, hb, nt, preferred_element_type=F32)
    lg = lg + lax.dot_general(wlo, hb, nt, preferred_element_type=F32)
    lg = lg + lax.dot_general(whi, hlo, nt, preferred_element_type=F32)
    lg_ref[...] = lg


def _outproj(x3, ys, oa, gt, sc, sh, g, wo, wrt, nbb, rb):
    nb, r, d = x3.shape
    t = nb * r
    rows = nbb * rb
    nj = r // rb
    row_map = lambda i, j: (i * nj + j, 0)
    mod_spec = pl.BlockSpec((nbb, 1, d), lambda i, j: (i, 0, 0))
    return pl.pallas_call(
        _outproj_kernel,
        grid=(nb // nbb, nj),
        in_specs=[pl.BlockSpec((nbb, rb, d), lambda i, j: (i, j, 0)),
                  pl.BlockSpec((rows, D_SSM), row_map),
                  pl.BlockSpec((rows, D_ATTN), row_map),
                  mod_spec, mod_spec, mod_spec,
                  pl.BlockSpec((1, 1, d), lambda i, j: (0, 0, 0)),
                  pl.BlockSpec((D_SSM + D_ATTN, d), lambda i, j: (0, 0)),
                  pl.BlockSpec((2, N_EXPERTS, d), lambda i, j: (0, 0, 0))],
        out_specs=[pl.BlockSpec((nbb, rb, d), lambda i, j: (i, j, 0)),
                   pl.BlockSpec((rows, d), row_map),
                   pl.BlockSpec((N_EXPERTS, rows), lambda i, j: (0, i * nj + j))],
        out_shape=[jax.ShapeDtypeStruct((nb, r, d), F32),
                   jax.ShapeDtypeStruct((t, d), F32),
                   jax.ShapeDtypeStruct((N_EXPERTS, t), F32)],
        compiler_params=_cparams(("arbitrary", "arbitrary")),
    )(x3, ys, oa, gt, sc, sh, g, wo, wrt)


def _router_kernel(lg_ref, rb_ref, ut_ref, idx_ref, w_ref, rank_ref, cnt_ref, carry_ref):
    i = pl.program_id(0)
    tm = lg_ref.shape[1]
    per_group = N_EXPERTS // N_EXPERT_GROUPS

    @pl.when(i == 0)
    def _():
        carry_ref[...] = jnp.zeros_like(carry_ref)

    scores = jax.nn.sigmoid(lg_ref[...])
    biased = scores + rb_ref[...]
    blks, grp = [], []
    for g in range(N_EXPERT_GROUPS):
        blk = biased[g * per_group:(g + 1) * per_group, :]
        m1 = jnp.max(blk, axis=0, keepdims=True)
        eq = blk == m1
        n_eq = jnp.sum(jnp.where(eq, 1.0, 0.0), axis=0, keepdims=True)
        m2 = jnp.max(jnp.where(eq, NEG_INF, blk), axis=0, keepdims=True)
        blks.append(blk)
        grp.append(m1 + jnp.where(n_eq >= 2.0, m1, m2))
    masked = []
    for g in range(N_EXPERT_GROUPS):
        beaten = jnp.zeros((1, tm), F32)
        for o in range(N_EXPERT_GROUPS):
            if o == g:
                continue
            ahead = (grp[o] >= grp[g]) if o < g else (grp[o] > grp[g])
            beaten = beaten + jnp.where(ahead, 1.0, 0.0)
        masked.append(jnp.where(beaten < float(TOPK_GROUPS), blks[g], NEG_INF))
    work = jnp.concatenate(masked, axis=0)

    eid = lax.broadcasted_iota(I32, (N_EXPERTS, tm), 0)
    chosen = jnp.zeros((N_EXPERTS, tm), F32)
    idxs, ws, sels = [], [], []
    for _ in range(TOP_K):
        m = jnp.max(work, axis=0, keepdims=True)
        first = jnp.min(jnp.where(work == m, eid, N_EXPERTS), axis=0, keepdims=True)
        sel = eid == first
        idxs.append(first)
        ws.append(jnp.sum(jnp.where(sel, scores, 0.0), axis=0, keepdims=True))
        sels.append(sel)
        chosen = jnp.where(sel, 1.0, chosen)
        work = jnp.where(sel, NEG_INF, work)
    wsum = ws[0]
    for wk in ws[1:]:
        wsum = wsum + wk

    prefix = jnp.dot(chosen.astype(BF16), ut_ref[...], preferred_element_type=F32) + carry_ref[...]
    carry_ref[...] = carry_ref[...] + jnp.sum(chosen, axis=1, keepdims=True)
    cnt_ref[...] = carry_ref[...]

    idx_ref[...] = jnp.zeros_like(idx_ref)
    w_ref[...] = jnp.zeros_like(w_ref)
    rank_ref[...] = jnp.zeros_like(rank_ref)
    for k in range(TOP_K):
        idx_ref[k:k + 1, :] = idxs[k]
        w_ref[k:k + 1, :] = ws[k] / wsum * ROUTED_SCALE
        rank = jnp.sum(jnp.where(sels[k], prefix, 0.0), axis=0, keepdims=True)
        rank_ref[k:k + 1, :] = rank.astype(I32)


def _router(lg, router_bias, tm):
    e, t = lg.shape
    ut = np.triu(np.ones((tm, tm), np.float32), 1)
    tok_spec = pl.BlockSpec((SUBLANES, tm), lambda i: (0, i))
    return pl.pallas_call(
        _router_kernel,
        grid=(t // tm,),
        in_specs=[pl.BlockSpec((e, tm), lambda i: (0, i)),
                  pl.BlockSpec((e, 1), lambda i: (0, 0)),
                  pl.BlockSpec((tm, tm), lambda i: (0, 0))],
        out_specs=[tok_spec, tok_spec, tok_spec, pl.BlockSpec((e, 1), lambda i: (0, 0))],
        out_shape=[jax.ShapeDtypeStruct((SUBLANES, t), I32), jax.ShapeDtypeStruct((SUBLANES, t), F32),
                   jax.ShapeDtypeStruct((SUBLANES, t), I32), jax.ShapeDtypeStruct((e, 1), F32)],
        scratch_shapes=[pltpu.VMEM((e, 1), F32)],
        compiler_params=_cparams(("arbitrary",)),
    )(lg, router_bias.reshape(e, 1), jnp.asarray(ut, BF16))


def _experts_kernel(be_ref, nv_ref, new_ref, x_ref, wg_ref, wu_ref, wd_ref, o_ref, wgu_s, wd_s):
    i = pl.program_id(0)
    de = wg_ref.shape[-1]
    valid = i < nv_ref[0]

    @pl.when(valid & (new_ref[i] == 1))
    def _():
        wgu_s[:, :de] = wg_ref[0].astype(BF16)
        wgu_s[:, de:] = wu_ref[0].astype(BF16)
        wd_s[...] = wd_ref[0].astype(BF16)

    @pl.when(valid)
    def _():
        gu = jnp.dot(x_ref[...].astype(BF16), wgu_s[...], preferred_element_type=F32)
        g = gu[:, :de]
        a = (g * jax.nn.sigmoid(g) * gu[:, de:]).astype(BF16)
        o_ref[...] = jnp.dot(a, wd_s[...], preferred_element_type=F32)

    @pl.when(jnp.logical_not(valid))
    def _():
        o_ref[...] = jnp.zeros_like(o_ref)


def _experts(block_e, n_valid, block_new, xs, w_eg, w_eu, w_ed, m):
    n_rows, d = xs.shape
    de = w_eg.shape[-1]
    nblk = n_rows // m
    xmap = lambda i, be, nv, nw: (jnp.minimum(i, nv[0] - 1), 0)
    wmap = lambda i, be, nv, nw: (be[i], 0, 0)
    grid_spec = pltpu.PrefetchScalarGridSpec(
        num_scalar_prefetch=3,
        grid=(nblk,),
        in_specs=[pl.BlockSpec((m, d), xmap),
                  pl.BlockSpec((1, d, de), wmap),
                  pl.BlockSpec((1, d, de), wmap),
                  pl.BlockSpec((1, de, d), wmap)],
        out_specs=pl.BlockSpec((m, d), lambda i, be, nv, nw: (i, 0)),
        scratch_shapes=[pltpu.VMEM((d, 2 * de), BF16), pltpu.VMEM((de, d), BF16)],
    )
    return pl.pallas_call(
        _experts_kernel,
        grid_spec=grid_spec,
        out_shape=jax.ShapeDtypeStruct((n_rows, d), F32),
        compiler_params=_cparams(("arbitrary",)),
    )(block_e, n_valid, block_new, xs, w_eg, w_eu, w_ed)


def _row_copy(src, src_row, dst, dst_row, sem):
    return pltpu.make_async_copy(src.at[pl.ds(src_row, 1)], dst.at[pl.ds(dst_row, 1)], sem)


def _final_kernel(dest_ref, x1_ref, h2_ref, w_ref, gt_ref, sc_ref, sh_ref, g_ref, wgu_ref, wd_ref, yb_hbm,
                  y_ref, gbuf, sem):
    nbb, rb, d = x1_ref.shape
    rows = nbb * rb
    ds = wd_ref.shape[0]

    def issue(r, c):
        for k in range(TOP_K):
            _row_copy(yb_hbm, dest_ref[0, 0, r * TOP_K + k], gbuf.at[k], r, sem).start(priority=k % 2)
        return c

    lax.fori_loop(0, rows, issue, 0, unroll=2)

    gu = jnp.dot(h2_ref[...].astype(BF16), wgu_ref[...], preferred_element_type=F32)
    g = gu[:, :ds]
    a = (g * jax.nn.sigmoid(g) * gu[:, ds:]).astype(BF16)
    ff = jnp.dot(a, wd_ref[...], preferred_element_type=F32)

    for k in range(TOP_K):
        pltpu.make_async_copy(yb_hbm.at[pl.ds(0, rows)], gbuf.at[k], sem).wait()
    routed = gbuf[0] * w_ref[:, 0:1]
    for k in range(1, TOP_K):
        routed = routed + gbuf[k] * w_ref[:, k:k + 1]
    ff = routed + ff
    x2 = x1_ref[...] + gt_ref[...] * ff.reshape(nbb, rb, d)
    y_ref[...] = _rms(x2, g_ref[...]) * (1.0 + sc_ref[...]) + sh_ref[...]


def _final(dest3, x1, h2, wts, yb, gt, sc, sh, g, wgu, wd, nbb, rb):
    nb, r, d = x1.shape
    rows = nbb * rb
    nj = r // rb
    row_map = lambda i, j: (i * nj + j, 0)
    mod_spec = pl.BlockSpec((nbb, 1, d), lambda i, j: (i, 0, 0))
    return pl.pallas_call(
        _final_kernel,
        grid=(nb // nbb, nj),
        in_specs=[pl.BlockSpec((1, 1, rows * TOP_K), lambda i, j: (i * nj + j, 0, 0), memory_space=pltpu.SMEM),
                  pl.BlockSpec((nbb, rb, d), lambda i, j: (i, j, 0)),
                  pl.BlockSpec((rows, d), row_map),
                  pl.BlockSpec((rows, SUBLANES), row_map),
                  mod_spec, mod_spec, mod_spec,
                  pl.BlockSpec((1, 1, d), lambda i, j: (0, 0, 0)),
                  pl.BlockSpec(wgu.shape, lambda i, j: (0, 0)),
                  pl.BlockSpec(wd.shape, lambda i, j: (0, 0)),
                  pl.BlockSpec(memory_space=pl.ANY)],
        out_specs=pl.BlockSpec((nbb, rb, d), lambda i, j: (i, j, 0)),
        out_shape=jax.ShapeDtypeStruct((nb, r, d), F32),
        scratch_shapes=[pltpu.VMEM((TOP_K, rows, d), F32), pltpu.SemaphoreType.DMA(())],
        compiler_params=_cparams(("arbitrary", "arbitrary")),
    )(dest3, x1, h2, wts, gt, sc, sh, g, wgu, wd, yb)


def _dispatch_kernel(plo_ref, pln_ref, nv_ref, dest_ref, *rest, group_tiles, m):
    x_refs = rest[:len(group_tiles)]
    xs_hbm, sem, zsem, zbuf = rest[len(group_tiles):]
    rows = x_refs[0].shape[0]
    step = pl.program_id(0)

    nblk = xs_hbm.shape[0] // m

    def pad_copies(act):
        def single_rows(off, n):
            for r in range(SUBLANES - 1):
                @pl.when(r < n)
                def _(r=r):
                    act(_row_copy(zbuf, 0, xs_hbm, off + r, zsem))

        def per_expert(e, c):
            off = plo_ref[e]
            ln = pln_ref[e]
            head = jnp.minimum((-off) & (SUBLANES - 1), ln)
            single_rows(off, head)
            off = off + head
            ln = ln - head
            bit = m // 2
            while bit >= SUBLANES:
                has = (ln & bit) != 0

                @pl.when(has)
                def _(off=off, bit=bit):
                    dst = xs_hbm.at[pl.ds(pl.multiple_of(off, SUBLANES), bit)]
                    act(pltpu.make_async_copy(zbuf.at[pl.ds(0, bit)], dst, zsem))

                off = off + jnp.where(has, bit, 0)
                bit //= 2
            single_rows(off, ln & (SUBLANES - 1))
            return c

        lax.fori_loop(0, N_EXPERTS, per_expert, 0)

        def per_block(b, c):
            @pl.when(b >= nv_ref[0])
            def _():
                act(pltpu.make_async_copy(zbuf, xs_hbm.at[pl.ds(pl.multiple_of(b * m, m), m)], zsem))

            return c

        lax.fori_loop(0, nblk, per_block, 0)

    @pl.when(step == 0)
    def _():
        zbuf[...] = jnp.zeros_like(zbuf)
        pad_copies(lambda cp: cp.start())
        pad_copies(lambda cp: cp.wait())

    first = 0
    for x_ref, n_tiles in zip(x_refs, group_tiles):
        @pl.when((step >= first) & (step < first + n_tiles))
        def _(x_ref=x_ref):
            def issue(r, c):
                for k in range(TOP_K):
                    _row_copy(x_ref, r, xs_hbm, dest_ref[0, 0, r * TOP_K + k], sem).start(priority=k % 2)
                return c

            lax.fori_loop(0, rows, issue, 0, unroll=2)
            for k in range(TOP_K):
                pltpu.make_async_copy(x_ref, xs_hbm.at[pl.ds(0, rows)], sem).wait()

        first += n_tiles


def _dispatch(pad_lo, pad_len, n_valid, dest3, h2_groups, n_rows, m):
    d = h2_groups[0].shape[1]
    rows = dest3.shape[-1] // TOP_K
    group_tiles = tuple(h2.shape[0] // rows for h2 in h2_groups)
    in_specs = [pl.BlockSpec((1, 1, rows * TOP_K), lambda i, *_: (i, 0, 0), memory_space=pltpu.SMEM)]
    first = 0
    for n_tiles in group_tiles:
        tile_map = lambda i, *_, first=first, n_tiles=n_tiles: (jnp.clip(i - first, 0, n_tiles - 1), 0)
        in_specs.append(pl.BlockSpec((rows, d), tile_map))
        first += n_tiles
    grid_spec = pltpu.PrefetchScalarGridSpec(
        num_scalar_prefetch=3,
        grid=(sum(group_tiles),),
        in_specs=in_specs,
        out_specs=pl.BlockSpec(memory_space=pl.ANY),
        scratch_shapes=[pltpu.SemaphoreType.DMA(()), pltpu.SemaphoreType.DMA(()), pltpu.VMEM((m, d), F32)],
    )
    return pl.pallas_call(
        functools.partial(_dispatch_kernel, group_tiles=group_tiles, m=m),
        grid_spec=grid_spec,
        out_shape=jax.ShapeDtypeStruct((n_rows, d), F32),
        compiler_params=_cparams(("arbitrary",)),
    )(pad_lo, pad_len, n_valid, dest3, *h2_groups)


def _moe(h2_groups, lg, router_bias, w_eg, w_eu, w_ed, tile):
    t = lg.shape[1]
    e = N_EXPERTS
    m = MOE_ROWS
    idx_t, w_t, rank_t, counts = _router(lg, router_bias, tile)
    counts = counts.reshape(e).astype(I32)
    padded = ((counts + m - 1) // m) * m
    pad_end = jnp.cumsum(padded)
    pad_start = pad_end - padded
    onehot = idx_t[:, :, None] == jnp.arange(e, dtype=I32)
    dest_t = jnp.sum(jnp.where(onehot, pad_start, 0), axis=-1) + rank_t
    dest3 = dest_t[:TOP_K].T.reshape(t // tile, 1, tile * TOP_K)
    n_rows = (-(-(t * TOP_K) // m)) * m + e * m
    nblk = n_rows // m
    block_start = jnp.arange(nblk, dtype=I32) * m
    block_e = jnp.minimum(jnp.sum(pad_end[None, :] <= block_start[:, None], axis=1), e - 1).astype(I32)
    block_new = jnp.concatenate([jnp.ones((1,), I32), (block_e[1:] != block_e[:-1]).astype(I32)])
    n_valid = (pad_end[-1] // m).astype(I32).reshape(1)
    xs = _dispatch(pad_start + counts, padded - counts, n_valid, dest3, h2_groups, n_rows, m)
    yb = _experts(block_e, n_valid, block_new, xs, w_eg, w_eu, w_ed, m)
    return dest3, w_t.T, yb


def kernel(x_prompt, x_sample, c_prompt, c_sample, cache_k, cache_v, cache_logf, state_ssm_re, state_ssm_im, page_table, w_ada, b_ada, g_norm1, w_in, b_fgate, ssm_lambda_re, ssm_lambda_im, ssm_log_step, ssm_b_re, ssm_b_im, ssm_c_re, ssm_c_im, ssm_d, w_glu, b_glu, g_ssm_out, g_attn_out, w_out, g_norm2, w_router, router_bias, w_exp_gate, w_exp_up, w_exp_down, w_sh_gate, w_sh_up, w_sh_down, g_final, w_ada_final, b_ada_final):
    depth = w_ada.shape[0]
    assert depth == 1, "one layer is supported"
    bp, lp, d = x_prompt.shape
    bs, ls, _ = x_sample.shape
    n_pages = page_table.shape[1]
    n_past = n_pages * PAGE
    n_groups = ssm_lambda_re.shape[1]

    n_c = bp + bs
    n_c_pad = -(-n_c // SUBLANES) * SUBLANES
    c_all = jnp.concatenate([c_prompt, c_sample, jnp.zeros((n_c_pad - n_c, d), F32)], axis=0)
    mod = _adaln(c_all, w_ada[0], b_ada[0])
    modf = _adaln(c_all, w_ada_final, b_ada_final)

    def mods(lo, hi):
        parts = [mod[lo:hi, k * d:(k + 1) * d][:, None, :] for k in range(6)]
        parts += [modf[lo:hi, k * d:(k + 1) * d][:, None, :] for k in range(2)]
        return parts

    w_u, w_q, w_k, w_v, w_f = jnp.split(w_in[0], [D_SSM, D_SSM + D_ATTN, D_SSM + 2 * D_ATTN,
                                                   D_SSM + 3 * D_ATTN], axis=1)
    w_fpad = jnp.concatenate([w_f, jnp.zeros((d, LANES - N_HEADS), F32)], axis=1)
    w_main_s = jnp.concatenate([w_u, w_k, w_v, w_fpad], axis=1).astype(BF16)
    w_main_p = jnp.concatenate([w_u, w_k, w_fpad], axis=1).astype(BF16)
    w_qkv = jnp.stack([w_q.T, w_k.T, w_v.T]).astype(BF16)
    b_f = jnp.concatenate([b_fgate[0], jnp.zeros((LANES - N_HEADS,), F32)]).reshape(1, LANES)
    g1 = g_norm1[0].reshape(1, 1, d)
    g2 = g_norm2[0].reshape(1, 1, d)
    gf = g_final.reshape(1, 1, d)
    wb, wc, a_re, a_im = _s5_params(ssm_lambda_re[0], ssm_lambda_im[0], ssm_log_step[0], ssm_b_re[0],
                                    ssm_b_im[0], ssm_c_re[0], ssm_c_im[0])
    dsk = ssm_d[0].reshape(1, D_SSM)
    wglu = w_glu[0].astype(BF16)
    bglu = b_glu[0].reshape(1, D_SSM)
    g_so = g_ssm_out[0].reshape(1, D_SSM)
    g_ao = g_attn_out[0].reshape(1, D_ATTN)
    wo = w_out[0].astype(BF16)
    wr_t = w_router[0].T
    wr_hi = wr_t.astype(BF16)
    wrt = jnp.stack([wr_hi, (wr_t - wr_hi.astype(F32)).astype(BF16)])
    wgu = jnp.concatenate([w_sh_gate[0], w_sh_up[0]], axis=1).astype(BF16)
    wsd = w_sh_down[0].astype(BF16)

    def ssm_state(re, im):
        return jnp.concatenate([re.reshape(-1, 16, LANES), im.reshape(-1, 16, LANES)], axis=1)

    def split_state(ht):
        n = ht.shape[0]
        return (ht[:, :16].reshape(1, n, n_groups, SSM_STATE), ht[:, 16:].reshape(1, n, n_groups, SSM_STATE))

    tm = 512
    sh1, sc1, gt1, sh2, sc2, gt2_p, shf_p, scf_p = mods(0, bp)
    u, lf, kb, qt, vt, ktf, vtf = _inproj(x_prompt, sc1, sh1, g1, w_main_p, w_qkv, b_f, 1, tm, True,
                                          HEAD_DIM ** -0.5 * LOG2E)
    crep = _crep(lf, bp, lp, LOG2E)
    ys, ht = _s5(u.reshape(bp, lp, D_SSM), jnp.zeros((bp, 32, LANES), F32), wb, wc, a_re, a_im, dsk,
                 wglu, bglu, g_so, bp, 256)
    oa = _attn_prompt(qt, kb.reshape(bp, lp, D_ATTN), vt, crep, g_ao)
    x1_p, h2_p, lg_p = _outproj(x_prompt, ys.reshape(bp * lp, D_SSM), oa.reshape(bp * lp, D_ATTN),
                                gt1, sc2, sh2, g2, wo, wrt, 1, tm)
    k_prompt = ktf.reshape(bp, N_HEADS, HEAD_DIM, lp).transpose(0, 3, 1, 2)[None]
    v_prompt = vtf.reshape(bp, N_HEADS, HEAD_DIM, lp).transpose(0, 3, 1, 2)[None]
    logf_prompt = lf[:, :N_HEADS].reshape(1, bp, lp, N_HEADS)
    sre_p, sim_p = split_state(ht)

    nbb = 64
    sh1, sc1, gt1, sh2, sc2, gt2_s, shf_s, scf_s = mods(bp, bp + bs)
    u, lf, k, v, q = _inproj(x_sample, sc1, sh1, g1, w_main_s, w_qkv, b_f, nbb, ls, False, HEAD_DIM ** -0.5)
    ys, ht = _s5(u.reshape(bs, ls, D_SSM), ssm_state(state_ssm_re[0], state_ssm_im[0]), wb, wc, a_re, a_im,
                 dsk, wglu, bglu, g_so, 32, ls)
    lf_past = cache_logf[0][page_table].reshape(bs, n_past, N_HEADS).transpose(0, 2, 1)
    lf_new = lf[:, :N_HEADS].reshape(bs, ls, N_HEADS).transpose(0, 2, 1)
    n_key_pad = -(-(n_past + ls) // LANES) * LANES
    lf_all = jnp.concatenate([lf_past, lf_new, jnp.zeros((bs, N_HEADS, n_key_pad - n_past - ls), F32)], axis=2)
    gsz = n_key_pad // LANES
    cn_s = _neg_cumsum(lf_all.reshape(-1, LANES), gsz, gsz * 64).reshape(bs, N_HEADS, n_key_pad)
    oa = _attn_paged(page_table, q.reshape(bs, ls, D_ATTN), k.reshape(bs, ls, D_ATTN),
                     v.reshape(bs, ls, D_ATTN), cn_s, g_ao,
                     cache_k[0].transpose(0, 2, 3, 1), cache_v[0].transpose(0, 2, 3, 1))
    x1_s, h2_s, lg_s = _outproj(x_sample, ys, oa.reshape(bs * ls, D_ATTN).astype(BF16),
                                gt1, sc2, sh2, g2, wo, wrt, nbb, ls)

    assert nbb * ls == tm
    dest3, wts, yb = _moe([h2_p, h2_s], jnp.concatenate([lg_p, lg_s], axis=1), router_bias[0],
                          w_exp_gate[0], w_exp_up[0], w_exp_down[0], tm)
    tiles_p = bp * lp // tm
    y_prompt = _final(dest3[:tiles_p], x1_p, h2_p, wts[:bp * lp], yb, gt2_p, scf_p, shf_p, gf, wgu, wsd, 1, tm)
    y_sample = _final(dest3[tiles_p:], x1_s, h2_s, wts[bp * lp:], yb, gt2_s, scf_s, shf_s, gf, wgu, wsd, nbb, ls)
    k_sample = k.reshape(1, bs, ls, N_HEADS, HEAD_DIM)
    v_sample = v.reshape(1, bs, ls, N_HEADS, HEAD_DIM)
    logf_sample = lf[:, :N_HEADS].reshape(1, bs, ls, N_HEADS)
    sre_s, sim_s = split_state(ht)

    return (y_prompt, y_sample, k_prompt, v_prompt, logf_prompt, sre_p, sim_p,
            k_sample, v_sample, logf_sample, sre_s, sim_s)
```

```python
import functools
import math

import jax
import jax.numpy as jnp
import numpy as np
from jax import lax
from jax.experimental import pallas as pl
from jax.experimental.pallas import tpu as pltpu
from jax.experimental.pallas import tpu_sc as plsc

F32 = jnp.float32
BF16 = jnp.bfloat16
I32 = jnp.int32

EPS = 1e-6
HEAD_DIM = 64
N_HEADS = 8
D_SSM = 512
D_ATTN = 512
SSM_GROUP = 16
SSM_STATE = 64
N_EXPERTS = 64
TOP_K = 6
N_EXPERT_GROUPS = 8
TOPK_GROUPS = 4
ROUTED_SCALE = 2.5
PAGE = 128

LANES = 128
SUBLANES = 8
VMEM_LIMIT = 48 * 1024 * 1024
MOE_ROWS = 512
ATTN_TILE = 256
ATTN_GROUP = 4
LOG2E = math.log2(math.e)
SC_GATHER_WINDOW = 24
SC_GATHER_BUFFERS = 4
NEG_INF = float("-inf")


def _cparams(sem):
    return pltpu.CompilerParams(dimension_semantics=sem, vmem_limit_bytes=VMEM_LIMIT)


def _bdot(a, b):
    return jnp.dot(a.astype(BF16), b.astype(BF16), preferred_element_type=F32)


def _bdot_nt(a, b):
    return lax.dot_general(a.astype(BF16), b.astype(BF16), (((1,), (1,)), ((), ())),
                           preferred_element_type=F32)


def _split3(v):
    hi = v.astype(BF16)
    r1 = v - hi.astype(F32)
    mid = r1.astype(BF16)
    lo = (r1 - mid.astype(F32)).astype(BF16)
    return hi, mid, lo


def _rms(x, g):
    return x * lax.rsqrt(jnp.mean(x * x, axis=-1, keepdims=True) + EPS) * g


def _adaln_kernel(c_ref, w_ref, b_ref, o_ref):
    c = c_ref[...]
    s = c * jax.nn.sigmoid(c)
    o_ref[...] = _bdot(s, w_ref[...]) + b_ref[...]


def _adaln(c, w, b):
    m, k = c.shape
    n = w.shape[1]
    tn = 1024
    return pl.pallas_call(
        _adaln_kernel,
        grid=(n // tn,),
        in_specs=[pl.BlockSpec((m, k), lambda j: (0, 0)),
                  pl.BlockSpec((k, tn), lambda j: (0, j)),
                  pl.BlockSpec((1, tn), lambda j: (0, j))],
        out_specs=pl.BlockSpec((m, tn), lambda j: (0, j)),
        out_shape=jax.ShapeDtypeStruct((m, n), F32),
        compiler_params=_cparams(("arbitrary",)),
    )(c, w, b.reshape(1, n))


def _inproj_kernel(x_ref, sc_ref, sh_ref, g_ref, w_ref, wqv_ref, bf_ref, *out_refs, transposed, q_scale):
    nbb, rb, d = x_ref.shape
    rows = nbb * rb
    x = x_ref[...]
    h = _rms(x, g_ref[...]) * (1.0 + sc_ref[...]) + sh_ref[...]
    hb = h.reshape(rows, d).astype(BF16)
    proj = jnp.dot(hb, w_ref[...], preferred_element_type=F32)
    u_ref, lf_ref = out_refs[:2]
    u_ref[...] = proj[:, :D_SSM]
    k = proj[:, D_SSM:D_SSM + D_ATTN]
    z = proj[:, proj.shape[1] - LANES:] + bf_ref[...]
    lf_ref[...] = jnp.minimum(z, 0.0) - jnp.log1p(jnp.exp(-jnp.abs(z)))
    nt = (((1,), (1,)), ((), ()))
    if transposed:
        kb_ref, qt_ref, vt_ref, ktf_ref, vtf_ref = out_refs[2:]
        kb_ref[...] = k.astype(BF16)
        qt = (lax.dot_general(wqv_ref[0], hb, nt, preferred_element_type=F32) * q_scale).astype(BF16)
        ktf_ref[0] = lax.dot_general(wqv_ref[1], hb, nt, preferred_element_type=F32)
        vt = lax.dot_general(wqv_ref[2], hb, nt, preferred_element_type=F32)
        vtf_ref[0] = vt
        vt = vt.astype(BF16)
        for c in range(rows // ATTN_TILE):
            qt_ref[c] = qt[:, c * ATTN_TILE:(c + 1) * ATTN_TILE]
            vt_ref[c] = vt[:, c * ATTN_TILE:(c + 1) * ATTN_TILE]
    else:
        k_ref, v_ref, q_ref = out_refs[2:]
        k_ref[...] = k
        v_ref[...] = proj[:, D_SSM + D_ATTN:D_SSM + 2 * D_ATTN]
        q_ref[...] = lax.dot_general(hb, wqv_ref[0], nt, preferred_element_type=F32) * q_scale


def _inproj(x3, sc, sh, g, w_main, w_qv, b_f, nbb, rb, transposed, q_scale):
    nb, r, d = x3.shape
    t = nb * r
    rows = nbb * rb
    nj = r // rb
    grid = (nb // nbb, nj)
    row_map = lambda i, j: (i * nj + j, 0)
    n_main = w_main.shape[1]
    outs = [jax.ShapeDtypeStruct((t, D_SSM), F32), jax.ShapeDtypeStruct((t, LANES), F32)]
    out_specs = [pl.BlockSpec((rows, D_SSM), row_map), pl.BlockSpec((rows, LANES), row_map)]
    if transposed:
        assert nbb == 1, "transposed outputs are laid out per sequence"
        nc = rows // ATTN_TILE
        chunk_map = lambda i, j: (i * nj + j, 0, 0)
        seq_map = lambda i, j: (i, 0, j)
        outs += [jax.ShapeDtypeStruct((t, D_ATTN), BF16),
                 jax.ShapeDtypeStruct((t // ATTN_TILE, D_ATTN, ATTN_TILE), BF16),
                 jax.ShapeDtypeStruct((t // ATTN_TILE, D_ATTN, ATTN_TILE), BF16),
                 jax.ShapeDtypeStruct((nb, D_ATTN, r), F32),
                 jax.ShapeDtypeStruct((nb, D_ATTN, r), F32)]
        out_specs += [pl.BlockSpec((rows, D_ATTN), row_map),
                      pl.BlockSpec((nc, D_ATTN, ATTN_TILE), chunk_map),
                      pl.BlockSpec((nc, D_ATTN, ATTN_TILE), chunk_map),
                      pl.BlockSpec((1, D_ATTN, rows), seq_map),
                      pl.BlockSpec((1, D_ATTN, rows), seq_map)]
    else:
        outs += [jax.ShapeDtypeStruct((t, D_ATTN), F32)] * 3
        out_specs += [pl.BlockSpec((rows, D_ATTN), row_map)] * 3
    return pl.pallas_call(
        functools.partial(_inproj_kernel, transposed=transposed, q_scale=q_scale),
        grid=grid,
        in_specs=[pl.BlockSpec((nbb, rb, d), lambda i, j: (i, j, 0)),
                  pl.BlockSpec((nbb, 1, d), lambda i, j: (i, 0, 0)),
                  pl.BlockSpec((nbb, 1, d), lambda i, j: (i, 0, 0)),
                  pl.BlockSpec((1, 1, d), lambda i, j: (0, 0, 0)),
                  pl.BlockSpec((d, n_main), lambda i, j: (0, 0)),
                  pl.BlockSpec((3, D_ATTN, d), lambda i, j: (0, 0, 0)),
                  pl.BlockSpec((1, LANES), lambda i, j: (0, 0))],
        out_specs=out_specs,
        out_shape=outs,
        compiler_params=_cparams(("arbitrary", "arbitrary")),
    )(x3, sc, sh, g, w_main, w_qv, b_f)


def _crep_kernel(lf_ref, tri_ref, ex_ref, o_ref, carry_ref, *, scale):
    j = pl.program_id(1)

    @pl.when(j == 0)
    def _():
        carry_ref[...] = jnp.zeros_like(carry_ref)

    tri = tri_ref[...]
    ex = ex_ref[...]
    cs = sum(jnp.dot(tri, p, preferred_element_type=F32) for p in _split3(lf_ref[...]))
    rep = sum(jnp.dot(p, ex, preferred_element_type=F32) for p in _split3(cs))
    rep = rep + carry_ref[...]
    rows = rep.shape[0]
    carry_ref[...] = rep[rows - 1:rows, :]
    for h in range(N_HEADS):
        o_ref[0, h] = rep[:, h * LANES:(h + 1) * LANES] * (-scale)


def _crep(lf128, nb, length, scale):
    rows = ATTN_TILE
    nj = length // rows
    tri = np.tril(np.ones((rows, rows), np.float32))
    ex = np.zeros((LANES, N_HEADS * LANES), np.float32)
    for h in range(N_HEADS):
        ex[h, h * LANES:(h + 1) * LANES] = 1.0
    return pl.pallas_call(
        functools.partial(_crep_kernel, scale=scale),
        grid=(nb, nj),
        in_specs=[pl.BlockSpec((rows, LANES), lambda i, j: (i * nj + j, 0)),
                  pl.BlockSpec((rows, rows), lambda i, j: (0, 0)),
                  pl.BlockSpec((LANES, N_HEADS * LANES), lambda i, j: (0, 0))],
        out_specs=pl.BlockSpec((1, N_HEADS, rows, LANES), lambda i, j: (i, 0, j, 0)),
        out_shape=jax.ShapeDtypeStruct((nb, N_HEADS, length, LANES), F32),
        scratch_shapes=[pltpu.VMEM((1, N_HEADS * LANES), F32)],
        compiler_params=_cparams(("arbitrary", "arbitrary")),
    )(lf128, jnp.asarray(tri, BF16), jnp.asarray(ex, BF16))


def _cumsum_kernel(x_ref, tri_ref, lt_ref, o_ref):
    tri = tri_ref[...]
    lt = lt_ref[...]
    cs = sum(jnp.dot(p, tri, preferred_element_type=F32) for p in _split3(x_ref[...]))
    tot = jnp.broadcast_to(cs[:, LANES - 1:LANES], cs.shape)
    carry = sum(jnp.dot(lt, p, preferred_element_type=F32) for p in _split3(tot))
    o_ref[...] = -(cs + carry)


def _neg_cumsum(x2, gsz, rb):
    r = x2.shape[0]
    tri = np.triu(np.ones((LANES, LANES), np.float32))
    ii = np.arange(rb)
    lt = ((ii[:, None] // gsz == ii[None, :] // gsz) & (ii[None, :] < ii[:, None])).astype(np.float32)
    return pl.pallas_call(
        _cumsum_kernel,
        grid=(r // rb,),
        in_specs=[pl.BlockSpec((rb, LANES), lambda i: (i, 0)),
                  pl.BlockSpec((LANES, LANES), lambda i: (0, 0)),
                  pl.BlockSpec((rb, rb), lambda i: (0, 0))],
        out_specs=pl.BlockSpec((rb, LANES), lambda i: (i, 0)),
        out_shape=jax.ShapeDtypeStruct((r, LANES), F32),
        compiler_params=_cparams(("arbitrary",)),
    )(x2, jnp.asarray(tri, BF16), jnp.asarray(lt, BF16))


def _gelu_tanh(x):
    return 0.5 * x * (1.0 + jnp.tanh(math.sqrt(2.0 / math.pi) * (x + 0.044715 * (x * x * x))))


def _s5_kernel(u_ref, h0_ref, wb_ref, wc_ref, are_ref, aim_ref, dsk_ref, wglu_ref, bglu_ref, gout_ref,
               y_ref, ht_ref, s_ref, hc_ref, *, nseq, tm):
    rows = nseq * tm
    sr = rows + SUBLANES
    ti = pl.program_id(1)

    @pl.when(ti == 0)
    def _():
        hc_ref[...] = h0_ref[...]

    u = u_ref[...].reshape(rows, D_SSM)
    ub = u.astype(BF16)
    for c in range(4):
        bu = jnp.dot(ub[:, c * LANES:(c + 1) * LANES], wb_ref[c], preferred_element_type=F32)
        for jj in range(4):
            s_ref[pl.ds((4 * c + jj) * sr, rows), :] = bu[:, jj * LANES:(jj + 1) * LANES]
            s_ref[pl.ds((16 + 4 * c + jj) * sr, rows), :] = bu[:, 512 + jj * LANES:512 + (jj + 1) * LANES]

    ar = (are_ref[0:8, :], are_ref[8:16, :])
    ai = (aim_ref[0:8, :], aim_ref[8:16, :])

    def seq_group(sg, carry):
        base = sg * 4
        hs = []
        for b in range(4):
            hs.append(tuple(hc_ref[base + b, pl.ds(8 * q, 8), :] for q in range(4)))

        def step(t, hs):
            new = []
            for b in range(4):
                row = (base + b) * tm + t
                hr0, hr1, hi0, hi1 = hs[b]
                bre0 = s_ref[pl.ds(row, 8, stride=sr), :]
                bre1 = s_ref[pl.ds(8 * sr + row, 8, stride=sr), :]
                bim0 = s_ref[pl.ds(16 * sr + row, 8, stride=sr), :]
                bim1 = s_ref[pl.ds(24 * sr + row, 8, stride=sr), :]
                nr0 = ar[0] * hr0 - ai[0] * hi0 + bre0
                nr1 = ar[1] * hr1 - ai[1] * hi1 + bre1
                ni0 = ar[0] * hi0 + ai[0] * hr0 + bim0
                ni1 = ar[1] * hi1 + ai[1] * hr1 + bim1
                s_ref[pl.ds(row, 8, stride=sr), :] = nr0
                s_ref[pl.ds(8 * sr + row, 8, stride=sr), :] = nr1
                s_ref[pl.ds(16 * sr + row, 8, stride=sr), :] = ni0
                s_ref[pl.ds(24 * sr + row, 8, stride=sr), :] = ni1
                new.append((nr0, nr1, ni0, ni1))
            return tuple(new)

        hs = lax.fori_loop(0, tm, step, tuple(hs), unroll=2)
        for b in range(4):
            for q in range(4):
                hc_ref[base + b, pl.ds(8 * q, 8), :] = hs[b][q]
        return carry

    lax.fori_loop(0, nseq // 4, seq_group, 0)
    ht_ref[...] = hc_ref[...]

    ys = []
    for c in range(4):
        blocks = [s_ref[pl.ds((4 * c + jj) * sr, rows), :].astype(BF16) for jj in range(4)]
        blocks += [s_ref[pl.ds((16 + 4 * c + jj) * sr, rows), :].astype(BF16) for jj in range(4)]
        hcat = jnp.concatenate(blocks, axis=1)
        ys.append(jnp.dot(hcat, wc_ref[c], preferred_element_type=F32))
    y = jnp.concatenate(ys, axis=1) + dsk_ref[...] * u
    y = _gelu_tanh(y)
    gate = jax.nn.sigmoid(jnp.dot(y.astype(BF16), wglu_ref[...], preferred_element_type=F32) + bglu_ref[...])
    y = y * gate
    y_ref[...] = _rms(y, gout_ref[...]).astype(BF16).reshape(y_ref.shape)


def _s5(u3, h0, wb, wc, a_re, a_im, dsk, wglu, bglu, gout, nseq, tm):
    nb, length, _ = u3.shape
    rows = nseq * tm
    sr = rows + SUBLANES
    grid = (nb // nseq, length // tm)
    const2 = lambda i, j: (0, 0)
    const3 = lambda i, j: (0, 0, 0)
    if tm == length:
        y_spec = pl.BlockSpec((rows, D_SSM), lambda i, j: (i, 0))
        y_shape = jax.ShapeDtypeStruct((nb * length, D_SSM), BF16)
    else:
        y_spec = pl.BlockSpec((nseq, tm, D_SSM), lambda i, j: (i, j, 0))
        y_shape = jax.ShapeDtypeStruct((nb, length, D_SSM), BF16)
    return pl.pallas_call(
        functools.partial(_s5_kernel, nseq=nseq, tm=tm),
        grid=grid,
        in_specs=[pl.BlockSpec((nseq, tm, D_SSM), lambda i, j: (i, j, 0)),
                  pl.BlockSpec((nseq, 32, LANES), lambda i, j: (i, 0, 0)),
                  pl.BlockSpec((4, LANES, 1024), const3),
                  pl.BlockSpec((4, 1024, LANES), const3),
                  pl.BlockSpec((16, LANES), const2),
                  pl.BlockSpec((16, LANES), const2),
                  pl.BlockSpec((1, D_SSM), const2),
                  pl.BlockSpec((D_SSM, D_SSM), const2),
                  pl.BlockSpec((1, D_SSM), const2),
                  pl.BlockSpec((1, D_SSM), const2)],
        out_specs=[y_spec, pl.BlockSpec((nseq, 32, LANES), lambda i, j: (i, 0, 0))],
        out_shape=[y_shape, jax.ShapeDtypeStruct((nb, 32, LANES), F32)],
        scratch_shapes=[pltpu.VMEM((32 * sr, LANES), F32), pltpu.VMEM((nseq, 32, LANES), F32)],
        compiler_params=_cparams(("arbitrary", "arbitrary")),
    )(u3, h0, wb, wc, a_re, a_im, dsk, wglu, bglu, gout)


def _s5_params(lam_re, lam_im, log_step, b_re, b_im, c_re, c_im):
    g = lam_re.shape[0]
    dt = jnp.exp(log_step)[:, None]
    mag = jnp.exp(lam_re * dt)
    a_re = mag * jnp.cos(lam_im * dt)
    a_im = mag * jnp.sin(lam_im * dt)
    den = lam_re * lam_re + lam_im * lam_im
    n_re = a_re - 1.0
    f_re = (n_re * lam_re + a_im * lam_im) / den
    f_im = (a_im * lam_re - n_re * lam_im) / den
    bb_re = f_re[..., None] * b_re - f_im[..., None] * b_im
    bb_im = f_re[..., None] * b_im + f_im[..., None] * b_re
    eye = jnp.eye(g, dtype=F32)
    n_state = g * SSM_STATE

    def in_mat(bb):
        return jnp.einsum('gpi,gh->gihp', bb, eye).reshape(g * SSM_GROUP, n_state)

    def out_mat(cc):
        return jnp.einsum('gip,gh->gphi', cc, eye).reshape(n_state, g * SSM_GROUP)

    wbr, wbi = in_mat(bb_re), in_mat(bb_im)
    wcr, wci = out_mat(c_re), out_mat(-c_im)
    wb = jnp.stack([jnp.concatenate([wbr[c * 128:(c + 1) * 128, c * 512:(c + 1) * 512],
                                     wbi[c * 128:(c + 1) * 128, c * 512:(c + 1) * 512]], axis=1)
                    for c in range(4)]).astype(BF16)
    wc = jnp.stack([jnp.concatenate([wcr[c * 512:(c + 1) * 512, c * 128:(c + 1) * 128],
                                     wci[c * 512:(c + 1) * 512, c * 128:(c + 1) * 128]], axis=0)
                    for c in range(4)]).astype(BF16)
    return wb, wc, a_re.reshape(16, LANES), a_im.reshape(16, LANES)


def _attn_kernel(qt_ref, k_ref, vt_ref, cr_ref, g_ref, o_ref, qz_ref, m_ref, l_ref, acc_ref, st_ref):
    tq = ATTN_TILE
    qi = pl.program_id(1)
    half = LANES // 2

    rowid = lax.broadcasted_iota(I32, (LANES, tq), 0)
    for j in range(N_HEADS // 2):
        qp = qt_ref[0, j * LANES:(j + 1) * LANES, :]
        qz_ref[2 * j] = jnp.where(rowid < half, qp, jnp.zeros_like(qp))
        qz_ref[2 * j + 1] = jnp.where(rowid >= half, qp, jnp.zeros_like(qp))
    m_ref[...] = jnp.full_like(m_ref, NEG_INF)
    l_ref[...] = jnp.zeros_like(l_ref)
    acc_ref[...] = jnp.zeros_like(acc_ref)

    key_row = lax.broadcasted_iota(I32, (tq, tq), 0)
    q_col = lax.broadcasted_iota(I32, (tq, tq), 1)
    causal = key_row <= q_col

    def tiles(kbs, masked):
        m_new = [m_ref[h:h + 1, :] for h in range(N_HEADS)]
        for c, kb in enumerate(kbs):
            ks = pl.multiple_of(kb * tq, tq)
            for j in range(N_HEADS // 2):
                kp = k_ref[0, pl.ds(ks, tq), j * LANES:(j + 1) * LANES]
                for e in range(2):
                    h = 2 * j + e
                    st = jnp.dot(kp, qz_ref[h], preferred_element_type=F32)
                    bias = cr_ref[0, h, pl.ds(ks, tq), :]
                    st = st + jnp.concatenate([bias] * (tq // LANES), axis=1)
                    if masked:
                        st = jnp.where(causal, st, NEG_INF)
                    st_ref[c, h] = st
                    m_new[h] = jnp.maximum(m_new[h], jnp.max(st, axis=0, keepdims=True))
        for h in range(N_HEADS):
            alpha = jnp.exp2(m_ref[h:h + 1, :] - m_new[h])
            rows = slice(h * HEAD_DIM, (h + 1) * HEAD_DIM)
            l_new = alpha * l_ref[h:h + 1, :]
            acc = alpha * acc_ref[rows, :]
            for c, kb in enumerate(kbs):
                p = jnp.exp2(st_ref[c, h] - m_new[h])
                l_new = l_new + jnp.sum(p, axis=0, keepdims=True)
                acc = acc + jnp.dot(vt_ref[kb, rows, :], p.astype(BF16), preferred_element_type=F32)
            l_ref[h:h + 1, :] = l_new
            m_ref[h:h + 1, :] = m_new[h]
            acc_ref[rows, :] = acc

    def body(g, c):
        tiles([g * ATTN_GROUP + i for i in range(ATTN_GROUP)], False)
        return c

    n_groups = qi // ATTN_GROUP
    lax.fori_loop(0, n_groups, body, 0)
    for r in range(1, ATTN_GROUP):
        @pl.when(qi - n_groups * ATTN_GROUP == r)
        def _(r=r):
            tiles([n_groups * ATTN_GROUP + i for i in range(r)], False)
    tiles([qi], True)

    for h in range(N_HEADS):
        rows = slice(h * HEAD_DIM, (h + 1) * HEAD_DIM)
        acc_ref[rows, :] = acc_ref[rows, :] / l_ref[h:h + 1, :]
    o = acc_ref[...].T
    o_ref[0] = _rms(o, g_ref[...]).astype(BF16)


def _attn_prompt(qt, k, vt, crep, g):
    b, length, d = k.shape
    tq = ATTN_TILE
    nq = length // tq
    once = pl.Buffered(1)
    return pl.pallas_call(
        _attn_kernel,
        grid=(b, nq),
        in_specs=[pl.BlockSpec((1, d, tq), lambda i, j: (i * nq + j, 0, 0)),
                  pl.BlockSpec((1, length, d), lambda i, j: (i, 0, 0), pipeline_mode=once),
                  pl.BlockSpec((nq, d, tq), lambda i, j: (i, 0, 0), pipeline_mode=once),
                  pl.BlockSpec((1, N_HEADS, length, LANES), lambda i, j: (i, 0, 0, 0), pipeline_mode=once),
                  pl.BlockSpec((1, d), lambda i, j: (0, 0))],
        out_specs=pl.BlockSpec((1, tq, d), lambda i, j: (i, j, 0)),
        out_shape=jax.ShapeDtypeStruct((b, length, d), BF16),
        scratch_shapes=[pltpu.VMEM((N_HEADS, LANES, tq), BF16), pltpu.VMEM((N_HEADS, tq), F32),
                        pltpu.VMEM((N_HEADS, tq), F32), pltpu.VMEM((d, tq), F32),
                        pltpu.VMEM((ATTN_GROUP, N_HEADS, tq, tq), F32)],
        compiler_params=_cparams(("arbitrary", "arbitrary")),
    )(qt, k, vt, crep, g)


def _attn_paged_kernel(pt_ref, q_ref, kn_ref, vn_ref, cn_ref, g_ref, kc_hbm, vc_hbm, o_ref,
                       kbuf, vbuf, sem, *, n_pages, n_new):
    i = pl.program_id(0)
    n = pl.num_programs(0)
    n_past = n_pages * PAGE

    def page_copies(seq, slot):
        cps = []
        for p in range(n_pages):
            pg = pt_ref[seq * n_pages + p]
            cps.append(pltpu.make_async_copy(kc_hbm.at[pg], kbuf.at[slot, p], sem.at[0, slot]))
            cps.append(pltpu.make_async_copy(vc_hbm.at[pg], vbuf.at[slot, p], sem.at[1, slot]))
        return cps

    @pl.when(i == 0)
    def _():
        for cp in page_copies(0, 0):
            cp.start()

    @pl.when(i + 1 < n)
    def _():
        for cp in page_copies(i + 1, (i + 1) % 2):
            cp.start()

    slot = i % 2
    for cp in page_copies(i, slot):
        cp.wait()

    d = q_ref.shape[-1]
    nr = N_HEADS * n_new
    new_bits = n_new.bit_length() - 1
    head_bits = HEAD_DIM.bit_length() - 1
    rowh = lax.shift_right_logical(lax.broadcasted_iota(I32, (nr, d), 0), new_bits)
    colh = lax.shift_right_logical(lax.broadcasted_iota(I32, (nr, d), 1), head_bits)
    bd = rowh == colh
    qrep = jnp.broadcast_to(q_ref[0][None], (N_HEADS, n_new, d)).reshape(nr, d)
    qbd = jnp.where(bd, qrep, 0.0).astype(BF16)
    cn = cn_ref[0]
    cnr = jnp.broadcast_to(cn[:, None, :], (N_HEADS, n_new, cn.shape[-1])).reshape(nr, cn.shape[-1])

    s_p = []
    for p in range(n_pages):
        kt = kbuf[slot, p].reshape(d, PAGE)
        s_p.append(_bdot(qbd, kt) + cnr[:, p * PAGE:(p + 1) * PAGE])
    s_n = _bdot_nt(qbd, kn_ref[0]) + cnr[:, n_past:n_past + n_new]
    qpos = lax.broadcasted_iota(I32, (nr, n_new), 0) & (n_new - 1)
    kpos = lax.broadcasted_iota(I32, (nr, n_new), 1)
    s_n = jnp.where(kpos <= qpos, s_n, NEG_INF)
    m = jnp.max(s_n, axis=1, keepdims=True)
    for sp in s_p:
        m = jnp.maximum(m, jnp.max(sp, axis=1, keepdims=True))
    p_n = jnp.exp(s_n - m)
    l = jnp.sum(p_n, axis=1, keepdims=True)
    of = _bdot(p_n, vn_ref[0])
    for p in range(n_pages):
        pp = jnp.exp(s_p[p] - m)
        l = l + jnp.sum(pp, axis=1, keepdims=True)
        of = of + _bdot_nt(pp, vbuf[slot, p].reshape(d, PAGE))
    of = jnp.where(bd, of / l, 0.0)
    o = jnp.sum(of.reshape(N_HEADS, n_new, d), axis=0)
    o_ref[0] = _rms(o, g_ref[...])


def _attn_paged(page_table, q, kn, vn, cn, g, cache_k, cache_v):
    nseq, n_new, d = q.shape
    assert n_new & (n_new - 1) == 0, "new-token count must be a power of two"
    n_pages = page_table.shape[1]
    n_past = n_pages * PAGE
    grid_spec = pltpu.PrefetchScalarGridSpec(
        num_scalar_prefetch=1,
        grid=(nseq,),
        in_specs=[pl.BlockSpec((1, n_new, d), lambda i, pt: (i, 0, 0)),
                  pl.BlockSpec((1, n_new, d), lambda i, pt: (i, 0, 0)),
                  pl.BlockSpec((1, n_new, d), lambda i, pt: (i, 0, 0)),
                  pl.BlockSpec((1, N_HEADS, cn.shape[-1]), lambda i, pt: (i, 0, 0)),
                  pl.BlockSpec((1, d), lambda i, pt: (0, 0)),
                  pl.BlockSpec(memory_space=pl.ANY),
                  pl.BlockSpec(memory_space=pl.ANY)],
        out_specs=pl.BlockSpec((1, n_new, d), lambda i, pt: (i, 0, 0)),
        scratch_shapes=[pltpu.VMEM((2, n_pages, N_HEADS, HEAD_DIM, PAGE), F32),
                        pltpu.VMEM((2, n_pages, N_HEADS, HEAD_DIM, PAGE), F32),
                        pltpu.SemaphoreType.DMA((2, 2))],
    )
    return pl.pallas_call(
        functools.partial(_attn_paged_kernel, n_pages=n_pages, n_new=n_new),
        grid_spec=grid_spec,
        out_shape=jax.ShapeDtypeStruct((nseq, n_new, d), F32),
        compiler_params=_cparams(("arbitrary",)),
    )(page_table.reshape(-1), q, kn, vn, cn, g, cache_k, cache_v)


def _outproj_kernel(x_ref, ys_ref, oa_ref, gt_ref, sc_ref, sh_ref, g_ref, wo_ref, wrt_ref,
                    x1_ref, h2_ref, lg_ref):
    nbb, rb, d = x_ref.shape
    mix = jnp.dot(ys_ref[...], wo_ref[0:D_SSM, :], preferred_element_type=F32)
    mix = mix + jnp.dot(oa_ref[...], wo_ref[D_SSM:, :], preferred_element_type=F32)
    x1 = x_ref[...] + gt_ref[...] * mix.reshape(nbb, rb, d)
    x1_ref[...] = x1
    h2 = (_rms(x1, g_ref[...]) * (1.0 + sc_ref[...]) + sh_ref[...]).reshape(nbb * rb, d)
    hb = h2.astype(BF16)
    h2_ref[...] = h2
    hlo = (h2 - hb.astype(F32)).astype(BF16)
    whi = wrt_ref[0]
    wlo = wrt_ref[1]
    nt = (((1,), (1,)), ((), ()))
    lg = lax.dot_general(whi, hb, nt, preferred_element_type=F32)
    lg = lg + lax.dot_general(wlo, hb, nt, preferred_element_type=F32)
    lg = lg + lax.dot_general(whi, hlo, nt, preferred_element_type=F32)
    lg_ref[...] = lg


def _outproj(x3, ys, oa, gt, sc, sh, g, wo, wrt, nbb, rb):
    nb, r, d = x3.shape
    t = nb * r
    rows = nbb * rb
    nj = r // rb
    row_map = lambda i, j: (i * nj + j, 0)
    mod_spec = pl.BlockSpec((nbb, 1, d), lambda i, j: (i, 0, 0))
    return pl.pallas_call(
        _outproj_kernel,
        grid=(nb // nbb, nj),
        in_specs=[pl.BlockSpec((nbb, rb, d), lambda i, j: (i, j, 0)),
                  pl.BlockSpec((rows, D_SSM), row_map),
                  pl.BlockSpec((rows, D_ATTN), row_map),
                  mod_spec, mod_spec, mod_spec,
                  pl.BlockSpec((1, 1, d), lambda i, j: (0, 0, 0)),
                  pl.BlockSpec((D_SSM + D_ATTN, d), lambda i, j: (0, 0)),
                  pl.BlockSpec((2, N_EXPERTS, d), lambda i, j: (0, 0, 0))],
        out_specs=[pl.BlockSpec((nbb, rb, d), lambda i, j: (i, j, 0)),
                   pl.BlockSpec((rows, d), row_map),
                   pl.BlockSpec((N_EXPERTS, rows), lambda i, j: (0, i * nj + j))],
        out_shape=[jax.ShapeDtypeStruct((nb, r, d), F32),
                   jax.ShapeDtypeStruct((t, d), F32),
                   jax.ShapeDtypeStruct((N_EXPERTS, t), F32)],
        compiler_params=_cparams(("arbitrary", "arbitrary")),
    )(x3, ys, oa, gt, sc, sh, g, wo, wrt)


def _router_kernel(lg_ref, rb_ref, ut_ref, idx_ref, w_ref, rank_ref, cnt_ref, carry_ref):
    i = pl.program_id(0)
    tm = lg_ref.shape[1]
    per_group = N_EXPERTS // N_EXPERT_GROUPS

    @pl.when(i == 0)
    def _():
        carry_ref[...] = jnp.zeros_like(carry_ref)

    scores = jax.nn.sigmoid(lg_ref[...])
    biased = scores + rb_ref[...]
    blks, grp = [], []
    for g in range(N_EXPERT_GROUPS):
        blk = biased[g * per_group:(g + 1) * per_group, :]
        m1 = jnp.max(blk, axis=0, keepdims=True)
        eq = blk == m1
        n_eq = jnp.sum(jnp.where(eq, 1.0, 0.0), axis=0, keepdims=True)
        m2 = jnp.max(jnp.where(eq, NEG_INF, blk), axis=0, keepdims=True)
        blks.append(blk)
        grp.append(m1 + jnp.where(n_eq >= 2.0, m1, m2))
    masked = []
    for g in range(N_EXPERT_GROUPS):
        beaten = jnp.zeros((1, tm), F32)
        for o in range(N_EXPERT_GROUPS):
            if o == g:
                continue
            ahead = (grp[o] >= grp[g]) if o < g else (grp[o] > grp[g])
            beaten = beaten + jnp.where(ahead, 1.0, 0.0)
        masked.append(jnp.where(beaten < float(TOPK_GROUPS), blks[g], NEG_INF))
    work = jnp.concatenate(masked, axis=0)

    eid = lax.broadcasted_iota(I32, (N_EXPERTS, tm), 0)
    chosen = jnp.zeros((N_EXPERTS, tm), F32)
    idxs, ws, sels = [], [], []
    for _ in range(TOP_K):
        m = jnp.max(work, axis=0, keepdims=True)
        first = jnp.min(jnp.where(work == m, eid, N_EXPERTS), axis=0, keepdims=True)
        sel = eid == first
        idxs.append(first)
        ws.append(jnp.sum(jnp.where(sel, scores, 0.0), axis=0, keepdims=True))
        sels.append(sel)
        chosen = jnp.where(sel, 1.0, chosen)
        work = jnp.where(sel, NEG_INF, work)
    wsum = ws[0]
    for wk in ws[1:]:
        wsum = wsum + wk

    prefix = jnp.dot(chosen.astype(BF16), ut_ref[...], preferred_element_type=F32) + carry_ref[...]
    carry_ref[...] = carry_ref[...] + jnp.sum(chosen, axis=1, keepdims=True)
    cnt_ref[...] = carry_ref[...]

    idx_ref[...] = jnp.zeros_like(idx_ref)
    w_ref[...] = jnp.zeros_like(w_ref)
    rank_ref[...] = jnp.zeros_like(rank_ref)
    for k in range(TOP_K):
        idx_ref[k:k + 1, :] = idxs[k]
        w_ref[k:k + 1, :] = ws[k] / wsum * ROUTED_SCALE
        rank = jnp.sum(jnp.where(sels[k], prefix, 0.0), axis=0, keepdims=True)
        rank_ref[k:k + 1, :] = rank.astype(I32)


def _router(lg, router_bias, tm):
    e, t = lg.shape
    ut = np.triu(np.ones((tm, tm), np.float32), 1)
    tok_spec = pl.BlockSpec((SUBLANES, tm), lambda i: (0, i))
    return pl.pallas_call(
        _router_kernel,
        grid=(t // tm,),
        in_specs=[pl.BlockSpec((e, tm), lambda i: (0, i)),
                  pl.BlockSpec((e, 1), lambda i: (0, 0)),
                  pl.BlockSpec((tm, tm), lambda i: (0, 0))],
        out_specs=[tok_spec, tok_spec, tok_spec, pl.BlockSpec((e, 1), lambda i: (0, 0))],
        out_shape=[jax.ShapeDtypeStruct((SUBLANES, t), I32), jax.ShapeDtypeStruct((SUBLANES, t), F32),
                   jax.ShapeDtypeStruct((SUBLANES, t), I32), jax.ShapeDtypeStruct((e, 1), F32)],
        scratch_shapes=[pltpu.VMEM((e, 1), F32)],
        compiler_params=_cparams(("arbitrary",)),
    )(lg, router_bias.reshape(e, 1), jnp.asarray(ut, BF16))


def _experts_kernel(be_ref, nv_ref, new_ref, x_ref, wg_ref, wu_ref, wd_ref, o_ref, wgu_s, wd_s):
    i = pl.program_id(0)
    de = wg_ref.shape[-1]
    valid = i < nv_ref[0]

    @pl.when(valid & (new_ref[i] == 1))
    def _():
        wgu_s[:, :de] = wg_ref[0].astype(BF16)
        wgu_s[:, de:] = wu_ref[0].astype(BF16)
        wd_s[...] = wd_ref[0].astype(BF16)

    @pl.when(valid)
    def _():
        gu = jnp.dot(x_ref[...].astype(BF16), wgu_s[...], preferred_element_type=F32)
        g = gu[:, :de]
        a = (g * jax.nn.sigmoid(g) * gu[:, de:]).astype(BF16)
        o_ref[...] = jnp.dot(a, wd_s[...], preferred_element_type=F32)

    @pl.when(jnp.logical_not(valid))
    def _():
        o_ref[...] = jnp.zeros_like(o_ref)


def _experts(block_e, n_valid, block_new, xs, w_eg, w_eu, w_ed, m):
    n_rows, d = xs.shape
    de = w_eg.shape[-1]
    nblk = n_rows // m
    xmap = lambda i, be, nv, nw: (jnp.minimum(i, nv[0] - 1), 0)
    wmap = lambda i, be, nv, nw: (be[i], 0, 0)
    grid_spec = pltpu.PrefetchScalarGridSpec(
        num_scalar_prefetch=3,
        grid=(nblk,),
        in_specs=[pl.BlockSpec((m, d), xmap),
                  pl.BlockSpec((1, d, de), wmap),
                  pl.BlockSpec((1, d, de), wmap),
                  pl.BlockSpec((1, de, d), wmap)],
        out_specs=pl.BlockSpec((m, d), lambda i, be, nv, nw: (i, 0)),
        scratch_shapes=[pltpu.VMEM((d, 2 * de), BF16), pltpu.VMEM((de, d), BF16)],
    )
    return pl.pallas_call(
        _experts_kernel,
        grid_spec=grid_spec,
        out_shape=jax.ShapeDtypeStruct((n_rows, d), F32),
        compiler_params=_cparams(("arbitrary",)),
    )(block_e, n_valid, block_new, xs, w_eg, w_eu, w_ed)


def _row_copy(src, src_row, dst, dst_row, sem):
    return pltpu.make_async_copy(src.at[pl.ds(src_row, 1)], dst.at[pl.ds(dst_row, 1)], sem)


def _gather_rows(table, idx, window):
    n = idx.shape[0]
    d = table.shape[1]
    info = plsc.get_sparse_core_info()
    n_workers = info.num_cores * info.num_subcores
    per_worker = n // n_workers
    n_buf = SC_GATHER_BUFFERS
    n_rounds = per_worker // (window * n_buf)
    assert n_rounds * window * n_buf * n_workers == n
    mesh = plsc.VectorSubcoreMesh(core_axis_name="c", subcore_axis_name="s")

    @functools.partial(
        pl.kernel, out_type=jax.ShapeDtypeStruct((n, d), table.dtype), mesh=mesh,
        scratch_types=[pltpu.VMEM((per_worker,), I32), pltpu.VMEM((n_buf, window, d), table.dtype),
                       pltpu.SemaphoreType.DMA((n_buf,)), pltpu.SemaphoreType.DMA((n_buf,))])
    def gather(x_hbm, i_hbm, o_hbm, i_v, buf, gsem, wsem):
        base = (lax.axis_index("s") * info.num_cores + lax.axis_index("c")) * per_worker
        pltpu.sync_copy(i_hbm.at[pl.ds(base, per_worker)], i_v)

        @pl.loop(0, n_rounds)
        def _(rnd):
            first = rnd * (window * n_buf)
            reads, writes = [], []
            for b in range(n_buf):
                rows = pl.ds(first + b * window, window)
                reads.append(pltpu.make_async_copy(x_hbm.at[i_v.at[rows]], buf.at[b], gsem.at[b]))
                writes.append(pltpu.make_async_copy(buf.at[b], o_hbm.at[pl.ds(base + first + b * window, window)],
                                                    wsem.at[b]))
            for cp in reads:
                cp.start()
            for b in range(n_buf):
                reads[b].wait()
                writes[b].start()
            for cp in writes:
                cp.wait()

    return gather(table, idx)


def _final_kernel(x1_ref, h2_ref, w_ref, gt_ref, sc_ref, sh_ref, g_ref, wgu_ref, wd_ref, *rest):
    yg_refs = rest[:TOP_K]
    y_ref = rest[TOP_K]
    nbb, rb, d = x1_ref.shape
    ds = wd_ref.shape[0]
    gu = jnp.dot(h2_ref[...].astype(BF16), wgu_ref[...], preferred_element_type=F32)
    g = gu[:, :ds]
    a = (g * jax.nn.sigmoid(g) * gu[:, ds:]).astype(BF16)
    ff = jnp.dot(a, wd_ref[...], preferred_element_type=F32)
    routed = yg_refs[0][...] * w_ref[:, 0:1]
    for k in range(1, TOP_K):
        routed = routed + yg_refs[k][...] * w_ref[:, k:k + 1]
    ff = routed + ff
    x2 = x1_ref[...] + gt_ref[...] * ff.reshape(nbb, rb, d)
    y_ref[...] = _rms(x2, g_ref[...]) * (1.0 + sc_ref[...]) + sh_ref[...]


def _final(x1, h2, wts, yg, first_tile, gt, sc, sh, g, wgu, wd, nbb, rb):
    nb, r, d = x1.shape
    rows = nbb * rb
    nj = r // rb
    tiles_all = yg.shape[0] // TOP_K // rows
    row_map = lambda i, j: (i * nj + j, 0)
    mod_spec = pl.BlockSpec((nbb, 1, d), lambda i, j: (i, 0, 0))
    slab_specs = [pl.BlockSpec((rows, d), lambda i, j, k=k: (k * tiles_all + first_tile + i * nj + j, 0))
                  for k in range(TOP_K)]
    return pl.pallas_call(
        _final_kernel,
        grid=(nb // nbb, nj),
        in_specs=[pl.BlockSpec((nbb, rb, d), lambda i, j: (i, j, 0)),
                  pl.BlockSpec((rows, d), row_map),
                  pl.BlockSpec((rows, SUBLANES), row_map),
                  mod_spec, mod_spec, mod_spec,
                  pl.BlockSpec((1, 1, d), lambda i, j: (0, 0, 0)),
                  pl.BlockSpec(wgu.shape, lambda i, j: (0, 0)),
                  pl.BlockSpec(wd.shape, lambda i, j: (0, 0))] + slab_specs,
        out_specs=pl.BlockSpec((nbb, rb, d), lambda i, j: (i, j, 0)),
        out_shape=jax.ShapeDtypeStruct((nb, r, d), F32),
        compiler_params=_cparams(("arbitrary", "arbitrary")),
    )(x1, h2, wts, gt, sc, sh, g, wgu, wd, *([yg] * TOP_K))


def _dispatch_kernel(plo_ref, pln_ref, nv_ref, dest_ref, *rest, group_tiles, m):
    x_refs = rest[:len(group_tiles)]
    xs_hbm, sem, zsem, zbuf = rest[len(group_tiles):]
    rows = x_refs[0].shape[0]
    step = pl.program_id(0)

    nblk = xs_hbm.shape[0] // m

    def pad_copies(act):
        def single_rows(off, n):
            for r in range(SUBLANES - 1):
                @pl.when(r < n)
                def _(r=r):
                    act(_row_copy(zbuf, 0, xs_hbm, off + r, zsem))

        def per_expert(e, c):
            off = plo_ref[e]
            ln = pln_ref[e]
            head = jnp.minimum((-off) & (SUBLANES - 1), ln)
            single_rows(off, head)
            off = off + head
            ln = ln - head
            bit = m // 2
            while bit >= SUBLANES:
                has = (ln & bit) != 0

                @pl.when(has)
                def _(off=off, bit=bit):
                    dst = xs_hbm.at[pl.ds(pl.multiple_of(off, SUBLANES), bit)]
                    act(pltpu.make_async_copy(zbuf.at[pl.ds(0, bit)], dst, zsem))

                off = off + jnp.where(has, bit, 0)
                bit //= 2
            single_rows(off, ln & (SUBLANES - 1))
            return c

        lax.fori_loop(0, N_EXPERTS, per_expert, 0)

        def per_block(b, c):
            @pl.when(b >= nv_ref[0])
            def _():
                act(pltpu.make_async_copy(zbuf, xs_hbm.at[pl.ds(pl.multiple_of(b * m, m), m)], zsem))

            return c

        lax.fori_loop(0, nblk, per_block, 0)

    @pl.when(step == 0)
    def _():
        zbuf[...] = jnp.zeros_like(zbuf)
        pad_copies(lambda cp: cp.start())
        pad_copies(lambda cp: cp.wait())

    first = 0
    for x_ref, n_tiles in zip(x_refs, group_tiles):
        @pl.when((step >= first) & (step < first + n_tiles))
        def _(x_ref=x_ref):
            def issue(r, c):
                for k in range(TOP_K):
                    _row_copy(x_ref, r, xs_hbm, dest_ref[0, 0, r * TOP_K + k], sem).start(priority=k % 2)
                return c

            lax.fori_loop(0, rows, issue, 0, unroll=2)
            for k in range(TOP_K):
                pltpu.make_async_copy(x_ref, xs_hbm.at[pl.ds(0, rows)], sem).wait()

        first += n_tiles


def _dispatch(pad_lo, pad_len, n_valid, dest3, h2_groups, n_rows, m):
    d = h2_groups[0].shape[1]
    rows = dest3.shape[-1] // TOP_K
    group_tiles = tuple(h2.shape[0] // rows for h2 in h2_groups)
    in_specs = [pl.BlockSpec((1, 1, rows * TOP_K), lambda i, *_: (i, 0, 0), memory_space=pltpu.SMEM)]
    first = 0
    for n_tiles in group_tiles:
        tile_map = lambda i, *_, first=first, n_tiles=n_tiles: (jnp.clip(i - first, 0, n_tiles - 1), 0)
        in_specs.append(pl.BlockSpec((rows, d), tile_map))
        first += n_tiles
    grid_spec = pltpu.PrefetchScalarGridSpec(
        num_scalar_prefetch=3,
        grid=(sum(group_tiles),),
        in_specs=in_specs,
        out_specs=pl.BlockSpec(memory_space=pl.ANY),
        scratch_shapes=[pltpu.SemaphoreType.DMA(()), pltpu.SemaphoreType.DMA(()), pltpu.VMEM((m, d), F32)],
    )
    return pl.pallas_call(
        functools.partial(_dispatch_kernel, group_tiles=group_tiles, m=m),
        grid_spec=grid_spec,
        out_shape=jax.ShapeDtypeStruct((n_rows, d), F32),
        compiler_params=_cparams(("arbitrary",)),
    )(pad_lo, pad_len, n_valid, dest3, *h2_groups)


def _moe(h2_groups, lg, router_bias, w_eg, w_eu, w_ed, tile):
    t = lg.shape[1]
    e = N_EXPERTS
    m = MOE_ROWS
    idx_t, w_t, rank_t, counts = _router(lg, router_bias, tile)
    counts = counts.reshape(e).astype(I32)
    padded = ((counts + m - 1) // m) * m
    pad_end = jnp.cumsum(padded)
    pad_start = pad_end - padded
    onehot = idx_t[:, :, None] == jnp.arange(e, dtype=I32)
    dest_t = jnp.sum(jnp.where(onehot, pad_start, 0), axis=-1) + rank_t
    dest3 = dest_t[:TOP_K].T.reshape(t // tile, 1, tile * TOP_K)
    n_rows = (-(-(t * TOP_K) // m)) * m + e * m
    nblk = n_rows // m
    block_start = jnp.arange(nblk, dtype=I32) * m
    block_e = jnp.minimum(jnp.sum(pad_end[None, :] <= block_start[:, None], axis=1), e - 1).astype(I32)
    block_new = jnp.concatenate([jnp.ones((1,), I32), (block_e[1:] != block_e[:-1]).astype(I32)])
    n_valid = (pad_end[-1] // m).astype(I32).reshape(1)
    xs = _dispatch(pad_start + counts, padded - counts, n_valid, dest3, h2_groups, n_rows, m)
    yb = _experts(block_e, n_valid, block_new, xs, w_eg, w_eu, w_ed, m)
    yg = _gather_rows(yb, dest_t[:TOP_K].reshape(-1), SC_GATHER_WINDOW)
    return w_t.T, yg


def kernel(x_prompt, x_sample, c_prompt, c_sample, cache_k, cache_v, cache_logf, state_ssm_re, state_ssm_im, page_table, w_ada, b_ada, g_norm1, w_in, b_fgate, ssm_lambda_re, ssm_lambda_im, ssm_log_step, ssm_b_re, ssm_b_im, ssm_c_re, ssm_c_im, ssm_d, w_glu, b_glu, g_ssm_out, g_attn_out, w_out, g_norm2, w_router, router_bias, w_exp_gate, w_exp_up, w_exp_down, w_sh_gate, w_sh_up, w_sh_down, g_final, w_ada_final, b_ada_final):
    depth = w_ada.shape[0]
    assert depth == 1, "one layer is supported"
    bp, lp, d = x_prompt.shape
    bs, ls, _ = x_sample.shape
    n_pages = page_table.shape[1]
    n_past = n_pages * PAGE
    n_groups = ssm_lambda_re.shape[1]

    n_c = bp + bs
    n_c_pad = -(-n_c // SUBLANES) * SUBLANES
    c_all = jnp.concatenate([c_prompt, c_sample, jnp.zeros((n_c_pad - n_c, d), F32)], axis=0)
    mod = _adaln(c_all, w_ada[0], b_ada[0])
    modf = _adaln(c_all, w_ada_final, b_ada_final)

    def mods(lo, hi):
        parts = [mod[lo:hi, k * d:(k + 1) * d][:, None, :] for k in range(6)]
        parts += [modf[lo:hi, k * d:(k + 1) * d][:, None, :] for k in range(2)]
        return parts

    w_u, w_q, w_k, w_v, w_f = jnp.split(w_in[0], [D_SSM, D_SSM + D_ATTN, D_SSM + 2 * D_ATTN,
                                                   D_SSM + 3 * D_ATTN], axis=1)
    w_fpad = jnp.concatenate([w_f, jnp.zeros((d, LANES - N_HEADS), F32)], axis=1)
    w_main_s = jnp.concatenate([w_u, w_k, w_v, w_fpad], axis=1).astype(BF16)
    w_main_p = jnp.concatenate([w_u, w_k, w_fpad], axis=1).astype(BF16)
    w_qkv = jnp.stack([w_q.T, w_k.T, w_v.T]).astype(BF16)
    b_f = jnp.concatenate([b_fgate[0], jnp.zeros((LANES - N_HEADS,), F32)]).reshape(1, LANES)
    g1 = g_norm1[0].reshape(1, 1, d)
    g2 = g_norm2[0].reshape(1, 1, d)
    gf = g_final.reshape(1, 1, d)
    wb, wc, a_re, a_im = _s5_params(ssm_lambda_re[0], ssm_lambda_im[0], ssm_log_step[0], ssm_b_re[0],
                                    ssm_b_im[0], ssm_c_re[0], ssm_c_im[0])
    dsk = ssm_d[0].reshape(1, D_SSM)
    wglu = w_glu[0].astype(BF16)
    bglu = b_glu[0].reshape(1, D_SSM)
    g_so = g_ssm_out[0].reshape(1, D_SSM)
    g_ao = g_attn_out[0].reshape(1, D_ATTN)
    wo = w_out[0].astype(BF16)
    wr_t = w_router[0].T
    wr_hi = wr_t.astype(BF16)
    wrt = jnp.stack([wr_hi, (wr_t - wr_hi.astype(F32)).astype(BF16)])
    wgu = jnp.concatenate([w_sh_gate[0], w_sh_up[0]], axis=1).astype(BF16)
    wsd = w_sh_down[0].astype(BF16)

    def ssm_state(re, im):
        return jnp.concatenate([re.reshape(-1, 16, LANES), im.reshape(-1, 16, LANES)], axis=1)

    def split_state(ht):
        n = ht.shape[0]
        return (ht[:, :16].reshape(1, n, n_groups, SSM_STATE), ht[:, 16:].reshape(1, n, n_groups, SSM_STATE))

    tm = 512
    sh1, sc1, gt1, sh2, sc2, gt2_p, shf_p, scf_p = mods(0, bp)
    u, lf, kb, qt, vt, ktf, vtf = _inproj(x_prompt, sc1, sh1, g1, w_main_p, w_qkv, b_f, 1, tm, True,
                                          HEAD_DIM ** -0.5 * LOG2E)
    crep = _crep(lf, bp, lp, LOG2E)
    ys, ht = _s5(u.reshape(bp, lp, D_SSM), jnp.zeros((bp, 32, LANES), F32), wb, wc, a_re, a_im, dsk,
                 wglu, bglu, g_so, bp, 256)
    oa = _attn_prompt(qt, kb.reshape(bp, lp, D_ATTN), vt, crep, g_ao)
    x1_p, h2_p, lg_p = _outproj(x_prompt, ys.reshape(bp * lp, D_SSM), oa.reshape(bp * lp, D_ATTN),
                                gt1, sc2, sh2, g2, wo, wrt, 1, tm)
    k_prompt = ktf.reshape(bp, N_HEADS, HEAD_DIM, lp).transpose(0, 3, 1, 2)[None]
    v_prompt = vtf.reshape(bp, N_HEADS, HEAD_DIM, lp).transpose(0, 3, 1, 2)[None]
    logf_prompt = lf[:, :N_HEADS].reshape(1, bp, lp, N_HEADS)
    sre_p, sim_p = split_state(ht)

    nbb = 64
    sh1, sc1, gt1, sh2, sc2, gt2_s, shf_s, scf_s = mods(bp, bp + bs)
    u, lf, k, v, q = _inproj(x_sample, sc1, sh1, g1, w_main_s, w_qkv, b_f, nbb, ls, False, HEAD_DIM ** -0.5)
    ys, ht = _s5(u.reshape(bs, ls, D_SSM), ssm_state(state_ssm_re[0], state_ssm_im[0]), wb, wc, a_re, a_im,
                 dsk, wglu, bglu, g_so, 32, ls)
    lf_past = cache_logf[0][page_table].reshape(bs, n_past, N_HEADS).transpose(0, 2, 1)
    lf_new = lf[:, :N_HEADS].reshape(bs, ls, N_HEADS).transpose(0, 2, 1)
    n_key_pad = -(-(n_past + ls) // LANES) * LANES
    lf_all = jnp.concatenate([lf_past, lf_new, jnp.zeros((bs, N_HEADS, n_key_pad - n_past - ls), F32)], axis=2)
    gsz = n_key_pad // LANES
    cn_s = _neg_cumsum(lf_all.reshape(-1, LANES), gsz, gsz * 64).reshape(bs, N_HEADS, n_key_pad)
    oa = _attn_paged(page_table, q.reshape(bs, ls, D_ATTN), k.reshape(bs, ls, D_ATTN),
                     v.reshape(bs, ls, D_ATTN), cn_s, g_ao,
                     cache_k[0].transpose(0, 2, 3, 1), cache_v[0].transpose(0, 2, 3, 1))
    x1_s, h2_s, lg_s = _outproj(x_sample, ys, oa.reshape(bs * ls, D_ATTN).astype(BF16),
                                gt1, sc2, sh2, g2, wo, wrt, nbb, ls)

    assert nbb * ls == tm
    wts, yg = _moe([h2_p, h2_s], jnp.concatenate([lg_p, lg_s], axis=1), router_bias[0],
                   w_exp_gate[0], w_exp_up[0], w_exp_down[0], tm)
    tiles_p = bp * lp // tm
    y_prompt = _final(x1_p, h2_p, wts[:bp * lp], yg, 0, gt2_p, scf_p, shf_p, gf, wgu, wsd, 1, tm)
    y_sample = _final(x1_s, h2_s, wts[bp * lp:], yg, tiles_p, gt2_s, scf_s, shf_s, gf, wgu, wsd, nbb, ls)
    k_sample = k.reshape(1, bs, ls, N_HEADS, HEAD_DIM)
    v_sample = v.reshape(1, bs, ls, N_HEADS, HEAD_DIM)
    logf_sample = lf[:, :N_HEADS].reshape(1, bs, ls, N_HEADS)
    sre_s, sim_s = split_state(ht)

    return (y_prompt, y_sample, k_prompt, v_prompt, logf_prompt, sre_p, sim_p,
            k_sample, v_sample, logf_sample, sre_s, sim_s)
```

```python
import functools
import math

import jax
import jax.numpy as jnp
import numpy as np
from jax import lax
from jax.experimental import pallas as pl
from jax.experimental.pallas import tpu as pltpu
from jax.experimental.pallas import tpu_sc as plsc

F32 = jnp.float32
BF16 = jnp.bfloat16
I32 = jnp.int32

EPS = 1e-6
HEAD_DIM = 64
N_HEADS = 8
D_SSM = 512
D_ATTN = 512
SSM_GROUP = 16
SSM_STATE = 64
N_EXPERTS = 64
TOP_K = 6
N_EXPERT_GROUPS = 8
TOPK_GROUPS = 4
ROUTED_SCALE = 2.5
PAGE = 128

LANES = 128
SUBLANES = 8
VMEM_LIMIT = 48 * 1024 * 1024
MOE_ROWS = 512
ATTN_TILE = 256
ATTN_GROUP = 4
LOG2E = math.log2(math.e)
SC_SCATTER_WINDOW = 16
SC_GATHER_WINDOW = 24
SC_GATHER_BUFFERS = 4
NEG_INF = float("-inf")


def _cparams(sem):
    return pltpu.CompilerParams(dimension_semantics=sem, vmem_limit_bytes=VMEM_LIMIT)


def _bdot(a, b):
    return jnp.dot(a.astype(BF16), b.astype(BF16), preferred_element_type=F32)


def _bdot_nt(a, b):
    return lax.dot_general(a.astype(BF16), b.astype(BF16), (((1,), (1,)), ((), ())),
                           preferred_element_type=F32)


def _split3(v):
    hi = v.astype(BF16)
    r1 = v - hi.astype(F32)
    mid = r1.astype(BF16)
    lo = (r1 - mid.astype(F32)).astype(BF16)
    return hi, mid, lo


def _rms(x, g):
    return x * lax.rsqrt(jnp.mean(x * x, axis=-1, keepdims=True) + EPS) * g


def _adaln_kernel(c_ref, w_ref, b_ref, o_ref):
    c = c_ref[...]
    s = c * jax.nn.sigmoid(c)
    o_ref[...] = _bdot(s, w_ref[...]) + b_ref[...]


def _adaln(c, w, b):
    m, k = c.shape
    n = w.shape[1]
    tn = 1024
    return pl.pallas_call(
        _adaln_kernel,
        grid=(n // tn,),
        in_specs=[pl.BlockSpec((m, k), lambda j: (0, 0)),
                  pl.BlockSpec((k, tn), lambda j: (0, j)),
                  pl.BlockSpec((1, tn), lambda j: (0, j))],
        out_specs=pl.BlockSpec((m, tn), lambda j: (0, j)),
        out_shape=jax.ShapeDtypeStruct((m, n), F32),
        compiler_params=_cparams(("arbitrary",)),
    )(c, w, b.reshape(1, n))


def _inproj_kernel(x_ref, sc_ref, sh_ref, g_ref, w_ref, wqv_ref, bf_ref, *out_refs, transposed, q_scale):
    nbb, rb, d = x_ref.shape
    rows = nbb * rb
    x = x_ref[...]
    h = _rms(x, g_ref[...]) * (1.0 + sc_ref[...]) + sh_ref[...]
    hb = h.reshape(rows, d).astype(BF16)
    proj = jnp.dot(hb, w_ref[...], preferred_element_type=F32)
    u_ref, lf_ref = out_refs[:2]
    u_ref[...] = proj[:, :D_SSM]
    k = proj[:, D_SSM:D_SSM + D_ATTN]
    z = proj[:, proj.shape[1] - LANES:] + bf_ref[...]
    lf_ref[...] = jnp.minimum(z, 0.0) - jnp.log1p(jnp.exp(-jnp.abs(z)))
    nt = (((1,), (1,)), ((), ()))
    if transposed:
        kb_ref, qt_ref, vt_ref, ktf_ref, vtf_ref = out_refs[2:]
        kb_ref[...] = k.astype(BF16)
        qt = (lax.dot_general(wqv_ref[0], hb, nt, preferred_element_type=F32) * q_scale).astype(BF16)
        ktf_ref[0] = lax.dot_general(wqv_ref[1], hb, nt, preferred_element_type=F32)
        vt = lax.dot_general(wqv_ref[2], hb, nt, preferred_element_type=F32)
        vtf_ref[0] = vt
        vt = vt.astype(BF16)
        for c in range(rows // ATTN_TILE):
            qt_ref[c] = qt[:, c * ATTN_TILE:(c + 1) * ATTN_TILE]
            vt_ref[c] = vt[:, c * ATTN_TILE:(c + 1) * ATTN_TILE]
    else:
        k_ref, v_ref, q_ref = out_refs[2:]
        k_ref[...] = k
        v_ref[...] = proj[:, D_SSM + D_ATTN:D_SSM + 2 * D_ATTN]
        q_ref[...] = lax.dot_general(hb, wqv_ref[0], nt, preferred_element_type=F32) * q_scale


def _inproj(x3, sc, sh, g, w_main, w_qv, b_f, nbb, rb, transposed, q_scale):
    nb, r, d = x3.shape
    t = nb * r
    rows = nbb * rb
    nj = r // rb
    grid = (nb // nbb, nj)
    row_map = lambda i, j: (i * nj + j, 0)
    n_main = w_main.shape[1]
    outs = [jax.ShapeDtypeStruct((t, D_SSM), F32), jax.ShapeDtypeStruct((t, LANES), F32)]
    out_specs = [pl.BlockSpec((rows, D_SSM), row_map), pl.BlockSpec((rows, LANES), row_map)]
    if transposed:
        assert nbb == 1, "transposed outputs are laid out per sequence"
        nc = rows // ATTN_TILE
        chunk_map = lambda i, j: (i * nj + j, 0, 0)
        seq_map = lambda i, j: (i, 0, j)
        outs += [jax.ShapeDtypeStruct((t, D_ATTN), BF16),
                 jax.ShapeDtypeStruct((t // ATTN_TILE, D_ATTN, ATTN_TILE), BF16),
                 jax.ShapeDtypeStruct((t // ATTN_TILE, D_ATTN, ATTN_TILE), BF16),
                 jax.ShapeDtypeStruct((nb, D_ATTN, r), F32),
                 jax.ShapeDtypeStruct((nb, D_ATTN, r), F32)]
        out_specs += [pl.BlockSpec((rows, D_ATTN), row_map),
                      pl.BlockSpec((nc, D_ATTN, ATTN_TILE), chunk_map),
                      pl.BlockSpec((nc, D_ATTN, ATTN_TILE), chunk_map),
                      pl.BlockSpec((1, D_ATTN, rows), seq_map),
                      pl.BlockSpec((1, D_ATTN, rows), seq_map)]
    else:
        outs += [jax.ShapeDtypeStruct((t, D_ATTN), F32)] * 3
        out_specs += [pl.BlockSpec((rows, D_ATTN), row_map)] * 3
    return pl.pallas_call(
        functools.partial(_inproj_kernel, transposed=transposed, q_scale=q_scale),
        grid=grid,
        in_specs=[pl.BlockSpec((nbb, rb, d), lambda i, j: (i, j, 0)),
                  pl.BlockSpec((nbb, 1, d), lambda i, j: (i, 0, 0)),
                  pl.BlockSpec((nbb, 1, d), lambda i, j: (i, 0, 0)),
                  pl.BlockSpec((1, 1, d), lambda i, j: (0, 0, 0)),
                  pl.BlockSpec((d, n_main), lambda i, j: (0, 0)),
                  pl.BlockSpec((3, D_ATTN, d), lambda i, j: (0, 0, 0)),
                  pl.BlockSpec((1, LANES), lambda i, j: (0, 0))],
        out_specs=out_specs,
        out_shape=outs,
        compiler_params=_cparams(("arbitrary", "arbitrary")),
    )(x3, sc, sh, g, w_main, w_qv, b_f)


def _crep_kernel(lf_ref, tri_ref, ex_ref, o_ref, carry_ref, *, scale):
    j = pl.program_id(1)

    @pl.when(j == 0)
    def _():
        carry_ref[...] = jnp.zeros_like(carry_ref)

    tri = tri_ref[...]
    ex = ex_ref[...]
    cs = sum(jnp.dot(tri, p, preferred_element_type=F32) for p in _split3(lf_ref[...]))
    rep = sum(jnp.dot(p, ex, preferred_element_type=F32) for p in _split3(cs))
    rep = rep + carry_ref[...]
    rows = rep.shape[0]
    carry_ref[...] = rep[rows - 1:rows, :]
    for h in range(N_HEADS):
        o_ref[0, h] = rep[:, h * LANES:(h + 1) * LANES] * (-scale)


def _crep(lf128, nb, length, scale):
    rows = ATTN_TILE
    nj = length // rows
    tri = np.tril(np.ones((rows, rows), np.float32))
    ex = np.zeros((LANES, N_HEADS * LANES), np.float32)
    for h in range(N_HEADS):
        ex[h, h * LANES:(h + 1) * LANES] = 1.0
    return pl.pallas_call(
        functools.partial(_crep_kernel, scale=scale),
        grid=(nb, nj),
        in_specs=[pl.BlockSpec((rows, LANES), lambda i, j: (i * nj + j, 0)),
                  pl.BlockSpec((rows, rows), lambda i, j: (0, 0)),
                  pl.BlockSpec((LANES, N_HEADS * LANES), lambda i, j: (0, 0))],
        out_specs=pl.BlockSpec((1, N_HEADS, rows, LANES), lambda i, j: (i, 0, j, 0)),
        out_shape=jax.ShapeDtypeStruct((nb, N_HEADS, length, LANES), F32),
        scratch_shapes=[pltpu.VMEM((1, N_HEADS * LANES), F32)],
        compiler_params=_cparams(("arbitrary", "arbitrary")),
    )(lf128, jnp.asarray(tri, BF16), jnp.asarray(ex, BF16))


def _cumsum_kernel(x_ref, tri_ref, lt_ref, o_ref):
    tri = tri_ref[...]
    lt = lt_ref[...]
    cs = sum(jnp.dot(p, tri, preferred_element_type=F32) for p in _split3(x_ref[...]))
    tot = jnp.broadcast_to(cs[:, LANES - 1:LANES], cs.shape)
    carry = sum(jnp.dot(lt, p, preferred_element_type=F32) for p in _split3(tot))
    o_ref[...] = -(cs + carry)


def _neg_cumsum(x2, gsz, rb):
    r = x2.shape[0]
    tri = np.triu(np.ones((LANES, LANES), np.float32))
    ii = np.arange(rb)
    lt = ((ii[:, None] // gsz == ii[None, :] // gsz) & (ii[None, :] < ii[:, None])).astype(np.float32)
    return pl.pallas_call(
        _cumsum_kernel,
        grid=(r // rb,),
        in_specs=[pl.BlockSpec((rb, LANES), lambda i: (i, 0)),
                  pl.BlockSpec((LANES, LANES), lambda i: (0, 0)),
                  pl.BlockSpec((rb, rb), lambda i: (0, 0))],
        out_specs=pl.BlockSpec((rb, LANES), lambda i: (i, 0)),
        out_shape=jax.ShapeDtypeStruct((r, LANES), F32),
        compiler_params=_cparams(("arbitrary",)),
    )(x2, jnp.asarray(tri, BF16), jnp.asarray(lt, BF16))


def _gelu_tanh(x):
    return 0.5 * x * (1.0 + jnp.tanh(math.sqrt(2.0 / math.pi) * (x + 0.044715 * (x * x * x))))


def _s5_kernel(u_ref, h0_ref, wb_ref, wc_ref, are_ref, aim_ref, dsk_ref, wglu_ref, bglu_ref, gout_ref,
               y_ref, ht_ref, s_ref, hc_ref, *, nseq, tm):
    rows = nseq * tm
    sr = rows + SUBLANES
    ti = pl.program_id(1)

    @pl.when(ti == 0)
    def _():
        hc_ref[...] = h0_ref[...]

    u = u_ref[...].reshape(rows, D_SSM)
    ub = u.astype(BF16)
    for c in range(4):
        bu = jnp.dot(ub[:, c * LANES:(c + 1) * LANES], wb_ref[c], preferred_element_type=F32)
        for jj in range(4):
            s_ref[pl.ds((4 * c + jj) * sr, rows), :] = bu[:, jj * LANES:(jj + 1) * LANES]
            s_ref[pl.ds((16 + 4 * c + jj) * sr, rows), :] = bu[:, 512 + jj * LANES:512 + (jj + 1) * LANES]

    ar = (are_ref[0:8, :], are_ref[8:16, :])
    ai = (aim_ref[0:8, :], aim_ref[8:16, :])

    def seq_group(sg, carry):
        base = sg * 4
        hs = []
        for b in range(4):
            hs.append(tuple(hc_ref[base + b, pl.ds(8 * q, 8), :] for q in range(4)))

        def step(t, hs):
            new = []
            for b in range(4):
                row = (base + b) * tm + t
                hr0, hr1, hi0, hi1 = hs[b]
                bre0 = s_ref[pl.ds(row, 8, stride=sr), :]
                bre1 = s_ref[pl.ds(8 * sr + row, 8, stride=sr), :]
                bim0 = s_ref[pl.ds(16 * sr + row, 8, stride=sr), :]
                bim1 = s_ref[pl.ds(24 * sr + row, 8, stride=sr), :]
                nr0 = ar[0] * hr0 - ai[0] * hi0 + bre0
                nr1 = ar[1] * hr1 - ai[1] * hi1 + bre1
                ni0 = ar[0] * hi0 + ai[0] * hr0 + bim0
                ni1 = ar[1] * hi1 + ai[1] * hr1 + bim1
                s_ref[pl.ds(row, 8, stride=sr), :] = nr0
                s_ref[pl.ds(8 * sr + row, 8, stride=sr), :] = nr1
                s_ref[pl.ds(16 * sr + row, 8, stride=sr), :] = ni0
                s_ref[pl.ds(24 * sr + row, 8, stride=sr), :] = ni1
                new.append((nr0, nr1, ni0, ni1))
            return tuple(new)

        hs = lax.fori_loop(0, tm, step, tuple(hs), unroll=2)
        for b in range(4):
            for q in range(4):
                hc_ref[base + b, pl.ds(8 * q, 8), :] = hs[b][q]
        return carry

    lax.fori_loop(0, nseq // 4, seq_group, 0)
    ht_ref[...] = hc_ref[...]

    ys = []
    for c in range(4):
        blocks = [s_ref[pl.ds((4 * c + jj) * sr, rows), :].astype(BF16) for jj in range(4)]
        blocks += [s_ref[pl.ds((16 + 4 * c + jj) * sr, rows), :].astype(BF16) for jj in range(4)]
        hcat = jnp.concatenate(blocks, axis=1)
        ys.append(jnp.dot(hcat, wc_ref[c], preferred_element_type=F32))
    y = jnp.concatenate(ys, axis=1) + dsk_ref[...] * u
    y = _gelu_tanh(y)
    gate = jax.nn.sigmoid(jnp.dot(y.astype(BF16), wglu_ref[...], preferred_element_type=F32) + bglu_ref[...])
    y = y * gate
    y_ref[...] = _rms(y, gout_ref[...]).astype(BF16).reshape(y_ref.shape)


def _s5(u3, h0, wb, wc, a_re, a_im, dsk, wglu, bglu, gout, nseq, tm):
    nb, length, _ = u3.shape
    rows = nseq * tm
    sr = rows + SUBLANES
    grid = (nb // nseq, length // tm)
    const2 = lambda i, j: (0, 0)
    const3 = lambda i, j: (0, 0, 0)
    if tm == length:
        y_spec = pl.BlockSpec((rows, D_SSM), lambda i, j: (i, 0))
        y_shape = jax.ShapeDtypeStruct((nb * length, D_SSM), BF16)
    else:
        y_spec = pl.BlockSpec((nseq, tm, D_SSM), lambda i, j: (i, j, 0))
        y_shape = jax.ShapeDtypeStruct((nb, length, D_SSM), BF16)
    return pl.pallas_call(
        functools.partial(_s5_kernel, nseq=nseq, tm=tm),
        grid=grid,
        in_specs=[pl.BlockSpec((nseq, tm, D_SSM), lambda i, j: (i, j, 0)),
                  pl.BlockSpec((nseq, 32, LANES), lambda i, j: (i, 0, 0)),
                  pl.BlockSpec((4, LANES, 1024), const3),
                  pl.BlockSpec((4, 1024, LANES), const3),
                  pl.BlockSpec((16, LANES), const2),
                  pl.BlockSpec((16, LANES), const2),
                  pl.BlockSpec((1, D_SSM), const2),
                  pl.BlockSpec((D_SSM, D_SSM), const2),
                  pl.BlockSpec((1, D_SSM), const2),
                  pl.BlockSpec((1, D_SSM), const2)],
        out_specs=[y_spec, pl.BlockSpec((nseq, 32, LANES), lambda i, j: (i, 0, 0))],
        out_shape=[y_shape, jax.ShapeDtypeStruct((nb, 32, LANES), F32)],
        scratch_shapes=[pltpu.VMEM((32 * sr, LANES), F32), pltpu.VMEM((nseq, 32, LANES), F32)],
        compiler_params=_cparams(("arbitrary", "arbitrary")),
    )(u3, h0, wb, wc, a_re, a_im, dsk, wglu, bglu, gout)


def _s5_params(lam_re, lam_im, log_step, b_re, b_im, c_re, c_im):
    g = lam_re.shape[0]
    dt = jnp.exp(log_step)[:, None]
    mag = jnp.exp(lam_re * dt)
    a_re = mag * jnp.cos(lam_im * dt)
    a_im = mag * jnp.sin(lam_im * dt)
    den = lam_re * lam_re + lam_im * lam_im
    n_re = a_re - 1.0
    f_re = (n_re * lam_re + a_im * lam_im) / den
    f_im = (a_im * lam_re - n_re * lam_im) / den
    bb_re = f_re[..., None] * b_re - f_im[..., None] * b_im
    bb_im = f_re[..., None] * b_im + f_im[..., None] * b_re
    eye = jnp.eye(g, dtype=F32)
    n_state = g * SSM_STATE

    def in_mat(bb):
        return jnp.einsum('gpi,gh->gihp', bb, eye).reshape(g * SSM_GROUP, n_state)

    def out_mat(cc):
        return jnp.einsum('gip,gh->gphi', cc, eye).reshape(n_state, g * SSM_GROUP)

    wbr, wbi = in_mat(bb_re), in_mat(bb_im)
    wcr, wci = out_mat(c_re), out_mat(-c_im)
    wb = jnp.stack([jnp.concatenate([wbr[c * 128:(c + 1) * 128, c * 512:(c + 1) * 512],
                                     wbi[c * 128:(c + 1) * 128, c * 512:(c + 1) * 512]], axis=1)
                    for c in range(4)]).astype(BF16)
    wc = jnp.stack([jnp.concatenate([wcr[c * 512:(c + 1) * 512, c * 128:(c + 1) * 128],
                                     wci[c * 512:(c + 1) * 512, c * 128:(c + 1) * 128]], axis=0)
                    for c in range(4)]).astype(BF16)
    return wb, wc, a_re.reshape(16, LANES), a_im.reshape(16, LANES)


def _attn_kernel(qt_ref, k_ref, vt_ref, cr_ref, g_ref, o_ref, qz_ref, m_ref, l_ref, acc_ref, st_ref):
    tq = ATTN_TILE
    qi = pl.program_id(1)
    half = LANES // 2

    rowid = lax.broadcasted_iota(I32, (LANES, tq), 0)
    for j in range(N_HEADS // 2):
        qp = qt_ref[0, j * LANES:(j + 1) * LANES, :]
        qz_ref[2 * j] = jnp.where(rowid < half, qp, jnp.zeros_like(qp))
        qz_ref[2 * j + 1] = jnp.where(rowid >= half, qp, jnp.zeros_like(qp))
    m_ref[...] = jnp.full_like(m_ref, NEG_INF)
    l_ref[...] = jnp.zeros_like(l_ref)
    acc_ref[...] = jnp.zeros_like(acc_ref)

    key_row = lax.broadcasted_iota(I32, (tq, tq), 0)
    q_col = lax.broadcasted_iota(I32, (tq, tq), 1)
    causal = key_row <= q_col

    def tiles(kbs, masked):
        m_new = [m_ref[h:h + 1, :] for h in range(N_HEADS)]
        for c, kb in enumerate(kbs):
            ks = pl.multiple_of(kb * tq, tq)
            for j in range(N_HEADS // 2):
                kp = k_ref[0, pl.ds(ks, tq), j * LANES:(j + 1) * LANES]
                for e in range(2):
                    h = 2 * j + e
                    st = jnp.dot(kp, qz_ref[h], preferred_element_type=F32)
                    bias = cr_ref[0, h, pl.ds(ks, tq), :]
                    st = st + jnp.concatenate([bias] * (tq // LANES), axis=1)
                    if masked:
                        st = jnp.where(causal, st, NEG_INF)
                    st_ref[c, h] = st
                    m_new[h] = jnp.maximum(m_new[h], jnp.max(st, axis=0, keepdims=True))
        for h in range(N_HEADS):
            alpha = jnp.exp2(m_ref[h:h + 1, :] - m_new[h])
            rows = slice(h * HEAD_DIM, (h + 1) * HEAD_DIM)
            l_new = alpha * l_ref[h:h + 1, :]
            acc = alpha * acc_ref[rows, :]
            for c, kb in enumerate(kbs):
                p = jnp.exp2(st_ref[c, h] - m_new[h])
                l_new = l_new + jnp.sum(p, axis=0, keepdims=True)
                acc = acc + jnp.dot(vt_ref[kb, rows, :], p.astype(BF16), preferred_element_type=F32)
            l_ref[h:h + 1, :] = l_new
            m_ref[h:h + 1, :] = m_new[h]
            acc_ref[rows, :] = acc

    def body(g, c):
        tiles([g * ATTN_GROUP + i for i in range(ATTN_GROUP)], False)
        return c

    n_groups = qi // ATTN_GROUP
    lax.fori_loop(0, n_groups, body, 0)
    for r in range(1, ATTN_GROUP):
        @pl.when(qi - n_groups * ATTN_GROUP == r)
        def _(r=r):
            tiles([n_groups * ATTN_GROUP + i for i in range(r)], False)
    tiles([qi], True)

    for h in range(N_HEADS):
        rows = slice(h * HEAD_DIM, (h + 1) * HEAD_DIM)
        acc_ref[rows, :] = acc_ref[rows, :] / l_ref[h:h + 1, :]
    o = acc_ref[...].T
    o_ref[0] = _rms(o, g_ref[...]).astype(BF16)


def _attn_prompt(qt, k, vt, crep, g):
    b, length, d = k.shape
    tq = ATTN_TILE
    nq = length // tq
    once = pl.Buffered(1)
    return pl.pallas_call(
        _attn_kernel,
        grid=(b, nq),
        in_specs=[pl.BlockSpec((1, d, tq), lambda i, j: (i * nq + j, 0, 0)),
                  pl.BlockSpec((1, length, d), lambda i, j: (i, 0, 0), pipeline_mode=once),
                  pl.BlockSpec((nq, d, tq), lambda i, j: (i, 0, 0), pipeline_mode=once),
                  pl.BlockSpec((1, N_HEADS, length, LANES), lambda i, j: (i, 0, 0, 0), pipeline_mode=once),
                  pl.BlockSpec((1, d), lambda i, j: (0, 0))],
        out_specs=pl.BlockSpec((1, tq, d), lambda i, j: (i, j, 0)),
        out_shape=jax.ShapeDtypeStruct((b, length, d), BF16),
        scratch_shapes=[pltpu.VMEM((N_HEADS, LANES, tq), BF16), pltpu.VMEM((N_HEADS, tq), F32),
                        pltpu.VMEM((N_HEADS, tq), F32), pltpu.VMEM((d, tq), F32),
                        pltpu.VMEM((ATTN_GROUP, N_HEADS, tq, tq), F32)],
        compiler_params=_cparams(("arbitrary", "arbitrary")),
    )(qt, k, vt, crep, g)


def _attn_paged_kernel(pt_ref, q_ref, kn_ref, vn_ref, cn_ref, g_ref, kc_hbm, vc_hbm, o_ref,
                       kbuf, vbuf, sem, *, n_pages, n_new):
    i = pl.program_id(0)
    n = pl.num_programs(0)
    n_past = n_pages * PAGE

    def page_copies(seq, slot):
        cps = []
        for p in range(n_pages):
            pg = pt_ref[seq * n_pages + p]
            cps.append(pltpu.make_async_copy(kc_hbm.at[pg], kbuf.at[slot, p], sem.at[0, slot]))
            cps.append(pltpu.make_async_copy(vc_hbm.at[pg], vbuf.at[slot, p], sem.at[1, slot]))
        return cps

    @pl.when(i == 0)
    def _():
        for cp in page_copies(0, 0):
            cp.start()

    @pl.when(i + 1 < n)
    def _():
        for cp in page_copies(i + 1, (i + 1) % 2):
            cp.start()

    slot = i % 2
    for cp in page_copies(i, slot):
        cp.wait()

    d = q_ref.shape[-1]
    nr = N_HEADS * n_new
    new_bits = n_new.bit_length() - 1
    head_bits = HEAD_DIM.bit_length() - 1
    rowh = lax.shift_right_logical(lax.broadcasted_iota(I32, (nr, d), 0), new_bits)
    colh = lax.shift_right_logical(lax.broadcasted_iota(I32, (nr, d), 1), head_bits)
    bd = rowh == colh
    qrep = jnp.broadcast_to(q_ref[0][None], (N_HEADS, n_new, d)).reshape(nr, d)
    qbd = jnp.where(bd, qrep, 0.0).astype(BF16)
    cn = cn_ref[0]
    cnr = jnp.broadcast_to(cn[:, None, :], (N_HEADS, n_new, cn.shape[-1])).reshape(nr, cn.shape[-1])

    s_p = []
    for p in range(n_pages):
        kt = kbuf[slot, p].reshape(d, PAGE)
        s_p.append(_bdot(qbd, kt) + cnr[:, p * PAGE:(p + 1) * PAGE])
    s_n = _bdot_nt(qbd, kn_ref[0]) + cnr[:, n_past:n_past + n_new]
    qpos = lax.broadcasted_iota(I32, (nr, n_new), 0) & (n_new - 1)
    kpos = lax.broadcasted_iota(I32, (nr, n_new), 1)
    s_n = jnp.where(kpos <= qpos, s_n, NEG_INF)
    m = jnp.max(s_n, axis=1, keepdims=True)
    for sp in s_p:
        m = jnp.maximum(m, jnp.max(sp, axis=1, keepdims=True))
    p_n = jnp.exp(s_n - m)
    l = jnp.sum(p_n, axis=1, keepdims=True)
    of = _bdot(p_n, vn_ref[0])
    for p in range(n_pages):
        pp = jnp.exp(s_p[p] - m)
        l = l + jnp.sum(pp, axis=1, keepdims=True)
        of = of + _bdot_nt(pp, vbuf[slot, p].reshape(d, PAGE))
    of = jnp.where(bd, of / l, 0.0)
    o = jnp.sum(of.reshape(N_HEADS, n_new, d), axis=0)
    o_ref[0] = _rms(o, g_ref[...])


def _attn_paged(page_table, q, kn, vn, cn, g, cache_k, cache_v):
    nseq, n_new, d = q.shape
    assert n_new & (n_new - 1) == 0, "new-token count must be a power of two"
    n_pages = page_table.shape[1]
    n_past = n_pages * PAGE
    grid_spec = pltpu.PrefetchScalarGridSpec(
        num_scalar_prefetch=1,
        grid=(nseq,),
        in_specs=[pl.BlockSpec((1, n_new, d), lambda i, pt: (i, 0, 0)),
                  pl.BlockSpec((1, n_new, d), lambda i, pt: (i, 0, 0)),
                  pl.BlockSpec((1, n_new, d), lambda i, pt: (i, 0, 0)),
                  pl.BlockSpec((1, N_HEADS, cn.shape[-1]), lambda i, pt: (i, 0, 0)),
                  pl.BlockSpec((1, d), lambda i, pt: (0, 0)),
                  pl.BlockSpec(memory_space=pl.ANY),
                  pl.BlockSpec(memory_space=pl.ANY)],
        out_specs=pl.BlockSpec((1, n_new, d), lambda i, pt: (i, 0, 0)),
        scratch_shapes=[pltpu.VMEM((2, n_pages, N_HEADS, HEAD_DIM, PAGE), F32),
                        pltpu.VMEM((2, n_pages, N_HEADS, HEAD_DIM, PAGE), F32),
                        pltpu.SemaphoreType.DMA((2, 2))],
    )
    return pl.pallas_call(
        functools.partial(_attn_paged_kernel, n_pages=n_pages, n_new=n_new),
        grid_spec=grid_spec,
        out_shape=jax.ShapeDtypeStruct((nseq, n_new, d), F32),
        compiler_params=_cparams(("arbitrary",)),
    )(page_table.reshape(-1), q, kn, vn, cn, g, cache_k, cache_v)


def _outproj_kernel(x_ref, ys_ref, oa_ref, gt_ref, sc_ref, sh_ref, g_ref, wo_ref, wrt_ref,
                    x1_ref, h2_ref, lg_ref):
    nbb, rb, d = x_ref.shape
    mix = jnp.dot(ys_ref[...], wo_ref[0:D_SSM, :], preferred_element_type=F32)
    mix = mix + jnp.dot(oa_ref[...], wo_ref[D_SSM:, :], preferred_element_type=F32)
    x1 = x_ref[...] + gt_ref[...] * mix.reshape(nbb, rb, d)
    x1_ref[...] = x1
    h2 = (_rms(x1, g_ref[...]) * (1.0 + sc_ref[...]) + sh_ref[...]).reshape(nbb * rb, d)
    hb = h2.astype(BF16)
    h2_ref[...] = h2
    hlo = (h2 - hb.astype(F32)).astype(BF16)
    whi = wrt_ref[0]
    wlo = wrt_ref[1]
    nt = (((1,), (1,)), ((), ()))
    lg = lax.dot_general(whi, hb, nt, preferred_element_type=F32)
    lg = lg + lax.dot_general(wlo, hb, nt, preferred_element_type=F32)
    lg = lg + lax.dot_general(whi, hlo, nt, preferred_element_type=F32)
    lg_ref[...] = lg


def _outproj(x3, ys, oa, gt, sc, sh, g, wo, wrt, nbb, rb):
    nb, r, d = x3.shape
    t = nb * r
    rows = nbb * rb
    nj = r // rb
    row_map = lambda i, j: (i * nj + j, 0)
    mod_spec = pl.BlockSpec((nbb, 1, d), lambda i, j: (i, 0, 0))
    return pl.pallas_call(
        _outproj_kernel,
        grid=(nb // nbb, nj),
        in_specs=[pl.BlockSpec((nbb, rb, d), lambda i, j: (i, j, 0)),
                  pl.BlockSpec((rows, D_SSM), row_map),
                  pl.BlockSpec((rows, D_ATTN), row_map),
                  mod_spec, mod_spec, mod_spec,
                  pl.BlockSpec((1, 1, d), lambda i, j: (0, 0, 0)),
                  pl.BlockSpec((D_SSM + D_ATTN, d), lambda i, j: (0, 0)),
                  pl.BlockSpec((2, N_EXPERTS, d), lambda i, j: (0, 0, 0))],
        out_specs=[pl.BlockSpec((nbb, rb, d), lambda i, j: (i, j, 0)),
                   pl.BlockSpec((rows, d), row_map),
                   pl.BlockSpec((N_EXPERTS, rows), lambda i, j: (0, i * nj + j))],
        out_shape=[jax.ShapeDtypeStruct((nb, r, d), F32),
                   jax.ShapeDtypeStruct((t, d), F32),
                   jax.ShapeDtypeStruct((N_EXPERTS, t), F32)],
        compiler_params=_cparams(("arbitrary", "arbitrary")),
    )(x3, ys, oa, gt, sc, sh, g, wo, wrt)


def _router_kernel(lg_ref, rb_ref, ut_ref, idx_ref, w_ref, rank_ref, cnt_ref, carry_ref):
    i = pl.program_id(0)
    tm = lg_ref.shape[1]
    per_group = N_EXPERTS // N_EXPERT_GROUPS

    @pl.when(i == 0)
    def _():
        carry_ref[...] = jnp.zeros_like(carry_ref)

    scores = jax.nn.sigmoid(lg_ref[...])
    biased = scores + rb_ref[...]
    blks, grp = [], []
    for g in range(N_EXPERT_GROUPS):
        blk = biased[g * per_group:(g + 1) * per_group, :]
        m1 = jnp.max(blk, axis=0, keepdims=True)
        eq = blk == m1
        n_eq = jnp.sum(jnp.where(eq, 1.0, 0.0), axis=0, keepdims=True)
        m2 = jnp.max(jnp.where(eq, NEG_INF, blk), axis=0, keepdims=True)
        blks.append(blk)
        grp.append(m1 + jnp.where(n_eq >= 2.0, m1, m2))
    masked = []
    for g in range(N_EXPERT_GROUPS):
        beaten = jnp.zeros((1, tm), F32)
        for o in range(N_EXPERT_GROUPS):
            if o == g:
                continue
            ahead = (grp[o] >= grp[g]) if o < g else (grp[o] > grp[g])
            beaten = beaten + jnp.where(ahead, 1.0, 0.0)
        masked.append(jnp.where(beaten < float(TOPK_GROUPS), blks[g], NEG_INF))
    work = jnp.concatenate(masked, axis=0)

    eid = lax.broadcasted_iota(I32, (N_EXPERTS, tm), 0)
    chosen = jnp.zeros((N_EXPERTS, tm), F32)
    idxs, ws, sels = [], [], []
    for _ in range(TOP_K):
        m = jnp.max(work, axis=0, keepdims=True)
        first = jnp.min(jnp.where(work == m, eid, N_EXPERTS), axis=0, keepdims=True)
        sel = eid == first
        idxs.append(first)
        ws.append(jnp.sum(jnp.where(sel, scores, 0.0), axis=0, keepdims=True))
        sels.append(sel)
        chosen = jnp.where(sel, 1.0, chosen)
        work = jnp.where(sel, NEG_INF, work)
    wsum = ws[0]
    for wk in ws[1:]:
        wsum = wsum + wk

    prefix = jnp.dot(chosen.astype(BF16), ut_ref[...], preferred_element_type=F32) + carry_ref[...]
    carry_ref[...] = carry_ref[...] + jnp.sum(chosen, axis=1, keepdims=True)
    cnt_ref[...] = carry_ref[...]

    idx_ref[...] = jnp.zeros_like(idx_ref)
    w_ref[...] = jnp.zeros_like(w_ref)
    rank_ref[...] = jnp.zeros_like(rank_ref)
    for k in range(TOP_K):
        idx_ref[k:k + 1, :] = idxs[k]
        w_ref[k:k + 1, :] = ws[k] / wsum * ROUTED_SCALE
        rank = jnp.sum(jnp.where(sels[k], prefix, 0.0), axis=0, keepdims=True)
        rank_ref[k:k + 1, :] = rank.astype(I32)


def _router(lg, router_bias, tm):
    e, t = lg.shape
    ut = np.triu(np.ones((tm, tm), np.float32), 1)
    tok_spec = pl.BlockSpec((SUBLANES, tm), lambda i: (0, i))
    return pl.pallas_call(
        _router_kernel,
        grid=(t // tm,),
        in_specs=[pl.BlockSpec((e, tm), lambda i: (0, i)),
                  pl.BlockSpec((e, 1), lambda i: (0, 0)),
                  pl.BlockSpec((tm, tm), lambda i: (0, 0))],
        out_specs=[tok_spec, tok_spec, tok_spec, pl.BlockSpec((e, 1), lambda i: (0, 0))],
        out_shape=[jax.ShapeDtypeStruct((SUBLANES, t), I32), jax.ShapeDtypeStruct((SUBLANES, t), F32),
                   jax.ShapeDtypeStruct((SUBLANES, t), I32), jax.ShapeDtypeStruct((e, 1), F32)],
        scratch_shapes=[pltpu.VMEM((e, 1), F32)],
        compiler_params=_cparams(("arbitrary",)),
    )(lg, router_bias.reshape(e, 1), jnp.asarray(ut, BF16))


def _experts_kernel(be_ref, nv_ref, new_ref, fill_ref, x_ref, wg_ref, wu_ref, wd_ref, o_ref, wgu_s, wd_s):
    i = pl.program_id(0)
    de = wg_ref.shape[-1]
    valid = i < nv_ref[0]

    @pl.when(valid & (new_ref[i] == 1))
    def _():
        wgu_s[:, :de] = wg_ref[0].astype(BF16)
        wgu_s[:, de:] = wu_ref[0].astype(BF16)
        wd_s[...] = wd_ref[0].astype(BF16)

    @pl.when(valid)
    def _():
        row = lax.broadcasted_iota(I32, x_ref.shape, 0)
        x = jnp.where(row < fill_ref[i], x_ref[...], 0.0)
        gu = jnp.dot(x.astype(BF16), wgu_s[...], preferred_element_type=F32)
        g = gu[:, :de]
        a = (g * jax.nn.sigmoid(g) * gu[:, de:]).astype(BF16)
        o_ref[...] = jnp.dot(a, wd_s[...], preferred_element_type=F32)

    @pl.when(jnp.logical_not(valid))
    def _():
        o_ref[...] = jnp.zeros_like(o_ref)


def _experts(block_e, n_valid, block_new, block_fill, xs, w_eg, w_eu, w_ed, m):
    n_rows, d = xs.shape
    de = w_eg.shape[-1]
    nblk = n_rows // m
    xmap = lambda i, be, nv, *_: (jnp.minimum(i, nv[0] - 1), 0)
    wmap = lambda i, be, *_: (be[i], 0, 0)
    grid_spec = pltpu.PrefetchScalarGridSpec(
        num_scalar_prefetch=4,
        grid=(nblk,),
        in_specs=[pl.BlockSpec((m, d), xmap),
                  pl.BlockSpec((1, d, de), wmap),
                  pl.BlockSpec((1, d, de), wmap),
                  pl.BlockSpec((1, de, d), wmap)],
        out_specs=pl.BlockSpec((m, d), lambda i, *_: (i, 0)),
        scratch_shapes=[pltpu.VMEM((d, 2 * de), BF16), pltpu.VMEM((de, d), BF16)],
    )
    return pl.pallas_call(
        _experts_kernel,
        grid_spec=grid_spec,
        out_shape=jax.ShapeDtypeStruct((n_rows, d), F32),
        compiler_params=_cparams(("arbitrary",)),
    )(block_e, n_valid, block_new, block_fill, xs, w_eg, w_eu, w_ed)


def _row_copy(src, src_row, dst, dst_row, sem):
    return pltpu.make_async_copy(src.at[pl.ds(src_row, 1)], dst.at[pl.ds(dst_row, 1)], sem)


def _gather_rows(table, idx, window):
    n = idx.shape[0]
    d = table.shape[1]
    info = plsc.get_sparse_core_info()
    n_workers = info.num_cores * info.num_subcores
    per_worker = n // n_workers
    n_buf = SC_GATHER_BUFFERS
    n_rounds = per_worker // (window * n_buf)
    assert n_rounds * window * n_buf * n_workers == n
    mesh = plsc.VectorSubcoreMesh(core_axis_name="c", subcore_axis_name="s")

    @functools.partial(
        pl.kernel, out_type=jax.ShapeDtypeStruct((n, d), table.dtype), mesh=mesh,
        scratch_types=[pltpu.VMEM((per_worker,), I32), pltpu.VMEM((n_buf, window, d), table.dtype),
                       pltpu.SemaphoreType.DMA((n_buf,)), pltpu.SemaphoreType.DMA((n_buf,))])
    def gather(x_hbm, i_hbm, o_hbm, i_v, buf, gsem, wsem):
        base = (lax.axis_index("s") * info.num_cores + lax.axis_index("c")) * per_worker
        pltpu.sync_copy(i_hbm.at[pl.ds(base, per_worker)], i_v)

        @pl.loop(0, n_rounds)
        def _(rnd):
            first = rnd * (window * n_buf)
            reads, writes = [], []
            for b in range(n_buf):
                rows = pl.ds(first + b * window, window)
                reads.append(pltpu.make_async_copy(x_hbm.at[i_v.at[rows]], buf.at[b], gsem.at[b]))
                writes.append(pltpu.make_async_copy(buf.at[b], o_hbm.at[pl.ds(base + first + b * window, window)],
                                                    wsem.at[b]))
            for cp in reads:
                cp.start()
            for b in range(n_buf):
                reads[b].wait()
                writes[b].start()
            for cp in writes:
                cp.wait()

    return gather(table, idx)


def _scatter_rows(x, dest_t, n_rows, window):
    t, d = x.shape
    info = plsc.get_sparse_core_info()
    n_workers = info.num_cores * info.num_subcores
    per_worker = t // n_workers
    chunks = per_worker // window
    n_buf = 2
    n_rounds = chunks // n_buf
    assert n_rounds * n_buf * window * n_workers == t
    idx_rows = -(-(TOP_K * chunks) // SUBLANES) * SUBLANES
    idx = dest_t[:TOP_K].reshape(TOP_K, n_workers, chunks, window).transpose(1, 0, 2, 3)
    idx = idx.reshape(n_workers, TOP_K * chunks, window)
    idx = jnp.pad(idx, ((0, 0), (0, idx_rows - TOP_K * chunks), (0, 0))).reshape(n_workers * idx_rows, window)
    mesh = plsc.VectorSubcoreMesh(core_axis_name="c", subcore_axis_name="s")

    @functools.partial(
        pl.kernel, out_type=jax.ShapeDtypeStruct((n_rows, d), x.dtype), mesh=mesh,
        scratch_types=[pltpu.VMEM((idx_rows, window), I32), pltpu.VMEM((n_buf, window, d), x.dtype),
                       pltpu.SemaphoreType.DMA((n_buf,)), pltpu.SemaphoreType.DMA((n_buf,))])
    def scatter(x_hbm, i_hbm, o_hbm, i_v, buf, rsem, wsem):
        wid = lax.axis_index("s") * info.num_cores + lax.axis_index("c")
        base = wid * per_worker
        pltpu.sync_copy(i_hbm.at[pl.ds(wid * idx_rows, idx_rows)], i_v)

        @pl.loop(0, n_rounds)
        def _(rnd):
            reads = [pltpu.make_async_copy(x_hbm.at[pl.ds(base + (rnd * n_buf + b) * window, window)],
                                           buf.at[b], rsem.at[b]) for b in range(n_buf)]
            for cp in reads:
                cp.start()
            writes = []
            for b in range(n_buf):
                reads[b].wait()
                for k in range(TOP_K):
                    cp = pltpu.make_async_copy(buf.at[b], o_hbm.at[i_v.at[k * chunks + rnd * n_buf + b]],
                                               wsem.at[b])
                    cp.start()
                    writes.append(cp)
            for cp in writes:
                cp.wait()

    return scatter(x, idx)


def _final_kernel(x1_ref, h2_ref, w_ref, gt_ref, sc_ref, sh_ref, g_ref, wgu_ref, wd_ref, *rest):
    yg_refs = rest[:TOP_K]
    y_ref = rest[TOP_K]
    nbb, rb, d = x1_ref.shape
    ds = wd_ref.shape[0]
    gu = jnp.dot(h2_ref[...].astype(BF16), wgu_ref[...], preferred_element_type=F32)
    g = gu[:, :ds]
    a = (g * jax.nn.sigmoid(g) * gu[:, ds:]).astype(BF16)
    ff = jnp.dot(a, wd_ref[...], preferred_element_type=F32)
    routed = yg_refs[0][...] * w_ref[:, 0:1]
    for k in range(1, TOP_K):
        routed = routed + yg_refs[k][...] * w_ref[:, k:k + 1]
    ff = routed + ff
    x2 = x1_ref[...] + gt_ref[...] * ff.reshape(nbb, rb, d)
    y_ref[...] = _rms(x2, g_ref[...]) * (1.0 + sc_ref[...]) + sh_ref[...]


def _final(x1, h2, wts, yg, first_tile, gt, sc, sh, g, wgu, wd, nbb, rb):
    nb, r, d = x1.shape
    rows = nbb * rb
    nj = r // rb
    tiles_all = yg.shape[0] // TOP_K // rows
    row_map = lambda i, j: (i * nj + j, 0)
    mod_spec = pl.BlockSpec((nbb, 1, d), lambda i, j: (i, 0, 0))
    slab_specs = [pl.BlockSpec((rows, d), lambda i, j, k=k: (k * tiles_all + first_tile + i * nj + j, 0))
                  for k in range(TOP_K)]
    return pl.pallas_call(
        _final_kernel,
        grid=(nb // nbb, nj),
        in_specs=[pl.BlockSpec((nbb, rb, d), lambda i, j: (i, j, 0)),
                  pl.BlockSpec((rows, d), row_map),
                  pl.BlockSpec((rows, SUBLANES), row_map),
                  mod_spec, mod_spec, mod_spec,
                  pl.BlockSpec((1, 1, d), lambda i, j: (0, 0, 0)),
                  pl.BlockSpec(wgu.shape, lambda i, j: (0, 0)),
                  pl.BlockSpec(wd.shape, lambda i, j: (0, 0))] + slab_specs,
        out_specs=pl.BlockSpec((nbb, rb, d), lambda i, j: (i, j, 0)),
        out_shape=jax.ShapeDtypeStruct((nb, r, d), F32),
        compiler_params=_cparams(("arbitrary", "arbitrary")),
    )(x1, h2, wts, gt, sc, sh, g, wgu, wd, *([yg] * TOP_K))


def _dispatch_kernel(plo_ref, pln_ref, nv_ref, dest_ref, *rest, group_tiles, m):
    x_refs = rest[:len(group_tiles)]
    xs_hbm, sem, zsem, zbuf = rest[len(group_tiles):]
    rows = x_refs[0].shape[0]
    step = pl.program_id(0)

    nblk = xs_hbm.shape[0] // m

    def pad_copies(act):
        def single_rows(off, n):
            for r in range(SUBLANES - 1):
                @pl.when(r < n)
                def _(r=r):
                    act(_row_copy(zbuf, 0, xs_hbm, off + r, zsem))

        def per_expert(e, c):
            off = plo_ref[e]
            ln = pln_ref[e]
            head = jnp.minimum((-off) & (SUBLANES - 1), ln)
            single_rows(off, head)
            off = off + head
            ln = ln - head
            bit = m // 2
            while bit >= SUBLANES:
                has = (ln & bit) != 0

                @pl.when(has)
                def _(off=off, bit=bit):
                    dst = xs_hbm.at[pl.ds(pl.multiple_of(off, SUBLANES), bit)]
                    act(pltpu.make_async_copy(zbuf.at[pl.ds(0, bit)], dst, zsem))

                off = off + jnp.where(has, bit, 0)
                bit //= 2
            single_rows(off, ln & (SUBLANES - 1))
            return c

        lax.fori_loop(0, N_EXPERTS, per_expert, 0)

        def per_block(b, c):
            @pl.when(b >= nv_ref[0])
            def _():
                act(pltpu.make_async_copy(zbuf, xs_hbm.at[pl.ds(pl.multiple_of(b * m, m), m)], zsem))

            return c

        lax.fori_loop(0, nblk, per_block, 0)

    @pl.when(step == 0)
    def _():
        zbuf[...] = jnp.zeros_like(zbuf)
        pad_copies(lambda cp: cp.start())
        pad_copies(lambda cp: cp.wait())

    first = 0
    for x_ref, n_tiles in zip(x_refs, group_tiles):
        @pl.when((step >= first) & (step < first + n_tiles))
        def _(x_ref=x_ref):
            def issue(r, c):
                for k in range(TOP_K):
                    _row_copy(x_ref, r, xs_hbm, dest_ref[0, 0, r * TOP_K + k], sem).start(priority=k % 2)
                return c

            lax.fori_loop(0, rows, issue, 0, unroll=2)
            for k in range(TOP_K):
                pltpu.make_async_copy(x_ref, xs_hbm.at[pl.ds(0, rows)], sem).wait()

        first += n_tiles


def _dispatch(pad_lo, pad_len, n_valid, dest3, h2_groups, n_rows, m):
    d = h2_groups[0].shape[1]
    rows = dest3.shape[-1] // TOP_K
    group_tiles = tuple(h2.shape[0] // rows for h2 in h2_groups)
    in_specs = [pl.BlockSpec((1, 1, rows * TOP_K), lambda i, *_: (i, 0, 0), memory_space=pltpu.SMEM)]
    first = 0
    for n_tiles in group_tiles:
        tile_map = lambda i, *_, first=first, n_tiles=n_tiles: (jnp.clip(i - first, 0, n_tiles - 1), 0)
        in_specs.append(pl.BlockSpec((rows, d), tile_map))
        first += n_tiles
    grid_spec = pltpu.PrefetchScalarGridSpec(
        num_scalar_prefetch=3,
        grid=(sum(group_tiles),),
        in_specs=in_specs,
        out_specs=pl.BlockSpec(memory_space=pl.ANY),
        scratch_shapes=[pltpu.SemaphoreType.DMA(()), pltpu.SemaphoreType.DMA(()), pltpu.VMEM((m, d), F32)],
    )
    return pl.pallas_call(
        functools.partial(_dispatch_kernel, group_tiles=group_tiles, m=m),
        grid_spec=grid_spec,
        out_shape=jax.ShapeDtypeStruct((n_rows, d), F32),
        compiler_params=_cparams(("arbitrary",)),
    )(pad_lo, pad_len, n_valid, dest3, *h2_groups)


def _moe(h2_groups, lg, router_bias, w_eg, w_eu, w_ed, tile):
    t = lg.shape[1]
    e = N_EXPERTS
    m = MOE_ROWS
    idx_t, w_t, rank_t, counts = _router(lg, router_bias, tile)
    counts = counts.reshape(e).astype(I32)
    padded = ((counts + m - 1) // m) * m
    pad_end = jnp.cumsum(padded)
    pad_start = pad_end - padded
    onehot = idx_t[:, :, None] == jnp.arange(e, dtype=I32)
    dest_t = jnp.sum(jnp.where(onehot, pad_start, 0), axis=-1) + rank_t
    dest3 = dest_t[:TOP_K].T.reshape(t // tile, 1, tile * TOP_K)
    n_rows = (-(-(t * TOP_K) // m)) * m + e * m
    nblk = n_rows // m
    block_start = jnp.arange(nblk, dtype=I32) * m
    block_e = jnp.minimum(jnp.sum(pad_end[None, :] <= block_start[:, None], axis=1), e - 1).astype(I32)
    block_new = jnp.concatenate([jnp.ones((1,), I32), (block_e[1:] != block_e[:-1]).astype(I32)])
    n_valid = (pad_end[-1] // m).astype(I32).reshape(1)
    of_block = block_e[:, None] == jnp.arange(e, dtype=I32)
    count_b = jnp.sum(jnp.where(of_block, counts, 0), axis=1)
    start_b = jnp.sum(jnp.where(of_block, pad_start, 0), axis=1)
    block_fill = jnp.clip(count_b - (block_start - start_b), 0, m).astype(I32)
    xs = _scatter_rows(jnp.concatenate(h2_groups, axis=0), dest_t, n_rows, SC_SCATTER_WINDOW)
    yb = _experts(block_e, n_valid, block_new, block_fill, xs, w_eg, w_eu, w_ed, m)
    yg = _gather_rows(yb, dest_t[:TOP_K].reshape(-1), SC_GATHER_WINDOW)
    return w_t.T, yg


def kernel(x_prompt, x_sample, c_prompt, c_sample, cache_k, cache_v, cache_logf, state_ssm_re, state_ssm_im, page_table, w_ada, b_ada, g_norm1, w_in, b_fgate, ssm_lambda_re, ssm_lambda_im, ssm_log_step, ssm_b_re, ssm_b_im, ssm_c_re, ssm_c_im, ssm_d, w_glu, b_glu, g_ssm_out, g_attn_out, w_out, g_norm2, w_router, router_bias, w_exp_gate, w_exp_up, w_exp_down, w_sh_gate, w_sh_up, w_sh_down, g_final, w_ada_final, b_ada_final):
    depth = w_ada.shape[0]
    assert depth == 1, "one layer is supported"
    bp, lp, d = x_prompt.shape
    bs, ls, _ = x_sample.shape
    n_pages = page_table.shape[1]
    n_past = n_pages * PAGE
    n_groups = ssm_lambda_re.shape[1]

    n_c = bp + bs
    n_c_pad = -(-n_c // SUBLANES) * SUBLANES
    c_all = jnp.concatenate([c_prompt, c_sample, jnp.zeros((n_c_pad - n_c, d), F32)], axis=0)
    mod = _adaln(c_all, w_ada[0], b_ada[0])
    modf = _adaln(c_all, w_ada_final, b_ada_final)

    def mods(lo, hi):
        parts = [mod[lo:hi, k * d:(k + 1) * d][:, None, :] for k in range(6)]
        parts += [modf[lo:hi, k * d:(k + 1) * d][:, None, :] for k in range(2)]
        return parts

    w_u, w_q, w_k, w_v, w_f = jnp.split(w_in[0], [D_SSM, D_SSM + D_ATTN, D_SSM + 2 * D_ATTN,
                                                   D_SSM + 3 * D_ATTN], axis=1)
    w_fpad = jnp.concatenate([w_f, jnp.zeros((d, LANES - N_HEADS), F32)], axis=1)
    w_main_s = jnp.concatenate([w_u, w_k, w_v, w_fpad], axis=1).astype(BF16)
    w_main_p = jnp.concatenate([w_u, w_k, w_fpad], axis=1).astype(BF16)
    w_qkv = jnp.stack([w_q.T, w_k.T, w_v.T]).astype(BF16)
    b_f = jnp.concatenate([b_fgate[0], jnp.zeros((LANES - N_HEADS,), F32)]).reshape(1, LANES)
    g1 = g_norm1[0].reshape(1, 1, d)
    g2 = g_norm2[0].reshape(1, 1, d)
    gf = g_final.reshape(1, 1, d)
    wb, wc, a_re, a_im = _s5_params(ssm_lambda_re[0], ssm_lambda_im[0], ssm_log_step[0], ssm_b_re[0],
                                    ssm_b_im[0], ssm_c_re[0], ssm_c_im[0])
    dsk = ssm_d[0].reshape(1, D_SSM)
    wglu = w_glu[0].astype(BF16)
    bglu = b_glu[0].reshape(1, D_SSM)
    g_so = g_ssm_out[0].reshape(1, D_SSM)
    g_ao = g_attn_out[0].reshape(1, D_ATTN)
    wo = w_out[0].astype(BF16)
    wr_t = w_router[0].T
    wr_hi = wr_t.astype(BF16)
    wrt = jnp.stack([wr_hi, (wr_t - wr_hi.astype(F32)).astype(BF16)])
    wgu = jnp.concatenate([w_sh_gate[0], w_sh_up[0]], axis=1).astype(BF16)
    wsd = w_sh_down[0].astype(BF16)

    def ssm_state(re, im):
        return jnp.concatenate([re.reshape(-1, 16, LANES), im.reshape(-1, 16, LANES)], axis=1)

    def split_state(ht):
        n = ht.shape[0]
        return (ht[:, :16].reshape(1, n, n_groups, SSM_STATE), ht[:, 16:].reshape(1, n, n_groups, SSM_STATE))

    tm = 512
    sh1, sc1, gt1, sh2, sc2, gt2_p, shf_p, scf_p = mods(0, bp)
    u, lf, kb, qt, vt, ktf, vtf = _inproj(x_prompt, sc1, sh1, g1, w_main_p, w_qkv, b_f, 1, tm, True,
                                          HEAD_DIM ** -0.5 * LOG2E)
    crep = _crep(lf, bp, lp, LOG2E)
    ys, ht = _s5(u.reshape(bp, lp, D_SSM), jnp.zeros((bp, 32, LANES), F32), wb, wc, a_re, a_im, dsk,
                 wglu, bglu, g_so, bp, 256)
    oa = _attn_prompt(qt, kb.reshape(bp, lp, D_ATTN), vt, crep, g_ao)
    x1_p, h2_p, lg_p = _outproj(x_prompt, ys.reshape(bp * lp, D_SSM), oa.reshape(bp * lp, D_ATTN),
                                gt1, sc2, sh2, g2, wo, wrt, 1, tm)
    k_prompt = ktf.reshape(bp, N_HEADS, HEAD_DIM, lp).transpose(0, 3, 1, 2)[None]
    v_prompt = vtf.reshape(bp, N_HEADS, HEAD_DIM, lp).transpose(0, 3, 1, 2)[None]
    logf_prompt = lf[:, :N_HEADS].reshape(1, bp, lp, N_HEADS)
    sre_p, sim_p = split_state(ht)

    nbb = 64
    sh1, sc1, gt1, sh2, sc2, gt2_s, shf_s, scf_s = mods(bp, bp + bs)
    u, lf, k, v, q = _inproj(x_sample, sc1, sh1, g1, w_main_s, w_qkv, b_f, nbb, ls, False, HEAD_DIM ** -0.5)
    ys, ht = _s5(u.reshape(bs, ls, D_SSM), ssm_state(state_ssm_re[0], state_ssm_im[0]), wb, wc, a_re, a_im,
                 dsk, wglu, bglu, g_so, 32, ls)
    lf_past = cache_logf[0][page_table].reshape(bs, n_past, N_HEADS).transpose(0, 2, 1)
    lf_new = lf[:, :N_HEADS].reshape(bs, ls, N_HEADS).transpose(0, 2, 1)
    n_key_pad = -(-(n_past + ls) // LANES) * LANES
    lf_all = jnp.concatenate([lf_past, lf_new, jnp.zeros((bs, N_HEADS, n_key_pad - n_past - ls), F32)], axis=2)
    gsz = n_key_pad // LANES
    cn_s = _neg_cumsum(lf_all.reshape(-1, LANES), gsz, gsz * 64).reshape(bs, N_HEADS, n_key_pad)
    oa = _attn_paged(page_table, q.reshape(bs, ls, D_ATTN), k.reshape(bs, ls, D_ATTN),
                     v.reshape(bs, ls, D_ATTN), cn_s, g_ao,
                     cache_k[0].transpose(0, 2, 3, 1), cache_v[0].transpose(0, 2, 3, 1))
    x1_s, h2_s, lg_s = _outproj(x_sample, ys, oa.reshape(bs * ls, D_ATTN).astype(BF16),
                                gt1, sc2, sh2, g2, wo, wrt, nbb, ls)

    assert nbb * ls == tm
    wts, yg = _moe([h2_p, h2_s], jnp.concatenate([lg_p, lg_s], axis=1), router_bias[0],
                   w_exp_gate[0], w_exp_up[0], w_exp_down[0], tm)
    tiles_p = bp * lp // tm
    y_prompt = _final(x1_p, h2_p, wts[:bp * lp], yg, 0, gt2_p, scf_p, shf_p, gf, wgu, wsd, 1, tm)
    y_sample = _final(x1_s, h2_s, wts[bp * lp:], yg, tiles_p, gt2_s, scf_s, shf_s, gf, wgu, wsd, nbb, ls)
    k_sample = k.reshape(1, bs, ls, N_HEADS, HEAD_DIM)
    v_sample = v.reshape(1, bs, ls, N_HEADS, HEAD_DIM)
    logf_sample = lf[:, :N_HEADS].reshape(1, bs, ls, N_HEADS)
    sre_s, sim_s = split_state(ht)

    return (y_prompt, y_sample, k_prompt, v_prompt, logf_prompt, sre_p, sim_p,
            k_sample, v_sample, logf_sample, sre_s, sim_s)
```

```python
import functools
import math

import jax
import jax.numpy as jnp
import numpy as np
from jax import lax
from jax.experimental import pallas as pl
from jax.experimental.pallas import tpu as pltpu
from jax.experimental.pallas import tpu_sc as plsc

F32 = jnp.float32
BF16 = jnp.bfloat16
I32 = jnp.int32

EPS = 1e-6
HEAD_DIM = 64
N_HEADS = 8
D_SSM = 512
D_ATTN = 512
SSM_GROUP = 16
SSM_STATE = 64
N_EXPERTS = 64
TOP_K = 6
N_EXPERT_GROUPS = 8
TOPK_GROUPS = 4
ROUTED_SCALE = 2.5
PAGE = 128

LANES = 128
SUBLANES = 8
VMEM_LIMIT = 48 * 1024 * 1024
MOE_ROWS = 512
ATTN_TILE = 256
ATTN_GROUP = 4
LOG2E = math.log2(math.e)
SC_SCATTER_WINDOW = 16
SC_GATHER_WINDOW = 24
SC_GATHER_BUFFERS = 4
NEG_INF = float("-inf")


def _cparams(sem):
    return pltpu.CompilerParams(dimension_semantics=sem, vmem_limit_bytes=VMEM_LIMIT)


def _bdot(a, b):
    return jnp.dot(a.astype(BF16), b.astype(BF16), preferred_element_type=F32)


def _bdot_nt(a, b):
    return lax.dot_general(a.astype(BF16), b.astype(BF16), (((1,), (1,)), ((), ())),
                           preferred_element_type=F32)


def _split3(v):
    hi = v.astype(BF16)
    r1 = v - hi.astype(F32)
    mid = r1.astype(BF16)
    lo = (r1 - mid.astype(F32)).astype(BF16)
    return hi, mid, lo


def _pack_bf16_pairs(xb):
    n = xb.shape[1] // 2
    bits = lax.bitcast_convert_type(xb.astype(F32), jnp.uint32)
    return lax.shift_right_logical(bits[:, :n], jnp.uint32(16)) | bits[:, n:]


def _unpack_bf16_pairs(xp):
    lo = lax.bitcast_convert_type(lax.shift_left(xp, jnp.uint32(16)), F32)
    hi = lax.bitcast_convert_type(xp & jnp.uint32(0xFFFF0000), F32)
    return jnp.concatenate([lo, hi], axis=1).astype(BF16)


def _rms(x, g):
    return x * lax.rsqrt(jnp.mean(x * x, axis=-1, keepdims=True) + EPS) * g


def _adaln_kernel(c_ref, w_ref, b_ref, o_ref):
    c = c_ref[...]
    s = c * jax.nn.sigmoid(c)
    o_ref[...] = _bdot(s, w_ref[...]) + b_ref[...]


def _adaln(c, w, b):
    m, k = c.shape
    n = w.shape[1]
    tn = 1024
    return pl.pallas_call(
        _adaln_kernel,
        grid=(n // tn,),
        in_specs=[pl.BlockSpec((m, k), lambda j: (0, 0)),
                  pl.BlockSpec((k, tn), lambda j: (0, j)),
                  pl.BlockSpec((1, tn), lambda j: (0, j))],
        out_specs=pl.BlockSpec((m, tn), lambda j: (0, j)),
        out_shape=jax.ShapeDtypeStruct((m, n), F32),
        compiler_params=_cparams(("arbitrary",)),
    )(c, w, b.reshape(1, n))


def _inproj_kernel(x_ref, sc_ref, sh_ref, g_ref, w_ref, wqv_ref, bf_ref, *out_refs, transposed, q_scale):
    nbb, rb, d = x_ref.shape
    rows = nbb * rb
    x = x_ref[...]
    h = _rms(x, g_ref[...]) * (1.0 + sc_ref[...]) + sh_ref[...]
    hb = h.reshape(rows, d).astype(BF16)
    proj = jnp.dot(hb, w_ref[...], preferred_element_type=F32)
    u_ref, lf_ref = out_refs[:2]
    u_ref[...] = proj[:, :D_SSM]
    k = proj[:, D_SSM:D_SSM + D_ATTN]
    z = proj[:, proj.shape[1] - LANES:] + bf_ref[...]
    lf_ref[...] = jnp.minimum(z, 0.0) - jnp.log1p(jnp.exp(-jnp.abs(z)))
    nt = (((1,), (1,)), ((), ()))
    if transposed:
        kb_ref, qt_ref, vt_ref, ktf_ref, vtf_ref = out_refs[2:]
        kb_ref[...] = k.astype(BF16)
        qt = (lax.dot_general(wqv_ref[0], hb, nt, preferred_element_type=F32) * q_scale).astype(BF16)
        ktf_ref[0] = lax.dot_general(wqv_ref[1], hb, nt, preferred_element_type=F32)
        vt = lax.dot_general(wqv_ref[2], hb, nt, preferred_element_type=F32)
        vtf_ref[0] = vt
        vt = vt.astype(BF16)
        for c in range(rows // ATTN_TILE):
            qt_ref[c] = qt[:, c * ATTN_TILE:(c + 1) * ATTN_TILE]
            vt_ref[c] = vt[:, c * ATTN_TILE:(c + 1) * ATTN_TILE]
    else:
        k_ref, v_ref, q_ref = out_refs[2:]
        k_ref[...] = k
        v_ref[...] = proj[:, D_SSM + D_ATTN:D_SSM + 2 * D_ATTN]
        q_ref[...] = lax.dot_general(hb, wqv_ref[0], nt, preferred_element_type=F32) * q_scale


def _inproj(x3, sc, sh, g, w_main, w_qv, b_f, nbb, rb, transposed, q_scale):
    nb, r, d = x3.shape
    t = nb * r
    rows = nbb * rb
    nj = r // rb
    grid = (nb // nbb, nj)
    row_map = lambda i, j: (i * nj + j, 0)
    n_main = w_main.shape[1]
    outs = [jax.ShapeDtypeStruct((t, D_SSM), F32), jax.ShapeDtypeStruct((t, LANES), F32)]
    out_specs = [pl.BlockSpec((rows, D_SSM), row_map), pl.BlockSpec((rows, LANES), row_map)]
    if transposed:
        assert nbb == 1, "transposed outputs are laid out per sequence"
        nc = rows // ATTN_TILE
        chunk_map = lambda i, j: (i * nj + j, 0, 0)
        seq_map = lambda i, j: (i, 0, j)
        outs += [jax.ShapeDtypeStruct((t, D_ATTN), BF16),
                 jax.ShapeDtypeStruct((t // ATTN_TILE, D_ATTN, ATTN_TILE), BF16),
                 jax.ShapeDtypeStruct((t // ATTN_TILE, D_ATTN, ATTN_TILE), BF16),
                 jax.ShapeDtypeStruct((nb, D_ATTN, r), F32),
                 jax.ShapeDtypeStruct((nb, D_ATTN, r), F32)]
        out_specs += [pl.BlockSpec((rows, D_ATTN), row_map),
                      pl.BlockSpec((nc, D_ATTN, ATTN_TILE), chunk_map),
                      pl.BlockSpec((nc, D_ATTN, ATTN_TILE), chunk_map),
                      pl.BlockSpec((1, D_ATTN, rows), seq_map),
                      pl.BlockSpec((1, D_ATTN, rows), seq_map)]
    else:
        outs += [jax.ShapeDtypeStruct((t, D_ATTN), F32)] * 3
        out_specs += [pl.BlockSpec((rows, D_ATTN), row_map)] * 3
    return pl.pallas_call(
        functools.partial(_inproj_kernel, transposed=transposed, q_scale=q_scale),
        grid=grid,
        in_specs=[pl.BlockSpec((nbb, rb, d), lambda i, j: (i, j, 0)),
                  pl.BlockSpec((nbb, 1, d), lambda i, j: (i, 0, 0)),
                  pl.BlockSpec((nbb, 1, d), lambda i, j: (i, 0, 0)),
                  pl.BlockSpec((1, 1, d), lambda i, j: (0, 0, 0)),
                  pl.BlockSpec((d, n_main), lambda i, j: (0, 0)),
                  pl.BlockSpec((3, D_ATTN, d), lambda i, j: (0, 0, 0)),
                  pl.BlockSpec((1, LANES), lambda i, j: (0, 0))],
        out_specs=out_specs,
        out_shape=outs,
        compiler_params=_cparams(("arbitrary", "arbitrary")),
    )(x3, sc, sh, g, w_main, w_qv, b_f)


def _crep_kernel(lf_ref, tri_ref, ex_ref, o_ref, carry_ref, *, scale):
    j = pl.program_id(1)

    @pl.when(j == 0)
    def _():
        carry_ref[...] = jnp.zeros_like(carry_ref)

    tri = tri_ref[...]
    ex = ex_ref[...]
    cs = sum(jnp.dot(tri, p, preferred_element_type=F32) for p in _split3(lf_ref[...]))
    rep = sum(jnp.dot(p, ex, preferred_element_type=F32) for p in _split3(cs))
    rep = rep + carry_ref[...]
    rows = rep.shape[0]
    carry_ref[...] = rep[rows - 1:rows, :]
    for h in range(N_HEADS):
        o_ref[0, h] = rep[:, h * LANES:(h + 1) * LANES] * (-scale)


def _crep(lf128, nb, length, scale):
    rows = ATTN_TILE
    nj = length // rows
    tri = np.tril(np.ones((rows, rows), np.float32))
    ex = np.zeros((LANES, N_HEADS * LANES), np.float32)
    for h in range(N_HEADS):
        ex[h, h * LANES:(h + 1) * LANES] = 1.0
    return pl.pallas_call(
        functools.partial(_crep_kernel, scale=scale),
        grid=(nb, nj),
        in_specs=[pl.BlockSpec((rows, LANES), lambda i, j: (i * nj + j, 0)),
                  pl.BlockSpec((rows, rows), lambda i, j: (0, 0)),
                  pl.BlockSpec((LANES, N_HEADS * LANES), lambda i, j: (0, 0))],
        out_specs=pl.BlockSpec((1, N_HEADS, rows, LANES), lambda i, j: (i, 0, j, 0)),
        out_shape=jax.ShapeDtypeStruct((nb, N_HEADS, length, LANES), F32),
        scratch_shapes=[pltpu.VMEM((1, N_HEADS * LANES), F32)],
        compiler_params=_cparams(("arbitrary", "arbitrary")),
    )(lf128, jnp.asarray(tri, BF16), jnp.asarray(ex, BF16))


def _cumsum_kernel(x_ref, tri_ref, lt_ref, o_ref):
    tri = tri_ref[...]
    lt = lt_ref[...]
    cs = sum(jnp.dot(p, tri, preferred_element_type=F32) for p in _split3(x_ref[...]))
    tot = jnp.broadcast_to(cs[:, LANES - 1:LANES], cs.shape)
    carry = sum(jnp.dot(lt, p, preferred_element_type=F32) for p in _split3(tot))
    o_ref[...] = -(cs + carry)


def _neg_cumsum(x2, gsz, rb):
    r = x2.shape[0]
    tri = np.triu(np.ones((LANES, LANES), np.float32))
    ii = np.arange(rb)
    lt = ((ii[:, None] // gsz == ii[None, :] // gsz) & (ii[None, :] < ii[:, None])).astype(np.float32)
    return pl.pallas_call(
        _cumsum_kernel,
        grid=(r // rb,),
        in_specs=[pl.BlockSpec((rb, LANES), lambda i: (i, 0)),
                  pl.BlockSpec((LANES, LANES), lambda i: (0, 0)),
                  pl.BlockSpec((rb, rb), lambda i: (0, 0))],
        out_specs=pl.BlockSpec((rb, LANES), lambda i: (i, 0)),
        out_shape=jax.ShapeDtypeStruct((r, LANES), F32),
        compiler_params=_cparams(("arbitrary",)),
    )(x2, jnp.asarray(tri, BF16), jnp.asarray(lt, BF16))


def _gelu_tanh(x):
    return 0.5 * x * (1.0 + jnp.tanh(math.sqrt(2.0 / math.pi) * (x + 0.044715 * (x * x * x))))


def _s5_kernel(u_ref, h0_ref, wb_ref, wc_ref, are_ref, aim_ref, dsk_ref, wglu_ref, bglu_ref, gout_ref,
               y_ref, ht_ref, s_ref, hc_ref, *, nseq, tm):
    rows = nseq * tm
    sr = rows + SUBLANES
    ti = pl.program_id(1)

    @pl.when(ti == 0)
    def _():
        hc_ref[...] = h0_ref[...]

    u = u_ref[...].reshape(rows, D_SSM)
    ub = u.astype(BF16)
    for c in range(4):
        bu = jnp.dot(ub[:, c * LANES:(c + 1) * LANES], wb_ref[c], preferred_element_type=F32)
        for jj in range(4):
            s_ref[pl.ds((4 * c + jj) * sr, rows), :] = bu[:, jj * LANES:(jj + 1) * LANES]
            s_ref[pl.ds((16 + 4 * c + jj) * sr, rows), :] = bu[:, 512 + jj * LANES:512 + (jj + 1) * LANES]

    ar = (are_ref[0:8, :], are_ref[8:16, :])
    ai = (aim_ref[0:8, :], aim_ref[8:16, :])

    def seq_group(sg, carry):
        base = sg * 4
        hs = []
        for b in range(4):
            hs.append(tuple(hc_ref[base + b, pl.ds(8 * q, 8), :] for q in range(4)))

        def step(t, hs):
            new = []
            for b in range(4):
                row = (base + b) * tm + t
                hr0, hr1, hi0, hi1 = hs[b]
                bre0 = s_ref[pl.ds(row, 8, stride=sr), :]
                bre1 = s_ref[pl.ds(8 * sr + row, 8, stride=sr), :]
                bim0 = s_ref[pl.ds(16 * sr + row, 8, stride=sr), :]
                bim1 = s_ref[pl.ds(24 * sr + row, 8, stride=sr), :]
                nr0 = ar[0] * hr0 - ai[0] * hi0 + bre0
                nr1 = ar[1] * hr1 - ai[1] * hi1 + bre1
                ni0 = ar[0] * hi0 + ai[0] * hr0 + bim0
                ni1 = ar[1] * hi1 + ai[1] * hr1 + bim1
                s_ref[pl.ds(row, 8, stride=sr), :] = nr0
                s_ref[pl.ds(8 * sr + row, 8, stride=sr), :] = nr1
                s_ref[pl.ds(16 * sr + row, 8, stride=sr), :] = ni0
                s_ref[pl.ds(24 * sr + row, 8, stride=sr), :] = ni1
                new.append((nr0, nr1, ni0, ni1))
            return tuple(new)

        hs = lax.fori_loop(0, tm, step, tuple(hs), unroll=2)
        for b in range(4):
            for q in range(4):
                hc_ref[base + b, pl.ds(8 * q, 8), :] = hs[b][q]
        return carry

    lax.fori_loop(0, nseq // 4, seq_group, 0)
    ht_ref[...] = hc_ref[...]

    ys = []
    for c in range(4):
        blocks = [s_ref[pl.ds((4 * c + jj) * sr, rows), :].astype(BF16) for jj in range(4)]
        blocks += [s_ref[pl.ds((16 + 4 * c + jj) * sr, rows), :].astype(BF16) for jj in range(4)]
        hcat = jnp.concatenate(blocks, axis=1)
        ys.append(jnp.dot(hcat, wc_ref[c], preferred_element_type=F32))
    y = jnp.concatenate(ys, axis=1) + dsk_ref[...] * u
    y = _gelu_tanh(y)
    gate = jax.nn.sigmoid(jnp.dot(y.astype(BF16), wglu_ref[...], preferred_element_type=F32) + bglu_ref[...])
    y = y * gate
    y_ref[...] = _rms(y, gout_ref[...]).astype(BF16).reshape(y_ref.shape)


def _s5(u3, h0, wb, wc, a_re, a_im, dsk, wglu, bglu, gout, nseq, tm):
    nb, length, _ = u3.shape
    rows = nseq * tm
    sr = rows + SUBLANES
    grid = (nb // nseq, length // tm)
    const2 = lambda i, j: (0, 0)
    const3 = lambda i, j: (0, 0, 0)
    if tm == length:
        y_spec = pl.BlockSpec((rows, D_SSM), lambda i, j: (i, 0))
        y_shape = jax.ShapeDtypeStruct((nb * length, D_SSM), BF16)
    else:
        y_spec = pl.BlockSpec((nseq, tm, D_SSM), lambda i, j: (i, j, 0))
        y_shape = jax.ShapeDtypeStruct((nb, length, D_SSM), BF16)
    return pl.pallas_call(
        functools.partial(_s5_kernel, nseq=nseq, tm=tm),
        grid=grid,
        in_specs=[pl.BlockSpec((nseq, tm, D_SSM), lambda i, j: (i, j, 0)),
                  pl.BlockSpec((nseq, 32, LANES), lambda i, j: (i, 0, 0)),
                  pl.BlockSpec((4, LANES, 1024), const3),
                  pl.BlockSpec((4, 1024, LANES), const3),
                  pl.BlockSpec((16, LANES), const2),
                  pl.BlockSpec((16, LANES), const2),
                  pl.BlockSpec((1, D_SSM), const2),
                  pl.BlockSpec((D_SSM, D_SSM), const2),
                  pl.BlockSpec((1, D_SSM), const2),
                  pl.BlockSpec((1, D_SSM), const2)],
        out_specs=[y_spec, pl.BlockSpec((nseq, 32, LANES), lambda i, j: (i, 0, 0))],
        out_shape=[y_shape, jax.ShapeDtypeStruct((nb, 32, LANES), F32)],
        scratch_shapes=[pltpu.VMEM((32 * sr, LANES), F32), pltpu.VMEM((nseq, 32, LANES), F32)],
        compiler_params=_cparams(("arbitrary", "arbitrary")),
    )(u3, h0, wb, wc, a_re, a_im, dsk, wglu, bglu, gout)


def _s5_params(lam_re, lam_im, log_step, b_re, b_im, c_re, c_im):
    g = lam_re.shape[0]
    dt = jnp.exp(log_step)[:, None]
    mag = jnp.exp(lam_re * dt)
    a_re = mag * jnp.cos(lam_im * dt)
    a_im = mag * jnp.sin(lam_im * dt)
    den = lam_re * lam_re + lam_im * lam_im
    n_re = a_re - 1.0
    f_re = (n_re * lam_re + a_im * lam_im) / den
    f_im = (a_im * lam_re - n_re * lam_im) / den
    bb_re = f_re[..., None] * b_re - f_im[..., None] * b_im
    bb_im = f_re[..., None] * b_im + f_im[..., None] * b_re
    eye = jnp.eye(g, dtype=F32)
    n_state = g * SSM_STATE

    def in_mat(bb):
        return jnp.einsum('gpi,gh->gihp', bb, eye).reshape(g * SSM_GROUP, n_state)

    def out_mat(cc):
        return jnp.einsum('gip,gh->gphi', cc, eye).reshape(n_state, g * SSM_GROUP)

    wbr, wbi = in_mat(bb_re), in_mat(bb_im)
    wcr, wci = out_mat(c_re), out_mat(-c_im)
    wb = jnp.stack([jnp.concatenate([wbr[c * 128:(c + 1) * 128, c * 512:(c + 1) * 512],
                                     wbi[c * 128:(c + 1) * 128, c * 512:(c + 1) * 512]], axis=1)
                    for c in range(4)]).astype(BF16)
    wc = jnp.stack([jnp.concatenate([wcr[c * 512:(c + 1) * 512, c * 128:(c + 1) * 128],
                                     wci[c * 512:(c + 1) * 512, c * 128:(c + 1) * 128]], axis=0)
                    for c in range(4)]).astype(BF16)
    return wb, wc, a_re.reshape(16, LANES), a_im.reshape(16, LANES)


def _attn_kernel(qt_ref, k_ref, vt_ref, cr_ref, g_ref, o_ref, qz_ref, m_ref, l_ref, acc_ref, st_ref):
    tq = ATTN_TILE
    qi = pl.program_id(1)
    half = LANES // 2

    rowid = lax.broadcasted_iota(I32, (LANES, tq), 0)
    for j in range(N_HEADS // 2):
        qp = qt_ref[0, j * LANES:(j + 1) * LANES, :]
        qz_ref[2 * j] = jnp.where(rowid < half, qp, jnp.zeros_like(qp))
        qz_ref[2 * j + 1] = jnp.where(rowid >= half, qp, jnp.zeros_like(qp))
    m_ref[...] = jnp.full_like(m_ref, NEG_INF)
    l_ref[...] = jnp.zeros_like(l_ref)
    acc_ref[...] = jnp.zeros_like(acc_ref)

    key_row = lax.broadcasted_iota(I32, (tq, tq), 0)
    q_col = lax.broadcasted_iota(I32, (tq, tq), 1)
    causal = key_row <= q_col

    def tiles(kbs, masked):
        m_new = [m_ref[h:h + 1, :] for h in range(N_HEADS)]
        for c, kb in enumerate(kbs):
            ks = pl.multiple_of(kb * tq, tq)
            for j in range(N_HEADS // 2):
                kp = k_ref[0, pl.ds(ks, tq), j * LANES:(j + 1) * LANES]
                for e in range(2):
                    h = 2 * j + e
                    st = jnp.dot(kp, qz_ref[h], preferred_element_type=F32)
                    bias = cr_ref[0, h, pl.ds(ks, tq), :]
                    st = st + jnp.concatenate([bias] * (tq // LANES), axis=1)
                    if masked:
                        st = jnp.where(causal, st, NEG_INF)
                    st_ref[c, h] = st
                    m_new[h] = jnp.maximum(m_new[h], jnp.max(st, axis=0, keepdims=True))
        for h in range(N_HEADS):
            alpha = jnp.exp2(m_ref[h:h + 1, :] - m_new[h])
            rows = slice(h * HEAD_DIM, (h + 1) * HEAD_DIM)
            l_new = alpha * l_ref[h:h + 1, :]
            acc = alpha * acc_ref[rows, :]
            for c, kb in enumerate(kbs):
                p = jnp.exp2(st_ref[c, h] - m_new[h])
                l_new = l_new + jnp.sum(p, axis=0, keepdims=True)
                acc = acc + jnp.dot(vt_ref[kb, rows, :], p.astype(BF16), preferred_element_type=F32)
            l_ref[h:h + 1, :] = l_new
            m_ref[h:h + 1, :] = m_new[h]
            acc_ref[rows, :] = acc

    def body(g, c):
        tiles([g * ATTN_GROUP + i for i in range(ATTN_GROUP)], False)
        return c

    n_groups = qi // ATTN_GROUP
    lax.fori_loop(0, n_groups, body, 0)
    for r in range(1, ATTN_GROUP):
        @pl.when(qi - n_groups * ATTN_GROUP == r)
        def _(r=r):
            tiles([n_groups * ATTN_GROUP + i for i in range(r)], False)
    tiles([qi], True)

    for h in range(N_HEADS):
        rows = slice(h * HEAD_DIM, (h + 1) * HEAD_DIM)
        acc_ref[rows, :] = acc_ref[rows, :] / l_ref[h:h + 1, :]
    o = acc_ref[...].T
    o_ref[0] = _rms(o, g_ref[...]).astype(BF16)


def _attn_prompt(qt, k, vt, crep, g):
    b, length, d = k.shape
    tq = ATTN_TILE
    nq = length // tq
    once = pl.Buffered(1)
    return pl.pallas_call(
        _attn_kernel,
        grid=(b, nq),
        in_specs=[pl.BlockSpec((1, d, tq), lambda i, j: (i * nq + j, 0, 0)),
                  pl.BlockSpec((1, length, d), lambda i, j: (i, 0, 0), pipeline_mode=once),
                  pl.BlockSpec((nq, d, tq), lambda i, j: (i, 0, 0), pipeline_mode=once),
                  pl.BlockSpec((1, N_HEADS, length, LANES), lambda i, j: (i, 0, 0, 0), pipeline_mode=once),
                  pl.BlockSpec((1, d), lambda i, j: (0, 0))],
        out_specs=pl.BlockSpec((1, tq, d), lambda i, j: (i, j, 0)),
        out_shape=jax.ShapeDtypeStruct((b, length, d), BF16),
        scratch_shapes=[pltpu.VMEM((N_HEADS, LANES, tq), BF16), pltpu.VMEM((N_HEADS, tq), F32),
                        pltpu.VMEM((N_HEADS, tq), F32), pltpu.VMEM((d, tq), F32),
                        pltpu.VMEM((ATTN_GROUP, N_HEADS, tq, tq), F32)],
        compiler_params=_cparams(("arbitrary", "arbitrary")),
    )(qt, k, vt, crep, g)


def _attn_paged_kernel(pt_ref, q_ref, kn_ref, vn_ref, cn_ref, g_ref, kc_hbm, vc_hbm, o_ref,
                       kbuf, vbuf, sem, *, n_pages, n_new):
    i = pl.program_id(0)
    n = pl.num_programs(0)
    n_past = n_pages * PAGE

    def page_copies(seq, slot):
        cps = []
        for p in range(n_pages):
            pg = pt_ref[seq * n_pages + p]
            cps.append(pltpu.make_async_copy(kc_hbm.at[pg], kbuf.at[slot, p], sem.at[0, slot]))
            cps.append(pltpu.make_async_copy(vc_hbm.at[pg], vbuf.at[slot, p], sem.at[1, slot]))
        return cps

    @pl.when(i == 0)
    def _():
        for cp in page_copies(0, 0):
            cp.start()

    @pl.when(i + 1 < n)
    def _():
        for cp in page_copies(i + 1, (i + 1) % 2):
            cp.start()

    slot = i % 2
    for cp in page_copies(i, slot):
        cp.wait()

    d = q_ref.shape[-1]
    nr = N_HEADS * n_new
    new_bits = n_new.bit_length() - 1
    head_bits = HEAD_DIM.bit_length() - 1
    rowh = lax.shift_right_logical(lax.broadcasted_iota(I32, (nr, d), 0), new_bits)
    colh = lax.shift_right_logical(lax.broadcasted_iota(I32, (nr, d), 1), head_bits)
    bd = rowh == colh
    qrep = jnp.broadcast_to(q_ref[0][None], (N_HEADS, n_new, d)).reshape(nr, d)
    qbd = jnp.where(bd, qrep, 0.0).astype(BF16)
    cn = cn_ref[0]
    cnr = jnp.broadcast_to(cn[:, None, :], (N_HEADS, n_new, cn.shape[-1])).reshape(nr, cn.shape[-1])

    s_p = []
    for p in range(n_pages):
        kt = kbuf[slot, p].reshape(d, PAGE)
        s_p.append(_bdot(qbd, kt) + cnr[:, p * PAGE:(p + 1) * PAGE])
    s_n = _bdot_nt(qbd, kn_ref[0]) + cnr[:, n_past:n_past + n_new]
    qpos = lax.broadcasted_iota(I32, (nr, n_new), 0) & (n_new - 1)
    kpos = lax.broadcasted_iota(I32, (nr, n_new), 1)
    s_n = jnp.where(kpos <= qpos, s_n, NEG_INF)
    m = jnp.max(s_n, axis=1, keepdims=True)
    for sp in s_p:
        m = jnp.maximum(m, jnp.max(sp, axis=1, keepdims=True))
    p_n = jnp.exp(s_n - m)
    l = jnp.sum(p_n, axis=1, keepdims=True)
    of = _bdot(p_n, vn_ref[0])
    for p in range(n_pages):
        pp = jnp.exp(s_p[p] - m)
        l = l + jnp.sum(pp, axis=1, keepdims=True)
        of = of + _bdot_nt(pp, vbuf[slot, p].reshape(d, PAGE))
    of = jnp.where(bd, of / l, 0.0)
    o = jnp.sum(of.reshape(N_HEADS, n_new, d), axis=0)
    o_ref[0] = _rms(o, g_ref[...])


def _attn_paged(page_table, q, kn, vn, cn, g, cache_k, cache_v):
    nseq, n_new, d = q.shape
    assert n_new & (n_new - 1) == 0, "new-token count must be a power of two"
    n_pages = page_table.shape[1]
    n_past = n_pages * PAGE
    grid_spec = pltpu.PrefetchScalarGridSpec(
        num_scalar_prefetch=1,
        grid=(nseq,),
        in_specs=[pl.BlockSpec((1, n_new, d), lambda i, pt: (i, 0, 0)),
                  pl.BlockSpec((1, n_new, d), lambda i, pt: (i, 0, 0)),
                  pl.BlockSpec((1, n_new, d), lambda i, pt: (i, 0, 0)),
                  pl.BlockSpec((1, N_HEADS, cn.shape[-1]), lambda i, pt: (i, 0, 0)),
                  pl.BlockSpec((1, d), lambda i, pt: (0, 0)),
                  pl.BlockSpec(memory_space=pl.ANY),
                  pl.BlockSpec(memory_space=pl.ANY)],
        out_specs=pl.BlockSpec((1, n_new, d), lambda i, pt: (i, 0, 0)),
        scratch_shapes=[pltpu.VMEM((2, n_pages, N_HEADS, HEAD_DIM, PAGE), F32),
                        pltpu.VMEM((2, n_pages, N_HEADS, HEAD_DIM, PAGE), F32),
                        pltpu.SemaphoreType.DMA((2, 2))],
    )
    return pl.pallas_call(
        functools.partial(_attn_paged_kernel, n_pages=n_pages, n_new=n_new),
        grid_spec=grid_spec,
        out_shape=jax.ShapeDtypeStruct((nseq, n_new, d), F32),
        compiler_params=_cparams(("arbitrary",)),
    )(page_table.reshape(-1), q, kn, vn, cn, g, cache_k, cache_v)


def _outproj_kernel(x_ref, ys_ref, oa_ref, gt_ref, sc_ref, sh_ref, g_ref, wo_ref, wrt_ref,
                    x1_ref, h2_ref, lg_ref):
    nbb, rb, d = x_ref.shape
    mix = jnp.dot(ys_ref[...], wo_ref[0:D_SSM, :], preferred_element_type=F32)
    mix = mix + jnp.dot(oa_ref[...], wo_ref[D_SSM:, :], preferred_element_type=F32)
    x1 = x_ref[...] + gt_ref[...] * mix.reshape(nbb, rb, d)
    x1_ref[...] = x1
    h2 = (_rms(x1, g_ref[...]) * (1.0 + sc_ref[...]) + sh_ref[...]).reshape(nbb * rb, d)
    hb = h2.astype(BF16)
    h2_ref[...] = _pack_bf16_pairs(hb)
    hlo = (h2 - hb.astype(F32)).astype(BF16)
    whi = wrt_ref[0]
    wlo = wrt_ref[1]
    nt = (((1,), (1,)), ((), ()))
    lg = lax.dot_general(whi, hb, nt, preferred_element_type=F32)
    lg = lg + lax.dot_general(wlo, hb, nt, preferred_element_type=F32)
    lg = lg + lax.dot_general(whi, hlo, nt, preferred_element_type=F32)
    lg_ref[...] = lg


def _outproj(x3, ys, oa, gt, sc, sh, g, wo, wrt, nbb, rb):
    nb, r, d = x3.shape
    t = nb * r
    rows = nbb * rb
    nj = r // rb
    row_map = lambda i, j: (i * nj + j, 0)
    mod_spec = pl.BlockSpec((nbb, 1, d), lambda i, j: (i, 0, 0))
    return pl.pallas_call(
        _outproj_kernel,
        grid=(nb // nbb, nj),
        in_specs=[pl.BlockSpec((nbb, rb, d), lambda i, j: (i, j, 0)),
                  pl.BlockSpec((rows, D_SSM), row_map),
                  pl.BlockSpec((rows, D_ATTN), row_map),
                  mod_spec, mod_spec, mod_spec,
                  pl.BlockSpec((1, 1, d), lambda i, j: (0, 0, 0)),
                  pl.BlockSpec((D_SSM + D_ATTN, d), lambda i, j: (0, 0)),
                  pl.BlockSpec((2, N_EXPERTS, d), lambda i, j: (0, 0, 0))],
        out_specs=[pl.BlockSpec((nbb, rb, d), lambda i, j: (i, j, 0)),
                   pl.BlockSpec((rows, d // 2), row_map),
                   pl.BlockSpec((N_EXPERTS, rows), lambda i, j: (0, i * nj + j))],
        out_shape=[jax.ShapeDtypeStruct((nb, r, d), F32),
                   jax.ShapeDtypeStruct((t, d // 2), jnp.uint32),
                   jax.ShapeDtypeStruct((N_EXPERTS, t), F32)],
        compiler_params=_cparams(("arbitrary", "arbitrary")),
    )(x3, ys, oa, gt, sc, sh, g, wo, wrt)


def _router_kernel(lg_ref, rb_ref, ut_ref, idx_ref, w_ref, rank_ref, cnt_ref, carry_ref):
    i = pl.program_id(0)
    tm = lg_ref.shape[1]
    per_group = N_EXPERTS // N_EXPERT_GROUPS

    @pl.when(i == 0)
    def _():
        carry_ref[...] = jnp.zeros_like(carry_ref)

    scores = jax.nn.sigmoid(lg_ref[...])
    biased = scores + rb_ref[...]
    blks, grp = [], []
    for g in range(N_EXPERT_GROUPS):
        blk = biased[g * per_group:(g + 1) * per_group, :]
        m1 = jnp.max(blk, axis=0, keepdims=True)
        eq = blk == m1
        n_eq = jnp.sum(jnp.where(eq, 1.0, 0.0), axis=0, keepdims=True)
        m2 = jnp.max(jnp.where(eq, NEG_INF, blk), axis=0, keepdims=True)
        blks.append(blk)
        grp.append(m1 + jnp.where(n_eq >= 2.0, m1, m2))
    masked = []
    for g in range(N_EXPERT_GROUPS):
        beaten = jnp.zeros((1, tm), F32)
        for o in range(N_EXPERT_GROUPS):
            if o == g:
                continue
            ahead = (grp[o] >= grp[g]) if o < g else (grp[o] > grp[g])
            beaten = beaten + jnp.where(ahead, 1.0, 0.0)
        masked.append(jnp.where(beaten < float(TOPK_GROUPS), blks[g], NEG_INF))
    work = jnp.concatenate(masked, axis=0)

    eid = lax.broadcasted_iota(I32, (N_EXPERTS, tm), 0)
    chosen = jnp.zeros((N_EXPERTS, tm), F32)
    idxs, ws, sels = [], [], []
    for _ in range(TOP_K):
        m = jnp.max(work, axis=0, keepdims=True)
        first = jnp.min(jnp.where(work == m, eid, N_EXPERTS), axis=0, keepdims=True)
        sel = eid == first
        idxs.append(first)
        ws.append(jnp.sum(jnp.where(sel, scores, 0.0), axis=0, keepdims=True))
        sels.append(sel)
        chosen = jnp.where(sel, 1.0, chosen)
        work = jnp.where(sel, NEG_INF, work)
    wsum = ws[0]
    for wk in ws[1:]:
        wsum = wsum + wk

    prefix = jnp.dot(chosen.astype(BF16), ut_ref[...], preferred_element_type=F32) + carry_ref[...]
    carry_ref[...] = carry_ref[...] + jnp.sum(chosen, axis=1, keepdims=True)
    cnt_ref[...] = carry_ref[...]

    idx_ref[...] = jnp.zeros_like(idx_ref)
    w_ref[...] = jnp.zeros_like(w_ref)
    rank_ref[...] = jnp.zeros_like(rank_ref)
    for k in range(TOP_K):
        idx_ref[k:k + 1, :] = idxs[k]
        w_ref[k:k + 1, :] = ws[k] / wsum * ROUTED_SCALE
        rank = jnp.sum(jnp.where(sels[k], prefix, 0.0), axis=0, keepdims=True)
        rank_ref[k:k + 1, :] = rank.astype(I32)


def _router(lg, router_bias, tm):
    e, t = lg.shape
    ut = np.triu(np.ones((tm, tm), np.float32), 1)
    tok_spec = pl.BlockSpec((SUBLANES, tm), lambda i: (0, i))
    return pl.pallas_call(
        _router_kernel,
        grid=(t // tm,),
        in_specs=[pl.BlockSpec((e, tm), lambda i: (0, i)),
                  pl.BlockSpec((e, 1), lambda i: (0, 0)),
                  pl.BlockSpec((tm, tm), lambda i: (0, 0))],
        out_specs=[tok_spec, tok_spec, tok_spec, pl.BlockSpec((e, 1), lambda i: (0, 0))],
        out_shape=[jax.ShapeDtypeStruct((SUBLANES, t), I32), jax.ShapeDtypeStruct((SUBLANES, t), F32),
                   jax.ShapeDtypeStruct((SUBLANES, t), I32), jax.ShapeDtypeStruct((e, 1), F32)],
        scratch_shapes=[pltpu.VMEM((e, 1), F32)],
        compiler_params=_cparams(("arbitrary",)),
    )(lg, router_bias.reshape(e, 1), jnp.asarray(ut, BF16))


def _experts_kernel(be_ref, nv_ref, new_ref, fill_ref, x_ref, wg_ref, wu_ref, wd_ref, o_ref, wgu_s, wd_s):
    i = pl.program_id(0)
    de = wg_ref.shape[-1]
    valid = i < nv_ref[0]

    @pl.when(valid & (new_ref[i] == 1))
    def _():
        wgu_s[:, :de] = wg_ref[0].astype(BF16)
        wgu_s[:, de:] = wu_ref[0].astype(BF16)
        wd_s[...] = wd_ref[0].astype(BF16)

    @pl.when(valid)
    def _():
        row = lax.broadcasted_iota(I32, x_ref.shape, 0)
        x = jnp.where(row < fill_ref[i], x_ref[...], jnp.uint32(0))
        gu = jnp.dot(_unpack_bf16_pairs(x), wgu_s[...], preferred_element_type=F32)
        g = gu[:, :de]
        a = (g * jax.nn.sigmoid(g) * gu[:, de:]).astype(BF16)
        o_ref[...] = jnp.dot(a, wd_s[...], preferred_element_type=F32)

    @pl.when(jnp.logical_not(valid))
    def _():
        o_ref[...] = jnp.zeros_like(o_ref)


def _experts(block_e, n_valid, block_new, block_fill, xs, w_eg, w_eu, w_ed, m):
    n_rows = xs.shape[0]
    d, de = w_eg.shape[-2:]
    nblk = n_rows // m
    xmap = lambda i, be, nv, *_: (jnp.minimum(i, nv[0] - 1), 0)
    wmap = lambda i, be, *_: (be[i], 0, 0)
    grid_spec = pltpu.PrefetchScalarGridSpec(
        num_scalar_prefetch=4,
        grid=(nblk,),
        in_specs=[pl.BlockSpec((m, xs.shape[1]), xmap),
                  pl.BlockSpec((1, d, de), wmap),
                  pl.BlockSpec((1, d, de), wmap),
                  pl.BlockSpec((1, de, d), wmap)],
        out_specs=pl.BlockSpec((m, d), lambda i, *_: (i, 0)),
        scratch_shapes=[pltpu.VMEM((d, 2 * de), BF16), pltpu.VMEM((de, d), BF16)],
    )
    return pl.pallas_call(
        _experts_kernel,
        grid_spec=grid_spec,
        out_shape=jax.ShapeDtypeStruct((n_rows, d), F32),
        compiler_params=_cparams(("arbitrary",)),
    )(block_e, n_valid, block_new, block_fill, xs, w_eg, w_eu, w_ed)


def _gather_rows(table, idx, window):
    n = idx.shape[0]
    d = table.shape[1]
    info = plsc.get_sparse_core_info()
    n_workers = info.num_cores * info.num_subcores
    per_worker = n // n_workers
    n_buf = SC_GATHER_BUFFERS
    n_rounds = per_worker // (window * n_buf)
    assert n_rounds * window * n_buf * n_workers == n
    mesh = plsc.VectorSubcoreMesh(core_axis_name="c", subcore_axis_name="s")

    @functools.partial(
        pl.kernel, out_type=jax.ShapeDtypeStruct((n, d), table.dtype), mesh=mesh,
        scratch_types=[pltpu.VMEM((per_worker,), I32), pltpu.VMEM((n_buf, window, d), table.dtype),
                       pltpu.SemaphoreType.DMA((n_buf,)), pltpu.SemaphoreType.DMA((n_buf,))])
    def gather(x_hbm, i_hbm, o_hbm, i_v, buf, gsem, wsem):
        base = (lax.axis_index("s") * info.num_cores + lax.axis_index("c")) * per_worker
        pltpu.sync_copy(i_hbm.at[pl.ds(base, per_worker)], i_v)

        @pl.loop(0, n_rounds)
        def _(rnd):
            first = rnd * (window * n_buf)
            reads, writes = [], []
            for b in range(n_buf):
                rows = pl.ds(first + b * window, window)
                reads.append(pltpu.make_async_copy(x_hbm.at[i_v.at[rows]], buf.at[b], gsem.at[b]))
                writes.append(pltpu.make_async_copy(buf.at[b], o_hbm.at[pl.ds(base + first + b * window, window)],
                                                    wsem.at[b]))
            for cp in reads:
                cp.start()
            for b in range(n_buf):
                reads[b].wait()
                writes[b].start()
            for cp in writes:
                cp.wait()

    return gather(table, idx)


def _scatter_rows(x, dest_t, n_rows, window):
    t, d = x.shape
    info = plsc.get_sparse_core_info()
    n_workers = info.num_cores * info.num_subcores
    per_worker = t // n_workers
    chunks = per_worker // window
    n_buf = 2
    n_rounds = chunks // n_buf
    assert n_rounds * n_buf * window * n_workers == t
    idx_rows = -(-(TOP_K * chunks) // SUBLANES) * SUBLANES
    idx = dest_t[:TOP_K].reshape(TOP_K, n_workers, chunks, window).transpose(1, 0, 2, 3)
    idx = idx.reshape(n_workers, TOP_K * chunks, window)
    idx = jnp.pad(idx, ((0, 0), (0, idx_rows - TOP_K * chunks), (0, 0))).reshape(n_workers * idx_rows, window)
    mesh = plsc.VectorSubcoreMesh(core_axis_name="c", subcore_axis_name="s")

    @functools.partial(
        pl.kernel, out_type=jax.ShapeDtypeStruct((n_rows, d), x.dtype), mesh=mesh,
        scratch_types=[pltpu.VMEM((idx_rows, window), I32), pltpu.VMEM((n_buf, window, d), x.dtype),
                       pltpu.SemaphoreType.DMA((n_buf,)), pltpu.SemaphoreType.DMA((n_buf,))])
    def scatter(x_hbm, i_hbm, o_hbm, i_v, buf, rsem, wsem):
        wid = lax.axis_index("s") * info.num_cores + lax.axis_index("c")
        base = wid * per_worker
        pltpu.sync_copy(i_hbm.at[pl.ds(wid * idx_rows, idx_rows)], i_v)

        @pl.loop(0, n_rounds)
        def _(rnd):
            reads = [pltpu.make_async_copy(x_hbm.at[pl.ds(base + (rnd * n_buf + b) * window, window)],
                                           buf.at[b], rsem.at[b]) for b in range(n_buf)]
            for cp in reads:
                cp.start()
            writes = []
            for b in range(n_buf):
                reads[b].wait()
                for k in range(TOP_K):
                    cp = pltpu.make_async_copy(buf.at[b], o_hbm.at[i_v.at[k * chunks + rnd * n_buf + b]],
                                               wsem.at[b])
                    cp.start()
                    writes.append(cp)
            for cp in writes:
                cp.wait()

    return scatter(x, idx)


def _final_kernel(x1_ref, h2_ref, w_ref, gt_ref, sc_ref, sh_ref, g_ref, wgu_ref, wd_ref, *rest):
    yg_refs = rest[:TOP_K]
    y_ref = rest[TOP_K]
    nbb, rb, d = x1_ref.shape
    ds = wd_ref.shape[0]
    gu = jnp.dot(_unpack_bf16_pairs(h2_ref[...]), wgu_ref[...], preferred_element_type=F32)
    g = gu[:, :ds]
    a = (g * jax.nn.sigmoid(g) * gu[:, ds:]).astype(BF16)
    ff = jnp.dot(a, wd_ref[...], preferred_element_type=F32)
    routed = yg_refs[0][...] * w_ref[:, 0:1]
    for k in range(1, TOP_K):
        routed = routed + yg_refs[k][...] * w_ref[:, k:k + 1]
    ff = routed + ff
    x2 = x1_ref[...] + gt_ref[...] * ff.reshape(nbb, rb, d)
    y_ref[...] = _rms(x2, g_ref[...]) * (1.0 + sc_ref[...]) + sh_ref[...]


def _final(x1, h2, wts, yg, first_tile, gt, sc, sh, g, wgu, wd, nbb, rb):
    nb, r, d = x1.shape
    rows = nbb * rb
    nj = r // rb
    tiles_all = yg.shape[0] // TOP_K // rows
    row_map = lambda i, j: (i * nj + j, 0)
    mod_spec = pl.BlockSpec((nbb, 1, d), lambda i, j: (i, 0, 0))
    slab_specs = [pl.BlockSpec((rows, d), lambda i, j, k=k: (k * tiles_all + first_tile + i * nj + j, 0))
                  for k in range(TOP_K)]
    return pl.pallas_call(
        _final_kernel,
        grid=(nb // nbb, nj),
        in_specs=[pl.BlockSpec((nbb, rb, d), lambda i, j: (i, j, 0)),
                  pl.BlockSpec((rows, h2.shape[1]), row_map),
                  pl.BlockSpec((rows, SUBLANES), row_map),
                  mod_spec, mod_spec, mod_spec,
                  pl.BlockSpec((1, 1, d), lambda i, j: (0, 0, 0)),
                  pl.BlockSpec(wgu.shape, lambda i, j: (0, 0)),
                  pl.BlockSpec(wd.shape, lambda i, j: (0, 0))] + slab_specs,
        out_specs=pl.BlockSpec((nbb, rb, d), lambda i, j: (i, j, 0)),
        out_shape=jax.ShapeDtypeStruct((nb, r, d), F32),
        compiler_params=_cparams(("arbitrary", "arbitrary")),
    )(x1, h2, wts, gt, sc, sh, g, wgu, wd, *([yg] * TOP_K))


def _moe(h2_groups, lg, router_bias, w_eg, w_eu, w_ed, tile):
    t = lg.shape[1]
    e = N_EXPERTS
    m = MOE_ROWS
    idx_t, w_t, rank_t, counts = _router(lg, router_bias, tile)
    counts = counts.reshape(e).astype(I32)
    padded = ((counts + m - 1) // m) * m
    pad_end = jnp.cumsum(padded)
    pad_start = pad_end - padded
    onehot = idx_t[:, :, None] == jnp.arange(e, dtype=I32)
    dest_t = jnp.sum(jnp.where(onehot, pad_start, 0), axis=-1) + rank_t
    n_rows = (-(-(t * TOP_K) // m)) * m + e * m
    nblk = n_rows // m
    block_start = jnp.arange(nblk, dtype=I32) * m
    block_e = jnp.minimum(jnp.sum(pad_end[None, :] <= block_start[:, None], axis=1), e - 1).astype(I32)
    block_new = jnp.concatenate([jnp.ones((1,), I32), (block_e[1:] != block_e[:-1]).astype(I32)])
    n_valid = (pad_end[-1] // m).astype(I32).reshape(1)
    of_block = block_e[:, None] == jnp.arange(e, dtype=I32)
    count_b = jnp.sum(jnp.where(of_block, counts, 0), axis=1)
    start_b = jnp.sum(jnp.where(of_block, pad_start, 0), axis=1)
    block_fill = jnp.clip(count_b - (block_start - start_b), 0, m).astype(I32)
    xs = _scatter_rows(jnp.concatenate(h2_groups, axis=0), dest_t, n_rows, SC_SCATTER_WINDOW)
    yb = _experts(block_e, n_valid, block_new, block_fill, xs, w_eg, w_eu, w_ed, m)
    yg = _gather_rows(yb, dest_t[:TOP_K].reshape(-1), SC_GATHER_WINDOW)
    return w_t.T, yg


def kernel(x_prompt, x_sample, c_prompt, c_sample, cache_k, cache_v, cache_logf, state_ssm_re, state_ssm_im, page_table, w_ada, b_ada, g_norm1, w_in, b_fgate, ssm_lambda_re, ssm_lambda_im, ssm_log_step, ssm_b_re, ssm_b_im, ssm_c_re, ssm_c_im, ssm_d, w_glu, b_glu, g_ssm_out, g_attn_out, w_out, g_norm2, w_router, router_bias, w_exp_gate, w_exp_up, w_exp_down, w_sh_gate, w_sh_up, w_sh_down, g_final, w_ada_final, b_ada_final):
    depth = w_ada.shape[0]
    assert depth == 1, "one layer is supported"
    bp, lp, d = x_prompt.shape
    bs, ls, _ = x_sample.shape
    n_pages = page_table.shape[1]
    n_past = n_pages * PAGE
    n_groups = ssm_lambda_re.shape[1]

    n_c = bp + bs
    n_c_pad = -(-n_c // SUBLANES) * SUBLANES
    c_all = jnp.concatenate([c_prompt, c_sample, jnp.zeros((n_c_pad - n_c, d), F32)], axis=0)
    mod = _adaln(c_all, w_ada[0], b_ada[0])
    modf = _adaln(c_all, w_ada_final, b_ada_final)

    def mods(lo, hi):
        parts = [mod[lo:hi, k * d:(k + 1) * d][:, None, :] for k in range(6)]
        parts += [modf[lo:hi, k * d:(k + 1) * d][:, None, :] for k in range(2)]
        return parts

    w_u, w_q, w_k, w_v, w_f = jnp.split(w_in[0], [D_SSM, D_SSM + D_ATTN, D_SSM + 2 * D_ATTN,
                                                   D_SSM + 3 * D_ATTN], axis=1)
    w_fpad = jnp.concatenate([w_f, jnp.zeros((d, LANES - N_HEADS), F32)], axis=1)
    w_main_s = jnp.concatenate([w_u, w_k, w_v, w_fpad], axis=1).astype(BF16)
    w_main_p = jnp.concatenate([w_u, w_k, w_fpad], axis=1).astype(BF16)
    w_qkv = jnp.stack([w_q.T, w_k.T, w_v.T]).astype(BF16)
    b_f = jnp.concatenate([b_fgate[0], jnp.zeros((LANES - N_HEADS,), F32)]).reshape(1, LANES)
    g1 = g_norm1[0].reshape(1, 1, d)
    g2 = g_norm2[0].reshape(1, 1, d)
    gf = g_final.reshape(1, 1, d)
    wb, wc, a_re, a_im = _s5_params(ssm_lambda_re[0], ssm_lambda_im[0], ssm_log_step[0], ssm_b_re[0],
                                    ssm_b_im[0], ssm_c_re[0], ssm_c_im[0])
    dsk = ssm_d[0].reshape(1, D_SSM)
    wglu = w_glu[0].astype(BF16)
    bglu = b_glu[0].reshape(1, D_SSM)
    g_so = g_ssm_out[0].reshape(1, D_SSM)
    g_ao = g_attn_out[0].reshape(1, D_ATTN)
    wo = w_out[0].astype(BF16)
    wr_t = w_router[0].T
    wr_hi = wr_t.astype(BF16)
    wrt = jnp.stack([wr_hi, (wr_t - wr_hi.astype(F32)).astype(BF16)])
    wgu = jnp.concatenate([w_sh_gate[0], w_sh_up[0]], axis=1).astype(BF16)
    wsd = w_sh_down[0].astype(BF16)

    def ssm_state(re, im):
        return jnp.concatenate([re.reshape(-1, 16, LANES), im.reshape(-1, 16, LANES)], axis=1)

    def split_state(ht):
        n = ht.shape[0]
        return (ht[:, :16].reshape(1, n, n_groups, SSM_STATE), ht[:, 16:].reshape(1, n, n_groups, SSM_STATE))

    tm = 512
    sh1, sc1, gt1, sh2, sc2, gt2_p, shf_p, scf_p = mods(0, bp)
    u, lf, kb, qt, vt, ktf, vtf = _inproj(x_prompt, sc1, sh1, g1, w_main_p, w_qkv, b_f, 1, tm, True,
                                          HEAD_DIM ** -0.5 * LOG2E)
    crep = _crep(lf, bp, lp, LOG2E)
    ys, ht = _s5(u.reshape(bp, lp, D_SSM), jnp.zeros((bp, 32, LANES), F32), wb, wc, a_re, a_im, dsk,
                 wglu, bglu, g_so, bp, 256)
    oa = _attn_prompt(qt, kb.reshape(bp, lp, D_ATTN), vt, crep, g_ao)
    x1_p, h2_p, lg_p = _outproj(x_prompt, ys.reshape(bp * lp, D_SSM), oa.reshape(bp * lp, D_ATTN),
                                gt1, sc2, sh2, g2, wo, wrt, 1, tm)
    k_prompt = ktf.reshape(bp, N_HEADS, HEAD_DIM, lp).transpose(0, 3, 1, 2)[None]
    v_prompt = vtf.reshape(bp, N_HEADS, HEAD_DIM, lp).transpose(0, 3, 1, 2)[None]
    logf_prompt = lf[:, :N_HEADS].reshape(1, bp, lp, N_HEADS)
    sre_p, sim_p = split_state(ht)

    nbb = 64
    sh1, sc1, gt1, sh2, sc2, gt2_s, shf_s, scf_s = mods(bp, bp + bs)
    u, lf, k, v, q = _inproj(x_sample, sc1, sh1, g1, w_main_s, w_qkv, b_f, nbb, ls, False, HEAD_DIM ** -0.5)
    ys, ht = _s5(u.reshape(bs, ls, D_SSM), ssm_state(state_ssm_re[0], state_ssm_im[0]), wb, wc, a_re, a_im,
                 dsk, wglu, bglu, g_so, 32, ls)
    lf_past = cache_logf[0][page_table].reshape(bs, n_past, N_HEADS).transpose(0, 2, 1)
    lf_new = lf[:, :N_HEADS].reshape(bs, ls, N_HEADS).transpose(0, 2, 1)
    n_key_pad = -(-(n_past + ls) // LANES) * LANES
    lf_all = jnp.concatenate([lf_past, lf_new, jnp.zeros((bs, N_HEADS, n_key_pad - n_past - ls), F32)], axis=2)
    gsz = n_key_pad // LANES
    cn_s = _neg_cumsum(lf_all.reshape(-1, LANES), gsz, gsz * 64).reshape(bs, N_HEADS, n_key_pad)
    oa = _attn_paged(page_table, q.reshape(bs, ls, D_ATTN), k.reshape(bs, ls, D_ATTN),
                     v.reshape(bs, ls, D_ATTN), cn_s, g_ao,
                     cache_k[0].transpose(0, 2, 3, 1), cache_v[0].transpose(0, 2, 3, 1))
    x1_s, h2_s, lg_s = _outproj(x_sample, ys, oa.reshape(bs * ls, D_ATTN).astype(BF16),
                                gt1, sc2, sh2, g2, wo, wrt, nbb, ls)

    assert nbb * ls == tm
    wts, yg = _moe([h2_p, h2_s], jnp.concatenate([lg_p, lg_s], axis=1), router_bias[0],
                   w_exp_gate[0], w_exp_up[0], w_exp_down[0], tm)
    tiles_p = bp * lp // tm
    y_prompt = _final(x1_p, h2_p, wts[:bp * lp], yg, 0, gt2_p, scf_p, shf_p, gf, wgu, wsd, 1, tm)
    y_sample = _final(x1_s, h2_s, wts[bp * lp:], yg, tiles_p, gt2_s, scf_s, shf_s, gf, wgu, wsd, nbb, ls)
    k_sample = k.reshape(1, bs, ls, N_HEADS, HEAD_DIM)
    v_sample = v.reshape(1, bs, ls, N_HEADS, HEAD_DIM)
    logf_sample = lf[:, :N_HEADS].reshape(1, bs, ls, N_HEADS)
    sre_s, sim_s = split_state(ht)

    return (y_prompt, y_sample, k_prompt, v_prompt, logf_prompt, sre_p, sim_p,
            k_sample, v_sample, logf_sample, sre_s, sim_s)
```

```python
import functools
import math

import jax
import jax.numpy as jnp
import numpy as np
from jax import lax
from jax.experimental import pallas as pl
from jax.experimental.pallas import tpu as pltpu
from jax.experimental.pallas import tpu_sc as plsc

F32 = jnp.float32
BF16 = jnp.bfloat16
I32 = jnp.int32

EPS = 1e-6
HEAD_DIM = 64
N_HEADS = 8
D_SSM = 512
D_ATTN = 512
SSM_GROUP = 16
SSM_STATE = 64
N_EXPERTS = 64
TOP_K = 6
N_EXPERT_GROUPS = 8
TOPK_GROUPS = 4
ROUTED_SCALE = 2.5
PAGE = 128

LANES = 128
SUBLANES = 8
VMEM_LIMIT = 48 * 1024 * 1024
MOE_ROWS = 512
ATTN_TILE = 256
ATTN_GROUP = 4
LOG2E = math.log2(math.e)
SC_SCATTER_WINDOW = 16
SC_GATHER_WINDOW = 48
SC_GATHER_BUFFERS = 4
NEG_INF = float("-inf")


def _cparams(sem):
    return pltpu.CompilerParams(dimension_semantics=sem, vmem_limit_bytes=VMEM_LIMIT)


def _bdot(a, b):
    return jnp.dot(a.astype(BF16), b.astype(BF16), preferred_element_type=F32)


def _bdot_nt(a, b):
    return lax.dot_general(a.astype(BF16), b.astype(BF16), (((1,), (1,)), ((), ())),
                           preferred_element_type=F32)


def _split3(v):
    hi = v.astype(BF16)
    r1 = v - hi.astype(F32)
    mid = r1.astype(BF16)
    lo = (r1 - mid.astype(F32)).astype(BF16)
    return hi, mid, lo


def _pack_bf16_pairs(xb):
    n = xb.shape[1] // 2
    bits = lax.bitcast_convert_type(xb.astype(F32), jnp.uint32)
    return lax.shift_right_logical(bits[:, :n], jnp.uint32(16)) | bits[:, n:]


def _unpack_bf16_pairs(xp):
    lo = lax.bitcast_convert_type(lax.shift_left(xp, jnp.uint32(16)), F32)
    hi = lax.bitcast_convert_type(xp & jnp.uint32(0xFFFF0000), F32)
    return jnp.concatenate([lo, hi], axis=1).astype(BF16)


def _rms(x, g):
    return x * lax.rsqrt(jnp.mean(x * x, axis=-1, keepdims=True) + EPS) * g


def _adaln_kernel(c_ref, w_ref, b_ref, o_ref):
    c = c_ref[...]
    s = c * jax.nn.sigmoid(c)
    o_ref[...] = _bdot(s, w_ref[...]) + b_ref[...]


def _adaln(c, w, b):
    m, k = c.shape
    n = w.shape[1]
    tn = 1024
    return pl.pallas_call(
        _adaln_kernel,
        grid=(n // tn,),
        in_specs=[pl.BlockSpec((m, k), lambda j: (0, 0)),
                  pl.BlockSpec((k, tn), lambda j: (0, j)),
                  pl.BlockSpec((1, tn), lambda j: (0, j))],
        out_specs=pl.BlockSpec((m, tn), lambda j: (0, j)),
        out_shape=jax.ShapeDtypeStruct((m, n), F32),
        compiler_params=_cparams(("arbitrary",)),
    )(c, w, b.reshape(1, n))


def _inproj_kernel(x_ref, sc_ref, sh_ref, g_ref, w_ref, wqv_ref, bf_ref, *out_refs, transposed, q_scale):
    nbb, rb, d = x_ref.shape
    rows = nbb * rb
    x = x_ref[...]
    h = _rms(x, g_ref[...]) * (1.0 + sc_ref[...]) + sh_ref[...]
    hb = h.reshape(rows, d).astype(BF16)
    proj = jnp.dot(hb, w_ref[...], preferred_element_type=F32)
    u_ref, lf_ref = out_refs[:2]
    u_ref[...] = proj[:, :D_SSM]
    k = proj[:, D_SSM:D_SSM + D_ATTN]
    z = proj[:, proj.shape[1] - LANES:] + bf_ref[...]
    lf_ref[...] = jnp.minimum(z, 0.0) - jnp.log1p(jnp.exp(-jnp.abs(z)))
    nt = (((1,), (1,)), ((), ()))
    if transposed:
        kb_ref, qt_ref, vt_ref, ktf_ref, vtf_ref = out_refs[2:]
        kb_ref[...] = k.astype(BF16)
        qt = (lax.dot_general(wqv_ref[0], hb, nt, preferred_element_type=F32) * q_scale).astype(BF16)
        ktf_ref[0] = lax.dot_general(wqv_ref[1], hb, nt, preferred_element_type=F32)
        vt = lax.dot_general(wqv_ref[2], hb, nt, preferred_element_type=F32)
        vtf_ref[0] = vt
        vt = vt.astype(BF16)
        for c in range(rows // ATTN_TILE):
            qt_ref[c] = qt[:, c * ATTN_TILE:(c + 1) * ATTN_TILE]
            vt_ref[c] = vt[:, c * ATTN_TILE:(c + 1) * ATTN_TILE]
    else:
        k_ref, v_ref, q_ref = out_refs[2:]
        k_ref[...] = k
        v_ref[...] = proj[:, D_SSM + D_ATTN:D_SSM + 2 * D_ATTN]
        q_ref[...] = lax.dot_general(hb, wqv_ref[0], nt, preferred_element_type=F32) * q_scale


def _inproj(x3, sc, sh, g, w_main, w_qv, b_f, nbb, rb, transposed, q_scale):
    nb, r, d = x3.shape
    t = nb * r
    rows = nbb * rb
    nj = r // rb
    grid = (nb // nbb, nj)
    row_map = lambda i, j: (i * nj + j, 0)
    n_main = w_main.shape[1]
    outs = [jax.ShapeDtypeStruct((t, D_SSM), F32), jax.ShapeDtypeStruct((t, LANES), F32)]
    out_specs = [pl.BlockSpec((rows, D_SSM), row_map), pl.BlockSpec((rows, LANES), row_map)]
    if transposed:
        assert nbb == 1, "transposed outputs are laid out per sequence"
        nc = rows // ATTN_TILE
        chunk_map = lambda i, j: (i * nj + j, 0, 0)
        seq_map = lambda i, j: (i, 0, j)
        outs += [jax.ShapeDtypeStruct((t, D_ATTN), BF16),
                 jax.ShapeDtypeStruct((t // ATTN_TILE, D_ATTN, ATTN_TILE), BF16),
                 jax.ShapeDtypeStruct((t // ATTN_TILE, D_ATTN, ATTN_TILE), BF16),
                 jax.ShapeDtypeStruct((nb, D_ATTN, r), F32),
                 jax.ShapeDtypeStruct((nb, D_ATTN, r), F32)]
        out_specs += [pl.BlockSpec((rows, D_ATTN), row_map),
                      pl.BlockSpec((nc, D_ATTN, ATTN_TILE), chunk_map),
                      pl.BlockSpec((nc, D_ATTN, ATTN_TILE), chunk_map),
                      pl.BlockSpec((1, D_ATTN, rows), seq_map),
                      pl.BlockSpec((1, D_ATTN, rows), seq_map)]
    else:
        outs += [jax.ShapeDtypeStruct((t, D_ATTN), F32)] * 3
        out_specs += [pl.BlockSpec((rows, D_ATTN), row_map)] * 3
    return pl.pallas_call(
        functools.partial(_inproj_kernel, transposed=transposed, q_scale=q_scale),
        grid=grid,
        in_specs=[pl.BlockSpec((nbb, rb, d), lambda i, j: (i, j, 0)),
                  pl.BlockSpec((nbb, 1, d), lambda i, j: (i, 0, 0)),
                  pl.BlockSpec((nbb, 1, d), lambda i, j: (i, 0, 0)),
                  pl.BlockSpec((1, 1, d), lambda i, j: (0, 0, 0)),
                  pl.BlockSpec((d, n_main), lambda i, j: (0, 0)),
                  pl.BlockSpec((3, D_ATTN, d), lambda i, j: (0, 0, 0)),
                  pl.BlockSpec((1, LANES), lambda i, j: (0, 0))],
        out_specs=out_specs,
        out_shape=outs,
        compiler_params=_cparams(("arbitrary", "arbitrary")),
    )(x3, sc, sh, g, w_main, w_qv, b_f)


def _crep_kernel(lf_ref, tri_ref, ex_ref, o_ref, carry_ref, *, scale):
    j = pl.program_id(1)

    @pl.when(j == 0)
    def _():
        carry_ref[...] = jnp.zeros_like(carry_ref)

    tri = tri_ref[...]
    ex = ex_ref[...]
    cs = sum(jnp.dot(tri, p, preferred_element_type=F32) for p in _split3(lf_ref[...]))
    rep = sum(jnp.dot(p, ex, preferred_element_type=F32) for p in _split3(cs))
    rep = rep + carry_ref[...]
    rows = rep.shape[0]
    carry_ref[...] = rep[rows - 1:rows, :]
    for h in range(N_HEADS):
        o_ref[0, h] = rep[:, h * LANES:(h + 1) * LANES] * (-scale)


def _crep(lf128, nb, length, scale):
    rows = ATTN_TILE
    nj = length // rows
    tri = np.tril(np.ones((rows, rows), np.float32))
    ex = np.zeros((LANES, N_HEADS * LANES), np.float32)
    for h in range(N_HEADS):
        ex[h, h * LANES:(h + 1) * LANES] = 1.0
    return pl.pallas_call(
        functools.partial(_crep_kernel, scale=scale),
        grid=(nb, nj),
        in_specs=[pl.BlockSpec((rows, LANES), lambda i, j: (i * nj + j, 0)),
                  pl.BlockSpec((rows, rows), lambda i, j: (0, 0)),
                  pl.BlockSpec((LANES, N_HEADS * LANES), lambda i, j: (0, 0))],
        out_specs=pl.BlockSpec((1, N_HEADS, rows, LANES), lambda i, j: (i, 0, j, 0)),
        out_shape=jax.ShapeDtypeStruct((nb, N_HEADS, length, LANES), F32),
        scratch_shapes=[pltpu.VMEM((1, N_HEADS * LANES), F32)],
        compiler_params=_cparams(("arbitrary", "arbitrary")),
    )(lf128, jnp.asarray(tri, BF16), jnp.asarray(ex, BF16))


def _cumsum_kernel(x_ref, tri_ref, lt_ref, o_ref):
    tri = tri_ref[...]
    lt = lt_ref[...]
    cs = sum(jnp.dot(p, tri, preferred_element_type=F32) for p in _split3(x_ref[...]))
    tot = jnp.broadcast_to(cs[:, LANES - 1:LANES], cs.shape)
    carry = sum(jnp.dot(lt, p, preferred_element_type=F32) for p in _split3(tot))
    o_ref[...] = -(cs + carry)


def _neg_cumsum(x2, gsz, rb):
    r = x2.shape[0]
    tri = np.triu(np.ones((LANES, LANES), np.float32))
    ii = np.arange(rb)
    lt = ((ii[:, None] // gsz == ii[None, :] // gsz) & (ii[None, :] < ii[:, None])).astype(np.float32)
    return pl.pallas_call(
        _cumsum_kernel,
        grid=(r // rb,),
        in_specs=[pl.BlockSpec((rb, LANES), lambda i: (i, 0)),
                  pl.BlockSpec((LANES, LANES), lambda i: (0, 0)),
                  pl.BlockSpec((rb, rb), lambda i: (0, 0))],
        out_specs=pl.BlockSpec((rb, LANES), lambda i: (i, 0)),
        out_shape=jax.ShapeDtypeStruct((r, LANES), F32),
        compiler_params=_cparams(("arbitrary",)),
    )(x2, jnp.asarray(tri, BF16), jnp.asarray(lt, BF16))


def _gelu_tanh(x):
    return 0.5 * x * (1.0 + jnp.tanh(math.sqrt(2.0 / math.pi) * (x + 0.044715 * (x * x * x))))


def _s5_kernel(u_ref, h0_ref, wb_ref, wc_ref, are_ref, aim_ref, dsk_ref, wglu_ref, bglu_ref, gout_ref,
               y_ref, ht_ref, s_ref, hc_ref, *, nseq, tm):
    rows = nseq * tm
    sr = rows + SUBLANES
    ti = pl.program_id(1)

    @pl.when(ti == 0)
    def _():
        hc_ref[...] = h0_ref[...]

    u = u_ref[...].reshape(rows, D_SSM)
    ub = u.astype(BF16)
    for c in range(4):
        bu = jnp.dot(ub[:, c * LANES:(c + 1) * LANES], wb_ref[c], preferred_element_type=F32)
        for jj in range(4):
            s_ref[pl.ds((4 * c + jj) * sr, rows), :] = bu[:, jj * LANES:(jj + 1) * LANES]
            s_ref[pl.ds((16 + 4 * c + jj) * sr, rows), :] = bu[:, 512 + jj * LANES:512 + (jj + 1) * LANES]

    ar = (are_ref[0:8, :], are_ref[8:16, :])
    ai = (aim_ref[0:8, :], aim_ref[8:16, :])

    def seq_group(sg, carry):
        base = sg * 4
        hs = []
        for b in range(4):
            hs.append(tuple(hc_ref[base + b, pl.ds(8 * q, 8), :] for q in range(4)))

        def step(t, hs):
            new = []
            for b in range(4):
                row = (base + b) * tm + t
                hr0, hr1, hi0, hi1 = hs[b]
                bre0 = s_ref[pl.ds(row, 8, stride=sr), :]
                bre1 = s_ref[pl.ds(8 * sr + row, 8, stride=sr), :]
                bim0 = s_ref[pl.ds(16 * sr + row, 8, stride=sr), :]
                bim1 = s_ref[pl.ds(24 * sr + row, 8, stride=sr), :]
                nr0 = ar[0] * hr0 - ai[0] * hi0 + bre0
                nr1 = ar[1] * hr1 - ai[1] * hi1 + bre1
                ni0 = ar[0] * hi0 + ai[0] * hr0 + bim0
                ni1 = ar[1] * hi1 + ai[1] * hr1 + bim1
                s_ref[pl.ds(row, 8, stride=sr), :] = nr0
                s_ref[pl.ds(8 * sr + row, 8, stride=sr), :] = nr1
                s_ref[pl.ds(16 * sr + row, 8, stride=sr), :] = ni0
                s_ref[pl.ds(24 * sr + row, 8, stride=sr), :] = ni1
                new.append((nr0, nr1, ni0, ni1))
            return tuple(new)

        hs = lax.fori_loop(0, tm, step, tuple(hs), unroll=2)
        for b in range(4):
            for q in range(4):
                hc_ref[base + b, pl.ds(8 * q, 8), :] = hs[b][q]
        return carry

    lax.fori_loop(0, nseq // 4, seq_group, 0)
    ht_ref[...] = hc_ref[...]

    ys = []
    for c in range(4):
        blocks = [s_ref[pl.ds((4 * c + jj) * sr, rows), :].astype(BF16) for jj in range(4)]
        blocks += [s_ref[pl.ds((16 + 4 * c + jj) * sr, rows), :].astype(BF16) for jj in range(4)]
        hcat = jnp.concatenate(blocks, axis=1)
        ys.append(jnp.dot(hcat, wc_ref[c], preferred_element_type=F32))
    y = jnp.concatenate(ys, axis=1) + dsk_ref[...] * u
    y = _gelu_tanh(y)
    gate = jax.nn.sigmoid(jnp.dot(y.astype(BF16), wglu_ref[...], preferred_element_type=F32) + bglu_ref[...])
    y = y * gate
    y_ref[...] = _rms(y, gout_ref[...]).astype(BF16).reshape(y_ref.shape)


def _s5(u3, h0, wb, wc, a_re, a_im, dsk, wglu, bglu, gout, nseq, tm):
    nb, length, _ = u3.shape
    rows = nseq * tm
    sr = rows + SUBLANES
    grid = (nb // nseq, length // tm)
    const2 = lambda i, j: (0, 0)
    const3 = lambda i, j: (0, 0, 0)
    if tm == length:
        y_spec = pl.BlockSpec((rows, D_SSM), lambda i, j: (i, 0))
        y_shape = jax.ShapeDtypeStruct((nb * length, D_SSM), BF16)
    else:
        y_spec = pl.BlockSpec((nseq, tm, D_SSM), lambda i, j: (i, j, 0))
        y_shape = jax.ShapeDtypeStruct((nb, length, D_SSM), BF16)
    return pl.pallas_call(
        functools.partial(_s5_kernel, nseq=nseq, tm=tm),
        grid=grid,
        in_specs=[pl.BlockSpec((nseq, tm, D_SSM), lambda i, j: (i, j, 0)),
                  pl.BlockSpec((nseq, 32, LANES), lambda i, j: (i, 0, 0)),
                  pl.BlockSpec((4, LANES, 1024), const3),
                  pl.BlockSpec((4, 1024, LANES), const3),
                  pl.BlockSpec((16, LANES), const2),
                  pl.BlockSpec((16, LANES), const2),
                  pl.BlockSpec((1, D_SSM), const2),
                  pl.BlockSpec((D_SSM, D_SSM), const2),
                  pl.BlockSpec((1, D_SSM), const2),
                  pl.BlockSpec((1, D_SSM), const2)],
        out_specs=[y_spec, pl.BlockSpec((nseq, 32, LANES), lambda i, j: (i, 0, 0))],
        out_shape=[y_shape, jax.ShapeDtypeStruct((nb, 32, LANES), F32)],
        scratch_shapes=[pltpu.VMEM((32 * sr, LANES), F32), pltpu.VMEM((nseq, 32, LANES), F32)],
        compiler_params=_cparams(("arbitrary", "arbitrary")),
    )(u3, h0, wb, wc, a_re, a_im, dsk, wglu, bglu, gout)


def _s5_params(lam_re, lam_im, log_step, b_re, b_im, c_re, c_im):
    g = lam_re.shape[0]
    dt = jnp.exp(log_step)[:, None]
    mag = jnp.exp(lam_re * dt)
    a_re = mag * jnp.cos(lam_im * dt)
    a_im = mag * jnp.sin(lam_im * dt)
    den = lam_re * lam_re + lam_im * lam_im
    n_re = a_re - 1.0
    f_re = (n_re * lam_re + a_im * lam_im) / den
    f_im = (a_im * lam_re - n_re * lam_im) / den
    bb_re = f_re[..., None] * b_re - f_im[..., None] * b_im
    bb_im = f_re[..., None] * b_im + f_im[..., None] * b_re
    eye = jnp.eye(g, dtype=F32)
    n_state = g * SSM_STATE

    def in_mat(bb):
        return jnp.einsum('gpi,gh->gihp', bb, eye).reshape(g * SSM_GROUP, n_state)

    def out_mat(cc):
        return jnp.einsum('gip,gh->gphi', cc, eye).reshape(n_state, g * SSM_GROUP)

    wbr, wbi = in_mat(bb_re), in_mat(bb_im)
    wcr, wci = out_mat(c_re), out_mat(-c_im)
    wb = jnp.stack([jnp.concatenate([wbr[c * 128:(c + 1) * 128, c * 512:(c + 1) * 512],
                                     wbi[c * 128:(c + 1) * 128, c * 512:(c + 1) * 512]], axis=1)
                    for c in range(4)]).astype(BF16)
    wc = jnp.stack([jnp.concatenate([wcr[c * 512:(c + 1) * 512, c * 128:(c + 1) * 128],
                                     wci[c * 512:(c + 1) * 512, c * 128:(c + 1) * 128]], axis=0)
                    for c in range(4)]).astype(BF16)
    return wb, wc, a_re.reshape(16, LANES), a_im.reshape(16, LANES)


def _attn_kernel(qt_ref, k_ref, vt_ref, cr_ref, g_ref, o_ref, qz_ref, m_ref, l_ref, acc_ref, st_ref):
    tq = ATTN_TILE
    qi = pl.program_id(1)
    half = LANES // 2

    rowid = lax.broadcasted_iota(I32, (LANES, tq), 0)
    for j in range(N_HEADS // 2):
        qp = qt_ref[0, j * LANES:(j + 1) * LANES, :]
        qz_ref[2 * j] = jnp.where(rowid < half, qp, jnp.zeros_like(qp))
        qz_ref[2 * j + 1] = jnp.where(rowid >= half, qp, jnp.zeros_like(qp))
    m_ref[...] = jnp.full_like(m_ref, NEG_INF)
    l_ref[...] = jnp.zeros_like(l_ref)
    acc_ref[...] = jnp.zeros_like(acc_ref)

    key_row = lax.broadcasted_iota(I32, (tq, tq), 0)
    q_col = lax.broadcasted_iota(I32, (tq, tq), 1)
    causal = key_row <= q_col

    def tiles(kbs, masked):
        m_new = [m_ref[h:h + 1, :] for h in range(N_HEADS)]
        for c, kb in enumerate(kbs):
            ks = pl.multiple_of(kb * tq, tq)
            for j in range(N_HEADS // 2):
                kp = k_ref[0, pl.ds(ks, tq), j * LANES:(j + 1) * LANES]
                for e in range(2):
                    h = 2 * j + e
                    st = jnp.dot(kp, qz_ref[h], preferred_element_type=F32)
                    bias = cr_ref[0, h, pl.ds(ks, tq), :]
                    st = st + jnp.concatenate([bias] * (tq // LANES), axis=1)
                    if masked:
                        st = jnp.where(causal, st, NEG_INF)
                    st_ref[c, h] = st
                    m_new[h] = jnp.maximum(m_new[h], jnp.max(st, axis=0, keepdims=True))
        for h in range(N_HEADS):
            alpha = jnp.exp2(m_ref[h:h + 1, :] - m_new[h])
            rows = slice(h * HEAD_DIM, (h + 1) * HEAD_DIM)
            l_new = alpha * l_ref[h:h + 1, :]
            acc = alpha * acc_ref[rows, :]
            for c, kb in enumerate(kbs):
                p = jnp.exp2(st_ref[c, h] - m_new[h])
                l_new = l_new + jnp.sum(p, axis=0, keepdims=True)
                acc = acc + jnp.dot(vt_ref[kb, rows, :], p.astype(BF16), preferred_element_type=F32)
            l_ref[h:h + 1, :] = l_new
            m_ref[h:h + 1, :] = m_new[h]
            acc_ref[rows, :] = acc

    def body(g, c):
        tiles([g * ATTN_GROUP + i for i in range(ATTN_GROUP)], False)
        return c

    n_groups = qi // ATTN_GROUP
    lax.fori_loop(0, n_groups, body, 0)
    for r in range(1, ATTN_GROUP):
        @pl.when(qi - n_groups * ATTN_GROUP == r)
        def _(r=r):
            tiles([n_groups * ATTN_GROUP + i for i in range(r)], False)
    tiles([qi], True)

    for h in range(N_HEADS):
        rows = slice(h * HEAD_DIM, (h + 1) * HEAD_DIM)
        acc_ref[rows, :] = acc_ref[rows, :] / l_ref[h:h + 1, :]
    o = acc_ref[...].T
    o_ref[0] = _rms(o, g_ref[...]).astype(BF16)


def _attn_prompt(qt, k, vt, crep, g):
    b, length, d = k.shape
    tq = ATTN_TILE
    nq = length // tq
    once = pl.Buffered(1)
    return pl.pallas_call(
        _attn_kernel,
        grid=(b, nq),
        in_specs=[pl.BlockSpec((1, d, tq), lambda i, j: (i * nq + j, 0, 0)),
                  pl.BlockSpec((1, length, d), lambda i, j: (i, 0, 0), pipeline_mode=once),
                  pl.BlockSpec((nq, d, tq), lambda i, j: (i, 0, 0), pipeline_mode=once),
                  pl.BlockSpec((1, N_HEADS, length, LANES), lambda i, j: (i, 0, 0, 0), pipeline_mode=once),
                  pl.BlockSpec((1, d), lambda i, j: (0, 0))],
        out_specs=pl.BlockSpec((1, tq, d), lambda i, j: (i, j, 0)),
        out_shape=jax.ShapeDtypeStruct((b, length, d), BF16),
        scratch_shapes=[pltpu.VMEM((N_HEADS, LANES, tq), BF16), pltpu.VMEM((N_HEADS, tq), F32),
                        pltpu.VMEM((N_HEADS, tq), F32), pltpu.VMEM((d, tq), F32),
                        pltpu.VMEM((ATTN_GROUP, N_HEADS, tq, tq), F32)],
        compiler_params=_cparams(("arbitrary", "arbitrary")),
    )(qt, k, vt, crep, g)


def _attn_paged_kernel(pt_ref, q_ref, kn_ref, vn_ref, cn_ref, g_ref, kc_hbm, vc_hbm, o_ref,
                       kbuf, vbuf, sem, *, n_pages, n_new):
    i = pl.program_id(0)
    n = pl.num_programs(0)
    n_past = n_pages * PAGE

    def page_copies(seq, slot):
        cps = []
        for p in range(n_pages):
            pg = pt_ref[seq * n_pages + p]
            cps.append(pltpu.make_async_copy(kc_hbm.at[pg], kbuf.at[slot, p], sem.at[0, slot]))
            cps.append(pltpu.make_async_copy(vc_hbm.at[pg], vbuf.at[slot, p], sem.at[1, slot]))
        return cps

    @pl.when(i == 0)
    def _():
        for cp in page_copies(0, 0):
            cp.start()

    @pl.when(i + 1 < n)
    def _():
        for cp in page_copies(i + 1, (i + 1) % 2):
            cp.start()

    slot = i % 2
    for cp in page_copies(i, slot):
        cp.wait()

    d = q_ref.shape[-1]
    nr = N_HEADS * n_new
    new_bits = n_new.bit_length() - 1
    head_bits = HEAD_DIM.bit_length() - 1
    rowh = lax.shift_right_logical(lax.broadcasted_iota(I32, (nr, d), 0), new_bits)
    colh = lax.shift_right_logical(lax.broadcasted_iota(I32, (nr, d), 1), head_bits)
    bd = rowh == colh
    qrep = jnp.broadcast_to(q_ref[0][None], (N_HEADS, n_new, d)).reshape(nr, d)
    qbd = jnp.where(bd, qrep, 0.0).astype(BF16)
    cn = cn_ref[0]
    cnr = jnp.broadcast_to(cn[:, None, :], (N_HEADS, n_new, cn.shape[-1])).reshape(nr, cn.shape[-1])

    s_p = []
    for p in range(n_pages):
        kt = kbuf[slot, p].reshape(d, PAGE)
        s_p.append(_bdot(qbd, kt) + cnr[:, p * PAGE:(p + 1) * PAGE])
    s_n = _bdot_nt(qbd, kn_ref[0]) + cnr[:, n_past:n_past + n_new]
    qpos = lax.broadcasted_iota(I32, (nr, n_new), 0) & (n_new - 1)
    kpos = lax.broadcasted_iota(I32, (nr, n_new), 1)
    s_n = jnp.where(kpos <= qpos, s_n, NEG_INF)
    m = jnp.max(s_n, axis=1, keepdims=True)
    for sp in s_p:
        m = jnp.maximum(m, jnp.max(sp, axis=1, keepdims=True))
    p_n = jnp.exp(s_n - m)
    l = jnp.sum(p_n, axis=1, keepdims=True)
    of = _bdot(p_n, vn_ref[0])
    for p in range(n_pages):
        pp = jnp.exp(s_p[p] - m)
        l = l + jnp.sum(pp, axis=1, keepdims=True)
        of = of + _bdot_nt(pp, vbuf[slot, p].reshape(d, PAGE))
    of = jnp.where(bd, of / l, 0.0)
    o = jnp.sum(of.reshape(N_HEADS, n_new, d), axis=0)
    o_ref[0] = _rms(o, g_ref[...])


def _attn_paged(page_table, q, kn, vn, cn, g, cache_k, cache_v):
    nseq, n_new, d = q.shape
    assert n_new & (n_new - 1) == 0, "new-token count must be a power of two"
    n_pages = page_table.shape[1]
    n_past = n_pages * PAGE
    grid_spec = pltpu.PrefetchScalarGridSpec(
        num_scalar_prefetch=1,
        grid=(nseq,),
        in_specs=[pl.BlockSpec((1, n_new, d), lambda i, pt: (i, 0, 0)),
                  pl.BlockSpec((1, n_new, d), lambda i, pt: (i, 0, 0)),
                  pl.BlockSpec((1, n_new, d), lambda i, pt: (i, 0, 0)),
                  pl.BlockSpec((1, N_HEADS, cn.shape[-1]), lambda i, pt: (i, 0, 0)),
                  pl.BlockSpec((1, d), lambda i, pt: (0, 0)),
                  pl.BlockSpec(memory_space=pl.ANY),
                  pl.BlockSpec(memory_space=pl.ANY)],
        out_specs=pl.BlockSpec((1, n_new, d), lambda i, pt: (i, 0, 0)),
        scratch_shapes=[pltpu.VMEM((2, n_pages, N_HEADS, HEAD_DIM, PAGE), F32),
                        pltpu.VMEM((2, n_pages, N_HEADS, HEAD_DIM, PAGE), F32),
                        pltpu.SemaphoreType.DMA((2, 2))],
    )
    return pl.pallas_call(
        functools.partial(_attn_paged_kernel, n_pages=n_pages, n_new=n_new),
        grid_spec=grid_spec,
        out_shape=jax.ShapeDtypeStruct((nseq, n_new, d), F32),
        compiler_params=_cparams(("arbitrary",)),
    )(page_table.reshape(-1), q, kn, vn, cn, g, cache_k, cache_v)


def _outproj_kernel(x_ref, ys_ref, oa_ref, gt_ref, sc_ref, sh_ref, g_ref, wo_ref, wrt_ref,
                    x1_ref, h2_ref, lg_ref):
    nbb, rb, d = x_ref.shape
    mix = jnp.dot(ys_ref[...], wo_ref[0:D_SSM, :], preferred_element_type=F32)
    mix = mix + jnp.dot(oa_ref[...], wo_ref[D_SSM:, :], preferred_element_type=F32)
    x1 = x_ref[...] + gt_ref[...] * mix.reshape(nbb, rb, d)
    x1_ref[...] = x1
    h2 = (_rms(x1, g_ref[...]) * (1.0 + sc_ref[...]) + sh_ref[...]).reshape(nbb * rb, d)
    hb = h2.astype(BF16)
    h2_ref[...] = _pack_bf16_pairs(hb)
    hlo = (h2 - hb.astype(F32)).astype(BF16)
    whi = wrt_ref[0]
    wlo = wrt_ref[1]
    nt = (((1,), (1,)), ((), ()))
    lg = lax.dot_general(whi, hb, nt, preferred_element_type=F32)
    lg = lg + lax.dot_general(wlo, hb, nt, preferred_element_type=F32)
    lg = lg + lax.dot_general(whi, hlo, nt, preferred_element_type=F32)
    lg_ref[...] = lg


def _outproj(x3, ys, oa, gt, sc, sh, g, wo, wrt, nbb, rb):
    nb, r, d = x3.shape
    t = nb * r
    rows = nbb * rb
    nj = r // rb
    row_map = lambda i, j: (i * nj + j, 0)
    mod_spec = pl.BlockSpec((nbb, 1, d), lambda i, j: (i, 0, 0))
    return pl.pallas_call(
        _outproj_kernel,
        grid=(nb // nbb, nj),
        in_specs=[pl.BlockSpec((nbb, rb, d), lambda i, j: (i, j, 0)),
                  pl.BlockSpec((rows, D_SSM), row_map),
                  pl.BlockSpec((rows, D_ATTN), row_map),
                  mod_spec, mod_spec, mod_spec,
                  pl.BlockSpec((1, 1, d), lambda i, j: (0, 0, 0)),
                  pl.BlockSpec((D_SSM + D_ATTN, d), lambda i, j: (0, 0)),
                  pl.BlockSpec((2, N_EXPERTS, d), lambda i, j: (0, 0, 0))],
        out_specs=[pl.BlockSpec((nbb, rb, d), lambda i, j: (i, j, 0)),
                   pl.BlockSpec((rows, d // 2), row_map),
                   pl.BlockSpec((N_EXPERTS, rows), lambda i, j: (0, i * nj + j))],
        out_shape=[jax.ShapeDtypeStruct((nb, r, d), F32),
                   jax.ShapeDtypeStruct((t, d // 2), jnp.uint32),
                   jax.ShapeDtypeStruct((N_EXPERTS, t), F32)],
        compiler_params=_cparams(("arbitrary", "arbitrary")),
    )(x3, ys, oa, gt, sc, sh, g, wo, wrt)


def _router_kernel(lg_ref, rb_ref, ut_ref, idx_ref, w_ref, rank_ref, cnt_ref, carry_ref):
    i = pl.program_id(0)
    tm = lg_ref.shape[1]
    per_group = N_EXPERTS // N_EXPERT_GROUPS

    @pl.when(i == 0)
    def _():
        carry_ref[...] = jnp.zeros_like(carry_ref)

    scores = jax.nn.sigmoid(lg_ref[...])
    biased = scores + rb_ref[...]
    blks, grp = [], []
    for g in range(N_EXPERT_GROUPS):
        blk = biased[g * per_group:(g + 1) * per_group, :]
        m1 = jnp.max(blk, axis=0, keepdims=True)
        eq = blk == m1
        n_eq = jnp.sum(jnp.where(eq, 1.0, 0.0), axis=0, keepdims=True)
        m2 = jnp.max(jnp.where(eq, NEG_INF, blk), axis=0, keepdims=True)
        blks.append(blk)
        grp.append(m1 + jnp.where(n_eq >= 2.0, m1, m2))
    masked = []
    for g in range(N_EXPERT_GROUPS):
        beaten = jnp.zeros((1, tm), F32)
        for o in range(N_EXPERT_GROUPS):
            if o == g:
                continue
            ahead = (grp[o] >= grp[g]) if o < g else (grp[o] > grp[g])
            beaten = beaten + jnp.where(ahead, 1.0, 0.0)
        masked.append(jnp.where(beaten < float(TOPK_GROUPS), blks[g], NEG_INF))
    work = jnp.concatenate(masked, axis=0)

    eid = lax.broadcasted_iota(I32, (N_EXPERTS, tm), 0)
    chosen = jnp.zeros((N_EXPERTS, tm), F32)
    idxs, ws, sels = [], [], []
    for _ in range(TOP_K):
        m = jnp.max(work, axis=0, keepdims=True)
        first = jnp.min(jnp.where(work == m, eid, N_EXPERTS), axis=0, keepdims=True)
        sel = eid == first
        idxs.append(first)
        ws.append(jnp.sum(jnp.where(sel, scores, 0.0), axis=0, keepdims=True))
        sels.append(sel)
        chosen = jnp.where(sel, 1.0, chosen)
        work = jnp.where(sel, NEG_INF, work)
    wsum = ws[0]
    for wk in ws[1:]:
        wsum = wsum + wk

    prefix = jnp.dot(chosen.astype(BF16), ut_ref[...], preferred_element_type=F32) + carry_ref[...]
    carry_ref[...] = carry_ref[...] + jnp.sum(chosen, axis=1, keepdims=True)
    cnt_ref[...] = carry_ref[...]

    idx_ref[...] = jnp.zeros_like(idx_ref)
    w_ref[...] = jnp.zeros_like(w_ref)
    rank_ref[...] = jnp.zeros_like(rank_ref)
    for k in range(TOP_K):
        idx_ref[k:k + 1, :] = idxs[k]
        w_ref[k:k + 1, :] = ws[k] / wsum * ROUTED_SCALE
        rank = jnp.sum(jnp.where(sels[k], prefix, 0.0), axis=0, keepdims=True)
        rank_ref[k:k + 1, :] = rank.astype(I32)


def _router(lg, router_bias, tm):
    e, t = lg.shape
    ut = np.triu(np.ones((tm, tm), np.float32), 1)
    tok_spec = pl.BlockSpec((SUBLANES, tm), lambda i: (0, i))
    return pl.pallas_call(
        _router_kernel,
        grid=(t // tm,),
        in_specs=[pl.BlockSpec((e, tm), lambda i: (0, i)),
                  pl.BlockSpec((e, 1), lambda i: (0, 0)),
                  pl.BlockSpec((tm, tm), lambda i: (0, 0))],
        out_specs=[tok_spec, tok_spec, tok_spec, pl.BlockSpec((e, 1), lambda i: (0, 0))],
        out_shape=[jax.ShapeDtypeStruct((SUBLANES, t), I32), jax.ShapeDtypeStruct((SUBLANES, t), F32),
                   jax.ShapeDtypeStruct((SUBLANES, t), I32), jax.ShapeDtypeStruct((e, 1), F32)],
        scratch_shapes=[pltpu.VMEM((e, 1), F32)],
        compiler_params=_cparams(("arbitrary",)),
    )(lg, router_bias.reshape(e, 1), jnp.asarray(ut, BF16))


def _experts_kernel(be_ref, nv_ref, new_ref, fill_ref, x_ref, wg_ref, wu_ref, wd_ref, o_ref, wgu_s, wd_s):
    i = pl.program_id(0)
    de = wg_ref.shape[-1]
    valid = i < nv_ref[0]

    @pl.when(valid & (new_ref[i] == 1))
    def _():
        wgu_s[:, :de] = wg_ref[0].astype(BF16)
        wgu_s[:, de:] = wu_ref[0].astype(BF16)
        wd_s[...] = wd_ref[0].astype(BF16)

    @pl.when(valid)
    def _():
        row = lax.broadcasted_iota(I32, x_ref.shape, 0)
        x = jnp.where(row < fill_ref[i], x_ref[...], jnp.uint32(0))
        gu = jnp.dot(_unpack_bf16_pairs(x), wgu_s[...], preferred_element_type=F32)
        g = gu[:, :de]
        a = (g * jax.nn.sigmoid(g) * gu[:, de:]).astype(BF16)
        o = jnp.dot(a, wd_s[...], preferred_element_type=F32)
        o_ref[...] = _pack_bf16_pairs(o.astype(BF16))

    @pl.when(jnp.logical_not(valid))
    def _():
        o_ref[...] = jnp.zeros_like(o_ref)


def _experts(block_e, n_valid, block_new, block_fill, xs, w_eg, w_eu, w_ed, m):
    n_rows = xs.shape[0]
    d, de = w_eg.shape[-2:]
    nblk = n_rows // m
    xmap = lambda i, be, nv, *_: (jnp.minimum(i, nv[0] - 1), 0)
    wmap = lambda i, be, *_: (be[i], 0, 0)
    grid_spec = pltpu.PrefetchScalarGridSpec(
        num_scalar_prefetch=4,
        grid=(nblk,),
        in_specs=[pl.BlockSpec((m, xs.shape[1]), xmap),
                  pl.BlockSpec((1, d, de), wmap),
                  pl.BlockSpec((1, d, de), wmap),
                  pl.BlockSpec((1, de, d), wmap)],
        out_specs=pl.BlockSpec((m, d // 2), lambda i, *_: (i, 0)),
        scratch_shapes=[pltpu.VMEM((d, 2 * de), BF16), pltpu.VMEM((de, d), BF16)],
    )
    return pl.pallas_call(
        _experts_kernel,
        grid_spec=grid_spec,
        out_shape=jax.ShapeDtypeStruct((n_rows, d // 2), jnp.uint32),
        compiler_params=_cparams(("arbitrary",)),
    )(block_e, n_valid, block_new, block_fill, xs, w_eg, w_eu, w_ed)


def _gather_rows(table, idx, window):
    n = idx.shape[0]
    d = table.shape[1]
    info = plsc.get_sparse_core_info()
    n_workers = info.num_cores * info.num_subcores
    per_worker = n // n_workers
    n_buf = SC_GATHER_BUFFERS
    n_rounds = per_worker // (window * n_buf)
    assert n_rounds * window * n_buf * n_workers == n
    mesh = plsc.VectorSubcoreMesh(core_axis_name="c", subcore_axis_name="s")

    @functools.partial(
        pl.kernel, out_type=jax.ShapeDtypeStruct((n, d), table.dtype), mesh=mesh,
        scratch_types=[pltpu.VMEM((per_worker,), I32), pltpu.VMEM((n_buf, window, d), table.dtype),
                       pltpu.SemaphoreType.DMA((n_buf,)), pltpu.SemaphoreType.DMA((n_buf,))])
    def gather(x_hbm, i_hbm, o_hbm, i_v, buf, gsem, wsem):
        base = (lax.axis_index("s") * info.num_cores + lax.axis_index("c")) * per_worker
        pltpu.sync_copy(i_hbm.at[pl.ds(base, per_worker)], i_v)

        @pl.loop(0, n_rounds)
        def _(rnd):
            first = rnd * (window * n_buf)
            reads, writes = [], []
            for b in range(n_buf):
                rows = pl.ds(first + b * window, window)
                reads.append(pltpu.make_async_copy(x_hbm.at[i_v.at[rows]], buf.at[b], gsem.at[b]))
                writes.append(pltpu.make_async_copy(buf.at[b], o_hbm.at[pl.ds(base + first + b * window, window)],
                                                    wsem.at[b]))
            for cp in reads:
                cp.start()
            for b in range(n_buf):
                reads[b].wait()
                writes[b].start()
            for cp in writes:
                cp.wait()

    return gather(table, idx)


def _scatter_rows(x, dest_t, n_rows, window):
    t, d = x.shape
    info = plsc.get_sparse_core_info()
    n_workers = info.num_cores * info.num_subcores
    per_worker = t // n_workers
    chunks = per_worker // window
    n_buf = 2
    n_rounds = chunks // n_buf
    assert n_rounds * n_buf * window * n_workers == t
    idx_rows = -(-(TOP_K * chunks) // SUBLANES) * SUBLANES
    idx = dest_t[:TOP_K].reshape(TOP_K, n_workers, chunks, window).transpose(1, 0, 2, 3)
    idx = idx.reshape(n_workers, TOP_K * chunks, window)
    idx = jnp.pad(idx, ((0, 0), (0, idx_rows - TOP_K * chunks), (0, 0))).reshape(n_workers * idx_rows, window)
    mesh = plsc.VectorSubcoreMesh(core_axis_name="c", subcore_axis_name="s")

    @functools.partial(
        pl.kernel, out_type=jax.ShapeDtypeStruct((n_rows, d), x.dtype), mesh=mesh,
        scratch_types=[pltpu.VMEM((idx_rows, window), I32), pltpu.VMEM((n_buf, window, d), x.dtype),
                       pltpu.SemaphoreType.DMA((n_buf,)), pltpu.SemaphoreType.DMA((n_buf,))])
    def scatter(x_hbm, i_hbm, o_hbm, i_v, buf, rsem, wsem):
        wid = lax.axis_index("s") * info.num_cores + lax.axis_index("c")
        base = wid * per_worker
        pltpu.sync_copy(i_hbm.at[pl.ds(wid * idx_rows, idx_rows)], i_v)

        @pl.loop(0, n_rounds)
        def _(rnd):
            reads = [pltpu.make_async_copy(x_hbm.at[pl.ds(base + (rnd * n_buf + b) * window, window)],
                                           buf.at[b], rsem.at[b]) for b in range(n_buf)]
            for cp in reads:
                cp.start()
            writes = []
            for b in range(n_buf):
                reads[b].wait()
                for k in range(TOP_K):
                    cp = pltpu.make_async_copy(buf.at[b], o_hbm.at[i_v.at[k * chunks + rnd * n_buf + b]],
                                               wsem.at[b])
                    cp.start()
                    writes.append(cp)
            for cp in writes:
                cp.wait()

    return scatter(x, idx)


def _final_kernel(x1_ref, h2_ref, w_ref, gt_ref, sc_ref, sh_ref, g_ref, wgu_ref, wd_ref, *rest):
    yg_refs = rest[:TOP_K]
    y_ref = rest[TOP_K]
    nbb, rb, d = x1_ref.shape
    ds = wd_ref.shape[0]
    gu = jnp.dot(_unpack_bf16_pairs(h2_ref[...]), wgu_ref[...], preferred_element_type=F32)
    g = gu[:, :ds]
    a = (g * jax.nn.sigmoid(g) * gu[:, ds:]).astype(BF16)
    ff = jnp.dot(a, wd_ref[...], preferred_element_type=F32)
    routed_lo = routed_hi = None
    for k in range(TOP_K):
        yk = yg_refs[k][...]
        w = w_ref[:, k:k + 1]
        lo = lax.bitcast_convert_type(lax.shift_left(yk, jnp.uint32(16)), F32) * w
        hi = lax.bitcast_convert_type(yk & jnp.uint32(0xFFFF0000), F32) * w
        routed_lo = lo if k == 0 else routed_lo + lo
        routed_hi = hi if k == 0 else routed_hi + hi
    ff = jnp.concatenate([routed_lo, routed_hi], axis=1) + ff
    x2 = x1_ref[...] + gt_ref[...] * ff.reshape(nbb, rb, d)
    y_ref[...] = _rms(x2, g_ref[...]) * (1.0 + sc_ref[...]) + sh_ref[...]


def _final(x1, h2, wts, yg, first_tile, gt, sc, sh, g, wgu, wd, nbb, rb):
    nb, r, d = x1.shape
    rows = nbb * rb
    nj = r // rb
    tiles_all = yg.shape[0] // TOP_K // rows
    row_map = lambda i, j: (i * nj + j, 0)
    mod_spec = pl.BlockSpec((nbb, 1, d), lambda i, j: (i, 0, 0))
    slab_specs = [pl.BlockSpec((rows, yg.shape[1]), lambda i, j, k=k: (k * tiles_all + first_tile + i * nj + j, 0))
                  for k in range(TOP_K)]
    return pl.pallas_call(
        _final_kernel,
        grid=(nb // nbb, nj),
        in_specs=[pl.BlockSpec((nbb, rb, d), lambda i, j: (i, j, 0)),
                  pl.BlockSpec((rows, h2.shape[1]), row_map),
                  pl.BlockSpec((rows, SUBLANES), row_map),
                  mod_spec, mod_spec, mod_spec,
                  pl.BlockSpec((1, 1, d), lambda i, j: (0, 0, 0)),
                  pl.BlockSpec(wgu.shape, lambda i, j: (0, 0)),
                  pl.BlockSpec(wd.shape, lambda i, j: (0, 0))] + slab_specs,
        out_specs=pl.BlockSpec((nbb, rb, d), lambda i, j: (i, j, 0)),
        out_shape=jax.ShapeDtypeStruct((nb, r, d), F32),
        compiler_params=_cparams(("arbitrary", "arbitrary")),
    )(x1, h2, wts, gt, sc, sh, g, wgu, wd, *([yg] * TOP_K))


def _moe(h2_groups, lg, router_bias, w_eg, w_eu, w_ed, tile):
    t = lg.shape[1]
    e = N_EXPERTS
    m = MOE_ROWS
    idx_t, w_t, rank_t, counts = _router(lg, router_bias, tile)
    counts = counts.reshape(e).astype(I32)
    padded = ((counts + m - 1) // m) * m
    pad_end = jnp.cumsum(padded)
    pad_start = pad_end - padded
    onehot = idx_t[:, :, None] == jnp.arange(e, dtype=I32)
    dest_t = jnp.sum(jnp.where(onehot, pad_start, 0), axis=-1) + rank_t
    n_rows = (-(-(t * TOP_K) // m)) * m + e * m
    nblk = n_rows // m
    block_start = jnp.arange(nblk, dtype=I32) * m
    block_e = jnp.minimum(jnp.sum(pad_end[None, :] <= block_start[:, None], axis=1), e - 1).astype(I32)
    block_new = jnp.concatenate([jnp.ones((1,), I32), (block_e[1:] != block_e[:-1]).astype(I32)])
    n_valid = (pad_end[-1] // m).astype(I32).reshape(1)
    of_block = block_e[:, None] == jnp.arange(e, dtype=I32)
    count_b = jnp.sum(jnp.where(of_block, counts, 0), axis=1)
    start_b = jnp.sum(jnp.where(of_block, pad_start, 0), axis=1)
    block_fill = jnp.clip(count_b - (block_start - start_b), 0, m).astype(I32)
    xs = _scatter_rows(jnp.concatenate(h2_groups, axis=0), dest_t, n_rows, SC_SCATTER_WINDOW)
    yb = _experts(block_e, n_valid, block_new, block_fill, xs, w_eg, w_eu, w_ed, m)
    yg = _gather_rows(yb, dest_t[:TOP_K].reshape(-1), SC_GATHER_WINDOW)
    return w_t.T, yg


def kernel(x_prompt, x_sample, c_prompt, c_sample, cache_k, cache_v, cache_logf, state_ssm_re, state_ssm_im, page_table, w_ada, b_ada, g_norm1, w_in, b_fgate, ssm_lambda_re, ssm_lambda_im, ssm_log_step, ssm_b_re, ssm_b_im, ssm_c_re, ssm_c_im, ssm_d, w_glu, b_glu, g_ssm_out, g_attn_out, w_out, g_norm2, w_router, router_bias, w_exp_gate, w_exp_up, w_exp_down, w_sh_gate, w_sh_up, w_sh_down, g_final, w_ada_final, b_ada_final):
    depth = w_ada.shape[0]
    assert depth == 1, "one layer is supported"
    bp, lp, d = x_prompt.shape
    bs, ls, _ = x_sample.shape
    n_pages = page_table.shape[1]
    n_past = n_pages * PAGE
    n_groups = ssm_lambda_re.shape[1]

    n_c = bp + bs
    n_c_pad = -(-n_c // SUBLANES) * SUBLANES
    c_all = jnp.concatenate([c_prompt, c_sample, jnp.zeros((n_c_pad - n_c, d), F32)], axis=0)
    mod = _adaln(c_all, w_ada[0], b_ada[0])
    modf = _adaln(c_all, w_ada_final, b_ada_final)

    def mods(lo, hi):
        parts = [mod[lo:hi, k * d:(k + 1) * d][:, None, :] for k in range(6)]
        parts += [modf[lo:hi, k * d:(k + 1) * d][:, None, :] for k in range(2)]
        return parts

    w_u, w_q, w_k, w_v, w_f = jnp.split(w_in[0], [D_SSM, D_SSM + D_ATTN, D_SSM + 2 * D_ATTN,
                                                   D_SSM + 3 * D_ATTN], axis=1)
    w_fpad = jnp.concatenate([w_f, jnp.zeros((d, LANES - N_HEADS), F32)], axis=1)
    w_main_s = jnp.concatenate([w_u, w_k, w_v, w_fpad], axis=1).astype(BF16)
    w_main_p = jnp.concatenate([w_u, w_k, w_fpad], axis=1).astype(BF16)
    w_qkv = jnp.stack([w_q.T, w_k.T, w_v.T]).astype(BF16)
    b_f = jnp.concatenate([b_fgate[0], jnp.zeros((LANES - N_HEADS,), F32)]).reshape(1, LANES)
    g1 = g_norm1[0].reshape(1, 1, d)
    g2 = g_norm2[0].reshape(1, 1, d)
    gf = g_final.reshape(1, 1, d)
    wb, wc, a_re, a_im = _s5_params(ssm_lambda_re[0], ssm_lambda_im[0], ssm_log_step[0], ssm_b_re[0],
                                    ssm_b_im[0], ssm_c_re[0], ssm_c_im[0])
    dsk = ssm_d[0].reshape(1, D_SSM)
    wglu = w_glu[0].astype(BF16)
    bglu = b_glu[0].reshape(1, D_SSM)
    g_so = g_ssm_out[0].reshape(1, D_SSM)
    g_ao = g_attn_out[0].reshape(1, D_ATTN)
    wo = w_out[0].astype(BF16)
    wr_t = w_router[0].T
    wr_hi = wr_t.astype(BF16)
    wrt = jnp.stack([wr_hi, (wr_t - wr_hi.astype(F32)).astype(BF16)])
    wgu = jnp.concatenate([w_sh_gate[0], w_sh_up[0]], axis=1).astype(BF16)
    wsd = w_sh_down[0].astype(BF16)

    def ssm_state(re, im):
        return jnp.concatenate([re.reshape(-1, 16, LANES), im.reshape(-1, 16, LANES)], axis=1)

    def split_state(ht):
        n = ht.shape[0]
        return (ht[:, :16].reshape(1, n, n_groups, SSM_STATE), ht[:, 16:].reshape(1, n, n_groups, SSM_STATE))

    tm = 512
    sh1, sc1, gt1, sh2, sc2, gt2_p, shf_p, scf_p = mods(0, bp)
    u, lf, kb, qt, vt, ktf, vtf = _inproj(x_prompt, sc1, sh1, g1, w_main_p, w_qkv, b_f, 1, tm, True,
                                          HEAD_DIM ** -0.5 * LOG2E)
    crep = _crep(lf, bp, lp, LOG2E)
    ys, ht = _s5(u.reshape(bp, lp, D_SSM), jnp.zeros((bp, 32, LANES), F32), wb, wc, a_re, a_im, dsk,
                 wglu, bglu, g_so, bp, 256)
    oa = _attn_prompt(qt, kb.reshape(bp, lp, D_ATTN), vt, crep, g_ao)
    x1_p, h2_p, lg_p = _outproj(x_prompt, ys.reshape(bp * lp, D_SSM), oa.reshape(bp * lp, D_ATTN),
                                gt1, sc2, sh2, g2, wo, wrt, 1, tm)
    k_prompt = ktf.reshape(bp, N_HEADS, HEAD_DIM, lp).transpose(0, 3, 1, 2)[None]
    v_prompt = vtf.reshape(bp, N_HEADS, HEAD_DIM, lp).transpose(0, 3, 1, 2)[None]
    logf_prompt = lf[:, :N_HEADS].reshape(1, bp, lp, N_HEADS)
    sre_p, sim_p = split_state(ht)

    nbb = 64
    sh1, sc1, gt1, sh2, sc2, gt2_s, shf_s, scf_s = mods(bp, bp + bs)
    u, lf, k, v, q = _inproj(x_sample, sc1, sh1, g1, w_main_s, w_qkv, b_f, nbb, ls, False, HEAD_DIM ** -0.5)
    ys, ht = _s5(u.reshape(bs, ls, D_SSM), ssm_state(state_ssm_re[0], state_ssm_im[0]), wb, wc, a_re, a_im,
                 dsk, wglu, bglu, g_so, 32, ls)
    lf_past = cache_logf[0][page_table].reshape(bs, n_past, N_HEADS).transpose(0, 2, 1)
    lf_new = lf[:, :N_HEADS].reshape(bs, ls, N_HEADS).transpose(0, 2, 1)
    n_key_pad = -(-(n_past + ls) // LANES) * LANES
    lf_all = jnp.concatenate([lf_past, lf_new, jnp.zeros((bs, N_HEADS, n_key_pad - n_past - ls), F32)], axis=2)
    gsz = n_key_pad // LANES
    cn_s = _neg_cumsum(lf_all.reshape(-1, LANES), gsz, gsz * 64).reshape(bs, N_HEADS, n_key_pad)
    oa = _attn_paged(page_table, q.reshape(bs, ls, D_ATTN), k.reshape(bs, ls, D_ATTN),
                     v.reshape(bs, ls, D_ATTN), cn_s, g_ao,
                     cache_k[0].transpose(0, 2, 3, 1), cache_v[0].transpose(0, 2, 3, 1))
    x1_s, h2_s, lg_s = _outproj(x_sample, ys, oa.reshape(bs * ls, D_ATTN).astype(BF16),
                                gt1, sc2, sh2, g2, wo, wrt, nbb, ls)

    assert nbb * ls == tm
    wts, yg = _moe([h2_p, h2_s], jnp.concatenate([lg_p, lg_s], axis=1), router_bias[0],
                   w_exp_gate[0], w_exp_up[0], w_exp_down[0], tm)
    tiles_p = bp * lp // tm
    y_prompt = _final(x1_p, h2_p, wts[:bp * lp], yg, 0, gt2_p, scf_p, shf_p, gf, wgu, wsd, 1, tm)
    y_sample = _final(x1_s, h2_s, wts[bp * lp:], yg, tiles_p, gt2_s, scf_s, shf_s, gf, wgu, wsd, nbb, ls)
    k_sample = k.reshape(1, bs, ls, N_HEADS, HEAD_DIM)
    v_sample = v.reshape(1, bs, ls, N_HEADS, HEAD_DIM)
    logf_sample = lf[:, :N_HEADS].reshape(1, bs, ls, N_HEADS)
    sre_s, sim_s = split_state(ht)

    return (y_prompt, y_sample, k_prompt, v_prompt, logf_prompt, sre_p, sim_p,
            k_sample, v_sample, logf_sample, sre_s, sim_s)
```

```python
import functools
import math

import jax
import jax.numpy as jnp
import numpy as np
from jax import lax
from jax.experimental import pallas as pl
from jax.experimental.pallas import tpu as pltpu
from jax.experimental.pallas import tpu_sc as plsc

F32 = jnp.float32
BF16 = jnp.bfloat16
I32 = jnp.int32

EPS = 1e-6
HEAD_DIM = 64
N_HEADS = 8
D_SSM = 512
D_ATTN = 512
SSM_GROUP = 16
SSM_STATE = 64
N_EXPERTS = 64
TOP_K = 6
N_EXPERT_GROUPS = 8
TOPK_GROUPS = 4
ROUTED_SCALE = 2.5
PAGE = 128

LANES = 128
SUBLANES = 8
VMEM_LIMIT = 48 * 1024 * 1024
ATTN_VMEM_LIMIT = 58 * 1024 * 1024
MOE_ROWS = 512
ATTN_TILE = 256
ATTN_GROUP = 4
LOG2E = math.log2(math.e)
SC_SCATTER_WINDOW = 16
SC_GATHER_WINDOW = 48
SC_GATHER_BUFFERS = 4
NEG_INF = float("-inf")


def _cparams(sem):
    return pltpu.CompilerParams(dimension_semantics=sem, vmem_limit_bytes=VMEM_LIMIT)


def _bdot(a, b):
    return jnp.dot(a.astype(BF16), b.astype(BF16), preferred_element_type=F32)


def _bdot_nt(a, b):
    return lax.dot_general(a.astype(BF16), b.astype(BF16), (((1,), (1,)), ((), ())),
                           preferred_element_type=F32)


def _split3(v):
    hi = v.astype(BF16)
    r1 = v - hi.astype(F32)
    mid = r1.astype(BF16)
    lo = (r1 - mid.astype(F32)).astype(BF16)
    return hi, mid, lo


def _pack_bf16_pairs(xb):
    n = xb.shape[1] // 2
    bits = lax.bitcast_convert_type(xb.astype(F32), jnp.uint32)
    return lax.shift_right_logical(bits[:, :n], jnp.uint32(16)) | bits[:, n:]


def _unpack_bf16_pairs(xp):
    lo = lax.bitcast_convert_type(lax.shift_left(xp, jnp.uint32(16)), F32)
    hi = lax.bitcast_convert_type(xp & jnp.uint32(0xFFFF0000), F32)
    return jnp.concatenate([lo, hi], axis=1).astype(BF16)


def _rms(x, g):
    return x * lax.rsqrt(jnp.mean(x * x, axis=-1, keepdims=True) + EPS) * g


def _adaln_kernel(c_ref, w_ref, b_ref, o_ref):
    c = c_ref[...]
    s = c * jax.nn.sigmoid(c)
    o_ref[...] = _bdot(s, w_ref[...]) + b_ref[...]


def _adaln(c, w, b):
    m, k = c.shape
    n = w.shape[1]
    tn = 1024
    return pl.pallas_call(
        _adaln_kernel,
        grid=(n // tn,),
        in_specs=[pl.BlockSpec((m, k), lambda j: (0, 0)),
                  pl.BlockSpec((k, tn), lambda j: (0, j)),
                  pl.BlockSpec((1, tn), lambda j: (0, j))],
        out_specs=pl.BlockSpec((m, tn), lambda j: (0, j)),
        out_shape=jax.ShapeDtypeStruct((m, n), F32),
        compiler_params=_cparams(("arbitrary",)),
    )(c, w, b.reshape(1, n))


def _inproj_kernel(x_ref, sc_ref, sh_ref, g_ref, w_ref, wqv_ref, bf_ref, *out_refs, transposed, q_scale):
    nbb, rb, d = x_ref.shape
    rows = nbb * rb
    x = x_ref[...]
    h = _rms(x, g_ref[...]) * (1.0 + sc_ref[...]) + sh_ref[...]
    hb = h.reshape(rows, d).astype(BF16)
    proj = jnp.dot(hb, w_ref[...], preferred_element_type=F32)
    u_ref, lf_ref = out_refs[:2]
    u_ref[...] = proj[:, :D_SSM]
    k = proj[:, D_SSM:D_SSM + D_ATTN]
    z = proj[:, proj.shape[1] - LANES:] + bf_ref[...]
    lf_ref[...] = jnp.minimum(z, 0.0) - jnp.log1p(jnp.exp(-jnp.abs(z)))
    nt = (((1,), (1,)), ((), ()))
    if transposed:
        kb_ref, qt_ref, vt_ref, ktf_ref, vtf_ref = out_refs[2:]
        kb_ref[...] = k.astype(BF16)
        qt = (lax.dot_general(wqv_ref[0], hb, nt, preferred_element_type=F32) * q_scale).astype(BF16)
        ktf_ref[0] = lax.dot_general(wqv_ref[1], hb, nt, preferred_element_type=F32)
        vt = lax.dot_general(wqv_ref[2], hb, nt, preferred_element_type=F32)
        vtf_ref[0] = vt
        vt = vt.astype(BF16)
        for c in range(rows // ATTN_TILE):
            qt_ref[c] = qt[:, c * ATTN_TILE:(c + 1) * ATTN_TILE]
            vt_ref[c] = vt[:, c * ATTN_TILE:(c + 1) * ATTN_TILE]
    else:
        k_ref, v_ref, q_ref = out_refs[2:]
        k_ref[...] = k
        v_ref[...] = proj[:, D_SSM + D_ATTN:D_SSM + 2 * D_ATTN]
        q_ref[...] = lax.dot_general(hb, wqv_ref[0], nt, preferred_element_type=F32) * q_scale


def _inproj(x3, sc, sh, g, w_main, w_qv, b_f, nbb, rb, transposed, q_scale):
    nb, r, d = x3.shape
    t = nb * r
    rows = nbb * rb
    nj = r // rb
    grid = (nb // nbb, nj)
    row_map = lambda i, j: (i * nj + j, 0)
    n_main = w_main.shape[1]
    outs = [jax.ShapeDtypeStruct((t, D_SSM), F32), jax.ShapeDtypeStruct((t, LANES), F32)]
    out_specs = [pl.BlockSpec((rows, D_SSM), row_map), pl.BlockSpec((rows, LANES), row_map)]
    if transposed:
        assert nbb == 1, "transposed outputs are laid out per sequence"
        nc = rows // ATTN_TILE
        chunk_map = lambda i, j: (i * nj + j, 0, 0)
        seq_map = lambda i, j: (i, 0, j)
        outs += [jax.ShapeDtypeStruct((t, D_ATTN), BF16),
                 jax.ShapeDtypeStruct((t // ATTN_TILE, D_ATTN, ATTN_TILE), BF16),
                 jax.ShapeDtypeStruct((t // ATTN_TILE, D_ATTN, ATTN_TILE), BF16),
                 jax.ShapeDtypeStruct((nb, D_ATTN, r), F32),
                 jax.ShapeDtypeStruct((nb, D_ATTN, r), F32)]
        out_specs += [pl.BlockSpec((rows, D_ATTN), row_map),
                      pl.BlockSpec((nc, D_ATTN, ATTN_TILE), chunk_map),
                      pl.BlockSpec((nc, D_ATTN, ATTN_TILE), chunk_map),
                      pl.BlockSpec((1, D_ATTN, rows), seq_map),
                      pl.BlockSpec((1, D_ATTN, rows), seq_map)]
    else:
        outs += [jax.ShapeDtypeStruct((t, D_ATTN), F32)] * 3
        out_specs += [pl.BlockSpec((rows, D_ATTN), row_map)] * 3
    return pl.pallas_call(
        functools.partial(_inproj_kernel, transposed=transposed, q_scale=q_scale),
        grid=grid,
        in_specs=[pl.BlockSpec((nbb, rb, d), lambda i, j: (i, j, 0)),
                  pl.BlockSpec((nbb, 1, d), lambda i, j: (i, 0, 0)),
                  pl.BlockSpec((nbb, 1, d), lambda i, j: (i, 0, 0)),
                  pl.BlockSpec((1, 1, d), lambda i, j: (0, 0, 0)),
                  pl.BlockSpec((d, n_main), lambda i, j: (0, 0)),
                  pl.BlockSpec((3, D_ATTN, d), lambda i, j: (0, 0, 0)),
                  pl.BlockSpec((1, LANES), lambda i, j: (0, 0))],
        out_specs=out_specs,
        out_shape=outs,
        compiler_params=_cparams(("arbitrary", "arbitrary")),
    )(x3, sc, sh, g, w_main, w_qv, b_f)


def _crep_kernel(lf_ref, tri_ref, ex_ref, o_ref, carry_ref, *, scale):
    j = pl.program_id(1)

    @pl.when(j == 0)
    def _():
        carry_ref[...] = jnp.zeros_like(carry_ref)

    tri = tri_ref[...]
    ex = ex_ref[...]
    cs = sum(jnp.dot(tri, p, preferred_element_type=F32) for p in _split3(lf_ref[...]))
    rep = sum(jnp.dot(p, ex, preferred_element_type=F32) for p in _split3(cs))
    rep = rep + carry_ref[...]
    rows = rep.shape[0]
    carry_ref[...] = rep[rows - 1:rows, :]
    for h in range(N_HEADS):
        o_ref[0, h] = rep[:, h * LANES:(h + 1) * LANES] * (-scale)


def _crep(lf128, nb, length, scale):
    rows = ATTN_TILE
    nj = length // rows
    tri = np.tril(np.ones((rows, rows), np.float32))
    ex = np.zeros((LANES, N_HEADS * LANES), np.float32)
    for h in range(N_HEADS):
        ex[h, h * LANES:(h + 1) * LANES] = 1.0
    return pl.pallas_call(
        functools.partial(_crep_kernel, scale=scale),
        grid=(nb, nj),
        in_specs=[pl.BlockSpec((rows, LANES), lambda i, j: (i * nj + j, 0)),
                  pl.BlockSpec((rows, rows), lambda i, j: (0, 0)),
                  pl.BlockSpec((LANES, N_HEADS * LANES), lambda i, j: (0, 0))],
        out_specs=pl.BlockSpec((1, N_HEADS, rows, LANES), lambda i, j: (i, 0, j, 0)),
        out_shape=jax.ShapeDtypeStruct((nb, N_HEADS, length, LANES), F32),
        scratch_shapes=[pltpu.VMEM((1, N_HEADS * LANES), F32)],
        compiler_params=_cparams(("arbitrary", "arbitrary")),
    )(lf128, jnp.asarray(tri, BF16), jnp.asarray(ex, BF16))


def _cumsum_kernel(x_ref, tri_ref, lt_ref, o_ref):
    tri = tri_ref[...]
    lt = lt_ref[...]
    cs = sum(jnp.dot(p, tri, preferred_element_type=F32) for p in _split3(x_ref[...]))
    tot = jnp.broadcast_to(cs[:, LANES - 1:LANES], cs.shape)
    carry = sum(jnp.dot(lt, p, preferred_element_type=F32) for p in _split3(tot))
    o_ref[...] = -(cs + carry)


def _neg_cumsum(x2, gsz, rb):
    r = x2.shape[0]
    tri = np.triu(np.ones((LANES, LANES), np.float32))
    ii = np.arange(rb)
    lt = ((ii[:, None] // gsz == ii[None, :] // gsz) & (ii[None, :] < ii[:, None])).astype(np.float32)
    return pl.pallas_call(
        _cumsum_kernel,
        grid=(r // rb,),
        in_specs=[pl.BlockSpec((rb, LANES), lambda i: (i, 0)),
                  pl.BlockSpec((LANES, LANES), lambda i: (0, 0)),
                  pl.BlockSpec((rb, rb), lambda i: (0, 0))],
        out_specs=pl.BlockSpec((rb, LANES), lambda i: (i, 0)),
        out_shape=jax.ShapeDtypeStruct((r, LANES), F32),
        compiler_params=_cparams(("arbitrary",)),
    )(x2, jnp.asarray(tri, BF16), jnp.asarray(lt, BF16))


def _gelu_tanh(x):
    return 0.5 * x * (1.0 + jnp.tanh(math.sqrt(2.0 / math.pi) * (x + 0.044715 * (x * x * x))))


def _s5_kernel(u_ref, h0_ref, wb_ref, wc_ref, are_ref, aim_ref, dsk_ref, wglu_ref, bglu_ref, gout_ref,
               y_ref, ht_ref, s_ref, hc_ref, *, nseq, tm):
    rows = nseq * tm
    sr = rows + SUBLANES
    ti = pl.program_id(1)

    @pl.when(ti == 0)
    def _():
        hc_ref[...] = h0_ref[...]

    u = u_ref[...].reshape(rows, D_SSM)
    ub = u.astype(BF16)
    for c in range(4):
        bu = jnp.dot(ub[:, c * LANES:(c + 1) * LANES], wb_ref[c], preferred_element_type=F32)
        for jj in range(4):
            s_ref[pl.ds((4 * c + jj) * sr, rows), :] = bu[:, jj * LANES:(jj + 1) * LANES]
            s_ref[pl.ds((16 + 4 * c + jj) * sr, rows), :] = bu[:, 512 + jj * LANES:512 + (jj + 1) * LANES]

    ar = (are_ref[0:8, :], are_ref[8:16, :])
    ai = (aim_ref[0:8, :], aim_ref[8:16, :])

    def seq_group(sg, carry):
        base = sg * 4
        hs = []
        for b in range(4):
            hs.append(tuple(hc_ref[base + b, pl.ds(8 * q, 8), :] for q in range(4)))

        def step(t, hs):
            new = []
            for b in range(4):
                row = (base + b) * tm + t
                hr0, hr1, hi0, hi1 = hs[b]
                bre0 = s_ref[pl.ds(row, 8, stride=sr), :]
                bre1 = s_ref[pl.ds(8 * sr + row, 8, stride=sr), :]
                bim0 = s_ref[pl.ds(16 * sr + row, 8, stride=sr), :]
                bim1 = s_ref[pl.ds(24 * sr + row, 8, stride=sr), :]
                nr0 = ar[0] * hr0 - ai[0] * hi0 + bre0
                nr1 = ar[1] * hr1 - ai[1] * hi1 + bre1
                ni0 = ar[0] * hi0 + ai[0] * hr0 + bim0
                ni1 = ar[1] * hi1 + ai[1] * hr1 + bim1
                s_ref[pl.ds(row, 8, stride=sr), :] = nr0
                s_ref[pl.ds(8 * sr + row, 8, stride=sr), :] = nr1
                s_ref[pl.ds(16 * sr + row, 8, stride=sr), :] = ni0
                s_ref[pl.ds(24 * sr + row, 8, stride=sr), :] = ni1
                new.append((nr0, nr1, ni0, ni1))
            return tuple(new)

        hs = lax.fori_loop(0, tm, step, tuple(hs), unroll=2)
        for b in range(4):
            for q in range(4):
                hc_ref[base + b, pl.ds(8 * q, 8), :] = hs[b][q]
        return carry

    lax.fori_loop(0, nseq // 4, seq_group, 0)
    ht_ref[...] = hc_ref[...]

    ys = []
    for c in range(4):
        blocks = [s_ref[pl.ds((4 * c + jj) * sr, rows), :].astype(BF16) for jj in range(4)]
        blocks += [s_ref[pl.ds((16 + 4 * c + jj) * sr, rows), :].astype(BF16) for jj in range(4)]
        hcat = jnp.concatenate(blocks, axis=1)
        ys.append(jnp.dot(hcat, wc_ref[c], preferred_element_type=F32))
    y = jnp.concatenate(ys, axis=1) + dsk_ref[...] * u
    y = _gelu_tanh(y)
    gate = jax.nn.sigmoid(jnp.dot(y.astype(BF16), wglu_ref[...], preferred_element_type=F32) + bglu_ref[...])
    y = y * gate
    y_ref[...] = _rms(y, gout_ref[...]).astype(BF16).reshape(y_ref.shape)


def _s5(u3, h0, wb, wc, a_re, a_im, dsk, wglu, bglu, gout, nseq, tm):
    nb, length, _ = u3.shape
    rows = nseq * tm
    sr = rows + SUBLANES
    grid = (nb // nseq, length // tm)
    const2 = lambda i, j: (0, 0)
    const3 = lambda i, j: (0, 0, 0)
    if tm == length:
        y_spec = pl.BlockSpec((rows, D_SSM), lambda i, j: (i, 0))
        y_shape = jax.ShapeDtypeStruct((nb * length, D_SSM), BF16)
    else:
        y_spec = pl.BlockSpec((nseq, tm, D_SSM), lambda i, j: (i, j, 0))
        y_shape = jax.ShapeDtypeStruct((nb, length, D_SSM), BF16)
    return pl.pallas_call(
        functools.partial(_s5_kernel, nseq=nseq, tm=tm),
        grid=grid,
        in_specs=[pl.BlockSpec((nseq, tm, D_SSM), lambda i, j: (i, j, 0)),
                  pl.BlockSpec((nseq, 32, LANES), lambda i, j: (i, 0, 0)),
                  pl.BlockSpec((4, LANES, 1024), const3),
                  pl.BlockSpec((4, 1024, LANES), const3),
                  pl.BlockSpec((16, LANES), const2),
                  pl.BlockSpec((16, LANES), const2),
                  pl.BlockSpec((1, D_SSM), const2),
                  pl.BlockSpec((D_SSM, D_SSM), const2),
                  pl.BlockSpec((1, D_SSM), const2),
                  pl.BlockSpec((1, D_SSM), const2)],
        out_specs=[y_spec, pl.BlockSpec((nseq, 32, LANES), lambda i, j: (i, 0, 0))],
        out_shape=[y_shape, jax.ShapeDtypeStruct((nb, 32, LANES), F32)],
        scratch_shapes=[pltpu.VMEM((32 * sr, LANES), F32), pltpu.VMEM((nseq, 32, LANES), F32)],
        compiler_params=_cparams(("arbitrary", "arbitrary")),
    )(u3, h0, wb, wc, a_re, a_im, dsk, wglu, bglu, gout)


def _s5_params(lam_re, lam_im, log_step, b_re, b_im, c_re, c_im):
    g = lam_re.shape[0]
    dt = jnp.exp(log_step)[:, None]
    mag = jnp.exp(lam_re * dt)
    a_re = mag * jnp.cos(lam_im * dt)
    a_im = mag * jnp.sin(lam_im * dt)
    den = lam_re * lam_re + lam_im * lam_im
    n_re = a_re - 1.0
    f_re = (n_re * lam_re + a_im * lam_im) / den
    f_im = (a_im * lam_re - n_re * lam_im) / den
    bb_re = f_re[..., None] * b_re - f_im[..., None] * b_im
    bb_im = f_re[..., None] * b_im + f_im[..., None] * b_re
    eye = jnp.eye(g, dtype=F32)
    n_state = g * SSM_STATE

    def in_mat(bb):
        return jnp.einsum('gpi,gh->gihp', bb, eye).reshape(g * SSM_GROUP, n_state)

    def out_mat(cc):
        return jnp.einsum('gip,gh->gphi', cc, eye).reshape(n_state, g * SSM_GROUP)

    wbr, wbi = in_mat(bb_re), in_mat(bb_im)
    wcr, wci = out_mat(c_re), out_mat(-c_im)
    wb = jnp.stack([jnp.concatenate([wbr[c * 128:(c + 1) * 128, c * 512:(c + 1) * 512],
                                     wbi[c * 128:(c + 1) * 128, c * 512:(c + 1) * 512]], axis=1)
                    for c in range(4)]).astype(BF16)
    wc = jnp.stack([jnp.concatenate([wcr[c * 512:(c + 1) * 512, c * 128:(c + 1) * 128],
                                     wci[c * 512:(c + 1) * 512, c * 128:(c + 1) * 128]], axis=0)
                    for c in range(4)]).astype(BF16)
    return wb, wc, a_re.reshape(16, LANES), a_im.reshape(16, LANES)


def _prompt_attention_step(qi, qt_ref, k_ref, vt_ref, cr_ref, g_ref, o_ref, qz_ref, m_ref, l_ref, acc_ref,
                           st_ref):
    tq = ATTN_TILE
    half = LANES // 2

    rowid = lax.broadcasted_iota(I32, (LANES, tq), 0)
    for j in range(N_HEADS // 2):
        qp = qt_ref[0, j * LANES:(j + 1) * LANES, :]
        qz_ref[2 * j] = jnp.where(rowid < half, qp, jnp.zeros_like(qp))
        qz_ref[2 * j + 1] = jnp.where(rowid >= half, qp, jnp.zeros_like(qp))
    m_ref[...] = jnp.full_like(m_ref, NEG_INF)
    l_ref[...] = jnp.zeros_like(l_ref)
    acc_ref[...] = jnp.zeros_like(acc_ref)

    key_row = lax.broadcasted_iota(I32, (tq, tq), 0)
    q_col = lax.broadcasted_iota(I32, (tq, tq), 1)
    causal = key_row <= q_col

    def tiles(kbs, masked):
        m_new = [m_ref[h:h + 1, :] for h in range(N_HEADS)]
        for c, kb in enumerate(kbs):
            ks = pl.multiple_of(kb * tq, tq)
            for j in range(N_HEADS // 2):
                kp = k_ref[0, pl.ds(ks, tq), j * LANES:(j + 1) * LANES]
                for e in range(2):
                    h = 2 * j + e
                    st = jnp.dot(kp, qz_ref[h], preferred_element_type=F32)
                    bias = cr_ref[0, h, pl.ds(ks, tq), :]
                    st = st + jnp.concatenate([bias] * (tq // LANES), axis=1)
                    if masked:
                        st = jnp.where(causal, st, NEG_INF)
                    st_ref[c, h] = st
                    m_new[h] = jnp.maximum(m_new[h], jnp.max(st, axis=0, keepdims=True))
        for h in range(N_HEADS):
            alpha = jnp.exp2(m_ref[h:h + 1, :] - m_new[h])
            rows = slice(h * HEAD_DIM, (h + 1) * HEAD_DIM)
            l_new = alpha * l_ref[h:h + 1, :]
            acc = alpha * acc_ref[rows, :]
            for c, kb in enumerate(kbs):
                p = jnp.exp2(st_ref[c, h] - m_new[h])
                l_new = l_new + jnp.sum(p, axis=0, keepdims=True)
                acc = acc + jnp.dot(vt_ref[kb, rows, :], p.astype(BF16), preferred_element_type=F32)
            l_ref[h:h + 1, :] = l_new
            m_ref[h:h + 1, :] = m_new[h]
            acc_ref[rows, :] = acc

    def body(g, c):
        tiles([g * ATTN_GROUP + i for i in range(ATTN_GROUP)], False)
        return c

    n_groups = qi // ATTN_GROUP
    lax.fori_loop(0, n_groups, body, 0)
    for r in range(1, ATTN_GROUP):
        @pl.when(qi - n_groups * ATTN_GROUP == r)
        def _(r=r):
            tiles([n_groups * ATTN_GROUP + i for i in range(r)], False)
    tiles([qi], True)

    for h in range(N_HEADS):
        rows = slice(h * HEAD_DIM, (h + 1) * HEAD_DIM)
        acc_ref[rows, :] = acc_ref[rows, :] / l_ref[h:h + 1, :]
    o = acc_ref[...].T
    o_ref[0] = _rms(o, g_ref[...]).astype(BF16)


def _paged_seq_attention(q8, kn, vn, cn, g, kpages, vpages, n_pages, n_new):
    n_past = n_pages * PAGE
    d = q8.shape[-1]
    nr = N_HEADS * n_new
    new_bits = n_new.bit_length() - 1
    head_bits = HEAD_DIM.bit_length() - 1
    rowh = lax.shift_right_logical(lax.broadcasted_iota(I32, (nr, d), 0), new_bits)
    colh = lax.shift_right_logical(lax.broadcasted_iota(I32, (nr, d), 1), head_bits)
    bd = rowh == colh
    qrep = jnp.broadcast_to(q8[None], (N_HEADS, n_new, d)).reshape(nr, d)
    qbd = jnp.where(bd, qrep, 0.0).astype(BF16)
    cnr = jnp.broadcast_to(cn[:, None, :], (N_HEADS, n_new, cn.shape[-1])).reshape(nr, cn.shape[-1])

    s_p = []
    for p in range(n_pages):
        kt = kpages[p].reshape(d, PAGE)
        s_p.append(_bdot(qbd, kt) + cnr[:, p * PAGE:(p + 1) * PAGE])
    s_n = _bdot_nt(qbd, kn) + cnr[:, n_past:n_past + n_new]
    qpos = lax.broadcasted_iota(I32, (nr, n_new), 0) & (n_new - 1)
    kpos = lax.broadcasted_iota(I32, (nr, n_new), 1)
    s_n = jnp.where(kpos <= qpos, s_n, NEG_INF)
    m = jnp.max(s_n, axis=1, keepdims=True)
    for sp in s_p:
        m = jnp.maximum(m, jnp.max(sp, axis=1, keepdims=True))
    p_n = jnp.exp(s_n - m)
    l = jnp.sum(p_n, axis=1, keepdims=True)
    of = _bdot(p_n, vn)
    for p in range(n_pages):
        pp = jnp.exp(s_p[p] - m)
        l = l + jnp.sum(pp, axis=1, keepdims=True)
        of = of + _bdot_nt(pp, vpages[p].reshape(d, PAGE))
    of = jnp.where(bd, of / l, 0.0)
    o = jnp.sum(of.reshape(N_HEADS, n_new, d), axis=0)
    return _rms(o, g)


def _attention_kernel(pt_ref, qt_ref, k_ref, vt_ref, cr_ref, g_ref, qs_ref, kn_ref, vn_ref, cn_ref, kc_hbm, vc_hbm,
                      o_ref, os_ref, qz_ref, m_ref, l_ref, acc_ref, st_ref, kbuf, vbuf, sem, *, n_pages, n_seq):
    step = pl.program_id(0) * pl.num_programs(1) + pl.program_id(1)
    per_step = qs_ref.shape[0]
    n_new = qs_ref.shape[1]

    def page_copies(seq, slot):
        cps = []
        for p in range(n_pages):
            pg = pt_ref[seq * n_pages + p]
            cps.append(pltpu.make_async_copy(kc_hbm.at[pg], kbuf.at[slot, p], sem.at[0, slot]))
            cps.append(pltpu.make_async_copy(vc_hbm.at[pg], vbuf.at[slot, p], sem.at[1, slot]))
        return cps

    @pl.when(step == 0)
    def _():
        for slot in range(2):
            for cp in page_copies(slot, slot):
                cp.start()

    def sample_seq(j):
        seq = step * per_step + j
        slot = j % 2
        for cp in page_copies(seq, slot):
            cp.wait()
        os_ref[j] = _paged_seq_attention(qs_ref[j], kn_ref[j], vn_ref[j], cn_ref[j], g_ref[...],
                                         kbuf.at[slot], vbuf.at[slot], n_pages, n_new)

        @pl.when(seq + 2 < n_seq)
        def _():
            for cp in page_copies(seq + 2, slot):
                cp.start()

    for j in range(per_step // 2):
        sample_seq(j)
    _prompt_attention_step(pl.program_id(1), qt_ref, k_ref, vt_ref, cr_ref, g_ref, o_ref,
                           qz_ref, m_ref, l_ref, acc_ref, st_ref)
    for j in range(per_step // 2, per_step):
        sample_seq(j)


def _attention(qt, k, vt, crep, g, page_table, qs, kn, vn, cn, cache_k, cache_v):
    b, length, d = k.shape
    tq = ATTN_TILE
    nq = length // tq
    n_seq, n_new, _ = qs.shape
    assert n_new & (n_new - 1) == 0, "new-token count must be a power of two"
    per_step = n_seq // (b * nq)
    assert per_step % 2 == 0 and per_step * b * nq == n_seq
    n_pages = page_table.shape[1]
    once = pl.Buffered(1)
    smap = lambda i, j, pt: (i * nq + j, 0, 0)
    grid_spec = pltpu.PrefetchScalarGridSpec(
        num_scalar_prefetch=1,
        grid=(b, nq),
        in_specs=[pl.BlockSpec((1, d, tq), smap),
                  pl.BlockSpec((1, length, d), lambda i, j, pt: (i, 0, 0), pipeline_mode=once),
                  pl.BlockSpec((nq, d, tq), lambda i, j, pt: (i, 0, 0), pipeline_mode=once),
                  pl.BlockSpec((1, N_HEADS, length, LANES), lambda i, j, pt: (i, 0, 0, 0), pipeline_mode=once),
                  pl.BlockSpec((1, d), lambda i, j, pt: (0, 0)),
                  pl.BlockSpec((per_step, n_new, d), smap),
                  pl.BlockSpec((per_step, n_new, d), smap),
                  pl.BlockSpec((per_step, n_new, d), smap),
                  pl.BlockSpec((per_step, N_HEADS, cn.shape[-1]), smap),
                  pl.BlockSpec(memory_space=pl.ANY),
                  pl.BlockSpec(memory_space=pl.ANY)],
        out_specs=[pl.BlockSpec((1, tq, d), lambda i, j, pt: (i, j, 0)),
                   pl.BlockSpec((per_step, n_new, d), smap)],
        scratch_shapes=[pltpu.VMEM((N_HEADS, LANES, tq), BF16), pltpu.VMEM((N_HEADS, tq), F32),
                        pltpu.VMEM((N_HEADS, tq), F32), pltpu.VMEM((d, tq), F32),
                        pltpu.VMEM((ATTN_GROUP, N_HEADS, tq, tq), F32),
                        pltpu.VMEM((2, n_pages, N_HEADS, HEAD_DIM, PAGE), F32),
                        pltpu.VMEM((2, n_pages, N_HEADS, HEAD_DIM, PAGE), F32),
                        pltpu.SemaphoreType.DMA((2, 2))],
    )
    return pl.pallas_call(
        functools.partial(_attention_kernel, n_pages=n_pages, n_seq=n_seq),
        grid_spec=grid_spec,
        out_shape=[jax.ShapeDtypeStruct((b, length, d), BF16), jax.ShapeDtypeStruct((n_seq, n_new, d), F32)],
        compiler_params=pltpu.CompilerParams(dimension_semantics=("arbitrary", "arbitrary"),
                                             vmem_limit_bytes=ATTN_VMEM_LIMIT),
    )(page_table.reshape(-1), qt, k, vt, crep, g, qs, kn, vn, cn, cache_k, cache_v)


def _outproj_kernel(x_ref, ys_ref, oa_ref, gt_ref, sc_ref, sh_ref, g_ref, wo_ref, wrt_ref,
                    x1_ref, h2_ref, lg_ref):
    nbb, rb, d = x_ref.shape
    mix = jnp.dot(ys_ref[...], wo_ref[0:D_SSM, :], preferred_element_type=F32)
    mix = mix + jnp.dot(oa_ref[...], wo_ref[D_SSM:, :], preferred_element_type=F32)
    x1 = x_ref[...] + gt_ref[...] * mix.reshape(nbb, rb, d)
    x1_ref[...] = x1
    h2 = (_rms(x1, g_ref[...]) * (1.0 + sc_ref[...]) + sh_ref[...]).reshape(nbb * rb, d)
    hb = h2.astype(BF16)
    h2_ref[...] = _pack_bf16_pairs(hb)
    hlo = (h2 - hb.astype(F32)).astype(BF16)
    whi = wrt_ref[0]
    wlo = wrt_ref[1]
    nt = (((1,), (1,)), ((), ()))
    lg = lax.dot_general(whi, hb, nt, preferred_element_type=F32)
    lg = lg + lax.dot_general(wlo, hb, nt, preferred_element_type=F32)
    lg = lg + lax.dot_general(whi, hlo, nt, preferred_element_type=F32)
    lg_ref[...] = lg


def _outproj(x3, ys, oa, gt, sc, sh, g, wo, wrt, nbb, rb):
    nb, r, d = x3.shape
    t = nb * r
    rows = nbb * rb
    nj = r // rb
    row_map = lambda i, j: (i * nj + j, 0)
    mod_spec = pl.BlockSpec((nbb, 1, d), lambda i, j: (i, 0, 0))
    return pl.pallas_call(
        _outproj_kernel,
        grid=(nb // nbb, nj),
        in_specs=[pl.BlockSpec((nbb, rb, d), lambda i, j: (i, j, 0)),
                  pl.BlockSpec((rows, D_SSM), row_map),
                  pl.BlockSpec((rows, D_ATTN), row_map),
                  mod_spec, mod_spec, mod_spec,
                  pl.BlockSpec((1, 1, d), lambda i, j: (0, 0, 0)),
                  pl.BlockSpec((D_SSM + D_ATTN, d), lambda i, j: (0, 0)),
                  pl.BlockSpec((2, N_EXPERTS, d), lambda i, j: (0, 0, 0))],
        out_specs=[pl.BlockSpec((nbb, rb, d), lambda i, j: (i, j, 0)),
                   pl.BlockSpec((rows, d // 2), row_map),
                   pl.BlockSpec((N_EXPERTS, rows), lambda i, j: (0, i * nj + j))],
        out_shape=[jax.ShapeDtypeStruct((nb, r, d), F32),
                   jax.ShapeDtypeStruct((t, d // 2), jnp.uint32),
                   jax.ShapeDtypeStruct((N_EXPERTS, t), F32)],
        compiler_params=_cparams(("arbitrary", "arbitrary")),
    )(x3, ys, oa, gt, sc, sh, g, wo, wrt)


def _router_kernel(lg_ref, rb_ref, ut_ref, idx_ref, w_ref, rank_ref, cnt_ref, carry_ref):
    i = pl.program_id(0)
    tm = lg_ref.shape[1]
    per_group = N_EXPERTS // N_EXPERT_GROUPS

    @pl.when(i == 0)
    def _():
        carry_ref[...] = jnp.zeros_like(carry_ref)

    scores = jax.nn.sigmoid(lg_ref[...])
    biased = scores + rb_ref[...]
    blks, grp = [], []
    for g in range(N_EXPERT_GROUPS):
        blk = biased[g * per_group:(g + 1) * per_group, :]
        m1 = jnp.max(blk, axis=0, keepdims=True)
        eq = blk == m1
        n_eq = jnp.sum(jnp.where(eq, 1.0, 0.0), axis=0, keepdims=True)
        m2 = jnp.max(jnp.where(eq, NEG_INF, blk), axis=0, keepdims=True)
        blks.append(blk)
        grp.append(m1 + jnp.where(n_eq >= 2.0, m1, m2))
    masked = []
    for g in range(N_EXPERT_GROUPS):
        beaten = jnp.zeros((1, tm), F32)
        for o in range(N_EXPERT_GROUPS):
            if o == g:
                continue
            ahead = (grp[o] >= grp[g]) if o < g else (grp[o] > grp[g])
            beaten = beaten + jnp.where(ahead, 1.0, 0.0)
        masked.append(jnp.where(beaten < float(TOPK_GROUPS), blks[g], NEG_INF))
    work = jnp.concatenate(masked, axis=0)

    eid = lax.broadcasted_iota(I32, (N_EXPERTS, tm), 0)
    chosen = jnp.zeros((N_EXPERTS, tm), F32)
    idxs, ws, sels = [], [], []
    for _ in range(TOP_K):
        m = jnp.max(work, axis=0, keepdims=True)
        first = jnp.min(jnp.where(work == m, eid, N_EXPERTS), axis=0, keepdims=True)
        sel = eid == first
        idxs.append(first)
        ws.append(jnp.sum(jnp.where(sel, scores, 0.0), axis=0, keepdims=True))
        sels.append(sel)
        chosen = jnp.where(sel, 1.0, chosen)
        work = jnp.where(sel, NEG_INF, work)
    wsum = ws[0]
    for wk in ws[1:]:
        wsum = wsum + wk

    prefix = jnp.dot(chosen.astype(BF16), ut_ref[...], preferred_element_type=F32) + carry_ref[...]
    carry_ref[...] = carry_ref[...] + jnp.sum(chosen, axis=1, keepdims=True)
    cnt_ref[...] = carry_ref[...]

    idx_ref[...] = jnp.zeros_like(idx_ref)
    w_ref[...] = jnp.zeros_like(w_ref)
    rank_ref[...] = jnp.zeros_like(rank_ref)
    for k in range(TOP_K):
        idx_ref[k:k + 1, :] = idxs[k]
        w_ref[k:k + 1, :] = ws[k] / wsum * ROUTED_SCALE
        rank = jnp.sum(jnp.where(sels[k], prefix, 0.0), axis=0, keepdims=True)
        rank_ref[k:k + 1, :] = rank.astype(I32)


def _router(lg, router_bias, tm):
    e, t = lg.shape
    ut = np.triu(np.ones((tm, tm), np.float32), 1)
    tok_spec = pl.BlockSpec((SUBLANES, tm), lambda i: (0, i))
    return pl.pallas_call(
        _router_kernel,
        grid=(t // tm,),
        in_specs=[pl.BlockSpec((e, tm), lambda i: (0, i)),
                  pl.BlockSpec((e, 1), lambda i: (0, 0)),
                  pl.BlockSpec((tm, tm), lambda i: (0, 0))],
        out_specs=[tok_spec, tok_spec, tok_spec, pl.BlockSpec((e, 1), lambda i: (0, 0))],
        out_shape=[jax.ShapeDtypeStruct((SUBLANES, t), I32), jax.ShapeDtypeStruct((SUBLANES, t), F32),
                   jax.ShapeDtypeStruct((SUBLANES, t), I32), jax.ShapeDtypeStruct((e, 1), F32)],
        scratch_shapes=[pltpu.VMEM((e, 1), F32)],
        compiler_params=_cparams(("arbitrary",)),
    )(lg, router_bias.reshape(e, 1), jnp.asarray(ut, BF16))


def _experts_kernel(be_ref, nv_ref, new_ref, fill_ref, x_ref, wg_ref, wu_ref, wd_ref, o_ref, wgu_s, wd_s):
    i = pl.program_id(0)
    de = wg_ref.shape[-1]
    valid = i < nv_ref[0]

    @pl.when(valid & (new_ref[i] == 1))
    def _():
        wgu_s[:, :de] = wg_ref[0].astype(BF16)
        wgu_s[:, de:] = wu_ref[0].astype(BF16)
        wd_s[...] = wd_ref[0].astype(BF16)

    @pl.when(valid)
    def _():
        row = lax.broadcasted_iota(I32, x_ref.shape, 0)
        x = jnp.where(row < fill_ref[i], x_ref[...], jnp.uint32(0))
        gu = jnp.dot(_unpack_bf16_pairs(x), wgu_s[...], preferred_element_type=F32)
        g = gu[:, :de]
        a = (g * jax.nn.sigmoid(g) * gu[:, de:]).astype(BF16)
        o = jnp.dot(a, wd_s[...], preferred_element_type=F32)
        o_ref[...] = _pack_bf16_pairs(o.astype(BF16))

    @pl.when(jnp.logical_not(valid))
    def _():
        o_ref[...] = jnp.zeros_like(o_ref)


def _experts(block_e, n_valid, block_new, block_fill, xs, w_eg, w_eu, w_ed, m):
    n_rows = xs.shape[0]
    d, de = w_eg.shape[-2:]
    nblk = n_rows // m
    xmap = lambda i, be, nv, *_: (jnp.minimum(i, nv[0] - 1), 0)
    wmap = lambda i, be, *_: (be[i], 0, 0)
    grid_spec = pltpu.PrefetchScalarGridSpec(
        num_scalar_prefetch=4,
        grid=(nblk,),
        in_specs=[pl.BlockSpec((m, xs.shape[1]), xmap),
                  pl.BlockSpec((1, d, de), wmap),
                  pl.BlockSpec((1, d, de), wmap),
                  pl.BlockSpec((1, de, d), wmap)],
        out_specs=pl.BlockSpec((m, d // 2), lambda i, *_: (i, 0)),
        scratch_shapes=[pltpu.VMEM((d, 2 * de), BF16), pltpu.VMEM((de, d), BF16)],
    )
    return pl.pallas_call(
        _experts_kernel,
        grid_spec=grid_spec,
        out_shape=jax.ShapeDtypeStruct((n_rows, d // 2), jnp.uint32),
        compiler_params=_cparams(("arbitrary",)),
    )(block_e, n_valid, block_new, block_fill, xs, w_eg, w_eu, w_ed)


def _gather_rows(table, idx, window):
    n = idx.shape[0]
    d = table.shape[1]
    info = plsc.get_sparse_core_info()
    n_workers = info.num_cores * info.num_subcores
    per_worker = n // n_workers
    n_buf = SC_GATHER_BUFFERS
    n_rounds = per_worker // (window * n_buf)
    assert n_rounds * window * n_buf * n_workers == n
    mesh = plsc.VectorSubcoreMesh(core_axis_name="c", subcore_axis_name="s")

    @functools.partial(
        pl.kernel, out_type=jax.ShapeDtypeStruct((n, d), table.dtype), mesh=mesh,
        scratch_types=[pltpu.VMEM((per_worker,), I32), pltpu.VMEM((n_buf, window, d), table.dtype),
                       pltpu.SemaphoreType.DMA((n_buf,)), pltpu.SemaphoreType.DMA((n_buf,))])
    def gather(x_hbm, i_hbm, o_hbm, i_v, buf, gsem, wsem):
        base = (lax.axis_index("s") * info.num_cores + lax.axis_index("c")) * per_worker
        pltpu.sync_copy(i_hbm.at[pl.ds(base, per_worker)], i_v)

        @pl.loop(0, n_rounds)
        def _(rnd):
            first = rnd * (window * n_buf)
            reads, writes = [], []
            for b in range(n_buf):
                rows = pl.ds(first + b * window, window)
                reads.append(pltpu.make_async_copy(x_hbm.at[i_v.at[rows]], buf.at[b], gsem.at[b]))
                writes.append(pltpu.make_async_copy(buf.at[b], o_hbm.at[pl.ds(base + first + b * window, window)],
                                                    wsem.at[b]))
            for cp in reads:
                cp.start()
            for b in range(n_buf):
                reads[b].wait()
                writes[b].start()
            for cp in writes:
                cp.wait()

    return gather(table, idx)


def _scatter_rows(x, dest_t, n_rows, window):
    t, d = x.shape
    info = plsc.get_sparse_core_info()
    n_workers = info.num_cores * info.num_subcores
    per_worker = t // n_workers
    chunks = per_worker // window
    n_buf = 2
    n_rounds = chunks // n_buf
    assert n_rounds * n_buf * window * n_workers == t
    idx_rows = -(-(TOP_K * chunks) // SUBLANES) * SUBLANES
    idx = dest_t[:TOP_K].reshape(TOP_K, n_workers, chunks, window).transpose(1, 0, 2, 3)
    idx = idx.reshape(n_workers, TOP_K * chunks, window)
    idx = jnp.pad(idx, ((0, 0), (0, idx_rows - TOP_K * chunks), (0, 0))).reshape(n_workers * idx_rows, window)
    mesh = plsc.VectorSubcoreMesh(core_axis_name="c", subcore_axis_name="s")

    @functools.partial(
        pl.kernel, out_type=jax.ShapeDtypeStruct((n_rows, d), x.dtype), mesh=mesh,
        scratch_types=[pltpu.VMEM((idx_rows, window), I32), pltpu.VMEM((n_buf, window, d), x.dtype),
                       pltpu.SemaphoreType.DMA((n_buf,)), pltpu.SemaphoreType.DMA((n_buf,))])
    def scatter(x_hbm, i_hbm, o_hbm, i_v, buf, rsem, wsem):
        wid = lax.axis_index("s") * info.num_cores + lax.axis_index("c")
        base = wid * per_worker
        pltpu.sync_copy(i_hbm.at[pl.ds(wid * idx_rows, idx_rows)], i_v)

        @pl.loop(0, n_rounds)
        def _(rnd):
            reads = [pltpu.make_async_copy(x_hbm.at[pl.ds(base + (rnd * n_buf + b) * window, window)],
                                           buf.at[b], rsem.at[b]) for b in range(n_buf)]
            for cp in reads:
                cp.start()
            writes = []
            for b in range(n_buf):
                reads[b].wait()
                for k in range(TOP_K):
                    cp = pltpu.make_async_copy(buf.at[b], o_hbm.at[i_v.at[k * chunks + rnd * n_buf + b]],
                                               wsem.at[b])
                    cp.start()
                    writes.append(cp)
            for cp in writes:
                cp.wait()

    return scatter(x, idx)


def _final_kernel(x1_ref, h2_ref, w_ref, gt_ref, sc_ref, sh_ref, g_ref, wgu_ref, wd_ref, *rest):
    yg_refs = rest[:TOP_K]
    y_ref = rest[TOP_K]
    nbb, rb, d = x1_ref.shape
    ds = wd_ref.shape[0]
    gu = jnp.dot(_unpack_bf16_pairs(h2_ref[...]), wgu_ref[...], preferred_element_type=F32)
    g = gu[:, :ds]
    a = (g * jax.nn.sigmoid(g) * gu[:, ds:]).astype(BF16)
    ff = jnp.dot(a, wd_ref[...], preferred_element_type=F32)
    routed_lo = routed_hi = None
    for k in range(TOP_K):
        yk = yg_refs[k][...]
        w = w_ref[:, k:k + 1]
        lo = lax.bitcast_convert_type(lax.shift_left(yk, jnp.uint32(16)), F32) * w
        hi = lax.bitcast_convert_type(yk & jnp.uint32(0xFFFF0000), F32) * w
        routed_lo = lo if k == 0 else routed_lo + lo
        routed_hi = hi if k == 0 else routed_hi + hi
    ff = jnp.concatenate([routed_lo, routed_hi], axis=1) + ff
    x2 = x1_ref[...] + gt_ref[...] * ff.reshape(nbb, rb, d)
    y_ref[...] = _rms(x2, g_ref[...]) * (1.0 + sc_ref[...]) + sh_ref[...]


def _final(x1, h2, wts, yg, first_tile, gt, sc, sh, g, wgu, wd, nbb, rb):
    nb, r, d = x1.shape
    rows = nbb * rb
    nj = r // rb
    tiles_all = yg.shape[0] // TOP_K // rows
    row_map = lambda i, j: (i * nj + j, 0)
    mod_spec = pl.BlockSpec((nbb, 1, d), lambda i, j: (i, 0, 0))
    slab_specs = [pl.BlockSpec((rows, yg.shape[1]), lambda i, j, k=k: (k * tiles_all + first_tile + i * nj + j, 0))
                  for k in range(TOP_K)]
    return pl.pallas_call(
        _final_kernel,
        grid=(nb // nbb, nj),
        in_specs=[pl.BlockSpec((nbb, rb, d), lambda i, j: (i, j, 0)),
                  pl.BlockSpec((rows, h2.shape[1]), row_map),
                  pl.BlockSpec((rows, SUBLANES), row_map),
                  mod_spec, mod_spec, mod_spec,
                  pl.BlockSpec((1, 1, d), lambda i, j: (0, 0, 0)),
                  pl.BlockSpec(wgu.shape, lambda i, j: (0, 0)),
                  pl.BlockSpec(wd.shape, lambda i, j: (0, 0))] + slab_specs,
        out_specs=pl.BlockSpec((nbb, rb, d), lambda i, j: (i, j, 0)),
        out_shape=jax.ShapeDtypeStruct((nb, r, d), F32),
        compiler_params=_cparams(("arbitrary", "arbitrary")),
    )(x1, h2, wts, gt, sc, sh, g, wgu, wd, *([yg] * TOP_K))


def _moe(h2_groups, lg, router_bias, w_eg, w_eu, w_ed, tile):
    t = lg.shape[1]
    e = N_EXPERTS
    m = MOE_ROWS
    idx_t, w_t, rank_t, counts = _router(lg, router_bias, tile)
    counts = counts.reshape(e).astype(I32)
    padded = ((counts + m - 1) // m) * m
    pad_end = jnp.cumsum(padded)
    pad_start = pad_end - padded
    onehot = idx_t[:, :, None] == jnp.arange(e, dtype=I32)
    dest_t = jnp.sum(jnp.where(onehot, pad_start, 0), axis=-1) + rank_t
    n_rows = (-(-(t * TOP_K) // m)) * m + e * m
    nblk = n_rows // m
    block_start = jnp.arange(nblk, dtype=I32) * m
    block_e = jnp.minimum(jnp.sum(pad_end[None, :] <= block_start[:, None], axis=1), e - 1).astype(I32)
    block_new = jnp.concatenate([jnp.ones((1,), I32), (block_e[1:] != block_e[:-1]).astype(I32)])
    n_valid = (pad_end[-1] // m).astype(I32).reshape(1)
    of_block = block_e[:, None] == jnp.arange(e, dtype=I32)
    count_b = jnp.sum(jnp.where(of_block, counts, 0), axis=1)
    start_b = jnp.sum(jnp.where(of_block, pad_start, 0), axis=1)
    block_fill = jnp.clip(count_b - (block_start - start_b), 0, m).astype(I32)
    xs = _scatter_rows(jnp.concatenate(h2_groups, axis=0), dest_t, n_rows, SC_SCATTER_WINDOW)
    yb = _experts(block_e, n_valid, block_new, block_fill, xs, w_eg, w_eu, w_ed, m)
    yg = _gather_rows(yb, dest_t[:TOP_K].reshape(-1), SC_GATHER_WINDOW)
    return w_t.T, yg


def kernel(x_prompt, x_sample, c_prompt, c_sample, cache_k, cache_v, cache_logf, state_ssm_re, state_ssm_im, page_table, w_ada, b_ada, g_norm1, w_in, b_fgate, ssm_lambda_re, ssm_lambda_im, ssm_log_step, ssm_b_re, ssm_b_im, ssm_c_re, ssm_c_im, ssm_d, w_glu, b_glu, g_ssm_out, g_attn_out, w_out, g_norm2, w_router, router_bias, w_exp_gate, w_exp_up, w_exp_down, w_sh_gate, w_sh_up, w_sh_down, g_final, w_ada_final, b_ada_final):
    depth = w_ada.shape[0]
    assert depth == 1, "one layer is supported"
    bp, lp, d = x_prompt.shape
    bs, ls, _ = x_sample.shape
    n_pages = page_table.shape[1]
    n_past = n_pages * PAGE
    n_groups = ssm_lambda_re.shape[1]

    n_c = bp + bs
    n_c_pad = -(-n_c // SUBLANES) * SUBLANES
    c_all = jnp.concatenate([c_prompt, c_sample, jnp.zeros((n_c_pad - n_c, d), F32)], axis=0)
    mod = _adaln(c_all, w_ada[0], b_ada[0])
    modf = _adaln(c_all, w_ada_final, b_ada_final)

    def mods(lo, hi):
        parts = [mod[lo:hi, k * d:(k + 1) * d][:, None, :] for k in range(6)]
        parts += [modf[lo:hi, k * d:(k + 1) * d][:, None, :] for k in range(2)]
        return parts

    w_u, w_q, w_k, w_v, w_f = jnp.split(w_in[0], [D_SSM, D_SSM + D_ATTN, D_SSM + 2 * D_ATTN,
                                                   D_SSM + 3 * D_ATTN], axis=1)
    w_fpad = jnp.concatenate([w_f, jnp.zeros((d, LANES - N_HEADS), F32)], axis=1)
    w_main_s = jnp.concatenate([w_u, w_k, w_v, w_fpad], axis=1).astype(BF16)
    w_main_p = jnp.concatenate([w_u, w_k, w_fpad], axis=1).astype(BF16)
    w_qkv = jnp.stack([w_q.T, w_k.T, w_v.T]).astype(BF16)
    b_f = jnp.concatenate([b_fgate[0], jnp.zeros((LANES - N_HEADS,), F32)]).reshape(1, LANES)
    g1 = g_norm1[0].reshape(1, 1, d)
    g2 = g_norm2[0].reshape(1, 1, d)
    gf = g_final.reshape(1, 1, d)
    wb, wc, a_re, a_im = _s5_params(ssm_lambda_re[0], ssm_lambda_im[0], ssm_log_step[0], ssm_b_re[0],
                                    ssm_b_im[0], ssm_c_re[0], ssm_c_im[0])
    dsk = ssm_d[0].reshape(1, D_SSM)
    wglu = w_glu[0].astype(BF16)
    bglu = b_glu[0].reshape(1, D_SSM)
    g_so = g_ssm_out[0].reshape(1, D_SSM)
    g_ao = g_attn_out[0].reshape(1, D_ATTN)
    wo = w_out[0].astype(BF16)
    wr_t = w_router[0].T
    wr_hi = wr_t.astype(BF16)
    wrt = jnp.stack([wr_hi, (wr_t - wr_hi.astype(F32)).astype(BF16)])
    wgu = jnp.concatenate([w_sh_gate[0], w_sh_up[0]], axis=1).astype(BF16)
    wsd = w_sh_down[0].astype(BF16)

    def ssm_state(re, im):
        return jnp.concatenate([re.reshape(-1, 16, LANES), im.reshape(-1, 16, LANES)], axis=1)

    def split_state(ht):
        n = ht.shape[0]
        return (ht[:, :16].reshape(1, n, n_groups, SSM_STATE), ht[:, 16:].reshape(1, n, n_groups, SSM_STATE))

    tm = 512
    sh1_p, sc1_p, gt1_p, sh2_p, sc2_p, gt2_p, shf_p, scf_p = mods(0, bp)
    u, lf, kb, qt, vt, ktf, vtf = _inproj(x_prompt, sc1_p, sh1_p, g1, w_main_p, w_qkv, b_f, 1, tm, True,
                                          HEAD_DIM ** -0.5 * LOG2E)
    crep = _crep(lf, bp, lp, LOG2E)
    ys_p, ht = _s5(u.reshape(bp, lp, D_SSM), jnp.zeros((bp, 32, LANES), F32), wb, wc, a_re, a_im, dsk,
                   wglu, bglu, g_so, bp, 256)
    k_prompt = ktf.reshape(bp, N_HEADS, HEAD_DIM, lp).transpose(0, 3, 1, 2)[None]
    v_prompt = vtf.reshape(bp, N_HEADS, HEAD_DIM, lp).transpose(0, 3, 1, 2)[None]
    logf_prompt = lf[:, :N_HEADS].reshape(1, bp, lp, N_HEADS)
    sre_p, sim_p = split_state(ht)

    nbb = 64
    sh1_s, sc1_s, gt1_s, sh2_s, sc2_s, gt2_s, shf_s, scf_s = mods(bp, bp + bs)
    u, lf, k, v, q = _inproj(x_sample, sc1_s, sh1_s, g1, w_main_s, w_qkv, b_f, nbb, ls, False, HEAD_DIM ** -0.5)
    ys_s, ht = _s5(u.reshape(bs, ls, D_SSM), ssm_state(state_ssm_re[0], state_ssm_im[0]), wb, wc, a_re, a_im,
                   dsk, wglu, bglu, g_so, 32, ls)
    lf_past = cache_logf[0][page_table].reshape(bs, n_past, N_HEADS).transpose(0, 2, 1)
    lf_new = lf[:, :N_HEADS].reshape(bs, ls, N_HEADS).transpose(0, 2, 1)
    n_key_pad = -(-(n_past + ls) // LANES) * LANES
    lf_all = jnp.concatenate([lf_past, lf_new, jnp.zeros((bs, N_HEADS, n_key_pad - n_past - ls), F32)], axis=2)
    gsz = n_key_pad // LANES
    cn_s = _neg_cumsum(lf_all.reshape(-1, LANES), gsz, gsz * 64).reshape(bs, N_HEADS, n_key_pad)

    oa_p, oa_s = _attention(qt, kb.reshape(bp, lp, D_ATTN), vt, crep, g_ao, page_table,
                            q.reshape(bs, ls, D_ATTN), k.reshape(bs, ls, D_ATTN), v.reshape(bs, ls, D_ATTN),
                            cn_s, cache_k[0].transpose(0, 2, 3, 1), cache_v[0].transpose(0, 2, 3, 1))
    x1_p, h2_p, lg_p = _outproj(x_prompt, ys_p.reshape(bp * lp, D_SSM), oa_p.reshape(bp * lp, D_ATTN),
                                gt1_p, sc2_p, sh2_p, g2, wo, wrt, 1, tm)
    x1_s, h2_s, lg_s = _outproj(x_sample, ys_s, oa_s.reshape(bs * ls, D_ATTN).astype(BF16),
                                gt1_s, sc2_s, sh2_s, g2, wo, wrt, nbb, ls)

    assert nbb * ls == tm
    wts, yg = _moe([h2_p, h2_s], jnp.concatenate([lg_p, lg_s], axis=1), router_bias[0],
                   w_exp_gate[0], w_exp_up[0], w_exp_down[0], tm)
    tiles_p = bp * lp // tm
    y_prompt = _final(x1_p, h2_p, wts[:bp * lp], yg, 0, gt2_p, scf_p, shf_p, gf, wgu, wsd, 1, tm)
    y_sample = _final(x1_s, h2_s, wts[bp * lp:], yg, tiles_p, gt2_s, scf_s, shf_s, gf, wgu, wsd, nbb, ls)
    k_sample = k.reshape(1, bs, ls, N_HEADS, HEAD_DIM)
    v_sample = v.reshape(1, bs, ls, N_HEADS, HEAD_DIM)
    logf_sample = lf[:, :N_HEADS].reshape(1, bs, ls, N_HEADS)
    sre_s, sim_s = split_state(ht)

    return (y_prompt, y_sample, k_prompt, v_prompt, logf_prompt, sre_p, sim_p,
            k_sample, v_sample, logf_sample, sre_s, sim_s)
```

```python
import functools
import math

import jax
import jax.numpy as jnp
import numpy as np
from jax import lax
from jax.experimental import pallas as pl
from jax.experimental.pallas import tpu as pltpu
from jax.experimental.pallas import tpu_sc as plsc

F32 = jnp.float32
BF16 = jnp.bfloat16
I32 = jnp.int32

EPS = 1e-6
HEAD_DIM = 64
N_HEADS = 8
D_SSM = 512
D_ATTN = 512
SSM_GROUP = 16
SSM_STATE = 64
N_EXPERTS = 64
TOP_K = 6
N_EXPERT_GROUPS = 8
TOPK_GROUPS = 4
ROUTED_SCALE = 2.5
PAGE = 128

LANES = 128
SUBLANES = 8
VMEM_LIMIT = 48 * 1024 * 1024
ATTN_VMEM_LIMIT = 58 * 1024 * 1024
MOE_ROWS = 512
ATTN_TILE = 256
ATTN_GROUP = 4
LOG2E = math.log2(math.e)
SC_SCATTER_WINDOW = 16
SC_GATHER_WINDOW = 48
SC_GATHER_BUFFERS = 4
NEG_INF = float("-inf")


def _cparams(sem):
    return pltpu.CompilerParams(dimension_semantics=sem, vmem_limit_bytes=VMEM_LIMIT)


def _bdot(a, b):
    return jnp.dot(a.astype(BF16), b.astype(BF16), preferred_element_type=F32)


def _bdot_nt(a, b):
    return lax.dot_general(a.astype(BF16), b.astype(BF16), (((1,), (1,)), ((), ())),
                           preferred_element_type=F32)


def _split3(v):
    hi = v.astype(BF16)
    r1 = v - hi.astype(F32)
    mid = r1.astype(BF16)
    lo = (r1 - mid.astype(F32)).astype(BF16)
    return hi, mid, lo


def _pack_bf16_pairs(xb):
    n = xb.shape[1] // 2
    bits = lax.bitcast_convert_type(xb.astype(F32), jnp.uint32)
    return lax.shift_right_logical(bits[:, :n], jnp.uint32(16)) | bits[:, n:]


def _unpack_bf16_pairs(xp):
    lo = lax.bitcast_convert_type(lax.shift_left(xp, jnp.uint32(16)), F32)
    hi = lax.bitcast_convert_type(xp & jnp.uint32(0xFFFF0000), F32)
    return jnp.concatenate([lo, hi], axis=1).astype(BF16)


def _rms(x, g):
    return x * lax.rsqrt(jnp.mean(x * x, axis=-1, keepdims=True) + EPS) * g


def _adaln_kernel(c_ref, w_ref, b_ref, o_ref):
    c = c_ref[...]
    s = c * jax.nn.sigmoid(c)
    o_ref[...] = _bdot(s, w_ref[...]) + b_ref[...]


def _adaln(c, w, b):
    m, k = c.shape
    n = w.shape[1]
    tn = 1024
    return pl.pallas_call(
        _adaln_kernel,
        grid=(n // tn,),
        in_specs=[pl.BlockSpec((m, k), lambda j: (0, 0)),
                  pl.BlockSpec((k, tn), lambda j: (0, j)),
                  pl.BlockSpec((1, tn), lambda j: (0, j))],
        out_specs=pl.BlockSpec((m, tn), lambda j: (0, j)),
        out_shape=jax.ShapeDtypeStruct((m, n), F32),
        compiler_params=_cparams(("arbitrary",)),
    )(c, w, b.reshape(1, n))


def _inproj_kernel(x_ref, sc_ref, sh_ref, g_ref, w_ref, wqv_ref, bf_ref, *out_refs, transposed, q_scale):
    nbb, rb, d = x_ref.shape
    rows = nbb * rb
    x = x_ref[...]
    h = _rms(x, g_ref[...]) * (1.0 + sc_ref[...]) + sh_ref[...]
    hb = h.reshape(rows, d).astype(BF16)
    proj = jnp.dot(hb, w_ref[...], preferred_element_type=F32)
    u_ref, lf_ref = out_refs[:2]
    u_ref[...] = proj[:, :D_SSM]
    k = proj[:, D_SSM:D_SSM + D_ATTN]
    z = proj[:, proj.shape[1] - LANES:] + bf_ref[...]
    lf_ref[...] = jnp.minimum(z, 0.0) - jnp.log1p(jnp.exp(-jnp.abs(z)))
    nt = (((1,), (1,)), ((), ()))
    if transposed:
        kb_ref, qt_ref, vt_ref, ktf_ref, vtf_ref = out_refs[2:]
        kb_ref[...] = k.astype(BF16)
        qt = (lax.dot_general(wqv_ref[0], hb, nt, preferred_element_type=F32) * q_scale).astype(BF16)
        ktf_ref[0] = lax.dot_general(wqv_ref[1], hb, nt, preferred_element_type=F32)
        vt = lax.dot_general(wqv_ref[2], hb, nt, preferred_element_type=F32)
        vtf_ref[0] = vt
        vt = vt.astype(BF16)
        for c in range(rows // ATTN_TILE):
            qt_ref[c] = qt[:, c * ATTN_TILE:(c + 1) * ATTN_TILE]
            vt_ref[c] = vt[:, c * ATTN_TILE:(c + 1) * ATTN_TILE]
    else:
        k_ref, v_ref, q_ref = out_refs[2:]
        k_ref[...] = k
        v_ref[...] = proj[:, D_SSM + D_ATTN:D_SSM + 2 * D_ATTN]
        q_ref[...] = lax.dot_general(hb, wqv_ref[0], nt, preferred_element_type=F32) * q_scale


def _inproj(x3, sc, sh, g, w_main, w_qv, b_f, nbb, rb, transposed, q_scale):
    nb, r, d = x3.shape
    t = nb * r
    rows = nbb * rb
    nj = r // rb
    grid = (nb // nbb, nj)
    row_map = lambda i, j: (i * nj + j, 0)
    n_main = w_main.shape[1]
    outs = [jax.ShapeDtypeStruct((t, D_SSM), F32), jax.ShapeDtypeStruct((t, LANES), F32)]
    out_specs = [pl.BlockSpec((rows, D_SSM), row_map), pl.BlockSpec((rows, LANES), row_map)]
    if transposed:
        assert nbb == 1, "transposed outputs are laid out per sequence"
        nc = rows // ATTN_TILE
        chunk_map = lambda i, j: (i * nj + j, 0, 0)
        seq_map = lambda i, j: (i, 0, j)
        outs += [jax.ShapeDtypeStruct((t, D_ATTN), BF16),
                 jax.ShapeDtypeStruct((t // ATTN_TILE, D_ATTN, ATTN_TILE), BF16),
                 jax.ShapeDtypeStruct((t // ATTN_TILE, D_ATTN, ATTN_TILE), BF16),
                 jax.ShapeDtypeStruct((nb, D_ATTN, r), F32),
                 jax.ShapeDtypeStruct((nb, D_ATTN, r), F32)]
        out_specs += [pl.BlockSpec((rows, D_ATTN), row_map),
                      pl.BlockSpec((nc, D_ATTN, ATTN_TILE), chunk_map),
                      pl.BlockSpec((nc, D_ATTN, ATTN_TILE), chunk_map),
                      pl.BlockSpec((1, D_ATTN, rows), seq_map),
                      pl.BlockSpec((1, D_ATTN, rows), seq_map)]
    else:
        outs += [jax.ShapeDtypeStruct((t, D_ATTN), F32)] * 3
        out_specs += [pl.BlockSpec((rows, D_ATTN), row_map)] * 3
    return pl.pallas_call(
        functools.partial(_inproj_kernel, transposed=transposed, q_scale=q_scale),
        grid=grid,
        in_specs=[pl.BlockSpec((nbb, rb, d), lambda i, j: (i, j, 0)),
                  pl.BlockSpec((nbb, 1, d), lambda i, j: (i, 0, 0)),
                  pl.BlockSpec((nbb, 1, d), lambda i, j: (i, 0, 0)),
                  pl.BlockSpec((1, 1, d), lambda i, j: (0, 0, 0)),
                  pl.BlockSpec((d, n_main), lambda i, j: (0, 0)),
                  pl.BlockSpec((3, D_ATTN, d), lambda i, j: (0, 0, 0)),
                  pl.BlockSpec((1, LANES), lambda i, j: (0, 0))],
        out_specs=out_specs,
        out_shape=outs,
        compiler_params=_cparams(("arbitrary", "arbitrary")),
    )(x3, sc, sh, g, w_main, w_qv, b_f)


def _crep_kernel(lf_ref, tri_ref, o_ref, carry_ref, *, scale):
    j = pl.program_id(1)

    @pl.when(j == 0)
    def _():
        carry_ref[...] = jnp.zeros_like(carry_ref)

    tri = tri_ref[...]
    cs = sum(jnp.dot(tri, p, preferred_element_type=F32) for p in _split3(lf_ref[...]))
    cs = cs + carry_ref[...]
    rows = cs.shape[0]
    carry_ref[...] = cs[rows - 1:rows, :]
    for h in range(N_HEADS):
        o_ref[0, h] = jnp.broadcast_to(cs[:, h:h + 1], (rows, LANES)) * (-scale)


def _crep(lf128, nb, length, scale):
    rows = min(length, 2 * ATTN_TILE)
    nj = length // rows
    tri = np.tril(np.ones((rows, rows), np.float32))
    return pl.pallas_call(
        functools.partial(_crep_kernel, scale=scale),
        grid=(nb, nj),
        in_specs=[pl.BlockSpec((rows, LANES), lambda i, j: (i * nj + j, 0)),
                  pl.BlockSpec((rows, rows), lambda i, j: (0, 0))],
        out_specs=pl.BlockSpec((1, N_HEADS, rows, LANES), lambda i, j: (i, 0, j, 0)),
        out_shape=jax.ShapeDtypeStruct((nb, N_HEADS, length, LANES), F32),
        scratch_shapes=[pltpu.VMEM((1, LANES), F32)],
        compiler_params=_cparams(("arbitrary", "arbitrary")),
    )(lf128, jnp.asarray(tri, BF16))


def _cumsum_kernel(x_ref, tri_ref, lt_ref, o_ref):
    tri = tri_ref[...]
    lt = lt_ref[...]
    cs = sum(jnp.dot(p, tri, preferred_element_type=F32) for p in _split3(x_ref[...]))
    tot = jnp.broadcast_to(cs[:, LANES - 1:LANES], cs.shape)
    carry = sum(jnp.dot(lt, p, preferred_element_type=F32) for p in _split3(tot))
    o_ref[...] = -(cs + carry)


def _neg_cumsum(x2, gsz, rb):
    r = x2.shape[0]
    tri = np.triu(np.ones((LANES, LANES), np.float32))
    ii = np.arange(rb)
    lt = ((ii[:, None] // gsz == ii[None, :] // gsz) & (ii[None, :] < ii[:, None])).astype(np.float32)
    return pl.pallas_call(
        _cumsum_kernel,
        grid=(r // rb,),
        in_specs=[pl.BlockSpec((rb, LANES), lambda i: (i, 0)),
                  pl.BlockSpec((LANES, LANES), lambda i: (0, 0)),
                  pl.BlockSpec((rb, rb), lambda i: (0, 0))],
        out_specs=pl.BlockSpec((rb, LANES), lambda i: (i, 0)),
        out_shape=jax.ShapeDtypeStruct((r, LANES), F32),
        compiler_params=_cparams(("arbitrary",)),
    )(x2, jnp.asarray(tri, BF16), jnp.asarray(lt, BF16))


def _gelu_tanh(x):
    return 0.5 * x * (1.0 + jnp.tanh(math.sqrt(2.0 / math.pi) * (x + 0.044715 * (x * x * x))))


def _s5_kernel(u_ref, h0_ref, wb_ref, wc_ref, are_ref, aim_ref, dsk_ref, wglu_ref, bglu_ref, gout_ref,
               y_ref, ht_ref, s_ref, hc_ref, *, nseq, tm):
    rows = nseq * tm
    sr = rows + SUBLANES
    ti = pl.program_id(1)

    @pl.when(ti == 0)
    def _():
        hc_ref[...] = h0_ref[...]

    u = u_ref[...].reshape(rows, D_SSM)
    ub = u.astype(BF16)
    for c in range(4):
        bu = jnp.dot(ub[:, c * LANES:(c + 1) * LANES], wb_ref[c], preferred_element_type=F32)
        for jj in range(4):
            s_ref[pl.ds((4 * c + jj) * sr, rows), :] = bu[:, jj * LANES:(jj + 1) * LANES]
            s_ref[pl.ds((16 + 4 * c + jj) * sr, rows), :] = bu[:, 512 + jj * LANES:512 + (jj + 1) * LANES]

    ar = (are_ref[0:8, :], are_ref[8:16, :])
    ai = (aim_ref[0:8, :], aim_ref[8:16, :])

    def seq_group(sg, carry):
        base = sg * 4
        hs = []
        for b in range(4):
            hs.append(tuple(hc_ref[base + b, pl.ds(8 * q, 8), :] for q in range(4)))

        def step(t, hs):
            new = []
            for b in range(4):
                row = (base + b) * tm + t
                hr0, hr1, hi0, hi1 = hs[b]
                bre0 = s_ref[pl.ds(row, 8, stride=sr), :]
                bre1 = s_ref[pl.ds(8 * sr + row, 8, stride=sr), :]
                bim0 = s_ref[pl.ds(16 * sr + row, 8, stride=sr), :]
                bim1 = s_ref[pl.ds(24 * sr + row, 8, stride=sr), :]
                nr0 = ar[0] * hr0 - ai[0] * hi0 + bre0
                nr1 = ar[1] * hr1 - ai[1] * hi1 + bre1
                ni0 = ar[0] * hi0 + ai[0] * hr0 + bim0
                ni1 = ar[1] * hi1 + ai[1] * hr1 + bim1
                s_ref[pl.ds(row, 8, stride=sr), :] = nr0
                s_ref[pl.ds(8 * sr + row, 8, stride=sr), :] = nr1
                s_ref[pl.ds(16 * sr + row, 8, stride=sr), :] = ni0
                s_ref[pl.ds(24 * sr + row, 8, stride=sr), :] = ni1
                new.append((nr0, nr1, ni0, ni1))
            return tuple(new)

        hs = lax.fori_loop(0, tm, step, tuple(hs), unroll=2)
        for b in range(4):
            for q in range(4):
                hc_ref[base + b, pl.ds(8 * q, 8), :] = hs[b][q]
        return carry

    lax.fori_loop(0, nseq // 4, seq_group, 0)
    ht_ref[...] = hc_ref[...]

    ys = []
    for c in range(4):
        blocks = [s_ref[pl.ds((4 * c + jj) * sr, rows), :].astype(BF16) for jj in range(4)]
        blocks += [s_ref[pl.ds((16 + 4 * c + jj) * sr, rows), :].astype(BF16) for jj in range(4)]
        hcat = jnp.concatenate(blocks, axis=1)
        ys.append(jnp.dot(hcat, wc_ref[c], preferred_element_type=F32))
    y = jnp.concatenate(ys, axis=1) + dsk_ref[...] * u
    y = _gelu_tanh(y)
    gate = jax.nn.sigmoid(jnp.dot(y.astype(BF16), wglu_ref[...], preferred_element_type=F32) + bglu_ref[...])
    y = y * gate
    y_ref[...] = _rms(y, gout_ref[...]).astype(BF16).reshape(y_ref.shape)


def _s5(u3, h0, wb, wc, a_re, a_im, dsk, wglu, bglu, gout, nseq, tm):
    nb, length, _ = u3.shape
    rows = nseq * tm
    sr = rows + SUBLANES
    grid = (nb // nseq, length // tm)
    const2 = lambda i, j: (0, 0)
    const3 = lambda i, j: (0, 0, 0)
    if tm == length:
        y_spec = pl.BlockSpec((rows, D_SSM), lambda i, j: (i, 0))
        y_shape = jax.ShapeDtypeStruct((nb * length, D_SSM), BF16)
    else:
        y_spec = pl.BlockSpec((nseq, tm, D_SSM), lambda i, j: (i, j, 0))
        y_shape = jax.ShapeDtypeStruct((nb, length, D_SSM), BF16)
    return pl.pallas_call(
        functools.partial(_s5_kernel, nseq=nseq, tm=tm),
        grid=grid,
        in_specs=[pl.BlockSpec((nseq, tm, D_SSM), lambda i, j: (i, j, 0)),
                  pl.BlockSpec((nseq, 32, LANES), lambda i, j: (i, 0, 0)),
                  pl.BlockSpec((4, LANES, 1024), const3),
                  pl.BlockSpec((4, 1024, LANES), const3),
                  pl.BlockSpec((16, LANES), const2),
                  pl.BlockSpec((16, LANES), const2),
                  pl.BlockSpec((1, D_SSM), const2),
                  pl.BlockSpec((D_SSM, D_SSM), const2),
                  pl.BlockSpec((1, D_SSM), const2),
                  pl.BlockSpec((1, D_SSM), const2)],
        out_specs=[y_spec, pl.BlockSpec((nseq, 32, LANES), lambda i, j: (i, 0, 0))],
        out_shape=[y_shape, jax.ShapeDtypeStruct((nb, 32, LANES), F32)],
        scratch_shapes=[pltpu.VMEM((32 * sr, LANES), F32), pltpu.VMEM((nseq, 32, LANES), F32)],
        compiler_params=_cparams(("arbitrary", "arbitrary")),
    )(u3, h0, wb, wc, a_re, a_im, dsk, wglu, bglu, gout)


def _s5_params(lam_re, lam_im, log_step, b_re, b_im, c_re, c_im):
    g = lam_re.shape[0]
    dt = jnp.exp(log_step)[:, None]
    mag = jnp.exp(lam_re * dt)
    a_re = mag * jnp.cos(lam_im * dt)
    a_im = mag * jnp.sin(lam_im * dt)
    den = lam_re * lam_re + lam_im * lam_im
    n_re = a_re - 1.0
    f_re = (n_re * lam_re + a_im * lam_im) / den
    f_im = (a_im * lam_re - n_re * lam_im) / den
    bb_re = f_re[..., None] * b_re - f_im[..., None] * b_im
    bb_im = f_re[..., None] * b_im + f_im[..., None] * b_re
    eye = jnp.eye(g, dtype=F32)
    n_state = g * SSM_STATE

    def in_mat(bb):
        return jnp.einsum('gpi,gh->gihp', bb, eye).reshape(g * SSM_GROUP, n_state)

    def out_mat(cc):
        return jnp.einsum('gip,gh->gphi', cc, eye).reshape(n_state, g * SSM_GROUP)

    wbr, wbi = in_mat(bb_re), in_mat(bb_im)
    wcr, wci = out_mat(c_re), out_mat(-c_im)
    wb = jnp.stack([jnp.concatenate([wbr[c * 128:(c + 1) * 128, c * 512:(c + 1) * 512],
                                     wbi[c * 128:(c + 1) * 128, c * 512:(c + 1) * 512]], axis=1)
                    for c in range(4)]).astype(BF16)
    wc = jnp.stack([jnp.concatenate([wcr[c * 512:(c + 1) * 512, c * 128:(c + 1) * 128],
                                     wci[c * 512:(c + 1) * 512, c * 128:(c + 1) * 128]], axis=0)
                    for c in range(4)]).astype(BF16)
    return wb, wc, a_re.reshape(16, LANES), a_im.reshape(16, LANES)


def _prompt_attention_step(qi, qt_ref, k_ref, vt_ref, cr_ref, g_ref, o_ref, qz_ref, m_ref, l_ref, acc_ref,
                           st_ref):
    tq = ATTN_TILE
    half = LANES // 2

    rowid = lax.broadcasted_iota(I32, (LANES, tq), 0)
    for j in range(N_HEADS // 2):
        qp = qt_ref[0, j * LANES:(j + 1) * LANES, :]
        qz_ref[2 * j] = jnp.where(rowid < half, qp, jnp.zeros_like(qp))
        qz_ref[2 * j + 1] = jnp.where(rowid >= half, qp, jnp.zeros_like(qp))
    m_ref[...] = jnp.full_like(m_ref, NEG_INF)
    l_ref[...] = jnp.zeros_like(l_ref)
    acc_ref[...] = jnp.zeros_like(acc_ref)

    key_row = lax.broadcasted_iota(I32, (tq, tq), 0)
    q_col = lax.broadcasted_iota(I32, (tq, tq), 1)
    causal = key_row <= q_col

    def tiles(kbs, masked):
        m_new = [m_ref[h:h + 1, :] for h in range(N_HEADS)]
        for c, kb in enumerate(kbs):
            ks = pl.multiple_of(kb * tq, tq)
            for j in range(N_HEADS // 2):
                kp = k_ref[0, pl.ds(ks, tq), j * LANES:(j + 1) * LANES]
                for e in range(2):
                    h = 2 * j + e
                    st = jnp.dot(kp, qz_ref[h], preferred_element_type=F32)
                    bias = cr_ref[0, h, pl.ds(ks, tq), :]
                    st = st + jnp.concatenate([bias] * (tq // LANES), axis=1)
                    if masked:
                        st = jnp.where(causal, st, NEG_INF)
                    st_ref[c, h] = st
                    m_new[h] = jnp.maximum(m_new[h], jnp.max(st, axis=0, keepdims=True))
        for h in range(N_HEADS):
            alpha = jnp.exp2(m_ref[h:h + 1, :] - m_new[h])
            rows = slice(h * HEAD_DIM, (h + 1) * HEAD_DIM)
            l_new = alpha * l_ref[h:h + 1, :]
            acc = alpha * acc_ref[rows, :]
            for c, kb in enumerate(kbs):
                p = jnp.exp2(st_ref[c, h] - m_new[h])
                l_new = l_new + jnp.sum(p, axis=0, keepdims=True)
                acc = acc + jnp.dot(vt_ref[kb, rows, :], p.astype(BF16), preferred_element_type=F32)
            l_ref[h:h + 1, :] = l_new
            m_ref[h:h + 1, :] = m_new[h]
            acc_ref[rows, :] = acc

    def body(g, c):
        tiles([g * ATTN_GROUP + i for i in range(ATTN_GROUP)], False)
        return c

    n_groups = qi // ATTN_GROUP
    lax.fori_loop(0, n_groups, body, 0)
    for r in range(1, ATTN_GROUP):
        @pl.when(qi - n_groups * ATTN_GROUP == r)
        def _(r=r):
            tiles([n_groups * ATTN_GROUP + i for i in range(r)], False)
    tiles([qi], True)

    for h in range(N_HEADS):
        rows = slice(h * HEAD_DIM, (h + 1) * HEAD_DIM)
        acc_ref[rows, :] = acc_ref[rows, :] / l_ref[h:h + 1, :]
    o = acc_ref[...].T
    o_ref[0] = _rms(o, g_ref[...]).astype(BF16)


def _paged_seq_attention(q8, kn, vn, cn, g, kpages, vpages, n_pages, n_new):
    n_past = n_pages * PAGE
    d = q8.shape[-1]
    nr = N_HEADS * n_new
    new_bits = n_new.bit_length() - 1
    head_bits = HEAD_DIM.bit_length() - 1
    rowh = lax.shift_right_logical(lax.broadcasted_iota(I32, (nr, d), 0), new_bits)
    colh = lax.shift_right_logical(lax.broadcasted_iota(I32, (nr, d), 1), head_bits)
    bd = rowh == colh
    qrep = jnp.broadcast_to(q8[None], (N_HEADS, n_new, d)).reshape(nr, d)
    qbd = jnp.where(bd, qrep, 0.0).astype(BF16)
    cnr = jnp.broadcast_to(cn[:, None, :], (N_HEADS, n_new, cn.shape[-1])).reshape(nr, cn.shape[-1])

    s_p = []
    for p in range(n_pages):
        kt = kpages[p].reshape(d, PAGE)
        s_p.append(_bdot(qbd, kt) + cnr[:, p * PAGE:(p + 1) * PAGE])
    s_n = _bdot_nt(qbd, kn) + cnr[:, n_past:n_past + n_new]
    qpos = lax.broadcasted_iota(I32, (nr, n_new), 0) & (n_new - 1)
    kpos = lax.broadcasted_iota(I32, (nr, n_new), 1)
    s_n = jnp.where(kpos <= qpos, s_n, NEG_INF)
    m = jnp.max(s_n, axis=1, keepdims=True)
    for sp in s_p:
        m = jnp.maximum(m, jnp.max(sp, axis=1, keepdims=True))
    p_n = jnp.exp(s_n - m)
    l = jnp.sum(p_n, axis=1, keepdims=True)
    of = _bdot(p_n, vn)
    for p in range(n_pages):
        pp = jnp.exp(s_p[p] - m)
        l = l + jnp.sum(pp, axis=1, keepdims=True)
        of = of + _bdot_nt(pp, vpages[p].reshape(d, PAGE))
    of = jnp.where(bd, of / l, 0.0)
    o = jnp.sum(of.reshape(N_HEADS, n_new, d), axis=0)
    return _rms(o, g)


def _attention_kernel(pt_ref, qt_ref, k_ref, vt_ref, cr_ref, g_ref, qs_ref, kn_ref, vn_ref, cn_ref, kc_hbm, vc_hbm,
                      o_ref, os_ref, qz_ref, m_ref, l_ref, acc_ref, st_ref, kbuf, vbuf, sem, *, n_pages, n_seq):
    step = pl.program_id(0) * pl.num_programs(1) + pl.program_id(1)
    per_step = qs_ref.shape[0]
    n_new = qs_ref.shape[1]

    def page_copies(seq, slot):
        cps = []
        for p in range(n_pages):
            pg = pt_ref[seq * n_pages + p]
            cps.append(pltpu.make_async_copy(kc_hbm.at[pg], kbuf.at[slot, p], sem.at[0, slot]))
            cps.append(pltpu.make_async_copy(vc_hbm.at[pg], vbuf.at[slot, p], sem.at[1, slot]))
        return cps

    @pl.when(step == 0)
    def _():
        for slot in range(2):
            for cp in page_copies(slot, slot):
                cp.start()

    def sample_seq(j):
        seq = step * per_step + j
        slot = j % 2
        for cp in page_copies(seq, slot):
            cp.wait()
        os_ref[j] = _paged_seq_attention(qs_ref[j], kn_ref[j], vn_ref[j], cn_ref[j], g_ref[...],
                                         kbuf.at[slot], vbuf.at[slot], n_pages, n_new)

        @pl.when(seq + 2 < n_seq)
        def _():
            for cp in page_copies(seq + 2, slot):
                cp.start()

    for j in range(per_step // 2):
        sample_seq(j)
    _prompt_attention_step(pl.program_id(1), qt_ref, k_ref, vt_ref, cr_ref, g_ref, o_ref,
                           qz_ref, m_ref, l_ref, acc_ref, st_ref)
    for j in range(per_step // 2, per_step):
        sample_seq(j)


def _attention(qt, k, vt, crep, g, page_table, qs, kn, vn, cn, cache_k, cache_v):
    b, length, d = k.shape
    tq = ATTN_TILE
    nq = length // tq
    n_seq, n_new, _ = qs.shape
    assert n_new & (n_new - 1) == 0, "new-token count must be a power of two"
    per_step = n_seq // (b * nq)
    assert per_step % 2 == 0 and per_step * b * nq == n_seq
    n_pages = page_table.shape[1]
    once = pl.Buffered(1)
    smap = lambda i, j, pt: (i * nq + j, 0, 0)
    grid_spec = pltpu.PrefetchScalarGridSpec(
        num_scalar_prefetch=1,
        grid=(b, nq),
        in_specs=[pl.BlockSpec((1, d, tq), smap),
                  pl.BlockSpec((1, length, d), lambda i, j, pt: (i, 0, 0), pipeline_mode=once),
                  pl.BlockSpec((nq, d, tq), lambda i, j, pt: (i, 0, 0), pipeline_mode=once),
                  pl.BlockSpec((1, N_HEADS, length, LANES), lambda i, j, pt: (i, 0, 0, 0), pipeline_mode=once),
                  pl.BlockSpec((1, d), lambda i, j, pt: (0, 0)),
                  pl.BlockSpec((per_step, n_new, d), smap),
                  pl.BlockSpec((per_step, n_new, d), smap),
                  pl.BlockSpec((per_step, n_new, d), smap),
                  pl.BlockSpec((per_step, N_HEADS, cn.shape[-1]), smap),
                  pl.BlockSpec(memory_space=pl.ANY),
                  pl.BlockSpec(memory_space=pl.ANY)],
        out_specs=[pl.BlockSpec((1, tq, d), lambda i, j, pt: (i, j, 0)),
                   pl.BlockSpec((per_step, n_new, d), smap)],
        scratch_shapes=[pltpu.VMEM((N_HEADS, LANES, tq), BF16), pltpu.VMEM((N_HEADS, tq), F32),
                        pltpu.VMEM((N_HEADS, tq), F32), pltpu.VMEM((d, tq), F32),
                        pltpu.VMEM((ATTN_GROUP, N_HEADS, tq, tq), F32),
                        pltpu.VMEM((2, n_pages, N_HEADS, HEAD_DIM, PAGE), F32),
                        pltpu.VMEM((2, n_pages, N_HEADS, HEAD_DIM, PAGE), F32),
                        pltpu.SemaphoreType.DMA((2, 2))],
    )
    return pl.pallas_call(
        functools.partial(_attention_kernel, n_pages=n_pages, n_seq=n_seq),
        grid_spec=grid_spec,
        out_shape=[jax.ShapeDtypeStruct((b, length, d), BF16), jax.ShapeDtypeStruct((n_seq, n_new, d), F32)],
        compiler_params=pltpu.CompilerParams(dimension_semantics=("arbitrary", "arbitrary"),
                                             vmem_limit_bytes=ATTN_VMEM_LIMIT),
    )(page_table.reshape(-1), qt, k, vt, crep, g, qs, kn, vn, cn, cache_k, cache_v)


def _outproj_kernel(x_ref, ys_ref, oa_ref, gt_ref, sc_ref, sh_ref, g_ref, wo_ref, wrt_ref,
                    x1_ref, h2_ref, lg_ref):
    nbb, rb, d = x_ref.shape
    mix = jnp.dot(ys_ref[...], wo_ref[0:D_SSM, :], preferred_element_type=F32)
    mix = mix + jnp.dot(oa_ref[...], wo_ref[D_SSM:, :], preferred_element_type=F32)
    x1 = x_ref[...] + gt_ref[...] * mix.reshape(nbb, rb, d)
    x1_ref[...] = x1
    h2 = (_rms(x1, g_ref[...]) * (1.0 + sc_ref[...]) + sh_ref[...]).reshape(nbb * rb, d)
    hb = h2.astype(BF16)
    h2_ref[...] = _pack_bf16_pairs(hb)
    hlo = (h2 - hb.astype(F32)).astype(BF16)
    whi = wrt_ref[0]
    wlo = wrt_ref[1]
    nt = (((1,), (1,)), ((), ()))
    lg = lax.dot_general(whi, hb, nt, preferred_element_type=F32)
    lg = lg + lax.dot_general(wlo, hb, nt, preferred_element_type=F32)
    lg = lg + lax.dot_general(whi, hlo, nt, preferred_element_type=F32)
    lg_ref[...] = lg


def _outproj(x3, ys, oa, gt, sc, sh, g, wo, wrt, nbb, rb):
    nb, r, d = x3.shape
    t = nb * r
    rows = nbb * rb
    nj = r // rb
    row_map = lambda i, j: (i * nj + j, 0)
    mod_spec = pl.BlockSpec((nbb, 1, d), lambda i, j: (i, 0, 0))
    return pl.pallas_call(
        _outproj_kernel,
        grid=(nb // nbb, nj),
        in_specs=[pl.BlockSpec((nbb, rb, d), lambda i, j: (i, j, 0)),
                  pl.BlockSpec((rows, D_SSM), row_map),
                  pl.BlockSpec((rows, D_ATTN), row_map),
                  mod_spec, mod_spec, mod_spec,
                  pl.BlockSpec((1, 1, d), lambda i, j: (0, 0, 0)),
                  pl.BlockSpec((D_SSM + D_ATTN, d), lambda i, j: (0, 0)),
                  pl.BlockSpec((2, N_EXPERTS, d), lambda i, j: (0, 0, 0))],
        out_specs=[pl.BlockSpec((nbb, rb, d), lambda i, j: (i, j, 0)),
                   pl.BlockSpec((rows, d // 2), row_map),
                   pl.BlockSpec((N_EXPERTS, rows), lambda i, j: (0, i * nj + j))],
        out_shape=[jax.ShapeDtypeStruct((nb, r, d), F32),
                   jax.ShapeDtypeStruct((t, d // 2), jnp.uint32),
                   jax.ShapeDtypeStruct((N_EXPERTS, t), F32)],
        compiler_params=_cparams(("arbitrary", "arbitrary")),
    )(x3, ys, oa, gt, sc, sh, g, wo, wrt)


def _router_kernel(lg_ref, rb_ref, ut_ref, idx_ref, w_ref, rank_ref, cnt_ref, carry_ref):
    i = pl.program_id(0)
    tm = lg_ref.shape[1]
    per_group = N_EXPERTS // N_EXPERT_GROUPS

    @pl.when(i == 0)
    def _():
        carry_ref[...] = jnp.zeros_like(carry_ref)

    scores = jax.nn.sigmoid(lg_ref[...])
    biased = scores + rb_ref[...]
    blks, grp = [], []
    for g in range(N_EXPERT_GROUPS):
        blk = biased[g * per_group:(g + 1) * per_group, :]
        m1 = jnp.max(blk, axis=0, keepdims=True)
        eq = blk == m1
        n_eq = jnp.sum(jnp.where(eq, 1.0, 0.0), axis=0, keepdims=True)
        m2 = jnp.max(jnp.where(eq, NEG_INF, blk), axis=0, keepdims=True)
        blks.append(blk)
        grp.append(m1 + jnp.where(n_eq >= 2.0, m1, m2))
    masked = []
    for g in range(N_EXPERT_GROUPS):
        beaten = jnp.zeros((1, tm), F32)
        for o in range(N_EXPERT_GROUPS):
            if o == g:
                continue
            ahead = (grp[o] >= grp[g]) if o < g else (grp[o] > grp[g])
            beaten = beaten + jnp.where(ahead, 1.0, 0.0)
        masked.append(jnp.where(beaten < float(TOPK_GROUPS), blks[g], NEG_INF))
    work = jnp.concatenate(masked, axis=0)

    eid = lax.broadcasted_iota(I32, (N_EXPERTS, tm), 0)
    chosen = jnp.zeros((N_EXPERTS, tm), F32)
    idxs, ws, sels = [], [], []
    for _ in range(TOP_K):
        m = jnp.max(work, axis=0, keepdims=True)
        first = jnp.min(jnp.where(work == m, eid, N_EXPERTS), axis=0, keepdims=True)
        sel = eid == first
        idxs.append(first)
        ws.append(jnp.sum(jnp.where(sel, scores, 0.0), axis=0, keepdims=True))
        sels.append(sel)
        chosen = jnp.where(sel, 1.0, chosen)
        work = jnp.where(sel, NEG_INF, work)
    wsum = ws[0]
    for wk in ws[1:]:
        wsum = wsum + wk

    prefix = jnp.dot(chosen.astype(BF16), ut_ref[...], preferred_element_type=F32) + carry_ref[...]
    carry_ref[...] = carry_ref[...] + jnp.sum(chosen, axis=1, keepdims=True)
    cnt_ref[...] = carry_ref[...]

    idx_ref[...] = jnp.zeros_like(idx_ref)
    w_ref[...] = jnp.zeros_like(w_ref)
    rank_ref[...] = jnp.zeros_like(rank_ref)
    for k in range(TOP_K):
        idx_ref[k:k + 1, :] = idxs[k]
        w_ref[k:k + 1, :] = ws[k] / wsum * ROUTED_SCALE
        rank = jnp.sum(jnp.where(sels[k], prefix, 0.0), axis=0, keepdims=True)
        rank_ref[k:k + 1, :] = rank.astype(I32)


def _router(lg, router_bias, tm):
    e, t = lg.shape
    ut = np.triu(np.ones((tm, tm), np.float32), 1)
    tok_spec = pl.BlockSpec((SUBLANES, tm), lambda i: (0, i))
    return pl.pallas_call(
        _router_kernel,
        grid=(t // tm,),
        in_specs=[pl.BlockSpec((e, tm), lambda i: (0, i)),
                  pl.BlockSpec((e, 1), lambda i: (0, 0)),
                  pl.BlockSpec((tm, tm), lambda i: (0, 0))],
        out_specs=[tok_spec, tok_spec, tok_spec, pl.BlockSpec((e, 1), lambda i: (0, 0))],
        out_shape=[jax.ShapeDtypeStruct((SUBLANES, t), I32), jax.ShapeDtypeStruct((SUBLANES, t), F32),
                   jax.ShapeDtypeStruct((SUBLANES, t), I32), jax.ShapeDtypeStruct((e, 1), F32)],
        scratch_shapes=[pltpu.VMEM((e, 1), F32)],
        compiler_params=_cparams(("arbitrary",)),
    )(lg, router_bias.reshape(e, 1), jnp.asarray(ut, BF16))


def _experts_kernel(be_ref, nv_ref, new_ref, fill_ref, x_ref, wg_ref, wu_ref, wd_ref, o_ref, wgu_s, wd_s):
    i = pl.program_id(0)
    de = wg_ref.shape[-1]
    valid = i < nv_ref[0]

    @pl.when(valid & (new_ref[i] == 1))
    def _():
        wgu_s[:, :de] = wg_ref[0].astype(BF16)
        wgu_s[:, de:] = wu_ref[0].astype(BF16)
        wd_s[...] = wd_ref[0].astype(BF16)

    @pl.when(valid)
    def _():
        row = lax.broadcasted_iota(I32, x_ref.shape, 0)
        x = jnp.where(row < fill_ref[i], x_ref[...], jnp.uint32(0))
        gu = jnp.dot(_unpack_bf16_pairs(x), wgu_s[...], preferred_element_type=F32)
        g = gu[:, :de]
        a = (g * jax.nn.sigmoid(g) * gu[:, de:]).astype(BF16)
        o = jnp.dot(a, wd_s[...], preferred_element_type=F32)
        o_ref[...] = _pack_bf16_pairs(o.astype(BF16))

    @pl.when(jnp.logical_not(valid))
    def _():
        o_ref[...] = jnp.zeros_like(o_ref)


def _experts(block_e, n_valid, block_new, block_fill, xs, w_eg, w_eu, w_ed, m):
    n_rows = xs.shape[0]
    d, de = w_eg.shape[-2:]
    nblk = n_rows // m
    xmap = lambda i, be, nv, *_: (jnp.minimum(i, nv[0] - 1), 0)
    wmap = lambda i, be, *_: (be[i], 0, 0)
    grid_spec = pltpu.PrefetchScalarGridSpec(
        num_scalar_prefetch=4,
        grid=(nblk,),
        in_specs=[pl.BlockSpec((m, xs.shape[1]), xmap),
                  pl.BlockSpec((1, d, de), wmap),
                  pl.BlockSpec((1, d, de), wmap),
                  pl.BlockSpec((1, de, d), wmap)],
        out_specs=pl.BlockSpec((m, d // 2), lambda i, *_: (i, 0)),
        scratch_shapes=[pltpu.VMEM((d, 2 * de), BF16), pltpu.VMEM((de, d), BF16)],
    )
    return pl.pallas_call(
        _experts_kernel,
        grid_spec=grid_spec,
        out_shape=jax.ShapeDtypeStruct((n_rows, d // 2), jnp.uint32),
        compiler_params=_cparams(("arbitrary",)),
    )(block_e, n_valid, block_new, block_fill, xs, w_eg, w_eu, w_ed)


def _gather_rows(table, idx, window):
    n = idx.shape[0]
    d = table.shape[1]
    info = plsc.get_sparse_core_info()
    n_workers = info.num_cores * info.num_subcores
    per_worker = n // n_workers
    n_buf = SC_GATHER_BUFFERS
    n_rounds = per_worker // (window * n_buf)
    assert n_rounds * window * n_buf * n_workers == n
    mesh = plsc.VectorSubcoreMesh(core_axis_name="c", subcore_axis_name="s")

    @functools.partial(
        pl.kernel, out_type=jax.ShapeDtypeStruct((n, d), table.dtype), mesh=mesh,
        scratch_types=[pltpu.VMEM((per_worker,), I32), pltpu.VMEM((n_buf, window, d), table.dtype),
                       pltpu.SemaphoreType.DMA((n_buf,)), pltpu.SemaphoreType.DMA((n_buf,))])
    def gather(x_hbm, i_hbm, o_hbm, i_v, buf, gsem, wsem):
        base = (lax.axis_index("s") * info.num_cores + lax.axis_index("c")) * per_worker
        pltpu.sync_copy(i_hbm.at[pl.ds(base, per_worker)], i_v)

        @pl.loop(0, n_rounds)
        def _(rnd):
            first = rnd * (window * n_buf)
            reads, writes = [], []
            for b in range(n_buf):
                rows = pl.ds(first + b * window, window)
                reads.append(pltpu.make_async_copy(x_hbm.at[i_v.at[rows]], buf.at[b], gsem.at[b]))
                writes.append(pltpu.make_async_copy(buf.at[b], o_hbm.at[pl.ds(base + first + b * window, window)],
                                                    wsem.at[b]))
            for cp in reads:
                cp.start()
            for b in range(n_buf):
                reads[b].wait()
                writes[b].start()
            for cp in writes:
                cp.wait()

    return gather(table, idx)


def _scatter_rows(groups, dest_t, n_rows, window):
    d = groups[0].shape[1]
    dtype = groups[0].dtype
    info = plsc.get_sparse_core_info()
    n_workers = info.num_cores * info.num_subcores
    n_buf = 2
    plan, idx_parts, first = [], [], 0
    row0 = 0
    for x in groups:
        t = x.shape[0]
        per_worker = t // n_workers
        chunks = per_worker // window
        assert chunks % n_buf == 0 and chunks * window * n_workers == t
        idx = dest_t[:TOP_K, first:first + t].reshape(TOP_K, n_workers, chunks, window).transpose(1, 0, 2, 3)
        idx_parts.append(idx.reshape(n_workers, TOP_K * chunks, window))
        plan.append((per_worker, chunks, row0))
        row0 += TOP_K * chunks
        first += t
    idx_rows = -(-row0 // SUBLANES) * SUBLANES
    idx = jnp.concatenate(idx_parts + [jnp.zeros((n_workers, idx_rows - row0, window), I32)], axis=1)
    idx = idx.reshape(n_workers * idx_rows, window)
    mesh = plsc.VectorSubcoreMesh(core_axis_name="c", subcore_axis_name="s")

    @functools.partial(
        pl.kernel, out_type=jax.ShapeDtypeStruct((n_rows, d), dtype), mesh=mesh,
        scratch_types=[pltpu.VMEM((idx_rows, window), I32), pltpu.VMEM((n_buf, window, d), dtype),
                       pltpu.SemaphoreType.DMA((n_buf,)), pltpu.SemaphoreType.DMA((n_buf,))])
    def scatter(*refs):
        x_hbms = refs[:len(groups)]
        i_hbm, o_hbm, i_v, buf, rsem, wsem = refs[len(groups):]
        wid = lax.axis_index("s") * info.num_cores + lax.axis_index("c")
        pltpu.sync_copy(i_hbm.at[pl.ds(wid * idx_rows, idx_rows)], i_v)

        for x_hbm, (per_worker, chunks, row0) in zip(x_hbms, plan):
            base = wid * per_worker

            @pl.loop(0, chunks // n_buf)
            def _(rnd, x_hbm=x_hbm, base=base, chunks=chunks, row0=row0):
                reads = [pltpu.make_async_copy(x_hbm.at[pl.ds(base + (rnd * n_buf + b) * window, window)],
                                               buf.at[b], rsem.at[b]) for b in range(n_buf)]
                for cp in reads:
                    cp.start()
                writes = []
                for b in range(n_buf):
                    reads[b].wait()
                    for k in range(TOP_K):
                        rows = i_v.at[row0 + k * chunks + rnd * n_buf + b]
                        cp = pltpu.make_async_copy(buf.at[b], o_hbm.at[rows], wsem.at[b])
                        cp.start()
                        writes.append(cp)
                for cp in writes:
                    cp.wait()

    return scatter(*groups, idx)


def _final_kernel(x1_ref, h2_ref, w_ref, gt_ref, sc_ref, sh_ref, g_ref, wgu_ref, wd_ref, *rest):
    yg_refs = rest[:TOP_K]
    y_ref = rest[TOP_K]
    nbb, rb, d = x1_ref.shape
    ds = wd_ref.shape[0]
    gu = jnp.dot(_unpack_bf16_pairs(h2_ref[...]), wgu_ref[...], preferred_element_type=F32)
    g = gu[:, :ds]
    a = (g * jax.nn.sigmoid(g) * gu[:, ds:]).astype(BF16)
    ff = jnp.dot(a, wd_ref[...], preferred_element_type=F32)
    routed_lo = routed_hi = None
    for k in range(TOP_K):
        yk = yg_refs[k][...]
        w = w_ref[:, k:k + 1]
        lo = lax.bitcast_convert_type(lax.shift_left(yk, jnp.uint32(16)), F32) * w
        hi = lax.bitcast_convert_type(yk & jnp.uint32(0xFFFF0000), F32) * w
        routed_lo = lo if k == 0 else routed_lo + lo
        routed_hi = hi if k == 0 else routed_hi + hi
    ff = jnp.concatenate([routed_lo, routed_hi], axis=1) + ff
    x2 = x1_ref[...] + gt_ref[...] * ff.reshape(nbb, rb, d)
    y_ref[...] = _rms(x2, g_ref[...]) * (1.0 + sc_ref[...]) + sh_ref[...]


def _final(x1, h2, wts, yg, first_tile, gt, sc, sh, g, wgu, wd, nbb, rb):
    nb, r, d = x1.shape
    rows = nbb * rb
    nj = r // rb
    tiles_all = yg.shape[0] // TOP_K // rows
    row_map = lambda i, j: (i * nj + j, 0)
    mod_spec = pl.BlockSpec((nbb, 1, d), lambda i, j: (i, 0, 0))
    slab_specs = [pl.BlockSpec((rows, yg.shape[1]), lambda i, j, k=k: (k * tiles_all + first_tile + i * nj + j, 0))
                  for k in range(TOP_K)]
    return pl.pallas_call(
        _final_kernel,
        grid=(nb // nbb, nj),
        in_specs=[pl.BlockSpec((nbb, rb, d), lambda i, j: (i, j, 0)),
                  pl.BlockSpec((rows, h2.shape[1]), row_map),
                  pl.BlockSpec((rows, SUBLANES), row_map),
                  mod_spec, mod_spec, mod_spec,
                  pl.BlockSpec((1, 1, d), lambda i, j: (0, 0, 0)),
                  pl.BlockSpec(wgu.shape, lambda i, j: (0, 0)),
                  pl.BlockSpec(wd.shape, lambda i, j: (0, 0))] + slab_specs,
        out_specs=pl.BlockSpec((nbb, rb, d), lambda i, j: (i, j, 0)),
        out_shape=jax.ShapeDtypeStruct((nb, r, d), F32),
        compiler_params=_cparams(("arbitrary", "arbitrary")),
    )(x1, h2, wts, gt, sc, sh, g, wgu, wd, *([yg] * TOP_K))


def _moe(h2_groups, lg, router_bias, w_eg, w_eu, w_ed, tile):
    t = lg.shape[1]
    e = N_EXPERTS
    m = MOE_ROWS
    idx_t, w_t, rank_t, counts = _router(lg, router_bias, tile)
    counts = counts.reshape(e).astype(I32)
    padded = ((counts + m - 1) // m) * m
    pad_end = jnp.cumsum(padded)
    pad_start = pad_end - padded
    onehot = idx_t[:, :, None] == jnp.arange(e, dtype=I32)
    dest_t = jnp.sum(jnp.where(onehot, pad_start, 0), axis=-1) + rank_t
    n_rows = (-(-(t * TOP_K) // m)) * m + e * m
    nblk = n_rows // m
    block_start = jnp.arange(nblk, dtype=I32) * m
    block_e = jnp.minimum(jnp.sum(pad_end[None, :] <= block_start[:, None], axis=1), e - 1).astype(I32)
    block_new = jnp.concatenate([jnp.ones((1,), I32), (block_e[1:] != block_e[:-1]).astype(I32)])
    n_valid = (pad_end[-1] // m).astype(I32).reshape(1)
    of_block = block_e[:, None] == jnp.arange(e, dtype=I32)
    count_b = jnp.sum(jnp.where(of_block, counts, 0), axis=1)
    start_b = jnp.sum(jnp.where(of_block, pad_start, 0), axis=1)
    block_fill = jnp.clip(count_b - (block_start - start_b), 0, m).astype(I32)
    xs = _scatter_rows(h2_groups, dest_t, n_rows, SC_SCATTER_WINDOW)
    yb = _experts(block_e, n_valid, block_new, block_fill, xs, w_eg, w_eu, w_ed, m)
    yg = _gather_rows(yb, dest_t[:TOP_K].reshape(-1), SC_GATHER_WINDOW)
    return w_t.T, yg


def kernel(x_prompt, x_sample, c_prompt, c_sample, cache_k, cache_v, cache_logf, state_ssm_re, state_ssm_im, page_table, w_ada, b_ada, g_norm1, w_in, b_fgate, ssm_lambda_re, ssm_lambda_im, ssm_log_step, ssm_b_re, ssm_b_im, ssm_c_re, ssm_c_im, ssm_d, w_glu, b_glu, g_ssm_out, g_attn_out, w_out, g_norm2, w_router, router_bias, w_exp_gate, w_exp_up, w_exp_down, w_sh_gate, w_sh_up, w_sh_down, g_final, w_ada_final, b_ada_final):
    depth = w_ada.shape[0]
    assert depth == 1, "one layer is supported"
    bp, lp, d = x_prompt.shape
    bs, ls, _ = x_sample.shape
    n_pages = page_table.shape[1]
    n_past = n_pages * PAGE
    n_groups = ssm_lambda_re.shape[1]

    n_c = bp + bs
    n_c_pad = -(-n_c // SUBLANES) * SUBLANES
    c_all = jnp.concatenate([c_prompt, c_sample, jnp.zeros((n_c_pad - n_c, d), F32)], axis=0)
    mod = _adaln(c_all, w_ada[0], b_ada[0])
    modf = _adaln(c_all, w_ada_final, b_ada_final)

    def mods(lo, hi):
        parts = [mod[lo:hi, k * d:(k + 1) * d][:, None, :] for k in range(6)]
        parts += [modf[lo:hi, k * d:(k + 1) * d][:, None, :] for k in range(2)]
        return parts

    w_u, w_q, w_k, w_v, w_f = jnp.split(w_in[0], [D_SSM, D_SSM + D_ATTN, D_SSM + 2 * D_ATTN,
                                                   D_SSM + 3 * D_ATTN], axis=1)
    w_fpad = jnp.concatenate([w_f, jnp.zeros((d, LANES - N_HEADS), F32)], axis=1)
    w_main_s = jnp.concatenate([w_u, w_k, w_v, w_fpad], axis=1).astype(BF16)
    w_main_p = jnp.concatenate([w_u, w_k, w_fpad], axis=1).astype(BF16)
    w_qkv = jnp.stack([w_q.T, w_k.T, w_v.T]).astype(BF16)
    b_f = jnp.concatenate([b_fgate[0], jnp.zeros((LANES - N_HEADS,), F32)]).reshape(1, LANES)
    g1 = g_norm1[0].reshape(1, 1, d)
    g2 = g_norm2[0].reshape(1, 1, d)
    gf = g_final.reshape(1, 1, d)
    wb, wc, a_re, a_im = _s5_params(ssm_lambda_re[0], ssm_lambda_im[0], ssm_log_step[0], ssm_b_re[0],
                                    ssm_b_im[0], ssm_c_re[0], ssm_c_im[0])
    dsk = ssm_d[0].reshape(1, D_SSM)
    wglu = w_glu[0].astype(BF16)
    bglu = b_glu[0].reshape(1, D_SSM)
    g_so = g_ssm_out[0].reshape(1, D_SSM)
    g_ao = g_attn_out[0].reshape(1, D_ATTN)
    wo = w_out[0].astype(BF16)
    wr_t = w_router[0].T
    wr_hi = wr_t.astype(BF16)
    wrt = jnp.stack([wr_hi, (wr_t - wr_hi.astype(F32)).astype(BF16)])
    wgu = jnp.concatenate([w_sh_gate[0], w_sh_up[0]], axis=1).astype(BF16)
    wsd = w_sh_down[0].astype(BF16)

    def ssm_state(re, im):
        return jnp.concatenate([re.reshape(-1, 16, LANES), im.reshape(-1, 16, LANES)], axis=1)

    def split_state(ht):
        n = ht.shape[0]
        return (ht[:, :16].reshape(1, n, n_groups, SSM_STATE), ht[:, 16:].reshape(1, n, n_groups, SSM_STATE))

    tm = 512
    sh1_p, sc1_p, gt1_p, sh2_p, sc2_p, gt2_p, shf_p, scf_p = mods(0, bp)
    u, lf, kb, qt, vt, ktf, vtf = _inproj(x_prompt, sc1_p, sh1_p, g1, w_main_p, w_qkv, b_f, 1, tm, True,
                                          HEAD_DIM ** -0.5 * LOG2E)
    crep = _crep(lf, bp, lp, LOG2E)
    ys_p, ht = _s5(u.reshape(bp, lp, D_SSM), jnp.zeros((bp, 32, LANES), F32), wb, wc, a_re, a_im, dsk,
                   wglu, bglu, g_so, bp, 256)
    k_prompt = ktf.reshape(bp, N_HEADS, HEAD_DIM, lp).transpose(0, 3, 1, 2)[None]
    v_prompt = vtf.reshape(bp, N_HEADS, HEAD_DIM, lp).transpose(0, 3, 1, 2)[None]
    logf_prompt = lf[:, :N_HEADS].reshape(1, bp, lp, N_HEADS)
    sre_p, sim_p = split_state(ht)

    nbb = 64
    sh1_s, sc1_s, gt1_s, sh2_s, sc2_s, gt2_s, shf_s, scf_s = mods(bp, bp + bs)
    u, lf, k, v, q = _inproj(x_sample, sc1_s, sh1_s, g1, w_main_s, w_qkv, b_f, nbb, ls, False, HEAD_DIM ** -0.5)
    ys_s, ht = _s5(u.reshape(bs, ls, D_SSM), ssm_state(state_ssm_re[0], state_ssm_im[0]), wb, wc, a_re, a_im,
                   dsk, wglu, bglu, g_so, 32, ls)
    lf_past = cache_logf[0][page_table].reshape(bs, n_past, N_HEADS).transpose(0, 2, 1)
    lf_new = lf[:, :N_HEADS].reshape(bs, ls, N_HEADS).transpose(0, 2, 1)
    n_key_pad = -(-(n_past + ls) // LANES) * LANES
    lf_all = jnp.concatenate([lf_past, lf_new, jnp.zeros((bs, N_HEADS, n_key_pad - n_past - ls), F32)], axis=2)
    gsz = n_key_pad // LANES
    cn_s = _neg_cumsum(lf_all.reshape(-1, LANES), gsz, gsz * 32).reshape(bs, N_HEADS, n_key_pad)

    oa_p, oa_s = _attention(qt, kb.reshape(bp, lp, D_ATTN), vt, crep, g_ao, page_table,
                            q.reshape(bs, ls, D_ATTN), k.reshape(bs, ls, D_ATTN), v.reshape(bs, ls, D_ATTN),
                            cn_s, cache_k[0].transpose(0, 2, 3, 1), cache_v[0].transpose(0, 2, 3, 1))
    x1_p, h2_p, lg_p = _outproj(x_prompt, ys_p.reshape(bp * lp, D_SSM), oa_p.reshape(bp * lp, D_ATTN),
                                gt1_p, sc2_p, sh2_p, g2, wo, wrt, 1, tm)
    x1_s, h2_s, lg_s = _outproj(x_sample, ys_s, oa_s.reshape(bs * ls, D_ATTN).astype(BF16),
                                gt1_s, sc2_s, sh2_s, g2, wo, wrt, nbb, ls)

    assert nbb * ls == tm
    wts, yg = _moe([h2_p, h2_s], jnp.concatenate([lg_p, lg_s], axis=1), router_bias[0],
                   w_exp_gate[0], w_exp_up[0], w_exp_down[0], tm)
    tiles_p = bp * lp // tm
    y_prompt = _final(x1_p, h2_p, wts[:bp * lp], yg, 0, gt2_p, scf_p, shf_p, gf, wgu, wsd, 1, tm)
    y_sample = _final(x1_s, h2_s, wts[bp * lp:], yg, tiles_p, gt2_s, scf_s, shf_s, gf, wgu, wsd, nbb, ls)
    k_sample = k.reshape(1, bs, ls, N_HEADS, HEAD_DIM)
    v_sample = v.reshape(1, bs, ls, N_HEADS, HEAD_DIM)
    logf_sample = lf[:, :N_HEADS].reshape(1, bs, ls, N_HEADS)
    sre_s, sim_s = split_state(ht)

    return (y_prompt, y_sample, k_prompt, v_prompt, logf_prompt, sre_p, sim_p,
            k_sample, v_sample, logf_sample, sre_s, sim_s)
```

```python
import functools
import math

import jax
import jax.numpy as jnp
import numpy as np
from jax import lax
from jax.experimental import pallas as pl
from jax.experimental.pallas import tpu as pltpu
from jax.experimental.pallas import tpu_sc as plsc

F32 = jnp.float32
BF16 = jnp.bfloat16
I32 = jnp.int32

EPS = 1e-6
HEAD_DIM = 64
N_HEADS = 8
D_SSM = 512
D_ATTN = 512
SSM_GROUP = 16
SSM_STATE = 64
N_EXPERTS = 64
TOP_K = 6
N_EXPERT_GROUPS = 8
TOPK_GROUPS = 4
ROUTED_SCALE = 2.5
PAGE = 128

LANES = 128
SUBLANES = 8
VMEM_LIMIT = 48 * 1024 * 1024
ATTN_VMEM_LIMIT = 58 * 1024 * 1024
MOE_ROWS = 512
ATTN_TILE = 256
ATTN_GROUP = 4
LOG2E = math.log2(math.e)
SC_SCATTER_WINDOW = 16
SC_GATHER_WINDOW = 48
SC_GATHER_BUFFERS = 4
NEG_INF = float("-inf")


def _cparams(sem):
    return pltpu.CompilerParams(dimension_semantics=sem, vmem_limit_bytes=VMEM_LIMIT)


def _bdot(a, b):
    return jnp.dot(a.astype(BF16), b.astype(BF16), preferred_element_type=F32)


def _bdot_nt(a, b):
    return lax.dot_general(a.astype(BF16), b.astype(BF16), (((1,), (1,)), ((), ())),
                           preferred_element_type=F32)


def _split3(v):
    hi = v.astype(BF16)
    r1 = v - hi.astype(F32)
    mid = r1.astype(BF16)
    lo = (r1 - mid.astype(F32)).astype(BF16)
    return hi, mid, lo


def _pack_bf16_pairs(xb):
    n = xb.shape[1] // 2
    bits = lax.bitcast_convert_type(xb.astype(F32), jnp.uint32)
    return lax.shift_right_logical(bits[:, :n], jnp.uint32(16)) | bits[:, n:]


def _unpack_bf16_pairs(xp):
    lo = lax.bitcast_convert_type(lax.shift_left(xp, jnp.uint32(16)), F32)
    hi = lax.bitcast_convert_type(xp & jnp.uint32(0xFFFF0000), F32)
    return jnp.concatenate([lo, hi], axis=1).astype(BF16)


def _rms(x, g):
    return x * lax.rsqrt(jnp.mean(x * x, axis=-1, keepdims=True) + EPS) * g


def _adaln_kernel(c_ref, w_ref, b_ref, o_ref):
    c = c_ref[...]
    s = c * jax.nn.sigmoid(c)
    o_ref[...] = _bdot(s, w_ref[...]) + b_ref[...]


def _adaln(c, w, b):
    m, k = c.shape
    n = w.shape[1]
    tn = 1024
    return pl.pallas_call(
        _adaln_kernel,
        grid=(n // tn,),
        in_specs=[pl.BlockSpec((m, k), lambda j: (0, 0)),
                  pl.BlockSpec((k, tn), lambda j: (0, j)),
                  pl.BlockSpec((1, tn), lambda j: (0, j))],
        out_specs=pl.BlockSpec((m, tn), lambda j: (0, j)),
        out_shape=jax.ShapeDtypeStruct((m, n), F32),
        compiler_params=_cparams(("arbitrary",)),
    )(c, w, b.reshape(1, n))


def _inproj_kernel(x_ref, sc_ref, sh_ref, g_ref, w_ref, wqv_ref, bf_ref, *out_refs, transposed, q_scale):
    nbb, rb, d = x_ref.shape
    rows = nbb * rb
    x = x_ref[...]
    h = _rms(x, g_ref[...]) * (1.0 + sc_ref[...]) + sh_ref[...]
    hb = h.reshape(rows, d).astype(BF16)
    proj = jnp.dot(hb, w_ref[...], preferred_element_type=F32)
    u_ref, lf_ref = out_refs[:2]
    u_ref[...] = proj[:, :D_SSM]
    k = proj[:, D_SSM:D_SSM + D_ATTN]
    z = proj[:, proj.shape[1] - LANES:] + bf_ref[...]
    lf = jnp.minimum(z, 0.0) - jnp.log1p(jnp.exp(-jnp.abs(z)))
    lf_ref[...] = lf
    nt = (((1,), (1,)), ((), ()))
    if transposed:
        kb_ref, qt_ref, vt_ref, ktf_ref, vtf_ref, lft_ref = out_refs[2:]
        lft_ref[0] = lf.T[:N_HEADS]
        kb_ref[...] = k.astype(BF16)
        qt = (lax.dot_general(wqv_ref[0], hb, nt, preferred_element_type=F32) * q_scale).astype(BF16)
        ktf_ref[0] = lax.dot_general(wqv_ref[1], hb, nt, preferred_element_type=F32)
        vt = lax.dot_general(wqv_ref[2], hb, nt, preferred_element_type=F32)
        vtf_ref[0] = vt
        vt = vt.astype(BF16)
        for c in range(rows // ATTN_TILE):
            qt_ref[c] = qt[:, c * ATTN_TILE:(c + 1) * ATTN_TILE]
            vt_ref[c] = vt[:, c * ATTN_TILE:(c + 1) * ATTN_TILE]
    else:
        k_ref, v_ref, q_ref = out_refs[2:]
        k_ref[...] = k
        v_ref[...] = proj[:, D_SSM + D_ATTN:D_SSM + 2 * D_ATTN]
        q_ref[...] = lax.dot_general(hb, wqv_ref[0], nt, preferred_element_type=F32) * q_scale


def _inproj(x3, sc, sh, g, w_main, w_qv, b_f, nbb, rb, transposed, q_scale):
    nb, r, d = x3.shape
    t = nb * r
    rows = nbb * rb
    nj = r // rb
    grid = (nb // nbb, nj)
    row_map = lambda i, j: (i * nj + j, 0)
    n_main = w_main.shape[1]
    outs = [jax.ShapeDtypeStruct((t, D_SSM), F32), jax.ShapeDtypeStruct((t, LANES), F32)]
    out_specs = [pl.BlockSpec((rows, D_SSM), row_map), pl.BlockSpec((rows, LANES), row_map)]
    if transposed:
        assert nbb == 1, "transposed outputs are laid out per sequence"
        nc = rows // ATTN_TILE
        chunk_map = lambda i, j: (i * nj + j, 0, 0)
        seq_map = lambda i, j: (i, 0, j)
        outs += [jax.ShapeDtypeStruct((t, D_ATTN), BF16),
                 jax.ShapeDtypeStruct((t // ATTN_TILE, D_ATTN, ATTN_TILE), BF16),
                 jax.ShapeDtypeStruct((t // ATTN_TILE, D_ATTN, ATTN_TILE), BF16),
                 jax.ShapeDtypeStruct((nb, D_ATTN, r), F32),
                 jax.ShapeDtypeStruct((nb, D_ATTN, r), F32),
                 jax.ShapeDtypeStruct((nb, N_HEADS, r), F32)]
        out_specs += [pl.BlockSpec((rows, D_ATTN), row_map),
                      pl.BlockSpec((nc, D_ATTN, ATTN_TILE), chunk_map),
                      pl.BlockSpec((nc, D_ATTN, ATTN_TILE), chunk_map),
                      pl.BlockSpec((1, D_ATTN, rows), seq_map),
                      pl.BlockSpec((1, D_ATTN, rows), seq_map),
                      pl.BlockSpec((1, N_HEADS, rows), seq_map)]
    else:
        outs += [jax.ShapeDtypeStruct((t, D_ATTN), F32)] * 3
        out_specs += [pl.BlockSpec((rows, D_ATTN), row_map)] * 3
    return pl.pallas_call(
        functools.partial(_inproj_kernel, transposed=transposed, q_scale=q_scale),
        grid=grid,
        in_specs=[pl.BlockSpec((nbb, rb, d), lambda i, j: (i, j, 0)),
                  pl.BlockSpec((nbb, 1, d), lambda i, j: (i, 0, 0)),
                  pl.BlockSpec((nbb, 1, d), lambda i, j: (i, 0, 0)),
                  pl.BlockSpec((1, 1, d), lambda i, j: (0, 0, 0)),
                  pl.BlockSpec((d, n_main), lambda i, j: (0, 0)),
                  pl.BlockSpec((3, D_ATTN, d), lambda i, j: (0, 0, 0)),
                  pl.BlockSpec((1, LANES), lambda i, j: (0, 0))],
        out_specs=out_specs,
        out_shape=outs,
        compiler_params=_cparams(("arbitrary", "arbitrary")),
    )(x3, sc, sh, g, w_main, w_qv, b_f)


def _crep_kernel(lf_ref, tri_ref, o_ref, carry_ref, *, scale):
    j = pl.program_id(1)

    @pl.when(j == 0)
    def _():
        carry_ref[...] = jnp.zeros_like(carry_ref)

    tri = tri_ref[...]
    cs = sum(jnp.dot(tri, p, preferred_element_type=F32) for p in _split3(lf_ref[...]))
    cs = cs + carry_ref[...]
    rows = cs.shape[0]
    carry_ref[...] = cs[rows - 1:rows, :]
    for h in range(N_HEADS):
        o_ref[0, h] = jnp.broadcast_to(cs[:, h:h + 1], (rows, LANES)) * (-scale)


def _crep(lf128, nb, length, scale):
    rows = min(length, 2 * ATTN_TILE)
    nj = length // rows
    tri = np.tril(np.ones((rows, rows), np.float32))
    return pl.pallas_call(
        functools.partial(_crep_kernel, scale=scale),
        grid=(nb, nj),
        in_specs=[pl.BlockSpec((rows, LANES), lambda i, j: (i * nj + j, 0)),
                  pl.BlockSpec((rows, rows), lambda i, j: (0, 0))],
        out_specs=pl.BlockSpec((1, N_HEADS, rows, LANES), lambda i, j: (i, 0, j, 0)),
        out_shape=jax.ShapeDtypeStruct((nb, N_HEADS, length, LANES), F32),
        scratch_shapes=[pltpu.VMEM((1, LANES), F32)],
        compiler_params=_cparams(("arbitrary", "arbitrary")),
    )(lf128, jnp.asarray(tri, BF16))


def _cumsum_kernel(x_ref, tri_ref, o_ref):
    tri = tri_ref[...]
    rb, n = x_ref.shape
    carry = jnp.zeros((rb, 1), F32)
    for c in range(n // LANES):
        lanes = slice(c * LANES, (c + 1) * LANES)
        cs = sum(jnp.dot(p, tri, preferred_element_type=F32) for p in _split3(x_ref[:, lanes])) + carry
        o_ref[:, lanes] = -cs
        carry = cs[:, LANES - 1:LANES]


def _neg_cumsum(x2, rb):
    r, n = x2.shape
    tri = np.triu(np.ones((LANES, LANES), np.float32))
    return pl.pallas_call(
        _cumsum_kernel,
        grid=(r // rb,),
        in_specs=[pl.BlockSpec((rb, n), lambda i: (i, 0)),
                  pl.BlockSpec((LANES, LANES), lambda i: (0, 0))],
        out_specs=pl.BlockSpec((rb, n), lambda i: (i, 0)),
        out_shape=jax.ShapeDtypeStruct((r, n), F32),
        compiler_params=_cparams(("arbitrary",)),
    )(x2, jnp.asarray(tri, BF16))


def _gelu_tanh(x):
    return 0.5 * x * (1.0 + jnp.tanh(math.sqrt(2.0 / math.pi) * (x + 0.044715 * (x * x * x))))


def _s5_kernel(u_ref, h0_ref, wb_ref, wc_ref, are_ref, aim_ref, dsk_ref, wglu_ref, bglu_ref, gout_ref,
               y_ref, ht_ref, s_ref, hc_ref, *, nseq, tm):
    rows = nseq * tm
    sr = rows + SUBLANES
    ti = pl.program_id(1)

    @pl.when(ti == 0)
    def _():
        hc_ref[...] = h0_ref[...]

    u = u_ref[...].reshape(rows, D_SSM)
    ub = u.astype(BF16)
    for c in range(4):
        bu = jnp.dot(ub[:, c * LANES:(c + 1) * LANES], wb_ref[c], preferred_element_type=F32)
        for jj in range(4):
            s_ref[pl.ds((4 * c + jj) * sr, rows), :] = bu[:, jj * LANES:(jj + 1) * LANES]
            s_ref[pl.ds((16 + 4 * c + jj) * sr, rows), :] = bu[:, 512 + jj * LANES:512 + (jj + 1) * LANES]

    ar = (are_ref[0:8, :], are_ref[8:16, :])
    ai = (aim_ref[0:8, :], aim_ref[8:16, :])

    def seq_group(sg, carry):
        base = sg * 4
        hs = []
        for b in range(4):
            hs.append(tuple(hc_ref[base + b, pl.ds(8 * q, 8), :] for q in range(4)))

        def step(t, hs):
            new = []
            for b in range(4):
                row = (base + b) * tm + t
                hr0, hr1, hi0, hi1 = hs[b]
                bre0 = s_ref[pl.ds(row, 8, stride=sr), :]
                bre1 = s_ref[pl.ds(8 * sr + row, 8, stride=sr), :]
                bim0 = s_ref[pl.ds(16 * sr + row, 8, stride=sr), :]
                bim1 = s_ref[pl.ds(24 * sr + row, 8, stride=sr), :]
                nr0 = ar[0] * hr0 - ai[0] * hi0 + bre0
                nr1 = ar[1] * hr1 - ai[1] * hi1 + bre1
                ni0 = ar[0] * hi0 + ai[0] * hr0 + bim0
                ni1 = ar[1] * hi1 + ai[1] * hr1 + bim1
                s_ref[pl.ds(row, 8, stride=sr), :] = nr0
                s_ref[pl.ds(8 * sr + row, 8, stride=sr), :] = nr1
                s_ref[pl.ds(16 * sr + row, 8, stride=sr), :] = ni0
                s_ref[pl.ds(24 * sr + row, 8, stride=sr), :] = ni1
                new.append((nr0, nr1, ni0, ni1))
            return tuple(new)

        hs = lax.fori_loop(0, tm, step, tuple(hs), unroll=2)
        for b in range(4):
            for q in range(4):
                hc_ref[base + b, pl.ds(8 * q, 8), :] = hs[b][q]
        return carry

    lax.fori_loop(0, nseq // 4, seq_group, 0)
    ht_ref[...] = hc_ref[...]

    ys = []
    for c in range(4):
        blocks = [s_ref[pl.ds((4 * c + jj) * sr, rows), :].astype(BF16) for jj in range(4)]
        blocks += [s_ref[pl.ds((16 + 4 * c + jj) * sr, rows), :].astype(BF16) for jj in range(4)]
        hcat = jnp.concatenate(blocks, axis=1)
        ys.append(jnp.dot(hcat, wc_ref[c], preferred_element_type=F32))
    y = jnp.concatenate(ys, axis=1) + dsk_ref[...] * u
    y = _gelu_tanh(y)
    gate = jax.nn.sigmoid(jnp.dot(y.astype(BF16), wglu_ref[...], preferred_element_type=F32) + bglu_ref[...])
    y = y * gate
    y_ref[...] = _rms(y, gout_ref[...]).astype(BF16).reshape(y_ref.shape)


def _s5(u3, h0, wb, wc, a_re, a_im, dsk, wglu, bglu, gout, nseq, tm):
    nb, length, _ = u3.shape
    rows = nseq * tm
    sr = rows + SUBLANES
    grid = (nb // nseq, length // tm)
    const2 = lambda i, j: (0, 0)
    const3 = lambda i, j: (0, 0, 0)
    if tm == length:
        y_spec = pl.BlockSpec((rows, D_SSM), lambda i, j: (i, 0))
        y_shape = jax.ShapeDtypeStruct((nb * length, D_SSM), BF16)
    else:
        y_spec = pl.BlockSpec((nseq, tm, D_SSM), lambda i, j: (i, j, 0))
        y_shape = jax.ShapeDtypeStruct((nb, length, D_SSM), BF16)
    return pl.pallas_call(
        functools.partial(_s5_kernel, nseq=nseq, tm=tm),
        grid=grid,
        in_specs=[pl.BlockSpec((nseq, tm, D_SSM), lambda i, j: (i, j, 0)),
                  pl.BlockSpec((nseq, 32, LANES), lambda i, j: (i, 0, 0)),
                  pl.BlockSpec((4, LANES, 1024), const3),
                  pl.BlockSpec((4, 1024, LANES), const3),
                  pl.BlockSpec((16, LANES), const2),
                  pl.BlockSpec((16, LANES), const2),
                  pl.BlockSpec((1, D_SSM), const2),
                  pl.BlockSpec((D_SSM, D_SSM), const2),
                  pl.BlockSpec((1, D_SSM), const2),
                  pl.BlockSpec((1, D_SSM), const2)],
        out_specs=[y_spec, pl.BlockSpec((nseq, 32, LANES), lambda i, j: (i, 0, 0))],
        out_shape=[y_shape, jax.ShapeDtypeStruct((nb, 32, LANES), F32)],
        scratch_shapes=[pltpu.VMEM((32 * sr, LANES), F32), pltpu.VMEM((nseq, 32, LANES), F32)],
        compiler_params=_cparams(("arbitrary", "arbitrary")),
    )(u3, h0, wb, wc, a_re, a_im, dsk, wglu, bglu, gout)


def _s5_params(lam_re, lam_im, log_step, b_re, b_im, c_re, c_im):
    g = lam_re.shape[0]
    dt = jnp.exp(log_step)[:, None]
    mag = jnp.exp(lam_re * dt)
    a_re = mag * jnp.cos(lam_im * dt)
    a_im = mag * jnp.sin(lam_im * dt)
    den = lam_re * lam_re + lam_im * lam_im
    n_re = a_re - 1.0
    f_re = (n_re * lam_re + a_im * lam_im) / den
    f_im = (a_im * lam_re - n_re * lam_im) / den
    bb_re = f_re[..., None] * b_re - f_im[..., None] * b_im
    bb_im = f_re[..., None] * b_im + f_im[..., None] * b_re
    eye = jnp.eye(g, dtype=F32)
    n_state = g * SSM_STATE

    def in_mat(bb):
        return jnp.einsum('gpi,gh->gihp', bb, eye).reshape(g * SSM_GROUP, n_state)

    def out_mat(cc):
        return jnp.einsum('gip,gh->gphi', cc, eye).reshape(n_state, g * SSM_GROUP)

    wbr, wbi = in_mat(bb_re), in_mat(bb_im)
    wcr, wci = out_mat(c_re), out_mat(-c_im)
    wb = jnp.stack([jnp.concatenate([wbr[c * 128:(c + 1) * 128, c * 512:(c + 1) * 512],
                                     wbi[c * 128:(c + 1) * 128, c * 512:(c + 1) * 512]], axis=1)
                    for c in range(4)]).astype(BF16)
    wc = jnp.stack([jnp.concatenate([wcr[c * 512:(c + 1) * 512, c * 128:(c + 1) * 128],
                                     wci[c * 512:(c + 1) * 512, c * 128:(c + 1) * 128]], axis=0)
                    for c in range(4)]).astype(BF16)
    return wb, wc, a_re.reshape(16, LANES), a_im.reshape(16, LANES)


def _prompt_attention_step(qi, qt_ref, k_ref, vt_ref, cr_ref, g_ref, o_ref, qz_ref, m_ref, l_ref, acc_ref,
                           st_ref):
    tq = ATTN_TILE
    half = LANES // 2

    rowid = lax.broadcasted_iota(I32, (LANES, tq), 0)
    for j in range(N_HEADS // 2):
        qp = qt_ref[0, j * LANES:(j + 1) * LANES, :]
        qz_ref[2 * j] = jnp.where(rowid < half, qp, jnp.zeros_like(qp))
        qz_ref[2 * j + 1] = jnp.where(rowid >= half, qp, jnp.zeros_like(qp))
    m_ref[...] = jnp.full_like(m_ref, NEG_INF)
    l_ref[...] = jnp.zeros_like(l_ref)
    acc_ref[...] = jnp.zeros_like(acc_ref)

    key_row = lax.broadcasted_iota(I32, (tq, tq), 0)
    q_col = lax.broadcasted_iota(I32, (tq, tq), 1)
    causal = key_row <= q_col

    def tiles(kbs, masked):
        m_new = [m_ref[h:h + 1, :] for h in range(N_HEADS)]
        for c, kb in enumerate(kbs):
            ks = pl.multiple_of(kb * tq, tq)
            for j in range(N_HEADS // 2):
                kp = k_ref[0, pl.ds(ks, tq), j * LANES:(j + 1) * LANES]
                for e in range(2):
                    h = 2 * j + e
                    st = jnp.dot(kp, qz_ref[h], preferred_element_type=F32)
                    bias = cr_ref[0, h, pl.ds(ks, tq), :]
                    st = st + jnp.concatenate([bias] * (tq // LANES), axis=1)
                    if masked:
                        st = jnp.where(causal, st, NEG_INF)
                    st_ref[c, h] = st
                    m_new[h] = jnp.maximum(m_new[h], jnp.max(st, axis=0, keepdims=True))
        for h in range(N_HEADS):
            alpha = jnp.exp2(m_ref[h:h + 1, :] - m_new[h])
            rows = slice(h * HEAD_DIM, (h + 1) * HEAD_DIM)
            l_new = alpha * l_ref[h:h + 1, :]
            acc = alpha * acc_ref[rows, :]
            for c, kb in enumerate(kbs):
                p = jnp.exp2(st_ref[c, h] - m_new[h])
                l_new = l_new + jnp.sum(p, axis=0, keepdims=True)
                acc = acc + jnp.dot(vt_ref[kb, rows, :], p.astype(BF16), preferred_element_type=F32)
            l_ref[h:h + 1, :] = l_new
            m_ref[h:h + 1, :] = m_new[h]
            acc_ref[rows, :] = acc

    def body(g, c):
        tiles([g * ATTN_GROUP + i for i in range(ATTN_GROUP)], False)
        return c

    n_groups = qi // ATTN_GROUP
    lax.fori_loop(0, n_groups, body, 0)
    for r in range(1, ATTN_GROUP):
        @pl.when(qi - n_groups * ATTN_GROUP == r)
        def _(r=r):
            tiles([n_groups * ATTN_GROUP + i for i in range(r)], False)
    tiles([qi], True)

    for h in range(N_HEADS):
        rows = slice(h * HEAD_DIM, (h + 1) * HEAD_DIM)
        acc_ref[rows, :] = acc_ref[rows, :] / l_ref[h:h + 1, :]
    o = acc_ref[...].T
    o_ref[0] = _rms(o, g_ref[...]).astype(BF16)


def _paged_seq_attention(q8, kn, vn, cn, g, kpages, vpages, n_pages, n_new):
    n_past = n_pages * PAGE
    d = q8.shape[-1]
    nr = N_HEADS * n_new
    new_bits = n_new.bit_length() - 1
    head_bits = HEAD_DIM.bit_length() - 1
    rowh = lax.shift_right_logical(lax.broadcasted_iota(I32, (nr, d), 0), new_bits)
    colh = lax.shift_right_logical(lax.broadcasted_iota(I32, (nr, d), 1), head_bits)
    bd = rowh == colh
    qrep = jnp.broadcast_to(q8[None], (N_HEADS, n_new, d)).reshape(nr, d)
    qbd = jnp.where(bd, qrep, 0.0).astype(BF16)
    cnr = jnp.broadcast_to(cn[:, None, :], (N_HEADS, n_new, cn.shape[-1])).reshape(nr, cn.shape[-1])

    s_p = []
    for p in range(n_pages):
        kt = kpages[p].reshape(d, PAGE)
        s_p.append(_bdot(qbd, kt) + cnr[:, p * PAGE:(p + 1) * PAGE])
    s_n = _bdot_nt(qbd, kn) + cnr[:, n_past:n_past + n_new]
    qpos = lax.broadcasted_iota(I32, (nr, n_new), 0) & (n_new - 1)
    kpos = lax.broadcasted_iota(I32, (nr, n_new), 1)
    s_n = jnp.where(kpos <= qpos, s_n, NEG_INF)
    m = jnp.max(s_n, axis=1, keepdims=True)
    for sp in s_p:
        m = jnp.maximum(m, jnp.max(sp, axis=1, keepdims=True))
    p_n = jnp.exp(s_n - m)
    l = jnp.sum(p_n, axis=1, keepdims=True)
    of = _bdot(p_n, vn)
    for p in range(n_pages):
        pp = jnp.exp(s_p[p] - m)
        l = l + jnp.sum(pp, axis=1, keepdims=True)
        of = of + _bdot_nt(pp, vpages[p].reshape(d, PAGE))
    of = jnp.where(bd, of / l, 0.0)
    o = jnp.sum(of.reshape(N_HEADS, n_new, d), axis=0)
    return _rms(o, g)


def _attention_kernel(pt_ref, qt_ref, k_ref, vt_ref, cr_ref, g_ref, qs_ref, kn_ref, vn_ref, cn_ref, kc_hbm, vc_hbm,
                      o_ref, os_ref, qz_ref, m_ref, l_ref, acc_ref, st_ref, kbuf, vbuf, sem, *, n_pages, n_seq):
    step = pl.program_id(0) * pl.num_programs(1) + pl.program_id(1)
    per_step = qs_ref.shape[0]
    n_new = qs_ref.shape[1]

    def page_copies(seq, slot):
        cps = []
        for p in range(n_pages):
            pg = pt_ref[seq * n_pages + p]
            cps.append(pltpu.make_async_copy(kc_hbm.at[pg], kbuf.at[slot, p], sem.at[0, slot]))
            cps.append(pltpu.make_async_copy(vc_hbm.at[pg], vbuf.at[slot, p], sem.at[1, slot]))
        return cps

    @pl.when(step == 0)
    def _():
        for slot in range(2):
            for cp in page_copies(slot, slot):
                cp.start()

    def sample_seq(j):
        seq = step * per_step + j
        slot = j % 2
        for cp in page_copies(seq, slot):
            cp.wait()
        os_ref[j] = _paged_seq_attention(qs_ref[j], kn_ref[j], vn_ref[j], cn_ref[j], g_ref[...],
                                         kbuf.at[slot], vbuf.at[slot], n_pages, n_new)

        @pl.when(seq + 2 < n_seq)
        def _():
            for cp in page_copies(seq + 2, slot):
                cp.start()

    for j in range(per_step // 2):
        sample_seq(j)
    _prompt_attention_step(pl.program_id(1), qt_ref, k_ref, vt_ref, cr_ref, g_ref, o_ref,
                           qz_ref, m_ref, l_ref, acc_ref, st_ref)
    for j in range(per_step // 2, per_step):
        sample_seq(j)


def _attention(qt, k, vt, crep, g, page_table, qs, kn, vn, cn, cache_k, cache_v):
    b, length, d = k.shape
    tq = ATTN_TILE
    nq = length // tq
    n_seq, n_new, _ = qs.shape
    assert n_new & (n_new - 1) == 0, "new-token count must be a power of two"
    per_step = n_seq // (b * nq)
    assert per_step % 2 == 0 and per_step * b * nq == n_seq
    n_pages = page_table.shape[1]
    once = pl.Buffered(1)
    smap = lambda i, j, pt: (i * nq + j, 0, 0)
    grid_spec = pltpu.PrefetchScalarGridSpec(
        num_scalar_prefetch=1,
        grid=(b, nq),
        in_specs=[pl.BlockSpec((1, d, tq), smap),
                  pl.BlockSpec((1, length, d), lambda i, j, pt: (i, 0, 0), pipeline_mode=once),
                  pl.BlockSpec((nq, d, tq), lambda i, j, pt: (i, 0, 0), pipeline_mode=once),
                  pl.BlockSpec((1, N_HEADS, length, LANES), lambda i, j, pt: (i, 0, 0, 0), pipeline_mode=once),
                  pl.BlockSpec((1, d), lambda i, j, pt: (0, 0)),
                  pl.BlockSpec((per_step, n_new, d), smap),
                  pl.BlockSpec((per_step, n_new, d), smap),
                  pl.BlockSpec((per_step, n_new, d), smap),
                  pl.BlockSpec((per_step, N_HEADS, cn.shape[-1]), smap),
                  pl.BlockSpec(memory_space=pl.ANY),
                  pl.BlockSpec(memory_space=pl.ANY)],
        out_specs=[pl.BlockSpec((1, tq, d), lambda i, j, pt: (i, j, 0)),
                   pl.BlockSpec((per_step, n_new, d), smap)],
        scratch_shapes=[pltpu.VMEM((N_HEADS, LANES, tq), BF16), pltpu.VMEM((N_HEADS, tq), F32),
                        pltpu.VMEM((N_HEADS, tq), F32), pltpu.VMEM((d, tq), F32),
                        pltpu.VMEM((ATTN_GROUP, N_HEADS, tq, tq), F32),
                        pltpu.VMEM((2, n_pages, N_HEADS, HEAD_DIM, PAGE), F32),
                        pltpu.VMEM((2, n_pages, N_HEADS, HEAD_DIM, PAGE), F32),
                        pltpu.SemaphoreType.DMA((2, 2))],
    )
    return pl.pallas_call(
        functools.partial(_attention_kernel, n_pages=n_pages, n_seq=n_seq),
        grid_spec=grid_spec,
        out_shape=[jax.ShapeDtypeStruct((b, length, d), BF16), jax.ShapeDtypeStruct((n_seq, n_new, d), F32)],
        compiler_params=pltpu.CompilerParams(dimension_semantics=("arbitrary", "arbitrary"),
                                             vmem_limit_bytes=ATTN_VMEM_LIMIT),
    )(page_table.reshape(-1), qt, k, vt, crep, g, qs, kn, vn, cn, cache_k, cache_v)


def _outproj_kernel(x_ref, ys_ref, oa_ref, gt_ref, sc_ref, sh_ref, g_ref, wo_ref, wrt_ref,
                    x1_ref, h2_ref, lg_ref):
    nbb, rb, d = x_ref.shape
    mix = jnp.dot(ys_ref[...], wo_ref[0:D_SSM, :], preferred_element_type=F32)
    mix = mix + jnp.dot(oa_ref[...], wo_ref[D_SSM:, :], preferred_element_type=F32)
    x1 = x_ref[...] + gt_ref[...] * mix.reshape(nbb, rb, d)
    x1_ref[...] = x1
    h2 = (_rms(x1, g_ref[...]) * (1.0 + sc_ref[...]) + sh_ref[...]).reshape(nbb * rb, d)
    hb = h2.astype(BF16)
    h2_ref[...] = _pack_bf16_pairs(hb)
    hlo = (h2 - hb.astype(F32)).astype(BF16)
    whi = wrt_ref[0]
    wlo = wrt_ref[1]
    nt = (((1,), (1,)), ((), ()))
    lg = lax.dot_general(whi, hb, nt, preferred_element_type=F32)
    lg = lg + lax.dot_general(wlo, hb, nt, preferred_element_type=F32)
    lg = lg + lax.dot_general(whi, hlo, nt, preferred_element_type=F32)
    lg_ref[...] = lg


def _outproj(x3, ys, oa, gt, sc, sh, g, wo, wrt, nbb, rb):
    nb, r, d = x3.shape
    t = nb * r
    rows = nbb * rb
    nj = r // rb
    row_map = lambda i, j: (i * nj + j, 0)
    mod_spec = pl.BlockSpec((nbb, 1, d), lambda i, j: (i, 0, 0))
    return pl.pallas_call(
        _outproj_kernel,
        grid=(nb // nbb, nj),
        in_specs=[pl.BlockSpec((nbb, rb, d), lambda i, j: (i, j, 0)),
                  pl.BlockSpec((rows, D_SSM), row_map),
                  pl.BlockSpec((rows, D_ATTN), row_map),
                  mod_spec, mod_spec, mod_spec,
                  pl.BlockSpec((1, 1, d), lambda i, j: (0, 0, 0)),
                  pl.BlockSpec((D_SSM + D_ATTN, d), lambda i, j: (0, 0)),
                  pl.BlockSpec((2, N_EXPERTS, d), lambda i, j: (0, 0, 0))],
        out_specs=[pl.BlockSpec((nbb, rb, d), lambda i, j: (i, j, 0)),
                   pl.BlockSpec((rows, d // 2), row_map),
                   pl.BlockSpec((N_EXPERTS, rows), lambda i, j: (0, i * nj + j))],
        out_shape=[jax.ShapeDtypeStruct((nb, r, d), F32),
                   jax.ShapeDtypeStruct((t, d // 2), jnp.uint32),
                   jax.ShapeDtypeStruct((N_EXPERTS, t), F32)],
        compiler_params=_cparams(("arbitrary", "arbitrary")),
    )(x3, ys, oa, gt, sc, sh, g, wo, wrt)


def _router_kernel(lg_ref, rb_ref, ut_ref, idx_ref, w_ref, rank_ref, cnt_ref, carry_ref):
    i = pl.program_id(0)
    tm = lg_ref.shape[1]
    per_group = N_EXPERTS // N_EXPERT_GROUPS

    @pl.when(i == 0)
    def _():
        carry_ref[...] = jnp.zeros_like(carry_ref)

    scores = jax.nn.sigmoid(lg_ref[...])
    biased = scores + rb_ref[...]
    blks, grp = [], []
    for g in range(N_EXPERT_GROUPS):
        blk = biased[g * per_group:(g + 1) * per_group, :]
        m1 = jnp.max(blk, axis=0, keepdims=True)
        eq = blk == m1
        n_eq = jnp.sum(jnp.where(eq, 1.0, 0.0), axis=0, keepdims=True)
        m2 = jnp.max(jnp.where(eq, NEG_INF, blk), axis=0, keepdims=True)
        blks.append(blk)
        grp.append(m1 + jnp.where(n_eq >= 2.0, m1, m2))
    masked = []
    for g in range(N_EXPERT_GROUPS):
        beaten = jnp.zeros((1, tm), F32)
        for o in range(N_EXPERT_GROUPS):
            if o == g:
                continue
            ahead = (grp[o] >= grp[g]) if o < g else (grp[o] > grp[g])
            beaten = beaten + jnp.where(ahead, 1.0, 0.0)
        masked.append(jnp.where(beaten < float(TOPK_GROUPS), blks[g], NEG_INF))
    work = jnp.concatenate(masked, axis=0)

    eid = lax.broadcasted_iota(I32, (N_EXPERTS, tm), 0)
    chosen = jnp.zeros((N_EXPERTS, tm), F32)
    idxs, ws, sels = [], [], []
    for _ in range(TOP_K):
        m = jnp.max(work, axis=0, keepdims=True)
        first = jnp.min(jnp.where(work == m, eid, N_EXPERTS), axis=0, keepdims=True)
        sel = eid == first
        idxs.append(first)
        ws.append(jnp.sum(jnp.where(sel, scores, 0.0), axis=0, keepdims=True))
        sels.append(sel)
        chosen = jnp.where(sel, 1.0, chosen)
        work = jnp.where(sel, NEG_INF, work)
    wsum = ws[0]
    for wk in ws[1:]:
        wsum = wsum + wk

    prefix = jnp.dot(chosen.astype(BF16), ut_ref[...], preferred_element_type=F32) + carry_ref[...]
    carry_ref[...] = carry_ref[...] + jnp.sum(chosen, axis=1, keepdims=True)
    cnt_ref[...] = carry_ref[...]

    idx_ref[...] = jnp.zeros_like(idx_ref)
    w_ref[...] = jnp.zeros_like(w_ref)
    rank_ref[...] = jnp.zeros_like(rank_ref)
    for k in range(TOP_K):
        idx_ref[k:k + 1, :] = idxs[k]
        w_ref[k:k + 1, :] = ws[k] / wsum * ROUTED_SCALE
        rank = jnp.sum(jnp.where(sels[k], prefix, 0.0), axis=0, keepdims=True)
        rank_ref[k:k + 1, :] = rank.astype(I32)


def _router(lg, router_bias, tm):
    e, t = lg.shape
    ut = np.triu(np.ones((tm, tm), np.float32), 1)
    tok_spec = pl.BlockSpec((SUBLANES, tm), lambda i: (0, i))
    return pl.pallas_call(
        _router_kernel,
        grid=(t // tm,),
        in_specs=[pl.BlockSpec((e, tm), lambda i: (0, i)),
                  pl.BlockSpec((e, 1), lambda i: (0, 0)),
                  pl.BlockSpec((tm, tm), lambda i: (0, 0))],
        out_specs=[tok_spec, tok_spec, tok_spec, pl.BlockSpec((e, 1), lambda i: (0, 0))],
        out_shape=[jax.ShapeDtypeStruct((SUBLANES, t), I32), jax.ShapeDtypeStruct((SUBLANES, t), F32),
                   jax.ShapeDtypeStruct((SUBLANES, t), I32), jax.ShapeDtypeStruct((e, 1), F32)],
        scratch_shapes=[pltpu.VMEM((e, 1), F32)],
        compiler_params=_cparams(("arbitrary",)),
    )(lg, router_bias.reshape(e, 1), jnp.asarray(ut, BF16))


def _experts_kernel(be_ref, nv_ref, new_ref, fill_ref, x_ref, wg_ref, wu_ref, wd_ref, o_ref, wgu_s, wd_s):
    i = pl.program_id(0)
    de = wg_ref.shape[-1]
    valid = i < nv_ref[0]

    @pl.when(valid & (new_ref[i] == 1))
    def _():
        wgu_s[:, :de] = wg_ref[0].astype(BF16)
        wgu_s[:, de:] = wu_ref[0].astype(BF16)
        wd_s[...] = wd_ref[0].astype(BF16)

    @pl.when(valid)
    def _():
        row = lax.broadcasted_iota(I32, x_ref.shape, 0)
        x = jnp.where(row < fill_ref[i], x_ref[...], jnp.uint32(0))
        gu = jnp.dot(_unpack_bf16_pairs(x), wgu_s[...], preferred_element_type=F32)
        g = gu[:, :de]
        a = (g * jax.nn.sigmoid(g) * gu[:, de:]).astype(BF16)
        o = jnp.dot(a, wd_s[...], preferred_element_type=F32)
        o_ref[...] = _pack_bf16_pairs(o.astype(BF16))

    @pl.when(jnp.logical_not(valid))
    def _():
        o_ref[...] = jnp.zeros_like(o_ref)


def _experts(block_e, n_valid, block_new, block_fill, xs, w_eg, w_eu, w_ed, m):
    n_rows = xs.shape[0]
    d, de = w_eg.shape[-2:]
    nblk = n_rows // m
    xmap = lambda i, be, nv, *_: (jnp.minimum(i, nv[0] - 1), 0)
    wmap = lambda i, be, *_: (be[i], 0, 0)
    grid_spec = pltpu.PrefetchScalarGridSpec(
        num_scalar_prefetch=4,
        grid=(nblk,),
        in_specs=[pl.BlockSpec((m, xs.shape[1]), xmap),
                  pl.BlockSpec((1, d, de), wmap),
                  pl.BlockSpec((1, d, de), wmap),
                  pl.BlockSpec((1, de, d), wmap)],
        out_specs=pl.BlockSpec((m, d // 2), lambda i, *_: (i, 0)),
        scratch_shapes=[pltpu.VMEM((d, 2 * de), BF16), pltpu.VMEM((de, d), BF16)],
    )
    return pl.pallas_call(
        _experts_kernel,
        grid_spec=grid_spec,
        out_shape=jax.ShapeDtypeStruct((n_rows, d // 2), jnp.uint32),
        compiler_params=_cparams(("arbitrary",)),
    )(block_e, n_valid, block_new, block_fill, xs, w_eg, w_eu, w_ed)


def _gather_rows(table, idx, window):
    n = idx.shape[0]
    d = table.shape[1]
    info = plsc.get_sparse_core_info()
    n_workers = info.num_cores * info.num_subcores
    per_worker = n // n_workers
    n_buf = SC_GATHER_BUFFERS
    n_rounds = per_worker // (window * n_buf)
    assert n_rounds * window * n_buf * n_workers == n
    mesh = plsc.VectorSubcoreMesh(core_axis_name="c", subcore_axis_name="s")

    @functools.partial(
        pl.kernel, out_type=jax.ShapeDtypeStruct((n, d), table.dtype), mesh=mesh,
        scratch_types=[pltpu.VMEM((per_worker,), I32), pltpu.VMEM((n_buf, window, d), table.dtype),
                       pltpu.SemaphoreType.DMA((n_buf,)), pltpu.SemaphoreType.DMA((n_buf,))])
    def gather(x_hbm, i_hbm, o_hbm, i_v, buf, gsem, wsem):
        base = (lax.axis_index("s") * info.num_cores + lax.axis_index("c")) * per_worker
        pltpu.sync_copy(i_hbm.at[pl.ds(base, per_worker)], i_v)

        @pl.loop(0, n_rounds)
        def _(rnd):
            first = rnd * (window * n_buf)
            reads, writes = [], []
            for b in range(n_buf):
                rows = pl.ds(first + b * window, window)
                reads.append(pltpu.make_async_copy(x_hbm.at[i_v.at[rows]], buf.at[b], gsem.at[b]))
                writes.append(pltpu.make_async_copy(buf.at[b], o_hbm.at[pl.ds(base + first + b * window, window)],
                                                    wsem.at[b]))
            for cp in reads:
                cp.start()
            for b in range(n_buf):
                reads[b].wait()
                writes[b].start()
            for cp in writes:
                cp.wait()

    return gather(table, idx)


def _scatter_rows(groups, dest_t, n_rows, window):
    d = groups[0].shape[1]
    dtype = groups[0].dtype
    info = plsc.get_sparse_core_info()
    n_workers = info.num_cores * info.num_subcores
    n_buf = 2
    plan, idx_parts, first = [], [], 0
    row0 = 0
    for x in groups:
        t = x.shape[0]
        per_worker = t // n_workers
        chunks = per_worker // window
        assert chunks % n_buf == 0 and chunks * window * n_workers == t
        idx = dest_t[:TOP_K, first:first + t].reshape(TOP_K, n_workers, chunks, window).transpose(1, 0, 2, 3)
        idx_parts.append(idx.reshape(n_workers, TOP_K * chunks, window))
        plan.append((per_worker, chunks, row0))
        row0 += TOP_K * chunks
        first += t
    idx_rows = -(-row0 // SUBLANES) * SUBLANES
    idx = jnp.concatenate(idx_parts + [jnp.zeros((n_workers, idx_rows - row0, window), I32)], axis=1)
    idx = idx.reshape(n_workers * idx_rows, window)
    mesh = plsc.VectorSubcoreMesh(core_axis_name="c", subcore_axis_name="s")

    @functools.partial(
        pl.kernel, out_type=jax.ShapeDtypeStruct((n_rows, d), dtype), mesh=mesh,
        scratch_types=[pltpu.VMEM((idx_rows, window), I32), pltpu.VMEM((n_buf, window, d), dtype),
                       pltpu.SemaphoreType.DMA((n_buf,)), pltpu.SemaphoreType.DMA((n_buf,))])
    def scatter(*refs):
        x_hbms = refs[:len(groups)]
        i_hbm, o_hbm, i_v, buf, rsem, wsem = refs[len(groups):]
        wid = lax.axis_index("s") * info.num_cores + lax.axis_index("c")
        pltpu.sync_copy(i_hbm.at[pl.ds(wid * idx_rows, idx_rows)], i_v)

        for x_hbm, (per_worker, chunks, row0) in zip(x_hbms, plan):
            base = wid * per_worker

            @pl.loop(0, chunks // n_buf)
            def _(rnd, x_hbm=x_hbm, base=base, chunks=chunks, row0=row0):
                reads = [pltpu.make_async_copy(x_hbm.at[pl.ds(base + (rnd * n_buf + b) * window, window)],
                                               buf.at[b], rsem.at[b]) for b in range(n_buf)]
                for cp in reads:
                    cp.start()
                writes = []
                for b in range(n_buf):
                    reads[b].wait()
                    for k in range(TOP_K):
                        rows = i_v.at[row0 + k * chunks + rnd * n_buf + b]
                        cp = pltpu.make_async_copy(buf.at[b], o_hbm.at[rows], wsem.at[b])
                        cp.start()
                        writes.append(cp)
                for cp in writes:
                    cp.wait()

    return scatter(*groups, idx)


def _final_kernel(x1_ref, h2_ref, w_ref, gt_ref, sc_ref, sh_ref, g_ref, wgu_ref, wd_ref, *rest):
    yg_refs = rest[:TOP_K]
    y_ref = rest[TOP_K]
    nbb, rb, d = x1_ref.shape
    ds = wd_ref.shape[0]
    gu = jnp.dot(_unpack_bf16_pairs(h2_ref[...]), wgu_ref[...], preferred_element_type=F32)
    g = gu[:, :ds]
    a = (g * jax.nn.sigmoid(g) * gu[:, ds:]).astype(BF16)
    ff = jnp.dot(a, wd_ref[...], preferred_element_type=F32)
    wts = w_ref[...].T
    routed_lo = routed_hi = None
    for k in range(TOP_K):
        yk = yg_refs[k][...]
        w = wts[:, k:k + 1]
        lo = lax.bitcast_convert_type(lax.shift_left(yk, jnp.uint32(16)), F32) * w
        hi = lax.bitcast_convert_type(yk & jnp.uint32(0xFFFF0000), F32) * w
        routed_lo = lo if k == 0 else routed_lo + lo
        routed_hi = hi if k == 0 else routed_hi + hi
    ff = jnp.concatenate([routed_lo, routed_hi], axis=1) + ff
    x2 = x1_ref[...] + gt_ref[...] * ff.reshape(nbb, rb, d)
    y_ref[...] = _rms(x2, g_ref[...]) * (1.0 + sc_ref[...]) + sh_ref[...]


def _final(x1, h2, wts, yg, first_tile, gt, sc, sh, g, wgu, wd, nbb, rb):
    nb, r, d = x1.shape
    rows = nbb * rb
    nj = r // rb
    tiles_all = yg.shape[0] // TOP_K // rows
    row_map = lambda i, j: (i * nj + j, 0)
    mod_spec = pl.BlockSpec((nbb, 1, d), lambda i, j: (i, 0, 0))
    slab_specs = [pl.BlockSpec((rows, yg.shape[1]), lambda i, j, k=k: (k * tiles_all + first_tile + i * nj + j, 0))
                  for k in range(TOP_K)]
    return pl.pallas_call(
        _final_kernel,
        grid=(nb // nbb, nj),
        in_specs=[pl.BlockSpec((nbb, rb, d), lambda i, j: (i, j, 0)),
                  pl.BlockSpec((rows, h2.shape[1]), row_map),
                  pl.BlockSpec((SUBLANES, rows), lambda i, j: (0, first_tile + i * nj + j)),
                  mod_spec, mod_spec, mod_spec,
                  pl.BlockSpec((1, 1, d), lambda i, j: (0, 0, 0)),
                  pl.BlockSpec(wgu.shape, lambda i, j: (0, 0)),
                  pl.BlockSpec(wd.shape, lambda i, j: (0, 0))] + slab_specs,
        out_specs=pl.BlockSpec((nbb, rb, d), lambda i, j: (i, j, 0)),
        out_shape=jax.ShapeDtypeStruct((nb, r, d), F32),
        compiler_params=_cparams(("arbitrary", "arbitrary")),
    )(x1, h2, wts, gt, sc, sh, g, wgu, wd, *([yg] * TOP_K))


def _moe(h2_groups, lg, router_bias, w_eg, w_eu, w_ed, tile):
    t = lg.shape[1]
    e = N_EXPERTS
    m = MOE_ROWS
    idx_t, w_t, rank_t, counts = _router(lg, router_bias, tile)
    counts = counts.reshape(e).astype(I32)
    padded = ((counts + m - 1) // m) * m
    pad_end = jnp.cumsum(padded)
    pad_start = pad_end - padded
    onehot = idx_t[:, :, None] == jnp.arange(e, dtype=I32)
    dest_t = jnp.sum(jnp.where(onehot, pad_start, 0), axis=-1) + rank_t
    n_rows = (-(-(t * TOP_K) // m)) * m + e * m
    nblk = n_rows // m
    block_start = jnp.arange(nblk, dtype=I32) * m
    block_e = jnp.minimum(jnp.sum(pad_end[None, :] <= block_start[:, None], axis=1), e - 1).astype(I32)
    block_new = jnp.concatenate([jnp.ones((1,), I32), (block_e[1:] != block_e[:-1]).astype(I32)])
    n_valid = (pad_end[-1] // m).astype(I32).reshape(1)
    of_block = block_e[:, None] == jnp.arange(e, dtype=I32)
    count_b = jnp.sum(jnp.where(of_block, counts, 0), axis=1)
    start_b = jnp.sum(jnp.where(of_block, pad_start, 0), axis=1)
    block_fill = jnp.clip(count_b - (block_start - start_b), 0, m).astype(I32)
    xs = _scatter_rows(h2_groups, dest_t, n_rows, SC_SCATTER_WINDOW)
    yb = _experts(block_e, n_valid, block_new, block_fill, xs, w_eg, w_eu, w_ed, m)
    yg = _gather_rows(yb, dest_t[:TOP_K].reshape(-1), SC_GATHER_WINDOW)
    return w_t, yg


def kernel(x_prompt, x_sample, c_prompt, c_sample, cache_k, cache_v, cache_logf, state_ssm_re, state_ssm_im, page_table, w_ada, b_ada, g_norm1, w_in, b_fgate, ssm_lambda_re, ssm_lambda_im, ssm_log_step, ssm_b_re, ssm_b_im, ssm_c_re, ssm_c_im, ssm_d, w_glu, b_glu, g_ssm_out, g_attn_out, w_out, g_norm2, w_router, router_bias, w_exp_gate, w_exp_up, w_exp_down, w_sh_gate, w_sh_up, w_sh_down, g_final, w_ada_final, b_ada_final):
    depth = w_ada.shape[0]
    assert depth == 1, "one layer is supported"
    bp, lp, d = x_prompt.shape
    bs, ls, _ = x_sample.shape
    n_pages = page_table.shape[1]
    n_past = n_pages * PAGE
    n_groups = ssm_lambda_re.shape[1]

    n_c = bp + bs
    n_c_pad = -(-n_c // SUBLANES) * SUBLANES
    c_all = jnp.concatenate([c_prompt, c_sample, jnp.zeros((n_c_pad - n_c, d), F32)], axis=0)
    mod = _adaln(c_all, w_ada[0], b_ada[0])
    modf = _adaln(c_all, w_ada_final, b_ada_final)

    def mods(lo, hi):
        parts = [mod[lo:hi, k * d:(k + 1) * d][:, None, :] for k in range(6)]
        parts += [modf[lo:hi, k * d:(k + 1) * d][:, None, :] for k in range(2)]
        return parts

    w_u, w_q, w_k, w_v, w_f = jnp.split(w_in[0], [D_SSM, D_SSM + D_ATTN, D_SSM + 2 * D_ATTN,
                                                   D_SSM + 3 * D_ATTN], axis=1)
    w_fpad = jnp.concatenate([w_f, jnp.zeros((d, LANES - N_HEADS), F32)], axis=1)
    w_main_s = jnp.concatenate([w_u, w_k, w_v, w_fpad], axis=1).astype(BF16)
    w_main_p = jnp.concatenate([w_u, w_k, w_fpad], axis=1).astype(BF16)
    w_qkv = jnp.stack([w_q.T, w_k.T, w_v.T]).astype(BF16)
    b_f = jnp.concatenate([b_fgate[0], jnp.zeros((LANES - N_HEADS,), F32)]).reshape(1, LANES)
    g1 = g_norm1[0].reshape(1, 1, d)
    g2 = g_norm2[0].reshape(1, 1, d)
    gf = g_final.reshape(1, 1, d)
    wb, wc, a_re, a_im = _s5_params(ssm_lambda_re[0], ssm_lambda_im[0], ssm_log_step[0], ssm_b_re[0],
                                    ssm_b_im[0], ssm_c_re[0], ssm_c_im[0])
    dsk = ssm_d[0].reshape(1, D_SSM)
    wglu = w_glu[0].astype(BF16)
    bglu = b_glu[0].reshape(1, D_SSM)
    g_so = g_ssm_out[0].reshape(1, D_SSM)
    g_ao = g_attn_out[0].reshape(1, D_ATTN)
    wo = w_out[0].astype(BF16)
    wr_t = w_router[0].T
    wr_hi = wr_t.astype(BF16)
    wrt = jnp.stack([wr_hi, (wr_t - wr_hi.astype(F32)).astype(BF16)])
    wgu = jnp.concatenate([w_sh_gate[0], w_sh_up[0]], axis=1).astype(BF16)
    wsd = w_sh_down[0].astype(BF16)

    def ssm_state(re, im):
        return jnp.concatenate([re.reshape(-1, 16, LANES), im.reshape(-1, 16, LANES)], axis=1)

    def split_state(ht):
        n = ht.shape[0]
        return (ht[:, :16].reshape(1, n, n_groups, SSM_STATE), ht[:, 16:].reshape(1, n, n_groups, SSM_STATE))

    tm = 512
    sh1_p, sc1_p, gt1_p, sh2_p, sc2_p, gt2_p, shf_p, scf_p = mods(0, bp)
    u, lf, kb, qt, vt, ktf, vtf, lft = _inproj(x_prompt, sc1_p, sh1_p, g1, w_main_p, w_qkv, b_f, 1, tm, True,
                                               HEAD_DIM ** -0.5 * LOG2E)
    crep = _crep(lf, bp, lp, LOG2E)
    ys_p, ht = _s5(u.reshape(bp, lp, D_SSM), jnp.zeros((bp, 32, LANES), F32), wb, wc, a_re, a_im, dsk,
                   wglu, bglu, g_so, bp, 256)
    k_prompt = ktf.reshape(bp, N_HEADS, HEAD_DIM, lp).transpose(0, 3, 1, 2)[None]
    v_prompt = vtf.reshape(bp, N_HEADS, HEAD_DIM, lp).transpose(0, 3, 1, 2)[None]
    logf_prompt = lft.transpose(0, 2, 1)[None]
    sre_p, sim_p = split_state(ht)

    nbb = 64
    sh1_s, sc1_s, gt1_s, sh2_s, sc2_s, gt2_s, shf_s, scf_s = mods(bp, bp + bs)
    u, lf, k, v, q = _inproj(x_sample, sc1_s, sh1_s, g1, w_main_s, w_qkv, b_f, nbb, ls, False, HEAD_DIM ** -0.5)
    ys_s, ht = _s5(u.reshape(bs, ls, D_SSM), ssm_state(state_ssm_re[0], state_ssm_im[0]), wb, wc, a_re, a_im,
                   dsk, wglu, bglu, g_so, 32, ls)
    lf_past = cache_logf[0][page_table].reshape(bs, n_past, N_HEADS).transpose(0, 2, 1)
    lf_new = lf[:, :N_HEADS].reshape(bs, ls, N_HEADS).transpose(0, 2, 1)
    n_key_pad = -(-(n_past + ls) // LANES) * LANES
    lf_all = jnp.concatenate([lf_past, lf_new, jnp.zeros((bs, N_HEADS, n_key_pad - n_past - ls), F32)], axis=2)
    cn_s = _neg_cumsum(lf_all.reshape(bs * N_HEADS, n_key_pad), 256).reshape(bs, N_HEADS, n_key_pad)

    oa_p, oa_s = _attention(qt, kb.reshape(bp, lp, D_ATTN), vt, crep, g_ao, page_table,
                            q.reshape(bs, ls, D_ATTN), k.reshape(bs, ls, D_ATTN), v.reshape(bs, ls, D_ATTN),
                            cn_s, cache_k[0].transpose(0, 2, 3, 1), cache_v[0].transpose(0, 2, 3, 1))
    x1_p, h2_p, lg_p = _outproj(x_prompt, ys_p.reshape(bp * lp, D_SSM), oa_p.reshape(bp * lp, D_ATTN),
                                gt1_p, sc2_p, sh2_p, g2, wo, wrt, 1, tm)
    x1_s, h2_s, lg_s = _outproj(x_sample, ys_s, oa_s.reshape(bs * ls, D_ATTN).astype(BF16),
                                gt1_s, sc2_s, sh2_s, g2, wo, wrt, nbb, ls)

    assert nbb * ls == tm
    wts, yg = _moe([h2_p, h2_s], jnp.concatenate([lg_p, lg_s], axis=1), router_bias[0],
                   w_exp_gate[0], w_exp_up[0], w_exp_down[0], tm)
    tiles_p = bp * lp // tm
    y_prompt = _final(x1_p, h2_p, wts, yg, 0, gt2_p, scf_p, shf_p, gf, wgu, wsd, 1, tm)
    y_sample = _final(x1_s, h2_s, wts, yg, tiles_p, gt2_s, scf_s, shf_s, gf, wgu, wsd, nbb, ls)
    k_sample = k.reshape(1, bs, ls, N_HEADS, HEAD_DIM)
    v_sample = v.reshape(1, bs, ls, N_HEADS, HEAD_DIM)
    logf_sample = lf[:, :N_HEADS].reshape(1, bs, ls, N_HEADS)
    sre_s, sim_s = split_state(ht)

    return (y_prompt, y_sample, k_prompt, v_prompt, logf_prompt, sre_p, sim_p,
            k_sample, v_sample, logf_sample, sre_s, sim_s)
```

```python
import functools
import math

import jax
import jax.numpy as jnp
import numpy as np
from jax import lax
from jax.experimental import pallas as pl
from jax.experimental.pallas import tpu as pltpu
from jax.experimental.pallas import tpu_sc as plsc

F32 = jnp.float32
BF16 = jnp.bfloat16
I32 = jnp.int32

EPS = 1e-6
HEAD_DIM = 64
N_HEADS = 8
D_SSM = 512
D_ATTN = 512
SSM_GROUP = 16
SSM_STATE = 64
N_EXPERTS = 64
TOP_K = 6
N_EXPERT_GROUPS = 8
TOPK_GROUPS = 4
ROUTED_SCALE = 2.5
PAGE = 128

LANES = 128
SUBLANES = 8
VMEM_LIMIT = 48 * 1024 * 1024
ATTN_VMEM_LIMIT = 58 * 1024 * 1024
ROW_TILE = 512
S5_TIME_TILE = 256
S5_SAMPLE_SEQS = 32
S5_SCAN_UNROLL = 4
PREFIX_ROWS = 256
MOE_ROWS = 512
ATTN_TILE = 256
ATTN_GROUP = 4
LOG2E = math.log2(math.e)
SC_SCATTER_WINDOW = 16
SC_GATHER_WINDOW = 48
SC_GATHER_BUFFERS = 4
NEG_INF = float("-inf")


def _cparams(sem):
    return pltpu.CompilerParams(dimension_semantics=sem, vmem_limit_bytes=VMEM_LIMIT)


def _bdot(a, b):
    return jnp.dot(a.astype(BF16), b.astype(BF16), preferred_element_type=F32)


def _bdot_nt(a, b):
    return lax.dot_general(a.astype(BF16), b.astype(BF16), (((1,), (1,)), ((), ())),
                           preferred_element_type=F32)


def _split3(v):
    hi = v.astype(BF16)
    r1 = v - hi.astype(F32)
    mid = r1.astype(BF16)
    lo = (r1 - mid.astype(F32)).astype(BF16)
    return hi, mid, lo


def _pack_bf16_pairs(xb):
    n = xb.shape[1] // 2
    bits = lax.bitcast_convert_type(xb.astype(F32), jnp.uint32)
    return lax.shift_right_logical(bits[:, :n], jnp.uint32(16)) | bits[:, n:]


def _unpack_bf16_pairs(xp):
    lo = lax.bitcast_convert_type(lax.shift_left(xp, jnp.uint32(16)), F32)
    hi = lax.bitcast_convert_type(xp & jnp.uint32(0xFFFF0000), F32)
    return jnp.concatenate([lo, hi], axis=1).astype(BF16)


def _rms(x, g):
    return x * lax.rsqrt(jnp.mean(x * x, axis=-1, keepdims=True) + EPS) * g


def _adaln_kernel(c_ref, w_ref, b_ref, o_ref):
    c = c_ref[...]
    s = c * jax.nn.sigmoid(c)
    o_ref[...] = _bdot(s, w_ref[...]) + b_ref[...]


def _adaln(c, w, b):
    m, k = c.shape
    n = w.shape[1]
    tn = 1024
    return pl.pallas_call(
        _adaln_kernel,
        grid=(n // tn,),
        in_specs=[pl.BlockSpec((m, k), lambda j: (0, 0)),
                  pl.BlockSpec((k, tn), lambda j: (0, j)),
                  pl.BlockSpec((1, tn), lambda j: (0, j))],
        out_specs=pl.BlockSpec((m, tn), lambda j: (0, j)),
        out_shape=jax.ShapeDtypeStruct((m, n), F32),
        compiler_params=_cparams(("arbitrary",)),
    )(c, w, b.reshape(1, n))


def _inproj_kernel(x_ref, sc_ref, sh_ref, g_ref, w_ref, wqv_ref, bf_ref, *out_refs, transposed, q_scale):
    nbb, rb, d = x_ref.shape
    rows = nbb * rb
    x = x_ref[...]
    h = _rms(x, g_ref[...]) * (1.0 + sc_ref[...]) + sh_ref[...]
    hb = h.reshape(rows, d).astype(BF16)
    proj = jnp.dot(hb, w_ref[...], preferred_element_type=F32)
    u_ref, lf_ref = out_refs[:2]
    u_ref[...] = proj[:, :D_SSM]
    k = proj[:, D_SSM:D_SSM + D_ATTN]
    z = proj[:, proj.shape[1] - LANES:] + bf_ref[...]
    lf = jnp.minimum(z, 0.0) - jnp.log1p(jnp.exp(-jnp.abs(z)))
    lf_ref[...] = lf
    nt = (((1,), (1,)), ((), ()))
    if transposed:
        kb_ref, qt_ref, vt_ref, ktf_ref, vtf_ref, lft_ref = out_refs[2:]
        lft_ref[0] = lf.T[:N_HEADS]
        kb_ref[...] = k.astype(BF16)
        qt = (lax.dot_general(wqv_ref[0], hb, nt, preferred_element_type=F32) * q_scale).astype(BF16)
        ktf_ref[0] = lax.dot_general(wqv_ref[1], hb, nt, preferred_element_type=F32)
        vt = lax.dot_general(wqv_ref[2], hb, nt, preferred_element_type=F32)
        vtf_ref[0] = vt
        vt = vt.astype(BF16)
        for c in range(rows // ATTN_TILE):
            qt_ref[c] = qt[:, c * ATTN_TILE:(c + 1) * ATTN_TILE]
            vt_ref[c] = vt[:, c * ATTN_TILE:(c + 1) * ATTN_TILE]
    else:
        k_ref, v_ref, q_ref = out_refs[2:]
        k_ref[...] = k
        v_ref[...] = proj[:, D_SSM + D_ATTN:D_SSM + 2 * D_ATTN]
        q_ref[...] = lax.dot_general(hb, wqv_ref[0], nt, preferred_element_type=F32) * q_scale


def _inproj(x3, sc, sh, g, w_main, w_qv, b_f, nbb, rb, transposed, q_scale):
    nb, r, d = x3.shape
    t = nb * r
    rows = nbb * rb
    nj = r // rb
    grid = (nb // nbb, nj)
    row_map = lambda i, j: (i * nj + j, 0)
    n_main = w_main.shape[1]
    outs = [jax.ShapeDtypeStruct((t, D_SSM), F32), jax.ShapeDtypeStruct((t, LANES), F32)]
    out_specs = [pl.BlockSpec((rows, D_SSM), row_map), pl.BlockSpec((rows, LANES), row_map)]
    if transposed:
        assert nbb == 1, "transposed outputs are laid out per sequence"
        nc = rows // ATTN_TILE
        chunk_map = lambda i, j: (i * nj + j, 0, 0)
        seq_map = lambda i, j: (i, 0, j)
        outs += [jax.ShapeDtypeStruct((t, D_ATTN), BF16),
                 jax.ShapeDtypeStruct((t // ATTN_TILE, D_ATTN, ATTN_TILE), BF16),
                 jax.ShapeDtypeStruct((t // ATTN_TILE, D_ATTN, ATTN_TILE), BF16),
                 jax.ShapeDtypeStruct((nb, D_ATTN, r), F32),
                 jax.ShapeDtypeStruct((nb, D_ATTN, r), F32),
                 jax.ShapeDtypeStruct((nb, N_HEADS, r), F32)]
        out_specs += [pl.BlockSpec((rows, D_ATTN), row_map),
                      pl.BlockSpec((nc, D_ATTN, ATTN_TILE), chunk_map),
                      pl.BlockSpec((nc, D_ATTN, ATTN_TILE), chunk_map),
                      pl.BlockSpec((1, D_ATTN, rows), seq_map),
                      pl.BlockSpec((1, D_ATTN, rows), seq_map),
                      pl.BlockSpec((1, N_HEADS, rows), seq_map)]
    else:
        outs += [jax.ShapeDtypeStruct((t, D_ATTN), F32)] * 3
        out_specs += [pl.BlockSpec((rows, D_ATTN), row_map)] * 3
    return pl.pallas_call(
        functools.partial(_inproj_kernel, transposed=transposed, q_scale=q_scale),
        grid=grid,
        in_specs=[pl.BlockSpec((nbb, rb, d), lambda i, j: (i, j, 0)),
                  pl.BlockSpec((nbb, 1, d), lambda i, j: (i, 0, 0)),
                  pl.BlockSpec((nbb, 1, d), lambda i, j: (i, 0, 0)),
                  pl.BlockSpec((1, 1, d), lambda i, j: (0, 0, 0)),
                  pl.BlockSpec((d, n_main), lambda i, j: (0, 0)),
                  pl.BlockSpec((3, D_ATTN, d), lambda i, j: (0, 0, 0)),
                  pl.BlockSpec((1, LANES), lambda i, j: (0, 0))],
        out_specs=out_specs,
        out_shape=outs,
        compiler_params=_cparams(("arbitrary", "arbitrary")),
    )(x3, sc, sh, g, w_main, w_qv, b_f)


def _crep_kernel(lf_ref, tri_ref, o_ref, carry_ref, *, scale):
    j = pl.program_id(1)

    @pl.when(j == 0)
    def _():
        carry_ref[...] = jnp.zeros_like(carry_ref)

    tri = tri_ref[...]
    cs = sum(jnp.dot(tri, p, preferred_element_type=F32) for p in _split3(lf_ref[...]))
    cs = cs + carry_ref[...]
    rows = cs.shape[0]
    carry_ref[...] = cs[rows - 1:rows, :]
    for h in range(N_HEADS):
        o_ref[0, h] = jnp.broadcast_to(cs[:, h:h + 1], (rows, LANES)) * (-scale)


def _crep(lf128, nb, length, scale):
    rows = min(length, 2 * ATTN_TILE)
    nj = length // rows
    tri = np.tril(np.ones((rows, rows), np.float32))
    return pl.pallas_call(
        functools.partial(_crep_kernel, scale=scale),
        grid=(nb, nj),
        in_specs=[pl.BlockSpec((rows, LANES), lambda i, j: (i * nj + j, 0)),
                  pl.BlockSpec((rows, rows), lambda i, j: (0, 0))],
        out_specs=pl.BlockSpec((1, N_HEADS, rows, LANES), lambda i, j: (i, 0, j, 0)),
        out_shape=jax.ShapeDtypeStruct((nb, N_HEADS, length, LANES), F32),
        scratch_shapes=[pltpu.VMEM((1, LANES), F32)],
        compiler_params=_cparams(("arbitrary", "arbitrary")),
    )(lf128, jnp.asarray(tri, BF16))


def _cumsum_kernel(x_ref, tri_ref, o_ref):
    tri = tri_ref[...]
    rb, n = x_ref.shape
    carry = jnp.zeros((rb, 1), F32)
    for c in range(n // LANES):
        lanes = slice(c * LANES, (c + 1) * LANES)
        cs = sum(jnp.dot(p, tri, preferred_element_type=F32) for p in _split3(x_ref[:, lanes])) + carry
        o_ref[:, lanes] = -cs
        carry = cs[:, LANES - 1:LANES]


def _neg_cumsum(x2, rb):
    r, n = x2.shape
    tri = np.triu(np.ones((LANES, LANES), np.float32))
    return pl.pallas_call(
        _cumsum_kernel,
        grid=(r // rb,),
        in_specs=[pl.BlockSpec((rb, n), lambda i: (i, 0)),
                  pl.BlockSpec((LANES, LANES), lambda i: (0, 0))],
        out_specs=pl.BlockSpec((rb, n), lambda i: (i, 0)),
        out_shape=jax.ShapeDtypeStruct((r, n), F32),
        compiler_params=_cparams(("arbitrary",)),
    )(x2, jnp.asarray(tri, BF16))


def _gelu_tanh(x):
    return 0.5 * x * (1.0 + jnp.tanh(math.sqrt(2.0 / math.pi) * (x + 0.044715 * (x * x * x))))


def _s5_kernel(u_ref, h0_ref, wb_ref, wc_ref, are_ref, aim_ref, dsk_ref, wglu_ref, bglu_ref, gout_ref,
               y_ref, ht_ref, s_ref, hc_ref, *, nseq, tm):
    rows = nseq * tm
    sr = rows + SUBLANES
    ti = pl.program_id(1)

    @pl.when(ti == 0)
    def _():
        hc_ref[...] = h0_ref[...]

    u = u_ref[...].reshape(rows, D_SSM)
    ub = u.astype(BF16)
    for c in range(4):
        bu = jnp.dot(ub[:, c * LANES:(c + 1) * LANES], wb_ref[c], preferred_element_type=F32)
        for jj in range(4):
            s_ref[pl.ds((4 * c + jj) * sr, rows), :] = bu[:, jj * LANES:(jj + 1) * LANES]
            s_ref[pl.ds((16 + 4 * c + jj) * sr, rows), :] = bu[:, 512 + jj * LANES:512 + (jj + 1) * LANES]

    ar = (are_ref[0:8, :], are_ref[8:16, :])
    ai = (aim_ref[0:8, :], aim_ref[8:16, :])

    def seq_group(sg, carry):
        base = sg * 4
        hs = []
        for b in range(4):
            hs.append(tuple(hc_ref[base + b, pl.ds(8 * q, 8), :] for q in range(4)))

        def step(t, hs):
            new = []
            for b in range(4):
                row = (base + b) * tm + t
                hr0, hr1, hi0, hi1 = hs[b]
                bre0 = s_ref[pl.ds(row, 8, stride=sr), :]
                bre1 = s_ref[pl.ds(8 * sr + row, 8, stride=sr), :]
                bim0 = s_ref[pl.ds(16 * sr + row, 8, stride=sr), :]
                bim1 = s_ref[pl.ds(24 * sr + row, 8, stride=sr), :]
                nr0 = ar[0] * hr0 - ai[0] * hi0 + bre0
                nr1 = ar[1] * hr1 - ai[1] * hi1 + bre1
                ni0 = ar[0] * hi0 + ai[0] * hr0 + bim0
                ni1 = ar[1] * hi1 + ai[1] * hr1 + bim1
                s_ref[pl.ds(row, 8, stride=sr), :] = nr0
                s_ref[pl.ds(8 * sr + row, 8, stride=sr), :] = nr1
                s_ref[pl.ds(16 * sr + row, 8, stride=sr), :] = ni0
                s_ref[pl.ds(24 * sr + row, 8, stride=sr), :] = ni1
                new.append((nr0, nr1, ni0, ni1))
            return tuple(new)

        hs = lax.fori_loop(0, tm, step, tuple(hs), unroll=S5_SCAN_UNROLL)
        for b in range(4):
            for q in range(4):
                hc_ref[base + b, pl.ds(8 * q, 8), :] = hs[b][q]
        return carry

    lax.fori_loop(0, nseq // 4, seq_group, 0)
    ht_ref[...] = hc_ref[...]

    ys = []
    for c in range(4):
        blocks = [s_ref[pl.ds((4 * c + jj) * sr, rows), :].astype(BF16) for jj in range(4)]
        blocks += [s_ref[pl.ds((16 + 4 * c + jj) * sr, rows), :].astype(BF16) for jj in range(4)]
        hcat = jnp.concatenate(blocks, axis=1)
        ys.append(jnp.dot(hcat, wc_ref[c], preferred_element_type=F32))
    y = jnp.concatenate(ys, axis=1) + dsk_ref[...] * u
    y = _gelu_tanh(y)
    gate = jax.nn.sigmoid(jnp.dot(y.astype(BF16), wglu_ref[...], preferred_element_type=F32) + bglu_ref[...])
    y = y * gate
    y_ref[...] = _rms(y, gout_ref[...]).astype(BF16).reshape(y_ref.shape)


def _s5(u3, h0, wb, wc, a_re, a_im, dsk, wglu, bglu, gout, nseq, tm):
    nb, length, _ = u3.shape
    rows = nseq * tm
    sr = rows + SUBLANES
    grid = (nb // nseq, length // tm)
    const2 = lambda i, j: (0, 0)
    const3 = lambda i, j: (0, 0, 0)
    if tm == length:
        y_spec = pl.BlockSpec((rows, D_SSM), lambda i, j: (i, 0))
        y_shape = jax.ShapeDtypeStruct((nb * length, D_SSM), BF16)
    else:
        y_spec = pl.BlockSpec((nseq, tm, D_SSM), lambda i, j: (i, j, 0))
        y_shape = jax.ShapeDtypeStruct((nb, length, D_SSM), BF16)
    return pl.pallas_call(
        functools.partial(_s5_kernel, nseq=nseq, tm=tm),
        grid=grid,
        in_specs=[pl.BlockSpec((nseq, tm, D_SSM), lambda i, j: (i, j, 0)),
                  pl.BlockSpec((nseq, 32, LANES), lambda i, j: (i, 0, 0)),
                  pl.BlockSpec((4, LANES, 1024), const3),
                  pl.BlockSpec((4, 1024, LANES), const3),
                  pl.BlockSpec((16, LANES), const2),
                  pl.BlockSpec((16, LANES), const2),
                  pl.BlockSpec((1, D_SSM), const2),
                  pl.BlockSpec((D_SSM, D_SSM), const2),
                  pl.BlockSpec((1, D_SSM), const2),
                  pl.BlockSpec((1, D_SSM), const2)],
        out_specs=[y_spec, pl.BlockSpec((nseq, 32, LANES), lambda i, j: (i, 0, 0))],
        out_shape=[y_shape, jax.ShapeDtypeStruct((nb, 32, LANES), F32)],
        scratch_shapes=[pltpu.VMEM((32 * sr, LANES), F32), pltpu.VMEM((nseq, 32, LANES), F32)],
        compiler_params=_cparams(("arbitrary", "arbitrary")),
    )(u3, h0, wb, wc, a_re, a_im, dsk, wglu, bglu, gout)


def _s5_params(lam_re, lam_im, log_step, b_re, b_im, c_re, c_im):
    g = lam_re.shape[0]
    dt = jnp.exp(log_step)[:, None]
    mag = jnp.exp(lam_re * dt)
    a_re = mag * jnp.cos(lam_im * dt)
    a_im = mag * jnp.sin(lam_im * dt)
    den = lam_re * lam_re + lam_im * lam_im
    n_re = a_re - 1.0
    f_re = (n_re * lam_re + a_im * lam_im) / den
    f_im = (a_im * lam_re - n_re * lam_im) / den
    bb_re = f_re[..., None] * b_re - f_im[..., None] * b_im
    bb_im = f_re[..., None] * b_im + f_im[..., None] * b_re
    eye = jnp.eye(g, dtype=F32)
    n_state = g * SSM_STATE

    def in_mat(bb):
        return jnp.einsum('gpi,gh->gihp', bb, eye).reshape(g * SSM_GROUP, n_state)

    def out_mat(cc):
        return jnp.einsum('gip,gh->gphi', cc, eye).reshape(n_state, g * SSM_GROUP)

    wbr, wbi = in_mat(bb_re), in_mat(bb_im)
    wcr, wci = out_mat(c_re), out_mat(-c_im)
    wb = jnp.stack([jnp.concatenate([wbr[c * 128:(c + 1) * 128, c * 512:(c + 1) * 512],
                                     wbi[c * 128:(c + 1) * 128, c * 512:(c + 1) * 512]], axis=1)
                    for c in range(4)]).astype(BF16)
    wc = jnp.stack([jnp.concatenate([wcr[c * 512:(c + 1) * 512, c * 128:(c + 1) * 128],
                                     wci[c * 512:(c + 1) * 512, c * 128:(c + 1) * 128]], axis=0)
                    for c in range(4)]).astype(BF16)
    return wb, wc, a_re.reshape(16, LANES), a_im.reshape(16, LANES)


def _prompt_attention_step(qi, qt_ref, k_ref, vt_ref, cr_ref, g_ref, o_ref, qz_ref, m_ref, l_ref, acc_ref,
                           st_ref):
    tq = ATTN_TILE
    half = LANES // 2

    rowid = lax.broadcasted_iota(I32, (LANES, tq), 0)
    for j in range(N_HEADS // 2):
        qp = qt_ref[0, j * LANES:(j + 1) * LANES, :]
        qz_ref[2 * j] = jnp.where(rowid < half, qp, jnp.zeros_like(qp))
        qz_ref[2 * j + 1] = jnp.where(rowid >= half, qp, jnp.zeros_like(qp))
    m_ref[...] = jnp.full_like(m_ref, NEG_INF)
    l_ref[...] = jnp.zeros_like(l_ref)
    acc_ref[...] = jnp.zeros_like(acc_ref)

    key_row = lax.broadcasted_iota(I32, (tq, tq), 0)
    q_col = lax.broadcasted_iota(I32, (tq, tq), 1)
    causal = key_row <= q_col

    def tiles(kbs, masked):
        m_new = [m_ref[h:h + 1, :] for h in range(N_HEADS)]
        for c, kb in enumerate(kbs):
            ks = pl.multiple_of(kb * tq, tq)
            for j in range(N_HEADS // 2):
                kp = k_ref[0, pl.ds(ks, tq), j * LANES:(j + 1) * LANES]
                for e in range(2):
                    h = 2 * j + e
                    st = jnp.dot(kp, qz_ref[h], preferred_element_type=F32)
                    bias = cr_ref[0, h, pl.ds(ks, tq), :]
                    st = st + jnp.concatenate([bias] * (tq // LANES), axis=1)
                    if masked:
                        st = jnp.where(causal, st, NEG_INF)
                    st_ref[c, h] = st
                    m_new[h] = jnp.maximum(m_new[h], jnp.max(st, axis=0, keepdims=True))
        for h in range(N_HEADS):
            alpha = jnp.exp2(m_ref[h:h + 1, :] - m_new[h])
            rows = slice(h * HEAD_DIM, (h + 1) * HEAD_DIM)
            l_new = alpha * l_ref[h:h + 1, :]
            acc = alpha * acc_ref[rows, :]
            for c, kb in enumerate(kbs):
                p = jnp.exp2(st_ref[c, h] - m_new[h])
                l_new = l_new + jnp.sum(p, axis=0, keepdims=True)
                acc = acc + jnp.dot(vt_ref[kb, rows, :], p.astype(BF16), preferred_element_type=F32)
            l_ref[h:h + 1, :] = l_new
            m_ref[h:h + 1, :] = m_new[h]
            acc_ref[rows, :] = acc

    def body(g, c):
        tiles([g * ATTN_GROUP + i for i in range(ATTN_GROUP)], False)
        return c

    n_groups = qi // ATTN_GROUP
    lax.fori_loop(0, n_groups, body, 0)
    for r in range(1, ATTN_GROUP):
        @pl.when(qi - n_groups * ATTN_GROUP == r)
        def _(r=r):
            tiles([n_groups * ATTN_GROUP + i for i in range(r)], False)
    tiles([qi], True)

    for h in range(N_HEADS):
        rows = slice(h * HEAD_DIM, (h + 1) * HEAD_DIM)
        acc_ref[rows, :] = acc_ref[rows, :] / l_ref[h:h + 1, :]
    o = acc_ref[...].T
    o_ref[0] = _rms(o, g_ref[...]).astype(BF16)


def _paged_seq_attention(q8, kn, vn, cn, g, kpages, vpages, n_pages, n_new):
    n_past = n_pages * PAGE
    d = q8.shape[-1]
    nr = N_HEADS * n_new
    new_bits = n_new.bit_length() - 1
    head_bits = HEAD_DIM.bit_length() - 1
    rowh = lax.shift_right_logical(lax.broadcasted_iota(I32, (nr, d), 0), new_bits)
    colh = lax.shift_right_logical(lax.broadcasted_iota(I32, (nr, d), 1), head_bits)
    bd = rowh == colh
    qrep = jnp.broadcast_to(q8[None], (N_HEADS, n_new, d)).reshape(nr, d)
    qbd = jnp.where(bd, qrep, 0.0).astype(BF16)
    cnr = jnp.broadcast_to(cn[:, None, :], (N_HEADS, n_new, cn.shape[-1])).reshape(nr, cn.shape[-1])

    s_p = []
    for p in range(n_pages):
        kt = kpages[p].reshape(d, PAGE)
        s_p.append(_bdot(qbd, kt) + cnr[:, p * PAGE:(p + 1) * PAGE])
    s_n = _bdot_nt(qbd, kn) + cnr[:, n_past:n_past + n_new]
    qpos = lax.broadcasted_iota(I32, (nr, n_new), 0) & (n_new - 1)
    kpos = lax.broadcasted_iota(I32, (nr, n_new), 1)
    s_n = jnp.where(kpos <= qpos, s_n, NEG_INF)
    m = jnp.max(s_n, axis=1, keepdims=True)
    for sp in s_p:
        m = jnp.maximum(m, jnp.max(sp, axis=1, keepdims=True))
    p_n = jnp.exp(s_n - m)
    l = jnp.sum(p_n, axis=1, keepdims=True)
    of = _bdot(p_n, vn)
    for p in range(n_pages):
        pp = jnp.exp(s_p[p] - m)
        l = l + jnp.sum(pp, axis=1, keepdims=True)
        of = of + _bdot_nt(pp, vpages[p].reshape(d, PAGE))
    of = jnp.where(bd, of / l, 0.0)
    o = jnp.sum(of.reshape(N_HEADS, n_new, d), axis=0)
    return _rms(o, g)


def _attention_kernel(pt_ref, qt_ref, k_ref, vt_ref, cr_ref, g_ref, qs_ref, kn_ref, vn_ref, cn_ref, kc_hbm, vc_hbm,
                      o_ref, os_ref, qz_ref, m_ref, l_ref, acc_ref, st_ref, kbuf, vbuf, sem, *, n_pages, n_seq):
    step = pl.program_id(0) * pl.num_programs(1) + pl.program_id(1)
    per_step = qs_ref.shape[0]
    n_new = qs_ref.shape[1]

    def page_copies(seq, slot):
        cps = []
        for p in range(n_pages):
            pg = pt_ref[seq * n_pages + p]
            cps.append(pltpu.make_async_copy(kc_hbm.at[pg], kbuf.at[slot, p], sem.at[0, slot]))
            cps.append(pltpu.make_async_copy(vc_hbm.at[pg], vbuf.at[slot, p], sem.at[1, slot]))
        return cps

    @pl.when(step == 0)
    def _():
        for slot in range(2):
            for cp in page_copies(slot, slot):
                cp.start()

    def sample_seq(j):
        seq = step * per_step + j
        slot = j % 2
        for cp in page_copies(seq, slot):
            cp.wait()
        os_ref[j] = _paged_seq_attention(qs_ref[j], kn_ref[j], vn_ref[j], cn_ref[j], g_ref[...],
                                         kbuf.at[slot], vbuf.at[slot], n_pages, n_new)

        @pl.when(seq + 2 < n_seq)
        def _():
            for cp in page_copies(seq + 2, slot):
                cp.start()

    for j in range(per_step // 2):
        sample_seq(j)
    _prompt_attention_step(pl.program_id(1), qt_ref, k_ref, vt_ref, cr_ref, g_ref, o_ref,
                           qz_ref, m_ref, l_ref, acc_ref, st_ref)
    for j in range(per_step // 2, per_step):
        sample_seq(j)


def _attention(qt, k, vt, crep, g, page_table, qs, kn, vn, cn, cache_k, cache_v):
    b, length, d = k.shape
    tq = ATTN_TILE
    nq = length // tq
    n_seq, n_new, _ = qs.shape
    assert n_new & (n_new - 1) == 0, "new-token count must be a power of two"
    per_step = n_seq // (b * nq)
    assert per_step % 2 == 0 and per_step * b * nq == n_seq
    n_pages = page_table.shape[1]
    once = pl.Buffered(1)
    smap = lambda i, j, pt: (i * nq + j, 0, 0)
    grid_spec = pltpu.PrefetchScalarGridSpec(
        num_scalar_prefetch=1,
        grid=(b, nq),
        in_specs=[pl.BlockSpec((1, d, tq), smap),
                  pl.BlockSpec((1, length, d), lambda i, j, pt: (i, 0, 0), pipeline_mode=once),
                  pl.BlockSpec((nq, d, tq), lambda i, j, pt: (i, 0, 0), pipeline_mode=once),
                  pl.BlockSpec((1, N_HEADS, length, LANES), lambda i, j, pt: (i, 0, 0, 0), pipeline_mode=once),
                  pl.BlockSpec((1, d), lambda i, j, pt: (0, 0)),
                  pl.BlockSpec((per_step, n_new, d), smap),
                  pl.BlockSpec((per_step, n_new, d), smap),
                  pl.BlockSpec((per_step, n_new, d), smap),
                  pl.BlockSpec((per_step, N_HEADS, cn.shape[-1]), smap),
                  pl.BlockSpec(memory_space=pl.ANY),
                  pl.BlockSpec(memory_space=pl.ANY)],
        out_specs=[pl.BlockSpec((1, tq, d), lambda i, j, pt: (i, j, 0)),
                   pl.BlockSpec((per_step, n_new, d), smap)],
        scratch_shapes=[pltpu.VMEM((N_HEADS, LANES, tq), BF16), pltpu.VMEM((N_HEADS, tq), F32),
                        pltpu.VMEM((N_HEADS, tq), F32), pltpu.VMEM((d, tq), F32),
                        pltpu.VMEM((ATTN_GROUP, N_HEADS, tq, tq), F32),
                        pltpu.VMEM((2, n_pages, N_HEADS, HEAD_DIM, PAGE), F32),
                        pltpu.VMEM((2, n_pages, N_HEADS, HEAD_DIM, PAGE), F32),
                        pltpu.SemaphoreType.DMA((2, 2))],
    )
    return pl.pallas_call(
        functools.partial(_attention_kernel, n_pages=n_pages, n_seq=n_seq),
        grid_spec=grid_spec,
        out_shape=[jax.ShapeDtypeStruct((b, length, d), BF16), jax.ShapeDtypeStruct((n_seq, n_new, d), F32)],
        compiler_params=pltpu.CompilerParams(dimension_semantics=("arbitrary", "arbitrary"),
                                             vmem_limit_bytes=ATTN_VMEM_LIMIT),
    )(page_table.reshape(-1), qt, k, vt, crep, g, qs, kn, vn, cn, cache_k, cache_v)


def _outproj_kernel(x_ref, ys_ref, oa_ref, gt_ref, sc_ref, sh_ref, g_ref, wo_ref, wrt_ref,
                    x1_ref, h2_ref, lg_ref):
    nbb, rb, d = x_ref.shape
    mix = jnp.dot(ys_ref[...], wo_ref[0:D_SSM, :], preferred_element_type=F32)
    mix = mix + jnp.dot(oa_ref[...], wo_ref[D_SSM:, :], preferred_element_type=F32)
    x1 = x_ref[...] + gt_ref[...] * mix.reshape(nbb, rb, d)
    x1_ref[...] = x1
    h2 = (_rms(x1, g_ref[...]) * (1.0 + sc_ref[...]) + sh_ref[...]).reshape(nbb * rb, d)
    hb = h2.astype(BF16)
    h2_ref[...] = _pack_bf16_pairs(hb)
    hlo = (h2 - hb.astype(F32)).astype(BF16)
    whi = wrt_ref[0]
    wlo = wrt_ref[1]
    nt = (((1,), (1,)), ((), ()))
    lg = lax.dot_general(whi, hb, nt, preferred_element_type=F32)
    lg = lg + lax.dot_general(wlo, hb, nt, preferred_element_type=F32)
    lg = lg + lax.dot_general(whi, hlo, nt, preferred_element_type=F32)
    lg_ref[...] = lg


def _outproj(x3, ys, oa, gt, sc, sh, g, wo, wrt, nbb, rb):
    nb, r, d = x3.shape
    t = nb * r
    rows = nbb * rb
    nj = r // rb
    row_map = lambda i, j: (i * nj + j, 0)
    mod_spec = pl.BlockSpec((nbb, 1, d), lambda i, j: (i, 0, 0))
    return pl.pallas_call(
        _outproj_kernel,
        grid=(nb // nbb, nj),
        in_specs=[pl.BlockSpec((nbb, rb, d), lambda i, j: (i, j, 0)),
                  pl.BlockSpec((rows, D_SSM), row_map),
                  pl.BlockSpec((rows, D_ATTN), row_map),
                  mod_spec, mod_spec, mod_spec,
                  pl.BlockSpec((1, 1, d), lambda i, j: (0, 0, 0)),
                  pl.BlockSpec((D_SSM + D_ATTN, d), lambda i, j: (0, 0)),
                  pl.BlockSpec((2, N_EXPERTS, d), lambda i, j: (0, 0, 0))],
        out_specs=[pl.BlockSpec((nbb, rb, d), lambda i, j: (i, j, 0)),
                   pl.BlockSpec((rows, d // 2), row_map),
                   pl.BlockSpec((N_EXPERTS, rows), lambda i, j: (0, i * nj + j))],
        out_shape=[jax.ShapeDtypeStruct((nb, r, d), F32),
                   jax.ShapeDtypeStruct((t, d // 2), jnp.uint32),
                   jax.ShapeDtypeStruct((N_EXPERTS, t), F32)],
        compiler_params=_cparams(("arbitrary", "arbitrary")),
    )(x3, ys, oa, gt, sc, sh, g, wo, wrt)


def _router_kernel(lg_ref, rb_ref, ut_ref, idx_ref, w_ref, rank_ref, cnt_ref, carry_ref):
    i = pl.program_id(0)
    tm = lg_ref.shape[1]
    per_group = N_EXPERTS // N_EXPERT_GROUPS

    @pl.when(i == 0)
    def _():
        carry_ref[...] = jnp.zeros_like(carry_ref)

    scores = jax.nn.sigmoid(lg_ref[...])
    biased = scores + rb_ref[...]
    blks, grp = [], []
    for g in range(N_EXPERT_GROUPS):
        blk = biased[g * per_group:(g + 1) * per_group, :]
        m1 = jnp.max(blk, axis=0, keepdims=True)
        eq = blk == m1
        n_eq = jnp.sum(jnp.where(eq, 1.0, 0.0), axis=0, keepdims=True)
        m2 = jnp.max(jnp.where(eq, NEG_INF, blk), axis=0, keepdims=True)
        blks.append(blk)
        grp.append(m1 + jnp.where(n_eq >= 2.0, m1, m2))
    masked = []
    for g in range(N_EXPERT_GROUPS):
        beaten = jnp.zeros((1, tm), F32)
        for o in range(N_EXPERT_GROUPS):
            if o == g:
                continue
            ahead = (grp[o] >= grp[g]) if o < g else (grp[o] > grp[g])
            beaten = beaten + jnp.where(ahead, 1.0, 0.0)
        masked.append(jnp.where(beaten < float(TOPK_GROUPS), blks[g], NEG_INF))
    work = jnp.concatenate(masked, axis=0)

    eid = lax.broadcasted_iota(I32, (N_EXPERTS, tm), 0)
    chosen = jnp.zeros((N_EXPERTS, tm), F32)
    idxs, ws, sels = [], [], []
    for _ in range(TOP_K):
        m = jnp.max(work, axis=0, keepdims=True)
        first = jnp.min(jnp.where(work == m, eid, N_EXPERTS), axis=0, keepdims=True)
        sel = eid == first
        idxs.append(first)
        ws.append(jnp.sum(jnp.where(sel, scores, 0.0), axis=0, keepdims=True))
        sels.append(sel)
        chosen = jnp.where(sel, 1.0, chosen)
        work = jnp.where(sel, NEG_INF, work)
    wsum = ws[0]
    for wk in ws[1:]:
        wsum = wsum + wk

    prefix = jnp.dot(chosen.astype(BF16), ut_ref[...], preferred_element_type=F32) + carry_ref[...]
    carry_ref[...] = carry_ref[...] + jnp.sum(chosen, axis=1, keepdims=True)
    cnt_ref[...] = carry_ref[...]

    idx_ref[...] = jnp.zeros_like(idx_ref)
    w_ref[...] = jnp.zeros_like(w_ref)
    rank_ref[...] = jnp.zeros_like(rank_ref)
    for k in range(TOP_K):
        idx_ref[k:k + 1, :] = idxs[k]
        w_ref[k:k + 1, :] = ws[k] / wsum * ROUTED_SCALE
        rank = jnp.sum(jnp.where(sels[k], prefix, 0.0), axis=0, keepdims=True)
        rank_ref[k:k + 1, :] = rank.astype(I32)


def _router(lg, router_bias, tm):
    e, t = lg.shape
    ut = np.triu(np.ones((tm, tm), np.float32), 1)
    tok_spec = pl.BlockSpec((SUBLANES, tm), lambda i: (0, i))
    return pl.pallas_call(
        _router_kernel,
        grid=(t // tm,),
        in_specs=[pl.BlockSpec((e, tm), lambda i: (0, i)),
                  pl.BlockSpec((e, 1), lambda i: (0, 0)),
                  pl.BlockSpec((tm, tm), lambda i: (0, 0))],
        out_specs=[tok_spec, tok_spec, tok_spec, pl.BlockSpec((e, 1), lambda i: (0, 0))],
        out_shape=[jax.ShapeDtypeStruct((SUBLANES, t), I32), jax.ShapeDtypeStruct((SUBLANES, t), F32),
                   jax.ShapeDtypeStruct((SUBLANES, t), I32), jax.ShapeDtypeStruct((e, 1), F32)],
        scratch_shapes=[pltpu.VMEM((e, 1), F32)],
        compiler_params=_cparams(("arbitrary",)),
    )(lg, router_bias.reshape(e, 1), jnp.asarray(ut, BF16))


def _experts_kernel(be_ref, nv_ref, new_ref, fill_ref, x_ref, wg_ref, wu_ref, wd_ref, o_ref, wgu_s, wd_s):
    i = pl.program_id(0)
    de = wg_ref.shape[-1]
    valid = i < nv_ref[0]

    @pl.when(valid & (new_ref[i] == 1))
    def _():
        wgu_s[:, :de] = wg_ref[0].astype(BF16)
        wgu_s[:, de:] = wu_ref[0].astype(BF16)
        wd_s[...] = wd_ref[0].astype(BF16)

    @pl.when(valid)
    def _():
        row = lax.broadcasted_iota(I32, x_ref.shape, 0)
        x = jnp.where(row < fill_ref[i], x_ref[...], jnp.uint32(0))
        gu = jnp.dot(_unpack_bf16_pairs(x), wgu_s[...], preferred_element_type=F32)
        g = gu[:, :de]
        a = (g * jax.nn.sigmoid(g) * gu[:, de:]).astype(BF16)
        o = jnp.dot(a, wd_s[...], preferred_element_type=F32)
        o_ref[...] = _pack_bf16_pairs(o.astype(BF16))

    @pl.when(jnp.logical_not(valid))
    def _():
        o_ref[...] = jnp.zeros_like(o_ref)


def _experts(block_e, n_valid, block_new, block_fill, xs, w_eg, w_eu, w_ed, m):
    n_rows = xs.shape[0]
    d, de = w_eg.shape[-2:]
    nblk = n_rows // m
    xmap = lambda i, be, nv, *_: (jnp.minimum(i, nv[0] - 1), 0)
    wmap = lambda i, be, *_: (be[i], 0, 0)
    grid_spec = pltpu.PrefetchScalarGridSpec(
        num_scalar_prefetch=4,
        grid=(nblk,),
        in_specs=[pl.BlockSpec((m, xs.shape[1]), xmap),
                  pl.BlockSpec((1, d, de), wmap),
                  pl.BlockSpec((1, d, de), wmap),
                  pl.BlockSpec((1, de, d), wmap)],
        out_specs=pl.BlockSpec((m, d // 2), lambda i, *_: (i, 0)),
        scratch_shapes=[pltpu.VMEM((d, 2 * de), BF16), pltpu.VMEM((de, d), BF16)],
    )
    return pl.pallas_call(
        _experts_kernel,
        grid_spec=grid_spec,
        out_shape=jax.ShapeDtypeStruct((n_rows, d // 2), jnp.uint32),
        compiler_params=_cparams(("arbitrary",)),
    )(block_e, n_valid, block_new, block_fill, xs, w_eg, w_eu, w_ed)


def _gather_rows(table, idx, window):
    n = idx.shape[0]
    d = table.shape[1]
    info = plsc.get_sparse_core_info()
    n_workers = info.num_cores * info.num_subcores
    per_worker = n // n_workers
    n_buf = SC_GATHER_BUFFERS
    n_rounds = per_worker // (window * n_buf)
    assert n_rounds * window * n_buf * n_workers == n
    mesh = plsc.VectorSubcoreMesh(core_axis_name="c", subcore_axis_name="s")

    @functools.partial(
        pl.kernel, out_type=jax.ShapeDtypeStruct((n, d), table.dtype), mesh=mesh,
        scratch_types=[pltpu.VMEM((per_worker,), I32), pltpu.VMEM((n_buf, window, d), table.dtype),
                       pltpu.SemaphoreType.DMA((n_buf,)), pltpu.SemaphoreType.DMA((n_buf,))])
    def gather(x_hbm, i_hbm, o_hbm, i_v, buf, gsem, wsem):
        base = (lax.axis_index("s") * info.num_cores + lax.axis_index("c")) * per_worker
        pltpu.sync_copy(i_hbm.at[pl.ds(base, per_worker)], i_v)

        @pl.loop(0, n_rounds)
        def _(rnd):
            first = rnd * (window * n_buf)
            reads, writes = [], []
            for b in range(n_buf):
                rows = pl.ds(first + b * window, window)
                reads.append(pltpu.make_async_copy(x_hbm.at[i_v.at[rows]], buf.at[b], gsem.at[b]))
                writes.append(pltpu.make_async_copy(buf.at[b], o_hbm.at[pl.ds(base + first + b * window, window)],
                                                    wsem.at[b]))
            for cp in reads:
                cp.start()
            for b in range(n_buf):
                reads[b].wait()
                writes[b].start()
            for cp in writes:
                cp.wait()

    return gather(table, idx)


def _scatter_rows(groups, dest_t, n_rows, window):
    d = groups[0].shape[1]
    dtype = groups[0].dtype
    info = plsc.get_sparse_core_info()
    n_workers = info.num_cores * info.num_subcores
    n_buf = 2
    plan, idx_parts, first = [], [], 0
    row0 = 0
    for x in groups:
        t = x.shape[0]
        per_worker = t // n_workers
        chunks = per_worker // window
        assert chunks % n_buf == 0 and chunks * window * n_workers == t
        idx = dest_t[:TOP_K, first:first + t].reshape(TOP_K, n_workers, chunks, window).transpose(1, 0, 2, 3)
        idx_parts.append(idx.reshape(n_workers, TOP_K * chunks, window))
        plan.append((per_worker, chunks, row0))
        row0 += TOP_K * chunks
        first += t
    idx_rows = -(-row0 // SUBLANES) * SUBLANES
    idx = jnp.concatenate(idx_parts + [jnp.zeros((n_workers, idx_rows - row0, window), I32)], axis=1)
    idx = idx.reshape(n_workers * idx_rows, window)
    mesh = plsc.VectorSubcoreMesh(core_axis_name="c", subcore_axis_name="s")

    @functools.partial(
        pl.kernel, out_type=jax.ShapeDtypeStruct((n_rows, d), dtype), mesh=mesh,
        scratch_types=[pltpu.VMEM((idx_rows, window), I32), pltpu.VMEM((n_buf, window, d), dtype),
                       pltpu.SemaphoreType.DMA((n_buf,)), pltpu.SemaphoreType.DMA((n_buf,))])
    def scatter(*refs):
        x_hbms = refs[:len(groups)]
        i_hbm, o_hbm, i_v, buf, rsem, wsem = refs[len(groups):]
        wid = lax.axis_index("s") * info.num_cores + lax.axis_index("c")
        pltpu.sync_copy(i_hbm.at[pl.ds(wid * idx_rows, idx_rows)], i_v)

        for x_hbm, (per_worker, chunks, row0) in zip(x_hbms, plan):
            base = wid * per_worker

            @pl.loop(0, chunks // n_buf)
            def _(rnd, x_hbm=x_hbm, base=base, chunks=chunks, row0=row0):
                reads = [pltpu.make_async_copy(x_hbm.at[pl.ds(base + (rnd * n_buf + b) * window, window)],
                                               buf.at[b], rsem.at[b]) for b in range(n_buf)]
                for cp in reads:
                    cp.start()
                writes = []
                for b in range(n_buf):
                    reads[b].wait()
                    for k in range(TOP_K):
                        rows = i_v.at[row0 + k * chunks + rnd * n_buf + b]
                        cp = pltpu.make_async_copy(buf.at[b], o_hbm.at[rows], wsem.at[b])
                        cp.start()
                        writes.append(cp)
                for cp in writes:
                    cp.wait()

    return scatter(*groups, idx)


def _final_kernel(x1_ref, h2_ref, w_ref, gt_ref, sc_ref, sh_ref, g_ref, wgu_ref, wd_ref, *rest):
    yg_refs = rest[:TOP_K]
    y_ref = rest[TOP_K]
    nbb, rb, d = x1_ref.shape
    ds = wd_ref.shape[0]
    gu = jnp.dot(_unpack_bf16_pairs(h2_ref[...]), wgu_ref[...], preferred_element_type=F32)
    g = gu[:, :ds]
    a = (g * jax.nn.sigmoid(g) * gu[:, ds:]).astype(BF16)
    ff = jnp.dot(a, wd_ref[...], preferred_element_type=F32)
    wts = w_ref[...].T
    routed_lo = routed_hi = None
    for k in range(TOP_K):
        yk = yg_refs[k][...]
        w = wts[:, k:k + 1]
        lo = lax.bitcast_convert_type(lax.shift_left(yk, jnp.uint32(16)), F32) * w
        hi = lax.bitcast_convert_type(yk & jnp.uint32(0xFFFF0000), F32) * w
        routed_lo = lo if k == 0 else routed_lo + lo
        routed_hi = hi if k == 0 else routed_hi + hi
    ff = jnp.concatenate([routed_lo, routed_hi], axis=1) + ff
    x2 = x1_ref[...] + gt_ref[...] * ff.reshape(nbb, rb, d)
    y_ref[...] = _rms(x2, g_ref[...]) * (1.0 + sc_ref[...]) + sh_ref[...]


def _final(x1, h2, wts, yg, first_tile, gt, sc, sh, g, wgu, wd, nbb, rb):
    nb, r, d = x1.shape
    rows = nbb * rb
    nj = r // rb
    tiles_all = yg.shape[0] // TOP_K // rows
    row_map = lambda i, j: (i * nj + j, 0)
    mod_spec = pl.BlockSpec((nbb, 1, d), lambda i, j: (i, 0, 0))
    slab_specs = [pl.BlockSpec((rows, yg.shape[1]), lambda i, j, k=k: (k * tiles_all + first_tile + i * nj + j, 0))
                  for k in range(TOP_K)]
    return pl.pallas_call(
        _final_kernel,
        grid=(nb // nbb, nj),
        in_specs=[pl.BlockSpec((nbb, rb, d), lambda i, j: (i, j, 0)),
                  pl.BlockSpec((rows, h2.shape[1]), row_map),
                  pl.BlockSpec((SUBLANES, rows), lambda i, j: (0, first_tile + i * nj + j)),
                  mod_spec, mod_spec, mod_spec,
                  pl.BlockSpec((1, 1, d), lambda i, j: (0, 0, 0)),
                  pl.BlockSpec(wgu.shape, lambda i, j: (0, 0)),
                  pl.BlockSpec(wd.shape, lambda i, j: (0, 0))] + slab_specs,
        out_specs=pl.BlockSpec((nbb, rb, d), lambda i, j: (i, j, 0)),
        out_shape=jax.ShapeDtypeStruct((nb, r, d), F32),
        compiler_params=_cparams(("arbitrary", "arbitrary")),
    )(x1, h2, wts, gt, sc, sh, g, wgu, wd, *([yg] * TOP_K))


def _moe(h2_groups, lg, router_bias, w_eg, w_eu, w_ed, tile):
    t = lg.shape[1]
    e = N_EXPERTS
    m = MOE_ROWS
    idx_t, w_t, rank_t, counts = _router(lg, router_bias, tile)
    counts = counts.reshape(e).astype(I32)
    padded = ((counts + m - 1) // m) * m
    pad_end = jnp.cumsum(padded)
    pad_start = pad_end - padded
    onehot = idx_t[:, :, None] == jnp.arange(e, dtype=I32)
    dest_t = jnp.sum(jnp.where(onehot, pad_start, 0), axis=-1) + rank_t
    n_rows = (-(-(t * TOP_K) // m)) * m + e * m
    nblk = n_rows // m
    block_start = jnp.arange(nblk, dtype=I32) * m
    block_e = jnp.minimum(jnp.sum(pad_end[None, :] <= block_start[:, None], axis=1), e - 1).astype(I32)
    block_new = jnp.concatenate([jnp.ones((1,), I32), (block_e[1:] != block_e[:-1]).astype(I32)])
    n_valid = (pad_end[-1] // m).astype(I32).reshape(1)
    of_block = block_e[:, None] == jnp.arange(e, dtype=I32)
    count_b = jnp.sum(jnp.where(of_block, counts, 0), axis=1)
    start_b = jnp.sum(jnp.where(of_block, pad_start, 0), axis=1)
    block_fill = jnp.clip(count_b - (block_start - start_b), 0, m).astype(I32)
    xs = _scatter_rows(h2_groups, dest_t, n_rows, SC_SCATTER_WINDOW)
    yb = _experts(block_e, n_valid, block_new, block_fill, xs, w_eg, w_eu, w_ed, m)
    yg = _gather_rows(yb, dest_t[:TOP_K].reshape(-1), SC_GATHER_WINDOW)
    return w_t, yg


def kernel(x_prompt, x_sample, c_prompt, c_sample, cache_k, cache_v, cache_logf, state_ssm_re, state_ssm_im, page_table, w_ada, b_ada, g_norm1, w_in, b_fgate, ssm_lambda_re, ssm_lambda_im, ssm_log_step, ssm_b_re, ssm_b_im, ssm_c_re, ssm_c_im, ssm_d, w_glu, b_glu, g_ssm_out, g_attn_out, w_out, g_norm2, w_router, router_bias, w_exp_gate, w_exp_up, w_exp_down, w_sh_gate, w_sh_up, w_sh_down, g_final, w_ada_final, b_ada_final):
    depth = w_ada.shape[0]
    assert depth == 1, "one layer is supported"
    bp, lp, d = x_prompt.shape
    bs, ls, _ = x_sample.shape
    n_pages = page_table.shape[1]
    n_past = n_pages * PAGE
    n_groups = ssm_lambda_re.shape[1]

    n_c = bp + bs
    n_c_pad = -(-n_c // SUBLANES) * SUBLANES
    c_all = jnp.concatenate([c_prompt, c_sample, jnp.zeros((n_c_pad - n_c, d), F32)], axis=0)
    mod = _adaln(c_all, w_ada[0], b_ada[0])
    modf = _adaln(c_all, w_ada_final, b_ada_final)

    def mods(lo, hi):
        parts = [mod[lo:hi, k * d:(k + 1) * d][:, None, :] for k in range(6)]
        parts += [modf[lo:hi, k * d:(k + 1) * d][:, None, :] for k in range(2)]
        return parts

    w_u, w_q, w_k, w_v, w_f = jnp.split(w_in[0], [D_SSM, D_SSM + D_ATTN, D_SSM + 2 * D_ATTN,
                                                   D_SSM + 3 * D_ATTN], axis=1)
    w_fpad = jnp.concatenate([w_f, jnp.zeros((d, LANES - N_HEADS), F32)], axis=1)
    w_main_s = jnp.concatenate([w_u, w_k, w_v, w_fpad], axis=1).astype(BF16)
    w_main_p = jnp.concatenate([w_u, w_k, w_fpad], axis=1).astype(BF16)
    w_qkv = jnp.stack([w_q.T, w_k.T, w_v.T]).astype(BF16)
    b_f = jnp.concatenate([b_fgate[0], jnp.zeros((LANES - N_HEADS,), F32)]).reshape(1, LANES)
    g1 = g_norm1[0].reshape(1, 1, d)
    g2 = g_norm2[0].reshape(1, 1, d)
    gf = g_final.reshape(1, 1, d)
    wb, wc, a_re, a_im = _s5_params(ssm_lambda_re[0], ssm_lambda_im[0], ssm_log_step[0], ssm_b_re[0],
                                    ssm_b_im[0], ssm_c_re[0], ssm_c_im[0])
    dsk = ssm_d[0].reshape(1, D_SSM)
    wglu = w_glu[0].astype(BF16)
    bglu = b_glu[0].reshape(1, D_SSM)
    g_so = g_ssm_out[0].reshape(1, D_SSM)
    g_ao = g_attn_out[0].reshape(1, D_ATTN)
    wo = w_out[0].astype(BF16)
    wr_t = w_router[0].T
    wr_hi = wr_t.astype(BF16)
    wrt = jnp.stack([wr_hi, (wr_t - wr_hi.astype(F32)).astype(BF16)])
    wgu = jnp.concatenate([w_sh_gate[0], w_sh_up[0]], axis=1).astype(BF16)
    wsd = w_sh_down[0].astype(BF16)

    def ssm_state(re, im):
        return jnp.concatenate([re.reshape(-1, 16, LANES), im.reshape(-1, 16, LANES)], axis=1)

    def split_state(ht):
        n = ht.shape[0]
        return (ht[:, :16].reshape(1, n, n_groups, SSM_STATE), ht[:, 16:].reshape(1, n, n_groups, SSM_STATE))

    tm = ROW_TILE
    sh1_p, sc1_p, gt1_p, sh2_p, sc2_p, gt2_p, shf_p, scf_p = mods(0, bp)
    u, lf, kb, qt, vt, ktf, vtf, lft = _inproj(x_prompt, sc1_p, sh1_p, g1, w_main_p, w_qkv, b_f, 1, tm, True,
                                               HEAD_DIM ** -0.5 * LOG2E)
    crep = _crep(lf, bp, lp, LOG2E)
    ys_p, ht = _s5(u.reshape(bp, lp, D_SSM), jnp.zeros((bp, 32, LANES), F32), wb, wc, a_re, a_im, dsk,
                   wglu, bglu, g_so, bp, S5_TIME_TILE)
    k_prompt = ktf.reshape(bp, N_HEADS, HEAD_DIM, lp).transpose(0, 3, 1, 2)[None]
    v_prompt = vtf.reshape(bp, N_HEADS, HEAD_DIM, lp).transpose(0, 3, 1, 2)[None]
    logf_prompt = lft.transpose(0, 2, 1)[None]
    sre_p, sim_p = split_state(ht)

    nbb = ROW_TILE // ls
    sh1_s, sc1_s, gt1_s, sh2_s, sc2_s, gt2_s, shf_s, scf_s = mods(bp, bp + bs)
    u, lf, k, v, q = _inproj(x_sample, sc1_s, sh1_s, g1, w_main_s, w_qkv, b_f, nbb, ls, False, HEAD_DIM ** -0.5)
    ys_s, ht = _s5(u.reshape(bs, ls, D_SSM), ssm_state(state_ssm_re[0], state_ssm_im[0]), wb, wc, a_re, a_im,
                   dsk, wglu, bglu, g_so, S5_SAMPLE_SEQS, ls)
    lf_past = cache_logf[0][page_table].reshape(bs, n_past, N_HEADS).transpose(0, 2, 1)
    lf_new = lf[:, :N_HEADS].reshape(bs, ls, N_HEADS).transpose(0, 2, 1)
    n_key_pad = -(-(n_past + ls) // LANES) * LANES
    lf_all = jnp.concatenate([lf_past, lf_new, jnp.zeros((bs, N_HEADS, n_key_pad - n_past - ls), F32)], axis=2)
    cn_s = _neg_cumsum(lf_all.reshape(bs * N_HEADS, n_key_pad), PREFIX_ROWS).reshape(bs, N_HEADS, n_key_pad)

    oa_p, oa_s = _attention(qt, kb.reshape(bp, lp, D_ATTN), vt, crep, g_ao, page_table,
                            q.reshape(bs, ls, D_ATTN), k.reshape(bs, ls, D_ATTN), v.reshape(bs, ls, D_ATTN),
                            cn_s, cache_k[0].transpose(0, 2, 3, 1), cache_v[0].transpose(0, 2, 3, 1))
    x1_p, h2_p, lg_p = _outproj(x_prompt, ys_p.reshape(bp * lp, D_SSM), oa_p.reshape(bp * lp, D_ATTN),
                                gt1_p, sc2_p, sh2_p, g2, wo, wrt, 1, tm)
    x1_s, h2_s, lg_s = _outproj(x_sample, ys_s, oa_s.reshape(bs * ls, D_ATTN).astype(BF16),
                                gt1_s, sc2_s, sh2_s, g2, wo, wrt, nbb, ls)

    assert nbb * ls == tm
    wts, yg = _moe([h2_p, h2_s], jnp.concatenate([lg_p, lg_s], axis=1), router_bias[0],
                   w_exp_gate[0], w_exp_up[0], w_exp_down[0], tm)
    tiles_p = bp * lp // tm
    y_prompt = _final(x1_p, h2_p, wts, yg, 0, gt2_p, scf_p, shf_p, gf, wgu, wsd, 1, tm)
    y_sample = _final(x1_s, h2_s, wts, yg, tiles_p, gt2_s, scf_s, shf_s, gf, wgu, wsd, nbb, ls)
    k_sample = k.reshape(1, bs, ls, N_HEADS, HEAD_DIM)
    v_sample = v.reshape(1, bs, ls, N_HEADS, HEAD_DIM)
    logf_sample = lf[:, :N_HEADS].reshape(1, bs, ls, N_HEADS)
    sre_s, sim_s = split_state(ht)

    return (y_prompt, y_sample, k_prompt, v_prompt, logf_prompt, sre_p, sim_p,
            k_sample, v_sample, logf_sample, sre_s, sim_s)
```

```python
import functools
import math

import jax
import jax.numpy as jnp
import numpy as np
from jax import lax
from jax.experimental import pallas as pl
from jax.experimental.pallas import tpu as pltpu
from jax.experimental.pallas import tpu_sc as plsc

F32 = jnp.float32
BF16 = jnp.bfloat16
I32 = jnp.int32

EPS = 1e-6
HEAD_DIM = 64
N_HEADS = 8
D_SSM = 512
D_ATTN = 512
SSM_GROUP = 16
SSM_STATE = 64
N_EXPERTS = 64
TOP_K = 6
N_EXPERT_GROUPS = 8
TOPK_GROUPS = 4
ROUTED_SCALE = 2.5
PAGE = 128

LANES = 128
SUBLANES = 8
VMEM_LIMIT = 48 * 1024 * 1024
ATTN_VMEM_LIMIT = 58 * 1024 * 1024
ROW_TILE = 512
S5_TIME_TILE = 256
S5_SAMPLE_SEQS = 32
S5_SCAN_UNROLL = 4
PREFIX_ROWS = 256
MOE_ROWS = 512
ATTN_TILE = 256
ATTN_GROUP = 4
LOG2E = math.log2(math.e)
SC_SCATTER_WINDOW = 16
SC_GATHER_WINDOW = 48
SC_GATHER_BUFFERS = 4
NEG_INF = float("-inf")


def _cparams(sem):
    return pltpu.CompilerParams(dimension_semantics=sem, vmem_limit_bytes=VMEM_LIMIT)


def _bdot(a, b):
    return jnp.dot(a.astype(BF16), b.astype(BF16), preferred_element_type=F32)


def _bdot_nt(a, b):
    return lax.dot_general(a.astype(BF16), b.astype(BF16), (((1,), (1,)), ((), ())),
                           preferred_element_type=F32)


def _split3(v):
    hi = v.astype(BF16)
    r1 = v - hi.astype(F32)
    mid = r1.astype(BF16)
    lo = (r1 - mid.astype(F32)).astype(BF16)
    return hi, mid, lo


def _pack_bf16_pairs(xb):
    n = xb.shape[1] // 2
    bits = lax.bitcast_convert_type(xb.astype(F32), jnp.uint32)
    return lax.shift_right_logical(bits[:, :n], jnp.uint32(16)) | bits[:, n:]


def _unpack_bf16_pairs(xp):
    lo = lax.bitcast_convert_type(lax.shift_left(xp, jnp.uint32(16)), F32)
    hi = lax.bitcast_convert_type(xp & jnp.uint32(0xFFFF0000), F32)
    return jnp.concatenate([lo, hi], axis=1).astype(BF16)


def _rms(x, g):
    return x * lax.rsqrt(jnp.mean(x * x, axis=-1, keepdims=True) + EPS) * g


def _adaln_kernel(c_ref, w_ref, b_ref, o_ref):
    c = c_ref[...]
    s = c * jax.nn.sigmoid(c)
    o_ref[...] = _bdot(s, w_ref[...]) + b_ref[...]


def _adaln(c, w, b):
    m, k = c.shape
    n = w.shape[1]
    tn = 1024
    return pl.pallas_call(
        _adaln_kernel,
        grid=(n // tn,),
        in_specs=[pl.BlockSpec((m, k), lambda j: (0, 0)),
                  pl.BlockSpec((k, tn), lambda j: (0, j)),
                  pl.BlockSpec((1, tn), lambda j: (0, j))],
        out_specs=pl.BlockSpec((m, tn), lambda j: (0, j)),
        out_shape=jax.ShapeDtypeStruct((m, n), F32),
        compiler_params=_cparams(("arbitrary",)),
    )(c, w, b.reshape(1, n))


def _inproj_kernel(x_ref, sc_ref, sh_ref, g_ref, w_ref, wqv_ref, bf_ref, *out_refs, transposed, q_scale):
    nbb, rb, d = x_ref.shape
    rows = nbb * rb
    x = x_ref[...]
    h = _rms(x, g_ref[...]) * (1.0 + sc_ref[...]) + sh_ref[...]
    hb = h.reshape(rows, d).astype(BF16)
    proj = jnp.dot(hb, w_ref[...], preferred_element_type=F32)
    u_ref, lf_ref = out_refs[:2]
    u_ref[...] = proj[:, :D_SSM]
    k = proj[:, D_SSM:D_SSM + D_ATTN]
    z = proj[:, proj.shape[1] - LANES:] + bf_ref[...]
    lf = jnp.minimum(z, 0.0) - jnp.log1p(jnp.exp(-jnp.abs(z)))
    lf_ref[...] = lf
    nt = (((1,), (1,)), ((), ()))
    if transposed:
        kb_ref, qt_ref, vt_ref, ktf_ref, vtf_ref, lft_ref = out_refs[2:]
        lft_ref[0] = lf.T[:N_HEADS]
        kb_ref[...] = k.astype(BF16)
        qt = (lax.dot_general(wqv_ref[0], hb, nt, preferred_element_type=F32) * q_scale).astype(BF16)
        ktf_ref[0] = lax.dot_general(wqv_ref[1], hb, nt, preferred_element_type=F32)
        vt = lax.dot_general(wqv_ref[2], hb, nt, preferred_element_type=F32)
        vtf_ref[0] = vt
        vt = vt.astype(BF16)
        for c in range(rows // ATTN_TILE):
            qt_ref[c] = qt[:, c * ATTN_TILE:(c + 1) * ATTN_TILE]
            vt_ref[c] = vt[:, c * ATTN_TILE:(c + 1) * ATTN_TILE]
    else:
        k_ref, v_ref, q_ref = out_refs[2:]
        k_ref[...] = k
        v_ref[...] = proj[:, D_SSM + D_ATTN:D_SSM + 2 * D_ATTN]
        q_ref[...] = lax.dot_general(hb, wqv_ref[0], nt, preferred_element_type=F32) * q_scale


def _inproj(x3, sc, sh, g, w_main, w_qv, b_f, nbb, rb, transposed, q_scale):
    nb, r, d = x3.shape
    t = nb * r
    rows = nbb * rb
    nj = r // rb
    grid = (nb // nbb, nj)
    row_map = lambda i, j: (i * nj + j, 0)
    n_main = w_main.shape[1]
    outs = [jax.ShapeDtypeStruct((t, D_SSM), F32), jax.ShapeDtypeStruct((t, LANES), F32)]
    out_specs = [pl.BlockSpec((rows, D_SSM), row_map), pl.BlockSpec((rows, LANES), row_map)]
    if transposed:
        assert nbb == 1, "transposed outputs are laid out per sequence"
        nc = rows // ATTN_TILE
        chunk_map = lambda i, j: (i * nj + j, 0, 0)
        seq_map = lambda i, j: (i, 0, j)
        outs += [jax.ShapeDtypeStruct((t, D_ATTN), BF16),
                 jax.ShapeDtypeStruct((t // ATTN_TILE, D_ATTN, ATTN_TILE), BF16),
                 jax.ShapeDtypeStruct((t // ATTN_TILE, D_ATTN, ATTN_TILE), BF16),
                 jax.ShapeDtypeStruct((nb, D_ATTN, r), F32),
                 jax.ShapeDtypeStruct((nb, D_ATTN, r), F32),
                 jax.ShapeDtypeStruct((nb, N_HEADS, r), F32)]
        out_specs += [pl.BlockSpec((rows, D_ATTN), row_map),
                      pl.BlockSpec((nc, D_ATTN, ATTN_TILE), chunk_map),
                      pl.BlockSpec((nc, D_ATTN, ATTN_TILE), chunk_map),
                      pl.BlockSpec((1, D_ATTN, rows), seq_map),
                      pl.BlockSpec((1, D_ATTN, rows), seq_map),
                      pl.BlockSpec((1, N_HEADS, rows), seq_map)]
    else:
        outs += [jax.ShapeDtypeStruct((t, D_ATTN), F32)] * 3
        out_specs += [pl.BlockSpec((rows, D_ATTN), row_map)] * 3
    return pl.pallas_call(
        functools.partial(_inproj_kernel, transposed=transposed, q_scale=q_scale),
        grid=grid,
        in_specs=[pl.BlockSpec((nbb, rb, d), lambda i, j: (i, j, 0)),
                  pl.BlockSpec((nbb, 1, d), lambda i, j: (i, 0, 0)),
                  pl.BlockSpec((nbb, 1, d), lambda i, j: (i, 0, 0)),
                  pl.BlockSpec((1, 1, d), lambda i, j: (0, 0, 0)),
                  pl.BlockSpec((d, n_main), lambda i, j: (0, 0)),
                  pl.BlockSpec((3, D_ATTN, d), lambda i, j: (0, 0, 0)),
                  pl.BlockSpec((1, LANES), lambda i, j: (0, 0))],
        out_specs=out_specs,
        out_shape=outs,
        compiler_params=_cparams(("arbitrary", "arbitrary")),
    )(x3, sc, sh, g, w_main, w_qv, b_f)


def _crep_kernel(lf_ref, tri_ref, o_ref, carry_ref, *, scale):
    j = pl.program_id(1)

    @pl.when(j == 0)
    def _():
        carry_ref[...] = jnp.zeros_like(carry_ref)

    tri = tri_ref[...]
    cs = sum(jnp.dot(tri, p, preferred_element_type=F32) for p in _split3(lf_ref[...]))
    cs = cs + carry_ref[...]
    rows = cs.shape[0]
    carry_ref[...] = cs[rows - 1:rows, :]
    for h in range(N_HEADS):
        o_ref[0, h] = jnp.broadcast_to(cs[:, h:h + 1], (rows, LANES)) * (-scale)


def _crep(lf128, nb, length, scale):
    rows = min(length, 2 * ATTN_TILE)
    nj = length // rows
    tri = np.tril(np.ones((rows, rows), np.float32))
    return pl.pallas_call(
        functools.partial(_crep_kernel, scale=scale),
        grid=(nb, nj),
        in_specs=[pl.BlockSpec((rows, LANES), lambda i, j: (i * nj + j, 0)),
                  pl.BlockSpec((rows, rows), lambda i, j: (0, 0))],
        out_specs=pl.BlockSpec((1, N_HEADS, rows, LANES), lambda i, j: (i, 0, j, 0)),
        out_shape=jax.ShapeDtypeStruct((nb, N_HEADS, length, LANES), F32),
        scratch_shapes=[pltpu.VMEM((1, LANES), F32)],
        compiler_params=_cparams(("arbitrary", "arbitrary")),
    )(lf128, jnp.asarray(tri, BF16))


def _cumsum_kernel(x_ref, tri_ref, o_ref):
    tri = tri_ref[...]
    rb, n = x_ref.shape
    carry = jnp.zeros((rb, 1), F32)
    for c in range(n // LANES):
        lanes = slice(c * LANES, (c + 1) * LANES)
        cs = sum(jnp.dot(p, tri, preferred_element_type=F32) for p in _split3(x_ref[:, lanes])) + carry
        o_ref[:, lanes] = -cs
        carry = cs[:, LANES - 1:LANES]


def _neg_cumsum(x2, rb):
    r, n = x2.shape
    tri = np.triu(np.ones((LANES, LANES), np.float32))
    return pl.pallas_call(
        _cumsum_kernel,
        grid=(r // rb,),
        in_specs=[pl.BlockSpec((rb, n), lambda i: (i, 0)),
                  pl.BlockSpec((LANES, LANES), lambda i: (0, 0))],
        out_specs=pl.BlockSpec((rb, n), lambda i: (i, 0)),
        out_shape=jax.ShapeDtypeStruct((r, n), F32),
        compiler_params=_cparams(("arbitrary",)),
    )(x2, jnp.asarray(tri, BF16))


def _gelu_tanh(x):
    return 0.5 * x * (1.0 + jnp.tanh(math.sqrt(2.0 / math.pi) * (x + 0.044715 * (x * x * x))))


def _s5_kernel(u_ref, h0_ref, wb_ref, wc_ref, are_ref, aim_ref, dsk_ref, wglu_ref, bglu_ref, gout_ref,
               y_ref, ht_ref, s_ref, hc_ref, *, nseq, tm):
    rows = nseq * tm
    sr = rows + SUBLANES
    ti = pl.program_id(1)

    @pl.when(ti == 0)
    def _():
        hc_ref[...] = h0_ref[...]

    u = u_ref[...].reshape(rows, D_SSM)
    ub = u.astype(BF16)
    for c in range(4):
        bu = jnp.dot(ub[:, c * LANES:(c + 1) * LANES], wb_ref[c], preferred_element_type=F32)
        for jj in range(4):
            s_ref[pl.ds((4 * c + jj) * sr, rows), :] = bu[:, jj * LANES:(jj + 1) * LANES]
            s_ref[pl.ds((16 + 4 * c + jj) * sr, rows), :] = bu[:, 512 + jj * LANES:512 + (jj + 1) * LANES]

    ar = (are_ref[0:8, :], are_ref[8:16, :])
    ai = (aim_ref[0:8, :], aim_ref[8:16, :])

    def seq_group(sg, carry):
        base = sg * 4
        hs = []
        for b in range(4):
            hs.append(tuple(hc_ref[base + b, pl.ds(8 * q, 8), :] for q in range(4)))

        def step(t, hs):
            new = []
            for b in range(4):
                row = (base + b) * tm + t
                hr0, hr1, hi0, hi1 = hs[b]
                bre0 = s_ref[pl.ds(row, 8, stride=sr), :]
                bre1 = s_ref[pl.ds(8 * sr + row, 8, stride=sr), :]
                bim0 = s_ref[pl.ds(16 * sr + row, 8, stride=sr), :]
                bim1 = s_ref[pl.ds(24 * sr + row, 8, stride=sr), :]
                nr0 = ar[0] * hr0 - ai[0] * hi0 + bre0
                nr1 = ar[1] * hr1 - ai[1] * hi1 + bre1
                ni0 = ar[0] * hi0 + ai[0] * hr0 + bim0
                ni1 = ar[1] * hi1 + ai[1] * hr1 + bim1
                s_ref[pl.ds(row, 8, stride=sr), :] = nr0
                s_ref[pl.ds(8 * sr + row, 8, stride=sr), :] = nr1
                s_ref[pl.ds(16 * sr + row, 8, stride=sr), :] = ni0
                s_ref[pl.ds(24 * sr + row, 8, stride=sr), :] = ni1
                new.append((nr0, nr1, ni0, ni1))
            return tuple(new)

        hs = lax.fori_loop(0, tm, step, tuple(hs), unroll=S5_SCAN_UNROLL)
        for b in range(4):
            for q in range(4):
                hc_ref[base + b, pl.ds(8 * q, 8), :] = hs[b][q]
        return carry

    lax.fori_loop(0, nseq // 4, seq_group, 0)
    ht_ref[...] = hc_ref[...]

    ys = []
    for c in range(4):
        blocks = [s_ref[pl.ds((4 * c + jj) * sr, rows), :].astype(BF16) for jj in range(4)]
        blocks += [s_ref[pl.ds((16 + 4 * c + jj) * sr, rows), :].astype(BF16) for jj in range(4)]
        hcat = jnp.concatenate(blocks, axis=1)
        ys.append(jnp.dot(hcat, wc_ref[c], preferred_element_type=F32))
    y = jnp.concatenate(ys, axis=1) + dsk_ref[...] * u
    y = _gelu_tanh(y)
    gate = jax.nn.sigmoid(jnp.dot(y.astype(BF16), wglu_ref[...], preferred_element_type=F32) + bglu_ref[...])
    y = y * gate
    y_ref[...] = _rms(y, gout_ref[...]).astype(BF16).reshape(y_ref.shape)


def _s5(u3, h0, wb, wc, a_re, a_im, dsk, wglu, bglu, gout, nseq, tm):
    nb, length, _ = u3.shape
    rows = nseq * tm
    sr = rows + SUBLANES
    grid = (nb // nseq, length // tm)
    const2 = lambda i, j: (0, 0)
    const3 = lambda i, j: (0, 0, 0)
    if tm == length:
        y_spec = pl.BlockSpec((rows, D_SSM), lambda i, j: (i, 0))
        y_shape = jax.ShapeDtypeStruct((nb * length, D_SSM), BF16)
    else:
        y_spec = pl.BlockSpec((nseq, tm, D_SSM), lambda i, j: (i, j, 0))
        y_shape = jax.ShapeDtypeStruct((nb, length, D_SSM), BF16)
    return pl.pallas_call(
        functools.partial(_s5_kernel, nseq=nseq, tm=tm),
        grid=grid,
        in_specs=[pl.BlockSpec((nseq, tm, D_SSM), lambda i, j: (i, j, 0)),
                  pl.BlockSpec((nseq, 32, LANES), lambda i, j: (i, 0, 0)),
                  pl.BlockSpec((4, LANES, 1024), const3),
                  pl.BlockSpec((4, 1024, LANES), const3),
                  pl.BlockSpec((16, LANES), const2),
                  pl.BlockSpec((16, LANES), const2),
                  pl.BlockSpec((1, D_SSM), const2),
                  pl.BlockSpec((D_SSM, D_SSM), const2),
                  pl.BlockSpec((1, D_SSM), const2),
                  pl.BlockSpec((1, D_SSM), const2)],
        out_specs=[y_spec, pl.BlockSpec((nseq, 32, LANES), lambda i, j: (i, 0, 0))],
        out_shape=[y_shape, jax.ShapeDtypeStruct((nb, 32, LANES), F32)],
        scratch_shapes=[pltpu.VMEM((32 * sr, LANES), F32), pltpu.VMEM((nseq, 32, LANES), F32)],
        compiler_params=_cparams(("arbitrary", "arbitrary")),
    )(u3, h0, wb, wc, a_re, a_im, dsk, wglu, bglu, gout)


def _s5_params(lam_re, lam_im, log_step, b_re, b_im, c_re, c_im):
    g = lam_re.shape[0]
    dt = jnp.exp(log_step)[:, None]
    mag = jnp.exp(lam_re * dt)
    a_re = mag * jnp.cos(lam_im * dt)
    a_im = mag * jnp.sin(lam_im * dt)
    den = lam_re * lam_re + lam_im * lam_im
    n_re = a_re - 1.0
    f_re = (n_re * lam_re + a_im * lam_im) / den
    f_im = (a_im * lam_re - n_re * lam_im) / den
    bb_re = f_re[..., None] * b_re - f_im[..., None] * b_im
    bb_im = f_re[..., None] * b_im + f_im[..., None] * b_re
    eye = jnp.eye(g, dtype=F32)
    n_state = g * SSM_STATE

    def in_mat(bb):
        return jnp.einsum('gpi,gh->gihp', bb, eye).reshape(g * SSM_GROUP, n_state)

    def out_mat(cc):
        return jnp.einsum('gip,gh->gphi', cc, eye).reshape(n_state, g * SSM_GROUP)

    wbr, wbi = in_mat(bb_re), in_mat(bb_im)
    wcr, wci = out_mat(c_re), out_mat(-c_im)
    wb = jnp.stack([jnp.concatenate([wbr[c * 128:(c + 1) * 128, c * 512:(c + 1) * 512],
                                     wbi[c * 128:(c + 1) * 128, c * 512:(c + 1) * 512]], axis=1)
                    for c in range(4)]).astype(BF16)
    wc = jnp.stack([jnp.concatenate([wcr[c * 512:(c + 1) * 512, c * 128:(c + 1) * 128],
                                     wci[c * 512:(c + 1) * 512, c * 128:(c + 1) * 128]], axis=0)
                    for c in range(4)]).astype(BF16)
    return wb, wc, a_re.reshape(16, LANES), a_im.reshape(16, LANES)


def _prompt_attention_step(qi, qt_ref, k_ref, vt_ref, cr_ref, g_ref, o_ref, qz_ref, m_ref, l_ref, acc_ref,
                           st_ref):
    tq = ATTN_TILE
    half = LANES // 2

    rowid = lax.broadcasted_iota(I32, (LANES, tq), 0)
    for j in range(N_HEADS // 2):
        qp = qt_ref[0, j * LANES:(j + 1) * LANES, :]
        qz_ref[2 * j] = jnp.where(rowid < half, qp, jnp.zeros_like(qp))
        qz_ref[2 * j + 1] = jnp.where(rowid >= half, qp, jnp.zeros_like(qp))
    m_ref[...] = jnp.full_like(m_ref, NEG_INF)
    l_ref[...] = jnp.zeros_like(l_ref)
    acc_ref[...] = jnp.zeros_like(acc_ref)

    key_row = lax.broadcasted_iota(I32, (tq, tq), 0)
    q_col = lax.broadcasted_iota(I32, (tq, tq), 1)
    causal = key_row <= q_col

    def tiles(kbs, masked):
        m_new = [m_ref[h:h + 1, :] for h in range(N_HEADS)]
        for c, kb in enumerate(kbs):
            ks = pl.multiple_of(kb * tq, tq)
            for j in range(N_HEADS // 2):
                kp = k_ref[0, pl.ds(ks, tq), j * LANES:(j + 1) * LANES]
                for e in range(2):
                    h = 2 * j + e
                    st = jnp.dot(kp, qz_ref[h], preferred_element_type=F32)
                    bias = cr_ref[0, h, pl.ds(ks, tq), :]
                    st = st + jnp.concatenate([bias] * (tq // LANES), axis=1)
                    if masked:
                        st = jnp.where(causal, st, NEG_INF)
                    st_ref[c, h] = st
                    m_new[h] = jnp.maximum(m_new[h], jnp.max(st, axis=0, keepdims=True))
        for h in range(N_HEADS):
            alpha = jnp.exp2(m_ref[h:h + 1, :] - m_new[h])
            rows = slice(h * HEAD_DIM, (h + 1) * HEAD_DIM)
            l_new = alpha * l_ref[h:h + 1, :]
            acc = alpha * acc_ref[rows, :]
            for c, kb in enumerate(kbs):
                p = jnp.exp2(st_ref[c, h] - m_new[h])
                l_new = l_new + jnp.sum(p, axis=0, keepdims=True)
                acc = acc + jnp.dot(vt_ref[kb, rows, :], p.astype(BF16), preferred_element_type=F32)
            l_ref[h:h + 1, :] = l_new
            m_ref[h:h + 1, :] = m_new[h]
            acc_ref[rows, :] = acc

    def body(g, c):
        tiles([g * ATTN_GROUP + i for i in range(ATTN_GROUP)], False)
        return c

    n_groups = qi // ATTN_GROUP
    lax.fori_loop(0, n_groups, body, 0)
    for r in range(1, ATTN_GROUP):
        @pl.when(qi - n_groups * ATTN_GROUP == r)
        def _(r=r):
            tiles([n_groups * ATTN_GROUP + i for i in range(r)], False)
    tiles([qi], True)

    for h in range(N_HEADS):
        rows = slice(h * HEAD_DIM, (h + 1) * HEAD_DIM)
        acc_ref[rows, :] = acc_ref[rows, :] / l_ref[h:h + 1, :]
    o = acc_ref[...].T
    o_ref[0] = _rms(o, g_ref[...]).astype(BF16)


def _paged_seq_attention(q8, kn, vn, cn, g, kpages, vpages, n_pages, n_new):
    n_past = n_pages * PAGE
    d = q8.shape[-1]
    nr = N_HEADS * n_new
    new_bits = n_new.bit_length() - 1
    head_bits = HEAD_DIM.bit_length() - 1
    rowh = lax.shift_right_logical(lax.broadcasted_iota(I32, (nr, d), 0), new_bits)
    colh = lax.shift_right_logical(lax.broadcasted_iota(I32, (nr, d), 1), head_bits)
    bd = rowh == colh
    qrep = jnp.broadcast_to(q8[None], (N_HEADS, n_new, d)).reshape(nr, d)
    qbd = jnp.where(bd, qrep, 0.0).astype(BF16)
    cnr = jnp.broadcast_to(cn[:, None, :], (N_HEADS, n_new, cn.shape[-1])).reshape(nr, cn.shape[-1])

    s_p = []
    for p in range(n_pages):
        kt = kpages[p].reshape(d, PAGE)
        s_p.append(_bdot(qbd, kt) + cnr[:, p * PAGE:(p + 1) * PAGE])
    s_n = _bdot_nt(qbd, kn) + cnr[:, n_past:n_past + n_new]
    qpos = lax.broadcasted_iota(I32, (nr, n_new), 0) & (n_new - 1)
    kpos = lax.broadcasted_iota(I32, (nr, n_new), 1)
    s_n = jnp.where(kpos <= qpos, s_n, NEG_INF)
    m = jnp.max(s_n, axis=1, keepdims=True)
    for sp in s_p:
        m = jnp.maximum(m, jnp.max(sp, axis=1, keepdims=True))
    p_n = jnp.exp(s_n - m)
    l = jnp.sum(p_n, axis=1, keepdims=True)
    of = _bdot(p_n, vn)
    for p in range(n_pages):
        pp = jnp.exp(s_p[p] - m)
        l = l + jnp.sum(pp, axis=1, keepdims=True)
        of = of + _bdot_nt(pp, vpages[p].reshape(d, PAGE))
    of = jnp.where(bd, of / l, 0.0)
    o = jnp.sum(of.reshape(N_HEADS, n_new, d), axis=0)
    return _rms(o, g)


def _attention_kernel(pt_ref, qt_ref, k_ref, vt_ref, cr_ref, g_ref, qs_ref, kn_ref, vn_ref, cn_ref, kc_hbm, vc_hbm,
                      o_ref, os_ref, qz_ref, m_ref, l_ref, acc_ref, st_ref, kbuf, vbuf, sem, *, n_pages, n_seq):
    step = pl.program_id(0) * pl.num_programs(1) + pl.program_id(1)
    per_step = qs_ref.shape[0]
    n_new = qs_ref.shape[1]

    def page_copies(seq, slot):
        cps = []
        for p in range(n_pages):
            pg = pt_ref[seq * n_pages + p]
            cps.append(pltpu.make_async_copy(kc_hbm.at[pg], kbuf.at[slot, p], sem.at[0, slot]))
            cps.append(pltpu.make_async_copy(vc_hbm.at[pg], vbuf.at[slot, p], sem.at[1, slot]))
        return cps

    @pl.when(step == 0)
    def _():
        for slot in range(2):
            for cp in page_copies(slot, slot):
                cp.start()

    def sample_seq(j):
        seq = step * per_step + j
        slot = j % 2
        for cp in page_copies(seq, slot):
            cp.wait()
        os_ref[j] = _paged_seq_attention(qs_ref[j], kn_ref[j], vn_ref[j], cn_ref[j], g_ref[...],
                                         kbuf.at[slot], vbuf.at[slot], n_pages, n_new)

        @pl.when(seq + 2 < n_seq)
        def _():
            for cp in page_copies(seq + 2, slot):
                cp.start()

    for j in range(per_step // 2):
        sample_seq(j)
    _prompt_attention_step(pl.program_id(1), qt_ref, k_ref, vt_ref, cr_ref, g_ref, o_ref,
                           qz_ref, m_ref, l_ref, acc_ref, st_ref)
    for j in range(per_step // 2, per_step):
        sample_seq(j)


def _attention(qt, k, vt, crep, g, page_table, qs, kn, vn, cn, cache_k, cache_v):
    b, length, d = k.shape
    tq = ATTN_TILE
    nq = length // tq
    n_seq, n_new, _ = qs.shape
    assert n_new & (n_new - 1) == 0, "new-token count must be a power of two"
    per_step = n_seq // (b * nq)
    assert per_step % 2 == 0 and per_step * b * nq == n_seq
    n_pages = page_table.shape[1]
    once = pl.Buffered(1)
    smap = lambda i, j, pt: (i * nq + j, 0, 0)
    grid_spec = pltpu.PrefetchScalarGridSpec(
        num_scalar_prefetch=1,
        grid=(b, nq),
        in_specs=[pl.BlockSpec((1, d, tq), smap),
                  pl.BlockSpec((1, length, d), lambda i, j, pt: (i, 0, 0), pipeline_mode=once),
                  pl.BlockSpec((nq, d, tq), lambda i, j, pt: (i, 0, 0), pipeline_mode=once),
                  pl.BlockSpec((1, N_HEADS, length, LANES), lambda i, j, pt: (i, 0, 0, 0), pipeline_mode=once),
                  pl.BlockSpec((1, d), lambda i, j, pt: (0, 0)),
                  pl.BlockSpec((per_step, n_new, d), smap),
                  pl.BlockSpec((per_step, n_new, d), smap),
                  pl.BlockSpec((per_step, n_new, d), smap),
                  pl.BlockSpec((per_step, N_HEADS, cn.shape[-1]), smap),
                  pl.BlockSpec(memory_space=pl.ANY),
                  pl.BlockSpec(memory_space=pl.ANY)],
        out_specs=[pl.BlockSpec((1, tq, d), lambda i, j, pt: (i, j, 0)),
                   pl.BlockSpec((per_step, n_new, d), smap)],
        scratch_shapes=[pltpu.VMEM((N_HEADS, LANES, tq), BF16), pltpu.VMEM((N_HEADS, tq), F32),
                        pltpu.VMEM((N_HEADS, tq), F32), pltpu.VMEM((d, tq), F32),
                        pltpu.VMEM((ATTN_GROUP, N_HEADS, tq, tq), F32),
                        pltpu.VMEM((2, n_pages, N_HEADS, HEAD_DIM, PAGE), F32),
                        pltpu.VMEM((2, n_pages, N_HEADS, HEAD_DIM, PAGE), F32),
                        pltpu.SemaphoreType.DMA((2, 2))],
    )
    return pl.pallas_call(
        functools.partial(_attention_kernel, n_pages=n_pages, n_seq=n_seq),
        grid_spec=grid_spec,
        out_shape=[jax.ShapeDtypeStruct((b, length, d), BF16), jax.ShapeDtypeStruct((n_seq, n_new, d), F32)],
        compiler_params=pltpu.CompilerParams(dimension_semantics=("arbitrary", "arbitrary"),
                                             vmem_limit_bytes=ATTN_VMEM_LIMIT),
    )(page_table.reshape(-1), qt, k, vt, crep, g, qs, kn, vn, cn, cache_k, cache_v)


def _outproj_kernel(x_ref, ys_ref, oa_ref, gt_ref, sc_ref, sh_ref, g_ref, wo_ref, wrt_ref,
                    x1_ref, h2_ref, lg_ref):
    nbb, rb, d = x_ref.shape
    mix = jnp.dot(ys_ref[...], wo_ref[0:D_SSM, :], preferred_element_type=F32)
    mix = mix + jnp.dot(oa_ref[...], wo_ref[D_SSM:, :], preferred_element_type=F32)
    x1 = x_ref[...] + gt_ref[...] * mix.reshape(nbb, rb, d)
    x1_ref[...] = x1
    h2 = (_rms(x1, g_ref[...]) * (1.0 + sc_ref[...]) + sh_ref[...]).reshape(nbb * rb, d)
    hb = h2.astype(BF16)
    h2_ref[...] = _pack_bf16_pairs(hb)
    hlo = (h2 - hb.astype(F32)).astype(BF16)
    whi = wrt_ref[0]
    wlo = wrt_ref[1]
    nt = (((1,), (1,)), ((), ()))
    lg = lax.dot_general(whi, hb, nt, preferred_element_type=F32)
    lg = lg + lax.dot_general(wlo, hb, nt, preferred_element_type=F32)
    lg = lg + lax.dot_general(whi, hlo, nt, preferred_element_type=F32)
    lg_ref[...] = lg


def _outproj(x3, ys, oa, gt, sc, sh, g, wo, wrt, nbb, rb):
    nb, r, d = x3.shape
    t = nb * r
    rows = nbb * rb
    nj = r // rb
    row_map = lambda i, j: (i * nj + j, 0)
    mod_spec = pl.BlockSpec((nbb, 1, d), lambda i, j: (i, 0, 0))
    return pl.pallas_call(
        _outproj_kernel,
        grid=(nb // nbb, nj),
        in_specs=[pl.BlockSpec((nbb, rb, d), lambda i, j: (i, j, 0)),
                  pl.BlockSpec((rows, D_SSM), row_map),
                  pl.BlockSpec((rows, D_ATTN), row_map),
                  mod_spec, mod_spec, mod_spec,
                  pl.BlockSpec((1, 1, d), lambda i, j: (0, 0, 0)),
                  pl.BlockSpec((D_SSM + D_ATTN, d), lambda i, j: (0, 0)),
                  pl.BlockSpec((2, N_EXPERTS, d), lambda i, j: (0, 0, 0))],
        out_specs=[pl.BlockSpec((nbb, rb, d), lambda i, j: (i, j, 0)),
                   pl.BlockSpec((rows, d // 2), row_map),
                   pl.BlockSpec((N_EXPERTS, rows), lambda i, j: (0, i * nj + j))],
        out_shape=[jax.ShapeDtypeStruct((nb, r, d), F32),
                   jax.ShapeDtypeStruct((t, d // 2), jnp.uint32),
                   jax.ShapeDtypeStruct((N_EXPERTS, t), F32)],
        compiler_params=_cparams(("arbitrary", "arbitrary")),
    )(x3, ys, oa, gt, sc, sh, g, wo, wrt)


def _router_kernel(lg_ref, rb_ref, ut_ref, idx_ref, w_ref, rank_ref, cnt_ref, carry_ref):
    i = pl.program_id(0)
    tm = lg_ref.shape[1]
    per_group = N_EXPERTS // N_EXPERT_GROUPS

    @pl.when(i == 0)
    def _():
        carry_ref[...] = jnp.zeros_like(carry_ref)

    scores = jax.nn.sigmoid(lg_ref[...])
    biased = scores + rb_ref[...]
    blks, grp = [], []
    for g in range(N_EXPERT_GROUPS):
        blk = biased[g * per_group:(g + 1) * per_group, :]
        m1 = jnp.max(blk, axis=0, keepdims=True)
        eq = blk == m1
        n_eq = jnp.sum(jnp.where(eq, 1.0, 0.0), axis=0, keepdims=True)
        m2 = jnp.max(jnp.where(eq, NEG_INF, blk), axis=0, keepdims=True)
        blks.append(blk)
        grp.append(m1 + jnp.where(n_eq >= 2.0, m1, m2))
    masked = []
    for g in range(N_EXPERT_GROUPS):
        beaten = jnp.zeros((1, tm), F32)
        for o in range(N_EXPERT_GROUPS):
            if o == g:
                continue
            ahead = (grp[o] >= grp[g]) if o < g else (grp[o] > grp[g])
            beaten = beaten + jnp.where(ahead, 1.0, 0.0)
        masked.append(jnp.where(beaten < float(TOPK_GROUPS), blks[g], NEG_INF))
    work = jnp.concatenate(masked, axis=0)

    eid = lax.broadcasted_iota(I32, (N_EXPERTS, tm), 0)
    chosen = jnp.zeros((N_EXPERTS, tm), F32)
    idxs, ws, sels = [], [], []
    for _ in range(TOP_K):
        m = jnp.max(work, axis=0, keepdims=True)
        first = jnp.min(jnp.where(work == m, eid, N_EXPERTS), axis=0, keepdims=True)
        sel = eid == first
        idxs.append(first)
        ws.append(jnp.sum(jnp.where(sel, scores, 0.0), axis=0, keepdims=True))
        sels.append(sel)
        chosen = jnp.where(sel, 1.0, chosen)
        work = jnp.where(sel, NEG_INF, work)
    wsum = ws[0]
    for wk in ws[1:]:
        wsum = wsum + wk

    prefix = jnp.dot(chosen.astype(BF16), ut_ref[...], preferred_element_type=F32) + carry_ref[...]
    carry_ref[...] = carry_ref[...] + jnp.sum(chosen, axis=1, keepdims=True)
    cnt_ref[...] = carry_ref[...]

    idx_ref[...] = jnp.zeros_like(idx_ref)
    w_ref[...] = jnp.zeros_like(w_ref)
    rank_ref[...] = jnp.zeros_like(rank_ref)
    for k in range(TOP_K):
        idx_ref[k:k + 1, :] = idxs[k]
        w_ref[k:k + 1, :] = ws[k] / wsum * ROUTED_SCALE
        rank = jnp.sum(jnp.where(sels[k], prefix, 0.0), axis=0, keepdims=True)
        rank_ref[k:k + 1, :] = rank.astype(I32)


def _router(lg, router_bias, tm):
    e, t = lg.shape
    ut = np.triu(np.ones((tm, tm), np.float32), 1)
    tok_spec = pl.BlockSpec((SUBLANES, tm), lambda i: (0, i))
    return pl.pallas_call(
        _router_kernel,
        grid=(t // tm,),
        in_specs=[pl.BlockSpec((e, tm), lambda i: (0, i)),
                  pl.BlockSpec((e, 1), lambda i: (0, 0)),
                  pl.BlockSpec((tm, tm), lambda i: (0, 0))],
        out_specs=[tok_spec, tok_spec, tok_spec, pl.BlockSpec((e, 1), lambda i: (0, 0))],
        out_shape=[jax.ShapeDtypeStruct((SUBLANES, t), I32), jax.ShapeDtypeStruct((SUBLANES, t), F32),
                   jax.ShapeDtypeStruct((SUBLANES, t), I32), jax.ShapeDtypeStruct((e, 1), F32)],
        scratch_shapes=[pltpu.VMEM((e, 1), F32)],
        compiler_params=_cparams(("arbitrary",)),
    )(lg, router_bias.reshape(e, 1), jnp.asarray(ut, BF16))


def _experts_kernel(be_ref, nv_ref, new_ref, fill_ref, x_ref, wg_ref, wu_ref, wd_ref, o_ref, wgu_s, wd_s):
    i = pl.program_id(0)
    de = wg_ref.shape[-1]
    valid = i < nv_ref[0]

    @pl.when(valid & (new_ref[i] == 1))
    def _():
        wgu_s[:, :de] = wg_ref[0].astype(BF16)
        wgu_s[:, de:] = wu_ref[0].astype(BF16)
        wd_s[...] = wd_ref[0].astype(BF16)

    @pl.when(valid)
    def _():
        row = lax.broadcasted_iota(I32, x_ref.shape, 0)
        x = jnp.where(row < fill_ref[i], x_ref[...], jnp.uint32(0))
        gu = jnp.dot(_unpack_bf16_pairs(x), wgu_s[...], preferred_element_type=F32)
        g = gu[:, :de]
        a = (g * jax.nn.sigmoid(g) * gu[:, de:]).astype(BF16)
        o = jnp.dot(a, wd_s[...], preferred_element_type=F32)
        o_ref[...] = _pack_bf16_pairs(o.astype(BF16))

    @pl.when(jnp.logical_not(valid))
    def _():
        o_ref[...] = jnp.zeros_like(o_ref)


def _experts(block_e, n_valid, block_new, block_fill, xs, w_eg, w_eu, w_ed, m):
    n_rows = xs.shape[0]
    d, de = w_eg.shape[-2:]
    nblk = n_rows // m
    xmap = lambda i, be, nv, *_: (jnp.minimum(i, nv[0] - 1), 0)
    wmap = lambda i, be, *_: (be[i], 0, 0)
    grid_spec = pltpu.PrefetchScalarGridSpec(
        num_scalar_prefetch=4,
        grid=(nblk,),
        in_specs=[pl.BlockSpec((m, xs.shape[1]), xmap),
                  pl.BlockSpec((1, d, de), wmap),
                  pl.BlockSpec((1, d, de), wmap),
                  pl.BlockSpec((1, de, d), wmap)],
        out_specs=pl.BlockSpec((m, d // 2), lambda i, *_: (i, 0)),
        scratch_shapes=[pltpu.VMEM((d, 2 * de), BF16), pltpu.VMEM((de, d), BF16)],
    )
    return pl.pallas_call(
        _experts_kernel,
        grid_spec=grid_spec,
        out_shape=jax.ShapeDtypeStruct((n_rows, d // 2), jnp.uint32),
        compiler_params=_cparams(("arbitrary",)),
    )(block_e, n_valid, block_new, block_fill, xs, w_eg, w_eu, w_ed)


def _gather_rows(table, idx, window):
    n = idx.shape[0]
    d = table.shape[1]
    info = plsc.get_sparse_core_info()
    n_workers = info.num_cores * info.num_subcores
    per_worker = n // n_workers
    n_buf = SC_GATHER_BUFFERS
    per_set = n_buf // 2
    n_rounds = per_worker // (window * per_set)
    assert n_buf % 2 == 0 and n_rounds % 2 == 0 and n_rounds * window * per_set * n_workers == n
    mesh = plsc.VectorSubcoreMesh(core_axis_name="c", subcore_axis_name="s")

    @functools.partial(
        pl.kernel, out_type=jax.ShapeDtypeStruct((n, d), table.dtype), mesh=mesh,
        scratch_types=[pltpu.VMEM((per_worker,), I32), pltpu.VMEM((n_buf, window, d), table.dtype),
                       pltpu.SemaphoreType.DMA((n_buf,)), pltpu.SemaphoreType.DMA((n_buf,))])
    def gather(x_hbm, i_hbm, o_hbm, i_v, buf, gsem, wsem):
        base = (lax.axis_index("s") * info.num_cores + lax.axis_index("c")) * per_worker
        pltpu.sync_copy(i_hbm.at[pl.ds(base, per_worker)], i_v)

        def reads(rnd, s):
            return [pltpu.make_async_copy(x_hbm.at[i_v.at[pl.ds((rnd * per_set + b) * window, window)]],
                                          buf.at[s * per_set + b], gsem.at[s * per_set + b])
                    for b in range(per_set)]

        def writes(rnd, s):
            return [pltpu.make_async_copy(buf.at[s * per_set + b],
                                          o_hbm.at[pl.ds(base + (rnd * per_set + b) * window, window)],
                                          wsem.at[s * per_set + b])
                    for b in range(per_set)]

        for cp in reads(0, 0):
            cp.start()

        @pl.loop(0, n_rounds, step=2)
        def _(r0):
            for s in range(2):
                rnd = r0 + s

                @pl.when(rnd >= 1)
                def _():
                    for cp in writes(rnd - 1, 1 - s):
                        cp.wait()

                @pl.when(rnd + 1 < n_rounds)
                def _():
                    for cp in reads(rnd + 1, 1 - s):
                        cp.start()

                for cp in reads(rnd, s):
                    cp.wait()
                for cp in writes(rnd, s):
                    cp.start()

        for cp in writes(n_rounds - 1, 1):
            cp.wait()

    return gather(table, idx)


def _scatter_rows(groups, dest_t, n_rows, window):
    d = groups[0].shape[1]
    dtype = groups[0].dtype
    info = plsc.get_sparse_core_info()
    n_workers = info.num_cores * info.num_subcores
    n_buf = 2
    plan, idx_parts, first = [], [], 0
    row0 = 0
    for x in groups:
        t = x.shape[0]
        per_worker = t // n_workers
        chunks = per_worker // window
        assert chunks % n_buf == 0 and chunks * window * n_workers == t
        idx = dest_t[:TOP_K, first:first + t].reshape(TOP_K, n_workers, chunks, window).transpose(1, 0, 2, 3)
        idx_parts.append(idx.reshape(n_workers, TOP_K * chunks, window))
        plan.append((per_worker, chunks, row0))
        row0 += TOP_K * chunks
        first += t
    idx_rows = -(-row0 // SUBLANES) * SUBLANES
    idx = jnp.concatenate(idx_parts + [jnp.zeros((n_workers, idx_rows - row0, window), I32)], axis=1)
    idx = idx.reshape(n_workers * idx_rows, window)
    mesh = plsc.VectorSubcoreMesh(core_axis_name="c", subcore_axis_name="s")

    @functools.partial(
        pl.kernel, out_type=jax.ShapeDtypeStruct((n_rows, d), dtype), mesh=mesh,
        scratch_types=[pltpu.VMEM((idx_rows, window), I32), pltpu.VMEM((n_buf, window, d), dtype),
                       pltpu.SemaphoreType.DMA((n_buf,)), pltpu.SemaphoreType.DMA((n_buf,))])
    def scatter(*refs):
        x_hbms = refs[:len(groups)]
        i_hbm, o_hbm, i_v, buf, rsem, wsem = refs[len(groups):]
        wid = lax.axis_index("s") * info.num_cores + lax.axis_index("c")
        pltpu.sync_copy(i_hbm.at[pl.ds(wid * idx_rows, idx_rows)], i_v)

        for x_hbm, (per_worker, chunks, row0) in zip(x_hbms, plan):
            base = wid * per_worker

            @pl.loop(0, chunks // n_buf)
            def _(rnd, x_hbm=x_hbm, base=base, chunks=chunks, row0=row0):
                reads = [pltpu.make_async_copy(x_hbm.at[pl.ds(base + (rnd * n_buf + b) * window, window)],
                                               buf.at[b], rsem.at[b]) for b in range(n_buf)]
                for cp in reads:
                    cp.start()
                writes = []
                for b in range(n_buf):
                    reads[b].wait()
                    for k in range(TOP_K):
                        rows = i_v.at[row0 + k * chunks + rnd * n_buf + b]
                        cp = pltpu.make_async_copy(buf.at[b], o_hbm.at[rows], wsem.at[b])
                        cp.start()
                        writes.append(cp)
                for cp in writes:
                    cp.wait()

    return scatter(*groups, idx)


def _final_kernel(x1_ref, h2_ref, w_ref, gt_ref, sc_ref, sh_ref, g_ref, wgu_ref, wd_ref, *rest):
    yg_refs = rest[:TOP_K]
    y_ref = rest[TOP_K]
    nbb, rb, d = x1_ref.shape
    ds = wd_ref.shape[0]
    gu = jnp.dot(_unpack_bf16_pairs(h2_ref[...]), wgu_ref[...], preferred_element_type=F32)
    g = gu[:, :ds]
    a = (g * jax.nn.sigmoid(g) * gu[:, ds:]).astype(BF16)
    ff = jnp.dot(a, wd_ref[...], preferred_element_type=F32)
    wts = w_ref[...].T
    routed_lo = routed_hi = None
    for k in range(TOP_K):
        yk = yg_refs[k][...]
        w = wts[:, k:k + 1]
        lo = lax.bitcast_convert_type(lax.shift_left(yk, jnp.uint32(16)), F32) * w
        hi = lax.bitcast_convert_type(yk & jnp.uint32(0xFFFF0000), F32) * w
        routed_lo = lo if k == 0 else routed_lo + lo
        routed_hi = hi if k == 0 else routed_hi + hi
    ff = jnp.concatenate([routed_lo, routed_hi], axis=1) + ff
    x2 = x1_ref[...] + gt_ref[...] * ff.reshape(nbb, rb, d)
    y_ref[...] = _rms(x2, g_ref[...]) * (1.0 + sc_ref[...]) + sh_ref[...]


def _final(x1, h2, wts, yg, first_tile, gt, sc, sh, g, wgu, wd, nbb, rb):
    nb, r, d = x1.shape
    rows = nbb * rb
    nj = r // rb
    tiles_all = yg.shape[0] // TOP_K // rows
    row_map = lambda i, j: (i * nj + j, 0)
    mod_spec = pl.BlockSpec((nbb, 1, d), lambda i, j: (i, 0, 0))
    slab_specs = [pl.BlockSpec((rows, yg.shape[1]), lambda i, j, k=k: (k * tiles_all + first_tile + i * nj + j, 0))
                  for k in range(TOP_K)]
    return pl.pallas_call(
        _final_kernel,
        grid=(nb // nbb, nj),
        in_specs=[pl.BlockSpec((nbb, rb, d), lambda i, j: (i, j, 0)),
                  pl.BlockSpec((rows, h2.shape[1]), row_map),
                  pl.BlockSpec((SUBLANES, rows), lambda i, j: (0, first_tile + i * nj + j)),
                  mod_spec, mod_spec, mod_spec,
                  pl.BlockSpec((1, 1, d), lambda i, j: (0, 0, 0)),
                  pl.BlockSpec(wgu.shape, lambda i, j: (0, 0)),
                  pl.BlockSpec(wd.shape, lambda i, j: (0, 0))] + slab_specs,
        out_specs=pl.BlockSpec((nbb, rb, d), lambda i, j: (i, j, 0)),
        out_shape=jax.ShapeDtypeStruct((nb, r, d), F32),
        compiler_params=_cparams(("arbitrary", "arbitrary")),
    )(x1, h2, wts, gt, sc, sh, g, wgu, wd, *([yg] * TOP_K))


def _moe(h2_groups, lg, router_bias, w_eg, w_eu, w_ed, tile):
    t = lg.shape[1]
    e = N_EXPERTS
    m = MOE_ROWS
    idx_t, w_t, rank_t, counts = _router(lg, router_bias, tile)
    counts = counts.reshape(e).astype(I32)
    padded = ((counts + m - 1) // m) * m
    pad_end = jnp.cumsum(padded)
    pad_start = pad_end - padded
    onehot = idx_t[:, :, None] == jnp.arange(e, dtype=I32)
    dest_t = jnp.sum(jnp.where(onehot, pad_start, 0), axis=-1) + rank_t
    n_rows = (-(-(t * TOP_K) // m)) * m + e * m
    nblk = n_rows // m
    block_start = jnp.arange(nblk, dtype=I32) * m
    block_e = jnp.minimum(jnp.sum(pad_end[None, :] <= block_start[:, None], axis=1), e - 1).astype(I32)
    block_new = jnp.concatenate([jnp.ones((1,), I32), (block_e[1:] != block_e[:-1]).astype(I32)])
    n_valid = (pad_end[-1] // m).astype(I32).reshape(1)
    of_block = block_e[:, None] == jnp.arange(e, dtype=I32)
    count_b = jnp.sum(jnp.where(of_block, counts, 0), axis=1)
    start_b = jnp.sum(jnp.where(of_block, pad_start, 0), axis=1)
    block_fill = jnp.clip(count_b - (block_start - start_b), 0, m).astype(I32)
    xs = _scatter_rows(h2_groups, dest_t, n_rows, SC_SCATTER_WINDOW)
    yb = _experts(block_e, n_valid, block_new, block_fill, xs, w_eg, w_eu, w_ed, m)
    yg = _gather_rows(yb, dest_t[:TOP_K].reshape(-1), SC_GATHER_WINDOW)
    return w_t, yg


def kernel(x_prompt, x_sample, c_prompt, c_sample, cache_k, cache_v, cache_logf, state_ssm_re, state_ssm_im, page_table, w_ada, b_ada, g_norm1, w_in, b_fgate, ssm_lambda_re, ssm_lambda_im, ssm_log_step, ssm_b_re, ssm_b_im, ssm_c_re, ssm_c_im, ssm_d, w_glu, b_glu, g_ssm_out, g_attn_out, w_out, g_norm2, w_router, router_bias, w_exp_gate, w_exp_up, w_exp_down, w_sh_gate, w_sh_up, w_sh_down, g_final, w_ada_final, b_ada_final):
    depth = w_ada.shape[0]
    assert depth == 1, "one layer is supported"
    bp, lp, d = x_prompt.shape
    bs, ls, _ = x_sample.shape
    n_pages = page_table.shape[1]
    n_past = n_pages * PAGE
    n_groups = ssm_lambda_re.shape[1]

    n_c = bp + bs
    n_c_pad = -(-n_c // SUBLANES) * SUBLANES
    c_all = jnp.concatenate([c_prompt, c_sample, jnp.zeros((n_c_pad - n_c, d), F32)], axis=0)
    mod = _adaln(c_all, w_ada[0], b_ada[0])
    modf = _adaln(c_all, w_ada_final, b_ada_final)

    def mods(lo, hi):
        parts = [mod[lo:hi, k * d:(k + 1) * d][:, None, :] for k in range(6)]
        parts += [modf[lo:hi, k * d:(k + 1) * d][:, None, :] for k in range(2)]
        return parts

    w_u, w_q, w_k, w_v, w_f = jnp.split(w_in[0], [D_SSM, D_SSM + D_ATTN, D_SSM + 2 * D_ATTN,
                                                   D_SSM + 3 * D_ATTN], axis=1)
    w_fpad = jnp.concatenate([w_f, jnp.zeros((d, LANES - N_HEADS), F32)], axis=1)
    w_main_s = jnp.concatenate([w_u, w_k, w_v, w_fpad], axis=1).astype(BF16)
    w_main_p = jnp.concatenate([w_u, w_k, w_fpad], axis=1).astype(BF16)
    w_qkv = jnp.stack([w_q.T, w_k.T, w_v.T]).astype(BF16)
    b_f = jnp.concatenate([b_fgate[0], jnp.zeros((LANES - N_HEADS,), F32)]).reshape(1, LANES)
    g1 = g_norm1[0].reshape(1, 1, d)
    g2 = g_norm2[0].reshape(1, 1, d)
    gf = g_final.reshape(1, 1, d)
    wb, wc, a_re, a_im = _s5_params(ssm_lambda_re[0], ssm_lambda_im[0], ssm_log_step[0], ssm_b_re[0],
                                    ssm_b_im[0], ssm_c_re[0], ssm_c_im[0])
    dsk = ssm_d[0].reshape(1, D_SSM)
    wglu = w_glu[0].astype(BF16)
    bglu = b_glu[0].reshape(1, D_SSM)
    g_so = g_ssm_out[0].reshape(1, D_SSM)
    g_ao = g_attn_out[0].reshape(1, D_ATTN)
    wo = w_out[0].astype(BF16)
    wr_t = w_router[0].T
    wr_hi = wr_t.astype(BF16)
    wrt = jnp.stack([wr_hi, (wr_t - wr_hi.astype(F32)).astype(BF16)])
    wgu = jnp.concatenate([w_sh_gate[0], w_sh_up[0]], axis=1).astype(BF16)
    wsd = w_sh_down[0].astype(BF16)

    def ssm_state(re, im):
        return jnp.concatenate([re.reshape(-1, 16, LANES), im.reshape(-1, 16, LANES)], axis=1)

    def split_state(ht):
        n = ht.shape[0]
        return (ht[:, :16].reshape(1, n, n_groups, SSM_STATE), ht[:, 16:].reshape(1, n, n_groups, SSM_STATE))

    tm = ROW_TILE
    sh1_p, sc1_p, gt1_p, sh2_p, sc2_p, gt2_p, shf_p, scf_p = mods(0, bp)
    u, lf, kb, qt, vt, ktf, vtf, lft = _inproj(x_prompt, sc1_p, sh1_p, g1, w_main_p, w_qkv, b_f, 1, tm, True,
                                               HEAD_DIM ** -0.5 * LOG2E)
    crep = _crep(lf, bp, lp, LOG2E)
    ys_p, ht = _s5(u.reshape(bp, lp, D_SSM), jnp.zeros((bp, 32, LANES), F32), wb, wc, a_re, a_im, dsk,
                   wglu, bglu, g_so, bp, S5_TIME_TILE)
    k_prompt = ktf.reshape(bp, N_HEADS, HEAD_DIM, lp).transpose(0, 3, 1, 2)[None]
    v_prompt = vtf.reshape(bp, N_HEADS, HEAD_DIM, lp).transpose(0, 3, 1, 2)[None]
    logf_prompt = lft.transpose(0, 2, 1)[None]
    sre_p, sim_p = split_state(ht)

    nbb = ROW_TILE // ls
    sh1_s, sc1_s, gt1_s, sh2_s, sc2_s, gt2_s, shf_s, scf_s = mods(bp, bp + bs)
    u, lf, k, v, q = _inproj(x_sample, sc1_s, sh1_s, g1, w_main_s, w_qkv, b_f, nbb, ls, False, HEAD_DIM ** -0.5)
    ys_s, ht = _s5(u.reshape(bs, ls, D_SSM), ssm_state(state_ssm_re[0], state_ssm_im[0]), wb, wc, a_re, a_im,
                   dsk, wglu, bglu, g_so, S5_SAMPLE_SEQS, ls)
    lf_past = cache_logf[0][page_table].reshape(bs, n_past, N_HEADS).transpose(0, 2, 1)
    lf_new = lf[:, :N_HEADS].reshape(bs, ls, N_HEADS).transpose(0, 2, 1)
    n_key_pad = -(-(n_past + ls) // LANES) * LANES
    lf_all = jnp.concatenate([lf_past, lf_new, jnp.zeros((bs, N_HEADS, n_key_pad - n_past - ls), F32)], axis=2)
    cn_s = _neg_cumsum(lf_all.reshape(bs * N_HEADS, n_key_pad), PREFIX_ROWS).reshape(bs, N_HEADS, n_key_pad)

    oa_p, oa_s = _attention(qt, kb.reshape(bp, lp, D_ATTN), vt, crep, g_ao, page_table,
                            q.reshape(bs, ls, D_ATTN), k.reshape(bs, ls, D_ATTN), v.reshape(bs, ls, D_ATTN),
                            cn_s, cache_k[0].transpose(0, 2, 3, 1), cache_v[0].transpose(0, 2, 3, 1))
    x1_p, h2_p, lg_p = _outproj(x_prompt, ys_p.reshape(bp * lp, D_SSM), oa_p.reshape(bp * lp, D_ATTN),
                                gt1_p, sc2_p, sh2_p, g2, wo, wrt, 1, tm)
    x1_s, h2_s, lg_s = _outproj(x_sample, ys_s, oa_s.reshape(bs * ls, D_ATTN).astype(BF16),
                                gt1_s, sc2_s, sh2_s, g2, wo, wrt, nbb, ls)

    assert nbb * ls == tm
    wts, yg = _moe([h2_p, h2_s], jnp.concatenate([lg_p, lg_s], axis=1), router_bias[0],
                   w_exp_gate[0], w_exp_up[0], w_exp_down[0], tm)
    tiles_p = bp * lp // tm
    y_prompt = _final(x1_p, h2_p, wts, yg, 0, gt2_p, scf_p, shf_p, gf, wgu, wsd, 1, tm)
    y_sample = _final(x1_s, h2_s, wts, yg, tiles_p, gt2_s, scf_s, shf_s, gf, wgu, wsd, nbb, ls)
    k_sample = k.reshape(1, bs, ls, N_HEADS, HEAD_DIM)
    v_sample = v.reshape(1, bs, ls, N_HEADS, HEAD_DIM)
    logf_sample = lf[:, :N_HEADS].reshape(1, bs, ls, N_HEADS)
    sre_s, sim_s = split_state(ht)

    return (y_prompt, y_sample, k_prompt, v_prompt, logf_prompt, sre_p, sim_p,
            k_sample, v_sample, logf_sample, sre_s, sim_s)
```

```python
import functools
import math

import jax
import jax.numpy as jnp
import numpy as np
from jax import lax
from jax.experimental import pallas as pl
from jax.experimental.pallas import tpu as pltpu
from jax.experimental.pallas import tpu_sc as plsc

F32 = jnp.float32
BF16 = jnp.bfloat16
I32 = jnp.int32

EPS = 1e-6
HEAD_DIM = 64
N_HEADS = 8
D_SSM = 512
D_ATTN = 512
SSM_GROUP = 16
SSM_STATE = 64
N_EXPERTS = 64
TOP_K = 6
N_EXPERT_GROUPS = 8
TOPK_GROUPS = 4
ROUTED_SCALE = 2.5
PAGE = 128

LANES = 128
SUBLANES = 8
VMEM_LIMIT = 48 * 1024 * 1024
ATTN_VMEM_LIMIT = 58 * 1024 * 1024
ROW_TILE = 512
S5_TIME_TILE = 256
S5_SAMPLE_SEQS = 32
S5_SCAN_UNROLL = 4
PREFIX_ROWS = 256
MOE_ROWS = 512
ATTN_TILE = 256
ATTN_GROUP = 4
LOG2E = math.log2(math.e)
SC_SCATTER_WINDOW = 16
SC_GATHER_WINDOW = 48
SC_GATHER_BUFFERS = 4
NEG_INF = float("-inf")


def _cparams(sem):
    return pltpu.CompilerParams(dimension_semantics=sem, vmem_limit_bytes=VMEM_LIMIT)


def _bdot(a, b):
    return jnp.dot(a.astype(BF16), b.astype(BF16), preferred_element_type=F32)


def _bdot_nt(a, b):
    return lax.dot_general(a.astype(BF16), b.astype(BF16), (((1,), (1,)), ((), ())),
                           preferred_element_type=F32)


def _split3(v):
    hi = v.astype(BF16)
    r1 = v - hi.astype(F32)
    mid = r1.astype(BF16)
    lo = (r1 - mid.astype(F32)).astype(BF16)
    return hi, mid, lo


def _pack_bf16_pairs(xb):
    n = xb.shape[1] // 2
    bits = lax.bitcast_convert_type(xb.astype(F32), jnp.uint32)
    return lax.shift_right_logical(bits[:, :n], jnp.uint32(16)) | bits[:, n:]


def _unpack_bf16_pairs(xp):
    lo = lax.bitcast_convert_type(lax.shift_left(xp, jnp.uint32(16)), F32)
    hi = lax.bitcast_convert_type(xp & jnp.uint32(0xFFFF0000), F32)
    return jnp.concatenate([lo, hi], axis=1).astype(BF16)


def _rms(x, g):
    return x * lax.rsqrt(jnp.mean(x * x, axis=-1, keepdims=True) + EPS) * g


def _adaln_kernel(c_ref, w_ref, b_ref, o_ref):
    c = c_ref[...]
    s = c * jax.nn.sigmoid(c)
    o_ref[...] = _bdot(s, w_ref[...]) + b_ref[...]


def _adaln(c, w, b):
    m, k = c.shape
    n = w.shape[1]
    tn = 1024
    return pl.pallas_call(
        _adaln_kernel,
        grid=(n // tn,),
        in_specs=[pl.BlockSpec((m, k), lambda j: (0, 0)),
                  pl.BlockSpec((k, tn), lambda j: (0, j)),
                  pl.BlockSpec((1, tn), lambda j: (0, j))],
        out_specs=pl.BlockSpec((m, tn), lambda j: (0, j)),
        out_shape=jax.ShapeDtypeStruct((m, n), F32),
        compiler_params=_cparams(("arbitrary",)),
    )(c, w, b.reshape(1, n))


def _inproj_kernel(x_ref, sc_ref, sh_ref, g_ref, w_ref, wqv_ref, bf_ref, *out_refs, transposed, q_scale):
    nbb, rb, d = x_ref.shape
    rows = nbb * rb
    x = x_ref[...]
    h = _rms(x, g_ref[...]) * (1.0 + sc_ref[...]) + sh_ref[...]
    hb = h.reshape(rows, d).astype(BF16)
    proj = jnp.dot(hb, w_ref[...], preferred_element_type=F32)
    u_ref, lf_ref = out_refs[:2]
    u_ref[...] = proj[:, :D_SSM]
    k = proj[:, D_SSM:D_SSM + D_ATTN]
    z = proj[:, proj.shape[1] - LANES:] + bf_ref[...]
    lf = jnp.minimum(z, 0.0) - jnp.log1p(jnp.exp(-jnp.abs(z)))
    lf_ref[...] = lf
    nt = (((1,), (1,)), ((), ()))
    if transposed:
        kb_ref, qt_ref, vt_ref, ktf_ref, vtf_ref, lft_ref = out_refs[2:]
        lft_ref[0] = lf.T[:N_HEADS]
        kb_ref[...] = k.astype(BF16)
        qt = (lax.dot_general(wqv_ref[0], hb, nt, preferred_element_type=F32) * q_scale).astype(BF16)
        ktf_ref[0] = lax.dot_general(wqv_ref[1], hb, nt, preferred_element_type=F32)
        vt = lax.dot_general(wqv_ref[2], hb, nt, preferred_element_type=F32)
        vtf_ref[0] = vt
        vt = vt.astype(BF16)
        for c in range(rows // ATTN_TILE):
            qt_ref[c] = qt[:, c * ATTN_TILE:(c + 1) * ATTN_TILE]
            vt_ref[c] = vt[:, c * ATTN_TILE:(c + 1) * ATTN_TILE]
    else:
        k_ref, v_ref, q_ref = out_refs[2:]
        k_ref[...] = k
        v_ref[...] = proj[:, D_SSM + D_ATTN:D_SSM + 2 * D_ATTN]
        q_ref[...] = lax.dot_general(hb, wqv_ref[0], nt, preferred_element_type=F32) * q_scale


def _inproj(x3, sc, sh, g, w_main, w_qv, b_f, nbb, rb, transposed, q_scale):
    nb, r, d = x3.shape
    t = nb * r
    rows = nbb * rb
    nj = r // rb
    grid = (nb // nbb, nj)
    row_map = lambda i, j: (i * nj + j, 0)
    n_main = w_main.shape[1]
    outs = [jax.ShapeDtypeStruct((t, D_SSM), F32), jax.ShapeDtypeStruct((t, LANES), F32)]
    out_specs = [pl.BlockSpec((rows, D_SSM), row_map), pl.BlockSpec((rows, LANES), row_map)]
    if transposed:
        assert nbb == 1, "transposed outputs are laid out per sequence"
        nc = rows // ATTN_TILE
        chunk_map = lambda i, j: (i * nj + j, 0, 0)
        seq_map = lambda i, j: (i, 0, j)
        outs += [jax.ShapeDtypeStruct((t, D_ATTN), BF16),
                 jax.ShapeDtypeStruct((t // ATTN_TILE, D_ATTN, ATTN_TILE), BF16),
                 jax.ShapeDtypeStruct((t // ATTN_TILE, D_ATTN, ATTN_TILE), BF16),
                 jax.ShapeDtypeStruct((nb, D_ATTN, r), F32),
                 jax.ShapeDtypeStruct((nb, D_ATTN, r), F32),
                 jax.ShapeDtypeStruct((nb, N_HEADS, r), F32)]
        out_specs += [pl.BlockSpec((rows, D_ATTN), row_map),
                      pl.BlockSpec((nc, D_ATTN, ATTN_TILE), chunk_map),
                      pl.BlockSpec((nc, D_ATTN, ATTN_TILE), chunk_map),
                      pl.BlockSpec((1, D_ATTN, rows), seq_map),
                      pl.BlockSpec((1, D_ATTN, rows), seq_map),
                      pl.BlockSpec((1, N_HEADS, rows), seq_map)]
    else:
        outs += [jax.ShapeDtypeStruct((t, D_ATTN), F32)] * 3
        out_specs += [pl.BlockSpec((rows, D_ATTN), row_map)] * 3
    return pl.pallas_call(
        functools.partial(_inproj_kernel, transposed=transposed, q_scale=q_scale),
        grid=grid,
        in_specs=[pl.BlockSpec((nbb, rb, d), lambda i, j: (i, j, 0)),
                  pl.BlockSpec((nbb, 1, d), lambda i, j: (i, 0, 0)),
                  pl.BlockSpec((nbb, 1, d), lambda i, j: (i, 0, 0)),
                  pl.BlockSpec((1, 1, d), lambda i, j: (0, 0, 0)),
                  pl.BlockSpec((d, n_main), lambda i, j: (0, 0)),
                  pl.BlockSpec((3, D_ATTN, d), lambda i, j: (0, 0, 0)),
                  pl.BlockSpec((1, LANES), lambda i, j: (0, 0))],
        out_specs=out_specs,
        out_shape=outs,
        compiler_params=_cparams(("arbitrary", "arbitrary")),
    )(x3, sc, sh, g, w_main, w_qv, b_f)


def _crep_kernel(lf_ref, tri_ref, o_ref, carry_ref, *, scale):
    j = pl.program_id(1)

    @pl.when(j == 0)
    def _():
        carry_ref[...] = jnp.zeros_like(carry_ref)

    tri = tri_ref[...]
    cs = sum(jnp.dot(tri, p, preferred_element_type=F32) for p in _split3(lf_ref[...]))
    cs = cs + carry_ref[...]
    rows = cs.shape[0]
    carry_ref[...] = cs[rows - 1:rows, :]
    for h in range(N_HEADS):
        o_ref[0, h] = jnp.broadcast_to(cs[:, h:h + 1], (rows, LANES)) * (-scale)


def _crep(lf128, nb, length, scale):
    rows = min(length, 2 * ATTN_TILE)
    nj = length // rows
    tri = np.tril(np.ones((rows, rows), np.float32))
    return pl.pallas_call(
        functools.partial(_crep_kernel, scale=scale),
        grid=(nb, nj),
        in_specs=[pl.BlockSpec((rows, LANES), lambda i, j: (i * nj + j, 0)),
                  pl.BlockSpec((rows, rows), lambda i, j: (0, 0))],
        out_specs=pl.BlockSpec((1, N_HEADS, rows, LANES), lambda i, j: (i, 0, j, 0)),
        out_shape=jax.ShapeDtypeStruct((nb, N_HEADS, length, LANES), F32),
        scratch_shapes=[pltpu.VMEM((1, LANES), F32)],
        compiler_params=_cparams(("arbitrary", "arbitrary")),
    )(lf128, jnp.asarray(tri, BF16))


def _cumsum_kernel(x_ref, tri_ref, o_ref):
    tri = tri_ref[...]
    rb, n = x_ref.shape
    carry = jnp.zeros((rb, 1), F32)
    for c in range(n // LANES):
        lanes = slice(c * LANES, (c + 1) * LANES)
        cs = sum(jnp.dot(p, tri, preferred_element_type=F32) for p in _split3(x_ref[:, lanes])) + carry
        o_ref[:, lanes] = -cs
        carry = cs[:, LANES - 1:LANES]


def _neg_cumsum(x2, rb):
    r, n = x2.shape
    tri = np.triu(np.ones((LANES, LANES), np.float32))
    return pl.pallas_call(
        _cumsum_kernel,
        grid=(r // rb,),
        in_specs=[pl.BlockSpec((rb, n), lambda i: (i, 0)),
                  pl.BlockSpec((LANES, LANES), lambda i: (0, 0))],
        out_specs=pl.BlockSpec((rb, n), lambda i: (i, 0)),
        out_shape=jax.ShapeDtypeStruct((r, n), F32),
        compiler_params=_cparams(("arbitrary",)),
    )(x2, jnp.asarray(tri, BF16))


def _gelu_tanh(x):
    return 0.5 * x * (1.0 + jnp.tanh(math.sqrt(2.0 / math.pi) * (x + 0.044715 * (x * x * x))))


def _s5_kernel(u_ref, h0_ref, wb_ref, wc_ref, are_ref, aim_ref, dsk_ref, wglu_ref, bglu_ref, gout_ref,
               y_ref, ht_ref, s_ref, hc_ref, *, nseq, tm):
    rows = nseq * tm
    sr = rows + SUBLANES
    ti = pl.program_id(1)

    @pl.when(ti == 0)
    def _():
        hc_ref[...] = h0_ref[...]

    u = u_ref[...].reshape(rows, D_SSM)
    ub = u.astype(BF16)
    for c in range(4):
        bu = jnp.dot(ub[:, c * LANES:(c + 1) * LANES], wb_ref[c], preferred_element_type=F32)
        for jj in range(4):
            s_ref[pl.ds((4 * c + jj) * sr, rows), :] = bu[:, jj * LANES:(jj + 1) * LANES]
            s_ref[pl.ds((16 + 4 * c + jj) * sr, rows), :] = bu[:, 512 + jj * LANES:512 + (jj + 1) * LANES]

    ar = (are_ref[0:8, :], are_ref[8:16, :])
    ai = (aim_ref[0:8, :], aim_ref[8:16, :])

    def seq_group(sg, carry):
        base = sg * 4
        hs = []
        for b in range(4):
            hs.append(tuple(hc_ref[base + b, pl.ds(8 * q, 8), :] for q in range(4)))

        def step(t, hs):
            new = []
            for b in range(4):
                row = (base + b) * tm + t
                hr0, hr1, hi0, hi1 = hs[b]
                bre0 = s_ref[pl.ds(row, 8, stride=sr), :]
                bre1 = s_ref[pl.ds(8 * sr + row, 8, stride=sr), :]
                bim0 = s_ref[pl.ds(16 * sr + row, 8, stride=sr), :]
                bim1 = s_ref[pl.ds(24 * sr + row, 8, stride=sr), :]
                nr0 = ar[0] * hr0 - ai[0] * hi0 + bre0
                nr1 = ar[1] * hr1 - ai[1] * hi1 + bre1
                ni0 = ar[0] * hi0 + ai[0] * hr0 + bim0
                ni1 = ar[1] * hi1 + ai[1] * hr1 + bim1
                s_ref[pl.ds(row, 8, stride=sr), :] = nr0
                s_ref[pl.ds(8 * sr + row, 8, stride=sr), :] = nr1
                s_ref[pl.ds(16 * sr + row, 8, stride=sr), :] = ni0
                s_ref[pl.ds(24 * sr + row, 8, stride=sr), :] = ni1
                new.append((nr0, nr1, ni0, ni1))
            return tuple(new)

        hs = lax.fori_loop(0, tm, step, tuple(hs), unroll=S5_SCAN_UNROLL)
        for b in range(4):
            for q in range(4):
                hc_ref[base + b, pl.ds(8 * q, 8), :] = hs[b][q]
        return carry

    lax.fori_loop(0, nseq // 4, seq_group, 0)
    ht_ref[...] = hc_ref[...]

    ys = []
    for c in range(4):
        blocks = [s_ref[pl.ds((4 * c + jj) * sr, rows), :].astype(BF16) for jj in range(4)]
        blocks += [s_ref[pl.ds((16 + 4 * c + jj) * sr, rows), :].astype(BF16) for jj in range(4)]
        hcat = jnp.concatenate(blocks, axis=1)
        ys.append(jnp.dot(hcat, wc_ref[c], preferred_element_type=F32))
    y = jnp.concatenate(ys, axis=1) + dsk_ref[...] * u
    y = _gelu_tanh(y)
    gate = jax.nn.sigmoid(jnp.dot(y.astype(BF16), wglu_ref[...], preferred_element_type=F32) + bglu_ref[...])
    y = y * gate
    y_ref[...] = _rms(y, gout_ref[...]).astype(BF16).reshape(y_ref.shape)


def _s5(u3, h0, wb, wc, a_re, a_im, dsk, wglu, bglu, gout, nseq, tm):
    nb, length, _ = u3.shape
    rows = nseq * tm
    sr = rows + SUBLANES
    grid = (nb // nseq, length // tm)
    const2 = lambda i, j: (0, 0)
    const3 = lambda i, j: (0, 0, 0)
    if tm == length:
        y_spec = pl.BlockSpec((rows, D_SSM), lambda i, j: (i, 0))
        y_shape = jax.ShapeDtypeStruct((nb * length, D_SSM), BF16)
    else:
        y_spec = pl.BlockSpec((nseq, tm, D_SSM), lambda i, j: (i, j, 0))
        y_shape = jax.ShapeDtypeStruct((nb, length, D_SSM), BF16)
    return pl.pallas_call(
        functools.partial(_s5_kernel, nseq=nseq, tm=tm),
        grid=grid,
        in_specs=[pl.BlockSpec((nseq, tm, D_SSM), lambda i, j: (i, j, 0)),
                  pl.BlockSpec((nseq, 32, LANES), lambda i, j: (i, 0, 0)),
                  pl.BlockSpec((4, LANES, 1024), const3),
                  pl.BlockSpec((4, 1024, LANES), const3),
                  pl.BlockSpec((16, LANES), const2),
                  pl.BlockSpec((16, LANES), const2),
                  pl.BlockSpec((1, D_SSM), const2),
                  pl.BlockSpec((D_SSM, D_SSM), const2),
                  pl.BlockSpec((1, D_SSM), const2),
                  pl.BlockSpec((1, D_SSM), const2)],
        out_specs=[y_spec, pl.BlockSpec((nseq, 32, LANES), lambda i, j: (i, 0, 0))],
        out_shape=[y_shape, jax.ShapeDtypeStruct((nb, 32, LANES), F32)],
        scratch_shapes=[pltpu.VMEM((32 * sr, LANES), F32), pltpu.VMEM((nseq, 32, LANES), F32)],
        compiler_params=_cparams(("arbitrary", "arbitrary")),
    )(u3, h0, wb, wc, a_re, a_im, dsk, wglu, bglu, gout)


def _s5_params(lam_re, lam_im, log_step, b_re, b_im, c_re, c_im):
    g = lam_re.shape[0]
    dt = jnp.exp(log_step)[:, None]
    mag = jnp.exp(lam_re * dt)
    a_re = mag * jnp.cos(lam_im * dt)
    a_im = mag * jnp.sin(lam_im * dt)
    den = lam_re * lam_re + lam_im * lam_im
    n_re = a_re - 1.0
    f_re = (n_re * lam_re + a_im * lam_im) / den
    f_im = (a_im * lam_re - n_re * lam_im) / den
    bb_re = f_re[..., None] * b_re - f_im[..., None] * b_im
    bb_im = f_re[..., None] * b_im + f_im[..., None] * b_re
    eye = jnp.eye(g, dtype=F32)
    n_state = g * SSM_STATE

    def in_mat(bb):
        return jnp.einsum('gpi,gh->gihp', bb, eye).reshape(g * SSM_GROUP, n_state)

    def out_mat(cc):
        return jnp.einsum('gip,gh->gphi', cc, eye).reshape(n_state, g * SSM_GROUP)

    wbr, wbi = in_mat(bb_re), in_mat(bb_im)
    wcr, wci = out_mat(c_re), out_mat(-c_im)
    wb = jnp.stack([jnp.concatenate([wbr[c * 128:(c + 1) * 128, c * 512:(c + 1) * 512],
                                     wbi[c * 128:(c + 1) * 128, c * 512:(c + 1) * 512]], axis=1)
                    for c in range(4)]).astype(BF16)
    wc = jnp.stack([jnp.concatenate([wcr[c * 512:(c + 1) * 512, c * 128:(c + 1) * 128],
                                     wci[c * 512:(c + 1) * 512, c * 128:(c + 1) * 128]], axis=0)
                    for c in range(4)]).astype(BF16)
    return wb, wc, a_re.reshape(16, LANES), a_im.reshape(16, LANES)


def _prompt_attention_step(qi, qt_ref, k_ref, vt_ref, cr_ref, g_ref, o_ref, qz_ref, m_ref, l_ref, acc_ref,
                           st_ref):
    tq = ATTN_TILE
    half = LANES // 2

    rowid = lax.broadcasted_iota(I32, (LANES, tq), 0)
    for j in range(N_HEADS // 2):
        qp = qt_ref[0, j * LANES:(j + 1) * LANES, :]
        qz_ref[2 * j] = jnp.where(rowid < half, qp, jnp.zeros_like(qp))
        qz_ref[2 * j + 1] = jnp.where(rowid >= half, qp, jnp.zeros_like(qp))
    m_ref[...] = jnp.full_like(m_ref, NEG_INF)
    l_ref[...] = jnp.zeros_like(l_ref)
    acc_ref[...] = jnp.zeros_like(acc_ref)

    key_row = lax.broadcasted_iota(I32, (tq, tq), 0)
    q_col = lax.broadcasted_iota(I32, (tq, tq), 1)
    causal = key_row <= q_col

    def tiles(kbs, masked):
        m_new = [m_ref[h:h + 1, :] for h in range(N_HEADS)]
        for c, kb in enumerate(kbs):
            ks = pl.multiple_of(kb * tq, tq)
            for j in range(N_HEADS // 2):
                kp = k_ref[0, pl.ds(ks, tq), j * LANES:(j + 1) * LANES]
                for e in range(2):
                    h = 2 * j + e
                    st = jnp.dot(kp, qz_ref[h], preferred_element_type=F32)
                    bias = cr_ref[0, h, pl.ds(ks, tq), :]
                    st = st + jnp.concatenate([bias] * (tq // LANES), axis=1)
                    if masked:
                        st = jnp.where(causal, st, NEG_INF)
                    st_ref[c, h] = st
                    m_new[h] = jnp.maximum(m_new[h], jnp.max(st, axis=0, keepdims=True))
        for h in range(N_HEADS):
            alpha = jnp.exp2(m_ref[h:h + 1, :] - m_new[h])
            rows = slice(h * HEAD_DIM, (h + 1) * HEAD_DIM)
            l_new = alpha * l_ref[h:h + 1, :]
            acc = alpha * acc_ref[rows, :]
            for c, kb in enumerate(kbs):
                p = jnp.exp2(st_ref[c, h] - m_new[h])
                l_new = l_new + jnp.sum(p, axis=0, keepdims=True)
                acc = acc + jnp.dot(vt_ref[kb, rows, :], p.astype(BF16), preferred_element_type=F32)
            l_ref[h:h + 1, :] = l_new
            m_ref[h:h + 1, :] = m_new[h]
            acc_ref[rows, :] = acc

    def body(g, c):
        tiles([g * ATTN_GROUP + i for i in range(ATTN_GROUP)], False)
        return c

    n_groups = qi // ATTN_GROUP
    lax.fori_loop(0, n_groups, body, 0)
    for r in range(1, ATTN_GROUP):
        @pl.when(qi - n_groups * ATTN_GROUP == r)
        def _(r=r):
            tiles([n_groups * ATTN_GROUP + i for i in range(r)], False)
    tiles([qi], True)

    for h in range(N_HEADS):
        rows = slice(h * HEAD_DIM, (h + 1) * HEAD_DIM)
        acc_ref[rows, :] = acc_ref[rows, :] / l_ref[h:h + 1, :]
    o = acc_ref[...].T
    o_ref[0] = _rms(o, g_ref[...]).astype(BF16)


def _paged_seq_attention(q8, kn, vn, cn, g, kpages, vpages, n_pages, n_new):
    n_past = n_pages * PAGE
    d = q8.shape[-1]
    nr = N_HEADS * n_new
    new_bits = n_new.bit_length() - 1
    head_bits = HEAD_DIM.bit_length() - 1
    rowh = lax.shift_right_logical(lax.broadcasted_iota(I32, (nr, d), 0), new_bits)
    colh = lax.shift_right_logical(lax.broadcasted_iota(I32, (nr, d), 1), head_bits)
    bd = rowh == colh
    qrep = jnp.broadcast_to(q8[None], (N_HEADS, n_new, d)).reshape(nr, d)
    qbd = jnp.where(bd, qrep, 0.0).astype(BF16)
    cnr = jnp.broadcast_to(cn[:, None, :], (N_HEADS, n_new, cn.shape[-1])).reshape(nr, cn.shape[-1])

    s_p = []
    for p in range(n_pages):
        kt = kpages[p].reshape(d, PAGE)
        s_p.append(_bdot(qbd, kt) + cnr[:, p * PAGE:(p + 1) * PAGE])
    s_n = _bdot_nt(qbd, kn) + cnr[:, n_past:n_past + n_new]
    qpos = lax.broadcasted_iota(I32, (nr, n_new), 0) & (n_new - 1)
    kpos = lax.broadcasted_iota(I32, (nr, n_new), 1)
    s_n = jnp.where(kpos <= qpos, s_n, NEG_INF)
    m = jnp.max(s_n, axis=1, keepdims=True)
    for sp in s_p:
        m = jnp.maximum(m, jnp.max(sp, axis=1, keepdims=True))
    p_n = jnp.exp(s_n - m)
    l = jnp.sum(p_n, axis=1, keepdims=True)
    of = _bdot(p_n, vn)
    for p in range(n_pages):
        pp = jnp.exp(s_p[p] - m)
        l = l + jnp.sum(pp, axis=1, keepdims=True)
        of = of + _bdot_nt(pp, vpages[p].reshape(d, PAGE))
    of = jnp.where(bd, of / l, 0.0)
    o = jnp.sum(of.reshape(N_HEADS, n_new, d), axis=0)
    return _rms(o, g)


def _attention_kernel(pt_ref, qt_ref, k_ref, vt_ref, cr_ref, g_ref, qs_ref, kn_ref, vn_ref, cn_ref, kc_hbm, vc_hbm,
                      o_ref, os_ref, qz_ref, m_ref, l_ref, acc_ref, st_ref, kbuf, vbuf, sem, *, n_pages, n_seq):
    step = pl.program_id(0) * pl.num_programs(1) + pl.program_id(1)
    per_step = qs_ref.shape[0]
    n_new = qs_ref.shape[1]

    def page_copies(seq, slot):
        cps = []
        for p in range(n_pages):
            pg = pt_ref[seq * n_pages + p]
            cps.append(pltpu.make_async_copy(kc_hbm.at[pg], kbuf.at[slot, p], sem.at[0, slot]))
            cps.append(pltpu.make_async_copy(vc_hbm.at[pg], vbuf.at[slot, p], sem.at[1, slot]))
        return cps

    @pl.when(step == 0)
    def _():
        for slot in range(2):
            for cp in page_copies(slot, slot):
                cp.start()

    def sample_seq(j):
        seq = step * per_step + j
        slot = j % 2
        for cp in page_copies(seq, slot):
            cp.wait()
        os_ref[j] = _paged_seq_attention(qs_ref[j], kn_ref[j], vn_ref[j], cn_ref[j], g_ref[...],
                                         kbuf.at[slot], vbuf.at[slot], n_pages, n_new)

        @pl.when(seq + 2 < n_seq)
        def _():
            for cp in page_copies(seq + 2, slot):
                cp.start()

    for j in range(per_step // 2):
        sample_seq(j)
    _prompt_attention_step(pl.program_id(1), qt_ref, k_ref, vt_ref, cr_ref, g_ref, o_ref,
                           qz_ref, m_ref, l_ref, acc_ref, st_ref)
    for j in range(per_step // 2, per_step):
        sample_seq(j)


def _attention(qt, k, vt, crep, g, page_table, qs, kn, vn, cn, cache_k, cache_v):
    b, length, d = k.shape
    tq = ATTN_TILE
    nq = length // tq
    n_seq, n_new, _ = qs.shape
    assert n_new & (n_new - 1) == 0, "new-token count must be a power of two"
    per_step = n_seq // (b * nq)
    assert per_step % 2 == 0 and per_step * b * nq == n_seq
    n_pages = page_table.shape[1]
    once = pl.Buffered(1)
    smap = lambda i, j, pt: (i * nq + j, 0, 0)
    grid_spec = pltpu.PrefetchScalarGridSpec(
        num_scalar_prefetch=1,
        grid=(b, nq),
        in_specs=[pl.BlockSpec((1, d, tq), smap),
                  pl.BlockSpec((1, length, d), lambda i, j, pt: (i, 0, 0), pipeline_mode=once),
                  pl.BlockSpec((nq, d, tq), lambda i, j, pt: (i, 0, 0), pipeline_mode=once),
                  pl.BlockSpec((1, N_HEADS, length, LANES), lambda i, j, pt: (i, 0, 0, 0), pipeline_mode=once),
                  pl.BlockSpec((1, d), lambda i, j, pt: (0, 0)),
                  pl.BlockSpec((per_step, n_new, d), smap),
                  pl.BlockSpec((per_step, n_new, d), smap),
                  pl.BlockSpec((per_step, n_new, d), smap),
                  pl.BlockSpec((per_step, N_HEADS, cn.shape[-1]), smap),
                  pl.BlockSpec(memory_space=pl.ANY),
                  pl.BlockSpec(memory_space=pl.ANY)],
        out_specs=[pl.BlockSpec((1, tq, d), lambda i, j, pt: (i, j, 0)),
                   pl.BlockSpec((per_step, n_new, d), smap)],
        scratch_shapes=[pltpu.VMEM((N_HEADS, LANES, tq), BF16), pltpu.VMEM((N_HEADS, tq), F32),
                        pltpu.VMEM((N_HEADS, tq), F32), pltpu.VMEM((d, tq), F32),
                        pltpu.VMEM((ATTN_GROUP, N_HEADS, tq, tq), F32),
                        pltpu.VMEM((2, n_pages, N_HEADS, HEAD_DIM, PAGE), F32),
                        pltpu.VMEM((2, n_pages, N_HEADS, HEAD_DIM, PAGE), F32),
                        pltpu.SemaphoreType.DMA((2, 2))],
    )
    return pl.pallas_call(
        functools.partial(_attention_kernel, n_pages=n_pages, n_seq=n_seq),
        grid_spec=grid_spec,
        out_shape=[jax.ShapeDtypeStruct((b, length, d), BF16), jax.ShapeDtypeStruct((n_seq, n_new, d), F32)],
        compiler_params=pltpu.CompilerParams(dimension_semantics=("arbitrary", "arbitrary"),
                                             vmem_limit_bytes=ATTN_VMEM_LIMIT),
    )(page_table.reshape(-1), qt, k, vt, crep, g, qs, kn, vn, cn, cache_k, cache_v)


def _outproj_kernel(x_ref, ys_ref, oa_ref, gt_ref, sc_ref, sh_ref, g_ref, wo_ref, wrt_ref,
                    x1_ref, h2_ref, lg_ref):
    nbb, rb, d = x_ref.shape
    mix = jnp.dot(ys_ref[...], wo_ref[0:D_SSM, :], preferred_element_type=F32)
    mix = mix + jnp.dot(oa_ref[...], wo_ref[D_SSM:, :], preferred_element_type=F32)
    x1 = x_ref[...] + gt_ref[...] * mix.reshape(nbb, rb, d)
    x1_ref[...] = x1
    h2 = (_rms(x1, g_ref[...]) * (1.0 + sc_ref[...]) + sh_ref[...]).reshape(nbb * rb, d)
    hb = h2.astype(BF16)
    h2_ref[...] = _pack_bf16_pairs(hb)
    hlo = (h2 - hb.astype(F32)).astype(BF16)
    whi = wrt_ref[0]
    wlo = wrt_ref[1]
    nt = (((1,), (1,)), ((), ()))
    lg = lax.dot_general(whi, hb, nt, preferred_element_type=F32)
    lg = lg + lax.dot_general(wlo, hb, nt, preferred_element_type=F32)
    lg = lg + lax.dot_general(whi, hlo, nt, preferred_element_type=F32)
    lg_ref[...] = lg


def _outproj(x3, ys, oa, gt, sc, sh, g, wo, wrt, nbb, rb):
    nb, r, d = x3.shape
    t = nb * r
    rows = nbb * rb
    nj = r // rb
    row_map = lambda i, j: (i * nj + j, 0)
    mod_spec = pl.BlockSpec((nbb, 1, d), lambda i, j: (i, 0, 0))
    return pl.pallas_call(
        _outproj_kernel,
        grid=(nb // nbb, nj),
        in_specs=[pl.BlockSpec((nbb, rb, d), lambda i, j: (i, j, 0)),
                  pl.BlockSpec((rows, D_SSM), row_map),
                  pl.BlockSpec((rows, D_ATTN), row_map),
                  mod_spec, mod_spec, mod_spec,
                  pl.BlockSpec((1, 1, d), lambda i, j: (0, 0, 0)),
                  pl.BlockSpec((D_SSM + D_ATTN, d), lambda i, j: (0, 0)),
                  pl.BlockSpec((2, N_EXPERTS, d), lambda i, j: (0, 0, 0))],
        out_specs=[pl.BlockSpec((nbb, rb, d), lambda i, j: (i, j, 0)),
                   pl.BlockSpec((rows, d // 2), row_map),
                   pl.BlockSpec((N_EXPERTS, rows), lambda i, j: (0, i * nj + j))],
        out_shape=[jax.ShapeDtypeStruct((nb, r, d), F32),
                   jax.ShapeDtypeStruct((t, d // 2), jnp.uint32),
                   jax.ShapeDtypeStruct((N_EXPERTS, t), F32)],
        compiler_params=_cparams(("arbitrary", "arbitrary")),
    )(x3, ys, oa, gt, sc, sh, g, wo, wrt)


def _router_kernel(lg_ref, rb_ref, ut_ref, idx_ref, w_ref, rank_ref, cnt_ref, carry_ref):
    i = pl.program_id(0)
    tm = lg_ref.shape[1]
    per_group = N_EXPERTS // N_EXPERT_GROUPS

    @pl.when(i == 0)
    def _():
        carry_ref[...] = jnp.zeros_like(carry_ref)

    scores = jax.nn.sigmoid(lg_ref[...])
    biased = scores + rb_ref[...]
    blks, grp = [], []
    for g in range(N_EXPERT_GROUPS):
        blk = biased[g * per_group:(g + 1) * per_group, :]
        m1 = jnp.max(blk, axis=0, keepdims=True)
        eq = blk == m1
        n_eq = jnp.sum(jnp.where(eq, 1.0, 0.0), axis=0, keepdims=True)
        m2 = jnp.max(jnp.where(eq, NEG_INF, blk), axis=0, keepdims=True)
        blks.append(blk)
        grp.append(m1 + jnp.where(n_eq >= 2.0, m1, m2))
    masked = []
    for g in range(N_EXPERT_GROUPS):
        beaten = jnp.zeros((1, tm), F32)
        for o in range(N_EXPERT_GROUPS):
            if o == g:
                continue
            ahead = (grp[o] >= grp[g]) if o < g else (grp[o] > grp[g])
            beaten = beaten + jnp.where(ahead, 1.0, 0.0)
        masked.append(jnp.where(beaten < float(TOPK_GROUPS), blks[g], NEG_INF))
    work = jnp.concatenate(masked, axis=0)

    eid = lax.broadcasted_iota(I32, (N_EXPERTS, tm), 0)
    chosen = jnp.zeros((N_EXPERTS, tm), F32)
    idxs, ws, sels = [], [], []
    for _ in range(TOP_K):
        m = jnp.max(work, axis=0, keepdims=True)
        first = jnp.min(jnp.where(work == m, eid, N_EXPERTS), axis=0, keepdims=True)
        sel = eid == first
        idxs.append(first)
        ws.append(jnp.sum(jnp.where(sel, scores, 0.0), axis=0, keepdims=True))
        sels.append(sel)
        chosen = jnp.where(sel, 1.0, chosen)
        work = jnp.where(sel, NEG_INF, work)
    wsum = ws[0]
    for wk in ws[1:]:
        wsum = wsum + wk

    prefix = jnp.dot(chosen.astype(BF16), ut_ref[...], preferred_element_type=F32) + carry_ref[...]
    carry_ref[...] = carry_ref[...] + jnp.sum(chosen, axis=1, keepdims=True)
    cnt_ref[...] = carry_ref[...]

    idx_ref[...] = jnp.zeros_like(idx_ref)
    w_ref[...] = jnp.zeros_like(w_ref)
    rank_ref[...] = jnp.zeros_like(rank_ref)
    for k in range(TOP_K):
        idx_ref[k:k + 1, :] = idxs[k]
        w_ref[k:k + 1, :] = ws[k] / wsum * ROUTED_SCALE
        rank = jnp.sum(jnp.where(sels[k], prefix, 0.0), axis=0, keepdims=True)
        rank_ref[k:k + 1, :] = rank.astype(I32)


def _router(lg, router_bias, tm):
    e, t = lg.shape
    ut = np.triu(np.ones((tm, tm), np.float32), 1)
    tok_spec = pl.BlockSpec((SUBLANES, tm), lambda i: (0, i))
    return pl.pallas_call(
        _router_kernel,
        grid=(t // tm,),
        in_specs=[pl.BlockSpec((e, tm), lambda i: (0, i)),
                  pl.BlockSpec((e, 1), lambda i: (0, 0)),
                  pl.BlockSpec((tm, tm), lambda i: (0, 0))],
        out_specs=[tok_spec, tok_spec, tok_spec, pl.BlockSpec((e, 1), lambda i: (0, 0))],
        out_shape=[jax.ShapeDtypeStruct((SUBLANES, t), I32), jax.ShapeDtypeStruct((SUBLANES, t), F32),
                   jax.ShapeDtypeStruct((SUBLANES, t), I32), jax.ShapeDtypeStruct((e, 1), F32)],
        scratch_shapes=[pltpu.VMEM((e, 1), F32)],
        compiler_params=_cparams(("arbitrary",)),
    )(lg, router_bias.reshape(e, 1), jnp.asarray(ut, BF16))


def _experts_kernel(be_ref, nv_ref, new_ref, fill_ref, x_ref, wg_ref, wu_ref, wd_ref, o_ref, wgu_s, wd_s):
    i = pl.program_id(0)
    de = wg_ref.shape[-1]
    valid = i < nv_ref[0]

    @pl.when(valid & (new_ref[i] == 1))
    def _():
        wgu_s[:, :de] = wg_ref[0].astype(BF16)
        wgu_s[:, de:] = wu_ref[0].astype(BF16)
        wd_s[...] = wd_ref[0].astype(BF16)

    @pl.when(valid)
    def _():
        row = lax.broadcasted_iota(I32, x_ref.shape, 0)
        x = jnp.where(row < fill_ref[i], x_ref[...], jnp.uint32(0))
        gu = jnp.dot(_unpack_bf16_pairs(x), wgu_s[...], preferred_element_type=F32)
        g = gu[:, :de]
        a = (g * jax.nn.sigmoid(g) * gu[:, de:]).astype(BF16)
        o = jnp.dot(a, wd_s[...], preferred_element_type=F32)
        o_ref[...] = _pack_bf16_pairs(o.astype(BF16))

    @pl.when(jnp.logical_not(valid))
    def _():
        o_ref[...] = jnp.zeros_like(o_ref)


def _experts(block_e, n_valid, block_new, block_fill, xs, w_eg, w_eu, w_ed, m):
    n_rows = xs.shape[0]
    d, de = w_eg.shape[-2:]
    nblk = n_rows // m
    xmap = lambda i, be, nv, *_: (jnp.minimum(i, nv[0] - 1), 0)
    wmap = lambda i, be, *_: (be[i], 0, 0)
    grid_spec = pltpu.PrefetchScalarGridSpec(
        num_scalar_prefetch=4,
        grid=(nblk,),
        in_specs=[pl.BlockSpec((m, xs.shape[1]), xmap),
                  pl.BlockSpec((1, d, de), wmap),
                  pl.BlockSpec((1, d, de), wmap),
                  pl.BlockSpec((1, de, d), wmap)],
        out_specs=pl.BlockSpec((m, d // 2), lambda i, *_: (i, 0)),
        scratch_shapes=[pltpu.VMEM((d, 2 * de), BF16), pltpu.VMEM((de, d), BF16)],
    )
    return pl.pallas_call(
        _experts_kernel,
        grid_spec=grid_spec,
        out_shape=jax.ShapeDtypeStruct((n_rows, d // 2), jnp.uint32),
        compiler_params=_cparams(("arbitrary",)),
    )(block_e, n_valid, block_new, block_fill, xs, w_eg, w_eu, w_ed)


def _gather_rows(table, idx, window):
    n = idx.shape[0]
    d = table.shape[1]
    info = plsc.get_sparse_core_info()
    n_workers = info.num_cores * info.num_subcores
    per_worker = n // n_workers
    n_buf = SC_GATHER_BUFFERS
    per_set = n_buf // 2
    n_rounds = per_worker // (window * per_set)
    assert n_buf % 2 == 0 and n_rounds % 2 == 0 and n_rounds * window * per_set * n_workers == n
    mesh = plsc.VectorSubcoreMesh(core_axis_name="c", subcore_axis_name="s")

    @functools.partial(
        pl.kernel, out_type=jax.ShapeDtypeStruct((n, d), table.dtype), mesh=mesh,
        scratch_types=[pltpu.VMEM((per_worker,), I32), pltpu.VMEM((n_buf, window, d), table.dtype),
                       pltpu.SemaphoreType.DMA((n_buf,)), pltpu.SemaphoreType.DMA((n_buf,))])
    def gather(x_hbm, i_hbm, o_hbm, i_v, buf, gsem, wsem):
        base = (lax.axis_index("s") * info.num_cores + lax.axis_index("c")) * per_worker
        pltpu.sync_copy(i_hbm.at[pl.ds(base, per_worker)], i_v)

        def reads(rnd, s):
            return [pltpu.make_async_copy(x_hbm.at[i_v.at[pl.ds((rnd * per_set + b) * window, window)]],
                                          buf.at[s * per_set + b], gsem.at[s * per_set + b])
                    for b in range(per_set)]

        def writes(rnd, s):
            return [pltpu.make_async_copy(buf.at[s * per_set + b],
                                          o_hbm.at[pl.ds(base + (rnd * per_set + b) * window, window)],
                                          wsem.at[s * per_set + b])
                    for b in range(per_set)]

        for cp in reads(0, 0):
            cp.start()

        @pl.loop(0, n_rounds, step=2)
        def _(r0):
            for s in range(2):
                rnd = r0 + s

                @pl.when(rnd >= 1)
                def _():
                    for cp in writes(rnd - 1, 1 - s):
                        cp.wait()

                @pl.when(rnd + 1 < n_rounds)
                def _():
                    for cp in reads(rnd + 1, 1 - s):
                        cp.start()

                for cp in reads(rnd, s):
                    cp.wait()
                for cp in writes(rnd, s):
                    cp.start()

        for cp in writes(n_rounds - 1, 1):
            cp.wait()

    return gather(table, idx)


def _scatter_rows(groups, dest_t, n_rows, window):
    d = groups[0].shape[1]
    dtype = groups[0].dtype
    info = plsc.get_sparse_core_info()
    n_workers = info.num_cores * info.num_subcores
    n_buf = 2
    plan, idx_parts, first = [], [], 0
    row0 = 0
    for x in groups:
        t = x.shape[0]
        per_worker = t // n_workers
        chunks = per_worker // window
        assert chunks % n_buf == 0 and chunks * window * n_workers == t
        idx = dest_t[:TOP_K, first:first + t].reshape(TOP_K, n_workers, chunks, window).transpose(1, 0, 2, 3)
        idx_parts.append(idx.reshape(n_workers, TOP_K * chunks, window))
        plan.append((per_worker, chunks, row0))
        row0 += TOP_K * chunks
        first += t
    idx_rows = -(-row0 // SUBLANES) * SUBLANES
    idx = jnp.concatenate(idx_parts + [jnp.zeros((n_workers, idx_rows - row0, window), I32)], axis=1)
    idx = idx.reshape(n_workers * idx_rows, window)
    mesh = plsc.VectorSubcoreMesh(core_axis_name="c", subcore_axis_name="s")

    @functools.partial(
        pl.kernel, out_type=jax.ShapeDtypeStruct((n_rows, d), dtype), mesh=mesh,
        scratch_types=[pltpu.VMEM((idx_rows, window), I32), pltpu.VMEM((n_buf, window, d), dtype),
                       pltpu.SemaphoreType.DMA((n_buf,)), pltpu.SemaphoreType.DMA((n_buf,))])
    def scatter(*refs):
        x_hbms = refs[:len(groups)]
        i_hbm, o_hbm, i_v, buf, rsem, wsem = refs[len(groups):]
        wid = lax.axis_index("s") * info.num_cores + lax.axis_index("c")
        pltpu.sync_copy(i_hbm.at[pl.ds(wid * idx_rows, idx_rows)], i_v)

        for x_hbm, (per_worker, chunks, row0) in zip(x_hbms, plan):
            base = wid * per_worker

            def read(c, s, x_hbm=x_hbm, base=base):
                return pltpu.make_async_copy(x_hbm.at[pl.ds(base + c * window, window)], buf.at[s], rsem.at[s])

            def writes(c, s, chunks=chunks, row0=row0):
                return [pltpu.make_async_copy(buf.at[s], o_hbm.at[i_v.at[row0 + k * chunks + c]], wsem.at[s])
                        for k in range(TOP_K)]

            read(0, 0).start()

            @pl.loop(0, chunks, step=n_buf)
            def _(c0, read=read, writes=writes, chunks=chunks):
                for s in range(n_buf):
                    c = c0 + s

                    @pl.when(c >= 1)
                    def _():
                        for cp in writes(c - 1, 1 - s):
                            cp.wait()

                    @pl.when(c + 1 < chunks)
                    def _():
                        read(c + 1, 1 - s).start()

                    read(c, s).wait()
                    for cp in writes(c, s):
                        cp.start()

            for cp in writes(chunks - 1, 1):
                cp.wait()

    return scatter(*groups, idx)


def _final_kernel(x1_ref, h2_ref, w_ref, gt_ref, sc_ref, sh_ref, g_ref, wgu_ref, wd_ref, *rest):
    yg_refs = rest[:TOP_K]
    y_ref = rest[TOP_K]
    nbb, rb, d = x1_ref.shape
    ds = wd_ref.shape[0]
    gu = jnp.dot(_unpack_bf16_pairs(h2_ref[...]), wgu_ref[...], preferred_element_type=F32)
    g = gu[:, :ds]
    a = (g * jax.nn.sigmoid(g) * gu[:, ds:]).astype(BF16)
    ff = jnp.dot(a, wd_ref[...], preferred_element_type=F32)
    wts = w_ref[...].T
    routed_lo = routed_hi = None
    for k in range(TOP_K):
        yk = yg_refs[k][...]
        w = wts[:, k:k + 1]
        lo = lax.bitcast_convert_type(lax.shift_left(yk, jnp.uint32(16)), F32) * w
        hi = lax.bitcast_convert_type(yk & jnp.uint32(0xFFFF0000), F32) * w
        routed_lo = lo if k == 0 else routed_lo + lo
        routed_hi = hi if k == 0 else routed_hi + hi
    ff = jnp.concatenate([routed_lo, routed_hi], axis=1) + ff
    x2 = x1_ref[...] + gt_ref[...] * ff.reshape(nbb, rb, d)
    y_ref[...] = _rms(x2, g_ref[...]) * (1.0 + sc_ref[...]) + sh_ref[...]


def _final(x1, h2, wts, yg, first_tile, gt, sc, sh, g, wgu, wd, nbb, rb):
    nb, r, d = x1.shape
    rows = nbb * rb
    nj = r // rb
    tiles_all = yg.shape[0] // TOP_K // rows
    row_map = lambda i, j: (i * nj + j, 0)
    mod_spec = pl.BlockSpec((nbb, 1, d), lambda i, j: (i, 0, 0))
    slab_specs = [pl.BlockSpec((rows, yg.shape[1]), lambda i, j, k=k: (k * tiles_all + first_tile + i * nj + j, 0))
                  for k in range(TOP_K)]
    return pl.pallas_call(
        _final_kernel,
        grid=(nb // nbb, nj),
        in_specs=[pl.BlockSpec((nbb, rb, d), lambda i, j: (i, j, 0)),
                  pl.BlockSpec((rows, h2.shape[1]), row_map),
                  pl.BlockSpec((SUBLANES, rows), lambda i, j: (0, first_tile + i * nj + j)),
                  mod_spec, mod_spec, mod_spec,
                  pl.BlockSpec((1, 1, d), lambda i, j: (0, 0, 0)),
                  pl.BlockSpec(wgu.shape, lambda i, j: (0, 0)),
                  pl.BlockSpec(wd.shape, lambda i, j: (0, 0))] + slab_specs,
        out_specs=pl.BlockSpec((nbb, rb, d), lambda i, j: (i, j, 0)),
        out_shape=jax.ShapeDtypeStruct((nb, r, d), F32),
        compiler_params=_cparams(("arbitrary", "arbitrary")),
    )(x1, h2, wts, gt, sc, sh, g, wgu, wd, *([yg] * TOP_K))


def _moe(h2_groups, lg, router_bias, w_eg, w_eu, w_ed, tile):
    t = lg.shape[1]
    e = N_EXPERTS
    m = MOE_ROWS
    idx_t, w_t, rank_t, counts = _router(lg, router_bias, tile)
    counts = counts.reshape(e).astype(I32)
    padded = ((counts + m - 1) // m) * m
    pad_end = jnp.cumsum(padded)
    pad_start = pad_end - padded
    onehot = idx_t[:, :, None] == jnp.arange(e, dtype=I32)
    dest_t = jnp.sum(jnp.where(onehot, pad_start, 0), axis=-1) + rank_t
    n_rows = (-(-(t * TOP_K) // m)) * m + e * m
    nblk = n_rows // m
    block_start = jnp.arange(nblk, dtype=I32) * m
    block_e = jnp.minimum(jnp.sum(pad_end[None, :] <= block_start[:, None], axis=1), e - 1).astype(I32)
    block_new = jnp.concatenate([jnp.ones((1,), I32), (block_e[1:] != block_e[:-1]).astype(I32)])
    n_valid = (pad_end[-1] // m).astype(I32).reshape(1)
    of_block = block_e[:, None] == jnp.arange(e, dtype=I32)
    count_b = jnp.sum(jnp.where(of_block, counts, 0), axis=1)
    start_b = jnp.sum(jnp.where(of_block, pad_start, 0), axis=1)
    block_fill = jnp.clip(count_b - (block_start - start_b), 0, m).astype(I32)
    xs = _scatter_rows(h2_groups, dest_t, n_rows, SC_SCATTER_WINDOW)
    yb = _experts(block_e, n_valid, block_new, block_fill, xs, w_eg, w_eu, w_ed, m)
    yg = _gather_rows(yb, dest_t[:TOP_K].reshape(-1), SC_GATHER_WINDOW)
    return w_t, yg


def kernel(x_prompt, x_sample, c_prompt, c_sample, cache_k, cache_v, cache_logf, state_ssm_re, state_ssm_im, page_table, w_ada, b_ada, g_norm1, w_in, b_fgate, ssm_lambda_re, ssm_lambda_im, ssm_log_step, ssm_b_re, ssm_b_im, ssm_c_re, ssm_c_im, ssm_d, w_glu, b_glu, g_ssm_out, g_attn_out, w_out, g_norm2, w_router, router_bias, w_exp_gate, w_exp_up, w_exp_down, w_sh_gate, w_sh_up, w_sh_down, g_final, w_ada_final, b_ada_final):
    depth = w_ada.shape[0]
    assert depth == 1, "one layer is supported"
    bp, lp, d = x_prompt.shape
    bs, ls, _ = x_sample.shape
    n_pages = page_table.shape[1]
    n_past = n_pages * PAGE
    n_groups = ssm_lambda_re.shape[1]

    n_c = bp + bs
    n_c_pad = -(-n_c // SUBLANES) * SUBLANES
    c_all = jnp.concatenate([c_prompt, c_sample, jnp.zeros((n_c_pad - n_c, d), F32)], axis=0)
    mod = _adaln(c_all, w_ada[0], b_ada[0])
    modf = _adaln(c_all, w_ada_final, b_ada_final)

    def mods(lo, hi):
        parts = [mod[lo:hi, k * d:(k + 1) * d][:, None, :] for k in range(6)]
        parts += [modf[lo:hi, k * d:(k + 1) * d][:, None, :] for k in range(2)]
        return parts

    w_u, w_q, w_k, w_v, w_f = jnp.split(w_in[0], [D_SSM, D_SSM + D_ATTN, D_SSM + 2 * D_ATTN,
                                                   D_SSM + 3 * D_ATTN], axis=1)
    w_fpad = jnp.concatenate([w_f, jnp.zeros((d, LANES - N_HEADS), F32)], axis=1)
    w_main_s = jnp.concatenate([w_u, w_k, w_v, w_fpad], axis=1).astype(BF16)
    w_main_p = jnp.concatenate([w_u, w_k, w_fpad], axis=1).astype(BF16)
    w_qkv = jnp.stack([w_q.T, w_k.T, w_v.T]).astype(BF16)
    b_f = jnp.concatenate([b_fgate[0], jnp.zeros((LANES - N_HEADS,), F32)]).reshape(1, LANES)
    g1 = g_norm1[0].reshape(1, 1, d)
    g2 = g_norm2[0].reshape(1, 1, d)
    gf = g_final.reshape(1, 1, d)
    wb, wc, a_re, a_im = _s5_params(ssm_lambda_re[0], ssm_lambda_im[0], ssm_log_step[0], ssm_b_re[0],
                                    ssm_b_im[0], ssm_c_re[0], ssm_c_im[0])
    dsk = ssm_d[0].reshape(1, D_SSM)
    wglu = w_glu[0].astype(BF16)
    bglu = b_glu[0].reshape(1, D_SSM)
    g_so = g_ssm_out[0].reshape(1, D_SSM)
    g_ao = g_attn_out[0].reshape(1, D_ATTN)
    wo = w_out[0].astype(BF16)
    wr_t = w_router[0].T
    wr_hi = wr_t.astype(BF16)
    wrt = jnp.stack([wr_hi, (wr_t - wr_hi.astype(F32)).astype(BF16)])
    wgu = jnp.concatenate([w_sh_gate[0], w_sh_up[0]], axis=1).astype(BF16)
    wsd = w_sh_down[0].astype(BF16)

    def ssm_state(re, im):
        return jnp.concatenate([re.reshape(-1, 16, LANES), im.reshape(-1, 16, LANES)], axis=1)

    def split_state(ht):
        n = ht.shape[0]
        return (ht[:, :16].reshape(1, n, n_groups, SSM_STATE), ht[:, 16:].reshape(1, n, n_groups, SSM_STATE))

    tm = ROW_TILE
    sh1_p, sc1_p, gt1_p, sh2_p, sc2_p, gt2_p, shf_p, scf_p = mods(0, bp)
    u, lf, kb, qt, vt, ktf, vtf, lft = _inproj(x_prompt, sc1_p, sh1_p, g1, w_main_p, w_qkv, b_f, 1, tm, True,
                                               HEAD_DIM ** -0.5 * LOG2E)
    crep = _crep(lf, bp, lp, LOG2E)
    ys_p, ht = _s5(u.reshape(bp, lp, D_SSM), jnp.zeros((bp, 32, LANES), F32), wb, wc, a_re, a_im, dsk,
                   wglu, bglu, g_so, bp, S5_TIME_TILE)
    k_prompt = ktf.reshape(bp, N_HEADS, HEAD_DIM, lp).transpose(0, 3, 1, 2)[None]
    v_prompt = vtf.reshape(bp, N_HEADS, HEAD_DIM, lp).transpose(0, 3, 1, 2)[None]
    logf_prompt = lft.transpose(0, 2, 1)[None]
    sre_p, sim_p = split_state(ht)

    nbb = ROW_TILE // ls
    sh1_s, sc1_s, gt1_s, sh2_s, sc2_s, gt2_s, shf_s, scf_s = mods(bp, bp + bs)
    u, lf, k, v, q = _inproj(x_sample, sc1_s, sh1_s, g1, w_main_s, w_qkv, b_f, nbb, ls, False, HEAD_DIM ** -0.5)
    ys_s, ht = _s5(u.reshape(bs, ls, D_SSM), ssm_state(state_ssm_re[0], state_ssm_im[0]), wb, wc, a_re, a_im,
                   dsk, wglu, bglu, g_so, S5_SAMPLE_SEQS, ls)
    lf_past = cache_logf[0][page_table].reshape(bs, n_past, N_HEADS).transpose(0, 2, 1)
    lf_new = lf[:, :N_HEADS].reshape(bs, ls, N_HEADS).transpose(0, 2, 1)
    n_key_pad = -(-(n_past + ls) // LANES) * LANES
    lf_all = jnp.concatenate([lf_past, lf_new, jnp.zeros((bs, N_HEADS, n_key_pad - n_past - ls), F32)], axis=2)
    cn_s = _neg_cumsum(lf_all.reshape(bs * N_HEADS, n_key_pad), PREFIX_ROWS).reshape(bs, N_HEADS, n_key_pad)

    oa_p, oa_s = _attention(qt, kb.reshape(bp, lp, D_ATTN), vt, crep, g_ao, page_table,
                            q.reshape(bs, ls, D_ATTN), k.reshape(bs, ls, D_ATTN), v.reshape(bs, ls, D_ATTN),
                            cn_s, cache_k[0].transpose(0, 2, 3, 1), cache_v[0].transpose(0, 2, 3, 1))
    x1_p, h2_p, lg_p = _outproj(x_prompt, ys_p.reshape(bp * lp, D_SSM), oa_p.reshape(bp * lp, D_ATTN),
                                gt1_p, sc2_p, sh2_p, g2, wo, wrt, 1, tm)
    x1_s, h2_s, lg_s = _outproj(x_sample, ys_s, oa_s.reshape(bs * ls, D_ATTN).astype(BF16),
                                gt1_s, sc2_s, sh2_s, g2, wo, wrt, nbb, ls)

    assert nbb * ls == tm
    wts, yg = _moe([h2_p, h2_s], jnp.concatenate([lg_p, lg_s], axis=1), router_bias[0],
                   w_exp_gate[0], w_exp_up[0], w_exp_down[0], tm)
    tiles_p = bp * lp // tm
    y_prompt = _final(x1_p, h2_p, wts, yg, 0, gt2_p, scf_p, shf_p, gf, wgu, wsd, 1, tm)
    y_sample = _final(x1_s, h2_s, wts, yg, tiles_p, gt2_s, scf_s, shf_s, gf, wgu, wsd, nbb, ls)
    k_sample = k.reshape(1, bs, ls, N_HEADS, HEAD_DIM)
    v_sample = v.reshape(1, bs, ls, N_HEADS, HEAD_DIM)
    logf_sample = lf[:, :N_HEADS].reshape(1, bs, ls, N_HEADS)
    sre_s, sim_s = split_state(ht)

    return (y_prompt, y_sample, k_prompt, v_prompt, logf_prompt, sre_p, sim_p,
            k_sample, v_sample, logf_sample, sre_s, sim_s)
```
